```python
import jax, jax.numpy as jnp
from jax import lax
import numpy as np

D_MODEL = 1024
BATCH = 8
SEQ = 4096
DEPTH = 2

CHUNK = 64
Q_BLOCK = 128
D_MIX = D_MODEL
N_HEADS = 8
QK_NOPE = 64
QK_ROPE = 32
QK_DIM = QK_NOPE + QK_ROPE
V_DIM = 64
Q_LORA = 384
KV_LORA = 256
ATTN_W = N_HEADS * V_DIM
CONV_W = D_MIX - ATTN_W
CONV_K = 31
D_FF = 2816
ROPE_THETA = 10000.0
EPS = 1e-6
IN_COLS = Q_LORA + KV_LORA + QK_ROPE + 2 * CONV_W

kernel_name = "hymba_mla_conformer_conv_macaron"


def rms_norm(x, g):
    xf = x.astype(jnp.float32)
    y = xf * lax.rsqrt(jnp.mean(xf * xf, axis=-1, keepdims=True) + EPS)
    return (y * g.astype(jnp.float32)).astype(x.dtype)


def layer_norm(x, g, b):
    xf = x.astype(jnp.float32)
    mu = jnp.mean(xf, axis=-1, keepdims=True)
    xc = xf - mu
    y = xc * lax.rsqrt(jnp.mean(xc * xc, axis=-1, keepdims=True) + EPS)
    return (y * g.astype(jnp.float32) + b.astype(jnp.float32)).astype(x.dtype)


def swiglu(x, w_gate, w_up, w_down):
    return (jax.nn.silu(x @ w_gate) * (x @ w_up)) @ w_down


def rope_tables(seq_len):
    pos = jnp.arange(seq_len, dtype=jnp.float32)
    inv_freq = 1.0 / (ROPE_THETA ** (jnp.arange(0, QK_ROPE, 2, dtype=jnp.float32) / QK_ROPE))
    ang = pos[:, None] * inv_freq[None, :]
    return jnp.cos(ang), jnp.sin(ang)


def apply_rope(x, cos, sin):
    xf = x.astype(jnp.float32)
    half = QK_ROPE // 2
    x1, x2 = xf[..., :half], xf[..., half:]
    c, s = cos[None, :, None, :], sin[None, :, None, :]
    out = jnp.concatenate([x1 * c - x2 * s, x1 * s + x2 * c], axis=-1)
    return out.astype(x.dtype)


def chunk_causal_attention(q, k, v):
    b, h, s, _ = q.shape
    scale = QK_DIM ** -0.5
    outs = []
    for i in range(s // Q_BLOCK):
        q0 = i * Q_BLOCK
        k_end = q0 + Q_BLOCK
        qs = q[:, :, q0:k_end]
        ks = k[:, :, :k_end]
        vs = v[:, :, :k_end]
        scores = jnp.einsum('bhqd,bhkd->bhqk', qs, ks).astype(jnp.float32) * scale
        q_chunk = (q0 + jnp.arange(Q_BLOCK)) // CHUNK
        k_chunk = jnp.arange(k_end) // CHUNK
        allowed = k_chunk[None, :] <= q_chunk[:, None]
        scores = jnp.where(allowed[None, None], scores, -1e30)
        probs = jax.nn.softmax(scores, axis=-1).astype(vs.dtype)
        outs.append(jnp.einsum('bhqk,bhkd->bqhd', probs, vs))
    out = jnp.concatenate(outs, axis=1)
    return out.reshape(b, s, h * V_DIM)


def mla_group(c_q_raw, c_kv_raw, k_pe_raw, q_lat_g, w_uq, kv_lat_g, w_ukv, q_g, k_g, cos, sin):
    b, s, _ = c_q_raw.shape
    q = (rms_norm(c_q_raw, q_lat_g) @ w_uq).reshape(b, s, N_HEADS, QK_DIM)
    kv = (rms_norm(c_kv_raw, kv_lat_g) @ w_ukv).reshape(b, s, N_HEADS, QK_NOPE + V_DIM)
    k_nope, v = kv[..., :QK_NOPE], kv[..., QK_NOPE:]
    k_pe = jnp.broadcast_to(k_pe_raw[:, :, None, :], (b, s, N_HEADS, QK_ROPE))
    k = jnp.concatenate([k_nope, k_pe], axis=-1)
    q = rms_norm(q, q_g)
    k = rms_norm(k, k_g)
    q = jnp.concatenate([q[..., :QK_NOPE], apply_rope(q[..., QK_NOPE:], cos, sin)], axis=-1)
    k = jnp.concatenate([k[..., :QK_NOPE], apply_rope(k[..., QK_NOPE:], cos, sin)], axis=-1)
    q = jnp.transpose(q, (0, 2, 1, 3))
    k = jnp.transpose(k, (0, 2, 1, 3))
    v = jnp.transpose(v, (0, 2, 1, 3))
    return chunk_causal_attention(q, k, v)


def conv_group(u_raw, conv_w, conv_b, ln_g, ln_b):
    a, g = u_raw[..., :CONV_W], u_raw[..., CONV_W:]
    u = a * jax.nn.sigmoid(g)
    y = lax.conv_general_dilated(
        u, conv_w[:, None, :].astype(u.dtype),
        window_strides=(1,), padding=[(CONV_K - 1, 0)],
        dimension_numbers=('NWC', 'WIO', 'NWC'),
        feature_group_count=CONV_W)
    y = y + conv_b
    return jax.nn.silu(layer_norm(y, ln_g, ln_b))


def _fwd_setup_inputs(seed: int = 0) -> dict:
    key = jax.random.key(seed)
    ks = jax.random.split(key, 24)
    L, D, F = DEPTH, D_MODEL, D_FF

    def w(k, shape, fan_in):
        return jax.random.normal(k, shape, jnp.float32) * (fan_in ** -0.5)

    def gain(k, shape):
        return 1.0 + 0.02 * jax.random.normal(k, shape, jnp.float32)

    def bias(k, shape):
        return 0.02 * jax.random.normal(k, shape, jnp.float32)

    return {
        "x": jax.random.normal(ks[0], (BATCH, SEQ, D), jnp.float32),
        "ffn1_norm": gain(ks[1], (L, D)),
        "ffn1_w_gate": w(ks[2], (L, D, F), D),
        "ffn1_w_up": w(ks[3], (L, D, F), D),
        "ffn1_w_down": w(ks[4], (L, F, D), F),
        "mix_norm": gain(ks[5], (L, D)),
        "w_in": w(ks[6], (L, D, IN_COLS), D),
        "q_latent_norm": gain(ks[7], (L, Q_LORA)),
        "w_uq": w(ks[8], (L, Q_LORA, N_HEADS * QK_DIM), Q_LORA),
        "kv_latent_norm": gain(ks[9], (L, KV_LORA)),
        "w_ukv": w(ks[10], (L, KV_LORA, N_HEADS * (QK_NOPE + V_DIM)), KV_LORA),
        "q_norm": gain(ks[11], (L, QK_DIM)),
        "k_norm": gain(ks[12], (L, QK_DIM)),
        "conv_w": w(ks[13], (L, CONV_K, CONV_W), CONV_K),
        "conv_b": bias(ks[14], (L, CONV_W)),
        "conv_ln_g": gain(ks[15], (L, CONV_W)),
        "conv_ln_b": bias(ks[16], (L, CONV_W)),
        "w_out": w(ks[17], (L, D_MIX, D), D_MIX),
        "ffn2_norm": gain(ks[18], (L, D)),
        "ffn2_w_gate": w(ks[19], (L, D, F), D),
        "ffn2_w_up": w(ks[20], (L, D, F), D),
        "ffn2_w_down": w(ks[21], (L, F, D), F),
        "post_norm": gain(ks[22], (L, D)),
    }


def _fwd_reference(x, ffn1_norm, ffn1_w_gate, ffn1_w_up, ffn1_w_down, mix_norm, w_in,
              q_latent_norm, w_uq, kv_latent_norm, w_ukv, q_norm, k_norm,
              conv_w, conv_b, conv_ln_g, conv_ln_b, w_out,
              ffn2_norm, ffn2_w_gate, ffn2_w_up, ffn2_w_down, post_norm):
    cos, sin = rope_tables(x.shape[1])
    o_kv = Q_LORA
    o_pe = Q_LORA + KV_LORA
    o_cv = Q_LORA + KV_LORA + QK_ROPE
    for l in range(DEPTH):
        x = x + 0.5 * swiglu(rms_norm(x, ffn1_norm[l]), ffn1_w_gate[l], ffn1_w_up[l], ffn1_w_down[l])
        h = rms_norm(x, mix_norm[l])
        p = h @ w_in[l]
        attn = mla_group(p[..., :o_kv], p[..., o_kv:o_pe], p[..., o_pe:o_cv],
                         q_latent_norm[l], w_uq[l], kv_latent_norm[l], w_ukv[l],
                         q_norm[l], k_norm[l], cos, sin)
        conv = conv_group(p[..., o_cv:], conv_w[l], conv_b[l], conv_ln_g[l], conv_ln_b[l])
        x = x + jnp.concatenate([attn, conv], axis=-1) @ w_out[l]
        x = x + 0.5 * swiglu(rms_norm(x, ffn2_norm[l]), ffn2_w_gate[l], ffn2_w_up[l], ffn2_w_down[l])
        x = rms_norm(x, post_norm[l])
    return x


import jax as _jax
import jax.numpy as _jnp

TWIN_FORMAT = 'train_step'
FWD_PARAMS = ['x', 'ffn1_norm', 'ffn1_w_gate', 'ffn1_w_up', 'ffn1_w_down', 'mix_norm', 'w_in', 'q_latent_norm', 'w_uq', 'kv_latent_norm', 'w_ukv', 'q_norm', 'k_norm', 'conv_w', 'conv_b', 'conv_ln_g', 'conv_ln_b', 'w_out', 'ffn2_norm', 'ffn2_w_gate', 'ffn2_w_up', 'ffn2_w_down', 'post_norm']
TWIN_WEIGHTS = ['ffn1_norm', 'ffn1_w_gate', 'ffn1_w_up', 'ffn1_w_down', 'mix_norm', 'w_in', 'q_latent_norm', 'w_uq', 'kv_latent_norm', 'w_ukv', 'q_norm', 'k_norm', 'conv_w', 'conv_b', 'conv_ln_g', 'conv_ln_b', 'w_out', 'ffn2_norm', 'ffn2_w_gate', 'ffn2_w_up', 'ffn2_w_down', 'post_norm']
TWIN_DIFF_INPUT = 'x'
TWIN_INPUTS = ['x', 'ffn1_norm', 'ffn1_w_gate', 'ffn1_w_up', 'ffn1_w_down', 'mix_norm', 'w_in', 'q_latent_norm', 'w_uq', 'kv_latent_norm', 'w_ukv', 'q_norm', 'k_norm', 'conv_w', 'conv_b', 'conv_ln_g', 'conv_ln_b', 'w_out', 'ffn2_norm', 'ffn2_w_gate', 'ffn2_w_up', 'ffn2_w_down', 'post_norm', 'loss_target', 'm_ffn1_norm', 'm_ffn1_w_gate', 'm_ffn1_w_up', 'm_ffn1_w_down', 'm_mix_norm', 'm_w_in', 'm_q_latent_norm', 'm_w_uq', 'm_kv_latent_norm', 'm_w_ukv', 'm_q_norm', 'm_k_norm', 'm_conv_w', 'm_conv_b', 'm_conv_ln_g', 'm_conv_ln_b', 'm_w_out', 'm_ffn2_norm', 'm_ffn2_w_gate', 'm_ffn2_w_up', 'm_ffn2_w_down', 'm_post_norm', 'v_ffn1_norm', 'v_ffn1_w_gate', 'v_ffn1_w_up', 'v_ffn1_w_down', 'v_mix_norm', 'v_w_in', 'v_q_latent_norm', 'v_w_uq', 'v_kv_latent_norm', 'v_w_ukv', 'v_q_norm', 'v_k_norm', 'v_conv_w', 'v_conv_b', 'v_conv_ln_g', 'v_conv_ln_b', 'v_w_out', 'v_ffn2_norm', 'v_ffn2_w_gate', 'v_ffn2_w_up', 'v_ffn2_w_down', 'v_post_norm']
TWIN_OUTPUTS = ['loss', 'grad_x', 'grad_ffn1_norm', 'grad_ffn1_w_gate', 'grad_ffn1_w_up', 'grad_ffn1_w_down', 'grad_mix_norm', 'grad_w_in', 'grad_q_latent_norm', 'grad_w_uq', 'grad_kv_latent_norm', 'grad_w_ukv', 'grad_q_norm', 'grad_k_norm', 'grad_conv_w', 'grad_conv_b', 'grad_conv_ln_g', 'grad_conv_ln_b', 'grad_w_out', 'grad_ffn2_norm', 'grad_ffn2_w_gate', 'grad_ffn2_w_up', 'grad_ffn2_w_down', 'grad_post_norm', 'delta_ffn1_norm', 'delta_ffn1_w_gate', 'delta_ffn1_w_up', 'delta_ffn1_w_down', 'delta_mix_norm', 'delta_w_in', 'delta_q_latent_norm', 'delta_w_uq', 'delta_kv_latent_norm', 'delta_w_ukv', 'delta_q_norm', 'delta_k_norm', 'delta_conv_w', 'delta_conv_b', 'delta_conv_ln_g', 'delta_conv_ln_b', 'delta_w_out', 'delta_ffn2_norm', 'delta_ffn2_w_gate', 'delta_ffn2_w_up', 'delta_ffn2_w_down', 'delta_post_norm', 'new_m_ffn1_norm', 'new_m_ffn1_w_gate', 'new_m_ffn1_w_up', 'new_m_ffn1_w_down', 'new_m_mix_norm', 'new_m_w_in', 'new_m_q_latent_norm', 'new_m_w_uq', 'new_m_kv_latent_norm', 'new_m_w_ukv', 'new_m_q_norm', 'new_m_k_norm', 'new_m_conv_w', 'new_m_conv_b', 'new_m_conv_ln_g', 'new_m_conv_ln_b', 'new_m_w_out', 'new_m_ffn2_norm', 'new_m_ffn2_w_gate', 'new_m_ffn2_w_up', 'new_m_ffn2_w_down', 'new_m_post_norm', 'new_v_ffn1_norm', 'new_v_ffn1_w_gate', 'new_v_ffn1_w_up', 'new_v_ffn1_w_down', 'new_v_mix_norm', 'new_v_w_in', 'new_v_q_latent_norm', 'new_v_w_uq', 'new_v_kv_latent_norm', 'new_v_w_ukv', 'new_v_q_norm', 'new_v_k_norm', 'new_v_conv_w', 'new_v_conv_b', 'new_v_conv_ln_g', 'new_v_conv_ln_b', 'new_v_w_out', 'new_v_ffn2_norm', 'new_v_ffn2_w_gate', 'new_v_ffn2_w_up', 'new_v_ffn2_w_down', 'new_v_post_norm']
TWIN_LEAF_KINDS = {'loss': 'loss', 'grad_x': 'grad_x', 'grad_ffn1_norm': 'grad_w', 'grad_ffn1_w_gate': 'grad_w', 'grad_ffn1_w_up': 'grad_w', 'grad_ffn1_w_down': 'grad_w', 'grad_mix_norm': 'grad_w', 'grad_w_in': 'grad_w', 'grad_q_latent_norm': 'grad_w', 'grad_w_uq': 'grad_w', 'grad_kv_latent_norm': 'grad_w', 'grad_w_ukv': 'grad_w', 'grad_q_norm': 'grad_w', 'grad_k_norm': 'grad_w', 'grad_conv_w': 'grad_w', 'grad_conv_b': 'grad_w', 'grad_conv_ln_g': 'grad_w', 'grad_conv_ln_b': 'grad_w', 'grad_w_out': 'grad_w', 'grad_ffn2_norm': 'grad_w', 'grad_ffn2_w_gate': 'grad_w', 'grad_ffn2_w_up': 'grad_w', 'grad_ffn2_w_down': 'grad_w', 'grad_post_norm': 'grad_w', 'delta_ffn1_norm': 'delta_w', 'delta_ffn1_w_gate': 'delta_w', 'delta_ffn1_w_up': 'delta_w', 'delta_ffn1_w_down': 'delta_w', 'delta_mix_norm': 'delta_w', 'delta_w_in': 'delta_w', 'delta_q_latent_norm': 'delta_w', 'delta_w_uq': 'delta_w', 'delta_kv_latent_norm': 'delta_w', 'delta_w_ukv': 'delta_w', 'delta_q_norm': 'delta_w', 'delta_k_norm': 'delta_w', 'delta_conv_w': 'delta_w', 'delta_conv_b': 'delta_w', 'delta_conv_ln_g': 'delta_w', 'delta_conv_ln_b': 'delta_w', 'delta_w_out': 'delta_w', 'delta_ffn2_norm': 'delta_w', 'delta_ffn2_w_gate': 'delta_w', 'delta_ffn2_w_up': 'delta_w', 'delta_ffn2_w_down': 'delta_w', 'delta_post_norm': 'delta_w', 'new_m_ffn1_norm': 'new_m', 'new_m_ffn1_w_gate': 'new_m', 'new_m_ffn1_w_up': 'new_m', 'new_m_ffn1_w_down': 'new_m', 'new_m_mix_norm': 'new_m', 'new_m_w_in': 'new_m', 'new_m_q_latent_norm': 'new_m', 'new_m_w_uq': 'new_m', 'new_m_kv_latent_norm': 'new_m', 'new_m_w_ukv': 'new_m', 'new_m_q_norm': 'new_m', 'new_m_k_norm': 'new_m', 'new_m_conv_w': 'new_m', 'new_m_conv_b': 'new_m', 'new_m_conv_ln_g': 'new_m', 'new_m_conv_ln_b': 'new_m', 'new_m_w_out': 'new_m', 'new_m_ffn2_norm': 'new_m', 'new_m_ffn2_w_gate': 'new_m', 'new_m_ffn2_w_up': 'new_m', 'new_m_ffn2_w_down': 'new_m', 'new_m_post_norm': 'new_m', 'new_v_ffn1_norm': 'new_v', 'new_v_ffn1_w_gate': 'new_v', 'new_v_ffn1_w_up': 'new_v', 'new_v_ffn1_w_down': 'new_v', 'new_v_mix_norm': 'new_v', 'new_v_w_in': 'new_v', 'new_v_q_latent_norm': 'new_v', 'new_v_w_uq': 'new_v', 'new_v_kv_latent_norm': 'new_v', 'new_v_w_ukv': 'new_v', 'new_v_q_norm': 'new_v', 'new_v_k_norm': 'new_v', 'new_v_conv_w': 'new_v', 'new_v_conv_b': 'new_v', 'new_v_conv_ln_g': 'new_v', 'new_v_conv_ln_b': 'new_v', 'new_v_w_out': 'new_v', 'new_v_ffn2_norm': 'new_v', 'new_v_ffn2_w_gate': 'new_v', 'new_v_ffn2_w_up': 'new_v', 'new_v_ffn2_w_down': 'new_v', 'new_v_post_norm': 'new_v'}


def _forward(args):
    return _fwd_reference(*[args[k] for k in FWD_PARAMS])


def _output_shape():
    out = _jax.eval_shape(lambda: _forward(_fwd_setup_inputs(0)))
    return out.shape, out.dtype

N_MICROBATCH = 1
ADAM_LR = 0.001
ADAM_B1 = 0.9
ADAM_B2 = 0.999
ADAM_EPS = 1e-08
ADAM_WD = 0.01
ADAM_STEP = 10
PER_EXAMPLE_BATCH_AXIS = {'x': 0, 'loss_target': 0}
SHARED_INPUTS = []
_WEIGHT_DTYPES = {'ffn1_norm': _jnp.float32, 'ffn1_w_gate': _jnp.float32, 'ffn1_w_up': _jnp.float32, 'ffn1_w_down': _jnp.float32, 'mix_norm': _jnp.float32, 'w_in': _jnp.float32, 'q_latent_norm': _jnp.float32, 'w_uq': _jnp.float32, 'kv_latent_norm': _jnp.float32, 'w_ukv': _jnp.float32, 'q_norm': _jnp.float32, 'k_norm': _jnp.float32, 'conv_w': _jnp.float32, 'conv_b': _jnp.float32, 'conv_ln_g': _jnp.float32, 'conv_ln_b': _jnp.float32, 'w_out': _jnp.float32, 'ffn2_norm': _jnp.float32, 'ffn2_w_gate': _jnp.float32, 'ffn2_w_up': _jnp.float32, 'ffn2_w_down': _jnp.float32, 'post_norm': _jnp.float32}
MOMENT_SCALE = {'ffn1_norm': 7.940259e-02, 'ffn1_w_gate': 3.500540e-02, 'ffn1_w_up': 3.383302e-02, 'ffn1_w_down': 5.617686e-02, 'mix_norm': 8.664651e-02, 'w_in': 6.710509e-02, 'q_latent_norm': 2.802076e-02, 'w_uq': 2.010418e-02, 'kv_latent_norm': 5.134521e-02, 'w_ukv': 2.658762e-02, 'q_norm': 5.566362e-02, 'k_norm': 5.565160e-02, 'conv_w': 1.054160e-01, 'conv_b': 2.356467e-01, 'conv_ln_g': 1.376704e-01, 'conv_ln_b': 1.427419e-01, 'w_out': 7.673272e-02, 'ffn2_norm': 6.950500e-02, 'ffn2_w_gate': 2.953467e-02, 'ffn2_w_up': 2.871721e-02, 'ffn2_w_down': 4.754339e-02, 'post_norm': 2.257942e+01}


def _to_microbatches(a, axis):
    t = _jnp.moveaxis(a, axis, 0)
    t = t.reshape((N_MICROBATCH, t.shape[0] // N_MICROBATCH) + t.shape[1:])
    return _jnp.moveaxis(t, 1, axis + 1)


def setup_inputs(seed: int = 0) -> dict:
    inp = _fwd_setup_inputs(seed)
    key = _jax.random.fold_in(_jax.random.key(seed), 7919)
    shape, _ = _output_shape()
    out = dict(inp)
    out["loss_target"] = _jax.random.normal(_jax.random.fold_in(key, 0), shape, _jnp.float32)
    for i, name in enumerate(TWIN_WEIGHTS):
        w = inp[name].astype(_jnp.float32)
        if MOMENT_SCALE is None:
            s = _jnp.sqrt(_jnp.mean(_jnp.square(w)) + 1e-30)
        else:
            s = MOMENT_SCALE[name]
        km, kv = _jax.random.split(_jax.random.fold_in(key, i + 1))
        out[name] = w
        out["m_" + name] = s * _jax.random.normal(km, w.shape, _jnp.float32)
        out["v_" + name] = (s * s) * _jax.random.uniform(kv, w.shape, _jnp.float32, 0.5, 1.5)
    if N_MICROBATCH > 1:
        for name, axis in PER_EXAMPLE_BATCH_AXIS.items():
            out[name] = _to_microbatches(out[name], axis)
    return {'x': out['x'], 'ffn1_norm': out['ffn1_norm'], 'ffn1_w_gate': out['ffn1_w_gate'], 'ffn1_w_up': out['ffn1_w_up'], 'ffn1_w_down': out['ffn1_w_down'], 'mix_norm': out['mix_norm'], 'w_in': out['w_in'], 'q_latent_norm': out['q_latent_norm'], 'w_uq': out['w_uq'], 'kv_latent_norm': out['kv_latent_norm'], 'w_ukv': out['w_ukv'], 'q_norm': out['q_norm'], 'k_norm': out['k_norm'], 'conv_w': out['conv_w'], 'conv_b': out['conv_b'], 'conv_ln_g': out['conv_ln_g'], 'conv_ln_b': out['conv_ln_b'], 'w_out': out['w_out'], 'ffn2_norm': out['ffn2_norm'], 'ffn2_w_gate': out['ffn2_w_gate'], 'ffn2_w_up': out['ffn2_w_up'], 'ffn2_w_down': out['ffn2_w_down'], 'post_norm': out['post_norm'], 'loss_target': out['loss_target'], 'm_ffn1_norm': out['m_ffn1_norm'], 'm_ffn1_w_gate': out['m_ffn1_w_gate'], 'm_ffn1_w_up': out['m_ffn1_w_up'], 'm_ffn1_w_down': out['m_ffn1_w_down'], 'm_mix_norm': out['m_mix_norm'], 'm_w_in': out['m_w_in'], 'm_q_latent_norm': out['m_q_latent_norm'], 'm_w_uq': out['m_w_uq'], 'm_kv_latent_norm': out['m_kv_latent_norm'], 'm_w_ukv': out['m_w_ukv'], 'm_q_norm': out['m_q_norm'], 'm_k_norm': out['m_k_norm'], 'm_conv_w': out['m_conv_w'], 'm_conv_b': out['m_conv_b'], 'm_conv_ln_g': out['m_conv_ln_g'], 'm_conv_ln_b': out['m_conv_ln_b'], 'm_w_out': out['m_w_out'], 'm_ffn2_norm': out['m_ffn2_norm'], 'm_ffn2_w_gate': out['m_ffn2_w_gate'], 'm_ffn2_w_up': out['m_ffn2_w_up'], 'm_ffn2_w_down': out['m_ffn2_w_down'], 'm_post_norm': out['m_post_norm'], 'v_ffn1_norm': out['v_ffn1_norm'], 'v_ffn1_w_gate': out['v_ffn1_w_gate'], 'v_ffn1_w_up': out['v_ffn1_w_up'], 'v_ffn1_w_down': out['v_ffn1_w_down'], 'v_mix_norm': out['v_mix_norm'], 'v_w_in': out['v_w_in'], 'v_q_latent_norm': out['v_q_latent_norm'], 'v_w_uq': out['v_w_uq'], 'v_kv_latent_norm': out['v_kv_latent_norm'], 'v_w_ukv': out['v_w_ukv'], 'v_q_norm': out['v_q_norm'], 'v_k_norm': out['v_k_norm'], 'v_conv_w': out['v_conv_w'], 'v_conv_b': out['v_conv_b'], 'v_conv_ln_g': out['v_conv_ln_g'], 'v_conv_ln_b': out['v_conv_ln_b'], 'v_w_out': out['v_w_out'], 'v_ffn2_norm': out['v_ffn2_norm'], 'v_ffn2_w_gate': out['v_ffn2_w_gate'], 'v_ffn2_w_up': out['v_ffn2_w_up'], 'v_ffn2_w_down': out['v_ffn2_w_down'], 'v_post_norm': out['v_post_norm']}


def _loss(weights, diff, rest, loss_target):
    with _jax.named_scope("forward"):
        args = {**rest, TWIN_DIFF_INPUT: diff, **{k: w.astype(_WEIGHT_DTYPES[k]) for k, w in weights.items()}}
        y = _forward(args)
    with _jax.named_scope("loss_head"):
        err = _jnp.square(y.astype(_jnp.float32) - loss_target)
        return 0.5 * _jnp.sum(_jnp.mean(err, axis=-1)) if err.ndim else 0.5 * err


def _adamw(w, g, m, v):
    m = ADAM_B1 * m + (1.0 - ADAM_B1) * g
    v = ADAM_B2 * v + (1.0 - ADAM_B2) * _jnp.square(g)
    m_hat = m / (1.0 - ADAM_B1 ** ADAM_STEP)
    v_hat = v / (1.0 - ADAM_B2 ** ADAM_STEP)
    delta = -ADAM_LR * (m_hat / (_jnp.sqrt(v_hat) + ADAM_EPS) + ADAM_WD * w)
    return delta, m, v


def reference(x, ffn1_norm, ffn1_w_gate, ffn1_w_up, ffn1_w_down, mix_norm, w_in, q_latent_norm, w_uq, kv_latent_norm, w_ukv, q_norm, k_norm, conv_w, conv_b, conv_ln_g, conv_ln_b, w_out, ffn2_norm, ffn2_w_gate, ffn2_w_up, ffn2_w_down, post_norm, loss_target, m_ffn1_norm, m_ffn1_w_gate, m_ffn1_w_up, m_ffn1_w_down, m_mix_norm, m_w_in, m_q_latent_norm, m_w_uq, m_kv_latent_norm, m_w_ukv, m_q_norm, m_k_norm, m_conv_w, m_conv_b, m_conv_ln_g, m_conv_ln_b, m_w_out, m_ffn2_norm, m_ffn2_w_gate, m_ffn2_w_up, m_ffn2_w_down, m_post_norm, v_ffn1_norm, v_ffn1_w_gate, v_ffn1_w_up, v_ffn1_w_down, v_mix_norm, v_w_in, v_q_latent_norm, v_w_uq, v_kv_latent_norm, v_w_ukv, v_q_norm, v_k_norm, v_conv_w, v_conv_b, v_conv_ln_g, v_conv_ln_b, v_w_out, v_ffn2_norm, v_ffn2_w_gate, v_ffn2_w_up, v_ffn2_w_down, v_post_norm):
    given = dict(x=x, ffn1_norm=ffn1_norm, ffn1_w_gate=ffn1_w_gate, ffn1_w_up=ffn1_w_up, ffn1_w_down=ffn1_w_down, mix_norm=mix_norm, w_in=w_in, q_latent_norm=q_latent_norm, w_uq=w_uq, kv_latent_norm=kv_latent_norm, w_ukv=w_ukv, q_norm=q_norm, k_norm=k_norm, conv_w=conv_w, conv_b=conv_b, conv_ln_g=conv_ln_g, conv_ln_b=conv_ln_b, w_out=w_out, ffn2_norm=ffn2_norm, ffn2_w_gate=ffn2_w_gate, ffn2_w_up=ffn2_w_up, ffn2_w_down=ffn2_w_down, post_norm=post_norm, loss_target=loss_target, m_ffn1_norm=m_ffn1_norm, m_ffn1_w_gate=m_ffn1_w_gate, m_ffn1_w_up=m_ffn1_w_up, m_ffn1_w_down=m_ffn1_w_down, m_mix_norm=m_mix_norm, m_w_in=m_w_in, m_q_latent_norm=m_q_latent_norm, m_w_uq=m_w_uq, m_kv_latent_norm=m_kv_latent_norm, m_w_ukv=m_w_ukv, m_q_norm=m_q_norm, m_k_norm=m_k_norm, m_conv_w=m_conv_w, m_conv_b=m_conv_b, m_conv_ln_g=m_conv_ln_g, m_conv_ln_b=m_conv_ln_b, m_w_out=m_w_out, m_ffn2_norm=m_ffn2_norm, m_ffn2_w_gate=m_ffn2_w_gate, m_ffn2_w_up=m_ffn2_w_up, m_ffn2_w_down=m_ffn2_w_down, m_post_norm=m_post_norm, v_ffn1_norm=v_ffn1_norm, v_ffn1_w_gate=v_ffn1_w_gate, v_ffn1_w_up=v_ffn1_w_up, v_ffn1_w_down=v_ffn1_w_down, v_mix_norm=v_mix_norm, v_w_in=v_w_in, v_q_latent_norm=v_q_latent_norm, v_w_uq=v_w_uq, v_kv_latent_norm=v_kv_latent_norm, v_w_ukv=v_w_ukv, v_q_norm=v_q_norm, v_k_norm=v_k_norm, v_conv_w=v_conv_w, v_conv_b=v_conv_b, v_conv_ln_g=v_conv_ln_g, v_conv_ln_b=v_conv_ln_b, v_w_out=v_w_out, v_ffn2_norm=v_ffn2_norm, v_ffn2_w_gate=v_ffn2_w_gate, v_ffn2_w_up=v_ffn2_w_up, v_ffn2_w_down=v_ffn2_w_down, v_post_norm=v_post_norm)
    weights = {n: given[n] for n in TWIN_WEIGHTS}
    shared = {n: given[n] for n in SHARED_INPUTS}
    per_example = {n: given[n] for n in ['x']}
    grad_fn = _jax.value_and_grad(_loss, argnums=(0, 1))

    def one_microbatch(ex, loss_target):
        ex = dict(ex)
        diff = ex.pop(TWIN_DIFF_INPUT)
        return grad_fn(weights, diff, {**shared, **ex}, loss_target)

    if N_MICROBATCH == 1:
        loss, (grad_w, grad_x) = one_microbatch(per_example, given["loss_target"])
    else:
        def body(carry, xs):
            loss_sum, grad_sum = carry
            l_k, (gw_k, gx_k) = one_microbatch(xs[0], xs[1])
            with _jax.named_scope("update"):
                return (loss_sum + l_k, _jax.tree.map(_jnp.add, grad_sum, gw_k)), gx_k

        init = (_jnp.zeros((), _jnp.float32), _jax.tree.map(_jnp.zeros_like, weights))
        (loss, grad_w), grad_x = _jax.lax.scan(body, init, (per_example, given["loss_target"]))
    with _jax.named_scope("update"):
        delta_w, new_m, new_v = {}, {}, {}
        for n in TWIN_WEIGHTS:
            delta_w[n], new_m[n], new_v[n] = _adamw(weights[n], grad_w[n], given["m_" + n], given["v_" + n])
    return (loss, grad_x, *[grad_w[n] for n in TWIN_WEIGHTS], *[delta_w[n] for n in TWIN_WEIGHTS],
            *[new_m[n] for n in TWIN_WEIGHTS], *[new_v[n] for n in TWIN_WEIGHTS])
```

```python
import jax
import jax.numpy as jnp
from jax import lax
from jax.experimental import pallas as pl
from jax.experimental.pallas import tpu as pltpu

F32 = jnp.float32
MM_DTYPE = jnp.bfloat16
WIRE_DTYPE = jnp.bfloat16
EPS = 1e-6
N_HEADS = 8
QK_NOPE = 64
QK_ROPE = 32
QK_DIM = QK_NOPE + QK_ROPE
V_DIM = 64
HEAD_PAD = 128
CHUNK_SHIFT = 6
CONV_K = 31
HALO = 32
ROPE_THETA = 10000.0
N_CHIPS = 4
ADAM_LR, ADAM_B1, ADAM_B2, ADAM_EPS, ADAM_WD, ADAM_STEP = 0.001, 0.9, 0.999, 1e-08, 0.01, 10
VMEM_LIMIT_BYTES = 56 * 2 ** 20
MESH = pl.DeviceIdType.MESH
ANY = pl.BlockSpec(memory_space=pl.ANY)

_DIMS = {
    "nn": (((1,), (0,)), ((), ())),
    "nt": (((1,), (1,)), ((), ())),
    "tn": (((0,), (0,)), ((), ())),
}


def _params(n_axes):
    return pltpu.CompilerParams(dimension_semantics=("arbitrary",) * n_axes, vmem_limit_bytes=VMEM_LIMIT_BYTES)


def _tile(n, cap):
    t = cap
    while n % t:
        t //= 2
    return t


def _mm(name, a, b, dims, grid, a_spec, b_spec, o_spec, out_shape, nk, acc_shape, into=None):
    n_axes = len(grid)

    def body(*refs):
        if into is not None:
            refs = refs[1:]
        a_ref, b_ref, o_ref = refs[:3]
        part = lax.dot_general(a_ref[...].astype(MM_DTYPE), b_ref[...].astype(MM_DTYPE), _DIMS[dims],
                               preferred_element_type=F32)
        if nk == 1:
            o_ref[...] = part.astype(o_ref.dtype)
        else:
            acc_ref = refs[3]
            k = pl.program_id(n_axes - 1)

            @pl.when(k == 0)
            def _():
                acc_ref[...] = part

            @pl.when(k > 0)
            def _():
                acc_ref[...] += part

            @pl.when(k == nk - 1)
            def _():
                o_ref[...] = acc_ref[...].astype(o_ref.dtype)

    in_specs = [a_spec, b_spec]
    args = [a, b]
    aliases = {}
    if into is not None:
        in_specs = [ANY] + in_specs
        args = [into] + args
        aliases = {0: 0}
    return pl.pallas_call(
        body, name=name, grid=grid, in_specs=in_specs, out_specs=o_spec, out_shape=out_shape,
        scratch_shapes=[] if nk == 1 else [pltpu.VMEM(acc_shape, F32)],
        input_output_aliases=aliases, compiler_params=_params(n_axes))(*args)


def _proj(name, a, w, l, dims, out_dtype=F32):
    s, k = a.shape
    g = w.shape[1]
    n = w.shape[3] if dims == "nn" else w.shape[2]
    tm = _tile(s, 512)
    return _mm(name, a, w, dims, (g, s // tm),
               pl.BlockSpec((tm, k), lambda j, i: (i, 0)),
               pl.BlockSpec((None, None) + w.shape[2:], lambda j, i: (l, j, 0, 0)),
               pl.BlockSpec((None, tm, n), lambda j, i: (j, i, 0)),
               jax.ShapeDtypeStruct((g, s, n), out_dtype), 1, None)


def _contract(name, a, w, l, dims, out_dtype=F32):
    g, s, kb = a.shape
    n = w.shape[3] if dims == "nn" else w.shape[2]
    tm = _tile(s, 512)
    return _mm(name, a, w, dims, (s // tm, g),
               pl.BlockSpec((None, tm, kb), lambda i, j: (j, i, 0)),
               pl.BlockSpec((None, None) + w.shape[2:], lambda i, j: (l, j, 0, 0)),
               pl.BlockSpec((tm, n), lambda i, j: (i, 0)),
               jax.ShapeDtypeStruct((s, n), out_dtype), g, (tm, n))


def _wgrad(name, a, b, out_dtype, layers=None, l=None, into=None):
    g = a.shape[0] if a.ndim == 3 else b.shape[0]
    s, k = a.shape[-2:]
    n = b.shape[-1]
    tm = _tile(s, 512)

    def spec(x, w):
        if x.ndim == 3:
            return pl.BlockSpec((None, tm, w), lambda j, i: (j, i, 0))
        return pl.BlockSpec((tm, w), lambda j, i: (i, 0))

    if layers is None:
        o_spec = pl.BlockSpec((None, k, n), lambda j, i: (j, 0, 0))
        out_shape = jax.ShapeDtypeStruct((g, k, n), out_dtype)
    else:
        o_spec = pl.BlockSpec((None, None, k, n), lambda j, i: (l, j, 0, 0))
        out_shape = jax.ShapeDtypeStruct((layers, g, k, n), out_dtype)
    return _mm(name, a, b, "tn", (g, s // tm), spec(a, k), spec(b, n), o_spec, out_shape, s // tm, (k, n), into=into)


def _rows(name, fn, s, tm, ins, outs, scratch=()):
    n = s // tm

    def spec(shape, kind):
        nd = len(shape)
        if kind == "r":
            return pl.BlockSpec(shape[:-2] + (tm, shape[-1]), lambda i: (0,) * (nd - 2) + (i, 0))
        if kind in ("f", "a"):
            return pl.BlockSpec(shape, lambda i: (0,) * nd)
        if kind == "p":
            return pl.BlockSpec((HALO, shape[-1]), lambda i: (jnp.maximum(i * (tm // HALO) - 1, 0), 0))
        if kind == "n":
            return pl.BlockSpec((HALO, shape[-1]), lambda i: (jnp.minimum((i + 1) * (tm // HALO), s // HALO - 1), 0))
        j = kind[1]
        return pl.BlockSpec((tm, HEAD_PAD), lambda i: (i, j))

    def body(*refs):
        fn(pl.program_id(0), n, *refs)

    return pl.pallas_call(
        body, name=name, grid=(n,),
        in_specs=[spec(a.shape, kind) for a, kind in ins],
        out_specs=[spec(shape, kind) for shape, _, kind in outs],
        out_shape=[jax.ShapeDtypeStruct(shape, dtype) for shape, dtype, _ in outs],
        scratch_shapes=list(scratch), compiler_params=_params(1))(*[a for a, _ in ins])


def _acc(ref, i, val):
    @pl.when(i == 0)
    def _():
        ref[...] = val

    @pl.when(i > 0)
    def _():
        ref[...] += val


def _sum0(x):
    return jnp.sum(x, axis=0, keepdims=True)


def _rstd(x, n):
    return lax.rsqrt(jnp.sum(x * x, axis=-1, keepdims=True) * (1.0 / n) + EPS)


def _rms_bwd(x, g, dy, n):
    r = _rstd(x, n)
    xh = x * r
    dyg = dy * g
    dx = r * (dyg - xh * (jnp.sum(dyg * xh, axis=-1, keepdims=True) * (1.0 / n)))
    return dx, _sum0(dy * xh)


def _sigmoid(x):
    return 1.0 / (1.0 + jnp.exp(-x))


def _norm_fwd(x, g):
    s, d = x.shape

    def fn(i, n, x_ref, g_ref, o_ref):
        xv = x_ref[...]
        o_ref[...] = (xv * _rstd(xv, d) * g_ref[...]).astype(o_ref.dtype)

    return _rows("norm_fwd", fn, s, _tile(s, 256), [(x, "r"), (g, "f")], [((s, d), MM_DTYPE, "r")])[0]


def _resid_norm(x, adds, scale, g, norm_dtype):
    s, d = x.shape
    k = len(adds)

    def fn(i, n, *refs):
        x_ref, add_refs, g_ref, xo_ref, no_ref = refs[0], refs[1:1 + k], refs[1 + k], refs[2 + k], refs[3 + k]
        tot = add_refs[0][...]
        for r in add_refs[1:]:
            tot = tot + r[...]
        xv = x_ref[...] + scale * tot
        xo_ref[...] = xv
        no_ref[...] = (xv * _rstd(xv, d) * g_ref[...]).astype(no_ref.dtype)

    return _rows("resid_norm", fn, s, _tile(s, 256), [(x, "r")] + [(a, "r") for a in adds] + [(g, "f")],
                 [((s, d), F32, "r"), ((s, d), norm_dtype, "r")])


def _norm_bwd(x, g, dn_parts, dres, out_scale):
    s, d = x.shape
    k = len(dn_parts)
    has_res = dres is not None

    def fn(i, n, *refs):
        x_ref, g_ref = refs[0], refs[1]
        dn_refs = refs[2:2 + k]
        pos = 2 + k
        dn = dn_refs[0][...]
        for r in dn_refs[1:]:
            dn = dn + r[...]
        dx, dg = _rms_bwd(x_ref[...], g_ref[...], dn, d)
        if has_res:
            dx = dx + refs[pos][...]
            pos += 1
        dx_ref, dxs_ref, dg_ref = refs[pos:pos + 3]
        dx_ref[...] = dx
        dxs_ref[...] = (out_scale * dx).astype(dxs_ref.dtype)
        _acc(dg_ref, i, dg)

    ins = [(x, "r"), (g, "f")] + [(p, "r") for p in dn_parts] + ([(dres, "r")] if has_res else [])
    return _rows("norm_bwd", fn, s, _tile(s, 256), ins,
                 [((s, d), F32, "r"), ((s, d), MM_DTYPE, "r"), ((1, d), F32, "a")])


def _swiglu_fwd(a, b):
    g, s, n4 = a.shape

    def fn(i, n, a_ref, b_ref, o_ref):
        av = a_ref[...]
        o_ref[...] = (av * _sigmoid(av) * b_ref[...]).astype(o_ref.dtype)

    return _rows("swiglu_fwd", fn, s, _tile(s, 256), [(a, "r"), (b, "r")], [((g, s, n4), MM_DTYPE, "r")])[0]


def _swiglu_bwd(a, b, dh):
    g, s, n4 = a.shape

    def fn(i, n, a_ref, b_ref, dh_ref, da_ref, db_ref):
        av, dhv = a_ref[...], dh_ref[...]
        sg = _sigmoid(av)
        da_ref[...] = (dhv * b_ref[...] * (sg * (1.0 + av * (1.0 - sg)))).astype(da_ref.dtype)
        db_ref[...] = (dhv * (av * sg)).astype(db_ref.dtype)

    return _rows("swiglu_bwd", fn, s, _tile(s, 256), [(a, "r"), (b, "r"), (dh, "r")],
                 [((g, s, n4), MM_DTYPE, "r"), ((g, s, n4), MM_DTYPE, "r")])


def _final_fwd(x, o, g):
    return _resid_norm(x, [o], 0.5, g, F32)


def _loss(y, t):
    s, d = y.shape

    def fn(i, n, y_ref, t_ref, dy_ref, l_ref):
        e = y_ref[...] - t_ref[...]
        dy_ref[...] = e * (1.0 / d)
        _acc(l_ref, i, (0.5 / d) * jnp.sum(jnp.sum(e * e, axis=-1, keepdims=True), axis=0, keepdims=True))

    return _rows("loss", fn, s, _tile(s, 256), [(y, "r"), (t, "r")], [((s, d), F32, "r"), ((1, 1), F32, "a")])


def _mla_pre(p, gql, gkvl, cw):
    s = p.shape[0]
    ql, kvl = gql.shape[-1], gkvl.shape[-1]
    o_q, o_kv = 2 * cw, 2 * cw + ql

    def fn(i, n, p_ref, gq_ref, gkv_ref, cq_ref, ckv_ref):
        cq = p_ref[:, o_q:o_q + ql]
        cq_ref[...] = (cq * _rstd(cq, ql) * gq_ref[...]).astype(cq_ref.dtype)
        ckv = p_ref[:, o_kv:o_kv + kvl]
        ckv_ref[...] = (ckv * _rstd(ckv, kvl) * gkv_ref[...]).astype(ckv_ref.dtype)

    return _rows("mla_pre", fn, s, _tile(s, 256), [(p, "r"), (gql, "f"), (gkvl, "f")],
                 [((s, ql), MM_DTYPE, "r"), ((s, kvl), MM_DTYPE, "r")])


def _rope(x, cosf, s1, s2):
    return x * cosf + pltpu.roll(x, HEAD_PAD - 16, 1) * s1 + pltpu.roll(x, 16, 1) * s2


def _rope_bwd(d, cosf, s1, s2):
    return d * cosf + pltpu.roll(d * s1, 16, 1) + pltpu.roll(d * s2, HEAD_PAD - 16, 1)


def _qk_post(q_pre, k_pre, p, pe_block, tabs, gq, gk):
    h, s, _ = q_pre.shape
    cosf, s1, s2 = tabs

    def fn(i, n, q_ref, k_ref, pe_ref, c_ref, s1_ref, s2_ref, gq_ref, gk_ref, qo_ref, ko_ref):
        c, a1, a2, pe = c_ref[...], s1_ref[...], s2_ref[...], pe_ref[...]
        for hh in range(h):
            qv = q_ref[hh]
            qo_ref[hh] = _rope(qv * _rstd(qv, QK_DIM) * gq_ref[...], c, a1, a2).astype(qo_ref.dtype)
            kv = k_ref[hh] + pe
            ko_ref[hh] = _rope(kv * _rstd(kv, QK_DIM) * gk_ref[...], c, a1, a2).astype(ko_ref.dtype)

    return _rows("qk_post", fn, s, _tile(s, 256),
                 [(q_pre, "r"), (k_pre, "r"), (p, ("c", pe_block)), (cosf, "r"), (s1, "r"), (s2, "r"), (gq, "f"), (gk, "f")],
                 [((h, s, HEAD_PAD), MM_DTYPE, "r"), ((h, s, HEAD_PAD), MM_DTYPE, "r")])


def _qk_post_bwd(q_pre, k_pre, p, pe_block, tabs, gq, gk, dq, dk):
    h, s, _ = q_pre.shape
    cosf, s1, s2 = tabs

    def fn(i, n, q_ref, k_ref, pe_ref, c_ref, s1_ref, s2_ref, gq_ref, gk_ref, dq_ref, dk_ref,
           dqo_ref, dko_ref, dpe_ref, dgq_ref, dgk_ref):
        c, a1, a2, pe = c_ref[...], s1_ref[...], s2_ref[...], pe_ref[...]
        lane = lax.broadcasted_iota(jnp.int32, pe.shape, 1)
        is_pe = (lane >= QK_NOPE) & (lane < QK_DIM)
        dpe = jnp.zeros(pe.shape, F32)
        dgq = jnp.zeros((1, HEAD_PAD), F32)
        dgk = jnp.zeros((1, HEAD_PAD), F32)
        for hh in range(h):
            dxq, g1 = _rms_bwd(q_ref[hh], gq_ref[...], _rope_bwd(dq_ref[hh], c, a1, a2), QK_DIM)
            dqo_ref[hh] = dxq.astype(dqo_ref.dtype)
            dxk, g2 = _rms_bwd(k_ref[hh] + pe, gk_ref[...], _rope_bwd(dk_ref[hh], c, a1, a2), QK_DIM)
            dko_ref[hh] = dxk.astype(dko_ref.dtype)
            dpe = dpe + dxk
            dgq = dgq + g1
            dgk = dgk + g2
        dpe_ref[...] = jnp.where(is_pe, dpe, 0.0)
        _acc(dgq_ref, i, dgq)
        _acc(dgk_ref, i, dgk)

    return _rows("qk_post_bwd", fn, s, _tile(s, 256),
                 [(q_pre, "r"), (k_pre, "r"), (p, ("c", pe_block)), (cosf, "r"), (s1, "r"), (s2, "r"), (gq, "f"), (gk, "f"),
                  (dq, "r"), (dk, "r")],
                 [((h, s, HEAD_PAD), MM_DTYPE, "r"), ((h, s, HEAD_PAD), MM_DTYPE, "r"), ((s, HEAD_PAD), F32, "r"),
                  ((1, HEAD_PAD), F32, "a"), ((1, HEAD_PAD), F32, "a")])


def _chunk_mask(tq, key_rows):
    r = jnp.right_shift(lax.broadcasted_iota(jnp.int32, (tq, tq), 0), CHUNK_SHIFT)
    c = jnp.right_shift(lax.broadcasted_iota(jnp.int32, (tq, tq), 1), CHUNK_SHIFT)
    return (r <= c) if key_rows else (c <= r)


def _attn_specs(h, s, tq, dp, dv):
    tile = lambda w: pl.BlockSpec((None, tq, w), lambda hh, i: (hh, i, 0))
    whole = lambda w: pl.BlockSpec((None, s, w), lambda hh, i: (hh, 0, 0))
    return tile, whole


def _attn_fwd(q, k, v):
    h, s, dp = q.shape
    dv = v.shape[-1]
    tq = _tile(s, 256)
    scale = QK_DIM ** -0.5
    tile, whole = _attn_specs(h, s, tq, dp, dv)

    def body(q_ref, k_ref, v_ref, o_ref, lse_ref):
        qi = pl.program_id(1)
        qv = q_ref[...]

        def block(kb, carry, diag):
            m, l, acc = carry
            off = pl.multiple_of(kb * tq, tq)
            ks = k_ref[pl.ds(off, tq), :]
            vs = v_ref[pl.ds(off, tq), :]
            sc = lax.dot_general(qv, ks, _DIMS["nt"], preferred_element_type=F32) * scale
            if diag:
                sc = jnp.where(_chunk_mask(tq, False), sc, -1e30)
            mn = jnp.maximum(m, jnp.max(sc, axis=-1, keepdims=True))
            al = jnp.exp(m - mn)
            pr = jnp.exp(sc - mn)
            l = al * l + jnp.sum(pr, axis=-1, keepdims=True)
            acc = al * acc + jnp.dot(pr.astype(MM_DTYPE), vs, preferred_element_type=F32)
            return mn, l, acc

        init = (jnp.full((tq, 1), -1e30, F32), jnp.zeros((tq, 1), F32), jnp.zeros((tq, dv), F32))
        carry = lax.fori_loop(0, qi, lambda kb, c: block(kb, c, False), init)
        m, l, acc = block(qi, carry, True)
        o_ref[...] = acc / l
        lse_ref[...] = m + jnp.log(l)

    return pl.pallas_call(
        body, name="attn_fwd", grid=(h, s // tq),
        in_specs=[tile(dp), whole(dp), whole(dv)], out_specs=[tile(dv), tile(1)],
        out_shape=[jax.ShapeDtypeStruct((h, s, dv), F32), jax.ShapeDtypeStruct((h, s, 1), F32)],
        compiler_params=_params(2))(q, k, v)


def _attn_dq(q, k, v, o, do, lse):
    h, s, dp = q.shape
    dv = v.shape[-1]
    tq = _tile(s, 256)
    scale = QK_DIM ** -0.5
    tile, whole = _attn_specs(h, s, tq, dp, dv)

    def body(q_ref, k_ref, v_ref, o_ref, do_ref, lse_ref, dq_ref, delta_ref):
        qi = pl.program_id(1)
        qv = q_ref[...]
        dov = do_ref[...]
        dob = dov.astype(MM_DTYPE)
        lse = lse_ref[...]
        delta = jnp.sum(dov * o_ref[...], axis=-1, keepdims=True)
        delta_ref[...] = delta

        def block(kb, dq, diag):
            off = pl.multiple_of(kb * tq, tq)
            ks = k_ref[pl.ds(off, tq), :]
            vs = v_ref[pl.ds(off, tq), :]
            sc = lax.dot_general(qv, ks, _DIMS["nt"], preferred_element_type=F32) * scale
            if diag:
                sc = jnp.where(_chunk_mask(tq, False), sc, -1e30)
            pr = jnp.exp(sc - lse)
            dpr = lax.dot_general(dob, vs, _DIMS["nt"], preferred_element_type=F32)
            ds = (pr * (dpr - delta) * scale).astype(MM_DTYPE)
            return dq + jnp.dot(ds, ks, preferred_element_type=F32)

        dq = lax.fori_loop(0, qi, lambda kb, c: block(kb, c, False), jnp.zeros((tq, dp), F32))
        dq_ref[...] = block(qi, dq, True)

    return pl.pallas_call(
        body, name="attn_dq", grid=(h, s // tq),
        in_specs=[tile(dp), whole(dp), whole(dv), tile(dv), tile(dv), tile(1)], out_specs=[tile(dp), tile(1)],
        out_shape=[jax.ShapeDtypeStruct((h, s, dp), F32), jax.ShapeDtypeStruct((h, s, 1), F32)],
        compiler_params=_params(2))(q, k, v, o, do, lse)


def _attn_dkv(q, k, v, do, lse_row, delta_row):
    h, s, dp = q.shape
    dv = v.shape[-1]
    tq = _tile(s, 256)
    nq = s // tq
    scale = QK_DIM ** -0.5
    tile, whole = _attn_specs(h, s, tq, dp, dv)
    rowvec = pl.BlockSpec((None, nq, 1, tq), lambda hh, i: (hh, 0, 0, 0))

    def body(q_ref, k_ref, v_ref, do_ref, lse_ref, delta_ref, dk_ref, dv_ref):
        ki = pl.program_id(1)
        kv = k_ref[...]
        vv = v_ref[...]

        def block(qb, carry, diag):
            dk, dvv = carry
            off = pl.multiple_of(qb * tq, tq)
            qs = q_ref[pl.ds(off, tq), :]
            dos = do_ref[pl.ds(off, tq), :].astype(MM_DTYPE)
            sc = lax.dot_general(kv, qs, _DIMS["nt"], preferred_element_type=F32) * scale
            if diag:
                sc = jnp.where(_chunk_mask(tq, True), sc, -1e30)
            pr = jnp.exp(sc - lse_ref[qb])
            dpr = lax.dot_general(vv, dos, _DIMS["nt"], preferred_element_type=F32)
            ds = (pr * (dpr - delta_ref[qb]) * scale).astype(MM_DTYPE)
            dvv = dvv + jnp.dot(pr.astype(MM_DTYPE), dos, preferred_element_type=F32)
            dk = dk + jnp.dot(ds, qs, preferred_element_type=F32)
            return dk, dvv

        carry = block(ki, (jnp.zeros((tq, dp), F32), jnp.zeros((tq, dv), F32)), True)
        dk, dvv = lax.fori_loop(ki + 1, nq, lambda qb, c: block(qb, c, False), carry)
        dk_ref[...] = dk
        dv_ref[...] = dvv.astype(dv_ref.dtype)

    return pl.pallas_call(
        body, name="attn_dkv", grid=(h, nq),
        in_specs=[whole(dp), tile(dp), tile(dv), whole(dv), rowvec, rowvec], out_specs=[tile(dp), tile(dv)],
        out_shape=[jax.ShapeDtypeStruct((h, s, dp), F32), jax.ShapeDtypeStruct((h, s, dv), MM_DTYPE)],
        compiler_params=_params(2))(q, k, v, do, lse_row, delta_row)


def _glu_with_halo(i, p_ref, ph_ref, ubuf, cw, tm):
    a, g = p_ref[:, 0:cw], p_ref[:, cw:2 * cw]
    ubuf[HALO:HALO + tm, :] = a * _sigmoid(g)
    ah, gh = ph_ref[:, 0:cw], ph_ref[:, cw:2 * cw]
    ubuf[0:HALO, :] = jnp.where(i > 0, ah * _sigmoid(gh), 0.0)
    return a, g


def _conv_taps(ubuf, w_ref, tm):
    base = HALO - (CONV_K - 1)
    y = w_ref[0:1, :] * ubuf[base:base + tm, :]
    for k in range(1, CONV_K):
        y = y + w_ref[k:k + 1, :] * ubuf[base + k:base + k + tm, :]
    return y


def _layer_norm_stats(y, cw):
    mu = jnp.sum(y, axis=-1, keepdims=True) * (1.0 / cw)
    yc = y - mu
    r = lax.rsqrt(jnp.sum(yc * yc, axis=-1, keepdims=True) * (1.0 / cw) + EPS)
    return yc * r, r


def _conv_fwd(p, w, cb, lg, lb):
    s = p.shape[0]
    cw = cb.shape[-1]
    tm = _tile(s, 256)

    def fn(i, n, p_ref, ph_ref, w_ref, cb_ref, lg_ref, lb_ref, o_ref, ubuf):
        _glu_with_halo(i, p_ref, ph_ref, ubuf, cw, tm)
        yh, _ = _layer_norm_stats(_conv_taps(ubuf, w_ref, tm) + cb_ref[...], cw)
        z = yh * lg_ref[...] + lb_ref[...]
        o_ref[...] = (z * _sigmoid(z)).astype(o_ref.dtype)

    return _rows("conv_fwd", fn, s, tm, [(p, "r"), (p, "p"), (w, "f"), (cb, "f"), (lg, "f"), (lb, "f")],
                 [((s, cw), MM_DTYPE, "r")], scratch=[pltpu.VMEM((tm + HALO, cw), F32)])[0]


def _conv_bwd_ln(p, w, cb, lg, lb, dc):
    s = p.shape[0]
    cw = cb.shape[-1]
    tm = _tile(s, 256)

    def fn(i, n, p_ref, ph_ref, w_ref, cb_ref, lg_ref, lb_ref, dc_ref, dy_ref, dlg_ref, dlb_ref, dcb_ref, ubuf):
        _glu_with_halo(i, p_ref, ph_ref, ubuf, cw, tm)
        yh, r = _layer_norm_stats(_conv_taps(ubuf, w_ref, tm) + cb_ref[...], cw)
        z = yh * lg_ref[...] + lb_ref[...]
        sg = _sigmoid(z)
        dz = dc_ref[...] * (sg * (1.0 + z * (1.0 - sg)))
        dyh = dz * lg_ref[...]
        m1 = jnp.sum(dyh, axis=-1, keepdims=True) * (1.0 / cw)
        m2 = jnp.sum(dyh * yh, axis=-1, keepdims=True) * (1.0 / cw)
        dy = r * (dyh - m1 - yh * m2)
        dy_ref[...] = dy
        _acc(dlg_ref, i, _sum0(dz * yh))
        _acc(dlb_ref, i, _sum0(dz))
        _acc(dcb_ref, i, _sum0(dy))

    return _rows("conv_bwd_ln", fn, s, tm,
                 [(p, "r"), (p, "p"), (w, "f"), (cb, "f"), (lg, "f"), (lb, "f"), (dc, "r")],
                 [((s, cw), F32, "r"), ((1, cw), F32, "a"), ((1, cw), F32, "a"), ((1, cw), F32, "a")],
                 scratch=[pltpu.VMEM((tm + HALO, cw), F32)])


def _conv_bwd_taps(p, w, dy):
    s = p.shape[0]
    cw = w.shape[-1]
    tm = _tile(s, 256)

    def fn(i, n, p_ref, ph_ref, w_ref, dy_ref, dyn_ref, dp_ref, dw_ref, ubuf, dybuf):
        a, g = _glu_with_halo(i, p_ref, ph_ref, ubuf, cw, tm)
        dyv = dy_ref[...]
        dybuf[0:tm, :] = dyv
        dybuf[tm:tm + HALO, :] = jnp.where(i < n - 1, dyn_ref[...], 0.0)
        base = HALO - (CONV_K - 1)
        du = jnp.zeros((tm, cw), F32)

        @pl.when(i == 0)
        def _():
            dw_ref[...] = jnp.zeros(dw_ref.shape, F32)

        for k in range(CONV_K):
            sh = CONV_K - 1 - k
            du = du + w_ref[k:k + 1, :] * dybuf[sh:sh + tm, :]
            dw_ref[k:k + 1, :] += _sum0(dyv * ubuf[base + k:base + k + tm, :])
        sg = _sigmoid(g)
        dp_ref[:, 0:cw] = (du * sg).astype(dp_ref.dtype)
        dp_ref[:, cw:2 * cw] = (du * a * sg * (1.0 - sg)).astype(dp_ref.dtype)

    return _rows("conv_bwd_taps", fn, s, tm, [(p, "r"), (p, "p"), (w, "f"), (dy, "r"), (dy, "n")],
                 [((s, 2 * cw), MM_DTYPE, "r"), ((CONV_K + 1, cw), F32, "a")],
                 scratch=[pltpu.VMEM((tm + HALO, cw), F32), pltpu.VMEM((tm + HALO, cw), F32)])


def _dp_assemble(p, gql, gkvl, dcq, dckv_parts, dp_conv, dpe, cw):
    s, width = p.shape
    ql, kvl = gql.shape[-1], gkvl.shape[-1]
    o_q, o_kv, o_pe = 2 * cw, 2 * cw + ql, 2 * cw + ql + kvl

    def fn(i, n, p_ref, gq_ref, gkv_ref, dcq_ref, dk1_ref, dk2_ref, dconv_ref, dpe_ref, dp_ref, dgq_ref, dgkv_ref):
        dp_ref[:, 0:o_q] = dconv_ref[...]
        dx, dg = _rms_bwd(p_ref[:, o_q:o_kv], gq_ref[...], dcq_ref[...], ql)
        dp_ref[:, o_q:o_kv] = dx.astype(dp_ref.dtype)
        _acc(dgq_ref, i, dg)
        dx, dg = _rms_bwd(p_ref[:, o_kv:o_pe], gkv_ref[...], dk1_ref[...] + dk2_ref[...], kvl)
        dp_ref[:, o_kv:o_pe] = dx.astype(dp_ref.dtype)
        _acc(dgkv_ref, i, dg)
        dp_ref[:, o_pe:width] = dpe_ref[...].astype(dp_ref.dtype)

    return _rows("dp_assemble", fn, s, _tile(s, 256),
                 [(p, "r"), (gql, "f"), (gkvl, "f"), (dcq, "r"), (dckv_parts[0], "r"), (dckv_parts[1], "r"),
                  (dp_conv, "r"), (dpe, "r")],
                 [((s, width), MM_DTYPE, "r"), ((1, ql), F32, "a"), ((1, kvl), F32, "a")])


def _add_leading(name, x, out_dtype):
    n, r, c = x.shape
    tm = _tile(r, 256)

    def fn(i, nt, x_ref, o_ref):
        tot = x_ref[0].astype(F32)
        for k in range(1, n):
            tot = tot + x_ref[k].astype(F32)
        o_ref[...] = tot.astype(o_ref.dtype)

    return _rows(name, fn, r, tm, [(x, "r")], [((r, c), out_dtype, "r")])[0]


def _adamw(w, g, m, v):
    r, c = w.shape
    tm = _tile(r, 256)

    def fn(i, n, w_ref, g_ref, m_ref, v_ref, d_ref, mo_ref, vo_ref):
        gv = g_ref[...]
        mn = ADAM_B1 * m_ref[...] + (1.0 - ADAM_B1) * gv
        vn = ADAM_B2 * v_ref[...] + (1.0 - ADAM_B2) * (gv * gv)
        mo_ref[...] = mn
        vo_ref[...] = vn
        m_hat = mn / (1.0 - ADAM_B1 ** ADAM_STEP)
        v_hat = vn / (1.0 - ADAM_B2 ** ADAM_STEP)
        d_ref[...] = -ADAM_LR * (m_hat / (jnp.sqrt(v_hat) + ADAM_EPS) + ADAM_WD * w_ref[...])

    return _rows("adamw", fn, r, tm, [(w, "r"), (g, "r"), (m, "r"), (v, "r")], [((r, c), F32, "r")] * 3)


def _place():
    x, y, c = lax.axis_index("x"), lax.axis_index("y"), lax.axis_index("c")
    other_chips = [(1 - x, y), (x, 1 - y), (1 - x, 1 - y)]
    return x, y, c, 2 * x + y, other_chips


def _gather_weights(shards):
    k_n = len(shards)

    def body(*refs):
        src, dst = refs[:k_n], refs[k_n:2 * k_n]
        send, recv, fsend, frecv, lsem = refs[2 * k_n:]
        x, y, c, me, chips = _place()
        local = [pltpu.make_async_copy(src[k], dst[k].at[:, me], lsem.at[k]) for k in range(k_n)]
        for cp in local:
            cp.start()

        def direct(k, j):
            return pltpu.make_async_remote_copy(
                src_ref=src[k].at[c], dst_ref=dst[k].at[c, me], send_sem=send.at[k, j], recv_sem=recv.at[k, j],
                device_id=(chips[j][0], chips[j][1], c), device_id_type=MESH)

        def landed(k, j, layer, sems):
            cx, cy = chips[j]
            blk = dst[k].at[layer, 2 * cx + cy]
            return pltpu.make_async_remote_copy(
                src_ref=blk, dst_ref=blk, send_sem=sems[0].at[k, j], recv_sem=sems[1].at[k, j],
                device_id=(x, y, 1 - c), device_id_type=MESH)

        sent = [direct(k, j) for j in range(3) for k in range(k_n)]
        for cp in sent:
            cp.start()
        passed = []
        for j in range(3):
            for k in range(k_n):
                landed(k, j, c, (send, recv)).wait_recv()
                fwd = landed(k, j, c, (fsend, frecv))
                fwd.start()
                passed.append(fwd)
        for j in range(3):
            for k in range(k_n):
                landed(k, j, 1 - c, (fsend, frecv)).wait_recv()
        for cp in sent + passed:
            cp.wait_send()
        for cp in local:
            cp.wait()

    dma = lambda: pltpu.SemaphoreType.DMA((k_n, 3))
    return pl.pallas_call(
        body, name="gather_weights", in_specs=[ANY] * k_n, out_specs=[ANY] * k_n,
        out_shape=[jax.ShapeDtypeStruct((s.shape[0], N_CHIPS) + s.shape[1:], s.dtype) for s in shards],
        scratch_shapes=[dma(), dma(), dma(), dma(), pltpu.SemaphoreType.DMA((k_n,))],
        compiler_params=pltpu.CompilerParams(has_side_effects=True))(*shards)


def _swap_layers(grads):
    k_n = len(grads)

    def body(*refs):
        src, dst = refs[:k_n], refs[k_n:2 * k_n]
        send, recv, lsem = refs[2 * k_n:]
        x, y, c, me, chips = _place()
        local = [pltpu.make_async_copy(src[k].at[c], dst[k].at[0], lsem.at[k]) for k in range(k_n)]
        away = [pltpu.make_async_remote_copy(
            src_ref=src[k].at[1 - c], dst_ref=dst[k].at[1], send_sem=send.at[k], recv_sem=recv.at[k],
            device_id=(x, y, 1 - c), device_id_type=MESH) for k in range(k_n)]
        for cp in local + away:
            cp.start()
        for cp in away:
            cp.wait()
        for cp in local:
            cp.wait()

    dma = lambda: pltpu.SemaphoreType.DMA((k_n,))
    return pl.pallas_call(
        body, name="swap_layers", in_specs=[ANY] * k_n, out_specs=[ANY] * k_n,
        out_shape=[jax.ShapeDtypeStruct(g.shape, g.dtype) for g in grads],
        scratch_shapes=[dma(), dma(), dma()],
        compiler_params=pltpu.CompilerParams(has_side_effects=True))(*grads)


def _scatter_to_owners(parts):
    k_n = len(parts)

    def body(*refs):
        src, dst = refs[:k_n], refs[k_n:2 * k_n]
        send, recv, lsem = refs[2 * k_n:]
        x, y, c, me, chips = _place()
        local = [pltpu.make_async_copy(src[k].at[me], dst[k].at[me], lsem.at[k]) for k in range(k_n)]
        away = [pltpu.make_async_remote_copy(
            src_ref=src[k].at[2 * chips[j][0] + chips[j][1]], dst_ref=dst[k].at[me],
            send_sem=send.at[k, j], recv_sem=recv.at[k, j],
            device_id=(chips[j][0], chips[j][1], c), device_id_type=MESH) for j in range(3) for k in range(k_n)]
        for cp in local + away:
            cp.start()
        for j in range(3):
            for k in range(k_n):
                cx, cy = chips[j]
                blk = dst[k].at[2 * cx + cy]
                pltpu.make_async_remote_copy(
                    src_ref=blk, dst_ref=blk, send_sem=send.at[k, j], recv_sem=recv.at[k, j],
                    device_id=(cx, cy, c), device_id_type=MESH).wait_recv()
        for cp in away:
            cp.wait_send()
        for cp in local:
            cp.wait()

    dma = lambda: pltpu.SemaphoreType.DMA((k_n, 3))
    return pl.pallas_call(
        body, name="scatter_to_owners", in_specs=[ANY] * k_n, out_specs=[ANY] * k_n,
        out_shape=[jax.ShapeDtypeStruct(p.shape, p.dtype) for p in parts],
        scratch_shapes=[dma(), dma(), pltpu.SemaphoreType.DMA((k_n,))],
        compiler_params=pltpu.CompilerParams(has_side_effects=True))(*parts)


def _share_layers(sums):
    k_n = len(sums)

    def body(*refs):
        src, dst = refs[:k_n], refs[k_n:2 * k_n]
        send, recv, lsem = refs[2 * k_n:]
        x, y, c, me, chips = _place()
        local = [pltpu.make_async_copy(src[k], dst[k].at[c], lsem.at[k]) for k in range(k_n)]
        away = [pltpu.make_async_remote_copy(
            src_ref=src[k], dst_ref=dst[k].at[c], send_sem=send.at[k], recv_sem=recv.at[k],
            device_id=(x, y, 1 - c), device_id_type=MESH) for k in range(k_n)]
        for cp in local + away:
            cp.start()
        for k in range(k_n):
            blk = dst[k].at[1 - c]
            pltpu.make_async_remote_copy(
                src_ref=blk, dst_ref=blk, send_sem=send.at[k], recv_sem=recv.at[k],
                device_id=(x, y, 1 - c), device_id_type=MESH).wait_recv()
        for cp in away:
            cp.wait_send()
        for cp in local:
            cp.wait()

    dma = lambda: pltpu.SemaphoreType.DMA((k_n,))
    return pl.pallas_call(
        body, name="share_layers", in_specs=[ANY] * k_n, out_specs=[ANY] * k_n,
        out_shape=[jax.ShapeDtypeStruct((2,) + s.shape, s.dtype) for s in sums],
        scratch_shapes=[dma(), dma(), dma()],
        compiler_params=pltpu.CompilerParams(has_side_effects=True))(*sums)


def _gather_small(vec):
    flips = [(fx, fy, fc) for fx in (0, 1) for fy in (0, 1) for fc in (0, 1)][1:]

    def body(src, dst, send, recv, lsem):
        x, y, c, _, _ = _place()
        me = 4 * x + 2 * y + c
        local = pltpu.make_async_copy(src, dst.at[me], lsem)
        local.start()
        away = []
        for j, (fx, fy, fc) in enumerate(flips):
            px, py, pc = x ^ fx, y ^ fy, c ^ fc
            away.append(pltpu.make_async_remote_copy(
                src_ref=src, dst_ref=dst.at[me], send_sem=send.at[j], recv_sem=recv.at[j],
                device_id=(px, py, pc), device_id_type=MESH))
        for cp in away:
            cp.start()
        for j, (fx, fy, fc) in enumerate(flips):
            blk = dst.at[4 * (x ^ fx) + 2 * (y ^ fy) + (c ^ fc)]
            pltpu.make_async_remote_copy(
                src_ref=blk, dst_ref=blk, send_sem=send.at[j], recv_sem=recv.at[j],
                device_id=(x ^ fx, y ^ fy, c ^ fc), device_id_type=MESH).wait_recv()
        for cp in away:
            cp.wait_send()
        local.wait()

    return pl.pallas_call(
        body, name="gather_small", in_specs=[ANY], out_specs=ANY,
        out_shape=jax.ShapeDtypeStruct((8,) + vec.shape, vec.dtype),
        scratch_shapes=[pltpu.SemaphoreType.DMA((7,)), pltpu.SemaphoreType.DMA((7,)), pltpu.SemaphoreType.DMA(())],
        compiler_params=pltpu.CompilerParams(has_side_effects=True))(vec)


SHARDED = ["ffn1_w_gate", "ffn1_w_up", "ffn1_w_down", "w_in", "w_uq", "w_ukv", "conv_w", "w_out",
           "ffn2_w_gate", "ffn2_w_up", "ffn2_w_down"]
SMALL = ["ffn1_norm", "mix_norm", "q_latent_norm", "kv_latent_norm", "q_norm", "k_norm", "conv_b", "conv_ln_g",
         "conv_ln_b", "ffn2_norm", "post_norm"]
ALL_WEIGHTS = ["ffn1_norm", "ffn1_w_gate", "ffn1_w_up", "ffn1_w_down", "mix_norm", "w_in", "q_latent_norm", "w_uq",
               "kv_latent_norm", "w_ukv", "q_norm", "k_norm", "conv_w", "conv_b", "conv_ln_g", "conv_ln_b", "w_out",
               "ffn2_norm", "ffn2_w_gate", "ffn2_w_up", "ffn2_w_down", "post_norm"]


def _rope_tables(s):
    pos = jnp.arange(s, dtype=F32)
    inv_freq = 1.0 / (ROPE_THETA ** (jnp.arange(0, QK_ROPE, 2, dtype=F32) / QK_ROPE))
    ang = pos[:, None] * inv_freq[None, :]
    cos, sin = jnp.cos(ang), jnp.sin(ang)
    half = QK_ROPE // 2
    one = jnp.ones((s, QK_NOPE), F32)
    zero = jnp.zeros((s, QK_NOPE), F32)
    pad = HEAD_PAD - QK_DIM
    z16 = jnp.zeros((s, half), F32)
    cosf = jnp.concatenate([one, cos, cos, jnp.ones((s, pad), F32)], axis=1)
    s1 = jnp.concatenate([zero, -sin, z16, jnp.zeros((s, pad), F32)], axis=1)
    s2 = jnp.concatenate([zero, z16, sin, jnp.zeros((s, pad), F32)], axis=1)
    return cosf, s1, s2


def _ffn_fwd(n, w, names, l):
    a = _proj("ffn_gate", n, w[names[0]], l, "nn")
    b = _proj("ffn_up", n, w[names[1]], l, "nn")
    hm = _swiglu_fwd(a, b)
    return a, b, hm, _contract("ffn_down", hm, w[names[2]], l, "nn")


def _ffn_bwd(n, a, b, hm, do, w, names, l, grads):
    layers = w[names[0]].shape[0]
    dh = _proj("ffn_down_t", do, w[names[2]], l, "nt")
    grads[names[2]] = _wgrad("ffn_down_wgrad", hm, do, WIRE_DTYPE, layers, l, grads.get(names[2]))
    da, db = _swiglu_bwd(a, b, dh)
    grads[names[0]] = _wgrad("ffn_gate_wgrad", n, da, WIRE_DTYPE, layers, l, grads.get(names[0]))
    grads[names[1]] = _wgrad("ffn_up_wgrad", n, db, WIRE_DTYPE, layers, l, grads.get(names[1]))
    return [_contract("ffn_gate_t", da, w[names[0]], l, "nt"), _contract("ffn_up_t", db, w[names[1]], l, "nt")]


def _vec(v, l):
    return v[l][None, :]


def _pad_lanes(v, width):
    return jnp.pad(v, [(0, 0)] * (v.ndim - 1) + [(0, width - v.shape[-1])])


def kernel(x, ffn1_norm, ffn1_w_gate, ffn1_w_up, ffn1_w_down, mix_norm, w_in, q_latent_norm, w_uq, kv_latent_norm, w_ukv, q_norm, k_norm, conv_w, conv_b, conv_ln_g, conv_ln_b, w_out, ffn2_norm, ffn2_w_gate, ffn2_w_up, ffn2_w_down, post_norm, loss_target, m_ffn1_norm, m_ffn1_w_gate, m_ffn1_w_up, m_ffn1_w_down, m_mix_norm, m_w_in, m_q_latent_norm, m_w_uq, m_kv_latent_norm, m_w_ukv, m_q_norm, m_k_norm, m_conv_w, m_conv_b, m_conv_ln_g, m_conv_ln_b, m_w_out, m_ffn2_norm, m_ffn2_w_gate, m_ffn2_w_up, m_ffn2_w_down, m_post_norm, v_ffn1_norm, v_ffn1_w_gate, v_ffn1_w_up, v_ffn1_w_down, v_mix_norm, v_w_in, v_q_latent_norm, v_w_uq, v_kv_latent_norm, v_w_ukv, v_q_norm, v_k_norm, v_conv_w, v_conv_b, v_conv_ln_g, v_conv_ln_b, v_w_out, v_ffn2_norm, v_ffn2_w_gate, v_ffn2_w_up, v_ffn2_w_down, v_post_norm):
    given = dict(locals())
    wts = {n: given[n] for n in ALL_WEIGHTS}
    mom = {n: given["m_" + n] for n in ALL_WEIGHTS}
    var = {n: given["v_" + n] for n in ALL_WEIGHTS}
    n_layers = ffn1_norm.shape[0]
    s, d = x.shape[1], x.shape[2]
    ql, kvl, cw = q_latent_norm.shape[-1], kv_latent_norm.shape[-1], conv_b.shape[-1]
    heads_per_chip = N_HEADS // N_CHIPS
    xs = x.reshape(s, d)
    target = loss_target.reshape(s, d)

    assert n_layers == 2, "core c of a chip finishes the gradients of layer c"
    local = [jnp.pad(wts[n], ((0, 0), (0, 1), (0, 0))).astype(MM_DTYPE) if n == "conv_w" else wts[n].astype(MM_DTYPE)
             for n in SHARDED]
    w = dict(zip(SHARDED, _gather_weights(local)))
    L = n_layers
    win = w["w_in"].transpose(0, 2, 1, 3).reshape(L, d, -1)
    zc = lambda n: jnp.zeros((L, d, n), MM_DTYPE)
    o_pe, o_cv = ql + kvl, ql + kvl + QK_ROPE
    w["w_in"] = jnp.concatenate([win[..., o_cv:], win[..., :o_pe], zc(QK_NOPE), win[..., o_pe:o_cv],
                                 zc(HEAD_PAD - QK_DIM)], axis=-1)[:, None]
    wq = w["w_uq"].reshape(L, N_CHIPS, ql, heads_per_chip, QK_DIM).transpose(0, 1, 3, 2, 4)
    w["wq"] = _pad_lanes(wq.reshape(L, N_HEADS, ql, QK_DIM), HEAD_PAD)
    wkv = w["w_ukv"].reshape(L, N_CHIPS, kvl, heads_per_chip, QK_NOPE + V_DIM).transpose(0, 1, 3, 2, 4)
    wkv = wkv.reshape(L, N_HEADS, kvl, QK_NOPE + V_DIM)
    w["wk"] = _pad_lanes(wkv[..., :QK_NOPE], HEAD_PAD)
    w["wv"] = wkv[..., QK_NOPE:]
    conv_full = w["conv_w"].transpose(0, 2, 1, 3).reshape(L, CONV_K + 1, cw).astype(F32)
    wout = w["w_out"].reshape(L, d, d)
    w["wo_a"] = wout[:, :N_HEADS * V_DIM].reshape(L, N_HEADS, V_DIM, d)
    w["wo_c"] = wout[:, N_HEADS * V_DIM:][:, None]
    tabs = _rope_tables(s)
    pe_block = (2 * cw + ql + kvl) // HEAD_PAD
    gq_pad, gk_pad = _pad_lanes(q_norm, HEAD_PAD), _pad_lanes(k_norm, HEAD_PAD)
    tq = _tile(s, 256)

    saved = []
    cur = xs
    for l in range(L):
        sv = {"x": cur}
        sv["n1"] = _norm_fwd(cur, _vec(ffn1_norm, l))
        sv["a1"], sv["b1"], sv["hm1"], o1 = _ffn_fwd(sv["n1"], w, SHARDED[0:3], l)
        sv["x1"], sv["h"] = _resid_norm(cur, [o1], 0.5, _vec(mix_norm, l), MM_DTYPE)
        sv["p"] = _proj("w_in", sv["h"], w["w_in"], l, "nn")[0]
        sv["cq"], sv["ckv"] = _mla_pre(sv["p"], _vec(q_latent_norm, l), _vec(kv_latent_norm, l), cw)
        sv["q_pre"] = _proj("wq", sv["cq"], w["wq"], l, "nn")
        sv["k_pre"] = _proj("wk", sv["ckv"], w["wk"], l, "nn")
        sv["v"] = _proj("wv", sv["ckv"], w["wv"], l, "nn", MM_DTYPE)
        sv["q"], sv["k"] = _qk_post(sv["q_pre"], sv["k_pre"], sv["p"], pe_block, tabs, _vec(gq_pad, l), _vec(gk_pad, l))
        sv["o"], sv["lse"] = _attn_fwd(sv["q"], sv["k"], sv["v"])
        sv["c"] = _conv_fwd(sv["p"], conv_full[l], _vec(conv_b, l), _vec(conv_ln_g, l), _vec(conv_ln_b, l))
        mix_a = _contract("wo_a", sv["o"], w["wo_a"], l, "nn")
        mix_c = _contract("wo_c", sv["c"][None], w["wo_c"], l, "nn")
        sv["x2"], sv["n2"] = _resid_norm(sv["x1"], [mix_a, mix_c], 1.0, _vec(ffn2_norm, l), MM_DTYPE)
        sv["a2"], sv["b2"], sv["hm2"], o2 = _ffn_fwd(sv["n2"], w, SHARDED[8:11], l)
        sv["x3"], cur = _final_fwd(sv["x2"], o2, _vec(post_norm, l))
        saved.append(sv)
    dy, loss_part = _loss(cur, target)

    grads = {}
    small = {n: [None] * L for n in SMALL}
    stacked = {}
    for l in reversed(range(L)):
        sv = saved[l]
        dx3, do2, small["post_norm"][l] = _norm_bwd(sv["x3"], _vec(post_norm, l), [dy], None, 0.5)
        dn2 = _ffn_bwd(sv["n2"], sv["a2"], sv["b2"], sv["hm2"], do2, w, SHARDED[8:11], l, grads)
        dx2, dx2b, small["ffn2_norm"][l] = _norm_bwd(sv["x2"], _vec(ffn2_norm, l), dn2, dx3, 1.0)
        do = _proj("wo_a_t", dx2b, w["wo_a"], l, "nt")
        dc = _proj("wo_c_t", dx2b, w["wo_c"], l, "nt")[0]
        g_wo_a = _wgrad("wo_a_wgrad", sv["o"], dx2b, WIRE_DTYPE)
        g_wo_c = _wgrad("wo_c_wgrad", sv["c"], dx2b[None], WIRE_DTYPE)
        stacked.setdefault("w_out", [None] * L)[l] = jnp.concatenate(
            [g_wo_a.reshape(N_HEADS * V_DIM, d), g_wo_c[0]], axis=0).reshape(N_CHIPS, d // N_CHIPS, d)
        dq, delta = _attn_dq(sv["q"], sv["k"], sv["v"], sv["o"], do, sv["lse"])
        row = lambda t: t.reshape(N_HEADS, s // tq, 1, tq)
        dk, dv = _attn_dkv(sv["q"], sv["k"], sv["v"], do, row(sv["lse"]), row(delta))
        dq_pre, dk_pre, dpe, small["q_norm"][l], small["k_norm"][l] = _qk_post_bwd(
            sv["q_pre"], sv["k_pre"], sv["p"], pe_block, tabs, _vec(gq_pad, l), _vec(gk_pad, l), dq, dk)
        dcq = _contract("wq_t", dq_pre, w["wq"], l, "nt")
        dckv = [_contract("wk_t", dk_pre, w["wk"], l, "nt"), _contract("wv_t", dv, w["wv"], l, "nt")]
        g_wq = _wgrad("wq_wgrad", sv["cq"], dq_pre, WIRE_DTYPE)[..., :QK_DIM]
        stacked.setdefault("w_uq", [None] * L)[l] = g_wq.reshape(N_CHIPS, heads_per_chip, ql, QK_DIM).transpose(
            0, 2, 1, 3).reshape(N_CHIPS, ql, heads_per_chip * QK_DIM)
        g_wkv = jnp.concatenate([_wgrad("wk_wgrad", sv["ckv"], dk_pre, WIRE_DTYPE)[..., :QK_NOPE],
                                 _wgrad("wv_wgrad", sv["ckv"], dv, WIRE_DTYPE)], axis=-1)
        stacked.setdefault("w_ukv", [None] * L)[l] = g_wkv.reshape(N_CHIPS, heads_per_chip, kvl, QK_NOPE + V_DIM).transpose(
            0, 2, 1, 3).reshape(N_CHIPS, kvl, heads_per_chip * (QK_NOPE + V_DIM))
        dyc, small["conv_ln_g"][l], small["conv_ln_b"][l], small["conv_b"][l] = _conv_bwd_ln(
            sv["p"], conv_full[l], _vec(conv_b, l), _vec(conv_ln_g, l), _vec(conv_ln_b, l), dc)
        dp_conv, g_conv = _conv_bwd_taps(sv["p"], conv_full[l], dyc)
        stacked.setdefault("conv_w", [None] * L)[l] = g_conv.reshape(CONV_K + 1, N_CHIPS, cw // N_CHIPS).transpose(
            1, 0, 2).astype(WIRE_DTYPE)
        dp, small["q_latent_norm"][l], small["kv_latent_norm"][l] = _dp_assemble(
            sv["p"], _vec(q_latent_norm, l), _vec(kv_latent_norm, l), dcq, dckv, dp_conv, dpe, cw)
        dh = _contract("w_in_t", dp[None], w["w_in"], l, "nt")
        g_win = _wgrad("w_in_wgrad", sv["h"], dp[None], WIRE_DTYPE)[0]
        o_q, o_pe_col = 2 * cw, 2 * cw + ql + kvl + QK_NOPE
        g_win = jnp.concatenate([g_win[:, o_q:o_q + ql + kvl], g_win[:, o_pe_col:o_pe_col + QK_ROPE], g_win[:, :o_q]], axis=1)
        stacked.setdefault("w_in", [None] * L)[l] = g_win.reshape(d, N_CHIPS, -1).transpose(1, 0, 2)
        dx1, do1, small["mix_norm"][l] = _norm_bwd(sv["x1"], _vec(mix_norm, l), [dh], dx2, 0.5)
        dn1 = _ffn_bwd(sv["n1"], sv["a1"], sv["b1"], sv["hm1"], do1, w, SHARDED[0:3], l, grads)
        dy, _, small["ffn1_norm"][l] = _norm_bwd(sv["x"], _vec(ffn1_norm, l), dn1, dx1, 1.0)
    grad_x = dy.reshape(x.shape)
    for n, per_layer in stacked.items():
        grads[n] = jnp.stack(per_layer, axis=0)

    shapes = [grads[n].shape for n in SHARDED]
    swapped = _swap_layers([grads[n] for n in SHARDED])
    chip_sums = [_add_leading("add_sibling", t.reshape(2, sh[1] * sh[2], sh[3]), WIRE_DTYPE).reshape(sh[1:])
                 for t, sh in zip(swapped, shapes)]
    from_chips = _scatter_to_owners(chip_sums)
    layer_sums = [_add_leading("add_chips", t, F32) for t in from_chips]
    full = dict(zip(SHARDED, _share_layers(layer_sums)))
    full["conv_w"] = full["conv_w"][:, :CONV_K]

    width = max(wts[n].shape[-1] for n in SMALL)
    pack = lambda vals: jnp.concatenate([_pad_lanes(v.reshape(-1, v.shape[-1]), width) for v in vals], axis=0)
    part = pack([jnp.concatenate(small[n], axis=0)[..., :wts[n].shape[-1]] for n in SMALL])
    rows = part.shape[0]
    part = jnp.pad(part, [(0, -rows % 8), (0, 0)])
    small_sum = _add_leading("add_devices", _gather_small(part), F32)
    for i, n in enumerate(SMALL):
        full[n] = small_sum[i * L:(i + 1) * L, :wts[n].shape[-1]]

    delta_w, new_m, new_v = {}, {}, {}
    for n in SHARDED:
        flat = lambda t: t.reshape(-1, t.shape[-1])
        r = flat(wts[n]).shape[0]
        fix = (lambda t: jnp.pad(flat(t), [(0, -r % 8), (0, 0)])) if r % 8 else flat
        outs = _adamw(fix(wts[n]), fix(full[n]), fix(mom[n]), fix(var[n]))
        delta_w[n], new_m[n], new_v[n] = [o[:r].reshape(wts[n].shape) for o in outs]
    pk = lambda src: jnp.pad(pack([src[n] for n in SMALL]), [(0, -rows % 8), (0, 0)])
    outs = _adamw(pk(wts), small_sum, pk(mom), pk(var))
    for i, n in enumerate(SMALL):
        delta_w[n], new_m[n], new_v[n] = [o[i * L:(i + 1) * L, :wts[n].shape[-1]] for o in outs]

    loss = lax.psum(loss_part[0, 0], ("x", "y", "c"))
    return (loss, grad_x, *[full[n] for n in ALL_WEIGHTS], *[delta_w[n] for n in ALL_WEIGHTS],
            *[new_m[n] for n in ALL_WEIGHTS], *[new_v[n] for n in ALL_WEIGHTS])
```

```python
import jax
import jax.numpy as jnp
from jax import lax
from jax.experimental import pallas as pl
from jax.experimental.pallas import tpu as pltpu

F32 = jnp.float32
MM_DTYPE = jnp.bfloat16
WIRE_DTYPE = jnp.bfloat16
EPS = 1e-6
N_HEADS = 8
QK_NOPE = 64
QK_ROPE = 32
QK_DIM = QK_NOPE + QK_ROPE
V_DIM = 64
HEAD_PAD = 128
HEADS_PER_STEP = 2
CHUNK_SHIFT = 6
CONV_K = 31
HALO = 32
ROPE_THETA = 10000.0
N_CHIPS = 4
ADAM_LR, ADAM_B1, ADAM_B2, ADAM_EPS, ADAM_WD, ADAM_STEP = 0.001, 0.9, 0.999, 1e-08, 0.01, 10
VMEM_LIMIT_BYTES = 56 * 2 ** 20
MESH = pl.DeviceIdType.MESH
ANY = pl.BlockSpec(memory_space=pl.ANY)

_DIMS = {
    "nn": (((1,), (0,)), ((), ())),
    "nt": (((1,), (1,)), ((), ())),
    "tn": (((0,), (0,)), ((), ())),
}


def _params(n_axes):
    return pltpu.CompilerParams(dimension_semantics=("arbitrary",) * n_axes, vmem_limit_bytes=VMEM_LIMIT_BYTES)


def _tile(n, cap):
    t = cap
    while n % t:
        t //= 2
    return t


def _mm(name, a, b, dims, grid, a_spec, b_spec, o_spec, out_shape, nk, acc_shape, into=None):
    n_axes = len(grid)

    def body(*refs):
        if into is not None:
            refs = refs[1:]
        a_ref, b_ref, o_ref = refs[:3]
        part = lax.dot_general(a_ref[...].astype(MM_DTYPE), b_ref[...].astype(MM_DTYPE), _DIMS[dims],
                               preferred_element_type=F32)
        if nk == 1:
            o_ref[...] = part.astype(o_ref.dtype)
        else:
            acc_ref = refs[3]
            k = pl.program_id(n_axes - 1)

            @pl.when(k == 0)
            def _():
                acc_ref[...] = part

            @pl.when(k > 0)
            def _():
                acc_ref[...] += part

            @pl.when(k == nk - 1)
            def _():
                o_ref[...] = acc_ref[...].astype(o_ref.dtype)

    in_specs = [a_spec, b_spec]
    args = [a, b]
    aliases = {}
    if into is not None:
        in_specs = [ANY] + in_specs
        args = [into] + args
        aliases = {0: 0}
    return pl.pallas_call(
        body, name=name, grid=grid, in_specs=in_specs, out_specs=o_spec, out_shape=out_shape,
        scratch_shapes=[] if nk == 1 else [pltpu.VMEM(acc_shape, F32)],
        input_output_aliases=aliases, compiler_params=_params(n_axes))(*args)


def _proj(name, a, w, l, dims, out_dtype=F32):
    s, k = a.shape
    g = w.shape[1]
    n = w.shape[3] if dims == "nn" else w.shape[2]
    tm = _tile(s, 512)
    return _mm(name, a, w, dims, (g, s // tm),
               pl.BlockSpec((tm, k), lambda j, i: (i, 0)),
               pl.BlockSpec((None, None) + w.shape[2:], lambda j, i: (l, j, 0, 0)),
               pl.BlockSpec((None, tm, n), lambda j, i: (j, i, 0)),
               jax.ShapeDtypeStruct((g, s, n), out_dtype), 1, None)


def _contract(name, a, w, l, dims, out_dtype=F32):
    g, s, kb = a.shape
    n = w.shape[3] if dims == "nn" else w.shape[2]
    tm = _tile(s, 512)
    return _mm(name, a, w, dims, (s // tm, g),
               pl.BlockSpec((None, tm, kb), lambda i, j: (j, i, 0)),
               pl.BlockSpec((None, None) + w.shape[2:], lambda i, j: (l, j, 0, 0)),
               pl.BlockSpec((tm, n), lambda i, j: (i, 0)),
               jax.ShapeDtypeStruct((s, n), out_dtype), g, (tm, n))


def _wgrad(name, a, b, out_dtype, layers=None, l=None, into=None):
    g = a.shape[0] if a.ndim == 3 else b.shape[0]
    s, k = a.shape[-2:]
    n = b.shape[-1]
    tm = _tile(s, 512)

    def spec(x, w):
        if x.ndim == 3:
            return pl.BlockSpec((None, tm, w), lambda j, i: (j, i, 0))
        return pl.BlockSpec((tm, w), lambda j, i: (i, 0))

    if layers is None:
        o_spec = pl.BlockSpec((None, k, n), lambda j, i: (j, 0, 0))
        out_shape = jax.ShapeDtypeStruct((g, k, n), out_dtype)
    else:
        o_spec = pl.BlockSpec((None, None, k, n), lambda j, i: (l, j, 0, 0))
        out_shape = jax.ShapeDtypeStruct((layers, g, k, n), out_dtype)
    return _mm(name, a, b, "tn", (g, s // tm), spec(a, k), spec(b, n), o_spec, out_shape, s // tm, (k, n), into=into)


def _rows(name, fn, s, tm, ins, outs, scratch=()):
    n = s // tm

    def spec(shape, kind):
        nd = len(shape)
        if kind == "r":
            return pl.BlockSpec(shape[:-2] + (tm, shape[-1]), lambda i: (0,) * (nd - 2) + (i, 0))
        if kind in ("f", "a"):
            return pl.BlockSpec(shape, lambda i: (0,) * nd)
        if kind == "p":
            return pl.BlockSpec((HALO, shape[-1]), lambda i: (jnp.maximum(i * (tm // HALO) - 1, 0), 0))
        if kind == "n":
            return pl.BlockSpec((HALO, shape[-1]), lambda i: (jnp.minimum((i + 1) * (tm // HALO), s // HALO - 1), 0))
        j = kind[1]
        return pl.BlockSpec((tm, HEAD_PAD), lambda i: (i, j))

    def body(*refs):
        fn(pl.program_id(0), n, *refs)

    return pl.pallas_call(
        body, name=name, grid=(n,),
        in_specs=[spec(a.shape, kind) for a, kind in ins],
        out_specs=[spec(shape, kind) for shape, _, kind in outs],
        out_shape=[jax.ShapeDtypeStruct(shape, dtype) for shape, dtype, _ in outs],
        scratch_shapes=list(scratch), compiler_params=_params(1))(*[a for a, _ in ins])


def _acc(ref, i, val):
    @pl.when(i == 0)
    def _():
        ref[...] = val

    @pl.when(i > 0)
    def _():
        ref[...] += val


def _sum0(x):
    return jnp.sum(x, axis=0, keepdims=True)


def _rstd(x, n):
    return lax.rsqrt(jnp.sum(x * x, axis=-1, keepdims=True) * (1.0 / n) + EPS)


def _rms_bwd(x, g, dy, n):
    r = _rstd(x, n)
    xh = x * r
    dyg = dy * g
    dx = r * (dyg - xh * (jnp.sum(dyg * xh, axis=-1, keepdims=True) * (1.0 / n)))
    return dx, _sum0(dy * xh)


def _sigmoid(x):
    return 1.0 / (1.0 + jnp.exp(-x))


def _norm_fwd(x, g):
    s, d = x.shape

    def fn(i, n, x_ref, g_ref, o_ref):
        xv = x_ref[...]
        o_ref[...] = (xv * _rstd(xv, d) * g_ref[...]).astype(o_ref.dtype)

    return _rows("norm_fwd", fn, s, _tile(s, 256), [(x, "r"), (g, "f")], [((s, d), MM_DTYPE, "r")])[0]


def _resid_norm(x, adds, scale, g, norm_dtype):
    s, d = x.shape
    k = len(adds)

    def fn(i, n, *refs):
        x_ref, add_refs, g_ref, xo_ref, no_ref = refs[0], refs[1:1 + k], refs[1 + k], refs[2 + k], refs[3 + k]
        tot = add_refs[0][...]
        for r in add_refs[1:]:
            tot = tot + r[...]
        xv = x_ref[...] + scale * tot
        xo_ref[...] = xv
        no_ref[...] = (xv * _rstd(xv, d) * g_ref[...]).astype(no_ref.dtype)

    return _rows("resid_norm", fn, s, _tile(s, 256), [(x, "r")] + [(a, "r") for a in adds] + [(g, "f")],
                 [((s, d), F32, "r"), ((s, d), norm_dtype, "r")])


def _norm_bwd(x, g, dn_parts, dres, out_scale):
    s, d = x.shape
    k = len(dn_parts)
    has_res = dres is not None

    def fn(i, n, *refs):
        x_ref, g_ref = refs[0], refs[1]
        dn_refs = refs[2:2 + k]
        pos = 2 + k
        dn = dn_refs[0][...]
        for r in dn_refs[1:]:
            dn = dn + r[...]
        dx, dg = _rms_bwd(x_ref[...], g_ref[...], dn, d)
        if has_res:
            dx = dx + refs[pos][...]
            pos += 1
        dx_ref, dxs_ref, dg_ref = refs[pos:pos + 3]
        dx_ref[...] = dx
        dxs_ref[...] = (out_scale * dx).astype(dxs_ref.dtype)
        _acc(dg_ref, i, dg)

    ins = [(x, "r"), (g, "f")] + [(p, "r") for p in dn_parts] + ([(dres, "r")] if has_res else [])
    return _rows("norm_bwd", fn, s, _tile(s, 256), ins,
                 [((s, d), F32, "r"), ((s, d), MM_DTYPE, "r"), ((1, d), F32, "a")])


def _swiglu_fwd(a, b):
    g, s, n4 = a.shape

    def fn(i, n, a_ref, b_ref, o_ref):
        av = a_ref[...]
        o_ref[...] = (av * _sigmoid(av) * b_ref[...]).astype(o_ref.dtype)

    return _rows("swiglu_fwd", fn, s, _tile(s, 256), [(a, "r"), (b, "r")], [((g, s, n4), MM_DTYPE, "r")])[0]


def _swiglu_bwd(a, b, dh):
    g, s, n4 = a.shape

    def fn(i, n, a_ref, b_ref, dh_ref, da_ref, db_ref):
        av, dhv = a_ref[...], dh_ref[...]
        sg = _sigmoid(av)
        da_ref[...] = (dhv * b_ref[...] * (sg * (1.0 + av * (1.0 - sg)))).astype(da_ref.dtype)
        db_ref[...] = (dhv * (av * sg)).astype(db_ref.dtype)

    return _rows("swiglu_bwd", fn, s, _tile(s, 256), [(a, "r"), (b, "r"), (dh, "r")],
                 [((g, s, n4), MM_DTYPE, "r"), ((g, s, n4), MM_DTYPE, "r")])


def _final_fwd(x, o, g):
    return _resid_norm(x, [o], 0.5, g, F32)


def _loss(y, t):
    s, d = y.shape

    def fn(i, n, y_ref, t_ref, dy_ref, l_ref):
        e = y_ref[...] - t_ref[...]
        dy_ref[...] = e * (1.0 / d)
        _acc(l_ref, i, (0.5 / d) * jnp.sum(jnp.sum(e * e, axis=-1, keepdims=True), axis=0, keepdims=True))

    return _rows("loss", fn, s, _tile(s, 256), [(y, "r"), (t, "r")], [((s, d), F32, "r"), ((1, 1), F32, "a")])


def _mla_pre(p, gql, gkvl, cw):
    s = p.shape[0]
    ql, kvl = gql.shape[-1], gkvl.shape[-1]
    o_q, o_kv = 2 * cw, 2 * cw + ql

    def fn(i, n, p_ref, gq_ref, gkv_ref, cq_ref, ckv_ref):
        cq = p_ref[:, o_q:o_q + ql]
        cq_ref[...] = (cq * _rstd(cq, ql) * gq_ref[...]).astype(cq_ref.dtype)
        ckv = p_ref[:, o_kv:o_kv + kvl]
        ckv_ref[...] = (ckv * _rstd(ckv, kvl) * gkv_ref[...]).astype(ckv_ref.dtype)

    return _rows("mla_pre", fn, s, _tile(s, 256), [(p, "r"), (gql, "f"), (gkvl, "f")],
                 [((s, ql), MM_DTYPE, "r"), ((s, kvl), MM_DTYPE, "r")])


def _rope(x, cosf, s1, s2):
    return x * cosf + pltpu.roll(x, HEAD_PAD - 16, 1) * s1 + pltpu.roll(x, 16, 1) * s2


def _rope_bwd(d, cosf, s1, s2):
    return d * cosf + pltpu.roll(d * s1, 16, 1) + pltpu.roll(d * s2, HEAD_PAD - 16, 1)


def _qk_post(q_pre, k_pre, p, pe_block, tabs, gq, gk):
    s, hw = q_pre.shape
    cosf, s1, s2 = tabs

    def fn(i, n, q_ref, k_ref, pe_ref, c_ref, s1_ref, s2_ref, gq_ref, gk_ref, qo_ref, ko_ref):
        c, a1, a2, pe = c_ref[...], s1_ref[...], s2_ref[...], pe_ref[...]
        for hh in range(N_HEADS):
            cols = slice(hh * HEAD_PAD, (hh + 1) * HEAD_PAD)
            qv = q_ref[:, cols]
            qo_ref[:, cols] = _rope(qv * _rstd(qv, QK_DIM) * gq_ref[...], c, a1, a2).astype(qo_ref.dtype)
            kv = k_ref[:, cols] + pe
            ko_ref[:, cols] = _rope(kv * _rstd(kv, QK_DIM) * gk_ref[...], c, a1, a2).astype(ko_ref.dtype)

    return _rows("qk_post", fn, s, _tile(s, 256),
                 [(q_pre, "r"), (k_pre, "r"), (p, ("c", pe_block)), (cosf, "r"), (s1, "r"), (s2, "r"), (gq, "f"), (gk, "f")],
                 [((s, hw), MM_DTYPE, "r"), ((s, hw), MM_DTYPE, "r")])


def _qk_post_bwd(q_pre, k_pre, p, pe_block, tabs, gq, gk, dq, dk):
    s, hw = q_pre.shape
    cosf, s1, s2 = tabs

    def fn(i, n, q_ref, k_ref, pe_ref, c_ref, s1_ref, s2_ref, gq_ref, gk_ref, dq_ref, dk_ref,
           dqo_ref, dko_ref, dpe_ref, dgq_ref, dgk_ref):
        c, a1, a2, pe = c_ref[...], s1_ref[...], s2_ref[...], pe_ref[...]
        lane = lax.broadcasted_iota(jnp.int32, pe.shape, 1)
        is_pe = (lane >= QK_NOPE) & (lane < QK_DIM)
        dpe = jnp.zeros(pe.shape, F32)
        dgq = jnp.zeros((1, HEAD_PAD), F32)
        dgk = jnp.zeros((1, HEAD_PAD), F32)
        for hh in range(N_HEADS):
            cols = slice(hh * HEAD_PAD, (hh + 1) * HEAD_PAD)
            dxq, g1 = _rms_bwd(q_ref[:, cols], gq_ref[...], _rope_bwd(dq_ref[:, cols], c, a1, a2), QK_DIM)
            dqo_ref[:, cols] = dxq.astype(dqo_ref.dtype)
            dxk, g2 = _rms_bwd(k_ref[:, cols] + pe, gk_ref[...], _rope_bwd(dk_ref[:, cols], c, a1, a2), QK_DIM)
            dko_ref[:, cols] = dxk.astype(dko_ref.dtype)
            dpe = dpe + dxk
            dgq = dgq + g1
            dgk = dgk + g2
        dpe_ref[...] = jnp.where(is_pe, dpe, 0.0)
        _acc(dgq_ref, i, dgq)
        _acc(dgk_ref, i, dgk)

    return _rows("qk_post_bwd", fn, s, _tile(s, 256),
                 [(q_pre, "r"), (k_pre, "r"), (p, ("c", pe_block)), (cosf, "r"), (s1, "r"), (s2, "r"), (gq, "f"), (gk, "f"),
                  (dq, "r"), (dk, "r")],
                 [((s, hw), MM_DTYPE, "r"), ((s, hw), MM_DTYPE, "r"), ((s, HEAD_PAD), F32, "r"),
                  ((1, HEAD_PAD), F32, "a"), ((1, HEAD_PAD), F32, "a")])


def _chunk_mask(tq, key_rows):
    r = jnp.right_shift(lax.broadcasted_iota(jnp.int32, (tq, tq), 0), CHUNK_SHIFT)
    c = jnp.right_shift(lax.broadcasted_iota(jnp.int32, (tq, tq), 1), CHUNK_SHIFT)
    return (r <= c) if key_rows else (c <= r)


def _attn_specs(s, tq):
    w = HEADS_PER_STEP * HEAD_PAD
    tile = pl.BlockSpec((tq, w), lambda hb, i: (i, hb))
    whole = pl.BlockSpec((s, w), lambda hb, i: (0, hb))
    stat = pl.BlockSpec((None, tq, HEADS_PER_STEP), lambda hb, i: (hb, i, 0))
    return tile, whole, stat


def _head(hh):
    return slice(hh * HEAD_PAD, (hh + 1) * HEAD_PAD)


def _attn_fwd(q, k, v):
    s, hw = q.shape
    tq = _tile(s, 256)
    groups = N_HEADS // HEADS_PER_STEP
    scale = QK_DIM ** -0.5
    tile, whole, stat = _attn_specs(s, tq)

    def body(q_ref, k_ref, v_ref, o_ref, lse_ref):
        qi = pl.program_id(1)
        qv = [q_ref[:, _head(hh)] for hh in range(HEADS_PER_STEP)]

        def block(kb, carry, diag):
            off = pl.multiple_of(kb * tq, tq)
            out = []
            for hh in range(HEADS_PER_STEP):
                m, l, acc = carry[hh]
                ks = k_ref[pl.ds(off, tq), _head(hh)]
                vs = v_ref[pl.ds(off, tq), _head(hh)]
                sc = lax.dot_general(qv[hh], ks, _DIMS["nt"], preferred_element_type=F32) * scale
                if diag:
                    sc = jnp.where(_chunk_mask(tq, False), sc, -1e30)
                mn = jnp.maximum(m, jnp.max(sc, axis=-1, keepdims=True))
                al = jnp.exp(m - mn)
                pr = jnp.exp(sc - mn)
                l = al * l + jnp.sum(pr, axis=-1, keepdims=True)
                acc = al * acc + jnp.dot(pr.astype(MM_DTYPE), vs, preferred_element_type=F32)
                out.append((mn, l, acc))
            return tuple(out)

        init = tuple((jnp.full((tq, 1), -1e30, F32), jnp.zeros((tq, 1), F32), jnp.zeros((tq, HEAD_PAD), F32))
                     for _ in range(HEADS_PER_STEP))
        carry = lax.fori_loop(0, qi, lambda kb, c: block(kb, c, False), init)
        carry = block(qi, carry, True)
        for hh in range(HEADS_PER_STEP):
            m, l, acc = carry[hh]
            o_ref[:, _head(hh)] = acc / l
            lse_ref[:, hh:hh + 1] = m + jnp.log(l)

    return pl.pallas_call(
        body, name="attn_fwd", grid=(groups, s // tq),
        in_specs=[tile, whole, whole], out_specs=[tile, stat],
        out_shape=[jax.ShapeDtypeStruct((s, hw), F32), jax.ShapeDtypeStruct((groups, s, HEADS_PER_STEP), F32)],
        compiler_params=_params(2))(q, k, v)


def _attn_dq(q, k, v, o, do, lse):
    s, hw = q.shape
    tq = _tile(s, 256)
    groups = N_HEADS // HEADS_PER_STEP
    scale = QK_DIM ** -0.5
    tile, whole, stat = _attn_specs(s, tq)

    def body(q_ref, k_ref, v_ref, o_ref, do_ref, lse_ref, dq_ref, delta_ref):
        qi = pl.program_id(1)
        qv, dob, lse, delta = [], [], [], []
        for hh in range(HEADS_PER_STEP):
            qv.append(q_ref[:, _head(hh)])
            dov = do_ref[:, _head(hh)]
            dob.append(dov.astype(MM_DTYPE))
            lse.append(lse_ref[:, hh:hh + 1])
            delta.append(jnp.sum(dov * o_ref[:, _head(hh)], axis=-1, keepdims=True))
            delta_ref[:, hh:hh + 1] = delta[hh]

        def block(kb, dq, diag):
            off = pl.multiple_of(kb * tq, tq)
            out = []
            for hh in range(HEADS_PER_STEP):
                ks = k_ref[pl.ds(off, tq), _head(hh)]
                vs = v_ref[pl.ds(off, tq), _head(hh)]
                sc = lax.dot_general(qv[hh], ks, _DIMS["nt"], preferred_element_type=F32) * scale
                if diag:
                    sc = jnp.where(_chunk_mask(tq, False), sc, -1e30)
                pr = jnp.exp(sc - lse[hh])
                dpr = lax.dot_general(dob[hh], vs, _DIMS["nt"], preferred_element_type=F32)
                ds = (pr * (dpr - delta[hh]) * scale).astype(MM_DTYPE)
                out.append(dq[hh] + jnp.dot(ds, ks, preferred_element_type=F32))
            return tuple(out)

        init = tuple(jnp.zeros((tq, HEAD_PAD), F32) for _ in range(HEADS_PER_STEP))
        dq = block(qi, lax.fori_loop(0, qi, lambda kb, c: block(kb, c, False), init), True)
        for hh in range(HEADS_PER_STEP):
            dq_ref[:, _head(hh)] = dq[hh]

    return pl.pallas_call(
        body, name="attn_dq", grid=(groups, s // tq),
        in_specs=[tile, whole, whole, tile, tile, stat], out_specs=[tile, stat],
        out_shape=[jax.ShapeDtypeStruct((s, hw), F32), jax.ShapeDtypeStruct((groups, s, HEADS_PER_STEP), F32)],
        compiler_params=_params(2))(q, k, v, o, do, lse)


def _attn_dkv(q, k, v, do, lse_row, delta_row):
    s, hw = q.shape
    tq = _tile(s, 256)
    nq = s // tq
    groups = N_HEADS // HEADS_PER_STEP
    scale = QK_DIM ** -0.5
    tile, whole, _ = _attn_specs(s, tq)
    rowvec = pl.BlockSpec((None, nq, HEADS_PER_STEP, tq), lambda hb, i: (hb, 0, 0, 0))

    def body(q_ref, k_ref, v_ref, do_ref, lse_ref, delta_ref, dk_ref, dv_ref):
        ki = pl.program_id(1)
        kv = [k_ref[:, _head(hh)] for hh in range(HEADS_PER_STEP)]
        vv = [v_ref[:, _head(hh)] for hh in range(HEADS_PER_STEP)]

        def block(qb, carry, diag):
            off = pl.multiple_of(qb * tq, tq)
            out = []
            for hh in range(HEADS_PER_STEP):
                dk, dvv = carry[hh]
                qs = q_ref[pl.ds(off, tq), _head(hh)]
                dos = do_ref[pl.ds(off, tq), _head(hh)].astype(MM_DTYPE)
                sc = lax.dot_general(kv[hh], qs, _DIMS["nt"], preferred_element_type=F32) * scale
                if diag:
                    sc = jnp.where(_chunk_mask(tq, True), sc, -1e30)
                pr = jnp.exp(sc - lse_ref[qb, hh:hh + 1, :])
                dpr = lax.dot_general(vv[hh], dos, _DIMS["nt"], preferred_element_type=F32)
                ds = (pr * (dpr - delta_ref[qb, hh:hh + 1, :]) * scale).astype(MM_DTYPE)
                dvv = dvv + jnp.dot(pr.astype(MM_DTYPE), dos, preferred_element_type=F32)
                dk = dk + jnp.dot(ds, qs, preferred_element_type=F32)
                out.append((dk, dvv))
            return tuple(out)

        init = tuple((jnp.zeros((tq, HEAD_PAD), F32), jnp.zeros((tq, HEAD_PAD), F32)) for _ in range(HEADS_PER_STEP))
        carry = lax.fori_loop(ki + 1, nq, lambda qb, c: block(qb, c, False), block(ki, init, True))
        for hh in range(HEADS_PER_STEP):
            dk_ref[:, _head(hh)] = carry[hh][0]
            dv_ref[:, _head(hh)] = carry[hh][1].astype(dv_ref.dtype)

    return pl.pallas_call(
        body, name="attn_dkv", grid=(groups, nq),
        in_specs=[whole, tile, tile, whole, rowvec, rowvec], out_specs=[tile, tile],
        out_shape=[jax.ShapeDtypeStruct((s, hw), F32), jax.ShapeDtypeStruct((s, hw), MM_DTYPE)],
        compiler_params=_params(2))(q, k, v, do, lse_row, delta_row)


def _glu_with_halo(i, p_ref, ph_ref, ubuf, cw, tm):
    a, g = p_ref[:, 0:cw], p_ref[:, cw:2 * cw]
    ubuf[HALO:HALO + tm, :] = a * _sigmoid(g)
    ah, gh = ph_ref[:, 0:cw], ph_ref[:, cw:2 * cw]
    ubuf[0:HALO, :] = jnp.where(i > 0, ah * _sigmoid(gh), 0.0)
    return a, g


def _conv_taps(ubuf, w_ref, tm):
    base = HALO - (CONV_K - 1)
    y = w_ref[0:1, :] * ubuf[base:base + tm, :]
    for k in range(1, CONV_K):
        y = y + w_ref[k:k + 1, :] * ubuf[base + k:base + k + tm, :]
    return y


def _layer_norm_stats(y, cw):
    mu = jnp.sum(y, axis=-1, keepdims=True) * (1.0 / cw)
    yc = y - mu
    r = lax.rsqrt(jnp.sum(yc * yc, axis=-1, keepdims=True) * (1.0 / cw) + EPS)
    return yc * r, r


def _conv_fwd(p, w, cb, lg, lb):
    s = p.shape[0]
    cw = cb.shape[-1]
    tm = _tile(s, 256)

    def fn(i, n, p_ref, ph_ref, w_ref, cb_ref, lg_ref, lb_ref, o_ref, ubuf):
        _glu_with_halo(i, p_ref, ph_ref, ubuf, cw, tm)
        yh, _ = _layer_norm_stats(_conv_taps(ubuf, w_ref, tm) + cb_ref[...], cw)
        z = yh * lg_ref[...] + lb_ref[...]
        o_ref[...] = (z * _sigmoid(z)).astype(o_ref.dtype)

    return _rows("conv_fwd", fn, s, tm, [(p, "r"), (p, "p"), (w, "f"), (cb, "f"), (lg, "f"), (lb, "f")],
                 [((s, cw), MM_DTYPE, "r")], scratch=[pltpu.VMEM((tm + HALO, cw), F32)])[0]


def _conv_bwd_ln(p, w, cb, lg, lb, dc):
    s = p.shape[0]
    cw = cb.shape[-1]
    tm = _tile(s, 256)

    def fn(i, n, p_ref, ph_ref, w_ref, cb_ref, lg_ref, lb_ref, dc_ref, dy_ref, dlg_ref, dlb_ref, dcb_ref, ubuf):
        _glu_with_halo(i, p_ref, ph_ref, ubuf, cw, tm)
        yh, r = _layer_norm_stats(_conv_taps(ubuf, w_ref, tm) + cb_ref[...], cw)
        z = yh * lg_ref[...] + lb_ref[...]
        sg = _sigmoid(z)
        dz = dc_ref[...] * (sg * (1.0 + z * (1.0 - sg)))
        dyh = dz * lg_ref[...]
        m1 = jnp.sum(dyh, axis=-1, keepdims=True) * (1.0 / cw)
        m2 = jnp.sum(dyh * yh, axis=-1, keepdims=True) * (1.0 / cw)
        dy = r * (dyh - m1 - yh * m2)
        dy_ref[...] = dy
        _acc(dlg_ref, i, _sum0(dz * yh))
        _acc(dlb_ref, i, _sum0(dz))
        _acc(dcb_ref, i, _sum0(dy))

    return _rows("conv_bwd_ln", fn, s, tm,
                 [(p, "r"), (p, "p"), (w, "f"), (cb, "f"), (lg, "f"), (lb, "f"), (dc, "r")],
                 [((s, cw), F32, "r"), ((1, cw), F32, "a"), ((1, cw), F32, "a"), ((1, cw), F32, "a")],
                 scratch=[pltpu.VMEM((tm + HALO, cw), F32)])


def _conv_bwd_taps(p, w, dy):
    s = p.shape[0]
    cw = w.shape[-1]
    tm = _tile(s, 256)

    def fn(i, n, p_ref, ph_ref, w_ref, dy_ref, dyn_ref, dp_ref, dw_ref, ubuf, dybuf):
        a, g = _glu_with_halo(i, p_ref, ph_ref, ubuf, cw, tm)
        dyv = dy_ref[...]
        dybuf[0:tm, :] = dyv
        dybuf[tm:tm + HALO, :] = jnp.where(i < n - 1, dyn_ref[...], 0.0)
        base = HALO - (CONV_K - 1)
        du = jnp.zeros((tm, cw), F32)

        @pl.when(i == 0)
        def _():
            dw_ref[...] = jnp.zeros(dw_ref.shape, F32)

        for k in range(CONV_K):
            sh = CONV_K - 1 - k
            du = du + w_ref[k:k + 1, :] * dybuf[sh:sh + tm, :]
            dw_ref[k:k + 1, :] += _sum0(dyv * ubuf[base + k:base + k + tm, :])
        sg = _sigmoid(g)
        dp_ref[:, 0:cw] = (du * sg).astype(dp_ref.dtype)
        dp_ref[:, cw:2 * cw] = (du * a * sg * (1.0 - sg)).astype(dp_ref.dtype)

    return _rows("conv_bwd_taps", fn, s, tm, [(p, "r"), (p, "p"), (w, "f"), (dy, "r"), (dy, "n")],
                 [((s, 2 * cw), MM_DTYPE, "r"), ((CONV_K + 1, cw), F32, "a")],
                 scratch=[pltpu.VMEM((tm + HALO, cw), F32), pltpu.VMEM((tm + HALO, cw), F32)])


def _dp_assemble(p, gql, gkvl, dcq, dckv_parts, dp_conv, dpe, cw):
    s, width = p.shape
    ql, kvl = gql.shape[-1], gkvl.shape[-1]
    o_q, o_kv, o_pe = 2 * cw, 2 * cw + ql, 2 * cw + ql + kvl

    def fn(i, n, p_ref, gq_ref, gkv_ref, dcq_ref, dk1_ref, dk2_ref, dconv_ref, dpe_ref, dp_ref, dgq_ref, dgkv_ref):
        dp_ref[:, 0:o_q] = dconv_ref[...]
        dx, dg = _rms_bwd(p_ref[:, o_q:o_kv], gq_ref[...], dcq_ref[...], ql)
        dp_ref[:, o_q:o_kv] = dx.astype(dp_ref.dtype)
        _acc(dgq_ref, i, dg)
        dx, dg = _rms_bwd(p_ref[:, o_kv:o_pe], gkv_ref[...], dk1_ref[...] + dk2_ref[...], kvl)
        dp_ref[:, o_kv:o_pe] = dx.astype(dp_ref.dtype)
        _acc(dgkv_ref, i, dg)
        dp_ref[:, o_pe:width] = dpe_ref[...].astype(dp_ref.dtype)

    return _rows("dp_assemble", fn, s, _tile(s, 256),
                 [(p, "r"), (gql, "f"), (gkvl, "f"), (dcq, "r"), (dckv_parts[0], "r"), (dckv_parts[1], "r"),
                  (dp_conv, "r"), (dpe, "r")],
                 [((s, width), MM_DTYPE, "r"), ((1, ql), F32, "a"), ((1, kvl), F32, "a")])


def _add_leading(name, x, out_dtype):
    n, r, c = x.shape
    tm = _tile(r, 256)

    def fn(i, nt, x_ref, o_ref):
        tot = x_ref[0].astype(F32)
        for k in range(1, n):
            tot = tot + x_ref[k].astype(F32)
        o_ref[...] = tot.astype(o_ref.dtype)

    return _rows(name, fn, r, tm, [(x, "r")], [((r, c), out_dtype, "r")])[0]


def _prefetch_call(name, fn, idx, grid, in_specs, out_spec, out_shape, args, aliases=None):
    return pl.pallas_call(
        fn, name=name, out_shape=out_shape,
        grid_spec=pltpu.PrefetchScalarGridSpec(num_scalar_prefetch=1, grid=grid, in_specs=in_specs, out_specs=out_spec),
        input_output_aliases=aliases or {}, compiler_params=_params(len(grid)))(idx, *args)


def _add_sibling(idx, mine, theirs):
    _, r, c = mine.shape
    tm = _tile(r, 256)

    def fn(idx_ref, a_ref, b_ref, o_ref):
        o_ref[...] = (a_ref[...].astype(F32) + b_ref[...].astype(F32)).astype(o_ref.dtype)

    return _prefetch_call(
        "add_sibling", fn, idx, (r // tm,),
        [pl.BlockSpec((None, tm, c), lambda i, ix: (ix[0], i, 0)), pl.BlockSpec((tm, c), lambda i, ix: (i, 0))],
        pl.BlockSpec((tm, c), lambda i, ix: (i, 0)), jax.ShapeDtypeStruct((r, c), WIRE_DTYPE), [mine, theirs])


def _add_chips(idx, own, others):
    _, r, c = own.shape
    tm = _tile(r, 256)

    def fn(idx_ref, a_ref, b_ref, o_ref):
        tot = a_ref[...].astype(F32)
        for k in range(3):
            tot = tot + b_ref[k].astype(F32)
        o_ref[...] = tot

    return _prefetch_call(
        "add_chips", fn, idx, (r // tm,),
        [pl.BlockSpec((None, tm, c), lambda i, ix: (ix[1], i, 0)), pl.BlockSpec((3, tm, c), lambda i, ix: (0, i, 0))],
        pl.BlockSpec((None, tm, c), lambda i, ix: (ix[0], i, 0)), jax.ShapeDtypeStruct((2, r, c), F32), [own, others])


def _adamw(w, g, m, v):
    r, c = w.shape
    tm = _tile(r, 256)

    def fn(i, n, w_ref, g_ref, m_ref, v_ref, d_ref, mo_ref, vo_ref):
        gv = g_ref[...]
        mn = ADAM_B1 * m_ref[...] + (1.0 - ADAM_B1) * gv
        vn = ADAM_B2 * v_ref[...] + (1.0 - ADAM_B2) * (gv * gv)
        mo_ref[...] = mn
        vo_ref[...] = vn
        m_hat = mn / (1.0 - ADAM_B1 ** ADAM_STEP)
        v_hat = vn / (1.0 - ADAM_B2 ** ADAM_STEP)
        d_ref[...] = -ADAM_LR * (m_hat / (jnp.sqrt(v_hat) + ADAM_EPS) + ADAM_WD * w_ref[...])

    return _rows("adamw", fn, r, tm, [(w, "r"), (g, "r"), (m, "r"), (v, "r")], [((r, c), F32, "r")] * 3)


def _place():
    x, y, c = lax.axis_index("x"), lax.axis_index("y"), lax.axis_index("c")
    other_chips = [(1 - x, y), (x, 1 - y), (1 - x, 1 - y)]
    return x, y, c, 2 * x + y, other_chips


def _gather_weights(bufs, shards):
    k_n = len(shards)

    def body(*refs):
        src, dst = refs[k_n:2 * k_n], refs[2 * k_n:3 * k_n]
        send, recv, fsend, frecv = refs[3 * k_n:]
        x, y, c, me, chips = _place()

        def direct(k, j):
            return pltpu.make_async_remote_copy(
                src_ref=src[k].at[c], dst_ref=dst[k].at[c, me], send_sem=send.at[k, j], recv_sem=recv.at[k, j],
                device_id=(chips[j][0], chips[j][1], c), device_id_type=MESH)

        def landed(k, j, layer, sems):
            cx, cy = chips[j]
            blk = dst[k].at[layer, 2 * cx + cy]
            return pltpu.make_async_remote_copy(
                src_ref=blk, dst_ref=blk, send_sem=sems[0].at[k, j], recv_sem=sems[1].at[k, j],
                device_id=(x, y, 1 - c), device_id_type=MESH)

        sent = [direct(k, j) for j in range(3) for k in range(k_n)]
        for cp in sent:
            cp.start()
        passed = []
        for j in range(3):
            for k in range(k_n):
                landed(k, j, c, (send, recv)).wait_recv()
                fwd = landed(k, j, c, (fsend, frecv))
                fwd.start()
                passed.append(fwd)
        for j in range(3):
            for k in range(k_n):
                landed(k, j, 1 - c, (fsend, frecv)).wait_recv()
        for cp in sent + passed:
            cp.wait_send()

    dma = lambda: pltpu.SemaphoreType.DMA((k_n, 3))
    return pl.pallas_call(
        body, name="gather_weights", in_specs=[ANY] * (2 * k_n), out_specs=[ANY] * k_n,
        out_shape=[jax.ShapeDtypeStruct(b.shape, b.dtype) for b in bufs],
        input_output_aliases={k: k for k in range(k_n)},
        scratch_shapes=[dma(), dma(), dma(), dma()],
        compiler_params=pltpu.CompilerParams(has_side_effects=True))(*bufs, *shards)


def _swap_layers(grads):
    k_n = len(grads)

    def body(*refs):
        src, dst = refs[:k_n], refs[k_n:2 * k_n]
        send, recv = refs[2 * k_n:]
        x, y, c, me, chips = _place()
        away = [pltpu.make_async_remote_copy(
            src_ref=src[k].at[1 - c], dst_ref=dst[k], send_sem=send.at[k], recv_sem=recv.at[k],
            device_id=(x, y, 1 - c), device_id_type=MESH) for k in range(k_n)]
        for cp in away:
            cp.start()
        for cp in away:
            cp.wait()

    dma = lambda: pltpu.SemaphoreType.DMA((k_n,))
    return pl.pallas_call(
        body, name="swap_layers", in_specs=[ANY] * k_n, out_specs=[ANY] * k_n,
        out_shape=[jax.ShapeDtypeStruct(g.shape[1:], g.dtype) for g in grads],
        scratch_shapes=[dma(), dma()],
        compiler_params=pltpu.CompilerParams(has_side_effects=True))(*grads)


def _scatter_to_owners(parts):
    k_n = len(parts)

    def body(*refs):
        src, dst = refs[:k_n], refs[k_n:2 * k_n]
        send, recv = refs[2 * k_n:]
        x, y, c, me, chips = _place()
        away = [pltpu.make_async_remote_copy(
            src_ref=src[k].at[2 * chips[j][0] + chips[j][1]], dst_ref=dst[k].at[j],
            send_sem=send.at[k, j], recv_sem=recv.at[k, j],
            device_id=(chips[j][0], chips[j][1], c), device_id_type=MESH) for j in range(3) for k in range(k_n)]
        for cp in away:
            cp.start()
        for cp in away:
            cp.wait()

    dma = lambda: pltpu.SemaphoreType.DMA((k_n, 3))
    return pl.pallas_call(
        body, name="scatter_to_owners", in_specs=[ANY] * k_n, out_specs=[ANY] * k_n,
        out_shape=[jax.ShapeDtypeStruct((3,) + p.shape[1:], p.dtype) for p in parts],
        scratch_shapes=[dma(), dma()],
        compiler_params=pltpu.CompilerParams(has_side_effects=True))(*parts)


def _share_layers(sums):
    k_n = len(sums)

    def body(*refs):
        dst = refs[k_n:2 * k_n]
        send, recv = refs[2 * k_n:]
        x, y, c, me, chips = _place()
        away = [pltpu.make_async_remote_copy(
            src_ref=dst[k].at[c], dst_ref=dst[k].at[c], send_sem=send.at[k], recv_sem=recv.at[k],
            device_id=(x, y, 1 - c), device_id_type=MESH) for k in range(k_n)]
        for cp in away:
            cp.start()
        for k in range(k_n):
            blk = dst[k].at[1 - c]
            pltpu.make_async_remote_copy(
                src_ref=blk, dst_ref=blk, send_sem=send.at[k], recv_sem=recv.at[k],
                device_id=(x, y, 1 - c), device_id_type=MESH).wait_recv()
        for cp in away:
            cp.wait_send()

    dma = lambda: pltpu.SemaphoreType.DMA((k_n,))
    return pl.pallas_call(
        body, name="share_layers", in_specs=[ANY] * k_n, out_specs=[ANY] * k_n,
        out_shape=[jax.ShapeDtypeStruct(s.shape, s.dtype) for s in sums],
        input_output_aliases={k: k for k in range(k_n)},
        scratch_shapes=[dma(), dma()],
        compiler_params=pltpu.CompilerParams(has_side_effects=True))(*sums)


def _gather_small(vec):
    flips = [(fx, fy, fc) for fx in (0, 1) for fy in (0, 1) for fc in (0, 1)][1:]

    def body(src, dst, send, recv, lsem):
        x, y, c, _, _ = _place()
        me = 4 * x + 2 * y + c
        local = pltpu.make_async_copy(src, dst.at[me], lsem)
        local.start()
        away = []
        for j, (fx, fy, fc) in enumerate(flips):
            px, py, pc = x ^ fx, y ^ fy, c ^ fc
            away.append(pltpu.make_async_remote_copy(
                src_ref=src, dst_ref=dst.at[me], send_sem=send.at[j], recv_sem=recv.at[j],
                device_id=(px, py, pc), device_id_type=MESH))
        for cp in away:
            cp.start()
        for j, (fx, fy, fc) in enumerate(flips):
            blk = dst.at[4 * (x ^ fx) + 2 * (y ^ fy) + (c ^ fc)]
            pltpu.make_async_remote_copy(
                src_ref=blk, dst_ref=blk, send_sem=send.at[j], recv_sem=recv.at[j],
                device_id=(x ^ fx, y ^ fy, c ^ fc), device_id_type=MESH).wait_recv()
        for cp in away:
            cp.wait_send()
        local.wait()

    return pl.pallas_call(
        body, name="gather_small", in_specs=[ANY], out_specs=ANY,
        out_shape=jax.ShapeDtypeStruct((8,) + vec.shape, vec.dtype),
        scratch_shapes=[pltpu.SemaphoreType.DMA((7,)), pltpu.SemaphoreType.DMA((7,)), pltpu.SemaphoreType.DMA(())],
        compiler_params=pltpu.CompilerParams(has_side_effects=True))(vec)


SHARDED = ["ffn1_w_gate", "ffn1_w_up", "ffn1_w_down", "w_in", "w_uq", "w_ukv", "conv_w", "w_out",
           "ffn2_w_gate", "ffn2_w_up", "ffn2_w_down"]
SMALL = ["ffn1_norm", "mix_norm", "q_latent_norm", "kv_latent_norm", "q_norm", "k_norm", "conv_b", "conv_ln_g",
         "conv_ln_b", "ffn2_norm", "post_norm"]
ALL_WEIGHTS = ["ffn1_norm", "ffn1_w_gate", "ffn1_w_up", "ffn1_w_down", "mix_norm", "w_in", "q_latent_norm", "w_uq",
               "kv_latent_norm", "w_ukv", "q_norm", "k_norm", "conv_w", "conv_b", "conv_ln_g", "conv_ln_b", "w_out",
               "ffn2_norm", "ffn2_w_gate", "ffn2_w_up", "ffn2_w_down", "post_norm"]


def _rope_tables(s):
    pos = jnp.arange(s, dtype=F32)
    inv_freq = 1.0 / (ROPE_THETA ** (jnp.arange(0, QK_ROPE, 2, dtype=F32) / QK_ROPE))
    ang = pos[:, None] * inv_freq[None, :]
    cos, sin = jnp.cos(ang), jnp.sin(ang)
    half = QK_ROPE // 2
    one = jnp.ones((s, QK_NOPE), F32)
    zero = jnp.zeros((s, QK_NOPE), F32)
    pad = HEAD_PAD - QK_DIM
    z16 = jnp.zeros((s, half), F32)
    cosf = jnp.concatenate([one, cos, cos, jnp.ones((s, pad), F32)], axis=1)
    s1 = jnp.concatenate([zero, -sin, z16, jnp.zeros((s, pad), F32)], axis=1)
    s2 = jnp.concatenate([zero, z16, sin, jnp.zeros((s, pad), F32)], axis=1)
    return cosf, s1, s2


def _ffn_fwd(n, w, names, l):
    a = _proj("ffn_gate", n, w[names[0]], l, "nn")
    b = _proj("ffn_up", n, w[names[1]], l, "nn")
    hm = _swiglu_fwd(a, b)
    return a, b, hm, _contract("ffn_down", hm, w[names[2]], l, "nn")


def _ffn_bwd(n, a, b, hm, do, w, names, l, grads):
    layers = w[names[0]].shape[0]
    dh = _proj("ffn_down_t", do, w[names[2]], l, "nt")
    grads[names[2]] = _wgrad("ffn_down_wgrad", hm, do, WIRE_DTYPE, layers, l, grads.get(names[2]))
    da, db = _swiglu_bwd(a, b, dh)
    grads[names[0]] = _wgrad("ffn_gate_wgrad", n, da, WIRE_DTYPE, layers, l, grads.get(names[0]))
    grads[names[1]] = _wgrad("ffn_up_wgrad", n, db, WIRE_DTYPE, layers, l, grads.get(names[1]))
    return [_contract("ffn_gate_t", da, w[names[0]], l, "nt"), _contract("ffn_up_t", db, w[names[1]], l, "nt")]


def _vec(v, l):
    return v[l][None, :]


def _pad_lanes(v, width):
    return jnp.pad(v, [(0, 0)] * (v.ndim - 1) + [(0, width - v.shape[-1])])


def kernel(x, ffn1_norm, ffn1_w_gate, ffn1_w_up, ffn1_w_down, mix_norm, w_in, q_latent_norm, w_uq, kv_latent_norm, w_ukv, q_norm, k_norm, conv_w, conv_b, conv_ln_g, conv_ln_b, w_out, ffn2_norm, ffn2_w_gate, ffn2_w_up, ffn2_w_down, post_norm, loss_target, m_ffn1_norm, m_ffn1_w_gate, m_ffn1_w_up, m_ffn1_w_down, m_mix_norm, m_w_in, m_q_latent_norm, m_w_uq, m_kv_latent_norm, m_w_ukv, m_q_norm, m_k_norm, m_conv_w, m_conv_b, m_conv_ln_g, m_conv_ln_b, m_w_out, m_ffn2_norm, m_ffn2_w_gate, m_ffn2_w_up, m_ffn2_w_down, m_post_norm, v_ffn1_norm, v_ffn1_w_gate, v_ffn1_w_up, v_ffn1_w_down, v_mix_norm, v_w_in, v_q_latent_norm, v_w_uq, v_kv_latent_norm, v_w_ukv, v_q_norm, v_k_norm, v_conv_w, v_conv_b, v_conv_ln_g, v_conv_ln_b, v_w_out, v_ffn2_norm, v_ffn2_w_gate, v_ffn2_w_up, v_ffn2_w_down, v_post_norm):
    given = dict(locals())
    wts = {n: given[n] for n in ALL_WEIGHTS}
    mom = {n: given["m_" + n] for n in ALL_WEIGHTS}
    var = {n: given["v_" + n] for n in ALL_WEIGHTS}
    n_layers = ffn1_norm.shape[0]
    s, d = x.shape[1], x.shape[2]
    ql, kvl, cw = q_latent_norm.shape[-1], kv_latent_norm.shape[-1], conv_b.shape[-1]
    heads_per_chip = N_HEADS // N_CHIPS
    xs = x.reshape(s, d)
    target = loss_target.reshape(s, d)

    assert n_layers == 2, "core c of a chip finishes the gradients of layer c"
    local = [jnp.pad(wts[n], ((0, 0), (0, 1), (0, 0))).astype(MM_DTYPE) if n == "conv_w" else wts[n].astype(MM_DTYPE)
             for n in SHARDED]
    L = n_layers
    my_c = lax.axis_index("c")
    my_chip = 2 * lax.axis_index("x") + lax.axis_index("y")
    place = jnp.stack([my_c, my_chip]).astype(jnp.int32)
    bufs = [lax.dynamic_update_slice(lax.empty((L, N_CHIPS) + t.shape[1:], t.dtype), t[:, None], (0, my_chip, 0, 0))
            for t in local]
    w = dict(zip(SHARDED, _gather_weights(bufs, local)))
    hw = N_HEADS * HEAD_PAD
    win = w["w_in"].transpose(0, 2, 1, 3).reshape(L, d, -1)
    zc = lambda n: jnp.zeros((L, d, n), MM_DTYPE)
    o_pe, o_cv = ql + kvl, ql + kvl + QK_ROPE
    w["w_in"] = jnp.concatenate([win[..., o_cv:], win[..., :o_pe], zc(QK_NOPE), win[..., o_pe:o_cv],
                                 zc(HEAD_PAD - QK_DIM)], axis=-1)[:, None]
    wq = w["w_uq"].transpose(0, 2, 1, 3).reshape(L, ql, N_HEADS, QK_DIM)
    w["wq"] = _pad_lanes(wq, HEAD_PAD).reshape(L, 1, ql, hw)
    wkv = w["w_ukv"].transpose(0, 2, 1, 3).reshape(L, kvl, N_HEADS, QK_NOPE + V_DIM)
    w["wk"] = _pad_lanes(wkv[..., :QK_NOPE], HEAD_PAD).reshape(L, 1, kvl, hw)
    w["wv"] = _pad_lanes(wkv[..., QK_NOPE:], HEAD_PAD).reshape(L, 1, kvl, hw)
    conv_full = w["conv_w"].transpose(0, 2, 1, 3).reshape(L, CONV_K + 1, cw).astype(F32)
    wout = w["w_out"].reshape(L, d, d)
    wo_a = wout[:, :N_HEADS * V_DIM].reshape(L, N_HEADS, V_DIM, d)
    w["wo_a"] = jnp.pad(wo_a, ((0, 0), (0, 0), (0, HEAD_PAD - V_DIM), (0, 0))).reshape(L, 1, hw, d)
    w["wo_c"] = wout[:, N_HEADS * V_DIM:][:, None]
    tabs = _rope_tables(s)
    pe_block = (2 * cw + ql + kvl) // HEAD_PAD
    gq_pad, gk_pad = _pad_lanes(q_norm, HEAD_PAD), _pad_lanes(k_norm, HEAD_PAD)
    tq = _tile(s, 256)
    groups = N_HEADS // HEADS_PER_STEP

    saved = []
    cur = xs
    for l in range(L):
        sv = {"x": cur}
        sv["n1"] = _norm_fwd(cur, _vec(ffn1_norm, l))
        sv["a1"], sv["b1"], sv["hm1"], o1 = _ffn_fwd(sv["n1"], w, SHARDED[0:3], l)
        sv["x1"], sv["h"] = _resid_norm(cur, [o1], 0.5, _vec(mix_norm, l), MM_DTYPE)
        sv["p"] = _proj("w_in", sv["h"], w["w_in"], l, "nn")[0]
        sv["cq"], sv["ckv"] = _mla_pre(sv["p"], _vec(q_latent_norm, l), _vec(kv_latent_norm, l), cw)
        sv["q_pre"] = _proj("wq", sv["cq"], w["wq"], l, "nn")[0]
        sv["k_pre"] = _proj("wk", sv["ckv"], w["wk"], l, "nn")[0]
        sv["v"] = _proj("wv", sv["ckv"], w["wv"], l, "nn", MM_DTYPE)[0]
        sv["q"], sv["k"] = _qk_post(sv["q_pre"], sv["k_pre"], sv["p"], pe_block, tabs, _vec(gq_pad, l), _vec(gk_pad, l))
        sv["o"], sv["lse"] = _attn_fwd(sv["q"], sv["k"], sv["v"])
        sv["c"] = _conv_fwd(sv["p"], conv_full[l], _vec(conv_b, l), _vec(conv_ln_g, l), _vec(conv_ln_b, l))
        mix_a = _contract("wo_a", sv["o"][None], w["wo_a"], l, "nn")
        mix_c = _contract("wo_c", sv["c"][None], w["wo_c"], l, "nn")
        sv["x2"], sv["n2"] = _resid_norm(sv["x1"], [mix_a, mix_c], 1.0, _vec(ffn2_norm, l), MM_DTYPE)
        sv["a2"], sv["b2"], sv["hm2"], o2 = _ffn_fwd(sv["n2"], w, SHARDED[8:11], l)
        sv["x3"], cur = _final_fwd(sv["x2"], o2, _vec(post_norm, l))
        saved.append(sv)
    dy, loss_part = _loss(cur, target)

    grads = {}
    small = {n: [None] * L for n in SMALL}
    stacked = {}
    for l in reversed(range(L)):
        sv = saved[l]
        dx3, do2, small["post_norm"][l] = _norm_bwd(sv["x3"], _vec(post_norm, l), [dy], None, 0.5)
        dn2 = _ffn_bwd(sv["n2"], sv["a2"], sv["b2"], sv["hm2"], do2, w, SHARDED[8:11], l, grads)
        dx2, dx2b, small["ffn2_norm"][l] = _norm_bwd(sv["x2"], _vec(ffn2_norm, l), dn2, dx3, 1.0)
        do = _proj("wo_a_t", dx2b, w["wo_a"], l, "nt")[0]
        dc = _proj("wo_c_t", dx2b, w["wo_c"], l, "nt")[0]
        g_wo_a = _wgrad("wo_a_wgrad", sv["o"], dx2b[None], WIRE_DTYPE)[0].reshape(N_HEADS, HEAD_PAD, d)[:, :V_DIM]
        g_wo_c = _wgrad("wo_c_wgrad", sv["c"], dx2b[None], WIRE_DTYPE)
        stacked.setdefault("w_out", [None] * L)[l] = jnp.concatenate(
            [g_wo_a.reshape(N_HEADS * V_DIM, d), g_wo_c[0]], axis=0).reshape(N_CHIPS, d // N_CHIPS, d)
        dq, delta = _attn_dq(sv["q"], sv["k"], sv["v"], sv["o"], do, sv["lse"])
        row = lambda t: t.reshape(groups, s // tq, tq, HEADS_PER_STEP).transpose(0, 1, 3, 2)
        dk, dv = _attn_dkv(sv["q"], sv["k"], sv["v"], do, row(sv["lse"]), row(delta))
        dq_pre, dk_pre, dpe, small["q_norm"][l], small["k_norm"][l] = _qk_post_bwd(
            sv["q_pre"], sv["k_pre"], sv["p"], pe_block, tabs, _vec(gq_pad, l), _vec(gk_pad, l), dq, dk)
        dcq = _contract("wq_t", dq_pre[None], w["wq"], l, "nt")
        dckv = [_contract("wk_t", dk_pre[None], w["wk"], l, "nt"), _contract("wv_t", dv[None], w["wv"], l, "nt")]
        per_head = lambda t, rows: t[0].reshape(rows, N_HEADS, HEAD_PAD)
        g_wq = per_head(_wgrad("wq_wgrad", sv["cq"], dq_pre[None], WIRE_DTYPE), ql)[..., :QK_DIM]
        stacked.setdefault("w_uq", [None] * L)[l] = g_wq.reshape(ql, N_CHIPS, -1).transpose(1, 0, 2)
        g_wkv = jnp.concatenate([per_head(_wgrad("wk_wgrad", sv["ckv"], dk_pre[None], WIRE_DTYPE), kvl)[..., :QK_NOPE],
                                 per_head(_wgrad("wv_wgrad", sv["ckv"], dv[None], WIRE_DTYPE), kvl)[..., :V_DIM]], axis=-1)
        stacked.setdefault("w_ukv", [None] * L)[l] = g_wkv.reshape(kvl, N_CHIPS, -1).transpose(1, 0, 2)
        dyc, small["conv_ln_g"][l], small["conv_ln_b"][l], small["conv_b"][l] = _conv_bwd_ln(
            sv["p"], conv_full[l], _vec(conv_b, l), _vec(conv_ln_g, l), _vec(conv_ln_b, l), dc)
        dp_conv, g_conv = _conv_bwd_taps(sv["p"], conv_full[l], dyc)
        stacked.setdefault("conv_w", [None] * L)[l] = g_conv.reshape(CONV_K + 1, N_CHIPS, cw // N_CHIPS).transpose(
            1, 0, 2).astype(WIRE_DTYPE)
        dp, small["q_latent_norm"][l], small["kv_latent_norm"][l] = _dp_assemble(
            sv["p"], _vec(q_latent_norm, l), _vec(kv_latent_norm, l), dcq, dckv, dp_conv, dpe, cw)
        dh = _contract("w_in_t", dp[None], w["w_in"], l, "nt")
        g_win = _wgrad("w_in_wgrad", sv["h"], dp[None], WIRE_DTYPE)[0]
        o_q, o_pe_col = 2 * cw, 2 * cw + ql + kvl + QK_NOPE
        g_win = jnp.concatenate([g_win[:, o_q:o_q + ql + kvl], g_win[:, o_pe_col:o_pe_col + QK_ROPE], g_win[:, :o_q]], axis=1)
        stacked.setdefault("w_in", [None] * L)[l] = g_win.reshape(d, N_CHIPS, -1).transpose(1, 0, 2)
        dx1, do1, small["mix_norm"][l] = _norm_bwd(sv["x1"], _vec(mix_norm, l), [dh], dx2, 0.5)
        dn1 = _ffn_bwd(sv["n1"], sv["a1"], sv["b1"], sv["hm1"], do1, w, SHARDED[0:3], l, grads)
        dy, _, small["ffn1_norm"][l] = _norm_bwd(sv["x"], _vec(ffn1_norm, l), dn1, dx1, 1.0)
    grad_x = dy.reshape(x.shape)
    for n, per_layer in stacked.items():
        grads[n] = jnp.stack(per_layer, axis=0)

    part_sums = [grads[n] for n in SHARDED]
    swapped = _swap_layers(part_sums)
    chip_sums = [_add_sibling(place, g.reshape(2, -1, g.shape[-1]), t.reshape(-1, t.shape[-1])).reshape(t.shape)
                 for g, t in zip(part_sums, swapped)]
    from_chips = _scatter_to_owners(chip_sums)
    layer_sums = [_add_chips(place, own, others) for own, others in zip(chip_sums, from_chips)]
    full = dict(zip(SHARDED, _share_layers(layer_sums)))
    full["conv_w"] = full["conv_w"][:, :CONV_K]

    width = max(wts[n].shape[-1] for n in SMALL)
    pack = lambda vals: jnp.concatenate([_pad_lanes(v.reshape(-1, v.shape[-1]), width) for v in vals], axis=0)
    part = pack([jnp.concatenate(small[n], axis=0)[..., :wts[n].shape[-1]] for n in SMALL])
    rows = part.shape[0]
    part = jnp.pad(part, [(0, -rows % 8), (0, 0)])
    small_sum = _add_leading("add_devices", _gather_small(part), F32)
    for i, n in enumerate(SMALL):
        full[n] = small_sum[i * L:(i + 1) * L, :wts[n].shape[-1]]

    delta_w, new_m, new_v = {}, {}, {}
    for n in SHARDED:
        flat = lambda t: t.reshape(-1, t.shape[-1])
        r = flat(wts[n]).shape[0]
        fix = (lambda t: jnp.pad(flat(t), [(0, -r % 8), (0, 0)])) if r % 8 else flat
        outs = _adamw(fix(wts[n]), fix(full[n]), fix(mom[n]), fix(var[n]))
        delta_w[n], new_m[n], new_v[n] = [o[:r].reshape(wts[n].shape) for o in outs]
    pk = lambda src: jnp.pad(pack([src[n] for n in SMALL]), [(0, -rows % 8), (0, 0)])
    outs = _adamw(pk(wts), small_sum, pk(mom), pk(var))
    for i, n in enumerate(SMALL):
        delta_w[n], new_m[n], new_v[n] = [o[i * L:(i + 1) * L, :wts[n].shape[-1]] for o in outs]

    loss = lax.psum(loss_part[0, 0], ("x", "y", "c"))
    return (loss, grad_x, *[full[n] for n in ALL_WEIGHTS], *[delta_w[n] for n in ALL_WEIGHTS],
            *[new_m[n] for n in ALL_WEIGHTS], *[new_v[n] for n in ALL_WEIGHTS])
```

```python
import jax
import jax.numpy as jnp
from jax import lax
from jax.experimental import pallas as pl
from jax.experimental.pallas import tpu as pltpu

F32 = jnp.float32
MM_DTYPE = jnp.bfloat16
WIRE_DTYPE = jnp.bfloat16
EPS = 1e-6
N_HEADS = 8
QK_NOPE = 64
QK_ROPE = 32
QK_DIM = QK_NOPE + QK_ROPE
V_DIM = 64
HEAD_PAD = 128
HEADS_PER_STEP = 2
CHUNK_SHIFT = 6
LOG2E = 1.4426950408889634
CONV_K = 31
HALO = 32
ROPE_THETA = 10000.0
N_CHIPS = 4
ADAM_LR, ADAM_B1, ADAM_B2, ADAM_EPS, ADAM_WD, ADAM_STEP = 0.001, 0.9, 0.999, 1e-08, 0.01, 10
VMEM_LIMIT_BYTES = 56 * 2 ** 20
MESH = pl.DeviceIdType.MESH
ANY = pl.BlockSpec(memory_space=pl.ANY)

_DIMS = {
    "nn": (((1,), (0,)), ((), ())),
    "nt": (((1,), (1,)), ((), ())),
    "tn": (((0,), (0,)), ((), ())),
}


def _params(n_axes):
    return pltpu.CompilerParams(dimension_semantics=("arbitrary",) * n_axes, vmem_limit_bytes=VMEM_LIMIT_BYTES)


def _tile(n, cap):
    t = cap
    while n % t:
        t //= 2
    return t


def _mm(name, a, b, dims, grid, a_spec, b_spec, o_spec, out_shape, nk, acc_shape, into=None):
    n_axes = len(grid)

    def body(*refs):
        if into is not None:
            refs = refs[1:]
        a_ref, b_ref, o_ref = refs[:3]
        part = lax.dot_general(a_ref[...].astype(MM_DTYPE), b_ref[...].astype(MM_DTYPE), _DIMS[dims],
                               preferred_element_type=F32)
        if nk == 1:
            o_ref[...] = part.astype(o_ref.dtype)
        else:
            acc_ref = refs[3]
            k = pl.program_id(n_axes - 1)

            @pl.when(k == 0)
            def _():
                acc_ref[...] = part

            @pl.when(k > 0)
            def _():
                acc_ref[...] += part

            @pl.when(k == nk - 1)
            def _():
                o_ref[...] = acc_ref[...].astype(o_ref.dtype)

    in_specs = [a_spec, b_spec]
    args = [a, b]
    aliases = {}
    if into is not None:
        in_specs = [ANY] + in_specs
        args = [into] + args
        aliases = {0: 0}
    return pl.pallas_call(
        body, name=name, grid=grid, in_specs=in_specs, out_specs=o_spec, out_shape=out_shape,
        scratch_shapes=[] if nk == 1 else [pltpu.VMEM(acc_shape, F32)],
        input_output_aliases=aliases, compiler_params=_params(n_axes))(*args)


def _proj(name, a, w, l, dims, out_dtype=F32):
    s, k = a.shape
    g = w.shape[1]
    n = w.shape[3] if dims == "nn" else w.shape[2]
    tm = _tile(s, 512)
    return _mm(name, a, w, dims, (g, s // tm),
               pl.BlockSpec((tm, k), lambda j, i: (i, 0)),
               pl.BlockSpec((None, None) + w.shape[2:], lambda j, i: (l, j, 0, 0)),
               pl.BlockSpec((None, tm, n), lambda j, i: (j, i, 0)),
               jax.ShapeDtypeStruct((g, s, n), out_dtype), 1, None)


def _contract(name, a, w, l, dims, out_dtype=F32):
    g, s, kb = a.shape
    n = w.shape[3] if dims == "nn" else w.shape[2]
    tm = _tile(s, 512)
    return _mm(name, a, w, dims, (s // tm, g),
               pl.BlockSpec((None, tm, kb), lambda i, j: (j, i, 0)),
               pl.BlockSpec((None, None) + w.shape[2:], lambda i, j: (l, j, 0, 0)),
               pl.BlockSpec((tm, n), lambda i, j: (i, 0)),
               jax.ShapeDtypeStruct((s, n), out_dtype), g, (tm, n))


def _wgrad(name, a, b, out_dtype, layers=None, l=None, into=None):
    g = a.shape[0] if a.ndim == 3 else b.shape[0]
    s, k = a.shape[-2:]
    n = b.shape[-1]
    tm = _tile(s, 512)

    def spec(x, w):
        if x.ndim == 3:
            return pl.BlockSpec((None, tm, w), lambda j, i: (j, i, 0))
        return pl.BlockSpec((tm, w), lambda j, i: (i, 0))

    if layers is None:
        o_spec = pl.BlockSpec((None, k, n), lambda j, i: (j, 0, 0))
        out_shape = jax.ShapeDtypeStruct((g, k, n), out_dtype)
    else:
        o_spec = pl.BlockSpec((None, None, k, n), lambda j, i: (l, j, 0, 0))
        out_shape = jax.ShapeDtypeStruct((layers, g, k, n), out_dtype)
    return _mm(name, a, b, "tn", (g, s // tm), spec(a, k), spec(b, n), o_spec, out_shape, s // tm, (k, n), into=into)


def _ffn_up_fused(n, wg, wu, l):
    s, k = n.shape
    g, n4 = wg.shape[1], wg.shape[3]
    tm = _tile(s, 512)

    def body(n_ref, wg_ref, wu_ref, a_ref, b_ref, h_ref):
        nv = n_ref[...]
        a = jnp.dot(nv, wg_ref[...], preferred_element_type=F32)
        b = jnp.dot(nv, wu_ref[...], preferred_element_type=F32)
        a_ref[...] = a.astype(a_ref.dtype)
        b_ref[...] = b.astype(b_ref.dtype)
        h_ref[...] = (a * _sigmoid(a) * b).astype(h_ref.dtype)

    w_spec = pl.BlockSpec((None, None, k, n4), lambda j, i: (l, j, 0, 0))
    o_spec = pl.BlockSpec((None, tm, n4), lambda j, i: (j, i, 0))
    return pl.pallas_call(
        body, name="ffn_up_fused", grid=(g, s // tm),
        in_specs=[pl.BlockSpec((tm, k), lambda j, i: (i, 0)), w_spec, w_spec], out_specs=[o_spec] * 3,
        out_shape=[jax.ShapeDtypeStruct((g, s, n4), MM_DTYPE)] * 3, compiler_params=_params(2))(n, wg, wu)


def _ffn_down_t_fused(do, wd, a, b, l):
    s, k = do.shape
    g, n4 = wd.shape[1], wd.shape[2]
    tm = _tile(s, 512)

    def body(do_ref, wd_ref, a_ref, b_ref, da_ref, db_ref):
        dh = lax.dot_general(do_ref[...], wd_ref[...], _DIMS["nt"], preferred_element_type=F32)
        av = a_ref[...].astype(F32)
        sg = _sigmoid(av)
        da_ref[...] = (dh * b_ref[...].astype(F32) * (sg * (1.0 + av * (1.0 - sg)))).astype(da_ref.dtype)
        db_ref[...] = (dh * (av * sg)).astype(db_ref.dtype)

    t_spec = pl.BlockSpec((None, tm, n4), lambda j, i: (j, i, 0))
    return pl.pallas_call(
        body, name="ffn_down_t_fused", grid=(g, s // tm),
        in_specs=[pl.BlockSpec((tm, k), lambda j, i: (i, 0)),
                  pl.BlockSpec((None, None, n4, k), lambda j, i: (l, j, 0, 0)), t_spec, t_spec],
        out_specs=[t_spec] * 2, out_shape=[jax.ShapeDtypeStruct((g, s, n4), MM_DTYPE)] * 2,
        compiler_params=_params(2))(do, wd, a, b)


def _blocked_sum(name, acts, weights, l, dims):
    g, s, kb = acts[0].shape
    n = weights[0].shape[3] if dims == "nn" else weights[0].shape[2]
    tm = _tile(s, 256)
    n_pairs = len(acts)

    def body(*refs):
        o_ref = refs[2 * n_pairs]
        acc = None
        for p in range(n_pairs):
            for j in range(g):
                t = lax.dot_general(refs[p][j], refs[n_pairs + p][j], _DIMS[dims], preferred_element_type=F32)
                acc = t if acc is None else acc + t
        o_ref[...] = acc

    return pl.pallas_call(
        body, name=name, grid=(s // tm,),
        in_specs=[pl.BlockSpec((g, tm, kb), lambda i: (0, i, 0))] * n_pairs
        + [pl.BlockSpec((None,) + w.shape[1:], lambda i: (l, 0, 0, 0)) for w in weights],
        out_specs=pl.BlockSpec((tm, n), lambda i: (i, 0)), out_shape=jax.ShapeDtypeStruct((s, n), F32),
        compiler_params=_params(1))(*acts, *weights)


def _wgrad_pair(name, nt, da, db, layers, l, into):
    k, s = nt.shape
    g, _, n4 = da.shape
    tm = _tile(s, 512)
    nk = s // tm

    def body(*refs):
        if into is not None:
            refs = refs[2:]
        nt_ref, da_ref, db_ref, og_ref, ou_ref, accg, accu = refs
        i = pl.program_id(1)
        ntv = nt_ref[...]
        pg = jnp.dot(ntv, da_ref[...], preferred_element_type=F32)
        pu = jnp.dot(ntv, db_ref[...], preferred_element_type=F32)

        @pl.when(i == 0)
        def _():
            accg[...] = pg
            accu[...] = pu

        @pl.when(i > 0)
        def _():
            accg[...] += pg
            accu[...] += pu

        @pl.when(i == nk - 1)
        def _():
            og_ref[...] = accg[...].astype(og_ref.dtype)
            ou_ref[...] = accu[...].astype(ou_ref.dtype)

    t_spec = pl.BlockSpec((None, tm, n4), lambda j, i: (j, i, 0))
    o_spec = pl.BlockSpec((None, None, k, n4), lambda j, i: (l, j, 0, 0))
    in_specs = [pl.BlockSpec((k, tm), lambda j, i: (0, i)), t_spec, t_spec]
    args = [nt, da, db]
    aliases = {}
    if into is not None:
        in_specs = [ANY, ANY] + in_specs
        args = list(into) + args
        aliases = {0: 0, 1: 1}
    return pl.pallas_call(
        body, name=name, grid=(g, nk), in_specs=in_specs, out_specs=[o_spec] * 2,
        out_shape=[jax.ShapeDtypeStruct((layers, g, k, n4), WIRE_DTYPE)] * 2,
        scratch_shapes=[pltpu.VMEM((k, n4), F32)] * 2, input_output_aliases=aliases,
        compiler_params=_params(2))(*args)


def _rows(name, fn, s, tm, ins, outs, scratch=()):
    n = s // tm

    def spec(shape, kind):
        nd = len(shape)
        if kind == "r":
            return pl.BlockSpec(shape[:-2] + (tm, shape[-1]), lambda i: (0,) * (nd - 2) + (i, 0))
        if kind in ("f", "a"):
            return pl.BlockSpec(shape, lambda i: (0,) * nd)
        if kind == "t":
            return pl.BlockSpec((shape[0], tm), lambda i: (0, i))
        if kind == "p":
            return pl.BlockSpec((HALO, shape[-1]), lambda i: (jnp.maximum(i * (tm // HALO) - 1, 0), 0))
        if kind == "n":
            return pl.BlockSpec((HALO, shape[-1]), lambda i: (jnp.minimum((i + 1) * (tm // HALO), s // HALO - 1), 0))
        j = kind[1]
        return pl.BlockSpec((tm, HEAD_PAD), lambda i: (i, j))

    def body(*refs):
        fn(pl.program_id(0), n, *refs)

    return pl.pallas_call(
        body, name=name, grid=(n,),
        in_specs=[spec(a.shape, kind) for a, kind in ins],
        out_specs=[spec(shape, kind) for shape, _, kind in outs],
        out_shape=[jax.ShapeDtypeStruct(shape, dtype) for shape, dtype, _ in outs],
        scratch_shapes=list(scratch), compiler_params=_params(1))(*[a for a, _ in ins])


def _acc(ref, i, val):
    @pl.when(i == 0)
    def _():
        ref[...] = val

    @pl.when(i > 0)
    def _():
        ref[...] += val


def _sum0(x):
    return jnp.sum(x, axis=0, keepdims=True)


def _rstd(x, n):
    return lax.rsqrt(jnp.sum(x * x, axis=-1, keepdims=True) * (1.0 / n) + EPS)


def _rms_bwd(x, g, dy, n):
    r = _rstd(x, n)
    xh = x * r
    dyg = dy * g
    dx = r * (dyg - xh * (jnp.sum(dyg * xh, axis=-1, keepdims=True) * (1.0 / n)))
    return dx, _sum0(dy * xh)


def _sigmoid(x):
    return 1.0 / (1.0 + jnp.exp(-x))


def _norm_fwd(x, g):
    s, d = x.shape

    def fn(i, n, x_ref, g_ref, o_ref, ot_ref):
        xv = x_ref[...]
        y = xv * _rstd(xv, d) * g_ref[...]
        o_ref[...] = y.astype(o_ref.dtype)
        ot_ref[...] = y.T.astype(ot_ref.dtype)

    return _rows("norm_fwd", fn, s, _tile(s, 256), [(x, "r"), (g, "f")],
                 [((s, d), MM_DTYPE, "r"), ((d, s), MM_DTYPE, "t")])


def _resid_norm(x, adds, scale, g, norm_dtype, transposed=False):
    s, d = x.shape
    k = len(adds)

    def fn(i, n, *refs):
        x_ref, add_refs, g_ref, xo_ref, no_ref = refs[0], refs[1:1 + k], refs[1 + k], refs[2 + k], refs[3 + k]
        tot = add_refs[0][...]
        for r in add_refs[1:]:
            tot = tot + r[...]
        xv = x_ref[...] + scale * tot
        xo_ref[...] = xv
        y = xv * _rstd(xv, d) * g_ref[...]
        no_ref[...] = y.astype(no_ref.dtype)
        if transposed:
            refs[4 + k][...] = y.T.astype(refs[4 + k].dtype)

    return _rows("resid_norm", fn, s, _tile(s, 256), [(x, "r")] + [(a, "r") for a in adds] + [(g, "f")],
                 [((s, d), F32, "r"), ((s, d), norm_dtype, "r")] + ([((d, s), MM_DTYPE, "t")] if transposed else []))


def _norm_bwd(x, g, dn_parts, dres, out_scale):
    s, d = x.shape
    k = len(dn_parts)
    has_res = dres is not None

    def fn(i, n, *refs):
        x_ref, g_ref = refs[0], refs[1]
        dn_refs = refs[2:2 + k]
        pos = 2 + k
        dn = dn_refs[0][...]
        for r in dn_refs[1:]:
            dn = dn + r[...]
        dx, dg = _rms_bwd(x_ref[...], g_ref[...], dn, d)
        if has_res:
            dx = dx + refs[pos][...]
            pos += 1
        dx_ref, dxs_ref, dg_ref = refs[pos:pos + 3]
        dx_ref[...] = dx
        dxs_ref[...] = (out_scale * dx).astype(dxs_ref.dtype)
        _acc(dg_ref, i, dg)

    ins = [(x, "r"), (g, "f")] + [(p, "r") for p in dn_parts] + ([(dres, "r")] if has_res else [])
    return _rows("norm_bwd", fn, s, _tile(s, 256), ins,
                 [((s, d), F32, "r"), ((s, d), MM_DTYPE, "r"), ((1, d), F32, "a")])


def _final_fwd(x, o, g):
    return _resid_norm(x, [o], 0.5, g, F32)


def _loss(y, t):
    s, d = y.shape

    def fn(i, n, y_ref, t_ref, dy_ref, l_ref):
        e = y_ref[...] - t_ref[...]
        dy_ref[...] = e * (1.0 / d)
        _acc(l_ref, i, (0.5 / d) * jnp.sum(jnp.sum(e * e, axis=-1, keepdims=True), axis=0, keepdims=True))

    return _rows("loss", fn, s, _tile(s, 256), [(y, "r"), (t, "r")], [((s, d), F32, "r"), ((1, 1), F32, "a")])


def _mla_pre(p, gql, gkvl, cw):
    s = p.shape[0]
    ql, kvl = gql.shape[-1], gkvl.shape[-1]
    o_q, o_kv = 2 * cw, 2 * cw + ql

    def fn(i, n, p_ref, gq_ref, gkv_ref, cq_ref, ckv_ref):
        cq = p_ref[:, o_q:o_q + ql]
        cq_ref[...] = (cq * _rstd(cq, ql) * gq_ref[...]).astype(cq_ref.dtype)
        ckv = p_ref[:, o_kv:o_kv + kvl]
        ckv_ref[...] = (ckv * _rstd(ckv, kvl) * gkv_ref[...]).astype(ckv_ref.dtype)

    return _rows("mla_pre", fn, s, _tile(s, 256), [(p, "r"), (gql, "f"), (gkvl, "f")],
                 [((s, ql), MM_DTYPE, "r"), ((s, kvl), MM_DTYPE, "r")])


def _rope(x, cosf, s1, s2):
    return x * cosf + pltpu.roll(x, HEAD_PAD - 16, 1) * s1 + pltpu.roll(x, 16, 1) * s2


def _rope_bwd(d, cosf, s1, s2):
    return d * cosf + pltpu.roll(d * s1, 16, 1) + pltpu.roll(d * s2, HEAD_PAD - 16, 1)


def _qk_post(q_pre, k_pre, p, pe_block, tabs, gq, gk):
    s, hw = q_pre.shape
    cosf, s1, s2 = tabs

    def fn(i, n, q_ref, k_ref, pe_ref, c_ref, s1_ref, s2_ref, gq_ref, gk_ref, qo_ref, ko_ref):
        c, a1, a2, pe = c_ref[...], s1_ref[...], s2_ref[...], pe_ref[...]
        for hh in range(N_HEADS):
            cols = slice(hh * HEAD_PAD, (hh + 1) * HEAD_PAD)
            qv = q_ref[:, cols]
            qo_ref[:, cols] = _rope(qv * _rstd(qv, QK_DIM) * gq_ref[...], c, a1, a2).astype(qo_ref.dtype)
            kv = k_ref[:, cols] + pe
            ko_ref[:, cols] = _rope(kv * _rstd(kv, QK_DIM) * gk_ref[...], c, a1, a2).astype(ko_ref.dtype)

    return _rows("qk_post", fn, s, _tile(s, 256),
                 [(q_pre, "r"), (k_pre, "r"), (p, ("c", pe_block)), (cosf, "r"), (s1, "r"), (s2, "r"), (gq, "f"), (gk, "f")],
                 [((s, hw), MM_DTYPE, "r"), ((s, hw), MM_DTYPE, "r")])


def _qk_post_bwd(q_pre, k_pre, p, pe_block, tabs, gq, gk, dq, dk):
    s, hw = q_pre.shape
    cosf, s1, s2 = tabs

    def fn(i, n, q_ref, k_ref, pe_ref, c_ref, s1_ref, s2_ref, gq_ref, gk_ref, dq_ref, dk_ref,
           dqo_ref, dko_ref, dpe_ref, dgq_ref, dgk_ref):
        c, a1, a2, pe = c_ref[...], s1_ref[...], s2_ref[...], pe_ref[...]
        lane = lax.broadcasted_iota(jnp.int32, pe.shape, 1)
        is_pe = (lane >= QK_NOPE) & (lane < QK_DIM)
        dpe = jnp.zeros(pe.shape, F32)
        dgq = jnp.zeros((1, HEAD_PAD), F32)
        dgk = jnp.zeros((1, HEAD_PAD), F32)
        for hh in range(N_HEADS):
            cols = slice(hh * HEAD_PAD, (hh + 1) * HEAD_PAD)
            dxq, g1 = _rms_bwd(q_ref[:, cols], gq_ref[...], _rope_bwd(dq_ref[:, cols], c, a1, a2), QK_DIM)
            dqo_ref[:, cols] = dxq.astype(dqo_ref.dtype)
            dxk, g2 = _rms_bwd(k_ref[:, cols] + pe, gk_ref[...], _rope_bwd(dk_ref[:, cols], c, a1, a2), QK_DIM)
            dko_ref[:, cols] = dxk.astype(dko_ref.dtype)
            dpe = dpe + dxk
            dgq = dgq + g1
            dgk = dgk + g2
        dpe_ref[...] = jnp.where(is_pe, dpe, 0.0)
        _acc(dgq_ref, i, dgq)
        _acc(dgk_ref, i, dgk)

    return _rows("qk_post_bwd", fn, s, _tile(s, 256),
                 [(q_pre, "r"), (k_pre, "r"), (p, ("c", pe_block)), (cosf, "r"), (s1, "r"), (s2, "r"), (gq, "f"), (gk, "f"),
                  (dq, "r"), (dk, "r")],
                 [((s, hw), MM_DTYPE, "r"), ((s, hw), MM_DTYPE, "r"), ((s, HEAD_PAD), F32, "r"),
                  ((1, HEAD_PAD), F32, "a"), ((1, HEAD_PAD), F32, "a")])


def _chunk_mask(tq, key_rows):
    r = jnp.right_shift(lax.broadcasted_iota(jnp.int32, (tq, tq), 0), CHUNK_SHIFT)
    c = jnp.right_shift(lax.broadcasted_iota(jnp.int32, (tq, tq), 1), CHUNK_SHIFT)
    return (r <= c) if key_rows else (c <= r)


def _attn_specs(s, tq):
    w = HEADS_PER_STEP * HEAD_PAD
    tile = pl.BlockSpec((tq, w), lambda hb, i: (i, hb))
    whole = pl.BlockSpec((s, w), lambda hb, i: (0, hb))
    stat = pl.BlockSpec((None, tq, HEADS_PER_STEP), lambda hb, i: (hb, i, 0))
    return tile, whole, stat


def _head(hh):
    return slice(hh * HEAD_PAD, (hh + 1) * HEAD_PAD)


def _attn_fwd(q, k, v):
    s, hw = q.shape
    tq = _tile(s, 256)
    groups = N_HEADS // HEADS_PER_STEP
    scale = QK_DIM ** -0.5
    tile, whole, stat = _attn_specs(s, tq)

    def body(q_ref, k_ref, v_ref, o_ref, lse_ref):
        qi = pl.program_id(1)
        qv = [q_ref[:, _head(hh)] for hh in range(HEADS_PER_STEP)]

        def block(kb, carry, diag):
            off = pl.multiple_of(kb * tq, tq)
            out = []
            for hh in range(HEADS_PER_STEP):
                m, l, acc = carry[hh]
                ks = k_ref[pl.ds(off, tq), _head(hh)]
                vs = v_ref[pl.ds(off, tq), _head(hh)]
                sc = lax.dot_general(qv[hh], ks, _DIMS["nt"], preferred_element_type=F32) * (scale * LOG2E)
                if diag:
                    sc = jnp.where(_chunk_mask(tq, False), sc, -1e30)
                mn = jnp.maximum(m, jnp.max(sc, axis=-1, keepdims=True))
                al = jnp.exp2(m - mn)
                pr = jnp.exp2(sc - mn)
                l = al * l + jnp.sum(pr, axis=-1, keepdims=True)
                acc = al * acc + jnp.dot(pr.astype(MM_DTYPE), vs, preferred_element_type=F32)
                out.append((mn, l, acc))
            return tuple(out)

        init = tuple((jnp.full((tq, 1), -1e30, F32), jnp.zeros((tq, 1), F32), jnp.zeros((tq, HEAD_PAD), F32))
                     for _ in range(HEADS_PER_STEP))
        carry = lax.fori_loop(0, qi, lambda kb, c: block(kb, c, False), init)
        carry = block(qi, carry, True)
        for hh in range(HEADS_PER_STEP):
            m, l, acc = carry[hh]
            o_ref[:, _head(hh)] = acc / l
            lse_ref[:, hh:hh + 1] = m + jnp.log(l) * LOG2E

    return pl.pallas_call(
        body, name="attn_fwd", grid=(groups, s // tq),
        in_specs=[tile, whole, whole], out_specs=[tile, stat],
        out_shape=[jax.ShapeDtypeStruct((s, hw), F32), jax.ShapeDtypeStruct((groups, s, HEADS_PER_STEP), F32)],
        compiler_params=_params(2))(q, k, v)


def _attn_dq(q, k, v, o, do, lse):
    s, hw = q.shape
    tq = _tile(s, 256)
    groups = N_HEADS // HEADS_PER_STEP
    scale = QK_DIM ** -0.5
    tile, whole, stat = _attn_specs(s, tq)

    def body(q_ref, k_ref, v_ref, o_ref, do_ref, lse_ref, dq_ref, delta_ref):
        qi = pl.program_id(1)
        qv, dob, lse, delta = [], [], [], []
        for hh in range(HEADS_PER_STEP):
            qv.append(q_ref[:, _head(hh)])
            dov = do_ref[:, _head(hh)]
            dob.append(dov.astype(MM_DTYPE))
            lse.append(lse_ref[:, hh:hh + 1])
            delta.append(jnp.sum(dov * o_ref[:, _head(hh)], axis=-1, keepdims=True))
            delta_ref[:, hh:hh + 1] = delta[hh]

        def block(kb, dq, diag):
            off = pl.multiple_of(kb * tq, tq)
            out = []
            for hh in range(HEADS_PER_STEP):
                ks = k_ref[pl.ds(off, tq), _head(hh)]
                vs = v_ref[pl.ds(off, tq), _head(hh)]
                sc = lax.dot_general(qv[hh], ks, _DIMS["nt"], preferred_element_type=F32) * (scale * LOG2E)
                if diag:
                    sc = jnp.where(_chunk_mask(tq, False), sc, -1e30)
                pr = jnp.exp2(sc - lse[hh])
                dpr = lax.dot_general(dob[hh], vs, _DIMS["nt"], preferred_element_type=F32)
                ds = (pr * (dpr - delta[hh]) * scale).astype(MM_DTYPE)
                out.append(dq[hh] + jnp.dot(ds, ks, preferred_element_type=F32))
            return tuple(out)

        init = tuple(jnp.zeros((tq, HEAD_PAD), F32) for _ in range(HEADS_PER_STEP))
        dq = block(qi, lax.fori_loop(0, qi, lambda kb, c: block(kb, c, False), init), True)
        for hh in range(HEADS_PER_STEP):
            dq_ref[:, _head(hh)] = dq[hh]

    return pl.pallas_call(
        body, name="attn_dq", grid=(groups, s // tq),
        in_specs=[tile, whole, whole, tile, tile, stat], out_specs=[tile, stat],
        out_shape=[jax.ShapeDtypeStruct((s, hw), F32), jax.ShapeDtypeStruct((groups, s, HEADS_PER_STEP), F32)],
        compiler_params=_params(2))(q, k, v, o, do, lse)


def _attn_dkv(q, k, v, do, lse_row, delta_row):
    s, hw = q.shape
    tq = _tile(s, 256)
    nq = s // tq
    groups = N_HEADS // HEADS_PER_STEP
    scale = QK_DIM ** -0.5
    tile, whole, _ = _attn_specs(s, tq)
    rowvec = pl.BlockSpec((None, nq, HEADS_PER_STEP, tq), lambda hb, i: (hb, 0, 0, 0))

    def body(q_ref, k_ref, v_ref, do_ref, lse_ref, delta_ref, dk_ref, dv_ref):
        ki = pl.program_id(1)
        kv = [k_ref[:, _head(hh)] for hh in range(HEADS_PER_STEP)]
        vv = [v_ref[:, _head(hh)] for hh in range(HEADS_PER_STEP)]

        def block(qb, carry, diag):
            off = pl.multiple_of(qb * tq, tq)
            out = []
            for hh in range(HEADS_PER_STEP):
                dk, dvv = carry[hh]
                qs = q_ref[pl.ds(off, tq), _head(hh)]
                dos = do_ref[pl.ds(off, tq), _head(hh)].astype(MM_DTYPE)
                sc = lax.dot_general(kv[hh], qs, _DIMS["nt"], preferred_element_type=F32) * (scale * LOG2E)
                if diag:
                    sc = jnp.where(_chunk_mask(tq, True), sc, -1e30)
                pr = jnp.exp2(sc - lse_ref[qb, hh:hh + 1, :])
                dpr = lax.dot_general(vv[hh], dos, _DIMS["nt"], preferred_element_type=F32)
                ds = (pr * (dpr - delta_ref[qb, hh:hh + 1, :]) * scale).astype(MM_DTYPE)
                dvv = dvv + jnp.dot(pr.astype(MM_DTYPE), dos, preferred_element_type=F32)
                dk = dk + jnp.dot(ds, qs, preferred_element_type=F32)
                out.append((dk, dvv))
            return tuple(out)

        init = tuple((jnp.zeros((tq, HEAD_PAD), F32), jnp.zeros((tq, HEAD_PAD), F32)) for _ in range(HEADS_PER_STEP))
        carry = lax.fori_loop(ki + 1, nq, lambda qb, c: block(qb, c, False), block(ki, init, True))
        for hh in range(HEADS_PER_STEP):
            dk_ref[:, _head(hh)] = carry[hh][0]
            dv_ref[:, _head(hh)] = carry[hh][1].astype(dv_ref.dtype)

    return pl.pallas_call(
        body, name="attn_dkv", grid=(groups, nq),
        in_specs=[whole, tile, tile, whole, rowvec, rowvec], out_specs=[tile, tile],
        out_shape=[jax.ShapeDtypeStruct((s, hw), F32), jax.ShapeDtypeStruct((s, hw), MM_DTYPE)],
        compiler_params=_params(2))(q, k, v, do, lse_row, delta_row)


def _glu_with_halo(i, p_ref, ph_ref, ubuf, cw, tm):
    a, g = p_ref[:, 0:cw], p_ref[:, cw:2 * cw]
    ubuf[HALO:HALO + tm, :] = a * _sigmoid(g)
    ah, gh = ph_ref[:, 0:cw], ph_ref[:, cw:2 * cw]
    ubuf[0:HALO, :] = jnp.where(i > 0, ah * _sigmoid(gh), 0.0)
    return a, g


def _conv_taps(ubuf, w_ref, tm):
    base = HALO - (CONV_K - 1)
    y = w_ref[0:1, :] * ubuf[base:base + tm, :]
    for k in range(1, CONV_K):
        y = y + w_ref[k:k + 1, :] * ubuf[base + k:base + k + tm, :]
    return y


def _layer_norm_stats(y, cw):
    mu = jnp.sum(y, axis=-1, keepdims=True) * (1.0 / cw)
    yc = y - mu
    r = lax.rsqrt(jnp.sum(yc * yc, axis=-1, keepdims=True) * (1.0 / cw) + EPS)
    return yc * r, r


def _conv_fwd(p, w, cb, lg, lb):
    s = p.shape[0]
    cw = cb.shape[-1]
    tm = _tile(s, 256)

    def fn(i, n, p_ref, ph_ref, w_ref, cb_ref, lg_ref, lb_ref, o_ref, ubuf):
        _glu_with_halo(i, p_ref, ph_ref, ubuf, cw, tm)
        yh, _ = _layer_norm_stats(_conv_taps(ubuf, w_ref, tm) + cb_ref[...], cw)
        z = yh * lg_ref[...] + lb_ref[...]
        o_ref[...] = (z * _sigmoid(z)).astype(o_ref.dtype)

    return _rows("conv_fwd", fn, s, tm, [(p, "r"), (p, "p"), (w, "f"), (cb, "f"), (lg, "f"), (lb, "f")],
                 [((s, cw), MM_DTYPE, "r")], scratch=[pltpu.VMEM((tm + HALO, cw), F32)])[0]


def _conv_bwd_ln(p, w, cb, lg, lb, dc):
    s = p.shape[0]
    cw = cb.shape[-1]
    tm = _tile(s, 256)

    def fn(i, n, p_ref, ph_ref, w_ref, cb_ref, lg_ref, lb_ref, dc_ref, dy_ref, dlg_ref, dlb_ref, dcb_ref, ubuf):
        _glu_with_halo(i, p_ref, ph_ref, ubuf, cw, tm)
        yh, r = _layer_norm_stats(_conv_taps(ubuf, w_ref, tm) + cb_ref[...], cw)
        z = yh * lg_ref[...] + lb_ref[...]
        sg = _sigmoid(z)
        dz = dc_ref[...] * (sg * (1.0 + z * (1.0 - sg)))
        dyh = dz * lg_ref[...]
        m1 = jnp.sum(dyh, axis=-1, keepdims=True) * (1.0 / cw)
        m2 = jnp.sum(dyh * yh, axis=-1, keepdims=True) * (1.0 / cw)
        dy = r * (dyh - m1 - yh * m2)
        dy_ref[...] = dy
        _acc(dlg_ref, i, _sum0(dz * yh))
        _acc(dlb_ref, i, _sum0(dz))
        _acc(dcb_ref, i, _sum0(dy))

    return _rows("conv_bwd_ln", fn, s, tm,
                 [(p, "r"), (p, "p"), (w, "f"), (cb, "f"), (lg, "f"), (lb, "f"), (dc, "r")],
                 [((s, cw), F32, "r"), ((1, cw), F32, "a"), ((1, cw), F32, "a"), ((1, cw), F32, "a")],
                 scratch=[pltpu.VMEM((tm + HALO, cw), F32)])


def _conv_bwd_taps(p, w, dy):
    s = p.shape[0]
    cw = w.shape[-1]
    tm = _tile(s, 256)

    def fn(i, n, p_ref, ph_ref, w_ref, dy_ref, dyn_ref, dp_ref, dw_ref, ubuf, dybuf):
        a, g = _glu_with_halo(i, p_ref, ph_ref, ubuf, cw, tm)
        dyv = dy_ref[...]
        dybuf[0:tm, :] = dyv
        dybuf[tm:tm + HALO, :] = jnp.where(i < n - 1, dyn_ref[...], 0.0)
        base = HALO - (CONV_K - 1)
        du = jnp.zeros((tm, cw), F32)

        @pl.when(i == 0)
        def _():
            dw_ref[...] = jnp.zeros(dw_ref.shape, F32)

        for k in range(CONV_K):
            sh = CONV_K - 1 - k
            du = du + w_ref[k:k + 1, :] * dybuf[sh:sh + tm, :]
            dw_ref[k:k + 1, :] += _sum0(dyv * ubuf[base + k:base + k + tm, :])
        sg = _sigmoid(g)
        dp_ref[:, 0:cw] = (du * sg).astype(dp_ref.dtype)
        dp_ref[:, cw:2 * cw] = (du * a * sg * (1.0 - sg)).astype(dp_ref.dtype)

    return _rows("conv_bwd_taps", fn, s, tm, [(p, "r"), (p, "p"), (w, "f"), (dy, "r"), (dy, "n")],
                 [((s, 2 * cw), MM_DTYPE, "r"), ((CONV_K + 1, cw), F32, "a")],
                 scratch=[pltpu.VMEM((tm + HALO, cw), F32), pltpu.VMEM((tm + HALO, cw), F32)])


def _dp_assemble(p, gql, gkvl, dcq, dckv_parts, dp_conv, dpe, cw):
    s, width = p.shape
    ql, kvl = gql.shape[-1], gkvl.shape[-1]
    o_q, o_kv, o_pe = 2 * cw, 2 * cw + ql, 2 * cw + ql + kvl

    def fn(i, n, p_ref, gq_ref, gkv_ref, dcq_ref, dk1_ref, dk2_ref, dconv_ref, dpe_ref, dp_ref, dgq_ref, dgkv_ref):
        dp_ref[:, 0:o_q] = dconv_ref[...]
        dx, dg = _rms_bwd(p_ref[:, o_q:o_kv], gq_ref[...], dcq_ref[...], ql)
        dp_ref[:, o_q:o_kv] = dx.astype(dp_ref.dtype)
        _acc(dgq_ref, i, dg)
        dx, dg = _rms_bwd(p_ref[:, o_kv:o_pe], gkv_ref[...], dk1_ref[...] + dk2_ref[...], kvl)
        dp_ref[:, o_kv:o_pe] = dx.astype(dp_ref.dtype)
        _acc(dgkv_ref, i, dg)
        dp_ref[:, o_pe:width] = dpe_ref[...].astype(dp_ref.dtype)

    return _rows("dp_assemble", fn, s, _tile(s, 256),
                 [(p, "r"), (gql, "f"), (gkvl, "f"), (dcq, "r"), (dckv_parts[0], "r"), (dckv_parts[1], "r"),
                  (dp_conv, "r"), (dpe, "r")],
                 [((s, width), MM_DTYPE, "r"), ((1, ql), F32, "a"), ((1, kvl), F32, "a")])


def _add_leading(name, x, out_dtype):
    n, r, c = x.shape
    tm = _tile(r, 256)

    def fn(i, nt, x_ref, o_ref):
        tot = x_ref[0].astype(F32)
        for k in range(1, n):
            tot = tot + x_ref[k].astype(F32)
        o_ref[...] = tot.astype(o_ref.dtype)

    return _rows(name, fn, r, tm, [(x, "r")], [((r, c), out_dtype, "r")])[0]


def _prefetch_call(name, fn, idx, grid, in_specs, out_spec, out_shape, args, aliases=None):
    return pl.pallas_call(
        fn, name=name, out_shape=out_shape,
        grid_spec=pltpu.PrefetchScalarGridSpec(num_scalar_prefetch=1, grid=grid, in_specs=in_specs, out_specs=out_spec),
        input_output_aliases=aliases or {}, compiler_params=_params(len(grid)))(idx, *args)


def _add_sibling(idx, mine, theirs):
    _, r, c = mine.shape
    tm = _tile(r, 256)

    def fn(idx_ref, a_ref, b_ref, o_ref):
        o_ref[...] = (a_ref[...].astype(F32) + b_ref[...].astype(F32)).astype(o_ref.dtype)

    return _prefetch_call(
        "add_sibling", fn, idx, (r // tm,),
        [pl.BlockSpec((None, tm, c), lambda i, ix: (ix[0], i, 0)), pl.BlockSpec((tm, c), lambda i, ix: (i, 0))],
        pl.BlockSpec((tm, c), lambda i, ix: (i, 0)), jax.ShapeDtypeStruct((r, c), WIRE_DTYPE), [mine, theirs])


def _add_chips(idx, own, others):
    _, r, c = own.shape
    tm = _tile(r, 256)

    def fn(idx_ref, a_ref, b_ref, o_ref):
        tot = a_ref[...].astype(F32)
        for k in range(3):
            tot = tot + b_ref[k].astype(F32)
        o_ref[...] = tot

    return _prefetch_call(
        "add_chips", fn, idx, (r // tm,),
        [pl.BlockSpec((None, tm, c), lambda i, ix: (ix[1], i, 0)), pl.BlockSpec((3, tm, c), lambda i, ix: (0, i, 0))],
        pl.BlockSpec((None, tm, c), lambda i, ix: (ix[0], i, 0)), jax.ShapeDtypeStruct((2, r, c), F32), [own, others])


def _adamw(w, g, m, v):
    r, c = w.shape
    tm = _tile(r, 256)

    def fn(i, n, w_ref, g_ref, m_ref, v_ref, d_ref, mo_ref, vo_ref):
        gv = g_ref[...]
        mn = ADAM_B1 * m_ref[...] + (1.0 - ADAM_B1) * gv
        vn = ADAM_B2 * v_ref[...] + (1.0 - ADAM_B2) * (gv * gv)
        mo_ref[...] = mn
        vo_ref[...] = vn
        m_hat = mn / (1.0 - ADAM_B1 ** ADAM_STEP)
        v_hat = vn / (1.0 - ADAM_B2 ** ADAM_STEP)
        d_ref[...] = -ADAM_LR * (m_hat / (jnp.sqrt(v_hat) + ADAM_EPS) + ADAM_WD * w_ref[...])

    return _rows("adamw", fn, r, tm, [(w, "r"), (g, "r"), (m, "r"), (v, "r")], [((r, c), F32, "r")] * 3)


def _place():
    x, y, c = lax.axis_index("x"), lax.axis_index("y"), lax.axis_index("c")
    other_chips = [(1 - x, y), (x, 1 - y), (1 - x, 1 - y)]
    return x, y, c, 2 * x + y, other_chips


def _gather_weights(bufs, shards):
    k_n = len(shards)

    def body(*refs):
        src, dst = refs[k_n:2 * k_n], refs[2 * k_n:3 * k_n]
        send, recv, fsend, frecv = refs[3 * k_n:]
        x, y, c, me, chips = _place()

        def direct(k, j):
            return pltpu.make_async_remote_copy(
                src_ref=src[k].at[c], dst_ref=dst[k].at[c, me], send_sem=send.at[k, j], recv_sem=recv.at[k, j],
                device_id=(chips[j][0], chips[j][1], c), device_id_type=MESH)

        def landed(k, j, layer, sems):
            cx, cy = chips[j]
            blk = dst[k].at[layer, 2 * cx + cy]
            return pltpu.make_async_remote_copy(
                src_ref=blk, dst_ref=blk, send_sem=sems[0].at[k, j], recv_sem=sems[1].at[k, j],
                device_id=(x, y, 1 - c), device_id_type=MESH)

        sent = [direct(k, j) for j in range(3) for k in range(k_n)]
        for cp in sent:
            cp.start()
        passed = []
        for j in range(3):
            for k in range(k_n):
                landed(k, j, c, (send, recv)).wait_recv()
                fwd = landed(k, j, c, (fsend, frecv))
                fwd.start()
                passed.append(fwd)
        for j in range(3):
            for k in range(k_n):
                landed(k, j, 1 - c, (fsend, frecv)).wait_recv()
        for cp in sent + passed:
            cp.wait_send()

    dma = lambda: pltpu.SemaphoreType.DMA((k_n, 3))
    return pl.pallas_call(
        body, name="gather_weights", in_specs=[ANY] * (2 * k_n), out_specs=[ANY] * k_n,
        out_shape=[jax.ShapeDtypeStruct(b.shape, b.dtype) for b in bufs],
        input_output_aliases={k: k for k in range(k_n)},
        scratch_shapes=[dma(), dma(), dma(), dma()],
        compiler_params=pltpu.CompilerParams(has_side_effects=True))(*bufs, *shards)


def _swap_layers(grads):
    k_n = len(grads)

    def body(*refs):
        src, dst = refs[:k_n], refs[k_n:2 * k_n]
        send, recv = refs[2 * k_n:]
        x, y, c, me, chips = _place()
        away = [pltpu.make_async_remote_copy(
            src_ref=src[k].at[1 - c], dst_ref=dst[k], send_sem=send.at[k], recv_sem=recv.at[k],
            device_id=(x, y, 1 - c), device_id_type=MESH) for k in range(k_n)]
        for cp in away:
            cp.start()
        for cp in away:
            cp.wait()

    dma = lambda: pltpu.SemaphoreType.DMA((k_n,))
    return pl.pallas_call(
        body, name="swap_layers", in_specs=[ANY] * k_n, out_specs=[ANY] * k_n,
        out_shape=[jax.ShapeDtypeStruct(g.shape[1:], g.dtype) for g in grads],
        scratch_shapes=[dma(), dma()],
        compiler_params=pltpu.CompilerParams(has_side_effects=True))(*grads)


def _scatter_to_owners(parts):
    k_n = len(parts)

    def body(*refs):
        src, dst = refs[:k_n], refs[k_n:2 * k_n]
        send, recv = refs[2 * k_n:]
        x, y, c, me, chips = _place()
        away = [pltpu.make_async_remote_copy(
            src_ref=src[k].at[2 * chips[j][0] + chips[j][1]], dst_ref=dst[k].at[j],
            send_sem=send.at[k, j], recv_sem=recv.at[k, j],
            device_id=(chips[j][0], chips[j][1], c), device_id_type=MESH) for j in range(3) for k in range(k_n)]
        for cp in away:
            cp.start()
        for cp in away:
            cp.wait()

    dma = lambda: pltpu.SemaphoreType.DMA((k_n, 3))
    return pl.pallas_call(
        body, name="scatter_to_owners", in_specs=[ANY] * k_n, out_specs=[ANY] * k_n,
        out_shape=[jax.ShapeDtypeStruct((3,) + p.shape[1:], p.dtype) for p in parts],
        scratch_shapes=[dma(), dma()],
        compiler_params=pltpu.CompilerParams(has_side_effects=True))(*parts)


def _share_layers(sums):
    k_n = len(sums)

    def body(*refs):
        dst = refs[k_n:2 * k_n]
        send, recv = refs[2 * k_n:]
        x, y, c, me, chips = _place()
        away = [pltpu.make_async_remote_copy(
            src_ref=dst[k].at[c], dst_ref=dst[k].at[c], send_sem=send.at[k], recv_sem=recv.at[k],
            device_id=(x, y, 1 - c), device_id_type=MESH) for k in range(k_n)]
        for cp in away:
            cp.start()
        for k in range(k_n):
            blk = dst[k].at[1 - c]
            pltpu.make_async_remote_copy(
                src_ref=blk, dst_ref=blk, send_sem=send.at[k], recv_sem=recv.at[k],
                device_id=(x, y, 1 - c), device_id_type=MESH).wait_recv()
        for cp in away:
            cp.wait_send()

    dma = lambda: pltpu.SemaphoreType.DMA((k_n,))
    return pl.pallas_call(
        body, name="share_layers", in_specs=[ANY] * k_n, out_specs=[ANY] * k_n,
        out_shape=[jax.ShapeDtypeStruct(s.shape, s.dtype) for s in sums],
        input_output_aliases={k: k for k in range(k_n)},
        scratch_shapes=[dma(), dma()],
        compiler_params=pltpu.CompilerParams(has_side_effects=True))(*sums)


def _gather_small(vec):
    flips = [(fx, fy, fc) for fx in (0, 1) for fy in (0, 1) for fc in (0, 1)][1:]

    def body(src, dst, send, recv, lsem):
        x, y, c, _, _ = _place()
        me = 4 * x + 2 * y + c
        local = pltpu.make_async_copy(src, dst.at[me], lsem)
        local.start()
        away = []
        for j, (fx, fy, fc) in enumerate(flips):
            px, py, pc = x ^ fx, y ^ fy, c ^ fc
            away.append(pltpu.make_async_remote_copy(
                src_ref=src, dst_ref=dst.at[me], send_sem=send.at[j], recv_sem=recv.at[j],
                device_id=(px, py, pc), device_id_type=MESH))
        for cp in away:
            cp.start()
        for j, (fx, fy, fc) in enumerate(flips):
            blk = dst.at[4 * (x ^ fx) + 2 * (y ^ fy) + (c ^ fc)]
            pltpu.make_async_remote_copy(
                src_ref=blk, dst_ref=blk, send_sem=send.at[j], recv_sem=recv.at[j],
                device_id=(x ^ fx, y ^ fy, c ^ fc), device_id_type=MESH).wait_recv()
        for cp in away:
            cp.wait_send()
        local.wait()

    return pl.pallas_call(
        body, name="gather_small", in_specs=[ANY], out_specs=ANY,
        out_shape=jax.ShapeDtypeStruct((8,) + vec.shape, vec.dtype),
        scratch_shapes=[pltpu.SemaphoreType.DMA((7,)), pltpu.SemaphoreType.DMA((7,)), pltpu.SemaphoreType.DMA(())],
        compiler_params=pltpu.CompilerParams(has_side_effects=True))(vec)


SHARDED = ["ffn1_w_gate", "ffn1_w_up", "ffn1_w_down", "w_in", "w_uq", "w_ukv", "conv_w", "w_out",
           "ffn2_w_gate", "ffn2_w_up", "ffn2_w_down"]
SMALL = ["ffn1_norm", "mix_norm", "q_latent_norm", "kv_latent_norm", "q_norm", "k_norm", "conv_b", "conv_ln_g",
         "conv_ln_b", "ffn2_norm", "post_norm"]
ALL_WEIGHTS = ["ffn1_norm", "ffn1_w_gate", "ffn1_w_up", "ffn1_w_down", "mix_norm", "w_in", "q_latent_norm", "w_uq",
               "kv_latent_norm", "w_ukv", "q_norm", "k_norm", "conv_w", "conv_b", "conv_ln_g", "conv_ln_b", "w_out",
               "ffn2_norm", "ffn2_w_gate", "ffn2_w_up", "ffn2_w_down", "post_norm"]


def _rope_tables(s):
    pos = jnp.arange(s, dtype=F32)
    inv_freq = 1.0 / (ROPE_THETA ** (jnp.arange(0, QK_ROPE, 2, dtype=F32) / QK_ROPE))
    ang = pos[:, None] * inv_freq[None, :]
    cos, sin = jnp.cos(ang), jnp.sin(ang)
    half = QK_ROPE // 2
    one = jnp.ones((s, QK_NOPE), F32)
    zero = jnp.zeros((s, QK_NOPE), F32)
    pad = HEAD_PAD - QK_DIM
    z16 = jnp.zeros((s, half), F32)
    cosf = jnp.concatenate([one, cos, cos, jnp.ones((s, pad), F32)], axis=1)
    s1 = jnp.concatenate([zero, -sin, z16, jnp.zeros((s, pad), F32)], axis=1)
    s2 = jnp.concatenate([zero, z16, sin, jnp.zeros((s, pad), F32)], axis=1)
    return cosf, s1, s2


def _ffn_fwd(n, w, names, l):
    a, b, hm = _ffn_up_fused(n, w[names[0]], w[names[1]], l)
    return a, b, hm, _blocked_sum("ffn_down", [hm], [w[names[2]]], l, "nn")


def _ffn_bwd(nt, a, b, hm, do, w, names, l, grads):
    gate, up, down = names
    layers = w[gate].shape[0]
    da, db = _ffn_down_t_fused(do, w[down], a, b, l)
    grads[down] = _wgrad("ffn_down_wgrad", hm, do, WIRE_DTYPE, layers, l, grads.get(down))
    grads[gate], grads[up] = _wgrad_pair("ffn_up_wgrad", nt, da, db, layers, l,
                                         (grads[gate], grads[up]) if gate in grads else None)
    return [_blocked_sum("ffn_up_t", [da, db], [w[gate], w[up]], l, "nt")]


def _vec(v, l):
    return v[l][None, :]


def _pad_lanes(v, width):
    return jnp.pad(v, [(0, 0)] * (v.ndim - 1) + [(0, width - v.shape[-1])])


def kernel(x, ffn1_norm, ffn1_w_gate, ffn1_w_up, ffn1_w_down, mix_norm, w_in, q_latent_norm, w_uq, kv_latent_norm, w_ukv, q_norm, k_norm, conv_w, conv_b, conv_ln_g, conv_ln_b, w_out, ffn2_norm, ffn2_w_gate, ffn2_w_up, ffn2_w_down, post_norm, loss_target, m_ffn1_norm, m_ffn1_w_gate, m_ffn1_w_up, m_ffn1_w_down, m_mix_norm, m_w_in, m_q_latent_norm, m_w_uq, m_kv_latent_norm, m_w_ukv, m_q_norm, m_k_norm, m_conv_w, m_conv_b, m_conv_ln_g, m_conv_ln_b, m_w_out, m_ffn2_norm, m_ffn2_w_gate, m_ffn2_w_up, m_ffn2_w_down, m_post_norm, v_ffn1_norm, v_ffn1_w_gate, v_ffn1_w_up, v_ffn1_w_down, v_mix_norm, v_w_in, v_q_latent_norm, v_w_uq, v_kv_latent_norm, v_w_ukv, v_q_norm, v_k_norm, v_conv_w, v_conv_b, v_conv_ln_g, v_conv_ln_b, v_w_out, v_ffn2_norm, v_ffn2_w_gate, v_ffn2_w_up, v_ffn2_w_down, v_post_norm):
    given = dict(locals())
    wts = {n: given[n] for n in ALL_WEIGHTS}
    mom = {n: given["m_" + n] for n in ALL_WEIGHTS}
    var = {n: given["v_" + n] for n in ALL_WEIGHTS}
    n_layers = ffn1_norm.shape[0]
    s, d = x.shape[1], x.shape[2]
    ql, kvl, cw = q_latent_norm.shape[-1], kv_latent_norm.shape[-1], conv_b.shape[-1]
    heads_per_chip = N_HEADS // N_CHIPS
    xs = x.reshape(s, d)
    target = loss_target.reshape(s, d)

    assert n_layers == 2, "core c of a chip finishes the gradients of layer c"
    local = [jnp.pad(wts[n], ((0, 0), (0, 1), (0, 0))).astype(MM_DTYPE) if n == "conv_w" else wts[n].astype(MM_DTYPE)
             for n in SHARDED]
    L = n_layers
    my_c = lax.axis_index("c")
    my_chip = 2 * lax.axis_index("x") + lax.axis_index("y")
    place = jnp.stack([my_c, my_chip]).astype(jnp.int32)
    bufs = [lax.dynamic_update_slice(lax.empty((L, N_CHIPS) + t.shape[1:], t.dtype), t[:, None], (0, my_chip, 0, 0))
            for t in local]
    w = dict(zip(SHARDED, _gather_weights(bufs, local)))
    hw = N_HEADS * HEAD_PAD
    win = w["w_in"].transpose(0, 2, 1, 3).reshape(L, d, -1)
    zc = lambda n: jnp.zeros((L, d, n), MM_DTYPE)
    o_pe, o_cv = ql + kvl, ql + kvl + QK_ROPE
    w["w_in"] = jnp.concatenate([win[..., o_cv:], win[..., :o_pe], zc(QK_NOPE), win[..., o_pe:o_cv],
                                 zc(HEAD_PAD - QK_DIM)], axis=-1)[:, None]
    wq = w["w_uq"].transpose(0, 2, 1, 3).reshape(L, ql, N_HEADS, QK_DIM)
    w["wq"] = _pad_lanes(wq, HEAD_PAD).reshape(L, 1, ql, hw)
    wkv = w["w_ukv"].transpose(0, 2, 1, 3).reshape(L, kvl, N_HEADS, QK_NOPE + V_DIM)
    w["wk"] = _pad_lanes(wkv[..., :QK_NOPE], HEAD_PAD).reshape(L, 1, kvl, hw)
    w["wv"] = _pad_lanes(wkv[..., QK_NOPE:], HEAD_PAD).reshape(L, 1, kvl, hw)
    conv_full = w["conv_w"].transpose(0, 2, 1, 3).reshape(L, CONV_K + 1, cw).astype(F32)
    wout = w["w_out"].reshape(L, d, d)
    wo_a = wout[:, :N_HEADS * V_DIM].reshape(L, N_HEADS, V_DIM, d)
    w["wo_a"] = jnp.pad(wo_a, ((0, 0), (0, 0), (0, HEAD_PAD - V_DIM), (0, 0))).reshape(L, 1, hw, d)
    w["wo_c"] = wout[:, N_HEADS * V_DIM:][:, None]
    tabs = _rope_tables(s)
    pe_block = (2 * cw + ql + kvl) // HEAD_PAD
    gq_pad, gk_pad = _pad_lanes(q_norm, HEAD_PAD), _pad_lanes(k_norm, HEAD_PAD)
    tq = _tile(s, 256)
    groups = N_HEADS // HEADS_PER_STEP

    saved = []
    cur = xs
    for l in range(L):
        sv = {"x": cur}
        n1, sv["n1t"] = _norm_fwd(cur, _vec(ffn1_norm, l))
        sv["a1"], sv["b1"], sv["hm1"], o1 = _ffn_fwd(n1, w, SHARDED[0:3], l)
        sv["x1"], sv["h"] = _resid_norm(cur, [o1], 0.5, _vec(mix_norm, l), MM_DTYPE)
        sv["p"] = _proj("w_in", sv["h"], w["w_in"], l, "nn")[0]
        sv["cq"], sv["ckv"] = _mla_pre(sv["p"], _vec(q_latent_norm, l), _vec(kv_latent_norm, l), cw)
        sv["q_pre"] = _proj("wq", sv["cq"], w["wq"], l, "nn")[0]
        sv["k_pre"] = _proj("wk", sv["ckv"], w["wk"], l, "nn")[0]
        sv["v"] = _proj("wv", sv["ckv"], w["wv"], l, "nn", MM_DTYPE)[0]
        sv["q"], sv["k"] = _qk_post(sv["q_pre"], sv["k_pre"], sv["p"], pe_block, tabs, _vec(gq_pad, l), _vec(gk_pad, l))
        sv["o"], sv["lse"] = _attn_fwd(sv["q"], sv["k"], sv["v"])
        sv["c"] = _conv_fwd(sv["p"], conv_full[l], _vec(conv_b, l), _vec(conv_ln_g, l), _vec(conv_ln_b, l))
        mix_a = _contract("wo_a", sv["o"][None], w["wo_a"], l, "nn")
        mix_c = _contract("wo_c", sv["c"][None], w["wo_c"], l, "nn")
        sv["x2"], n2, sv["n2t"] = _resid_norm(sv["x1"], [mix_a, mix_c], 1.0, _vec(ffn2_norm, l), MM_DTYPE, True)
        sv["a2"], sv["b2"], sv["hm2"], o2 = _ffn_fwd(n2, w, SHARDED[8:11], l)
        sv["x3"], cur = _final_fwd(sv["x2"], o2, _vec(post_norm, l))
        saved.append(sv)
    dy, loss_part = _loss(cur, target)

    grads = {}
    small = {n: [None] * L for n in SMALL}
    stacked = {}
    for l in reversed(range(L)):
        sv = saved[l]
        dx3, do2, small["post_norm"][l] = _norm_bwd(sv["x3"], _vec(post_norm, l), [dy], None, 0.5)
        dn2 = _ffn_bwd(sv["n2t"], sv["a2"], sv["b2"], sv["hm2"], do2, w, SHARDED[8:11], l, grads)
        dx2, dx2b, small["ffn2_norm"][l] = _norm_bwd(sv["x2"], _vec(ffn2_norm, l), dn2, dx3, 1.0)
        do = _proj("wo_a_t", dx2b, w["wo_a"], l, "nt")[0]
        dc = _proj("wo_c_t", dx2b, w["wo_c"], l, "nt")[0]
        g_wo_a = _wgrad("wo_a_wgrad", sv["o"], dx2b[None], WIRE_DTYPE)[0].reshape(N_HEADS, HEAD_PAD, d)[:, :V_DIM]
        g_wo_c = _wgrad("wo_c_wgrad", sv["c"], dx2b[None], WIRE_DTYPE)
        stacked.setdefault("w_out", [None] * L)[l] = jnp.concatenate(
            [g_wo_a.reshape(N_HEADS * V_DIM, d), g_wo_c[0]], axis=0).reshape(N_CHIPS, d // N_CHIPS, d)
        dq, delta = _attn_dq(sv["q"], sv["k"], sv["v"], sv["o"], do, sv["lse"])
        row = lambda t: t.reshape(groups, s // tq, tq, HEADS_PER_STEP).transpose(0, 1, 3, 2)
        dk, dv = _attn_dkv(sv["q"], sv["k"], sv["v"], do, row(sv["lse"]), row(delta))
        dq_pre, dk_pre, dpe, small["q_norm"][l], small["k_norm"][l] = _qk_post_bwd(
            sv["q_pre"], sv["k_pre"], sv["p"], pe_block, tabs, _vec(gq_pad, l), _vec(gk_pad, l), dq, dk)
        dcq = _contract("wq_t", dq_pre[None], w["wq"], l, "nt")
        dckv = [_contract("wk_t", dk_pre[None], w["wk"], l, "nt"), _contract("wv_t", dv[None], w["wv"], l, "nt")]
        per_head = lambda t, rows: t[0].reshape(rows, N_HEADS, HEAD_PAD)
        g_wq = per_head(_wgrad("wq_wgrad", sv["cq"], dq_pre[None], WIRE_DTYPE), ql)[..., :QK_DIM]
        stacked.setdefault("w_uq", [None] * L)[l] = g_wq.reshape(ql, N_CHIPS, -1).transpose(1, 0, 2)
        g_wkv = jnp.concatenate([per_head(_wgrad("wk_wgrad", sv["ckv"], dk_pre[None], WIRE_DTYPE), kvl)[..., :QK_NOPE],
                                 per_head(_wgrad("wv_wgrad", sv["ckv"], dv[None], WIRE_DTYPE), kvl)[..., :V_DIM]], axis=-1)
        stacked.setdefault("w_ukv", [None] * L)[l] = g_wkv.reshape(kvl, N_CHIPS, -1).transpose(1, 0, 2)
        dyc, small["conv_ln_g"][l], small["conv_ln_b"][l], small["conv_b"][l] = _conv_bwd_ln(
            sv["p"], conv_full[l], _vec(conv_b, l), _vec(conv_ln_g, l), _vec(conv_ln_b, l), dc)
        dp_conv, g_conv = _conv_bwd_taps(sv["p"], conv_full[l], dyc)
        stacked.setdefault("conv_w", [None] * L)[l] = g_conv.reshape(CONV_K + 1, N_CHIPS, cw // N_CHIPS).transpose(
            1, 0, 2).astype(WIRE_DTYPE)
        dp, small["q_latent_norm"][l], small["kv_latent_norm"][l] = _dp_assemble(
            sv["p"], _vec(q_latent_norm, l), _vec(kv_latent_norm, l), dcq, dckv, dp_conv, dpe, cw)
        dh = _contract("w_in_t", dp[None], w["w_in"], l, "nt")
        g_win = _wgrad("w_in_wgrad", sv["h"], dp[None], WIRE_DTYPE)[0]
        o_q, o_pe_col = 2 * cw, 2 * cw + ql + kvl + QK_NOPE
        g_win = jnp.concatenate([g_win[:, o_q:o_q + ql + kvl], g_win[:, o_pe_col:o_pe_col + QK_ROPE], g_win[:, :o_q]], axis=1)
        stacked.setdefault("w_in", [None] * L)[l] = g_win.reshape(d, N_CHIPS, -1).transpose(1, 0, 2)
        dx1, do1, small["mix_norm"][l] = _norm_bwd(sv["x1"], _vec(mix_norm, l), [dh], dx2, 0.5)
        dn1 = _ffn_bwd(sv["n1t"], sv["a1"], sv["b1"], sv["hm1"], do1, w, SHARDED[0:3], l, grads)
        dy, _, small["ffn1_norm"][l] = _norm_bwd(sv["x"], _vec(ffn1_norm, l), dn1, dx1, 1.0)
    grad_x = dy.reshape(x.shape)
    for n, per_layer in stacked.items():
        grads[n] = jnp.stack(per_layer, axis=0)

    part_sums = [grads[n] for n in SHARDED]
    swapped = _swap_layers(part_sums)
    chip_sums = [_add_sibling(place, g.reshape(2, -1, g.shape[-1]), t.reshape(-1, t.shape[-1])).reshape(t.shape)
                 for g, t in zip(part_sums, swapped)]
    from_chips = _scatter_to_owners(chip_sums)
    layer_sums = [_add_chips(place, own, others) for own, others in zip(chip_sums, from_chips)]
    full = dict(zip(SHARDED, _share_layers(layer_sums)))
    full["conv_w"] = full["conv_w"][:, :CONV_K]

    width = max(wts[n].shape[-1] for n in SMALL)
    pack = lambda vals: jnp.concatenate([_pad_lanes(v.reshape(-1, v.shape[-1]), width) for v in vals], axis=0)
    part = pack([jnp.concatenate(small[n], axis=0)[..., :wts[n].shape[-1]] for n in SMALL])
    rows = part.shape[0]
    part = jnp.pad(part, [(0, -rows % 8), (0, 0)])
    small_sum = _add_leading("add_devices", _gather_small(part), F32)
    for i, n in enumerate(SMALL):
        full[n] = small_sum[i * L:(i + 1) * L, :wts[n].shape[-1]]

    delta_w, new_m, new_v = {}, {}, {}
    for n in SHARDED:
        flat = lambda t: t.reshape(-1, t.shape[-1])
        r = flat(wts[n]).shape[0]
        fix = (lambda t: jnp.pad(flat(t), [(0, -r % 8), (0, 0)])) if r % 8 else flat
        outs = _adamw(fix(wts[n]), fix(full[n]), fix(mom[n]), fix(var[n]))
        delta_w[n], new_m[n], new_v[n] = [o[:r].reshape(wts[n].shape) for o in outs]
    pk = lambda src: jnp.pad(pack([src[n] for n in SMALL]), [(0, -rows % 8), (0, 0)])
    outs = _adamw(pk(wts), small_sum, pk(mom), pk(var))
    for i, n in enumerate(SMALL):
        delta_w[n], new_m[n], new_v[n] = [o[i * L:(i + 1) * L, :wts[n].shape[-1]] for o in outs]

    loss = lax.psum(loss_part[0, 0], ("x", "y", "c"))
    return (loss, grad_x, *[full[n] for n in ALL_WEIGHTS], *[delta_w[n] for n in ALL_WEIGHTS],
            *[new_m[n] for n in ALL_WEIGHTS], *[new_v[n] for n in ALL_WEIGHTS])
```

```python
import jax
import jax.numpy as jnp
from jax import lax
from jax.experimental import pallas as pl
from jax.experimental.pallas import tpu as pltpu

F32 = jnp.float32
MM_DTYPE = jnp.bfloat16
WIRE_DTYPE = jnp.bfloat16
EPS = 1e-6
N_HEADS = 8
QK_NOPE = 64
QK_ROPE = 32
QK_DIM = QK_NOPE + QK_ROPE
V_DIM = 64
HEAD_PAD = 128
HEADS_PER_STEP = 2
CHUNK_SHIFT = 6
LOG2E = 1.4426950408889634
CONV_K = 31
HALO = 32
ROPE_THETA = 10000.0
N_CHIPS = 4
ADAM_LR, ADAM_B1, ADAM_B2, ADAM_EPS, ADAM_WD, ADAM_STEP = 0.001, 0.9, 0.999, 1e-08, 0.01, 10
VMEM_LIMIT_BYTES = 56 * 2 ** 20
MESH = pl.DeviceIdType.MESH
ANY = pl.BlockSpec(memory_space=pl.ANY)

_DIMS = {
    "nn": (((1,), (0,)), ((), ())),
    "nt": (((1,), (1,)), ((), ())),
    "tn": (((0,), (0,)), ((), ())),
}


def _params(n_axes):
    return pltpu.CompilerParams(dimension_semantics=("arbitrary",) * n_axes, vmem_limit_bytes=VMEM_LIMIT_BYTES)


def _tile(n, cap):
    t = cap
    while n % t:
        t //= 2
    return t


def _mm(name, a, b, dims, grid, a_spec, b_spec, o_spec, out_shape, nk, acc_shape, into=None):
    n_axes = len(grid)

    def body(*refs):
        if into is not None:
            refs = refs[1:]
        a_ref, b_ref, o_ref = refs[:3]
        part = lax.dot_general(a_ref[...].astype(MM_DTYPE), b_ref[...].astype(MM_DTYPE), _DIMS[dims],
                               preferred_element_type=F32)
        if nk == 1:
            o_ref[...] = part.astype(o_ref.dtype)
        else:
            acc_ref = refs[3]
            k = pl.program_id(n_axes - 1)

            @pl.when(k == 0)
            def _():
                acc_ref[...] = part

            @pl.when(k > 0)
            def _():
                acc_ref[...] += part

            @pl.when(k == nk - 1)
            def _():
                o_ref[...] = acc_ref[...].astype(o_ref.dtype)

    in_specs = [a_spec, b_spec]
    args = [a, b]
    aliases = {}
    if into is not None:
        in_specs = [ANY] + in_specs
        args = [into] + args
        aliases = {0: 0}
    return pl.pallas_call(
        body, name=name, grid=grid, in_specs=in_specs, out_specs=o_spec, out_shape=out_shape,
        scratch_shapes=[] if nk == 1 else [pltpu.VMEM(acc_shape, F32)],
        input_output_aliases=aliases, compiler_params=_params(n_axes))(*args)


def _proj(name, a, w, l, dims, out_dtype=F32):
    s, k = a.shape
    g = w.shape[1]
    n = w.shape[3] if dims == "nn" else w.shape[2]
    tm = _tile(s, 512)
    return _mm(name, a, w, dims, (g, s // tm),
               pl.BlockSpec((tm, k), lambda j, i: (i, 0)),
               pl.BlockSpec((None, None) + w.shape[2:], lambda j, i: (l, j, 0, 0)),
               pl.BlockSpec((None, tm, n), lambda j, i: (j, i, 0)),
               jax.ShapeDtypeStruct((g, s, n), out_dtype), 1, None)


def _contract(name, a, w, l, dims, out_dtype=F32):
    g, s, kb = a.shape
    n = w.shape[3] if dims == "nn" else w.shape[2]
    tm = _tile(s, 512)
    return _mm(name, a, w, dims, (s // tm, g),
               pl.BlockSpec((None, tm, kb), lambda i, j: (j, i, 0)),
               pl.BlockSpec((None, None) + w.shape[2:], lambda i, j: (l, j, 0, 0)),
               pl.BlockSpec((tm, n), lambda i, j: (i, 0)),
               jax.ShapeDtypeStruct((s, n), out_dtype), g, (tm, n))


def _wgrad(name, a, b, out_dtype, layers=None, l=None, into=None):
    g = a.shape[0] if a.ndim == 3 else b.shape[0]
    s, k = a.shape[-2:]
    n = b.shape[-1]
    tm = _tile(s, 512)

    def spec(x, w):
        if x.ndim == 3:
            return pl.BlockSpec((None, tm, w), lambda j, i: (j, i, 0))
        return pl.BlockSpec((tm, w), lambda j, i: (i, 0))

    if layers is None:
        o_spec = pl.BlockSpec((None, k, n), lambda j, i: (j, 0, 0))
        out_shape = jax.ShapeDtypeStruct((g, k, n), out_dtype)
    else:
        o_spec = pl.BlockSpec((None, None, k, n), lambda j, i: (l, j, 0, 0))
        out_shape = jax.ShapeDtypeStruct((layers, g, k, n), out_dtype)
    return _mm(name, a, b, "tn", (g, s // tm), spec(a, k), spec(b, n), o_spec, out_shape, s // tm, (k, n), into=into)


def _ffn_up_fused(n, wg, wu, l):
    s, k = n.shape
    g, n4 = wg.shape[1], wg.shape[3]
    tm = _tile(s, 512)

    def body(n_ref, wg_ref, wu_ref, a_ref, b_ref, h_ref):
        nv = n_ref[...]
        a = jnp.dot(nv, wg_ref[...], preferred_element_type=F32)
        b = jnp.dot(nv, wu_ref[...], preferred_element_type=F32)
        a_ref[...] = a.astype(a_ref.dtype)
        b_ref[...] = b.astype(b_ref.dtype)
        h_ref[...] = (a * _sigmoid(a) * b).astype(h_ref.dtype)

    w_spec = pl.BlockSpec((None, None, k, n4), lambda j, i: (l, j, 0, 0))
    o_spec = pl.BlockSpec((None, tm, n4), lambda j, i: (j, i, 0))
    return pl.pallas_call(
        body, name="ffn_up_fused", grid=(g, s // tm),
        in_specs=[pl.BlockSpec((tm, k), lambda j, i: (i, 0)), w_spec, w_spec], out_specs=[o_spec] * 3,
        out_shape=[jax.ShapeDtypeStruct((g, s, n4), MM_DTYPE)] * 3, compiler_params=_params(2))(n, wg, wu)


def _ffn_down_t_fused(do, wd, a, b, l):
    s, k = do.shape
    g, n4 = wd.shape[1], wd.shape[2]
    tm = _tile(s, 512)

    def body(do_ref, wd_ref, a_ref, b_ref, da_ref, db_ref):
        dh = lax.dot_general(do_ref[...], wd_ref[...], _DIMS["nt"], preferred_element_type=F32)
        av = a_ref[...].astype(F32)
        sg = _sigmoid(av)
        da_ref[...] = (dh * b_ref[...].astype(F32) * (sg * (1.0 + av * (1.0 - sg)))).astype(da_ref.dtype)
        db_ref[...] = (dh * (av * sg)).astype(db_ref.dtype)

    t_spec = pl.BlockSpec((None, tm, n4), lambda j, i: (j, i, 0))
    return pl.pallas_call(
        body, name="ffn_down_t_fused", grid=(g, s // tm),
        in_specs=[pl.BlockSpec((tm, k), lambda j, i: (i, 0)),
                  pl.BlockSpec((None, None, n4, k), lambda j, i: (l, j, 0, 0)), t_spec, t_spec],
        out_specs=[t_spec] * 2, out_shape=[jax.ShapeDtypeStruct((g, s, n4), MM_DTYPE)] * 2,
        compiler_params=_params(2))(do, wd, a, b)


def _blocked_sum(name, acts, weights, l, dims):
    g, s, kb = acts[0].shape
    n = weights[0].shape[3] if dims == "nn" else weights[0].shape[2]
    tm = _tile(s, 256)
    n_pairs = len(acts)

    def body(*refs):
        o_ref = refs[2 * n_pairs]
        acc = None
        for p in range(n_pairs):
            for j in range(g):
                t = lax.dot_general(refs[p][j], refs[n_pairs + p][j], _DIMS[dims], preferred_element_type=F32)
                acc = t if acc is None else acc + t
        o_ref[...] = acc

    return pl.pallas_call(
        body, name=name, grid=(s // tm,),
        in_specs=[pl.BlockSpec((g, tm, kb), lambda i: (0, i, 0))] * n_pairs
        + [pl.BlockSpec((None,) + w.shape[1:], lambda i: (l, 0, 0, 0)) for w in weights],
        out_specs=pl.BlockSpec((tm, n), lambda i: (i, 0)), out_shape=jax.ShapeDtypeStruct((s, n), F32),
        compiler_params=_params(1))(*acts, *weights)


def _wgrad_pair(name, nt, da, db, layers, l, into):
    k, s = nt.shape
    g, _, n4 = da.shape
    tm = _tile(s, 512)
    nk = s // tm

    def body(*refs):
        if into is not None:
            refs = refs[2:]
        nt_ref, da_ref, db_ref, og_ref, ou_ref, accg, accu = refs
        i = pl.program_id(1)
        ntv = nt_ref[...]
        pg = jnp.dot(ntv, da_ref[...], preferred_element_type=F32)
        pu = jnp.dot(ntv, db_ref[...], preferred_element_type=F32)

        @pl.when(i == 0)
        def _():
            accg[...] = pg
            accu[...] = pu

        @pl.when(i > 0)
        def _():
            accg[...] += pg
            accu[...] += pu

        @pl.when(i == nk - 1)
        def _():
            og_ref[...] = accg[...].astype(og_ref.dtype)
            ou_ref[...] = accu[...].astype(ou_ref.dtype)

    t_spec = pl.BlockSpec((None, tm, n4), lambda j, i: (j, i, 0))
    o_spec = pl.BlockSpec((None, None, k, n4), lambda j, i: (l, j, 0, 0))
    in_specs = [pl.BlockSpec((k, tm), lambda j, i: (0, i)), t_spec, t_spec]
    args = [nt, da, db]
    aliases = {}
    if into is not None:
        in_specs = [ANY, ANY] + in_specs
        args = list(into) + args
        aliases = {0: 0, 1: 1}
    return pl.pallas_call(
        body, name=name, grid=(g, nk), in_specs=in_specs, out_specs=[o_spec] * 2,
        out_shape=[jax.ShapeDtypeStruct((layers, g, k, n4), WIRE_DTYPE)] * 2,
        scratch_shapes=[pltpu.VMEM((k, n4), F32)] * 2, input_output_aliases=aliases,
        compiler_params=_params(2))(*args)


def _rows(name, fn, s, tm, ins, outs, scratch=()):
    n = s // tm

    def spec(shape, kind):
        nd = len(shape)
        if kind == "r":
            return pl.BlockSpec(shape[:-2] + (tm, shape[-1]), lambda i: (0,) * (nd - 2) + (i, 0))
        if kind in ("f", "a"):
            return pl.BlockSpec(shape, lambda i: (0,) * nd)
        if kind == "t":
            return pl.BlockSpec((shape[0], tm), lambda i: (0, i))
        if kind == "b":
            return pl.BlockSpec((None,) + shape[1:], lambda i: (i, 0, 0))
        if kind == "p":
            return pl.BlockSpec((HALO, shape[-1]), lambda i: (jnp.maximum(i * (tm // HALO) - 1, 0), 0))
        if kind == "n":
            return pl.BlockSpec((HALO, shape[-1]), lambda i: (jnp.minimum((i + 1) * (tm // HALO), s // HALO - 1), 0))
        j = kind[1]
        return pl.BlockSpec((tm, HEAD_PAD), lambda i: (i, j))

    def body(*refs):
        fn(pl.program_id(0), n, *refs)

    return pl.pallas_call(
        body, name=name, grid=(n,),
        in_specs=[spec(a.shape, kind) for a, kind in ins],
        out_specs=[spec(shape, kind) for shape, _, kind in outs],
        out_shape=[jax.ShapeDtypeStruct(shape, dtype) for shape, dtype, _ in outs],
        scratch_shapes=list(scratch), compiler_params=_params(1))(*[a for a, _ in ins])


def _acc(ref, i, val):
    @pl.when(i == 0)
    def _():
        ref[...] = val

    @pl.when(i > 0)
    def _():
        ref[...] += val


def _sum0(x):
    return jnp.sum(x, axis=0, keepdims=True)


def _rstd(x, n):
    return lax.rsqrt(jnp.sum(x * x, axis=-1, keepdims=True) * (1.0 / n) + EPS)


def _rms_bwd(x, g, dy, n):
    r = _rstd(x, n)
    xh = x * r
    dyg = dy * g
    dx = r * (dyg - xh * (jnp.sum(dyg * xh, axis=-1, keepdims=True) * (1.0 / n)))
    return dx, _sum0(dy * xh)


def _sigmoid(x):
    return 1.0 / (1.0 + jnp.exp(-x))


def _norm_fwd(x, g):
    s, d = x.shape

    def fn(i, n, x_ref, g_ref, o_ref, ot_ref):
        xv = x_ref[...]
        y = xv * _rstd(xv, d) * g_ref[...]
        o_ref[...] = y.astype(o_ref.dtype)
        ot_ref[...] = y.T.astype(ot_ref.dtype)

    return _rows("norm_fwd", fn, s, _tile(s, 256), [(x, "r"), (g, "f")],
                 [((s, d), MM_DTYPE, "r"), ((d, s), MM_DTYPE, "t")])


def _resid_norm(x, adds, scale, g, norm_dtype, transposed=False):
    s, d = x.shape
    k = len(adds)

    def fn(i, n, *refs):
        x_ref, add_refs, g_ref, xo_ref, no_ref = refs[0], refs[1:1 + k], refs[1 + k], refs[2 + k], refs[3 + k]
        tot = add_refs[0][...]
        for r in add_refs[1:]:
            tot = tot + r[...]
        xv = x_ref[...] + scale * tot
        xo_ref[...] = xv
        y = xv * _rstd(xv, d) * g_ref[...]
        no_ref[...] = y.astype(no_ref.dtype)
        if transposed:
            refs[4 + k][...] = y.T.astype(refs[4 + k].dtype)

    return _rows("resid_norm", fn, s, _tile(s, 256), [(x, "r")] + [(a, "r") for a in adds] + [(g, "f")],
                 [((s, d), F32, "r"), ((s, d), norm_dtype, "r")] + ([((d, s), MM_DTYPE, "t")] if transposed else []))


def _norm_bwd(x, g, dn_parts, dres, out_scale):
    s, d = x.shape
    k = len(dn_parts)
    has_res = dres is not None

    def fn(i, n, *refs):
        x_ref, g_ref = refs[0], refs[1]
        dn_refs = refs[2:2 + k]
        pos = 2 + k
        dn = dn_refs[0][...]
        for r in dn_refs[1:]:
            dn = dn + r[...]
        dx, dg = _rms_bwd(x_ref[...], g_ref[...], dn, d)
        if has_res:
            dx = dx + refs[pos][...]
            pos += 1
        dx_ref, dxs_ref, dg_ref = refs[pos:pos + 3]
        dx_ref[...] = dx
        dxs_ref[...] = (out_scale * dx).astype(dxs_ref.dtype)
        _acc(dg_ref, i, dg)

    ins = [(x, "r"), (g, "f")] + [(p, "r") for p in dn_parts] + ([(dres, "r")] if has_res else [])
    return _rows("norm_bwd", fn, s, _tile(s, 256), ins,
                 [((s, d), F32, "r"), ((s, d), MM_DTYPE, "r"), ((1, d), F32, "a")])


def _final_fwd(x, o, g):
    return _resid_norm(x, [o], 0.5, g, F32)


def _loss(y, t):
    s, d = y.shape

    def fn(i, n, y_ref, t_ref, dy_ref, l_ref):
        e = y_ref[...] - t_ref[...]
        dy_ref[...] = e * (1.0 / d)
        _acc(l_ref, i, (0.5 / d) * jnp.sum(jnp.sum(e * e, axis=-1, keepdims=True), axis=0, keepdims=True))

    return _rows("loss", fn, s, _tile(s, 256), [(y, "r"), (t, "r")], [((s, d), F32, "r"), ((1, 1), F32, "a")])


def _mla_pre(p, gql, gkvl, cw):
    s = p.shape[0]
    ql, kvl = gql.shape[-1], gkvl.shape[-1]
    o_q, o_kv = 2 * cw, 2 * cw + ql

    def fn(i, n, p_ref, gq_ref, gkv_ref, cq_ref, ckv_ref, ckvt_ref):
        cq = p_ref[:, o_q:o_q + ql]
        cq_ref[...] = (cq * _rstd(cq, ql) * gq_ref[...]).astype(cq_ref.dtype)
        ckv = p_ref[:, o_kv:o_kv + kvl]
        y = ckv * _rstd(ckv, kvl) * gkv_ref[...]
        ckv_ref[...] = y.astype(ckv_ref.dtype)
        ckvt_ref[...] = y.T.astype(ckvt_ref.dtype)

    return _rows("mla_pre", fn, s, _tile(s, 256), [(p, "r"), (gql, "f"), (gkvl, "f")],
                 [((s, ql), MM_DTYPE, "r"), ((s, kvl), MM_DTYPE, "r"), ((kvl, s), MM_DTYPE, "t")])


def _rope(x, cosf, s1, s2):
    return x * cosf + pltpu.roll(x, HEAD_PAD - 16, 1) * s1 + pltpu.roll(x, 16, 1) * s2


def _rope_bwd(d, cosf, s1, s2):
    return d * cosf + pltpu.roll(d * s1, 16, 1) + pltpu.roll(d * s2, HEAD_PAD - 16, 1)


def _qk_post(q_pre, k_pre, p, pe_block, tabs, gq, gk):
    s, hw = q_pre.shape
    cosf, s1, s2 = tabs

    tm = _tile(s, 256)

    def fn(i, n, q_ref, k_ref, pe_ref, c_ref, s1_ref, s2_ref, gq_ref, gk_ref, qo_ref, ko_ref, kt_ref):
        c, a1, a2, pe = c_ref[...], s1_ref[...], s2_ref[...], pe_ref[...]
        for hh in range(N_HEADS):
            cols = slice(hh * HEAD_PAD, (hh + 1) * HEAD_PAD)
            qv = q_ref[:, cols]
            qo_ref[:, cols] = _rope(qv * _rstd(qv, QK_DIM) * gq_ref[...], c, a1, a2).astype(qo_ref.dtype)
            kv = k_ref[:, cols] + pe
            kout = _rope(kv * _rstd(kv, QK_DIM) * gk_ref[...], c, a1, a2)
            ko_ref[:, cols] = kout.astype(ko_ref.dtype)
            kt_ref[cols, :] = kout.T.astype(kt_ref.dtype)

    return _rows("qk_post", fn, s, tm,
                 [(q_pre, "r"), (k_pre, "r"), (p, ("c", pe_block)), (cosf, "r"), (s1, "r"), (s2, "r"), (gq, "f"), (gk, "f")],
                 [((s, hw), MM_DTYPE, "r"), ((s, hw), MM_DTYPE, "r"), ((s // tm, hw, tm), MM_DTYPE, "b")])


def _qk_post_bwd(q_pre, k_pre, p, pe_block, tabs, gq, gk, dq, dk):
    s, hw = q_pre.shape
    cosf, s1, s2 = tabs

    def fn(i, n, q_ref, k_ref, pe_ref, c_ref, s1_ref, s2_ref, gq_ref, gk_ref, dq_ref, dk_ref,
           dqo_ref, dko_ref, dpe_ref, dgq_ref, dgk_ref):
        c, a1, a2, pe = c_ref[...], s1_ref[...], s2_ref[...], pe_ref[...]
        lane = lax.broadcasted_iota(jnp.int32, pe.shape, 1)
        is_pe = (lane >= QK_NOPE) & (lane < QK_DIM)
        dpe = jnp.zeros(pe.shape, F32)
        dgq = jnp.zeros((1, HEAD_PAD), F32)
        dgk = jnp.zeros((1, HEAD_PAD), F32)
        for hh in range(N_HEADS):
            cols = slice(hh * HEAD_PAD, (hh + 1) * HEAD_PAD)
            dxq, g1 = _rms_bwd(q_ref[:, cols], gq_ref[...], _rope_bwd(dq_ref[cols, :].T, c, a1, a2), QK_DIM)
            dqo_ref[:, cols] = dxq.astype(dqo_ref.dtype)
            dxk, g2 = _rms_bwd(k_ref[:, cols] + pe, gk_ref[...], _rope_bwd(dk_ref[:, cols], c, a1, a2), QK_DIM)
            dko_ref[:, cols] = dxk.astype(dko_ref.dtype)
            dpe = dpe + dxk
            dgq = dgq + g1
            dgk = dgk + g2
        dpe_ref[...] = jnp.where(is_pe, dpe, 0.0)
        _acc(dgq_ref, i, dgq)
        _acc(dgk_ref, i, dgk)

    return _rows("qk_post_bwd", fn, s, _tile(s, 256),
                 [(q_pre, "r"), (k_pre, "r"), (p, ("c", pe_block)), (cosf, "r"), (s1, "r"), (s2, "r"), (gq, "f"), (gk, "f"),
                  (dq, "b"), (dk, "r")],
                 [((s, hw), MM_DTYPE, "r"), ((s, hw), MM_DTYPE, "r"), ((s, HEAD_PAD), F32, "r"),
                  ((1, HEAD_PAD), F32, "a"), ((1, HEAD_PAD), F32, "a")])


def _chunk_mask(tq, key_rows):
    r = jnp.right_shift(lax.broadcasted_iota(jnp.int32, (tq, tq), 0), CHUNK_SHIFT)
    c = jnp.right_shift(lax.broadcasted_iota(jnp.int32, (tq, tq), 1), CHUNK_SHIFT)
    return (r <= c) if key_rows else (c <= r)


def _attn_specs(s, tq):
    w = HEADS_PER_STEP * HEAD_PAD
    tile = pl.BlockSpec((tq, w), lambda hb, i: (i, hb))
    whole = pl.BlockSpec((s, w), lambda hb, i: (0, hb))
    rows = pl.BlockSpec((None, None, HEADS_PER_STEP, tq), lambda hb, i: (hb, i, 0, 0))
    blocks = pl.BlockSpec((s // tq, w, tq), lambda hb, i: (0, hb, 0))
    return tile, whole, rows, blocks


def _head(hh):
    return slice(hh * HEAD_PAD, (hh + 1) * HEAD_PAD)


def _attn_fwd(q, k, vt):
    s, hw = q.shape
    tq = _tile(s, 256)
    nq = s // tq
    groups = N_HEADS // HEADS_PER_STEP
    scale = QK_DIM ** -0.5
    tile, whole, rows, blocks = _attn_specs(s, tq)

    def body(q_ref, k_ref, vt_ref, o_ref, lse_ref):
        qi = pl.program_id(1)
        qv = [q_ref[:, _head(hh)] for hh in range(HEADS_PER_STEP)]

        def block(kb, carry, diag):
            off = pl.multiple_of(kb * tq, tq)
            out = []
            for hh in range(HEADS_PER_STEP):
                m, l, acc = carry[hh]
                ks = k_ref[pl.ds(off, tq), _head(hh)]
                sc = lax.dot_general(ks, qv[hh], _DIMS["nt"], preferred_element_type=F32) * (scale * LOG2E)
                if diag:
                    sc = jnp.where(_chunk_mask(tq, True), sc, -1e30)
                mn = jnp.maximum(m, jnp.max(sc, axis=0, keepdims=True))
                al = jnp.exp2(m - mn)
                pr = jnp.exp2(sc - mn)
                l = al * l + jnp.sum(pr, axis=0, keepdims=True)
                acc = al * acc + jnp.dot(vt_ref[kb, _head(hh), :], pr.astype(MM_DTYPE), preferred_element_type=F32)
                out.append((mn, l, acc))
            return tuple(out)

        init = tuple((jnp.full((1, tq), -1e30, F32), jnp.zeros((1, tq), F32), jnp.zeros((HEAD_PAD, tq), F32))
                     for _ in range(HEADS_PER_STEP))
        carry = lax.fori_loop(0, qi, lambda kb, c: block(kb, c, False), init)
        carry = block(qi, carry, True)
        for hh in range(HEADS_PER_STEP):
            m, l, acc = carry[hh]
            o_ref[:, _head(hh)] = (acc / l).T
            lse_ref[hh:hh + 1, :] = m + jnp.log(l) * LOG2E

    return pl.pallas_call(
        body, name="attn_fwd", grid=(groups, nq),
        in_specs=[tile, whole, blocks], out_specs=[tile, rows],
        out_shape=[jax.ShapeDtypeStruct((s, hw), F32), jax.ShapeDtypeStruct((groups, nq, HEADS_PER_STEP, tq), F32)],
        compiler_params=_params(2))(q, k, vt)


def _attn_delta(o, do):
    s, hw = o.shape
    tq = _tile(s, 256)
    groups = N_HEADS // HEADS_PER_STEP
    tile, _, rows, _ = _attn_specs(s, tq)

    def body(o_ref, do_ref, d_ref):
        for hh in range(HEADS_PER_STEP):
            prod = o_ref[:, _head(hh)] * do_ref[:, _head(hh)]
            d_ref[hh:hh + 1, :] = jnp.sum(prod.T, axis=0, keepdims=True)

    return pl.pallas_call(
        body, name="attn_delta", grid=(groups, s // tq), in_specs=[tile, tile], out_specs=rows,
        out_shape=jax.ShapeDtypeStruct((groups, s // tq, HEADS_PER_STEP, tq), F32),
        compiler_params=_params(2))(o, do)


def _attn_bwd(q, k, kt, v, do, lse_row, delta_row):
    s, hw = q.shape
    tq = _tile(s, 256)
    nq = s // tq
    groups = N_HEADS // HEADS_PER_STEP
    w = HEADS_PER_STEP * HEAD_PAD
    scale = QK_DIM ** -0.5
    tile, whole, _, blocks = _attn_specs(s, tq)
    rowvec = pl.BlockSpec((None, nq, HEADS_PER_STEP, tq), lambda hb, i: (hb, 0, 0, 0))
    kt_tile = pl.BlockSpec((None, w, tq), lambda hb, i: (i, hb, 0))

    def body(q_ref, k_ref, kt_ref, v_ref, do_ref, lse_ref, delta_ref, dqt_ref, dk_ref, dv_ref):
        ki = pl.program_id(1)
        kv = [k_ref[:, _head(hh)] for hh in range(HEADS_PER_STEP)]
        vv = [v_ref[:, _head(hh)] for hh in range(HEADS_PER_STEP)]
        ktv = [kt_ref[_head(hh), :] for hh in range(HEADS_PER_STEP)]

        @pl.when(ki == 0)
        def _():
            dqt_ref[...] = jnp.zeros(dqt_ref.shape, F32)

        def block(qb, carry, diag):
            off = pl.multiple_of(qb * tq, tq)
            out = []
            for hh in range(HEADS_PER_STEP):
                dk, dvv = carry[hh]
                qs = q_ref[pl.ds(off, tq), _head(hh)]
                dos = do_ref[pl.ds(off, tq), _head(hh)].astype(MM_DTYPE)
                sc = lax.dot_general(kv[hh], qs, _DIMS["nt"], preferred_element_type=F32) * (scale * LOG2E)
                if diag:
                    sc = jnp.where(_chunk_mask(tq, True), sc, -1e30)
                pr = jnp.exp2(sc - lse_ref[qb, hh:hh + 1, :])
                dpr = lax.dot_general(vv[hh], dos, _DIMS["nt"], preferred_element_type=F32)
                ds = (pr * (dpr - delta_ref[qb, hh:hh + 1, :]) * scale).astype(MM_DTYPE)
                dvv = dvv + jnp.dot(pr.astype(MM_DTYPE), dos, preferred_element_type=F32)
                dk = dk + jnp.dot(ds, qs, preferred_element_type=F32)
                dqt_ref[qb, _head(hh), :] += jnp.dot(ktv[hh], ds, preferred_element_type=F32)
                out.append((dk, dvv))
            return tuple(out)

        init = tuple((jnp.zeros((tq, HEAD_PAD), F32), jnp.zeros((tq, HEAD_PAD), F32)) for _ in range(HEADS_PER_STEP))
        carry = lax.fori_loop(ki + 1, nq, lambda qb, c: block(qb, c, False), block(ki, init, True))
        for hh in range(HEADS_PER_STEP):
            dk_ref[:, _head(hh)] = carry[hh][0]
            dv_ref[:, _head(hh)] = carry[hh][1].astype(dv_ref.dtype)

    return pl.pallas_call(
        body, name="attn_bwd", grid=(groups, nq),
        in_specs=[whole, tile, kt_tile, tile, whole, rowvec, rowvec], out_specs=[blocks, tile, tile],
        out_shape=[jax.ShapeDtypeStruct((nq, hw, tq), F32), jax.ShapeDtypeStruct((s, hw), F32),
                   jax.ShapeDtypeStruct((s, hw), MM_DTYPE)],
        compiler_params=_params(2))(q, k, kt, v, do, lse_row, delta_row)


def _glu_with_halo(i, p_ref, ph_ref, ubuf, cw, tm):
    a, g = p_ref[:, 0:cw], p_ref[:, cw:2 * cw]
    ubuf[HALO:HALO + tm, :] = a * _sigmoid(g)
    ah, gh = ph_ref[:, 0:cw], ph_ref[:, cw:2 * cw]
    ubuf[0:HALO, :] = jnp.where(i > 0, ah * _sigmoid(gh), 0.0)
    return a, g


def _conv_taps(ubuf, w_ref, tm):
    base = HALO - (CONV_K - 1)
    y = w_ref[0:1, :] * ubuf[base:base + tm, :]
    for k in range(1, CONV_K):
        y = y + w_ref[k:k + 1, :] * ubuf[base + k:base + k + tm, :]
    return y


def _layer_norm_stats(y, cw):
    mu = jnp.sum(y, axis=-1, keepdims=True) * (1.0 / cw)
    yc = y - mu
    r = lax.rsqrt(jnp.sum(yc * yc, axis=-1, keepdims=True) * (1.0 / cw) + EPS)
    return yc * r, r


def _conv_fwd(p, w, cb, lg, lb):
    s = p.shape[0]
    cw = cb.shape[-1]
    tm = _tile(s, 256)

    def fn(i, n, p_ref, ph_ref, w_ref, cb_ref, lg_ref, lb_ref, o_ref, ubuf):
        _glu_with_halo(i, p_ref, ph_ref, ubuf, cw, tm)
        yh, _ = _layer_norm_stats(_conv_taps(ubuf, w_ref, tm) + cb_ref[...], cw)
        z = yh * lg_ref[...] + lb_ref[...]
        o_ref[...] = (z * _sigmoid(z)).astype(o_ref.dtype)

    return _rows("conv_fwd", fn, s, tm, [(p, "r"), (p, "p"), (w, "f"), (cb, "f"), (lg, "f"), (lb, "f")],
                 [((s, cw), MM_DTYPE, "r")], scratch=[pltpu.VMEM((tm + HALO, cw), F32)])[0]


def _conv_bwd_ln(p, w, cb, lg, lb, dc):
    s = p.shape[0]
    cw = cb.shape[-1]
    tm = _tile(s, 256)

    def fn(i, n, p_ref, ph_ref, w_ref, cb_ref, lg_ref, lb_ref, dc_ref, dy_ref, dlg_ref, dlb_ref, dcb_ref, ubuf):
        _glu_with_halo(i, p_ref, ph_ref, ubuf, cw, tm)
        yh, r = _layer_norm_stats(_conv_taps(ubuf, w_ref, tm) + cb_ref[...], cw)
        z = yh * lg_ref[...] + lb_ref[...]
        sg = _sigmoid(z)
        dz = dc_ref[...] * (sg * (1.0 + z * (1.0 - sg)))
        dyh = dz * lg_ref[...]
        m1 = jnp.sum(dyh, axis=-1, keepdims=True) * (1.0 / cw)
        m2 = jnp.sum(dyh * yh, axis=-1, keepdims=True) * (1.0 / cw)
        dy = r * (dyh - m1 - yh * m2)
        dy_ref[...] = dy
        _acc(dlg_ref, i, _sum0(dz * yh))
        _acc(dlb_ref, i, _sum0(dz))
        _acc(dcb_ref, i, _sum0(dy))

    return _rows("conv_bwd_ln", fn, s, tm,
                 [(p, "r"), (p, "p"), (w, "f"), (cb, "f"), (lg, "f"), (lb, "f"), (dc, "r")],
                 [((s, cw), F32, "r"), ((1, cw), F32, "a"), ((1, cw), F32, "a"), ((1, cw), F32, "a")],
                 scratch=[pltpu.VMEM((tm + HALO, cw), F32)])


def _conv_bwd_taps(p, w, dy):
    s = p.shape[0]
    cw = w.shape[-1]
    tm = _tile(s, 256)

    def fn(i, n, p_ref, ph_ref, w_ref, dy_ref, dyn_ref, dp_ref, dw_ref, ubuf, dybuf):
        a, g = _glu_with_halo(i, p_ref, ph_ref, ubuf, cw, tm)
        dyv = dy_ref[...]
        dybuf[0:tm, :] = dyv
        dybuf[tm:tm + HALO, :] = jnp.where(i < n - 1, dyn_ref[...], 0.0)
        base = HALO - (CONV_K - 1)
        du = jnp.zeros((tm, cw), F32)

        @pl.when(i == 0)
        def _():
            dw_ref[...] = jnp.zeros(dw_ref.shape, F32)

        for k in range(CONV_K):
            sh = CONV_K - 1 - k
            du = du + w_ref[k:k + 1, :] * dybuf[sh:sh + tm, :]
            dw_ref[k:k + 1, :] += _sum0(dyv * ubuf[base + k:base + k + tm, :])
        sg = _sigmoid(g)
        dp_ref[:, 0:cw] = (du * sg).astype(dp_ref.dtype)
        dp_ref[:, cw:2 * cw] = (du * a * sg * (1.0 - sg)).astype(dp_ref.dtype)

    return _rows("conv_bwd_taps", fn, s, tm, [(p, "r"), (p, "p"), (w, "f"), (dy, "r"), (dy, "n")],
                 [((s, 2 * cw), MM_DTYPE, "r"), ((CONV_K + 1, cw), F32, "a")],
                 scratch=[pltpu.VMEM((tm + HALO, cw), F32), pltpu.VMEM((tm + HALO, cw), F32)])


def _dp_assemble(p, gql, gkvl, dcq, dckv_parts, dp_conv, dpe, cw):
    s, width = p.shape
    ql, kvl = gql.shape[-1], gkvl.shape[-1]
    o_q, o_kv, o_pe = 2 * cw, 2 * cw + ql, 2 * cw + ql + kvl

    def fn(i, n, p_ref, gq_ref, gkv_ref, dcq_ref, dk1_ref, dk2_ref, dconv_ref, dpe_ref, dp_ref, dgq_ref, dgkv_ref):
        dp_ref[:, 0:o_q] = dconv_ref[...]
        dx, dg = _rms_bwd(p_ref[:, o_q:o_kv], gq_ref[...], dcq_ref[...], ql)
        dp_ref[:, o_q:o_kv] = dx.astype(dp_ref.dtype)
        _acc(dgq_ref, i, dg)
        dx, dg = _rms_bwd(p_ref[:, o_kv:o_pe], gkv_ref[...], dk1_ref[...] + dk2_ref[...], kvl)
        dp_ref[:, o_kv:o_pe] = dx.astype(dp_ref.dtype)
        _acc(dgkv_ref, i, dg)
        dp_ref[:, o_pe:width] = dpe_ref[...].astype(dp_ref.dtype)

    return _rows("dp_assemble", fn, s, _tile(s, 256),
                 [(p, "r"), (gql, "f"), (gkvl, "f"), (dcq, "r"), (dckv_parts[0], "r"), (dckv_parts[1], "r"),
                  (dp_conv, "r"), (dpe, "r")],
                 [((s, width), MM_DTYPE, "r"), ((1, ql), F32, "a"), ((1, kvl), F32, "a")])


def _add_leading(name, x, out_dtype):
    n, r, c = x.shape
    tm = _tile(r, 256)

    def fn(i, nt, x_ref, o_ref):
        tot = x_ref[0].astype(F32)
        for k in range(1, n):
            tot = tot + x_ref[k].astype(F32)
        o_ref[...] = tot.astype(o_ref.dtype)

    return _rows(name, fn, r, tm, [(x, "r")], [((r, c), out_dtype, "r")])[0]


def _prefetch_call(name, fn, idx, grid, in_specs, out_spec, out_shape, args, aliases=None):
    return pl.pallas_call(
        fn, name=name, out_shape=out_shape,
        grid_spec=pltpu.PrefetchScalarGridSpec(num_scalar_prefetch=1, grid=grid, in_specs=in_specs, out_specs=out_spec),
        input_output_aliases=aliases or {}, compiler_params=_params(len(grid)))(idx, *args)


def _add_sibling(idx, mine, theirs):
    _, r, c = mine.shape
    tm = _tile(r, 256)

    def fn(idx_ref, a_ref, b_ref, o_ref):
        o_ref[...] = (a_ref[...].astype(F32) + b_ref[...].astype(F32)).astype(o_ref.dtype)

    return _prefetch_call(
        "add_sibling", fn, idx, (r // tm,),
        [pl.BlockSpec((None, tm, c), lambda i, ix: (ix[0], i, 0)), pl.BlockSpec((tm, c), lambda i, ix: (i, 0))],
        pl.BlockSpec((tm, c), lambda i, ix: (i, 0)), jax.ShapeDtypeStruct((r, c), WIRE_DTYPE), [mine, theirs])


def _add_chips(idx, own, others):
    _, r, c = own.shape
    tm = _tile(r, 256)

    def fn(idx_ref, a_ref, b_ref, o_ref):
        tot = a_ref[...].astype(F32)
        for k in range(3):
            tot = tot + b_ref[k].astype(F32)
        o_ref[...] = tot

    return _prefetch_call(
        "add_chips", fn, idx, (r // tm,),
        [pl.BlockSpec((None, tm, c), lambda i, ix: (ix[1], i, 0)), pl.BlockSpec((3, tm, c), lambda i, ix: (0, i, 0))],
        pl.BlockSpec((None, tm, c), lambda i, ix: (ix[0], i, 0)), jax.ShapeDtypeStruct((2, r, c), F32), [own, others])


def _adamw(w, g, m, v):
    r, c = w.shape
    tm = _tile(r, 256)

    def fn(i, n, w_ref, g_ref, m_ref, v_ref, d_ref, mo_ref, vo_ref):
        gv = g_ref[...]
        mn = ADAM_B1 * m_ref[...] + (1.0 - ADAM_B1) * gv
        vn = ADAM_B2 * v_ref[...] + (1.0 - ADAM_B2) * (gv * gv)
        mo_ref[...] = mn
        vo_ref[...] = vn
        m_hat = mn / (1.0 - ADAM_B1 ** ADAM_STEP)
        v_hat = vn / (1.0 - ADAM_B2 ** ADAM_STEP)
        d_ref[...] = -ADAM_LR * (m_hat / (jnp.sqrt(v_hat) + ADAM_EPS) + ADAM_WD * w_ref[...])

    return _rows("adamw", fn, r, tm, [(w, "r"), (g, "r"), (m, "r"), (v, "r")], [((r, c), F32, "r")] * 3)


def _place():
    x, y, c = lax.axis_index("x"), lax.axis_index("y"), lax.axis_index("c")
    other_chips = [(1 - x, y), (x, 1 - y), (1 - x, 1 - y)]
    return x, y, c, 2 * x + y, other_chips


def _gather_weights(bufs, shards):
    k_n = len(shards)

    def body(*refs):
        src, dst = refs[k_n:2 * k_n], refs[2 * k_n:3 * k_n]
        send, recv, fsend, frecv = refs[3 * k_n:]
        x, y, c, me, chips = _place()

        def direct(k, j):
            return pltpu.make_async_remote_copy(
                src_ref=src[k].at[c], dst_ref=dst[k].at[c, me], send_sem=send.at[k, j], recv_sem=recv.at[k, j],
                device_id=(chips[j][0], chips[j][1], c), device_id_type=MESH)

        def landed(k, j, layer, sems):
            cx, cy = chips[j]
            blk = dst[k].at[layer, 2 * cx + cy]
            return pltpu.make_async_remote_copy(
                src_ref=blk, dst_ref=blk, send_sem=sems[0].at[k, j], recv_sem=sems[1].at[k, j],
                device_id=(x, y, 1 - c), device_id_type=MESH)

        sent = [direct(k, j) for j in range(3) for k in range(k_n)]
        for cp in sent:
            cp.start()
        passed = []
        for j in range(3):
            for k in range(k_n):
                landed(k, j, c, (send, recv)).wait_recv()
                fwd = landed(k, j, c, (fsend, frecv))
                fwd.start()
                passed.append(fwd)
        for j in range(3):
            for k in range(k_n):
                landed(k, j, 1 - c, (fsend, frecv)).wait_recv()
        for cp in sent + passed:
            cp.wait_send()

    dma = lambda: pltpu.SemaphoreType.DMA((k_n, 3))
    return pl.pallas_call(
        body, name="gather_weights", in_specs=[ANY] * (2 * k_n), out_specs=[ANY] * k_n,
        out_shape=[jax.ShapeDtypeStruct(b.shape, b.dtype) for b in bufs],
        input_output_aliases={k: k for k in range(k_n)},
        scratch_shapes=[dma(), dma(), dma(), dma()],
        compiler_params=pltpu.CompilerParams(has_side_effects=True))(*bufs, *shards)


def _swap_layers(grads):
    k_n = len(grads)

    def body(*refs):
        src, dst = refs[:k_n], refs[k_n:2 * k_n]
        send, recv = refs[2 * k_n:]
        x, y, c, me, chips = _place()
        away = [pltpu.make_async_remote_copy(
            src_ref=src[k].at[1 - c], dst_ref=dst[k], send_sem=send.at[k], recv_sem=recv.at[k],
            device_id=(x, y, 1 - c), device_id_type=MESH) for k in range(k_n)]
        for cp in away:
            cp.start()
        for cp in away:
            cp.wait()

    dma = lambda: pltpu.SemaphoreType.DMA((k_n,))
    return pl.pallas_call(
        body, name="swap_layers", in_specs=[ANY] * k_n, out_specs=[ANY] * k_n,
        out_shape=[jax.ShapeDtypeStruct(g.shape[1:], g.dtype) for g in grads],
        scratch_shapes=[dma(), dma()],
        compiler_params=pltpu.CompilerParams(has_side_effects=True))(*grads)


def _scatter_to_owners(parts):
    k_n = len(parts)

    def body(*refs):
        src, dst = refs[:k_n], refs[k_n:2 * k_n]
        send, recv = refs[2 * k_n:]
        x, y, c, me, chips = _place()
        away = [pltpu.make_async_remote_copy(
            src_ref=src[k].at[2 * chips[j][0] + chips[j][1]], dst_ref=dst[k].at[j],
            send_sem=send.at[k, j], recv_sem=recv.at[k, j],
            device_id=(chips[j][0], chips[j][1], c), device_id_type=MESH) for j in range(3) for k in range(k_n)]
        for cp in away:
            cp.start()
        for cp in away:
            cp.wait()

    dma = lambda: pltpu.SemaphoreType.DMA((k_n, 3))
    return pl.pallas_call(
        body, name="scatter_to_owners", in_specs=[ANY] * k_n, out_specs=[ANY] * k_n,
        out_shape=[jax.ShapeDtypeStruct((3,) + p.shape[1:], p.dtype) for p in parts],
        scratch_shapes=[dma(), dma()],
        compiler_params=pltpu.CompilerParams(has_side_effects=True))(*parts)


def _share_layers(sums):
    k_n = len(sums)

    def body(*refs):
        dst = refs[k_n:2 * k_n]
        send, recv = refs[2 * k_n:]
        x, y, c, me, chips = _place()
        away = [pltpu.make_async_remote_copy(
            src_ref=dst[k].at[c], dst_ref=dst[k].at[c], send_sem=send.at[k], recv_sem=recv.at[k],
            device_id=(x, y, 1 - c), device_id_type=MESH) for k in range(k_n)]
        for cp in away:
            cp.start()
        for k in range(k_n):
            blk = dst[k].at[1 - c]
            pltpu.make_async_remote_copy(
                src_ref=blk, dst_ref=blk, send_sem=send.at[k], recv_sem=recv.at[k],
                device_id=(x, y, 1 - c), device_id_type=MESH).wait_recv()
        for cp in away:
            cp.wait_send()

    dma = lambda: pltpu.SemaphoreType.DMA((k_n,))
    return pl.pallas_call(
        body, name="share_layers", in_specs=[ANY] * k_n, out_specs=[ANY] * k_n,
        out_shape=[jax.ShapeDtypeStruct(s.shape, s.dtype) for s in sums],
        input_output_aliases={k: k for k in range(k_n)},
        scratch_shapes=[dma(), dma()],
        compiler_params=pltpu.CompilerParams(has_side_effects=True))(*sums)


def _gather_small(vec):
    flips = [(fx, fy, fc) for fx in (0, 1) for fy in (0, 1) for fc in (0, 1)][1:]

    def body(src, dst, send, recv, lsem):
        x, y, c, _, _ = _place()
        me = 4 * x + 2 * y + c
        local = pltpu.make_async_copy(src, dst.at[me], lsem)
        local.start()
        away = []
        for j, (fx, fy, fc) in enumerate(flips):
            px, py, pc = x ^ fx, y ^ fy, c ^ fc
            away.append(pltpu.make_async_remote_copy(
                src_ref=src, dst_ref=dst.at[me], send_sem=send.at[j], recv_sem=recv.at[j],
                device_id=(px, py, pc), device_id_type=MESH))
        for cp in away:
            cp.start()
        for j, (fx, fy, fc) in enumerate(flips):
            blk = dst.at[4 * (x ^ fx) + 2 * (y ^ fy) + (c ^ fc)]
            pltpu.make_async_remote_copy(
                src_ref=blk, dst_ref=blk, send_sem=send.at[j], recv_sem=recv.at[j],
                device_id=(x ^ fx, y ^ fy, c ^ fc), device_id_type=MESH).wait_recv()
        for cp in away:
            cp.wait_send()
        local.wait()

    return pl.pallas_call(
        body, name="gather_small", in_specs=[ANY], out_specs=ANY,
        out_shape=jax.ShapeDtypeStruct((8,) + vec.shape, vec.dtype),
        scratch_shapes=[pltpu.SemaphoreType.DMA((7,)), pltpu.SemaphoreType.DMA((7,)), pltpu.SemaphoreType.DMA(())],
        compiler_params=pltpu.CompilerParams(has_side_effects=True))(vec)


SHARDED = ["ffn1_w_gate", "ffn1_w_up", "ffn1_w_down", "w_in", "w_uq", "w_ukv", "conv_w", "w_out",
           "ffn2_w_gate", "ffn2_w_up", "ffn2_w_down"]
SMALL = ["ffn1_norm", "mix_norm", "q_latent_norm", "kv_latent_norm", "q_norm", "k_norm", "conv_b", "conv_ln_g",
         "conv_ln_b", "ffn2_norm", "post_norm"]
ALL_WEIGHTS = ["ffn1_norm", "ffn1_w_gate", "ffn1_w_up", "ffn1_w_down", "mix_norm", "w_in", "q_latent_norm", "w_uq",
               "kv_latent_norm", "w_ukv", "q_norm", "k_norm", "conv_w", "conv_b", "conv_ln_g", "conv_ln_b", "w_out",
               "ffn2_norm", "ffn2_w_gate", "ffn2_w_up", "ffn2_w_down", "post_norm"]


def _rope_tables(s):
    pos = jnp.arange(s, dtype=F32)
    inv_freq = 1.0 / (ROPE_THETA ** (jnp.arange(0, QK_ROPE, 2, dtype=F32) / QK_ROPE))
    ang = pos[:, None] * inv_freq[None, :]
    cos, sin = jnp.cos(ang), jnp.sin(ang)
    half = QK_ROPE // 2
    one = jnp.ones((s, QK_NOPE), F32)
    zero = jnp.zeros((s, QK_NOPE), F32)
    pad = HEAD_PAD - QK_DIM
    z16 = jnp.zeros((s, half), F32)
    cosf = jnp.concatenate([one, cos, cos, jnp.ones((s, pad), F32)], axis=1)
    s1 = jnp.concatenate([zero, -sin, z16, jnp.zeros((s, pad), F32)], axis=1)
    s2 = jnp.concatenate([zero, z16, sin, jnp.zeros((s, pad), F32)], axis=1)
    return cosf, s1, s2


def _ffn_fwd(n, w, names, l):
    a, b, hm = _ffn_up_fused(n, w[names[0]], w[names[1]], l)
    return a, b, hm, _blocked_sum("ffn_down", [hm], [w[names[2]]], l, "nn")


def _ffn_bwd(nt, a, b, hm, do, w, names, l, grads):
    gate, up, down = names
    layers = w[gate].shape[0]
    da, db = _ffn_down_t_fused(do, w[down], a, b, l)
    grads[down] = _wgrad("ffn_down_wgrad", hm, do, WIRE_DTYPE, layers, l, grads.get(down))
    grads[gate], grads[up] = _wgrad_pair("ffn_up_wgrad", nt, da, db, layers, l,
                                         (grads[gate], grads[up]) if gate in grads else None)
    return [_blocked_sum("ffn_up_t", [da, db], [w[gate], w[up]], l, "nt")]


def _vec(v, l):
    return v[l][None, :]


def _pad_lanes(v, width):
    return jnp.pad(v, [(0, 0)] * (v.ndim - 1) + [(0, width - v.shape[-1])])


def kernel(x, ffn1_norm, ffn1_w_gate, ffn1_w_up, ffn1_w_down, mix_norm, w_in, q_latent_norm, w_uq, kv_latent_norm, w_ukv, q_norm, k_norm, conv_w, conv_b, conv_ln_g, conv_ln_b, w_out, ffn2_norm, ffn2_w_gate, ffn2_w_up, ffn2_w_down, post_norm, loss_target, m_ffn1_norm, m_ffn1_w_gate, m_ffn1_w_up, m_ffn1_w_down, m_mix_norm, m_w_in, m_q_latent_norm, m_w_uq, m_kv_latent_norm, m_w_ukv, m_q_norm, m_k_norm, m_conv_w, m_conv_b, m_conv_ln_g, m_conv_ln_b, m_w_out, m_ffn2_norm, m_ffn2_w_gate, m_ffn2_w_up, m_ffn2_w_down, m_post_norm, v_ffn1_norm, v_ffn1_w_gate, v_ffn1_w_up, v_ffn1_w_down, v_mix_norm, v_w_in, v_q_latent_norm, v_w_uq, v_kv_latent_norm, v_w_ukv, v_q_norm, v_k_norm, v_conv_w, v_conv_b, v_conv_ln_g, v_conv_ln_b, v_w_out, v_ffn2_norm, v_ffn2_w_gate, v_ffn2_w_up, v_ffn2_w_down, v_post_norm):
    given = dict(locals())
    wts = {n: given[n] for n in ALL_WEIGHTS}
    mom = {n: given["m_" + n] for n in ALL_WEIGHTS}
    var = {n: given["v_" + n] for n in ALL_WEIGHTS}
    n_layers = ffn1_norm.shape[0]
    s, d = x.shape[1], x.shape[2]
    ql, kvl, cw = q_latent_norm.shape[-1], kv_latent_norm.shape[-1], conv_b.shape[-1]
    heads_per_chip = N_HEADS // N_CHIPS
    xs = x.reshape(s, d)
    target = loss_target.reshape(s, d)

    assert n_layers == 2, "core c of a chip finishes the gradients of layer c"
    local = [jnp.pad(wts[n], ((0, 0), (0, 1), (0, 0))).astype(MM_DTYPE) if n == "conv_w" else wts[n].astype(MM_DTYPE)
             for n in SHARDED]
    L = n_layers
    my_c = lax.axis_index("c")
    my_chip = 2 * lax.axis_index("x") + lax.axis_index("y")
    place = jnp.stack([my_c, my_chip]).astype(jnp.int32)
    bufs = [lax.dynamic_update_slice(lax.empty((L, N_CHIPS) + t.shape[1:], t.dtype), t[:, None], (0, my_chip, 0, 0))
            for t in local]
    w = dict(zip(SHARDED, _gather_weights(bufs, local)))
    hw = N_HEADS * HEAD_PAD
    win = w["w_in"].transpose(0, 2, 1, 3).reshape(L, d, -1)
    zc = lambda n: jnp.zeros((L, d, n), MM_DTYPE)
    o_pe, o_cv = ql + kvl, ql + kvl + QK_ROPE
    w["w_in"] = jnp.concatenate([win[..., o_cv:], win[..., :o_pe], zc(QK_NOPE), win[..., o_pe:o_cv],
                                 zc(HEAD_PAD - QK_DIM)], axis=-1)[:, None]
    wq = w["w_uq"].transpose(0, 2, 1, 3).reshape(L, ql, N_HEADS, QK_DIM)
    w["wq"] = _pad_lanes(wq, HEAD_PAD).reshape(L, 1, ql, hw)
    wkv = w["w_ukv"].transpose(0, 2, 1, 3).reshape(L, kvl, N_HEADS, QK_NOPE + V_DIM)
    w["wk"] = _pad_lanes(wkv[..., :QK_NOPE], HEAD_PAD).reshape(L, 1, kvl, hw)
    w["wv"] = _pad_lanes(wkv[..., QK_NOPE:], HEAD_PAD).reshape(L, 1, kvl, hw)
    wvt = w["wv"][:, 0].transpose(0, 2, 1)
    conv_full = w["conv_w"].transpose(0, 2, 1, 3).reshape(L, CONV_K + 1, cw).astype(F32)
    wout = w["w_out"].reshape(L, d, d)
    wo_a = wout[:, :N_HEADS * V_DIM].reshape(L, N_HEADS, V_DIM, d)
    w["wo_a"] = jnp.pad(wo_a, ((0, 0), (0, 0), (0, HEAD_PAD - V_DIM), (0, 0))).reshape(L, 1, hw, d)
    w["wo_c"] = wout[:, N_HEADS * V_DIM:][:, None]
    tabs = _rope_tables(s)
    pe_block = (2 * cw + ql + kvl) // HEAD_PAD
    gq_pad, gk_pad = _pad_lanes(q_norm, HEAD_PAD), _pad_lanes(k_norm, HEAD_PAD)
    tq = _tile(s, 256)
    groups = N_HEADS // HEADS_PER_STEP

    saved = []
    cur = xs
    for l in range(L):
        sv = {"x": cur}
        n1, sv["n1t"] = _norm_fwd(cur, _vec(ffn1_norm, l))
        sv["a1"], sv["b1"], sv["hm1"], o1 = _ffn_fwd(n1, w, SHARDED[0:3], l)
        sv["x1"], sv["h"] = _resid_norm(cur, [o1], 0.5, _vec(mix_norm, l), MM_DTYPE)
        sv["p"] = _proj("w_in", sv["h"], w["w_in"], l, "nn")[0]
        sv["cq"], sv["ckv"], ckvt = _mla_pre(sv["p"], _vec(q_latent_norm, l), _vec(kv_latent_norm, l), cw)
        sv["q_pre"] = _proj("wq", sv["cq"], w["wq"], l, "nn")[0]
        sv["k_pre"] = _proj("wk", sv["ckv"], w["wk"], l, "nn")[0]
        sv["v"] = _proj("wv", sv["ckv"], w["wv"], l, "nn", MM_DTYPE)[0]
        vt = _mm("wv_transposed", wvt, ckvt, "nn", (s // tq,),
                 pl.BlockSpec((None, hw, kvl), lambda i, l=l: (l, 0, 0)), pl.BlockSpec((kvl, tq), lambda i: (0, i)),
                 pl.BlockSpec((None, hw, tq), lambda i: (i, 0, 0)),
                 jax.ShapeDtypeStruct((s // tq, hw, tq), MM_DTYPE), 1, None)
        sv["q"], sv["k"], sv["kt"] = _qk_post(sv["q_pre"], sv["k_pre"], sv["p"], pe_block, tabs,
                                              _vec(gq_pad, l), _vec(gk_pad, l))
        sv["o"], sv["lse"] = _attn_fwd(sv["q"], sv["k"], vt)
        sv["c"] = _conv_fwd(sv["p"], conv_full[l], _vec(conv_b, l), _vec(conv_ln_g, l), _vec(conv_ln_b, l))
        mix_a = _contract("wo_a", sv["o"][None], w["wo_a"], l, "nn")
        mix_c = _contract("wo_c", sv["c"][None], w["wo_c"], l, "nn")
        sv["x2"], n2, sv["n2t"] = _resid_norm(sv["x1"], [mix_a, mix_c], 1.0, _vec(ffn2_norm, l), MM_DTYPE, True)
        sv["a2"], sv["b2"], sv["hm2"], o2 = _ffn_fwd(n2, w, SHARDED[8:11], l)
        sv["x3"], cur = _final_fwd(sv["x2"], o2, _vec(post_norm, l))
        saved.append(sv)
    dy, loss_part = _loss(cur, target)

    grads = {}
    small = {n: [None] * L for n in SMALL}
    stacked = {}
    for l in reversed(range(L)):
        sv = saved[l]
        dx3, do2, small["post_norm"][l] = _norm_bwd(sv["x3"], _vec(post_norm, l), [dy], None, 0.5)
        dn2 = _ffn_bwd(sv["n2t"], sv["a2"], sv["b2"], sv["hm2"], do2, w, SHARDED[8:11], l, grads)
        dx2, dx2b, small["ffn2_norm"][l] = _norm_bwd(sv["x2"], _vec(ffn2_norm, l), dn2, dx3, 1.0)
        do = _proj("wo_a_t", dx2b, w["wo_a"], l, "nt")[0]
        dc = _proj("wo_c_t", dx2b, w["wo_c"], l, "nt")[0]
        g_wo_a = _wgrad("wo_a_wgrad", sv["o"], dx2b[None], WIRE_DTYPE)[0].reshape(N_HEADS, HEAD_PAD, d)[:, :V_DIM]
        g_wo_c = _wgrad("wo_c_wgrad", sv["c"], dx2b[None], WIRE_DTYPE)
        stacked.setdefault("w_out", [None] * L)[l] = jnp.concatenate(
            [g_wo_a.reshape(N_HEADS * V_DIM, d), g_wo_c[0]], axis=0).reshape(N_CHIPS, d // N_CHIPS, d)
        dq, dk, dv = _attn_bwd(sv["q"], sv["k"], sv["kt"], sv["v"], do, sv["lse"], _attn_delta(sv["o"], do))
        dq_pre, dk_pre, dpe, small["q_norm"][l], small["k_norm"][l] = _qk_post_bwd(
            sv["q_pre"], sv["k_pre"], sv["p"], pe_block, tabs, _vec(gq_pad, l), _vec(gk_pad, l), dq, dk)
        dcq = _contract("wq_t", dq_pre[None], w["wq"], l, "nt")
        dckv = [_contract("wk_t", dk_pre[None], w["wk"], l, "nt"), _contract("wv_t", dv[None], w["wv"], l, "nt")]
        per_head = lambda t, rows: t[0].reshape(rows, N_HEADS, HEAD_PAD)
        g_wq = per_head(_wgrad("wq_wgrad", sv["cq"], dq_pre[None], WIRE_DTYPE), ql)[..., :QK_DIM]
        stacked.setdefault("w_uq", [None] * L)[l] = g_wq.reshape(ql, N_CHIPS, -1).transpose(1, 0, 2)
        g_wkv = jnp.concatenate([per_head(_wgrad("wk_wgrad", sv["ckv"], dk_pre[None], WIRE_DTYPE), kvl)[..., :QK_NOPE],
                                 per_head(_wgrad("wv_wgrad", sv["ckv"], dv[None], WIRE_DTYPE), kvl)[..., :V_DIM]], axis=-1)
        stacked.setdefault("w_ukv", [None] * L)[l] = g_wkv.reshape(kvl, N_CHIPS, -1).transpose(1, 0, 2)
        dyc, small["conv_ln_g"][l], small["conv_ln_b"][l], small["conv_b"][l] = _conv_bwd_ln(
            sv["p"], conv_full[l], _vec(conv_b, l), _vec(conv_ln_g, l), _vec(conv_ln_b, l), dc)
        dp_conv, g_conv = _conv_bwd_taps(sv["p"], conv_full[l], dyc)
        stacked.setdefault("conv_w", [None] * L)[l] = g_conv.reshape(CONV_K + 1, N_CHIPS, cw // N_CHIPS).transpose(
            1, 0, 2).astype(WIRE_DTYPE)
        dp, small["q_latent_norm"][l], small["kv_latent_norm"][l] = _dp_assemble(
            sv["p"], _vec(q_latent_norm, l), _vec(kv_latent_norm, l), dcq, dckv, dp_conv, dpe, cw)
        dh = _contract("w_in_t", dp[None], w["w_in"], l, "nt")
        g_win = _wgrad("w_in_wgrad", sv["h"], dp[None], WIRE_DTYPE)[0]
        o_q, o_pe_col = 2 * cw, 2 * cw + ql + kvl + QK_NOPE
        g_win = jnp.concatenate([g_win[:, o_q:o_q + ql + kvl], g_win[:, o_pe_col:o_pe_col + QK_ROPE], g_win[:, :o_q]], axis=1)
        stacked.setdefault("w_in", [None] * L)[l] = g_win.reshape(d, N_CHIPS, -1).transpose(1, 0, 2)
        dx1, do1, small["mix_norm"][l] = _norm_bwd(sv["x1"], _vec(mix_norm, l), [dh], dx2, 0.5)
        dn1 = _ffn_bwd(sv["n1t"], sv["a1"], sv["b1"], sv["hm1"], do1, w, SHARDED[0:3], l, grads)
        dy, _, small["ffn1_norm"][l] = _norm_bwd(sv["x"], _vec(ffn1_norm, l), dn1, dx1, 1.0)
    grad_x = dy.reshape(x.shape)
    for n, per_layer in stacked.items():
        grads[n] = jnp.stack(per_layer, axis=0)

    part_sums = [grads[n] for n in SHARDED]
    swapped = _swap_layers(part_sums)
    chip_sums = [_add_sibling(place, g.reshape(2, -1, g.shape[-1]), t.reshape(-1, t.shape[-1])).reshape(t.shape)
                 for g, t in zip(part_sums, swapped)]
    from_chips = _scatter_to_owners(chip_sums)
    layer_sums = [_add_chips(place, own, others) for own, others in zip(chip_sums, from_chips)]
    full = dict(zip(SHARDED, _share_layers(layer_sums)))
    full["conv_w"] = full["conv_w"][:, :CONV_K]

    width = max(wts[n].shape[-1] for n in SMALL)
    pack = lambda vals: jnp.concatenate([_pad_lanes(v.reshape(-1, v.shape[-1]), width) for v in vals], axis=0)
    part = pack([jnp.concatenate(small[n], axis=0)[..., :wts[n].shape[-1]] for n in SMALL])
    rows = part.shape[0]
    part = jnp.pad(part, [(0, -rows % 8), (0, 0)])
    small_sum = _add_leading("add_devices", _gather_small(part), F32)
    for i, n in enumerate(SMALL):
        full[n] = small_sum[i * L:(i + 1) * L, :wts[n].shape[-1]]

    delta_w, new_m, new_v = {}, {}, {}
    for n in SHARDED:
        flat = lambda t: t.reshape(-1, t.shape[-1])
        r = flat(wts[n]).shape[0]
        fix = (lambda t: jnp.pad(flat(t), [(0, -r % 8), (0, 0)])) if r % 8 else flat
        outs = _adamw(fix(wts[n]), fix(full[n]), fix(mom[n]), fix(var[n]))
        delta_w[n], new_m[n], new_v[n] = [o[:r].reshape(wts[n].shape) for o in outs]
    pk = lambda src: jnp.pad(pack([src[n] for n in SMALL]), [(0, -rows % 8), (0, 0)])
    outs = _adamw(pk(wts), small_sum, pk(mom), pk(var))
    for i, n in enumerate(SMALL):
        delta_w[n], new_m[n], new_v[n] = [o[i * L:(i + 1) * L, :wts[n].shape[-1]] for o in outs]

    loss = lax.psum(loss_part[0, 0], ("x", "y", "c"))
    return (loss, grad_x, *[full[n] for n in ALL_WEIGHTS], *[delta_w[n] for n in ALL_WEIGHTS],
            *[new_m[n] for n in ALL_WEIGHTS], *[new_v[n] for n in ALL_WEIGHTS])
```

```python
import jax
import jax.numpy as jnp
from jax import lax
from jax.experimental import pallas as pl
from jax.experimental.pallas import tpu as pltpu

F32 = jnp.float32
MM_DTYPE = jnp.bfloat16
WIRE_DTYPE = jnp.bfloat16
EPS = 1e-6
N_HEADS = 8
QK_NOPE = 64
QK_ROPE = 32
QK_DIM = QK_NOPE + QK_ROPE
V_DIM = 64
HEAD_PAD = 128
HEADS_PER_STEP = 2
CHUNK_SHIFT = 6
LOG2E = 1.4426950408889634
CONV_K = 31
HALO = 32
ROPE_THETA = 10000.0
N_CHIPS = 4
ADAM_LR, ADAM_B1, ADAM_B2, ADAM_EPS, ADAM_WD, ADAM_STEP = 0.001, 0.9, 0.999, 1e-08, 0.01, 10
VMEM_LIMIT_BYTES = 56 * 2 ** 20
MESH = pl.DeviceIdType.MESH
ANY = pl.BlockSpec(memory_space=pl.ANY)

_DIMS = {
    "nn": (((1,), (0,)), ((), ())),
    "nt": (((1,), (1,)), ((), ())),
    "tn": (((0,), (0,)), ((), ())),
}


def _params(n_axes):
    return pltpu.CompilerParams(dimension_semantics=("arbitrary",) * n_axes, vmem_limit_bytes=VMEM_LIMIT_BYTES)


def _tile(n, cap):
    t = cap
    while n % t:
        t //= 2
    return t


def _mm(name, a, b, dims, grid, a_spec, b_spec, o_spec, out_shape, nk, acc_shape, into=None):
    n_axes = len(grid)

    def body(*refs):
        if into is not None:
            refs = refs[1:]
        a_ref, b_ref, o_ref = refs[:3]
        part = lax.dot_general(a_ref[...].astype(MM_DTYPE), b_ref[...].astype(MM_DTYPE), _DIMS[dims],
                               preferred_element_type=F32)
        if nk == 1:
            o_ref[...] = part.astype(o_ref.dtype)
        else:
            acc_ref = refs[3]
            k = pl.program_id(n_axes - 1)

            @pl.when(k == 0)
            def _():
                acc_ref[...] = part

            @pl.when(k > 0)
            def _():
                acc_ref[...] += part

            @pl.when(k == nk - 1)
            def _():
                o_ref[...] = acc_ref[...].astype(o_ref.dtype)

    in_specs = [a_spec, b_spec]
    args = [a, b]
    aliases = {}
    if into is not None:
        in_specs = [ANY] + in_specs
        args = [into] + args
        aliases = {0: 0}
    return pl.pallas_call(
        body, name=name, grid=grid, in_specs=in_specs, out_specs=o_spec, out_shape=out_shape,
        scratch_shapes=[] if nk == 1 else [pltpu.VMEM(acc_shape, F32)],
        input_output_aliases=aliases, compiler_params=_params(n_axes))(*args)


def _proj(name, a, w, l, dims, out_dtype=F32):
    s, k = a.shape
    g = w.shape[1]
    n = w.shape[3] if dims == "nn" else w.shape[2]
    tm = _tile(s, 512)
    return _mm(name, a, w, dims, (g, s // tm),
               pl.BlockSpec((tm, k), lambda j, i: (i, 0)),
               pl.BlockSpec((None, None) + w.shape[2:], lambda j, i: (l, j, 0, 0)),
               pl.BlockSpec((None, tm, n), lambda j, i: (j, i, 0)),
               jax.ShapeDtypeStruct((g, s, n), out_dtype), 1, None)


def _contract(name, a, w, l, dims, out_dtype=F32):
    g, s, kb = a.shape
    n = w.shape[3] if dims == "nn" else w.shape[2]
    tm = _tile(s, 512)
    return _mm(name, a, w, dims, (s // tm, g),
               pl.BlockSpec((None, tm, kb), lambda i, j: (j, i, 0)),
               pl.BlockSpec((None, None) + w.shape[2:], lambda i, j: (l, j, 0, 0)),
               pl.BlockSpec((tm, n), lambda i, j: (i, 0)),
               jax.ShapeDtypeStruct((s, n), out_dtype), g, (tm, n))


def _wgrad(name, a, b, out_dtype, layers=None, l=None, into=None):
    g = a.shape[0] if a.ndim == 3 else b.shape[0]
    s, k = a.shape[-2:]
    n = b.shape[-1]
    tm = _tile(s, 512)

    def spec(x, w):
        if x.ndim == 3:
            return pl.BlockSpec((None, tm, w), lambda j, i: (j, i, 0))
        return pl.BlockSpec((tm, w), lambda j, i: (i, 0))

    if layers is None:
        o_spec = pl.BlockSpec((None, k, n), lambda j, i: (j, 0, 0))
        out_shape = jax.ShapeDtypeStruct((g, k, n), out_dtype)
    else:
        o_spec = pl.BlockSpec((None, None, k, n), lambda j, i: (l, j, 0, 0))
        out_shape = jax.ShapeDtypeStruct((layers, g, k, n), out_dtype)
    return _mm(name, a, b, "tn", (g, s // tm), spec(a, k), spec(b, n), o_spec, out_shape, s // tm, (k, n), into=into)


def _ffn_up_fused(n, wg, wu, l):
    s, k = n.shape
    g, n4 = wg.shape[1], wg.shape[3]
    tm = _tile(s, 512)

    def body(n_ref, wg_ref, wu_ref, a_ref, b_ref, h_ref):
        nv = n_ref[...]
        a = jnp.dot(nv, wg_ref[...], preferred_element_type=F32)
        b = jnp.dot(nv, wu_ref[...], preferred_element_type=F32)
        a_ref[...] = a.astype(a_ref.dtype)
        b_ref[...] = b.astype(b_ref.dtype)
        h_ref[...] = (a * _sigmoid(a) * b).astype(h_ref.dtype)

    w_spec = pl.BlockSpec((None, None, k, n4), lambda j, i: (l, j, 0, 0))
    o_spec = pl.BlockSpec((None, tm, n4), lambda j, i: (j, i, 0))
    return pl.pallas_call(
        body, name="ffn_up_fused", grid=(g, s // tm),
        in_specs=[pl.BlockSpec((tm, k), lambda j, i: (i, 0)), w_spec, w_spec], out_specs=[o_spec] * 3,
        out_shape=[jax.ShapeDtypeStruct((g, s, n4), MM_DTYPE)] * 3, compiler_params=_params(2))(n, wg, wu)


def _ffn_down_t_fused(do, wd, a, b, l):
    s, k = do.shape
    g, n4 = wd.shape[1], wd.shape[2]
    tm = _tile(s, 512)

    def body(do_ref, wd_ref, a_ref, b_ref, da_ref, db_ref):
        dh = lax.dot_general(do_ref[...], wd_ref[...], _DIMS["nt"], preferred_element_type=F32)
        av = a_ref[...].astype(F32)
        sg = _sigmoid(av)
        da_ref[...] = (dh * b_ref[...].astype(F32) * (sg * (1.0 + av * (1.0 - sg)))).astype(da_ref.dtype)
        db_ref[...] = (dh * (av * sg)).astype(db_ref.dtype)

    t_spec = pl.BlockSpec((None, tm, n4), lambda j, i: (j, i, 0))
    return pl.pallas_call(
        body, name="ffn_down_t_fused", grid=(g, s // tm),
        in_specs=[pl.BlockSpec((tm, k), lambda j, i: (i, 0)),
                  pl.BlockSpec((None, None, n4, k), lambda j, i: (l, j, 0, 0)), t_spec, t_spec],
        out_specs=[t_spec] * 2, out_shape=[jax.ShapeDtypeStruct((g, s, n4), MM_DTYPE)] * 2,
        compiler_params=_params(2))(do, wd, a, b)


def _blocked_sum(name, acts, weights, l, dims):
    g, s, kb = acts[0].shape
    n = weights[0].shape[3] if dims == "nn" else weights[0].shape[2]
    tm = _tile(s, 256)
    n_pairs = len(acts)

    def body(*refs):
        o_ref = refs[2 * n_pairs]
        acc = None
        for p in range(n_pairs):
            for j in range(g):
                t = lax.dot_general(refs[p][j], refs[n_pairs + p][j], _DIMS[dims], preferred_element_type=F32)
                acc = t if acc is None else acc + t
        o_ref[...] = acc

    return pl.pallas_call(
        body, name=name, grid=(s // tm,),
        in_specs=[pl.BlockSpec((g, tm, kb), lambda i: (0, i, 0))] * n_pairs
        + [pl.BlockSpec((None,) + w.shape[1:], lambda i: (l, 0, 0, 0)) for w in weights],
        out_specs=pl.BlockSpec((tm, n), lambda i: (i, 0)), out_shape=jax.ShapeDtypeStruct((s, n), F32),
        compiler_params=_params(1))(*acts, *weights)


def _wgrad_pair(name, nt, da, db, layers, l, into):
    k, s = nt.shape
    g, _, n4 = da.shape
    tm = _tile(s, 512)
    nk = s // tm

    def body(*refs):
        if into is not None:
            refs = refs[2:]
        nt_ref, da_ref, db_ref, og_ref, ou_ref, accg, accu = refs
        i = pl.program_id(1)
        ntv = nt_ref[...]
        pg = jnp.dot(ntv, da_ref[...], preferred_element_type=F32)
        pu = jnp.dot(ntv, db_ref[...], preferred_element_type=F32)

        @pl.when(i == 0)
        def _():
            accg[...] = pg
            accu[...] = pu

        @pl.when(i > 0)
        def _():
            accg[...] += pg
            accu[...] += pu

        @pl.when(i == nk - 1)
        def _():
            og_ref[...] = accg[...].astype(og_ref.dtype)
            ou_ref[...] = accu[...].astype(ou_ref.dtype)

    t_spec = pl.BlockSpec((None, tm, n4), lambda j, i: (j, i, 0))
    o_spec = pl.BlockSpec((None, None, k, n4), lambda j, i: (l, j, 0, 0))
    in_specs = [pl.BlockSpec((k, tm), lambda j, i: (0, i)), t_spec, t_spec]
    args = [nt, da, db]
    aliases = {}
    if into is not None:
        in_specs = [ANY, ANY] + in_specs
        args = list(into) + args
        aliases = {0: 0, 1: 1}
    return pl.pallas_call(
        body, name=name, grid=(g, nk), in_specs=in_specs, out_specs=[o_spec] * 2,
        out_shape=[jax.ShapeDtypeStruct((layers, g, k, n4), WIRE_DTYPE)] * 2,
        scratch_shapes=[pltpu.VMEM((k, n4), F32)] * 2, input_output_aliases=aliases,
        compiler_params=_params(2))(*args)


def _rows(name, fn, s, tm, ins, outs, scratch=()):
    n = s // tm

    def spec(shape, kind):
        nd = len(shape)
        if kind == "r":
            return pl.BlockSpec(shape[:-2] + (tm, shape[-1]), lambda i: (0,) * (nd - 2) + (i, 0))
        if kind in ("f", "a"):
            return pl.BlockSpec(shape, lambda i: (0,) * nd)
        if kind == "t":
            return pl.BlockSpec((shape[0], tm), lambda i: (0, i))
        if kind == "b":
            return pl.BlockSpec((None,) + shape[1:], lambda i: (i, 0, 0))
        if kind == "p":
            return pl.BlockSpec((HALO, shape[-1]), lambda i: (jnp.maximum(i * (tm // HALO) - 1, 0), 0))
        if kind == "n":
            return pl.BlockSpec((HALO, shape[-1]), lambda i: (jnp.minimum((i + 1) * (tm // HALO), s // HALO - 1), 0))
        j = kind[1]
        return pl.BlockSpec((tm, HEAD_PAD), lambda i: (i, j))

    def body(*refs):
        fn(pl.program_id(0), n, *refs)

    return pl.pallas_call(
        body, name=name, grid=(n,),
        in_specs=[spec(a.shape, kind) for a, kind in ins],
        out_specs=[spec(shape, kind) for shape, _, kind in outs],
        out_shape=[jax.ShapeDtypeStruct(shape, dtype) for shape, dtype, _ in outs],
        scratch_shapes=list(scratch), compiler_params=_params(1))(*[a for a, _ in ins])


def _acc(ref, i, val):
    @pl.when(i == 0)
    def _():
        ref[...] = val

    @pl.when(i > 0)
    def _():
        ref[...] += val


def _sum0(x):
    return jnp.sum(x, axis=0, keepdims=True)


def _rstd(x, n):
    return lax.rsqrt(jnp.sum(x * x, axis=-1, keepdims=True) * (1.0 / n) + EPS)


def _rms_bwd(x, g, dy, n):
    r = _rstd(x, n)
    xh = x * r
    dyg = dy * g
    dx = r * (dyg - xh * (jnp.sum(dyg * xh, axis=-1, keepdims=True) * (1.0 / n)))
    return dx, _sum0(dy * xh)


def _sigmoid(x):
    return 1.0 / (1.0 + jnp.exp(-x))


def _norm_fwd(x, g):
    s, d = x.shape

    def fn(i, n, x_ref, g_ref, o_ref, ot_ref):
        xv = x_ref[...]
        y = xv * _rstd(xv, d) * g_ref[...]
        o_ref[...] = y.astype(o_ref.dtype)
        ot_ref[...] = y.T.astype(ot_ref.dtype)

    return _rows("norm_fwd", fn, s, _tile(s, 256), [(x, "r"), (g, "f")],
                 [((s, d), MM_DTYPE, "r"), ((d, s), MM_DTYPE, "t")])


def _resid_norm(x, adds, scale, g, norm_dtype, transposed=False):
    s, d = x.shape
    k = len(adds)

    def fn(i, n, *refs):
        x_ref, add_refs, g_ref, xo_ref, no_ref = refs[0], refs[1:1 + k], refs[1 + k], refs[2 + k], refs[3 + k]
        tot = add_refs[0][...]
        for r in add_refs[1:]:
            tot = tot + r[...]
        xv = x_ref[...] + scale * tot
        xo_ref[...] = xv
        y = xv * _rstd(xv, d) * g_ref[...]
        no_ref[...] = y.astype(no_ref.dtype)
        if transposed:
            refs[4 + k][...] = y.T.astype(refs[4 + k].dtype)

    return _rows("resid_norm", fn, s, _tile(s, 256), [(x, "r")] + [(a, "r") for a in adds] + [(g, "f")],
                 [((s, d), F32, "r"), ((s, d), norm_dtype, "r")] + ([((d, s), MM_DTYPE, "t")] if transposed else []))


def _norm_bwd(x, g, dn_parts, dres, out_scale):
    s, d = x.shape
    k = len(dn_parts)
    has_res = dres is not None

    def fn(i, n, *refs):
        x_ref, g_ref = refs[0], refs[1]
        dn_refs = refs[2:2 + k]
        pos = 2 + k
        dn = dn_refs[0][...]
        for r in dn_refs[1:]:
            dn = dn + r[...]
        dx, dg = _rms_bwd(x_ref[...], g_ref[...], dn, d)
        if has_res:
            dx = dx + refs[pos][...]
            pos += 1
        dx_ref, dxs_ref, dg_ref = refs[pos:pos + 3]
        dx_ref[...] = dx
        dxs_ref[...] = (out_scale * dx).astype(dxs_ref.dtype)
        _acc(dg_ref, i, dg)

    ins = [(x, "r"), (g, "f")] + [(p, "r") for p in dn_parts] + ([(dres, "r")] if has_res else [])
    return _rows("norm_bwd", fn, s, _tile(s, 256), ins,
                 [((s, d), F32, "r"), ((s, d), MM_DTYPE, "r"), ((1, d), F32, "a")])


def _final_fwd(x, o, g):
    return _resid_norm(x, [o], 0.5, g, F32)


def _loss(y, t):
    s, d = y.shape

    def fn(i, n, y_ref, t_ref, dy_ref, l_ref):
        e = y_ref[...] - t_ref[...]
        dy_ref[...] = e * (1.0 / d)
        _acc(l_ref, i, (0.5 / d) * jnp.sum(jnp.sum(e * e, axis=-1, keepdims=True), axis=0, keepdims=True))

    return _rows("loss", fn, s, _tile(s, 256), [(y, "r"), (t, "r")], [((s, d), F32, "r"), ((1, 1), F32, "a")])


def _mla_pre(p, gql, gkvl, cw):
    s = p.shape[0]
    ql, kvl = gql.shape[-1], gkvl.shape[-1]
    o_q, o_kv = 2 * cw, 2 * cw + ql

    def fn(i, n, p_ref, gq_ref, gkv_ref, cq_ref, ckv_ref, ckvt_ref):
        cq = p_ref[:, o_q:o_q + ql]
        cq_ref[...] = (cq * _rstd(cq, ql) * gq_ref[...]).astype(cq_ref.dtype)
        ckv = p_ref[:, o_kv:o_kv + kvl]
        y = ckv * _rstd(ckv, kvl) * gkv_ref[...]
        ckv_ref[...] = y.astype(ckv_ref.dtype)
        ckvt_ref[...] = y.T.astype(ckvt_ref.dtype)

    return _rows("mla_pre", fn, s, _tile(s, 256), [(p, "r"), (gql, "f"), (gkvl, "f")],
                 [((s, ql), MM_DTYPE, "r"), ((s, kvl), MM_DTYPE, "r"), ((kvl, s), MM_DTYPE, "t")])


def _rope(x, cosf, s1, s2):
    return x * cosf + pltpu.roll(x, HEAD_PAD - 16, 1) * s1 + pltpu.roll(x, 16, 1) * s2


def _rope_bwd(d, cosf, s1, s2):
    return d * cosf + pltpu.roll(d * s1, 16, 1) + pltpu.roll(d * s2, HEAD_PAD - 16, 1)


def _qk_post(q_pre, k_pre, p, pe_block, tabs, gq, gk):
    s, hw = q_pre.shape
    cosf, s1, s2 = tabs

    tm = _tile(s, 256)

    def fn(i, n, q_ref, k_ref, pe_ref, c_ref, s1_ref, s2_ref, gq_ref, gk_ref, qo_ref, ko_ref, kt_ref):
        c, a1, a2, pe = c_ref[...], s1_ref[...], s2_ref[...], pe_ref[...]
        for hh in range(N_HEADS):
            cols = slice(hh * HEAD_PAD, (hh + 1) * HEAD_PAD)
            qv = q_ref[:, cols]
            qo_ref[:, cols] = _rope(qv * _rstd(qv, QK_DIM) * gq_ref[...], c, a1, a2).astype(qo_ref.dtype)
            kv = k_ref[:, cols] + pe
            kout = _rope(kv * _rstd(kv, QK_DIM) * gk_ref[...], c, a1, a2)
            ko_ref[:, cols] = kout.astype(ko_ref.dtype)
            kt_ref[cols, :] = kout.T.astype(kt_ref.dtype)

    return _rows("qk_post", fn, s, tm,
                 [(q_pre, "r"), (k_pre, "r"), (p, ("c", pe_block)), (cosf, "r"), (s1, "r"), (s2, "r"), (gq, "f"), (gk, "f")],
                 [((s, hw), MM_DTYPE, "r"), ((s, hw), MM_DTYPE, "r"), ((s // tm, hw, tm), MM_DTYPE, "b")])


def _qk_post_bwd(q_pre, k_pre, p, pe_block, tabs, gq, gk, dq, dk):
    s, hw = q_pre.shape
    cosf, s1, s2 = tabs

    def fn(i, n, q_ref, k_ref, pe_ref, c_ref, s1_ref, s2_ref, gq_ref, gk_ref, dq_ref, dk_ref,
           dqo_ref, dko_ref, dpe_ref, dgq_ref, dgk_ref):
        c, a1, a2, pe = c_ref[...], s1_ref[...], s2_ref[...], pe_ref[...]
        lane = lax.broadcasted_iota(jnp.int32, pe.shape, 1)
        is_pe = (lane >= QK_NOPE) & (lane < QK_DIM)
        dpe = jnp.zeros(pe.shape, F32)
        dgq = jnp.zeros((1, HEAD_PAD), F32)
        dgk = jnp.zeros((1, HEAD_PAD), F32)
        for hh in range(N_HEADS):
            cols = slice(hh * HEAD_PAD, (hh + 1) * HEAD_PAD)
            dxq, g1 = _rms_bwd(q_ref[:, cols], gq_ref[...], _rope_bwd(dq_ref[cols, :].T, c, a1, a2), QK_DIM)
            dqo_ref[:, cols] = dxq.astype(dqo_ref.dtype)
            dxk, g2 = _rms_bwd(k_ref[:, cols] + pe, gk_ref[...], _rope_bwd(dk_ref[:, cols], c, a1, a2), QK_DIM)
            dko_ref[:, cols] = dxk.astype(dko_ref.dtype)
            dpe = dpe + dxk
            dgq = dgq + g1
            dgk = dgk + g2
        dpe_ref[...] = jnp.where(is_pe, dpe, 0.0)
        _acc(dgq_ref, i, dgq)
        _acc(dgk_ref, i, dgk)

    return _rows("qk_post_bwd", fn, s, _tile(s, 256),
                 [(q_pre, "r"), (k_pre, "r"), (p, ("c", pe_block)), (cosf, "r"), (s1, "r"), (s2, "r"), (gq, "f"), (gk, "f"),
                  (dq, "b"), (dk, "r")],
                 [((s, hw), MM_DTYPE, "r"), ((s, hw), MM_DTYPE, "r"), ((s, HEAD_PAD), F32, "r"),
                  ((1, HEAD_PAD), F32, "a"), ((1, HEAD_PAD), F32, "a")])


def _chunk_mask(tq, key_rows):
    r = jnp.right_shift(lax.broadcasted_iota(jnp.int32, (tq, tq), 0), CHUNK_SHIFT)
    c = jnp.right_shift(lax.broadcasted_iota(jnp.int32, (tq, tq), 1), CHUNK_SHIFT)
    return (r <= c) if key_rows else (c <= r)


def _attn_specs(s, tq):
    w = HEADS_PER_STEP * HEAD_PAD
    tile = pl.BlockSpec((tq, w), lambda hb, i: (i, hb))
    whole = pl.BlockSpec((s, w), lambda hb, i: (0, hb))
    rows = pl.BlockSpec((None, None, HEADS_PER_STEP, tq), lambda hb, i: (hb, i, 0, 0))
    blocks = pl.BlockSpec((s // tq, w, tq), lambda hb, i: (0, hb, 0))
    return tile, whole, rows, blocks


def _head(hh):
    return slice(hh * HEAD_PAD, (hh + 1) * HEAD_PAD)


def _attn_fwd(q, k, vt):
    s, hw = q.shape
    tq = _tile(s, 256)
    nq = s // tq
    groups = N_HEADS // HEADS_PER_STEP
    scale = QK_DIM ** -0.5
    tile, whole, rows, blocks = _attn_specs(s, tq)

    def body(q_ref, k_ref, vt_ref, o_ref, lse_ref):
        qi = pl.program_id(1)
        qv = [q_ref[:, _head(hh)] for hh in range(HEADS_PER_STEP)]

        def scores(kb):
            off = pl.multiple_of(kb * tq, tq)
            return tuple(lax.dot_general(k_ref[pl.ds(off, tq), _head(hh)], qv[hh], _DIMS["nt"],
                                         preferred_element_type=F32) for hh in range(HEADS_PER_STEP))

        def update(kb, stats, raw, diag):
            out = []
            for hh in range(HEADS_PER_STEP):
                m, l, acc = stats[hh]
                sc = raw[hh] * (scale * LOG2E)
                if diag:
                    sc = jnp.where(_chunk_mask(tq, True), sc, -1e30)
                mn = jnp.maximum(m, jnp.max(sc, axis=0, keepdims=True))
                al = jnp.exp2(m - mn)
                pr = jnp.exp2(sc - mn)
                l = al * l + jnp.sum(pr, axis=0, keepdims=True)
                acc = al * acc + jnp.dot(vt_ref[kb, _head(hh), :], pr.astype(MM_DTYPE), preferred_element_type=F32)
                out.append((mn, l, acc))
            return tuple(out)

        def step(kb, carry):
            stats, raw = carry
            nxt = scores(kb + 1)
            return update(kb, stats, raw, False), nxt

        init = tuple((jnp.full((1, tq), -1e30, F32), jnp.zeros((1, tq), F32), jnp.zeros((HEAD_PAD, tq), F32))
                     for _ in range(HEADS_PER_STEP))
        stats, raw = lax.fori_loop(0, qi, step, (init, scores(0)))
        carry = update(qi, stats, raw, True)
        for hh in range(HEADS_PER_STEP):
            m, l, acc = carry[hh]
            o_ref[:, _head(hh)] = (acc / l).T
            lse_ref[hh:hh + 1, :] = m + jnp.log(l) * LOG2E

    return pl.pallas_call(
        body, name="attn_fwd", grid=(groups, nq),
        in_specs=[tile, whole, blocks], out_specs=[tile, rows],
        out_shape=[jax.ShapeDtypeStruct((s, hw), F32), jax.ShapeDtypeStruct((groups, nq, HEADS_PER_STEP, tq), F32)],
        compiler_params=_params(2))(q, k, vt)


def _attn_delta(o, do):
    s, hw = o.shape
    tq = _tile(s, 256)
    groups = N_HEADS // HEADS_PER_STEP
    tile, _, rows, _ = _attn_specs(s, tq)

    def body(o_ref, do_ref, d_ref):
        for hh in range(HEADS_PER_STEP):
            prod = o_ref[:, _head(hh)] * do_ref[:, _head(hh)]
            d_ref[hh:hh + 1, :] = jnp.sum(prod.T, axis=0, keepdims=True)

    return pl.pallas_call(
        body, name="attn_delta", grid=(groups, s // tq), in_specs=[tile, tile], out_specs=rows,
        out_shape=jax.ShapeDtypeStruct((groups, s // tq, HEADS_PER_STEP, tq), F32),
        compiler_params=_params(2))(o, do)


def _attn_bwd(q, k, kt, v, do, lse_row, delta_row):
    s, hw = q.shape
    tq = _tile(s, 256)
    nq = s // tq
    groups = N_HEADS // HEADS_PER_STEP
    w = HEADS_PER_STEP * HEAD_PAD
    scale = QK_DIM ** -0.5
    tile, whole, _, blocks = _attn_specs(s, tq)
    rowvec = pl.BlockSpec((None, nq, HEADS_PER_STEP, tq), lambda hb, i: (hb, 0, 0, 0))
    kt_tile = pl.BlockSpec((None, w, tq), lambda hb, i: (i, hb, 0))

    def body(q_ref, k_ref, kt_ref, v_ref, do_ref, lse_ref, delta_ref, dqt_ref, dk_ref, dv_ref):
        ki = pl.program_id(1)
        kv = [k_ref[:, _head(hh)] for hh in range(HEADS_PER_STEP)]
        vv = [v_ref[:, _head(hh)] for hh in range(HEADS_PER_STEP)]
        ktv = [kt_ref[_head(hh), :] for hh in range(HEADS_PER_STEP)]

        @pl.when(ki == 0)
        def _():
            dqt_ref[...] = jnp.zeros(dqt_ref.shape, F32)

        def block(qb, carry, diag):
            off = pl.multiple_of(qb * tq, tq)
            out = []
            for hh in range(HEADS_PER_STEP):
                dk, dvv = carry[hh]
                qs = q_ref[pl.ds(off, tq), _head(hh)]
                dos = do_ref[pl.ds(off, tq), _head(hh)].astype(MM_DTYPE)
                sc = lax.dot_general(kv[hh], qs, _DIMS["nt"], preferred_element_type=F32) * (scale * LOG2E)
                if diag:
                    sc = jnp.where(_chunk_mask(tq, True), sc, -1e30)
                pr = jnp.exp2(sc - lse_ref[qb, hh:hh + 1, :])
                dpr = lax.dot_general(vv[hh], dos, _DIMS["nt"], preferred_element_type=F32)
                ds = (pr * (dpr - delta_ref[qb, hh:hh + 1, :]) * scale).astype(MM_DTYPE)
                dvv = dvv + jnp.dot(pr.astype(MM_DTYPE), dos, preferred_element_type=F32)
                dk = dk + jnp.dot(ds, qs, preferred_element_type=F32)
                dqt_ref[qb, _head(hh), :] += jnp.dot(ktv[hh], ds, preferred_element_type=F32)
                out.append((dk, dvv))
            return tuple(out)

        init = tuple((jnp.zeros((tq, HEAD_PAD), F32), jnp.zeros((tq, HEAD_PAD), F32)) for _ in range(HEADS_PER_STEP))
        carry = lax.fori_loop(ki + 1, nq, lambda qb, c: block(qb, c, False), block(ki, init, True))
        for hh in range(HEADS_PER_STEP):
            dk_ref[:, _head(hh)] = carry[hh][0]
            dv_ref[:, _head(hh)] = carry[hh][1].astype(dv_ref.dtype)

    return pl.pallas_call(
        body, name="attn_bwd", grid=(groups, nq),
        in_specs=[whole, tile, kt_tile, tile, whole, rowvec, rowvec], out_specs=[blocks, tile, tile],
        out_shape=[jax.ShapeDtypeStruct((nq, hw, tq), F32), jax.ShapeDtypeStruct((s, hw), F32),
                   jax.ShapeDtypeStruct((s, hw), MM_DTYPE)],
        compiler_params=_params(2))(q, k, kt, v, do, lse_row, delta_row)


def _glu_with_halo(i, p_ref, ph_ref, ubuf, cw, tm):
    a, g = p_ref[:, 0:cw], p_ref[:, cw:2 * cw]
    ubuf[HALO:HALO + tm, :] = a * _sigmoid(g)
    ah, gh = ph_ref[:, 0:cw], ph_ref[:, cw:2 * cw]
    ubuf[0:HALO, :] = jnp.where(i > 0, ah * _sigmoid(gh), 0.0)
    return a, g


def _conv_taps(ubuf, w_ref, tm):
    base = HALO - (CONV_K - 1)
    y = w_ref[0:1, :] * ubuf[base:base + tm, :]
    for k in range(1, CONV_K):
        y = y + w_ref[k:k + 1, :] * ubuf[base + k:base + k + tm, :]
    return y


def _layer_norm_stats(y, cw):
    mu = jnp.sum(y, axis=-1, keepdims=True) * (1.0 / cw)
    yc = y - mu
    r = lax.rsqrt(jnp.sum(yc * yc, axis=-1, keepdims=True) * (1.0 / cw) + EPS)
    return yc * r, r


def _conv_fwd(p, w, cb, lg, lb):
    s = p.shape[0]
    cw = cb.shape[-1]
    tm = _tile(s, 256)

    def fn(i, n, p_ref, ph_ref, w_ref, cb_ref, lg_ref, lb_ref, o_ref, ubuf):
        _glu_with_halo(i, p_ref, ph_ref, ubuf, cw, tm)
        yh, _ = _layer_norm_stats(_conv_taps(ubuf, w_ref, tm) + cb_ref[...], cw)
        z = yh * lg_ref[...] + lb_ref[...]
        o_ref[...] = (z * _sigmoid(z)).astype(o_ref.dtype)

    return _rows("conv_fwd", fn, s, tm, [(p, "r"), (p, "p"), (w, "f"), (cb, "f"), (lg, "f"), (lb, "f")],
                 [((s, cw), MM_DTYPE, "r")], scratch=[pltpu.VMEM((tm + HALO, cw), F32)])[0]


def _conv_bwd_ln(p, w, cb, lg, lb, dc):
    s = p.shape[0]
    cw = cb.shape[-1]
    tm = _tile(s, 256)

    def fn(i, n, p_ref, ph_ref, w_ref, cb_ref, lg_ref, lb_ref, dc_ref, dy_ref, dlg_ref, dlb_ref, dcb_ref, ubuf):
        _glu_with_halo(i, p_ref, ph_ref, ubuf, cw, tm)
        yh, r = _layer_norm_stats(_conv_taps(ubuf, w_ref, tm) + cb_ref[...], cw)
        z = yh * lg_ref[...] + lb_ref[...]
        sg = _sigmoid(z)
        dz = dc_ref[...] * (sg * (1.0 + z * (1.0 - sg)))
        dyh = dz * lg_ref[...]
        m1 = jnp.sum(dyh, axis=-1, keepdims=True) * (1.0 / cw)
        m2 = jnp.sum(dyh * yh, axis=-1, keepdims=True) * (1.0 / cw)
        dy = r * (dyh - m1 - yh * m2)
        dy_ref[...] = dy
        _acc(dlg_ref, i, _sum0(dz * yh))
        _acc(dlb_ref, i, _sum0(dz))
        _acc(dcb_ref, i, _sum0(dy))

    return _rows("conv_bwd_ln", fn, s, tm,
                 [(p, "r"), (p, "p"), (w, "f"), (cb, "f"), (lg, "f"), (lb, "f"), (dc, "r")],
                 [((s, cw), F32, "r"), ((1, cw), F32, "a"), ((1, cw), F32, "a"), ((1, cw), F32, "a")],
                 scratch=[pltpu.VMEM((tm + HALO, cw), F32)])


def _conv_bwd_taps(p, w, dy):
    s = p.shape[0]
    cw = w.shape[-1]
    tm = _tile(s, 256)

    def fn(i, n, p_ref, ph_ref, w_ref, dy_ref, dyn_ref, dp_ref, dw_ref, ubuf, dybuf):
        a, g = _glu_with_halo(i, p_ref, ph_ref, ubuf, cw, tm)
        dyv = dy_ref[...]
        dybuf[0:tm, :] = dyv
        dybuf[tm:tm + HALO, :] = jnp.where(i < n - 1, dyn_ref[...], 0.0)
        base = HALO - (CONV_K - 1)
        du = jnp.zeros((tm, cw), F32)

        @pl.when(i == 0)
        def _():
            dw_ref[...] = jnp.zeros(dw_ref.shape, F32)

        for k in range(CONV_K):
            sh = CONV_K - 1 - k
            du = du + w_ref[k:k + 1, :] * dybuf[sh:sh + tm, :]
            dw_ref[k:k + 1, :] += _sum0(dyv * ubuf[base + k:base + k + tm, :])
        sg = _sigmoid(g)
        dp_ref[:, 0:cw] = (du * sg).astype(dp_ref.dtype)
        dp_ref[:, cw:2 * cw] = (du * a * sg * (1.0 - sg)).astype(dp_ref.dtype)

    return _rows("conv_bwd_taps", fn, s, tm, [(p, "r"), (p, "p"), (w, "f"), (dy, "r"), (dy, "n")],
                 [((s, 2 * cw), MM_DTYPE, "r"), ((CONV_K + 1, cw), F32, "a")],
                 scratch=[pltpu.VMEM((tm + HALO, cw), F32), pltpu.VMEM((tm + HALO, cw), F32)])


def _dp_assemble(p, gql, gkvl, dcq, dckv_parts, dp_conv, dpe, cw):
    s, width = p.shape
    ql, kvl = gql.shape[-1], gkvl.shape[-1]
    o_q, o_kv, o_pe = 2 * cw, 2 * cw + ql, 2 * cw + ql + kvl

    def fn(i, n, p_ref, gq_ref, gkv_ref, dcq_ref, dk1_ref, dk2_ref, dconv_ref, dpe_ref, dp_ref, dgq_ref, dgkv_ref):
        dp_ref[:, 0:o_q] = dconv_ref[...]
        dx, dg = _rms_bwd(p_ref[:, o_q:o_kv], gq_ref[...], dcq_ref[...], ql)
        dp_ref[:, o_q:o_kv] = dx.astype(dp_ref.dtype)
        _acc(dgq_ref, i, dg)
        dx, dg = _rms_bwd(p_ref[:, o_kv:o_pe], gkv_ref[...], dk1_ref[...] + dk2_ref[...], kvl)
        dp_ref[:, o_kv:o_pe] = dx.astype(dp_ref.dtype)
        _acc(dgkv_ref, i, dg)
        dp_ref[:, o_pe:width] = dpe_ref[...].astype(dp_ref.dtype)

    return _rows("dp_assemble", fn, s, _tile(s, 256),
                 [(p, "r"), (gql, "f"), (gkvl, "f"), (dcq, "r"), (dckv_parts[0], "r"), (dckv_parts[1], "r"),
                  (dp_conv, "r"), (dpe, "r")],
                 [((s, width), MM_DTYPE, "r"), ((1, ql), F32, "a"), ((1, kvl), F32, "a")])


def _add_leading(name, x, out_dtype):
    n, r, c = x.shape
    tm = _tile(r, 256)

    def fn(i, nt, x_ref, o_ref):
        tot = x_ref[0].astype(F32)
        for k in range(1, n):
            tot = tot + x_ref[k].astype(F32)
        o_ref[...] = tot.astype(o_ref.dtype)

    return _rows(name, fn, r, tm, [(x, "r")], [((r, c), out_dtype, "r")])[0]


def _prefetch_call(name, fn, idx, grid, in_specs, out_spec, out_shape, args, aliases=None):
    return pl.pallas_call(
        fn, name=name, out_shape=out_shape,
        grid_spec=pltpu.PrefetchScalarGridSpec(num_scalar_prefetch=1, grid=grid, in_specs=in_specs, out_specs=out_spec),
        input_output_aliases=aliases or {}, compiler_params=_params(len(grid)))(idx, *args)


def _add_sibling(idx, mine, theirs):
    _, r, c = mine.shape
    tm = _tile(r, 256)

    def fn(idx_ref, a_ref, b_ref, o_ref):
        o_ref[...] = (a_ref[...].astype(F32) + b_ref[...].astype(F32)).astype(o_ref.dtype)

    return _prefetch_call(
        "add_sibling", fn, idx, (r // tm,),
        [pl.BlockSpec((None, tm, c), lambda i, ix: (ix[0], i, 0)), pl.BlockSpec((tm, c), lambda i, ix: (i, 0))],
        pl.BlockSpec((tm, c), lambda i, ix: (i, 0)), jax.ShapeDtypeStruct((r, c), WIRE_DTYPE), [mine, theirs])


def _add_chips(idx, own, others):
    _, r, c = own.shape
    tm = _tile(r, 256)

    def fn(idx_ref, a_ref, b_ref, o_ref):
        tot = a_ref[...].astype(F32)
        for k in range(3):
            tot = tot + b_ref[k].astype(F32)
        o_ref[...] = tot

    return _prefetch_call(
        "add_chips", fn, idx, (r // tm,),
        [pl.BlockSpec((None, tm, c), lambda i, ix: (ix[1], i, 0)), pl.BlockSpec((3, tm, c), lambda i, ix: (0, i, 0))],
        pl.BlockSpec((None, tm, c), lambda i, ix: (ix[0], i, 0)), jax.ShapeDtypeStruct((2, r, c), F32), [own, others])


def _adamw(w, g, m, v):
    r, c = w.shape
    tm = _tile(r, 256)

    def fn(i, n, w_ref, g_ref, m_ref, v_ref, d_ref, mo_ref, vo_ref):
        gv = g_ref[...]
        mn = ADAM_B1 * m_ref[...] + (1.0 - ADAM_B1) * gv
        vn = ADAM_B2 * v_ref[...] + (1.0 - ADAM_B2) * (gv * gv)
        mo_ref[...] = mn
        vo_ref[...] = vn
        m_hat = mn / (1.0 - ADAM_B1 ** ADAM_STEP)
        v_hat = vn / (1.0 - ADAM_B2 ** ADAM_STEP)
        d_ref[...] = -ADAM_LR * (m_hat / (jnp.sqrt(v_hat) + ADAM_EPS) + ADAM_WD * w_ref[...])

    return _rows("adamw", fn, r, tm, [(w, "r"), (g, "r"), (m, "r"), (v, "r")], [((r, c), F32, "r")] * 3)


def _place():
    x, y, c = lax.axis_index("x"), lax.axis_index("y"), lax.axis_index("c")
    other_chips = [(1 - x, y), (x, 1 - y), (1 - x, 1 - y)]
    return x, y, c, 2 * x + y, other_chips


def _gather_weights(bufs, shards):
    k_n = len(shards)

    def body(*refs):
        src, dst = refs[k_n:2 * k_n], refs[2 * k_n:3 * k_n]
        send, recv, fsend, frecv = refs[3 * k_n:]
        x, y, c, me, chips = _place()

        def direct(k, j):
            return pltpu.make_async_remote_copy(
                src_ref=src[k].at[c], dst_ref=dst[k].at[c, me], send_sem=send.at[k, j], recv_sem=recv.at[k, j],
                device_id=(chips[j][0], chips[j][1], c), device_id_type=MESH)

        def landed(k, j, layer, sems):
            cx, cy = chips[j]
            blk = dst[k].at[layer, 2 * cx + cy]
            return pltpu.make_async_remote_copy(
                src_ref=blk, dst_ref=blk, send_sem=sems[0].at[k, j], recv_sem=sems[1].at[k, j],
                device_id=(x, y, 1 - c), device_id_type=MESH)

        sent = [direct(k, j) for j in range(3) for k in range(k_n)]
        for cp in sent:
            cp.start()
        passed = []
        for j in range(3):
            for k in range(k_n):
                landed(k, j, c, (send, recv)).wait_recv()
                fwd = landed(k, j, c, (fsend, frecv))
                fwd.start()
                passed.append(fwd)
        for j in range(3):
            for k in range(k_n):
                landed(k, j, 1 - c, (fsend, frecv)).wait_recv()
        for cp in sent + passed:
            cp.wait_send()

    dma = lambda: pltpu.SemaphoreType.DMA((k_n, 3))
    return pl.pallas_call(
        body, name="gather_weights", in_specs=[ANY] * (2 * k_n), out_specs=[ANY] * k_n,
        out_shape=[jax.ShapeDtypeStruct(b.shape, b.dtype) for b in bufs],
        input_output_aliases={k: k for k in range(k_n)},
        scratch_shapes=[dma(), dma(), dma(), dma()],
        compiler_params=pltpu.CompilerParams(has_side_effects=True))(*bufs, *shards)


def _swap_layers(grads):
    k_n = len(grads)

    def body(*refs):
        src, dst = refs[:k_n], refs[k_n:2 * k_n]
        send, recv = refs[2 * k_n:]
        x, y, c, me, chips = _place()
        away = [pltpu.make_async_remote_copy(
            src_ref=src[k].at[1 - c], dst_ref=dst[k], send_sem=send.at[k], recv_sem=recv.at[k],
            device_id=(x, y, 1 - c), device_id_type=MESH) for k in range(k_n)]
        for cp in away:
            cp.start()
        for cp in away:
            cp.wait()

    dma = lambda: pltpu.SemaphoreType.DMA((k_n,))
    return pl.pallas_call(
        body, name="swap_layers", in_specs=[ANY] * k_n, out_specs=[ANY] * k_n,
        out_shape=[jax.ShapeDtypeStruct(g.shape[1:], g.dtype) for g in grads],
        scratch_shapes=[dma(), dma()],
        compiler_params=pltpu.CompilerParams(has_side_effects=True))(*grads)


def _scatter_to_owners(parts):
    k_n = len(parts)

    def body(*refs):
        src, dst = refs[:k_n], refs[k_n:2 * k_n]
        send, recv = refs[2 * k_n:]
        x, y, c, me, chips = _place()
        away = [pltpu.make_async_remote_copy(
            src_ref=src[k].at[2 * chips[j][0] + chips[j][1]], dst_ref=dst[k].at[j],
            send_sem=send.at[k, j], recv_sem=recv.at[k, j],
            device_id=(chips[j][0], chips[j][1], c), device_id_type=MESH) for j in range(3) for k in range(k_n)]
        for cp in away:
            cp.start()
        for cp in away:
            cp.wait()

    dma = lambda: pltpu.SemaphoreType.DMA((k_n, 3))
    return pl.pallas_call(
        body, name="scatter_to_owners", in_specs=[ANY] * k_n, out_specs=[ANY] * k_n,
        out_shape=[jax.ShapeDtypeStruct((3,) + p.shape[1:], p.dtype) for p in parts],
        scratch_shapes=[dma(), dma()],
        compiler_params=pltpu.CompilerParams(has_side_effects=True))(*parts)


def _share_layers(sums):
    k_n = len(sums)

    def body(*refs):
        dst = refs[k_n:2 * k_n]
        send, recv = refs[2 * k_n:]
        x, y, c, me, chips = _place()
        away = [pltpu.make_async_remote_copy(
            src_ref=dst[k].at[c], dst_ref=dst[k].at[c], send_sem=send.at[k], recv_sem=recv.at[k],
            device_id=(x, y, 1 - c), device_id_type=MESH) for k in range(k_n)]
        for cp in away:
            cp.start()
        for k in range(k_n):
            blk = dst[k].at[1 - c]
            pltpu.make_async_remote_copy(
                src_ref=blk, dst_ref=blk, send_sem=send.at[k], recv_sem=recv.at[k],
                device_id=(x, y, 1 - c), device_id_type=MESH).wait_recv()
        for cp in away:
            cp.wait_send()

    dma = lambda: pltpu.SemaphoreType.DMA((k_n,))
    return pl.pallas_call(
        body, name="share_layers", in_specs=[ANY] * k_n, out_specs=[ANY] * k_n,
        out_shape=[jax.ShapeDtypeStruct(s.shape, s.dtype) for s in sums],
        input_output_aliases={k: k for k in range(k_n)},
        scratch_shapes=[dma(), dma()],
        compiler_params=pltpu.CompilerParams(has_side_effects=True))(*sums)


def _gather_small(vec):
    flips = [(fx, fy, fc) for fx in (0, 1) for fy in (0, 1) for fc in (0, 1)][1:]

    def body(src, dst, send, recv, lsem):
        x, y, c, _, _ = _place()
        me = 4 * x + 2 * y + c
        local = pltpu.make_async_copy(src, dst.at[me], lsem)
        local.start()
        away = []
        for j, (fx, fy, fc) in enumerate(flips):
            px, py, pc = x ^ fx, y ^ fy, c ^ fc
            away.append(pltpu.make_async_remote_copy(
                src_ref=src, dst_ref=dst.at[me], send_sem=send.at[j], recv_sem=recv.at[j],
                device_id=(px, py, pc), device_id_type=MESH))
        for cp in away:
            cp.start()
        for j, (fx, fy, fc) in enumerate(flips):
            blk = dst.at[4 * (x ^ fx) + 2 * (y ^ fy) + (c ^ fc)]
            pltpu.make_async_remote_copy(
                src_ref=blk, dst_ref=blk, send_sem=send.at[j], recv_sem=recv.at[j],
                device_id=(x ^ fx, y ^ fy, c ^ fc), device_id_type=MESH).wait_recv()
        for cp in away:
            cp.wait_send()
        local.wait()

    return pl.pallas_call(
        body, name="gather_small", in_specs=[ANY], out_specs=ANY,
        out_shape=jax.ShapeDtypeStruct((8,) + vec.shape, vec.dtype),
        scratch_shapes=[pltpu.SemaphoreType.DMA((7,)), pltpu.SemaphoreType.DMA((7,)), pltpu.SemaphoreType.DMA(())],
        compiler_params=pltpu.CompilerParams(has_side_effects=True))(vec)


SHARDED = ["ffn1_w_gate", "ffn1_w_up", "ffn1_w_down", "w_in", "w_uq", "w_ukv", "conv_w", "w_out",
           "ffn2_w_gate", "ffn2_w_up", "ffn2_w_down"]
SMALL = ["ffn1_norm", "mix_norm", "q_latent_norm", "kv_latent_norm", "q_norm", "k_norm", "conv_b", "conv_ln_g",
         "conv_ln_b", "ffn2_norm", "post_norm"]
ALL_WEIGHTS = ["ffn1_norm", "ffn1_w_gate", "ffn1_w_up", "ffn1_w_down", "mix_norm", "w_in", "q_latent_norm", "w_uq",
               "kv_latent_norm", "w_ukv", "q_norm", "k_norm", "conv_w", "conv_b", "conv_ln_g", "conv_ln_b", "w_out",
               "ffn2_norm", "ffn2_w_gate", "ffn2_w_up", "ffn2_w_down", "post_norm"]


def _rope_tables(s):
    pos = jnp.arange(s, dtype=F32)
    inv_freq = 1.0 / (ROPE_THETA ** (jnp.arange(0, QK_ROPE, 2, dtype=F32) / QK_ROPE))
    ang = pos[:, None] * inv_freq[None, :]
    cos, sin = jnp.cos(ang), jnp.sin(ang)
    half = QK_ROPE // 2
    one = jnp.ones((s, QK_NOPE), F32)
    zero = jnp.zeros((s, QK_NOPE), F32)
    pad = HEAD_PAD - QK_DIM
    z16 = jnp.zeros((s, half), F32)
    cosf = jnp.concatenate([one, cos, cos, jnp.ones((s, pad), F32)], axis=1)
    s1 = jnp.concatenate([zero, -sin, z16, jnp.zeros((s, pad), F32)], axis=1)
    s2 = jnp.concatenate([zero, z16, sin, jnp.zeros((s, pad), F32)], axis=1)
    return cosf, s1, s2


def _ffn_fwd(n, w, names, l):
    a, b, hm = _ffn_up_fused(n, w[names[0]], w[names[1]], l)
    return a, b, hm, _blocked_sum("ffn_down", [hm], [w[names[2]]], l, "nn")


def _ffn_bwd(nt, a, b, hm, do, w, names, l, grads):
    gate, up, down = names
    layers = w[gate].shape[0]
    da, db = _ffn_down_t_fused(do, w[down], a, b, l)
    grads[down] = _wgrad("ffn_down_wgrad", hm, do, WIRE_DTYPE, layers, l, grads.get(down))
    grads[gate], grads[up] = _wgrad_pair("ffn_up_wgrad", nt, da, db, layers, l,
                                         (grads[gate], grads[up]) if gate in grads else None)
    return [_blocked_sum("ffn_up_t", [da, db], [w[gate], w[up]], l, "nt")]


def _vec(v, l):
    return v[l][None, :]


def _pad_lanes(v, width):
    return jnp.pad(v, [(0, 0)] * (v.ndim - 1) + [(0, width - v.shape[-1])])


def kernel(x, ffn1_norm, ffn1_w_gate, ffn1_w_up, ffn1_w_down, mix_norm, w_in, q_latent_norm, w_uq, kv_latent_norm, w_ukv, q_norm, k_norm, conv_w, conv_b, conv_ln_g, conv_ln_b, w_out, ffn2_norm, ffn2_w_gate, ffn2_w_up, ffn2_w_down, post_norm, loss_target, m_ffn1_norm, m_ffn1_w_gate, m_ffn1_w_up, m_ffn1_w_down, m_mix_norm, m_w_in, m_q_latent_norm, m_w_uq, m_kv_latent_norm, m_w_ukv, m_q_norm, m_k_norm, m_conv_w, m_conv_b, m_conv_ln_g, m_conv_ln_b, m_w_out, m_ffn2_norm, m_ffn2_w_gate, m_ffn2_w_up, m_ffn2_w_down, m_post_norm, v_ffn1_norm, v_ffn1_w_gate, v_ffn1_w_up, v_ffn1_w_down, v_mix_norm, v_w_in, v_q_latent_norm, v_w_uq, v_kv_latent_norm, v_w_ukv, v_q_norm, v_k_norm, v_conv_w, v_conv_b, v_conv_ln_g, v_conv_ln_b, v_w_out, v_ffn2_norm, v_ffn2_w_gate, v_ffn2_w_up, v_ffn2_w_down, v_post_norm):
    given = dict(locals())
    wts = {n: given[n] for n in ALL_WEIGHTS}
    mom = {n: given["m_" + n] for n in ALL_WEIGHTS}
    var = {n: given["v_" + n] for n in ALL_WEIGHTS}
    n_layers = ffn1_norm.shape[0]
    s, d = x.shape[1], x.shape[2]
    ql, kvl, cw = q_latent_norm.shape[-1], kv_latent_norm.shape[-1], conv_b.shape[-1]
    heads_per_chip = N_HEADS // N_CHIPS
    xs = x.reshape(s, d)
    target = loss_target.reshape(s, d)

    assert n_layers == 2, "core c of a chip finishes the gradients of layer c"
    local = [jnp.pad(wts[n], ((0, 0), (0, 1), (0, 0))).astype(MM_DTYPE) if n == "conv_w" else wts[n].astype(MM_DTYPE)
             for n in SHARDED]
    L = n_layers
    my_c = lax.axis_index("c")
    my_chip = 2 * lax.axis_index("x") + lax.axis_index("y")
    place = jnp.stack([my_c, my_chip]).astype(jnp.int32)
    bufs = [lax.dynamic_update_slice(lax.empty((L, N_CHIPS) + t.shape[1:], t.dtype), t[:, None], (0, my_chip, 0, 0))
            for t in local]
    w = dict(zip(SHARDED, _gather_weights(bufs, local)))
    hw = N_HEADS * HEAD_PAD
    win = w["w_in"].transpose(0, 2, 1, 3).reshape(L, d, -1)
    zc = lambda n: jnp.zeros((L, d, n), MM_DTYPE)
    o_pe, o_cv = ql + kvl, ql + kvl + QK_ROPE
    w["w_in"] = jnp.concatenate([win[..., o_cv:], win[..., :o_pe], zc(QK_NOPE), win[..., o_pe:o_cv],
                                 zc(HEAD_PAD - QK_DIM)], axis=-1)[:, None]
    wq = w["w_uq"].transpose(0, 2, 1, 3).reshape(L, ql, N_HEADS, QK_DIM)
    w["wq"] = _pad_lanes(wq, HEAD_PAD).reshape(L, 1, ql, hw)
    wkv = w["w_ukv"].transpose(0, 2, 1, 3).reshape(L, kvl, N_HEADS, QK_NOPE + V_DIM)
    w["wk"] = _pad_lanes(wkv[..., :QK_NOPE], HEAD_PAD).reshape(L, 1, kvl, hw)
    w["wv"] = _pad_lanes(wkv[..., QK_NOPE:], HEAD_PAD).reshape(L, 1, kvl, hw)
    wvt = w["wv"][:, 0].transpose(0, 2, 1)
    conv_full = w["conv_w"].transpose(0, 2, 1, 3).reshape(L, CONV_K + 1, cw).astype(F32)
    wout = w["w_out"].reshape(L, d, d)
    wo_a = wout[:, :N_HEADS * V_DIM].reshape(L, N_HEADS, V_DIM, d)
    w["wo_a"] = jnp.pad(wo_a, ((0, 0), (0, 0), (0, HEAD_PAD - V_DIM), (0, 0))).reshape(L, 1, hw, d)
    w["wo_c"] = wout[:, N_HEADS * V_DIM:][:, None]
    tabs = _rope_tables(s)
    pe_block = (2 * cw + ql + kvl) // HEAD_PAD
    gq_pad, gk_pad = _pad_lanes(q_norm, HEAD_PAD), _pad_lanes(k_norm, HEAD_PAD)
    tq = _tile(s, 256)
    groups = N_HEADS // HEADS_PER_STEP

    saved = []
    cur = xs
    for l in range(L):
        sv = {"x": cur}
        n1, sv["n1t"] = _norm_fwd(cur, _vec(ffn1_norm, l))
        sv["a1"], sv["b1"], sv["hm1"], o1 = _ffn_fwd(n1, w, SHARDED[0:3], l)
        sv["x1"], sv["h"] = _resid_norm(cur, [o1], 0.5, _vec(mix_norm, l), MM_DTYPE)
        sv["p"] = _proj("w_in", sv["h"], w["w_in"], l, "nn")[0]
        sv["cq"], sv["ckv"], ckvt = _mla_pre(sv["p"], _vec(q_latent_norm, l), _vec(kv_latent_norm, l), cw)
        sv["q_pre"] = _proj("wq", sv["cq"], w["wq"], l, "nn")[0]
        sv["k_pre"] = _proj("wk", sv["ckv"], w["wk"], l, "nn")[0]
        sv["v"] = _proj("wv", sv["ckv"], w["wv"], l, "nn", MM_DTYPE)[0]
        vt = _mm("wv_transposed", wvt, ckvt, "nn", (s // tq,),
                 pl.BlockSpec((None, hw, kvl), lambda i, l=l: (l, 0, 0)), pl.BlockSpec((kvl, tq), lambda i: (0, i)),
                 pl.BlockSpec((None, hw, tq), lambda i: (i, 0, 0)),
                 jax.ShapeDtypeStruct((s // tq, hw, tq), MM_DTYPE), 1, None)
        sv["q"], sv["k"], sv["kt"] = _qk_post(sv["q_pre"], sv["k_pre"], sv["p"], pe_block, tabs,
                                              _vec(gq_pad, l), _vec(gk_pad, l))
        sv["o"], sv["lse"] = _attn_fwd(sv["q"], sv["k"], vt)
        sv["c"] = _conv_fwd(sv["p"], conv_full[l], _vec(conv_b, l), _vec(conv_ln_g, l), _vec(conv_ln_b, l))
        mix_a = _contract("wo_a", sv["o"][None], w["wo_a"], l, "nn")
        mix_c = _contract("wo_c", sv["c"][None], w["wo_c"], l, "nn")
        sv["x2"], n2, sv["n2t"] = _resid_norm(sv["x1"], [mix_a, mix_c], 1.0, _vec(ffn2_norm, l), MM_DTYPE, True)
        sv["a2"], sv["b2"], sv["hm2"], o2 = _ffn_fwd(n2, w, SHARDED[8:11], l)
        sv["x3"], cur = _final_fwd(sv["x2"], o2, _vec(post_norm, l))
        saved.append(sv)
    dy, loss_part = _loss(cur, target)

    grads = {}
    small = {n: [None] * L for n in SMALL}
    stacked = {}
    for l in reversed(range(L)):
        sv = saved[l]
        dx3, do2, small["post_norm"][l] = _norm_bwd(sv["x3"], _vec(post_norm, l), [dy], None, 0.5)
        dn2 = _ffn_bwd(sv["n2t"], sv["a2"], sv["b2"], sv["hm2"], do2, w, SHARDED[8:11], l, grads)
        dx2, dx2b, small["ffn2_norm"][l] = _norm_bwd(sv["x2"], _vec(ffn2_norm, l), dn2, dx3, 1.0)
        do = _proj("wo_a_t", dx2b, w["wo_a"], l, "nt")[0]
        dc = _proj("wo_c_t", dx2b, w["wo_c"], l, "nt")[0]
        g_wo_a = _wgrad("wo_a_wgrad", sv["o"], dx2b[None], WIRE_DTYPE)[0].reshape(N_HEADS, HEAD_PAD, d)[:, :V_DIM]
        g_wo_c = _wgrad("wo_c_wgrad", sv["c"], dx2b[None], WIRE_DTYPE)
        stacked.setdefault("w_out", [None] * L)[l] = jnp.concatenate(
            [g_wo_a.reshape(N_HEADS * V_DIM, d), g_wo_c[0]], axis=0).reshape(N_CHIPS, d // N_CHIPS, d)
        dq, dk, dv = _attn_bwd(sv["q"], sv["k"], sv["kt"], sv["v"], do, sv["lse"], _attn_delta(sv["o"], do))
        dq_pre, dk_pre, dpe, small["q_norm"][l], small["k_norm"][l] = _qk_post_bwd(
            sv["q_pre"], sv["k_pre"], sv["p"], pe_block, tabs, _vec(gq_pad, l), _vec(gk_pad, l), dq, dk)
        dcq = _contract("wq_t", dq_pre[None], w["wq"], l, "nt")
        dckv = [_contract("wk_t", dk_pre[None], w["wk"], l, "nt"), _contract("wv_t", dv[None], w["wv"], l, "nt")]
        per_head = lambda t, rows: t[0].reshape(rows, N_HEADS, HEAD_PAD)
        g_wq = per_head(_wgrad("wq_wgrad", sv["cq"], dq_pre[None], WIRE_DTYPE), ql)[..., :QK_DIM]
        stacked.setdefault("w_uq", [None] * L)[l] = g_wq.reshape(ql, N_CHIPS, -1).transpose(1, 0, 2)
        g_wkv = jnp.concatenate([per_head(_wgrad("wk_wgrad", sv["ckv"], dk_pre[None], WIRE_DTYPE), kvl)[..., :QK_NOPE],
                                 per_head(_wgrad("wv_wgrad", sv["ckv"], dv[None], WIRE_DTYPE), kvl)[..., :V_DIM]], axis=-1)
        stacked.setdefault("w_ukv", [None] * L)[l] = g_wkv.reshape(kvl, N_CHIPS, -1).transpose(1, 0, 2)
        dyc, small["conv_ln_g"][l], small["conv_ln_b"][l], small["conv_b"][l] = _conv_bwd_ln(
            sv["p"], conv_full[l], _vec(conv_b, l), _vec(conv_ln_g, l), _vec(conv_ln_b, l), dc)
        dp_conv, g_conv = _conv_bwd_taps(sv["p"], conv_full[l], dyc)
        stacked.setdefault("conv_w", [None] * L)[l] = g_conv.reshape(CONV_K + 1, N_CHIPS, cw // N_CHIPS).transpose(
            1, 0, 2).astype(WIRE_DTYPE)
        dp, small["q_latent_norm"][l], small["kv_latent_norm"][l] = _dp_assemble(
            sv["p"], _vec(q_latent_norm, l), _vec(kv_latent_norm, l), dcq, dckv, dp_conv, dpe, cw)
        dh = _contract("w_in_t", dp[None], w["w_in"], l, "nt")
        g_win = _wgrad("w_in_wgrad", sv["h"], dp[None], WIRE_DTYPE)[0]
        o_q, o_pe_col = 2 * cw, 2 * cw + ql + kvl + QK_NOPE
        g_win = jnp.concatenate([g_win[:, o_q:o_q + ql + kvl], g_win[:, o_pe_col:o_pe_col + QK_ROPE], g_win[:, :o_q]], axis=1)
        stacked.setdefault("w_in", [None] * L)[l] = g_win.reshape(d, N_CHIPS, -1).transpose(1, 0, 2)
        dx1, do1, small["mix_norm"][l] = _norm_bwd(sv["x1"], _vec(mix_norm, l), [dh], dx2, 0.5)
        dn1 = _ffn_bwd(sv["n1t"], sv["a1"], sv["b1"], sv["hm1"], do1, w, SHARDED[0:3], l, grads)
        dy, _, small["ffn1_norm"][l] = _norm_bwd(sv["x"], _vec(ffn1_norm, l), dn1, dx1, 1.0)
    grad_x = dy.reshape(x.shape)
    for n, per_layer in stacked.items():
        grads[n] = jnp.stack(per_layer, axis=0)

    part_sums = [grads[n] for n in SHARDED]
    swapped = _swap_layers(part_sums)
    chip_sums = [_add_sibling(place, g.reshape(2, -1, g.shape[-1]), t.reshape(-1, t.shape[-1])).reshape(t.shape)
                 for g, t in zip(part_sums, swapped)]
    from_chips = _scatter_to_owners(chip_sums)
    layer_sums = [_add_chips(place, own, others) for own, others in zip(chip_sums, from_chips)]
    full = dict(zip(SHARDED, _share_layers(layer_sums)))
    full["conv_w"] = full["conv_w"][:, :CONV_K]

    width = max(wts[n].shape[-1] for n in SMALL)
    pack = lambda vals: jnp.concatenate([_pad_lanes(v.reshape(-1, v.shape[-1]), width) for v in vals], axis=0)
    part = pack([jnp.concatenate(small[n], axis=0)[..., :wts[n].shape[-1]] for n in SMALL])
    rows = part.shape[0]
    part = jnp.pad(part, [(0, -rows % 8), (0, 0)])
    small_sum = _add_leading("add_devices", _gather_small(part), F32)
    for i, n in enumerate(SMALL):
        full[n] = small_sum[i * L:(i + 1) * L, :wts[n].shape[-1]]

    delta_w, new_m, new_v = {}, {}, {}
    for n in SHARDED:
        flat = lambda t: t.reshape(-1, t.shape[-1])
        r = flat(wts[n]).shape[0]
        fix = (lambda t: jnp.pad(flat(t), [(0, -r % 8), (0, 0)])) if r % 8 else flat
        outs = _adamw(fix(wts[n]), fix(full[n]), fix(mom[n]), fix(var[n]))
        delta_w[n], new_m[n], new_v[n] = [o[:r].reshape(wts[n].shape) for o in outs]
    pk = lambda src: jnp.pad(pack([src[n] for n in SMALL]), [(0, -rows % 8), (0, 0)])
    outs = _adamw(pk(wts), small_sum, pk(mom), pk(var))
    for i, n in enumerate(SMALL):
        delta_w[n], new_m[n], new_v[n] = [o[i * L:(i + 1) * L, :wts[n].shape[-1]] for o in outs]

    loss = lax.psum(loss_part[0, 0], ("x", "y", "c"))
    return (loss, grad_x, *[full[n] for n in ALL_WEIGHTS], *[delta_w[n] for n in ALL_WEIGHTS],
            *[new_m[n] for n in ALL_WEIGHTS], *[new_v[n] for n in ALL_WEIGHTS])
```

```python
import jax
import jax.numpy as jnp
from jax import lax
from jax.experimental import pallas as pl
from jax.experimental.pallas import tpu as pltpu

F32 = jnp.float32
MM_DTYPE = jnp.bfloat16
WIRE_DTYPE = jnp.bfloat16
EPS = 1e-6
N_HEADS = 8
QK_NOPE = 64
QK_ROPE = 32
QK_DIM = QK_NOPE + QK_ROPE
V_DIM = 64
HEAD_PAD = 128
HEADS_PER_STEP = 2
CHUNK_SHIFT = 6
LOG2E = 1.4426950408889634
CONV_K = 31
HALO = 32
ROPE_THETA = 10000.0
N_CHIPS = 4
ADAM_LR, ADAM_B1, ADAM_B2, ADAM_EPS, ADAM_WD, ADAM_STEP = 0.001, 0.9, 0.999, 1e-08, 0.01, 10
VMEM_LIMIT_BYTES = 56 * 2 ** 20
MESH = pl.DeviceIdType.MESH
ANY = pl.BlockSpec(memory_space=pl.ANY)

_DIMS = {
    "nn": (((1,), (0,)), ((), ())),
    "nt": (((1,), (1,)), ((), ())),
    "tn": (((0,), (0,)), ((), ())),
}


def _params(n_axes):
    return pltpu.CompilerParams(dimension_semantics=("arbitrary",) * n_axes, vmem_limit_bytes=VMEM_LIMIT_BYTES)


def _tile(n, cap):
    t = cap
    while n % t:
        t //= 2
    return t


def _mm(name, a, b, dims, grid, a_spec, b_spec, o_spec, out_shape, nk, acc_shape, into=None):
    n_axes = len(grid)

    def body(*refs):
        if into is not None:
            refs = refs[1:]
        a_ref, b_ref, o_ref = refs[:3]
        part = lax.dot_general(a_ref[...].astype(MM_DTYPE), b_ref[...].astype(MM_DTYPE), _DIMS[dims],
                               preferred_element_type=F32)
        if nk == 1:
            o_ref[...] = part.astype(o_ref.dtype)
        else:
            acc_ref = refs[3]
            k = pl.program_id(n_axes - 1)

            @pl.when(k == 0)
            def _():
                acc_ref[...] = part

            @pl.when(k > 0)
            def _():
                acc_ref[...] += part

            @pl.when(k == nk - 1)
            def _():
                o_ref[...] = acc_ref[...].astype(o_ref.dtype)

    in_specs = [a_spec, b_spec]
    args = [a, b]
    aliases = {}
    if into is not None:
        in_specs = [ANY] + in_specs
        args = [into] + args
        aliases = {0: 0}
    return pl.pallas_call(
        body, name=name, grid=grid, in_specs=in_specs, out_specs=o_spec, out_shape=out_shape,
        scratch_shapes=[] if nk == 1 else [pltpu.VMEM(acc_shape, F32)],
        input_output_aliases=aliases, compiler_params=_params(n_axes))(*args)


def _proj(name, a, w, l, dims, out_dtype=F32):
    s, k = a.shape
    g = w.shape[1]
    n = w.shape[3] if dims == "nn" else w.shape[2]
    tm = _tile(s, 512)
    return _mm(name, a, w, dims, (g, s // tm),
               pl.BlockSpec((tm, k), lambda j, i: (i, 0)),
               pl.BlockSpec((None, None) + w.shape[2:], lambda j, i: (l, j, 0, 0)),
               pl.BlockSpec((None, tm, n), lambda j, i: (j, i, 0)),
               jax.ShapeDtypeStruct((g, s, n), out_dtype), 1, None)


def _contract(name, a, w, l, dims, out_dtype=F32):
    g, s, kb = a.shape
    n = w.shape[3] if dims == "nn" else w.shape[2]
    tm = _tile(s, 512)
    return _mm(name, a, w, dims, (s // tm, g),
               pl.BlockSpec((None, tm, kb), lambda i, j: (j, i, 0)),
               pl.BlockSpec((None, None) + w.shape[2:], lambda i, j: (l, j, 0, 0)),
               pl.BlockSpec((tm, n), lambda i, j: (i, 0)),
               jax.ShapeDtypeStruct((s, n), out_dtype), g, (tm, n))


def _wgrad(name, a, b, out_dtype, layers=None, l=None, into=None):
    g = a.shape[0] if a.ndim == 3 else b.shape[0]
    s, k = a.shape[-2:]
    n = b.shape[-1]
    tm = _tile(s, 512)

    def spec(x, w):
        if x.ndim == 3:
            return pl.BlockSpec((None, tm, w), lambda j, i: (j, i, 0))
        return pl.BlockSpec((tm, w), lambda j, i: (i, 0))

    if layers is None:
        o_spec = pl.BlockSpec((None, k, n), lambda j, i: (j, 0, 0))
        out_shape = jax.ShapeDtypeStruct((g, k, n), out_dtype)
    else:
        o_spec = pl.BlockSpec((None, None, k, n), lambda j, i: (l, j, 0, 0))
        out_shape = jax.ShapeDtypeStruct((layers, g, k, n), out_dtype)
    return _mm(name, a, b, "tn", (g, s // tm), spec(a, k), spec(b, n), o_spec, out_shape, s // tm, (k, n), into=into)


def _ffn_up_fused(n, wg, wu, l):
    s, k = n.shape
    g, n4 = wg.shape[1], wg.shape[3]
    tm = _tile(s, 512)

    def body(n_ref, wg_ref, wu_ref, a_ref, b_ref, h_ref):
        nv = n_ref[...]
        a = jnp.dot(nv, wg_ref[...], preferred_element_type=F32)
        b = jnp.dot(nv, wu_ref[...], preferred_element_type=F32)
        a_ref[...] = a.astype(a_ref.dtype)
        b_ref[...] = b.astype(b_ref.dtype)
        h_ref[...] = (a * _sigmoid(a) * b).astype(h_ref.dtype)

    w_spec = pl.BlockSpec((None, None, k, n4), lambda j, i: (l, j, 0, 0))
    o_spec = pl.BlockSpec((None, tm, n4), lambda j, i: (j, i, 0))
    return pl.pallas_call(
        body, name="ffn_up_fused", grid=(g, s // tm),
        in_specs=[pl.BlockSpec((tm, k), lambda j, i: (i, 0)), w_spec, w_spec], out_specs=[o_spec] * 3,
        out_shape=[jax.ShapeDtypeStruct((g, s, n4), MM_DTYPE)] * 3, compiler_params=_params(2))(n, wg, wu)


def _ffn_down_t_fused(do, wd, a, b, l):
    s, k = do.shape
    g, n4 = wd.shape[1], wd.shape[2]
    tm = _tile(s, 512)

    def body(do_ref, wd_ref, a_ref, b_ref, da_ref, db_ref):
        dh = lax.dot_general(do_ref[...], wd_ref[...], _DIMS["nt"], preferred_element_type=F32)
        av = a_ref[...].astype(F32)
        sg = _sigmoid(av)
        da_ref[...] = (dh * b_ref[...].astype(F32) * (sg * (1.0 + av * (1.0 - sg)))).astype(da_ref.dtype)
        db_ref[...] = (dh * (av * sg)).astype(db_ref.dtype)

    t_spec = pl.BlockSpec((None, tm, n4), lambda j, i: (j, i, 0))
    return pl.pallas_call(
        body, name="ffn_down_t_fused", grid=(g, s // tm),
        in_specs=[pl.BlockSpec((tm, k), lambda j, i: (i, 0)),
                  pl.BlockSpec((None, None, n4, k), lambda j, i: (l, j, 0, 0)), t_spec, t_spec],
        out_specs=[t_spec] * 2, out_shape=[jax.ShapeDtypeStruct((g, s, n4), MM_DTYPE)] * 2,
        compiler_params=_params(2))(do, wd, a, b)


def _blocked_sum(name, acts, weights, l, dims):
    g, s, kb = acts[0].shape
    n = weights[0].shape[3] if dims == "nn" else weights[0].shape[2]
    tm = _tile(s, 256)
    n_pairs = len(acts)

    def body(*refs):
        o_ref = refs[2 * n_pairs]
        acc = None
        for p in range(n_pairs):
            for j in range(g):
                t = lax.dot_general(refs[p][j], refs[n_pairs + p][j], _DIMS[dims], preferred_element_type=F32)
                acc = t if acc is None else acc + t
        o_ref[...] = acc

    return pl.pallas_call(
        body, name=name, grid=(s // tm,),
        in_specs=[pl.BlockSpec((g, tm, kb), lambda i: (0, i, 0))] * n_pairs
        + [pl.BlockSpec((None,) + w.shape[1:], lambda i: (l, 0, 0, 0)) for w in weights],
        out_specs=pl.BlockSpec((tm, n), lambda i: (i, 0)), out_shape=jax.ShapeDtypeStruct((s, n), F32),
        compiler_params=_params(1))(*acts, *weights)


def _wgrad_pair(name, nt, da, db, layers, l, into):
    k, s = nt.shape
    g, _, n4 = da.shape
    tm = _tile(s, 512)
    nk = s // tm

    def body(*refs):
        if into is not None:
            refs = refs[2:]
        nt_ref, da_ref, db_ref, og_ref, ou_ref, accg, accu = refs
        i = pl.program_id(1)
        ntv = nt_ref[...]
        pg = jnp.dot(ntv, da_ref[...], preferred_element_type=F32)
        pu = jnp.dot(ntv, db_ref[...], preferred_element_type=F32)

        @pl.when(i == 0)
        def _():
            accg[...] = pg
            accu[...] = pu

        @pl.when(i > 0)
        def _():
            accg[...] += pg
            accu[...] += pu

        @pl.when(i == nk - 1)
        def _():
            og_ref[...] = accg[...].astype(og_ref.dtype)
            ou_ref[...] = accu[...].astype(ou_ref.dtype)

    t_spec = pl.BlockSpec((None, tm, n4), lambda j, i: (j, i, 0))
    o_spec = pl.BlockSpec((None, None, k, n4), lambda j, i: (l, j, 0, 0))
    in_specs = [pl.BlockSpec((k, tm), lambda j, i: (0, i)), t_spec, t_spec]
    args = [nt, da, db]
    aliases = {}
    if into is not None:
        in_specs = [ANY, ANY] + in_specs
        args = list(into) + args
        aliases = {0: 0, 1: 1}
    return pl.pallas_call(
        body, name=name, grid=(g, nk), in_specs=in_specs, out_specs=[o_spec] * 2,
        out_shape=[jax.ShapeDtypeStruct((layers, g, k, n4), WIRE_DTYPE)] * 2,
        scratch_shapes=[pltpu.VMEM((k, n4), F32)] * 2, input_output_aliases=aliases,
        compiler_params=_params(2))(*args)


def _rows(name, fn, s, tm, ins, outs, scratch=()):
    n = s // tm

    def spec(shape, kind):
        nd = len(shape)
        if kind == "r":
            return pl.BlockSpec(shape[:-2] + (tm, shape[-1]), lambda i: (0,) * (nd - 2) + (i, 0))
        if kind in ("f", "a"):
            return pl.BlockSpec(shape, lambda i: (0,) * nd)
        if kind == "t":
            return pl.BlockSpec((shape[0], tm), lambda i: (0, i))
        if kind == "b":
            return pl.BlockSpec((None,) + shape[1:], lambda i: (i, 0, 0))
        if kind == "p":
            return pl.BlockSpec((HALO, shape[-1]), lambda i: (jnp.maximum(i * (tm // HALO) - 1, 0), 0))
        if kind == "n":
            return pl.BlockSpec((HALO, shape[-1]), lambda i: (jnp.minimum((i + 1) * (tm // HALO), s // HALO - 1), 0))
        j = kind[1]
        return pl.BlockSpec((tm, HEAD_PAD), lambda i: (i, j))

    def body(*refs):
        fn(pl.program_id(0), n, *refs)

    return pl.pallas_call(
        body, name=name, grid=(n,),
        in_specs=[spec(a.shape, kind) for a, kind in ins],
        out_specs=[spec(shape, kind) for shape, _, kind in outs],
        out_shape=[jax.ShapeDtypeStruct(shape, dtype) for shape, dtype, _ in outs],
        scratch_shapes=list(scratch), compiler_params=_params(1))(*[a for a, _ in ins])


def _acc(ref, i, val):
    @pl.when(i == 0)
    def _():
        ref[...] = val

    @pl.when(i > 0)
    def _():
        ref[...] += val


def _sum0(x):
    return jnp.sum(x, axis=0, keepdims=True)


def _rstd(x, n):
    return lax.rsqrt(jnp.sum(x * x, axis=-1, keepdims=True) * (1.0 / n) + EPS)


def _rms_bwd(x, g, dy, n):
    r = _rstd(x, n)
    xh = x * r
    dyg = dy * g
    dx = r * (dyg - xh * (jnp.sum(dyg * xh, axis=-1, keepdims=True) * (1.0 / n)))
    return dx, _sum0(dy * xh)


def _sigmoid(x):
    return 1.0 / (1.0 + jnp.exp(-x))


def _norm_fwd(x, g):
    s, d = x.shape

    def fn(i, n, x_ref, g_ref, o_ref, ot_ref):
        xv = x_ref[...]
        y = xv * _rstd(xv, d) * g_ref[...]
        o_ref[...] = y.astype(o_ref.dtype)
        ot_ref[...] = y.T.astype(ot_ref.dtype)

    return _rows("norm_fwd", fn, s, _tile(s, 256), [(x, "r"), (g, "f")],
                 [((s, d), MM_DTYPE, "r"), ((d, s), MM_DTYPE, "t")])


def _resid_norm(x, adds, scale, g, norm_dtype, transposed=False):
    s, d = x.shape
    k = len(adds)

    def fn(i, n, *refs):
        x_ref, add_refs, g_ref, xo_ref, no_ref = refs[0], refs[1:1 + k], refs[1 + k], refs[2 + k], refs[3 + k]
        tot = add_refs[0][...]
        for r in add_refs[1:]:
            tot = tot + r[...]
        xv = x_ref[...] + scale * tot
        xo_ref[...] = xv
        y = xv * _rstd(xv, d) * g_ref[...]
        no_ref[...] = y.astype(no_ref.dtype)
        if transposed:
            refs[4 + k][...] = y.T.astype(refs[4 + k].dtype)

    return _rows("resid_norm", fn, s, _tile(s, 256), [(x, "r")] + [(a, "r") for a in adds] + [(g, "f")],
                 [((s, d), F32, "r"), ((s, d), norm_dtype, "r")] + ([((d, s), MM_DTYPE, "t")] if transposed else []))


def _norm_bwd(x, g, dn_parts, dres, out_scale):
    s, d = x.shape
    k = len(dn_parts)
    has_res = dres is not None

    def fn(i, n, *refs):
        x_ref, g_ref = refs[0], refs[1]
        dn_refs = refs[2:2 + k]
        pos = 2 + k
        dn = dn_refs[0][...]
        for r in dn_refs[1:]:
            dn = dn + r[...]
        dx, dg = _rms_bwd(x_ref[...], g_ref[...], dn, d)
        if has_res:
            dx = dx + refs[pos][...]
            pos += 1
        dx_ref, dxs_ref, dg_ref = refs[pos:pos + 3]
        dx_ref[...] = dx
        dxs_ref[...] = (out_scale * dx).astype(dxs_ref.dtype)
        _acc(dg_ref, i, dg)

    ins = [(x, "r"), (g, "f")] + [(p, "r") for p in dn_parts] + ([(dres, "r")] if has_res else [])
    return _rows("norm_bwd", fn, s, _tile(s, 256), ins,
                 [((s, d), F32, "r"), ((s, d), MM_DTYPE, "r"), ((1, d), F32, "a")])


def _final_fwd(x, o, g):
    return _resid_norm(x, [o], 0.5, g, F32)


def _loss(y, t):
    s, d = y.shape

    def fn(i, n, y_ref, t_ref, dy_ref, l_ref):
        e = y_ref[...] - t_ref[...]
        dy_ref[...] = e * (1.0 / d)
        _acc(l_ref, i, (0.5 / d) * jnp.sum(jnp.sum(e * e, axis=-1, keepdims=True), axis=0, keepdims=True))

    return _rows("loss", fn, s, _tile(s, 256), [(y, "r"), (t, "r")], [((s, d), F32, "r"), ((1, 1), F32, "a")])


def _mla_pre(p, gql, gkvl, cw):
    s = p.shape[0]
    ql, kvl = gql.shape[-1], gkvl.shape[-1]
    o_q, o_kv = 2 * cw, 2 * cw + ql

    def fn(i, n, p_ref, gq_ref, gkv_ref, cq_ref, ckv_ref, ckvt_ref):
        cq = p_ref[:, o_q:o_q + ql]
        cq_ref[...] = (cq * _rstd(cq, ql) * gq_ref[...]).astype(cq_ref.dtype)
        ckv = p_ref[:, o_kv:o_kv + kvl]
        y = ckv * _rstd(ckv, kvl) * gkv_ref[...]
        ckv_ref[...] = y.astype(ckv_ref.dtype)
        ckvt_ref[...] = y.T.astype(ckvt_ref.dtype)

    return _rows("mla_pre", fn, s, _tile(s, 256), [(p, "r"), (gql, "f"), (gkvl, "f")],
                 [((s, ql), MM_DTYPE, "r"), ((s, kvl), MM_DTYPE, "r"), ((kvl, s), MM_DTYPE, "t")])


def _rope(x, cosf, s1, s2):
    return x * cosf + pltpu.roll(x, HEAD_PAD - 16, 1) * s1 + pltpu.roll(x, 16, 1) * s2


def _rope_bwd(d, cosf, s1, s2):
    return d * cosf + pltpu.roll(d * s1, 16, 1) + pltpu.roll(d * s2, HEAD_PAD - 16, 1)


def _qk_post(q_pre, k_pre, p, pe_block, tabs, gq, gk):
    s, hw = q_pre.shape
    cosf, s1, s2 = tabs

    tm = _tile(s, 256)

    def fn(i, n, q_ref, k_ref, pe_ref, c_ref, s1_ref, s2_ref, gq_ref, gk_ref, qo_ref, ko_ref, kt_ref):
        c, a1, a2, pe = c_ref[...], s1_ref[...], s2_ref[...], pe_ref[...]
        for hh in range(N_HEADS):
            cols = slice(hh * HEAD_PAD, (hh + 1) * HEAD_PAD)
            qv = q_ref[:, cols]
            qo_ref[:, cols] = _rope(qv * _rstd(qv, QK_DIM) * gq_ref[...], c, a1, a2).astype(qo_ref.dtype)
            kv = k_ref[:, cols] + pe
            kout = _rope(kv * _rstd(kv, QK_DIM) * gk_ref[...], c, a1, a2)
            ko_ref[:, cols] = kout.astype(ko_ref.dtype)
            kt_ref[cols, :] = kout.T.astype(kt_ref.dtype)

    return _rows("qk_post", fn, s, tm,
                 [(q_pre, "r"), (k_pre, "r"), (p, ("c", pe_block)), (cosf, "r"), (s1, "r"), (s2, "r"), (gq, "f"), (gk, "f")],
                 [((s, hw), MM_DTYPE, "r"), ((s, hw), MM_DTYPE, "r"), ((s // tm, hw, tm), MM_DTYPE, "b")])


def _qk_post_bwd(q_pre, k_pre, p, pe_block, tabs, gq, gk, dq, dk):
    s, hw = q_pre.shape
    cosf, s1, s2 = tabs

    def fn(i, n, q_ref, k_ref, pe_ref, c_ref, s1_ref, s2_ref, gq_ref, gk_ref, dq_ref, dk_ref,
           dqo_ref, dko_ref, dpe_ref, dgq_ref, dgk_ref):
        c, a1, a2, pe = c_ref[...], s1_ref[...], s2_ref[...], pe_ref[...]
        lane = lax.broadcasted_iota(jnp.int32, pe.shape, 1)
        is_pe = (lane >= QK_NOPE) & (lane < QK_DIM)
        dpe = jnp.zeros(pe.shape, F32)
        dgq = jnp.zeros((1, HEAD_PAD), F32)
        dgk = jnp.zeros((1, HEAD_PAD), F32)
        for hh in range(N_HEADS):
            cols = slice(hh * HEAD_PAD, (hh + 1) * HEAD_PAD)
            dxq, g1 = _rms_bwd(q_ref[:, cols], gq_ref[...], _rope_bwd(dq_ref[cols, :].T, c, a1, a2), QK_DIM)
            dqo_ref[:, cols] = dxq.astype(dqo_ref.dtype)
            dxk, g2 = _rms_bwd(k_ref[:, cols] + pe, gk_ref[...], _rope_bwd(dk_ref[:, cols], c, a1, a2), QK_DIM)
            dko_ref[:, cols] = dxk.astype(dko_ref.dtype)
            dpe = dpe + dxk
            dgq = dgq + g1
            dgk = dgk + g2
        dpe_ref[...] = jnp.where(is_pe, dpe, 0.0)
        _acc(dgq_ref, i, dgq)
        _acc(dgk_ref, i, dgk)

    return _rows("qk_post_bwd", fn, s, _tile(s, 256),
                 [(q_pre, "r"), (k_pre, "r"), (p, ("c", pe_block)), (cosf, "r"), (s1, "r"), (s2, "r"), (gq, "f"), (gk, "f"),
                  (dq, "b"), (dk, "r")],
                 [((s, hw), MM_DTYPE, "r"), ((s, hw), MM_DTYPE, "r"), ((s, HEAD_PAD), F32, "r"),
                  ((1, HEAD_PAD), F32, "a"), ((1, HEAD_PAD), F32, "a")])


def _chunk_mask(tq, key_rows):
    r = jnp.right_shift(lax.broadcasted_iota(jnp.int32, (tq, tq), 0), CHUNK_SHIFT)
    c = jnp.right_shift(lax.broadcasted_iota(jnp.int32, (tq, tq), 1), CHUNK_SHIFT)
    return (r <= c) if key_rows else (c <= r)


def _attn_specs(s, tq):
    w = HEADS_PER_STEP * HEAD_PAD
    tile = pl.BlockSpec((tq, w), lambda hb, i: (i, hb))
    whole = pl.BlockSpec((s, w), lambda hb, i: (0, hb))
    rows = pl.BlockSpec((None, None, HEADS_PER_STEP, tq), lambda hb, i: (hb, i, 0, 0))
    blocks = pl.BlockSpec((s // tq, w, tq), lambda hb, i: (0, hb, 0))
    return tile, whole, rows, blocks


def _head(hh):
    return slice(hh * HEAD_PAD, (hh + 1) * HEAD_PAD)


def _attn_fwd(q, k, vt):
    s, hw = q.shape
    tq = _tile(s, 256)
    nq = s // tq
    groups = N_HEADS // HEADS_PER_STEP
    scale = QK_DIM ** -0.5
    tile, whole, rows, blocks = _attn_specs(s, tq)

    def body(q_ref, k_ref, vt_ref, o_ref, lse_ref):
        qi = pl.program_id(1)
        qv = [q_ref[:, _head(hh)] for hh in range(HEADS_PER_STEP)]

        def scores(kb):
            off = pl.multiple_of(kb * tq, tq)
            return tuple(lax.dot_general(k_ref[pl.ds(off, tq), _head(hh)], qv[hh], _DIMS["nt"],
                                         preferred_element_type=F32) for hh in range(HEADS_PER_STEP))

        def update(kb, stats, raw, diag):
            heads = range(HEADS_PER_STEP)
            mn, al, pr, ln = [], [], [], []
            for hh in heads:
                m, l, _ = stats[hh]
                sc = raw[hh] * (scale * LOG2E)
                if diag:
                    sc = jnp.where(_chunk_mask(tq, True), sc, -1e30)
                mn.append(jnp.maximum(m, jnp.max(sc, axis=0, keepdims=True)))
                al.append(jnp.exp2(m - mn[hh]))
                p = jnp.exp2(sc - mn[hh])
                ln.append(al[hh] * l + jnp.sum(p, axis=0, keepdims=True))
                pr.append(p.astype(MM_DTYPE))
            return tuple((mn[hh], ln[hh], al[hh] * stats[hh][2]
                          + jnp.dot(vt_ref[kb, _head(hh), :], pr[hh], preferred_element_type=F32)) for hh in heads)

        def step(kb, carry):
            stats, raw = carry
            nxt = scores(kb + 1)
            return update(kb, stats, raw, False), nxt

        init = tuple((jnp.full((1, tq), -1e30, F32), jnp.zeros((1, tq), F32), jnp.zeros((HEAD_PAD, tq), F32))
                     for _ in range(HEADS_PER_STEP))
        stats, raw = lax.fori_loop(0, qi, step, (init, scores(0)))
        carry = update(qi, stats, raw, True)
        for hh in range(HEADS_PER_STEP):
            m, l, acc = carry[hh]
            o_ref[:, _head(hh)] = (acc / l).T
            lse_ref[hh:hh + 1, :] = m + jnp.log(l) * LOG2E

    return pl.pallas_call(
        body, name="attn_fwd", grid=(groups, nq),
        in_specs=[tile, whole, blocks], out_specs=[tile, rows],
        out_shape=[jax.ShapeDtypeStruct((s, hw), F32), jax.ShapeDtypeStruct((groups, nq, HEADS_PER_STEP, tq), F32)],
        compiler_params=_params(2))(q, k, vt)


def _attn_delta(o, do):
    s, hw = o.shape
    tq = _tile(s, 256)
    groups = N_HEADS // HEADS_PER_STEP
    tile, _, rows, _ = _attn_specs(s, tq)

    def body(o_ref, do_ref, d_ref):
        for hh in range(HEADS_PER_STEP):
            prod = o_ref[:, _head(hh)] * do_ref[:, _head(hh)]
            d_ref[hh:hh + 1, :] = jnp.sum(prod.T, axis=0, keepdims=True)

    return pl.pallas_call(
        body, name="attn_delta", grid=(groups, s // tq), in_specs=[tile, tile], out_specs=rows,
        out_shape=jax.ShapeDtypeStruct((groups, s // tq, HEADS_PER_STEP, tq), F32),
        compiler_params=_params(2))(o, do)


def _attn_bwd(q, k, kt, v, do, lse_row, delta_row):
    s, hw = q.shape
    tq = _tile(s, 256)
    nq = s // tq
    groups = N_HEADS // HEADS_PER_STEP
    w = HEADS_PER_STEP * HEAD_PAD
    scale = QK_DIM ** -0.5
    tile, whole, _, blocks = _attn_specs(s, tq)
    rowvec = pl.BlockSpec((None, nq, HEADS_PER_STEP, tq), lambda hb, i: (hb, 0, 0, 0))
    kt_tile = pl.BlockSpec((None, w, tq), lambda hb, i: (i, hb, 0))

    def body(q_ref, k_ref, kt_ref, v_ref, do_ref, lse_ref, delta_ref, dqt_ref, dk_ref, dv_ref):
        ki = pl.program_id(1)
        kv = [k_ref[:, _head(hh)] for hh in range(HEADS_PER_STEP)]
        vv = [v_ref[:, _head(hh)] for hh in range(HEADS_PER_STEP)]
        ktv = [kt_ref[_head(hh), :] for hh in range(HEADS_PER_STEP)]

        @pl.when(ki == 0)
        def _():
            dqt_ref[...] = jnp.zeros(dqt_ref.shape, F32)

        def block(qb, carry, diag):
            off = pl.multiple_of(qb * tq, tq)
            heads = range(HEADS_PER_STEP)
            qs = [q_ref[pl.ds(off, tq), _head(hh)] for hh in heads]
            dos = [do_ref[pl.ds(off, tq), _head(hh)].astype(MM_DTYPE) for hh in heads]
            raw = [lax.dot_general(kv[hh], qs[hh], _DIMS["nt"], preferred_element_type=F32) for hh in heads]
            dpr = [lax.dot_general(vv[hh], dos[hh], _DIMS["nt"], preferred_element_type=F32) for hh in heads]
            pr, ds = [], []
            for hh in heads:
                sc = raw[hh] * (scale * LOG2E)
                if diag:
                    sc = jnp.where(_chunk_mask(tq, True), sc, -1e30)
                p = jnp.exp2(sc - lse_ref[qb, hh:hh + 1, :])
                pr.append(p.astype(MM_DTYPE))
                ds.append((p * (dpr[hh] - delta_ref[qb, hh:hh + 1, :]) * scale).astype(MM_DTYPE))
            out = []
            for hh in heads:
                dk, dvv = carry[hh]
                dvv = dvv + jnp.dot(pr[hh], dos[hh], preferred_element_type=F32)
                dk = dk + jnp.dot(ds[hh], qs[hh], preferred_element_type=F32)
                dqt_ref[qb, _head(hh), :] += jnp.dot(ktv[hh], ds[hh], preferred_element_type=F32)
                out.append((dk, dvv))
            return tuple(out)

        init = tuple((jnp.zeros((tq, HEAD_PAD), F32), jnp.zeros((tq, HEAD_PAD), F32)) for _ in range(HEADS_PER_STEP))
        carry = lax.fori_loop(ki + 1, nq, lambda qb, c: block(qb, c, False), block(ki, init, True))
        for hh in range(HEADS_PER_STEP):
            dk_ref[:, _head(hh)] = carry[hh][0]
            dv_ref[:, _head(hh)] = carry[hh][1].astype(dv_ref.dtype)

    return pl.pallas_call(
        body, name="attn_bwd", grid=(groups, nq),
        in_specs=[whole, tile, kt_tile, tile, whole, rowvec, rowvec], out_specs=[blocks, tile, tile],
        out_shape=[jax.ShapeDtypeStruct((nq, hw, tq), F32), jax.ShapeDtypeStruct((s, hw), F32),
                   jax.ShapeDtypeStruct((s, hw), MM_DTYPE)],
        compiler_params=_params(2))(q, k, kt, v, do, lse_row, delta_row)


def _glu_with_halo(i, p_ref, ph_ref, ubuf, cw, tm):
    a, g = p_ref[:, 0:cw], p_ref[:, cw:2 * cw]
    ubuf[HALO:HALO + tm, :] = a * _sigmoid(g)
    ah, gh = ph_ref[:, 0:cw], ph_ref[:, cw:2 * cw]
    ubuf[0:HALO, :] = jnp.where(i > 0, ah * _sigmoid(gh), 0.0)
    return a, g


def _conv_taps(ubuf, w_ref, tm):
    base = HALO - (CONV_K - 1)
    y = w_ref[0:1, :] * ubuf[base:base + tm, :]
    for k in range(1, CONV_K):
        y = y + w_ref[k:k + 1, :] * ubuf[base + k:base + k + tm, :]
    return y


def _layer_norm_stats(y, cw):
    mu = jnp.sum(y, axis=-1, keepdims=True) * (1.0 / cw)
    yc = y - mu
    r = lax.rsqrt(jnp.sum(yc * yc, axis=-1, keepdims=True) * (1.0 / cw) + EPS)
    return yc * r, r


def _conv_fwd(p, w, cb, lg, lb):
    s = p.shape[0]
    cw = cb.shape[-1]
    tm = _tile(s, 256)

    def fn(i, n, p_ref, ph_ref, w_ref, cb_ref, lg_ref, lb_ref, o_ref, ubuf):
        _glu_with_halo(i, p_ref, ph_ref, ubuf, cw, tm)
        yh, _ = _layer_norm_stats(_conv_taps(ubuf, w_ref, tm) + cb_ref[...], cw)
        z = yh * lg_ref[...] + lb_ref[...]
        o_ref[...] = (z * _sigmoid(z)).astype(o_ref.dtype)

    return _rows("conv_fwd", fn, s, tm, [(p, "r"), (p, "p"), (w, "f"), (cb, "f"), (lg, "f"), (lb, "f")],
                 [((s, cw), MM_DTYPE, "r")], scratch=[pltpu.VMEM((tm + HALO, cw), F32)])[0]


def _conv_bwd_ln(p, w, cb, lg, lb, dc):
    s = p.shape[0]
    cw = cb.shape[-1]
    tm = _tile(s, 256)

    def fn(i, n, p_ref, ph_ref, w_ref, cb_ref, lg_ref, lb_ref, dc_ref, dy_ref, dlg_ref, dlb_ref, dcb_ref, ubuf):
        _glu_with_halo(i, p_ref, ph_ref, ubuf, cw, tm)
        yh, r = _layer_norm_stats(_conv_taps(ubuf, w_ref, tm) + cb_ref[...], cw)
        z = yh * lg_ref[...] + lb_ref[...]
        sg = _sigmoid(z)
        dz = dc_ref[...] * (sg * (1.0 + z * (1.0 - sg)))
        dyh = dz * lg_ref[...]
        m1 = jnp.sum(dyh, axis=-1, keepdims=True) * (1.0 / cw)
        m2 = jnp.sum(dyh * yh, axis=-1, keepdims=True) * (1.0 / cw)
        dy = r * (dyh - m1 - yh * m2)
        dy_ref[...] = dy
        _acc(dlg_ref, i, _sum0(dz * yh))
        _acc(dlb_ref, i, _sum0(dz))
        _acc(dcb_ref, i, _sum0(dy))

    return _rows("conv_bwd_ln", fn, s, tm,
                 [(p, "r"), (p, "p"), (w, "f"), (cb, "f"), (lg, "f"), (lb, "f"), (dc, "r")],
                 [((s, cw), F32, "r"), ((1, cw), F32, "a"), ((1, cw), F32, "a"), ((1, cw), F32, "a")],
                 scratch=[pltpu.VMEM((tm + HALO, cw), F32)])


def _conv_bwd_taps(p, w, dy):
    s = p.shape[0]
    cw = w.shape[-1]
    tm = _tile(s, 256)

    def fn(i, n, p_ref, ph_ref, w_ref, dy_ref, dyn_ref, dp_ref, dw_ref, ubuf, dybuf):
        a, g = _glu_with_halo(i, p_ref, ph_ref, ubuf, cw, tm)
        dyv = dy_ref[...]
        dybuf[0:tm, :] = dyv
        dybuf[tm:tm + HALO, :] = jnp.where(i < n - 1, dyn_ref[...], 0.0)
        base = HALO - (CONV_K - 1)
        du = jnp.zeros((tm, cw), F32)

        @pl.when(i == 0)
        def _():
            dw_ref[...] = jnp.zeros(dw_ref.shape, F32)

        for k in range(CONV_K):
            sh = CONV_K - 1 - k
            du = du + w_ref[k:k + 1, :] * dybuf[sh:sh + tm, :]
            dw_ref[k:k + 1, :] += _sum0(dyv * ubuf[base + k:base + k + tm, :])
        sg = _sigmoid(g)
        dp_ref[:, 0:cw] = (du * sg).astype(dp_ref.dtype)
        dp_ref[:, cw:2 * cw] = (du * a * sg * (1.0 - sg)).astype(dp_ref.dtype)

    return _rows("conv_bwd_taps", fn, s, tm, [(p, "r"), (p, "p"), (w, "f"), (dy, "r"), (dy, "n")],
                 [((s, 2 * cw), MM_DTYPE, "r"), ((CONV_K + 1, cw), F32, "a")],
                 scratch=[pltpu.VMEM((tm + HALO, cw), F32), pltpu.VMEM((tm + HALO, cw), F32)])


def _dp_assemble(p, gql, gkvl, dcq, dckv_parts, dp_conv, dpe, cw):
    s, width = p.shape
    ql, kvl = gql.shape[-1], gkvl.shape[-1]
    o_q, o_kv, o_pe = 2 * cw, 2 * cw + ql, 2 * cw + ql + kvl

    def fn(i, n, p_ref, gq_ref, gkv_ref, dcq_ref, dk1_ref, dk2_ref, dconv_ref, dpe_ref, dp_ref, dgq_ref, dgkv_ref):
        dp_ref[:, 0:o_q] = dconv_ref[...]
        dx, dg = _rms_bwd(p_ref[:, o_q:o_kv], gq_ref[...], dcq_ref[...], ql)
        dp_ref[:, o_q:o_kv] = dx.astype(dp_ref.dtype)
        _acc(dgq_ref, i, dg)
        dx, dg = _rms_bwd(p_ref[:, o_kv:o_pe], gkv_ref[...], dk1_ref[...] + dk2_ref[...], kvl)
        dp_ref[:, o_kv:o_pe] = dx.astype(dp_ref.dtype)
        _acc(dgkv_ref, i, dg)
        dp_ref[:, o_pe:width] = dpe_ref[...].astype(dp_ref.dtype)

    return _rows("dp_assemble", fn, s, _tile(s, 256),
                 [(p, "r"), (gql, "f"), (gkvl, "f"), (dcq, "r"), (dckv_parts[0], "r"), (dckv_parts[1], "r"),
                  (dp_conv, "r"), (dpe, "r")],
                 [((s, width), MM_DTYPE, "r"), ((1, ql), F32, "a"), ((1, kvl), F32, "a")])


def _add_leading(name, x, out_dtype):
    n, r, c = x.shape
    tm = _tile(r, 256)

    def fn(i, nt, x_ref, o_ref):
        tot = x_ref[0].astype(F32)
        for k in range(1, n):
            tot = tot + x_ref[k].astype(F32)
        o_ref[...] = tot.astype(o_ref.dtype)

    return _rows(name, fn, r, tm, [(x, "r")], [((r, c), out_dtype, "r")])[0]


def _prefetch_call(name, fn, idx, grid, in_specs, out_spec, out_shape, args, aliases=None):
    return pl.pallas_call(
        fn, name=name, out_shape=out_shape,
        grid_spec=pltpu.PrefetchScalarGridSpec(num_scalar_prefetch=1, grid=grid, in_specs=in_specs, out_specs=out_spec),
        input_output_aliases=aliases or {}, compiler_params=_params(len(grid)))(idx, *args)


def _add_sibling(idx, mine, theirs):
    _, r, c = mine.shape
    tm = _tile(r, 256)

    def fn(idx_ref, a_ref, b_ref, o_ref):
        o_ref[...] = (a_ref[...].astype(F32) + b_ref[...].astype(F32)).astype(o_ref.dtype)

    return _prefetch_call(
        "add_sibling", fn, idx, (r // tm,),
        [pl.BlockSpec((None, tm, c), lambda i, ix: (ix[0], i, 0)), pl.BlockSpec((tm, c), lambda i, ix: (i, 0))],
        pl.BlockSpec((tm, c), lambda i, ix: (i, 0)), jax.ShapeDtypeStruct((r, c), WIRE_DTYPE), [mine, theirs])


def _add_chips(idx, own, others):
    _, r, c = own.shape
    tm = _tile(r, 256)

    def fn(idx_ref, a_ref, b_ref, o_ref):
        tot = a_ref[...].astype(F32)
        for k in range(3):
            tot = tot + b_ref[k].astype(F32)
        o_ref[...] = tot

    return _prefetch_call(
        "add_chips", fn, idx, (r // tm,),
        [pl.BlockSpec((None, tm, c), lambda i, ix: (ix[1], i, 0)), pl.BlockSpec((3, tm, c), lambda i, ix: (0, i, 0))],
        pl.BlockSpec((None, tm, c), lambda i, ix: (ix[0], i, 0)), jax.ShapeDtypeStruct((2, r, c), F32), [own, others])


def _adamw(w, g, m, v):
    r, c = w.shape
    tm = _tile(r, 256)

    def fn(i, n, w_ref, g_ref, m_ref, v_ref, d_ref, mo_ref, vo_ref):
        gv = g_ref[...]
        mn = ADAM_B1 * m_ref[...] + (1.0 - ADAM_B1) * gv
        vn = ADAM_B2 * v_ref[...] + (1.0 - ADAM_B2) * (gv * gv)
        mo_ref[...] = mn
        vo_ref[...] = vn
        m_hat = mn / (1.0 - ADAM_B1 ** ADAM_STEP)
        v_hat = vn / (1.0 - ADAM_B2 ** ADAM_STEP)
        d_ref[...] = -ADAM_LR * (m_hat / (jnp.sqrt(v_hat) + ADAM_EPS) + ADAM_WD * w_ref[...])

    return _rows("adamw", fn, r, tm, [(w, "r"), (g, "r"), (m, "r"), (v, "r")], [((r, c), F32, "r")] * 3)


def _place():
    x, y, c = lax.axis_index("x"), lax.axis_index("y"), lax.axis_index("c")
    other_chips = [(1 - x, y), (x, 1 - y), (1 - x, 1 - y)]
    return x, y, c, 2 * x + y, other_chips


def _gather_weights(bufs, shards):
    k_n = len(shards)

    def body(*refs):
        src, dst = refs[k_n:2 * k_n], refs[2 * k_n:3 * k_n]
        send, recv, fsend, frecv = refs[3 * k_n:]
        x, y, c, me, chips = _place()

        def direct(k, j):
            return pltpu.make_async_remote_copy(
                src_ref=src[k].at[c], dst_ref=dst[k].at[c, me], send_sem=send.at[k, j], recv_sem=recv.at[k, j],
                device_id=(chips[j][0], chips[j][1], c), device_id_type=MESH)

        def landed(k, j, layer, sems):
            cx, cy = chips[j]
            blk = dst[k].at[layer, 2 * cx + cy]
            return pltpu.make_async_remote_copy(
                src_ref=blk, dst_ref=blk, send_sem=sems[0].at[k, j], recv_sem=sems[1].at[k, j],
                device_id=(x, y, 1 - c), device_id_type=MESH)

        sent = [direct(k, j) for j in range(3) for k in range(k_n)]
        for cp in sent:
            cp.start()
        passed = []
        for j in range(3):
            for k in range(k_n):
                landed(k, j, c, (send, recv)).wait_recv()
                fwd = landed(k, j, c, (fsend, frecv))
                fwd.start()
                passed.append(fwd)
        for j in range(3):
            for k in range(k_n):
                landed(k, j, 1 - c, (fsend, frecv)).wait_recv()
        for cp in sent + passed:
            cp.wait_send()

    dma = lambda: pltpu.SemaphoreType.DMA((k_n, 3))
    return pl.pallas_call(
        body, name="gather_weights", in_specs=[ANY] * (2 * k_n), out_specs=[ANY] * k_n,
        out_shape=[jax.ShapeDtypeStruct(b.shape, b.dtype) for b in bufs],
        input_output_aliases={k: k for k in range(k_n)},
        scratch_shapes=[dma(), dma(), dma(), dma()],
        compiler_params=pltpu.CompilerParams(has_side_effects=True))(*bufs, *shards)


def _swap_layers(grads):
    k_n = len(grads)

    def body(*refs):
        src, dst = refs[:k_n], refs[k_n:2 * k_n]
        send, recv = refs[2 * k_n:]
        x, y, c, me, chips = _place()
        away = [pltpu.make_async_remote_copy(
            src_ref=src[k].at[1 - c], dst_ref=dst[k], send_sem=send.at[k], recv_sem=recv.at[k],
            device_id=(x, y, 1 - c), device_id_type=MESH) for k in range(k_n)]
        for cp in away:
            cp.start()
        for cp in away:
            cp.wait()

    dma = lambda: pltpu.SemaphoreType.DMA((k_n,))
    return pl.pallas_call(
        body, name="swap_layers", in_specs=[ANY] * k_n, out_specs=[ANY] * k_n,
        out_shape=[jax.ShapeDtypeStruct(g.shape[1:], g.dtype) for g in grads],
        scratch_shapes=[dma(), dma()],
        compiler_params=pltpu.CompilerParams(has_side_effects=True))(*grads)


def _scatter_to_owners(parts):
    k_n = len(parts)

    def body(*refs):
        src, dst = refs[:k_n], refs[k_n:2 * k_n]
        send, recv = refs[2 * k_n:]
        x, y, c, me, chips = _place()
        away = [pltpu.make_async_remote_copy(
            src_ref=src[k].at[2 * chips[j][0] + chips[j][1]], dst_ref=dst[k].at[j],
            send_sem=send.at[k, j], recv_sem=recv.at[k, j],
            device_id=(chips[j][0], chips[j][1], c), device_id_type=MESH) for j in range(3) for k in range(k_n)]
        for cp in away:
            cp.start()
        for cp in away:
            cp.wait()

    dma = lambda: pltpu.SemaphoreType.DMA((k_n, 3))
    return pl.pallas_call(
        body, name="scatter_to_owners", in_specs=[ANY] * k_n, out_specs=[ANY] * k_n,
        out_shape=[jax.ShapeDtypeStruct((3,) + p.shape[1:], p.dtype) for p in parts],
        scratch_shapes=[dma(), dma()],
        compiler_params=pltpu.CompilerParams(has_side_effects=True))(*parts)


def _share_layers(sums):
    k_n = len(sums)

    def body(*refs):
        dst = refs[k_n:2 * k_n]
        send, recv = refs[2 * k_n:]
        x, y, c, me, chips = _place()
        away = [pltpu.make_async_remote_copy(
            src_ref=dst[k].at[c], dst_ref=dst[k].at[c], send_sem=send.at[k], recv_sem=recv.at[k],
            device_id=(x, y, 1 - c), device_id_type=MESH) for k in range(k_n)]
        for cp in away:
            cp.start()
        for k in range(k_n):
            blk = dst[k].at[1 - c]
            pltpu.make_async_remote_copy(
                src_ref=blk, dst_ref=blk, send_sem=send.at[k], recv_sem=recv.at[k],
                device_id=(x, y, 1 - c), device_id_type=MESH).wait_recv()
        for cp in away:
            cp.wait_send()

    dma = lambda: pltpu.SemaphoreType.DMA((k_n,))
    return pl.pallas_call(
        body, name="share_layers", in_specs=[ANY] * k_n, out_specs=[ANY] * k_n,
        out_shape=[jax.ShapeDtypeStruct(s.shape, s.dtype) for s in sums],
        input_output_aliases={k: k for k in range(k_n)},
        scratch_shapes=[dma(), dma()],
        compiler_params=pltpu.CompilerParams(has_side_effects=True))(*sums)


def _gather_small(vec):
    flips = [(fx, fy, fc) for fx in (0, 1) for fy in (0, 1) for fc in (0, 1)][1:]

    def body(src, dst, send, recv, lsem):
        x, y, c, _, _ = _place()
        me = 4 * x + 2 * y + c
        local = pltpu.make_async_copy(src, dst.at[me], lsem)
        local.start()
        away = []
        for j, (fx, fy, fc) in enumerate(flips):
            px, py, pc = x ^ fx, y ^ fy, c ^ fc
            away.append(pltpu.make_async_remote_copy(
                src_ref=src, dst_ref=dst.at[me], send_sem=send.at[j], recv_sem=recv.at[j],
                device_id=(px, py, pc), device_id_type=MESH))
        for cp in away:
            cp.start()
        for j, (fx, fy, fc) in enumerate(flips):
            blk = dst.at[4 * (x ^ fx) + 2 * (y ^ fy) + (c ^ fc)]
            pltpu.make_async_remote_copy(
                src_ref=blk, dst_ref=blk, send_sem=send.at[j], recv_sem=recv.at[j],
                device_id=(x ^ fx, y ^ fy, c ^ fc), device_id_type=MESH).wait_recv()
        for cp in away:
            cp.wait_send()
        local.wait()

    return pl.pallas_call(
        body, name="gather_small", in_specs=[ANY], out_specs=ANY,
        out_shape=jax.ShapeDtypeStruct((8,) + vec.shape, vec.dtype),
        scratch_shapes=[pltpu.SemaphoreType.DMA((7,)), pltpu.SemaphoreType.DMA((7,)), pltpu.SemaphoreType.DMA(())],
        compiler_params=pltpu.CompilerParams(has_side_effects=True))(vec)


SHARDED = ["ffn1_w_gate", "ffn1_w_up", "ffn1_w_down", "w_in", "w_uq", "w_ukv", "conv_w", "w_out",
           "ffn2_w_gate", "ffn2_w_up", "ffn2_w_down"]
SMALL = ["ffn1_norm", "mix_norm", "q_latent_norm", "kv_latent_norm", "q_norm", "k_norm", "conv_b", "conv_ln_g",
         "conv_ln_b", "ffn2_norm", "post_norm"]
ALL_WEIGHTS = ["ffn1_norm", "ffn1_w_gate", "ffn1_w_up", "ffn1_w_down", "mix_norm", "w_in", "q_latent_norm", "w_uq",
               "kv_latent_norm", "w_ukv", "q_norm", "k_norm", "conv_w", "conv_b", "conv_ln_g", "conv_ln_b", "w_out",
               "ffn2_norm", "ffn2_w_gate", "ffn2_w_up", "ffn2_w_down", "post_norm"]


def _rope_tables(s):
    pos = jnp.arange(s, dtype=F32)
    inv_freq = 1.0 / (ROPE_THETA ** (jnp.arange(0, QK_ROPE, 2, dtype=F32) / QK_ROPE))
    ang = pos[:, None] * inv_freq[None, :]
    cos, sin = jnp.cos(ang), jnp.sin(ang)
    half = QK_ROPE // 2
    one = jnp.ones((s, QK_NOPE), F32)
    zero = jnp.zeros((s, QK_NOPE), F32)
    pad = HEAD_PAD - QK_DIM
    z16 = jnp.zeros((s, half), F32)
    cosf = jnp.concatenate([one, cos, cos, jnp.ones((s, pad), F32)], axis=1)
    s1 = jnp.concatenate([zero, -sin, z16, jnp.zeros((s, pad), F32)], axis=1)
    s2 = jnp.concatenate([zero, z16, sin, jnp.zeros((s, pad), F32)], axis=1)
    return cosf, s1, s2


def _ffn_fwd(n, w, names, l):
    a, b, hm = _ffn_up_fused(n, w[names[0]], w[names[1]], l)
    return a, b, hm, _blocked_sum("ffn_down", [hm], [w[names[2]]], l, "nn")


def _ffn_bwd(nt, a, b, hm, do, w, names, l, grads):
    gate, up, down = names
    layers = w[gate].shape[0]
    da, db = _ffn_down_t_fused(do, w[down], a, b, l)
    grads[down] = _wgrad("ffn_down_wgrad", hm, do, WIRE_DTYPE, layers, l, grads.get(down))
    grads[gate], grads[up] = _wgrad_pair("ffn_up_wgrad", nt, da, db, layers, l,
                                         (grads[gate], grads[up]) if gate in grads else None)
    return [_blocked_sum("ffn_up_t", [da, db], [w[gate], w[up]], l, "nt")]


def _vec(v, l):
    return v[l][None, :]


def _pad_lanes(v, width):
    return jnp.pad(v, [(0, 0)] * (v.ndim - 1) + [(0, width - v.shape[-1])])


def kernel(x, ffn1_norm, ffn1_w_gate, ffn1_w_up, ffn1_w_down, mix_norm, w_in, q_latent_norm, w_uq, kv_latent_norm, w_ukv, q_norm, k_norm, conv_w, conv_b, conv_ln_g, conv_ln_b, w_out, ffn2_norm, ffn2_w_gate, ffn2_w_up, ffn2_w_down, post_norm, loss_target, m_ffn1_norm, m_ffn1_w_gate, m_ffn1_w_up, m_ffn1_w_down, m_mix_norm, m_w_in, m_q_latent_norm, m_w_uq, m_kv_latent_norm, m_w_ukv, m_q_norm, m_k_norm, m_conv_w, m_conv_b, m_conv_ln_g, m_conv_ln_b, m_w_out, m_ffn2_norm, m_ffn2_w_gate, m_ffn2_w_up, m_ffn2_w_down, m_post_norm, v_ffn1_norm, v_ffn1_w_gate, v_ffn1_w_up, v_ffn1_w_down, v_mix_norm, v_w_in, v_q_latent_norm, v_w_uq, v_kv_latent_norm, v_w_ukv, v_q_norm, v_k_norm, v_conv_w, v_conv_b, v_conv_ln_g, v_conv_ln_b, v_w_out, v_ffn2_norm, v_ffn2_w_gate, v_ffn2_w_up, v_ffn2_w_down, v_post_norm):
    given = dict(locals())
    wts = {n: given[n] for n in ALL_WEIGHTS}
    mom = {n: given["m_" + n] for n in ALL_WEIGHTS}
    var = {n: given["v_" + n] for n in ALL_WEIGHTS}
    n_layers = ffn1_norm.shape[0]
    s, d = x.shape[1], x.shape[2]
    ql, kvl, cw = q_latent_norm.shape[-1], kv_latent_norm.shape[-1], conv_b.shape[-1]
    heads_per_chip = N_HEADS // N_CHIPS
    xs = x.reshape(s, d)
    target = loss_target.reshape(s, d)

    assert n_layers == 2, "core c of a chip finishes the gradients of layer c"
    local = [jnp.pad(wts[n], ((0, 0), (0, 1), (0, 0))).astype(MM_DTYPE) if n == "conv_w" else wts[n].astype(MM_DTYPE)
             for n in SHARDED]
    L = n_layers
    my_c = lax.axis_index("c")
    my_chip = 2 * lax.axis_index("x") + lax.axis_index("y")
    place = jnp.stack([my_c, my_chip]).astype(jnp.int32)
    bufs = [lax.dynamic_update_slice(lax.empty((L, N_CHIPS) + t.shape[1:], t.dtype), t[:, None], (0, my_chip, 0, 0))
            for t in local]
    w = dict(zip(SHARDED, _gather_weights(bufs, local)))
    hw = N_HEADS * HEAD_PAD
    win = w["w_in"].transpose(0, 2, 1, 3).reshape(L, d, -1)
    zc = lambda n: jnp.zeros((L, d, n), MM_DTYPE)
    o_pe, o_cv = ql + kvl, ql + kvl + QK_ROPE
    w["w_in"] = jnp.concatenate([win[..., o_cv:], win[..., :o_pe], zc(QK_NOPE), win[..., o_pe:o_cv],
                                 zc(HEAD_PAD - QK_DIM)], axis=-1)[:, None]
    wq = w["w_uq"].transpose(0, 2, 1, 3).reshape(L, ql, N_HEADS, QK_DIM)
    w["wq"] = _pad_lanes(wq, HEAD_PAD).reshape(L, 1, ql, hw)
    wkv = w["w_ukv"].transpose(0, 2, 1, 3).reshape(L, kvl, N_HEADS, QK_NOPE + V_DIM)
    w["wk"] = _pad_lanes(wkv[..., :QK_NOPE], HEAD_PAD).reshape(L, 1, kvl, hw)
    w["wv"] = _pad_lanes(wkv[..., QK_NOPE:], HEAD_PAD).reshape(L, 1, kvl, hw)
    wvt = w["wv"][:, 0].transpose(0, 2, 1)
    conv_full = w["conv_w"].transpose(0, 2, 1, 3).reshape(L, CONV_K + 1, cw).astype(F32)
    wout = w["w_out"].reshape(L, d, d)
    wo_a = wout[:, :N_HEADS * V_DIM].reshape(L, N_HEADS, V_DIM, d)
    w["wo_a"] = jnp.pad(wo_a, ((0, 0), (0, 0), (0, HEAD_PAD - V_DIM), (0, 0))).reshape(L, 1, hw, d)
    w["wo_c"] = wout[:, N_HEADS * V_DIM:][:, None]
    tabs = _rope_tables(s)
    pe_block = (2 * cw + ql + kvl) // HEAD_PAD
    gq_pad, gk_pad = _pad_lanes(q_norm, HEAD_PAD), _pad_lanes(k_norm, HEAD_PAD)
    tq = _tile(s, 256)
    groups = N_HEADS // HEADS_PER_STEP

    saved = []
    cur = xs
    for l in range(L):
        sv = {"x": cur}
        n1, sv["n1t"] = _norm_fwd(cur, _vec(ffn1_norm, l))
        sv["a1"], sv["b1"], sv["hm1"], o1 = _ffn_fwd(n1, w, SHARDED[0:3], l)
        sv["x1"], sv["h"] = _resid_norm(cur, [o1], 0.5, _vec(mix_norm, l), MM_DTYPE)
        sv["p"] = _proj("w_in", sv["h"], w["w_in"], l, "nn")[0]
        sv["cq"], sv["ckv"], ckvt = _mla_pre(sv["p"], _vec(q_latent_norm, l), _vec(kv_latent_norm, l), cw)
        sv["q_pre"] = _proj("wq", sv["cq"], w["wq"], l, "nn")[0]
        sv["k_pre"] = _proj("wk", sv["ckv"], w["wk"], l, "nn")[0]
        sv["v"] = _proj("wv", sv["ckv"], w["wv"], l, "nn", MM_DTYPE)[0]
        vt = _mm("wv_transposed", wvt, ckvt, "nn", (s // tq,),
                 pl.BlockSpec((None, hw, kvl), lambda i, l=l: (l, 0, 0)), pl.BlockSpec((kvl, tq), lambda i: (0, i)),
                 pl.BlockSpec((None, hw, tq), lambda i: (i, 0, 0)),
                 jax.ShapeDtypeStruct((s // tq, hw, tq), MM_DTYPE), 1, None)
        sv["q"], sv["k"], sv["kt"] = _qk_post(sv["q_pre"], sv["k_pre"], sv["p"], pe_block, tabs,
                                              _vec(gq_pad, l), _vec(gk_pad, l))
        sv["o"], sv["lse"] = _attn_fwd(sv["q"], sv["k"], vt)
        sv["c"] = _conv_fwd(sv["p"], conv_full[l], _vec(conv_b, l), _vec(conv_ln_g, l), _vec(conv_ln_b, l))
        mix_a = _contract("wo_a", sv["o"][None], w["wo_a"], l, "nn")
        mix_c = _contract("wo_c", sv["c"][None], w["wo_c"], l, "nn")
        sv["x2"], n2, sv["n2t"] = _resid_norm(sv["x1"], [mix_a, mix_c], 1.0, _vec(ffn2_norm, l), MM_DTYPE, True)
        sv["a2"], sv["b2"], sv["hm2"], o2 = _ffn_fwd(n2, w, SHARDED[8:11], l)
        sv["x3"], cur = _final_fwd(sv["x2"], o2, _vec(post_norm, l))
        saved.append(sv)
    dy, loss_part = _loss(cur, target)

    grads = {}
    small = {n: [None] * L for n in SMALL}
    stacked = {}
    for l in reversed(range(L)):
        sv = saved[l]
        dx3, do2, small["post_norm"][l] = _norm_bwd(sv["x3"], _vec(post_norm, l), [dy], None, 0.5)
        dn2 = _ffn_bwd(sv["n2t"], sv["a2"], sv["b2"], sv["hm2"], do2, w, SHARDED[8:11], l, grads)
        dx2, dx2b, small["ffn2_norm"][l] = _norm_bwd(sv["x2"], _vec(ffn2_norm, l), dn2, dx3, 1.0)
        do = _proj("wo_a_t", dx2b, w["wo_a"], l, "nt")[0]
        dc = _proj("wo_c_t", dx2b, w["wo_c"], l, "nt")[0]
        g_wo_a = _wgrad("wo_a_wgrad", sv["o"], dx2b[None], WIRE_DTYPE)[0].reshape(N_HEADS, HEAD_PAD, d)[:, :V_DIM]
        g_wo_c = _wgrad("wo_c_wgrad", sv["c"], dx2b[None], WIRE_DTYPE)
        stacked.setdefault("w_out", [None] * L)[l] = jnp.concatenate(
            [g_wo_a.reshape(N_HEADS * V_DIM, d), g_wo_c[0]], axis=0).reshape(N_CHIPS, d // N_CHIPS, d)
        dq, dk, dv = _attn_bwd(sv["q"], sv["k"], sv["kt"], sv["v"], do, sv["lse"], _attn_delta(sv["o"], do))
        dq_pre, dk_pre, dpe, small["q_norm"][l], small["k_norm"][l] = _qk_post_bwd(
            sv["q_pre"], sv["k_pre"], sv["p"], pe_block, tabs, _vec(gq_pad, l), _vec(gk_pad, l), dq, dk)
        dcq = _contract("wq_t", dq_pre[None], w["wq"], l, "nt")
        dckv = [_contract("wk_t", dk_pre[None], w["wk"], l, "nt"), _contract("wv_t", dv[None], w["wv"], l, "nt")]
        per_head = lambda t, rows: t[0].reshape(rows, N_HEADS, HEAD_PAD)
        g_wq = per_head(_wgrad("wq_wgrad", sv["cq"], dq_pre[None], WIRE_DTYPE), ql)[..., :QK_DIM]
        stacked.setdefault("w_uq", [None] * L)[l] = g_wq.reshape(ql, N_CHIPS, -1).transpose(1, 0, 2)
        g_wkv = jnp.concatenate([per_head(_wgrad("wk_wgrad", sv["ckv"], dk_pre[None], WIRE_DTYPE), kvl)[..., :QK_NOPE],
                                 per_head(_wgrad("wv_wgrad", sv["ckv"], dv[None], WIRE_DTYPE), kvl)[..., :V_DIM]], axis=-1)
        stacked.setdefault("w_ukv", [None] * L)[l] = g_wkv.reshape(kvl, N_CHIPS, -1).transpose(1, 0, 2)
        dyc, small["conv_ln_g"][l], small["conv_ln_b"][l], small["conv_b"][l] = _conv_bwd_ln(
            sv["p"], conv_full[l], _vec(conv_b, l), _vec(conv_ln_g, l), _vec(conv_ln_b, l), dc)
        dp_conv, g_conv = _conv_bwd_taps(sv["p"], conv_full[l], dyc)
        stacked.setdefault("conv_w", [None] * L)[l] = g_conv.reshape(CONV_K + 1, N_CHIPS, cw // N_CHIPS).transpose(
            1, 0, 2).astype(WIRE_DTYPE)
        dp, small["q_latent_norm"][l], small["kv_latent_norm"][l] = _dp_assemble(
            sv["p"], _vec(q_latent_norm, l), _vec(kv_latent_norm, l), dcq, dckv, dp_conv, dpe, cw)
        dh = _contract("w_in_t", dp[None], w["w_in"], l, "nt")
        g_win = _wgrad("w_in_wgrad", sv["h"], dp[None], WIRE_DTYPE)[0]
        o_q, o_pe_col = 2 * cw, 2 * cw + ql + kvl + QK_NOPE
        g_win = jnp.concatenate([g_win[:, o_q:o_q + ql + kvl], g_win[:, o_pe_col:o_pe_col + QK_ROPE], g_win[:, :o_q]], axis=1)
        stacked.setdefault("w_in", [None] * L)[l] = g_win.reshape(d, N_CHIPS, -1).transpose(1, 0, 2)
        dx1, do1, small["mix_norm"][l] = _norm_bwd(sv["x1"], _vec(mix_norm, l), [dh], dx2, 0.5)
        dn1 = _ffn_bwd(sv["n1t"], sv["a1"], sv["b1"], sv["hm1"], do1, w, SHARDED[0:3], l, grads)
        dy, _, small["ffn1_norm"][l] = _norm_bwd(sv["x"], _vec(ffn1_norm, l), dn1, dx1, 1.0)
    grad_x = dy.reshape(x.shape)
    for n, per_layer in stacked.items():
        grads[n] = jnp.stack(per_layer, axis=0)

    part_sums = [grads[n] for n in SHARDED]
    swapped = _swap_layers(part_sums)
    chip_sums = [_add_sibling(place, g.reshape(2, -1, g.shape[-1]), t.reshape(-1, t.shape[-1])).reshape(t.shape)
                 for g, t in zip(part_sums, swapped)]
    from_chips = _scatter_to_owners(chip_sums)
    layer_sums = [_add_chips(place, own, others) for own, others in zip(chip_sums, from_chips)]
    full = dict(zip(SHARDED, _share_layers(layer_sums)))
    full["conv_w"] = full["conv_w"][:, :CONV_K]

    width = max(wts[n].shape[-1] for n in SMALL)
    pack = lambda vals: jnp.concatenate([_pad_lanes(v.reshape(-1, v.shape[-1]), width) for v in vals], axis=0)
    part = pack([jnp.concatenate(small[n], axis=0)[..., :wts[n].shape[-1]] for n in SMALL])
    rows = part.shape[0]
    part = jnp.pad(part, [(0, -rows % 8), (0, 0)])
    small_sum = _add_leading("add_devices", _gather_small(part), F32)
    for i, n in enumerate(SMALL):
        full[n] = small_sum[i * L:(i + 1) * L, :wts[n].shape[-1]]

    delta_w, new_m, new_v = {}, {}, {}
    for n in SHARDED:
        shape = wts[n].shape
        swap = shape[-1] % HEAD_PAD != 0 and shape[-2] % HEAD_PAD == 0
        view = (lambda t: jnp.swapaxes(t, 1, 2)) if swap else (lambda t: t)
        flat = lambda t: view(t).reshape(-1, view(t).shape[-1])
        r = flat(wts[n]).shape[0]
        fix = (lambda t: jnp.pad(flat(t), [(0, -r % 8), (0, 0)])) if r % 8 else flat
        outs = _adamw(fix(wts[n]), fix(full[n]), fix(mom[n]), fix(var[n]))
        delta_w[n], new_m[n], new_v[n] = [view(o[:r].reshape(view(wts[n]).shape)) for o in outs]
    pk = lambda src: jnp.pad(pack([src[n] for n in SMALL]), [(0, -rows % 8), (0, 0)])
    outs = _adamw(pk(wts), small_sum, pk(mom), pk(var))
    for i, n in enumerate(SMALL):
        delta_w[n], new_m[n], new_v[n] = [o[i * L:(i + 1) * L, :wts[n].shape[-1]] for o in outs]

    loss = lax.psum(loss_part[0, 0], ("x", "y", "c"))
    return (loss, grad_x, *[full[n] for n in ALL_WEIGHTS], *[delta_w[n] for n in ALL_WEIGHTS],
            *[new_m[n] for n in ALL_WEIGHTS], *[new_v[n] for n in ALL_WEIGHTS])
```

```python
import jax
import jax.numpy as jnp
from jax import lax
from jax.experimental import pallas as pl
from jax.experimental.pallas import tpu as pltpu

F32 = jnp.float32
MM_DTYPE = jnp.bfloat16
WIRE_DTYPE = jnp.bfloat16
EPS = 1e-6
N_HEADS = 8
QK_NOPE = 64
QK_ROPE = 32
QK_DIM = QK_NOPE + QK_ROPE
V_DIM = 64
HEAD_PAD = 128
HEADS_PER_STEP = 2
CHUNK_SHIFT = 6
LOG2E = 1.4426950408889634
CONV_K = 31
HALO = 32
ROPE_THETA = 10000.0
N_CHIPS = 4
ADAM_LR, ADAM_B1, ADAM_B2, ADAM_EPS, ADAM_WD, ADAM_STEP = 0.001, 0.9, 0.999, 1e-08, 0.01, 10
VMEM_LIMIT_BYTES = 56 * 2 ** 20
MESH = pl.DeviceIdType.MESH
ANY = pl.BlockSpec(memory_space=pl.ANY)

_DIMS = {
    "nn": (((1,), (0,)), ((), ())),
    "nt": (((1,), (1,)), ((), ())),
    "tn": (((0,), (0,)), ((), ())),
}


def _params(n_axes):
    return pltpu.CompilerParams(dimension_semantics=("arbitrary",) * n_axes, vmem_limit_bytes=VMEM_LIMIT_BYTES)


def _tile(n, cap):
    t = cap
    while n % t:
        t //= 2
    return t


def _mm(name, a, b, dims, grid, a_spec, b_spec, o_spec, out_shape, nk, acc_shape, into=None):
    n_axes = len(grid)

    def body(*refs):
        if into is not None:
            refs = refs[1:]
        a_ref, b_ref, o_ref = refs[:3]
        part = lax.dot_general(a_ref[...].astype(MM_DTYPE), b_ref[...].astype(MM_DTYPE), _DIMS[dims],
                               preferred_element_type=F32)
        if nk == 1:
            o_ref[...] = part.astype(o_ref.dtype)
        else:
            acc_ref = refs[3]
            k = pl.program_id(n_axes - 1)

            @pl.when(k == 0)
            def _():
                acc_ref[...] = part

            @pl.when(k > 0)
            def _():
                acc_ref[...] += part

            @pl.when(k == nk - 1)
            def _():
                o_ref[...] = acc_ref[...].astype(o_ref.dtype)

    in_specs = [a_spec, b_spec]
    args = [a, b]
    aliases = {}
    if into is not None:
        in_specs = [ANY] + in_specs
        args = [into] + args
        aliases = {0: 0}
    return pl.pallas_call(
        body, name=name, grid=grid, in_specs=in_specs, out_specs=o_spec, out_shape=out_shape,
        scratch_shapes=[] if nk == 1 else [pltpu.VMEM(acc_shape, F32)],
        input_output_aliases=aliases, compiler_params=_params(n_axes))(*args)


def _proj(name, a, w, l, dims, out_dtype=F32):
    s, k = a.shape
    g = w.shape[1]
    n = w.shape[3] if dims == "nn" else w.shape[2]
    tm = _tile(s, 512)
    return _mm(name, a, w, dims, (g, s // tm),
               pl.BlockSpec((tm, k), lambda j, i: (i, 0)),
               pl.BlockSpec((None, None) + w.shape[2:], lambda j, i: (l, j, 0, 0)),
               pl.BlockSpec((None, tm, n), lambda j, i: (j, i, 0)),
               jax.ShapeDtypeStruct((g, s, n), out_dtype), 1, None)


def _contract(name, a, w, l, dims, out_dtype=F32):
    g, s, kb = a.shape
    n = w.shape[3] if dims == "nn" else w.shape[2]
    tm = _tile(s, 512)
    return _mm(name, a, w, dims, (s // tm, g),
               pl.BlockSpec((None, tm, kb), lambda i, j: (j, i, 0)),
               pl.BlockSpec((None, None) + w.shape[2:], lambda i, j: (l, j, 0, 0)),
               pl.BlockSpec((tm, n), lambda i, j: (i, 0)),
               jax.ShapeDtypeStruct((s, n), out_dtype), g, (tm, n))


def _wgrad(name, a, b, out_dtype, layers=None, l=None, into=None):
    g = a.shape[0] if a.ndim == 3 else b.shape[0]
    s, k = a.shape[-2:]
    n = b.shape[-1]
    tm = _tile(s, 512)

    def spec(x, w):
        if x.ndim == 3:
            return pl.BlockSpec((None, tm, w), lambda j, i: (j, i, 0))
        return pl.BlockSpec((tm, w), lambda j, i: (i, 0))

    if layers is None:
        o_spec = pl.BlockSpec((None, k, n), lambda j, i: (j, 0, 0))
        out_shape = jax.ShapeDtypeStruct((g, k, n), out_dtype)
    else:
        o_spec = pl.BlockSpec((None, None, k, n), lambda j, i: (l, j, 0, 0))
        out_shape = jax.ShapeDtypeStruct((layers, g, k, n), out_dtype)
    return _mm(name, a, b, "tn", (g, s // tm), spec(a, k), spec(b, n), o_spec, out_shape, s // tm, (k, n), into=into)


def _ffn_up_fused(n, wg, wu, l):
    s, k = n.shape
    g, n4 = wg.shape[1], wg.shape[3]
    tm = _tile(s, 512)

    def body(n_ref, wg_ref, wu_ref, a_ref, b_ref, h_ref):
        nv = n_ref[...]
        a = jnp.dot(nv, wg_ref[...], preferred_element_type=F32)
        b = jnp.dot(nv, wu_ref[...], preferred_element_type=F32)
        a_ref[...] = a.astype(a_ref.dtype)
        b_ref[...] = b.astype(b_ref.dtype)
        h_ref[...] = (a * _sigmoid(a) * b).astype(h_ref.dtype)

    w_spec = pl.BlockSpec((None, None, k, n4), lambda j, i: (l, j, 0, 0))
    o_spec = pl.BlockSpec((None, tm, n4), lambda j, i: (j, i, 0))
    return pl.pallas_call(
        body, name="ffn_up_fused", grid=(g, s // tm),
        in_specs=[pl.BlockSpec((tm, k), lambda j, i: (i, 0)), w_spec, w_spec], out_specs=[o_spec] * 3,
        out_shape=[jax.ShapeDtypeStruct((g, s, n4), MM_DTYPE)] * 3, compiler_params=_params(2))(n, wg, wu)


def _ffn_down_t_fused(do, wd, a, b, l):
    s, k = do.shape
    g, n4 = wd.shape[1], wd.shape[2]
    tm = _tile(s, 512)

    def body(do_ref, wd_ref, a_ref, b_ref, da_ref, db_ref):
        dh = lax.dot_general(do_ref[...], wd_ref[...], _DIMS["nt"], preferred_element_type=F32)
        av = a_ref[...].astype(F32)
        sg = _sigmoid(av)
        da_ref[...] = (dh * b_ref[...].astype(F32) * (sg * (1.0 + av * (1.0 - sg)))).astype(da_ref.dtype)
        db_ref[...] = (dh * (av * sg)).astype(db_ref.dtype)

    t_spec = pl.BlockSpec((None, tm, n4), lambda j, i: (j, i, 0))
    return pl.pallas_call(
        body, name="ffn_down_t_fused", grid=(g, s // tm),
        in_specs=[pl.BlockSpec((tm, k), lambda j, i: (i, 0)),
                  pl.BlockSpec((None, None, n4, k), lambda j, i: (l, j, 0, 0)), t_spec, t_spec],
        out_specs=[t_spec] * 2, out_shape=[jax.ShapeDtypeStruct((g, s, n4), MM_DTYPE)] * 2,
        compiler_params=_params(2))(do, wd, a, b)


def _blocked_sum(name, acts, weights, l, dims):
    g, s, kb = acts[0].shape
    n = weights[0].shape[3] if dims == "nn" else weights[0].shape[2]
    tm = _tile(s, 256)
    n_pairs = len(acts)

    def body(*refs):
        o_ref = refs[2 * n_pairs]
        acc = None
        for p in range(n_pairs):
            for j in range(g):
                t = lax.dot_general(refs[p][j], refs[n_pairs + p][j], _DIMS[dims], preferred_element_type=F32)
                acc = t if acc is None else acc + t
        o_ref[...] = acc

    return pl.pallas_call(
        body, name=name, grid=(s // tm,),
        in_specs=[pl.BlockSpec((g, tm, kb), lambda i: (0, i, 0))] * n_pairs
        + [pl.BlockSpec((None,) + w.shape[1:], lambda i: (l, 0, 0, 0)) for w in weights],
        out_specs=pl.BlockSpec((tm, n), lambda i: (i, 0)), out_shape=jax.ShapeDtypeStruct((s, n), F32),
        compiler_params=_params(1))(*acts, *weights)


def _wgrad_pair(name, nt, da, db, layers, l, into):
    k, s = nt.shape
    g, _, n4 = da.shape
    tm = _tile(s, 512)
    nk = s // tm

    def body(*refs):
        if into is not None:
            refs = refs[2:]
        nt_ref, da_ref, db_ref, og_ref, ou_ref, accg, accu = refs
        i = pl.program_id(1)
        ntv = nt_ref[...]
        pg = jnp.dot(ntv, da_ref[...], preferred_element_type=F32)
        pu = jnp.dot(ntv, db_ref[...], preferred_element_type=F32)

        @pl.when(i == 0)
        def _():
            accg[...] = pg
            accu[...] = pu

        @pl.when(i > 0)
        def _():
            accg[...] += pg
            accu[...] += pu

        @pl.when(i == nk - 1)
        def _():
            og_ref[...] = accg[...].astype(og_ref.dtype)
            ou_ref[...] = accu[...].astype(ou_ref.dtype)

    t_spec = pl.BlockSpec((None, tm, n4), lambda j, i: (j, i, 0))
    o_spec = pl.BlockSpec((None, None, k, n4), lambda j, i: (l, j, 0, 0))
    in_specs = [pl.BlockSpec((k, tm), lambda j, i: (0, i)), t_spec, t_spec]
    args = [nt, da, db]
    aliases = {}
    if into is not None:
        in_specs = [ANY, ANY] + in_specs
        args = list(into) + args
        aliases = {0: 0, 1: 1}
    return pl.pallas_call(
        body, name=name, grid=(g, nk), in_specs=in_specs, out_specs=[o_spec] * 2,
        out_shape=[jax.ShapeDtypeStruct((layers, g, k, n4), WIRE_DTYPE)] * 2,
        scratch_shapes=[pltpu.VMEM((k, n4), F32)] * 2, input_output_aliases=aliases,
        compiler_params=_params(2))(*args)


def _rows(name, fn, s, tm, ins, outs, scratch=()):
    n = s // tm

    def spec(shape, kind):
        nd = len(shape)
        if kind == "r":
            return pl.BlockSpec(shape[:-2] + (tm, shape[-1]), lambda i: (0,) * (nd - 2) + (i, 0))
        if kind in ("f", "a"):
            return pl.BlockSpec(shape, lambda i: (0,) * nd)
        if kind == "t":
            return pl.BlockSpec((shape[0], tm), lambda i: (0, i))
        if kind == "b":
            return pl.BlockSpec((None,) + shape[1:], lambda i: (i, 0, 0))
        if kind == "p":
            return pl.BlockSpec((HALO, shape[-1]), lambda i: (jnp.maximum(i * (tm // HALO) - 1, 0), 0))
        if kind == "n":
            return pl.BlockSpec((HALO, shape[-1]), lambda i: (jnp.minimum((i + 1) * (tm // HALO), s // HALO - 1), 0))
        j = kind[1]
        return pl.BlockSpec((tm, HEAD_PAD), lambda i: (i, j))

    def body(*refs):
        fn(pl.program_id(0), n, *refs)

    return pl.pallas_call(
        body, name=name, grid=(n,),
        in_specs=[spec(a.shape, kind) for a, kind in ins],
        out_specs=[spec(shape, kind) for shape, _, kind in outs],
        out_shape=[jax.ShapeDtypeStruct(shape, dtype) for shape, dtype, _ in outs],
        scratch_shapes=list(scratch), compiler_params=_params(1))(*[a for a, _ in ins])


def _acc(ref, i, val):
    @pl.when(i == 0)
    def _():
        ref[...] = val

    @pl.when(i > 0)
    def _():
        ref[...] += val


def _sum0(x):
    return jnp.sum(x, axis=0, keepdims=True)


def _rstd(x, n):
    return lax.rsqrt(jnp.sum(x * x, axis=-1, keepdims=True) * (1.0 / n) + EPS)


def _rms_bwd(x, g, dy, n):
    r = _rstd(x, n)
    xh = x * r
    dyg = dy * g
    dx = r * (dyg - xh * (jnp.sum(dyg * xh, axis=-1, keepdims=True) * (1.0 / n)))
    return dx, _sum0(dy * xh)


def _sigmoid(x):
    return 1.0 / (1.0 + jnp.exp(-x))


def _norm_fwd(x, g):
    s, d = x.shape

    def fn(i, n, x_ref, g_ref, o_ref, ot_ref):
        xv = x_ref[...]
        y = xv * _rstd(xv, d) * g_ref[...]
        o_ref[...] = y.astype(o_ref.dtype)
        ot_ref[...] = y.T.astype(ot_ref.dtype)

    return _rows("norm_fwd", fn, s, _tile(s, 256), [(x, "r"), (g, "f")],
                 [((s, d), MM_DTYPE, "r"), ((d, s), MM_DTYPE, "t")])


def _resid_norm(x, adds, scale, g, norm_dtype, transposed=False):
    s, d = x.shape
    k = len(adds)

    def fn(i, n, *refs):
        x_ref, add_refs, g_ref, xo_ref, no_ref = refs[0], refs[1:1 + k], refs[1 + k], refs[2 + k], refs[3 + k]
        tot = add_refs[0][...]
        for r in add_refs[1:]:
            tot = tot + r[...]
        xv = x_ref[...] + scale * tot
        xo_ref[...] = xv
        y = xv * _rstd(xv, d) * g_ref[...]
        no_ref[...] = y.astype(no_ref.dtype)
        if transposed:
            refs[4 + k][...] = y.T.astype(refs[4 + k].dtype)

    return _rows("resid_norm", fn, s, _tile(s, 256), [(x, "r")] + [(a, "r") for a in adds] + [(g, "f")],
                 [((s, d), F32, "r"), ((s, d), norm_dtype, "r")] + ([((d, s), MM_DTYPE, "t")] if transposed else []))


def _norm_bwd(x, g, dn_parts, dres, out_scale):
    s, d = x.shape
    k = len(dn_parts)
    has_res = dres is not None

    def fn(i, n, *refs):
        x_ref, g_ref = refs[0], refs[1]
        dn_refs = refs[2:2 + k]
        pos = 2 + k
        dn = dn_refs[0][...]
        for r in dn_refs[1:]:
            dn = dn + r[...]
        dx, dg = _rms_bwd(x_ref[...], g_ref[...], dn, d)
        if has_res:
            dx = dx + refs[pos][...]
            pos += 1
        dx_ref, dxs_ref, dg_ref = refs[pos:pos + 3]
        dx_ref[...] = dx
        dxs_ref[...] = (out_scale * dx).astype(dxs_ref.dtype)
        _acc(dg_ref, i, dg)

    ins = [(x, "r"), (g, "f")] + [(p, "r") for p in dn_parts] + ([(dres, "r")] if has_res else [])
    return _rows("norm_bwd", fn, s, _tile(s, 256), ins,
                 [((s, d), F32, "r"), ((s, d), MM_DTYPE, "r"), ((1, d), F32, "a")])


def _final_fwd(x, o, g):
    return _resid_norm(x, [o], 0.5, g, F32)


def _loss(y, t):
    s, d = y.shape

    def fn(i, n, y_ref, t_ref, dy_ref, l_ref):
        e = y_ref[...] - t_ref[...]
        dy_ref[...] = e * (1.0 / d)
        _acc(l_ref, i, (0.5 / d) * jnp.sum(jnp.sum(e * e, axis=-1, keepdims=True), axis=0, keepdims=True))

    return _rows("loss", fn, s, _tile(s, 256), [(y, "r"), (t, "r")], [((s, d), F32, "r"), ((1, 1), F32, "a")])


def _mla_pre(p, gql, gkvl, cw):
    s = p.shape[0]
    ql, kvl = gql.shape[-1], gkvl.shape[-1]
    o_q, o_kv = 2 * cw, 2 * cw + ql

    def fn(i, n, p_ref, gq_ref, gkv_ref, cq_ref, ckv_ref, ckvt_ref):
        cq = p_ref[:, o_q:o_q + ql]
        cq_ref[...] = (cq * _rstd(cq, ql) * gq_ref[...]).astype(cq_ref.dtype)
        ckv = p_ref[:, o_kv:o_kv + kvl]
        y = ckv * _rstd(ckv, kvl) * gkv_ref[...]
        ckv_ref[...] = y.astype(ckv_ref.dtype)
        ckvt_ref[...] = y.T.astype(ckvt_ref.dtype)

    return _rows("mla_pre", fn, s, _tile(s, 256), [(p, "r"), (gql, "f"), (gkvl, "f")],
                 [((s, ql), MM_DTYPE, "r"), ((s, kvl), MM_DTYPE, "r"), ((kvl, s), MM_DTYPE, "t")])


def _rope(x, cosf, s1, s2):
    return x * cosf + pltpu.roll(x, HEAD_PAD - 16, 1) * s1 + pltpu.roll(x, 16, 1) * s2


def _rope_bwd(d, cosf, s1, s2):
    return d * cosf + pltpu.roll(d * s1, 16, 1) + pltpu.roll(d * s2, HEAD_PAD - 16, 1)


def _qk_post(q_pre, k_pre, p, pe_block, tabs, gq, gk):
    s, hw = q_pre.shape
    cosf, s1, s2 = tabs

    tm = _tile(s, 256)

    def fn(i, n, q_ref, k_ref, pe_ref, c_ref, s1_ref, s2_ref, gq_ref, gk_ref, qo_ref, ko_ref, kt_ref):
        c, a1, a2, pe = c_ref[...], s1_ref[...], s2_ref[...], pe_ref[...]
        for hh in range(N_HEADS):
            cols = slice(hh * HEAD_PAD, (hh + 1) * HEAD_PAD)
            qv = q_ref[:, cols]
            qo_ref[:, cols] = _rope(qv * _rstd(qv, QK_DIM) * gq_ref[...], c, a1, a2).astype(qo_ref.dtype)
            kv = k_ref[:, cols] + pe
            kout = _rope(kv * _rstd(kv, QK_DIM) * gk_ref[...], c, a1, a2)
            ko_ref[:, cols] = kout.astype(ko_ref.dtype)
            kt_ref[cols, :] = kout.T.astype(kt_ref.dtype)

    return _rows("qk_post", fn, s, tm,
                 [(q_pre, "r"), (k_pre, "r"), (p, ("c", pe_block)), (cosf, "r"), (s1, "r"), (s2, "r"), (gq, "f"), (gk, "f")],
                 [((s, hw), MM_DTYPE, "r"), ((s, hw), MM_DTYPE, "r"), ((s // tm, hw, tm), MM_DTYPE, "b")])


def _qk_post_bwd(q_pre, k_pre, p, pe_block, tabs, gq, gk, dq, dk):
    s, hw = q_pre.shape
    cosf, s1, s2 = tabs

    def fn(i, n, q_ref, k_ref, pe_ref, c_ref, s1_ref, s2_ref, gq_ref, gk_ref, dq_ref, dk_ref,
           dqo_ref, dko_ref, dpe_ref, dgq_ref, dgk_ref):
        c, a1, a2, pe = c_ref[...], s1_ref[...], s2_ref[...], pe_ref[...]
        lane = lax.broadcasted_iota(jnp.int32, pe.shape, 1)
        is_pe = (lane >= QK_NOPE) & (lane < QK_DIM)
        dpe = jnp.zeros(pe.shape, F32)
        dgq = jnp.zeros((1, HEAD_PAD), F32)
        dgk = jnp.zeros((1, HEAD_PAD), F32)
        for hh in range(N_HEADS):
            cols = slice(hh * HEAD_PAD, (hh + 1) * HEAD_PAD)
            dxq, g1 = _rms_bwd(q_ref[:, cols], gq_ref[...], _rope_bwd(dq_ref[cols, :].T, c, a1, a2), QK_DIM)
            dqo_ref[:, cols] = dxq.astype(dqo_ref.dtype)
            dxk, g2 = _rms_bwd(k_ref[:, cols] + pe, gk_ref[...], _rope_bwd(dk_ref[:, cols], c, a1, a2), QK_DIM)
            dko_ref[:, cols] = dxk.astype(dko_ref.dtype)
            dpe = dpe + dxk
            dgq = dgq + g1
            dgk = dgk + g2
        dpe_ref[...] = jnp.where(is_pe, dpe, 0.0)
        _acc(dgq_ref, i, dgq)
        _acc(dgk_ref, i, dgk)

    return _rows("qk_post_bwd", fn, s, _tile(s, 256),
                 [(q_pre, "r"), (k_pre, "r"), (p, ("c", pe_block)), (cosf, "r"), (s1, "r"), (s2, "r"), (gq, "f"), (gk, "f"),
                  (dq, "b"), (dk, "r")],
                 [((s, hw), MM_DTYPE, "r"), ((s, hw), MM_DTYPE, "r"), ((s, HEAD_PAD), F32, "r"),
                  ((1, HEAD_PAD), F32, "a"), ((1, HEAD_PAD), F32, "a")])


def _chunk_mask(tq, key_rows):
    r = jnp.right_shift(lax.broadcasted_iota(jnp.int32, (tq, tq), 0), CHUNK_SHIFT)
    c = jnp.right_shift(lax.broadcasted_iota(jnp.int32, (tq, tq), 1), CHUNK_SHIFT)
    return (r <= c) if key_rows else (c <= r)


def _attn_specs(s, tq):
    w = HEADS_PER_STEP * HEAD_PAD
    tile = pl.BlockSpec((tq, w), lambda hb, i: (i, hb))
    whole = pl.BlockSpec((s, w), lambda hb, i: (0, hb))
    rows = pl.BlockSpec((None, None, HEADS_PER_STEP, tq), lambda hb, i: (hb, i, 0, 0))
    blocks = pl.BlockSpec((s // tq, w, tq), lambda hb, i: (0, hb, 0))
    return tile, whole, rows, blocks


def _head(hh):
    return slice(hh * HEAD_PAD, (hh + 1) * HEAD_PAD)


def _attn_fwd(q, k, vt):
    s, hw = q.shape
    tq = _tile(s, 256)
    nq = s // tq
    groups = N_HEADS // HEADS_PER_STEP
    scale = QK_DIM ** -0.5
    tile, whole, rows, blocks = _attn_specs(s, tq)

    def body(q_ref, k_ref, vt_ref, o_ref, lse_ref):
        qi = pl.program_id(1)
        qv = [q_ref[:, _head(hh)] for hh in range(HEADS_PER_STEP)]

        def scores(kb):
            off = pl.multiple_of(kb * tq, tq)
            return tuple(lax.dot_general(k_ref[pl.ds(off, tq), _head(hh)], qv[hh], _DIMS["nt"],
                                         preferred_element_type=F32) for hh in range(HEADS_PER_STEP))

        def update(kb, stats, raw, diag):
            heads = range(HEADS_PER_STEP)
            mn, al, pr, ln = [], [], [], []
            for hh in heads:
                m, l, _ = stats[hh]
                sc = raw[hh] * (scale * LOG2E)
                if diag:
                    sc = jnp.where(_chunk_mask(tq, True), sc, -1e30)
                mn.append(jnp.maximum(m, jnp.max(sc, axis=0, keepdims=True)))
                al.append(jnp.exp2(m - mn[hh]))
                p = jnp.exp2(sc - mn[hh])
                ln.append(al[hh] * l + jnp.sum(p, axis=0, keepdims=True))
                pr.append(p.astype(MM_DTYPE))
            return tuple((mn[hh], ln[hh], al[hh] * stats[hh][2]
                          + jnp.dot(vt_ref[kb, _head(hh), :], pr[hh], preferred_element_type=F32)) for hh in heads)

        def step(kb, carry):
            stats, raw = carry
            nxt = scores(kb + 1)
            return update(kb, stats, raw, False), nxt

        init = tuple((jnp.full((1, tq), -1e30, F32), jnp.zeros((1, tq), F32), jnp.zeros((HEAD_PAD, tq), F32))
                     for _ in range(HEADS_PER_STEP))
        stats, raw = lax.fori_loop(0, qi, step, (init, scores(0)))
        carry = update(qi, stats, raw, True)
        for hh in range(HEADS_PER_STEP):
            m, l, acc = carry[hh]
            o_ref[:, _head(hh)] = (acc / l).T
            lse_ref[hh:hh + 1, :] = m + jnp.log(l) * LOG2E

    return pl.pallas_call(
        body, name="attn_fwd", grid=(groups, nq),
        in_specs=[tile, whole, blocks], out_specs=[tile, rows],
        out_shape=[jax.ShapeDtypeStruct((s, hw), F32), jax.ShapeDtypeStruct((groups, nq, HEADS_PER_STEP, tq), F32)],
        compiler_params=_params(2))(q, k, vt)


def _attn_delta(o, do):
    s, hw = o.shape
    tq = _tile(s, 256)
    groups = N_HEADS // HEADS_PER_STEP
    tile, _, rows, _ = _attn_specs(s, tq)

    def body(o_ref, do_ref, d_ref):
        for hh in range(HEADS_PER_STEP):
            prod = o_ref[:, _head(hh)] * do_ref[:, _head(hh)]
            d_ref[hh:hh + 1, :] = jnp.sum(prod.T, axis=0, keepdims=True)

    return pl.pallas_call(
        body, name="attn_delta", grid=(groups, s // tq), in_specs=[tile, tile], out_specs=rows,
        out_shape=jax.ShapeDtypeStruct((groups, s // tq, HEADS_PER_STEP, tq), F32),
        compiler_params=_params(2))(o, do)


def _attn_bwd(q, k, kt, v, do, lse_row, delta_row):
    s, hw = q.shape
    tq = _tile(s, 256)
    nq = s // tq
    groups = N_HEADS // HEADS_PER_STEP
    w = HEADS_PER_STEP * HEAD_PAD
    scale = QK_DIM ** -0.5
    tile, whole, _, blocks = _attn_specs(s, tq)
    rowvec = pl.BlockSpec((None, nq, HEADS_PER_STEP, tq), lambda hb, i: (hb, 0, 0, 0))
    kt_tile = pl.BlockSpec((None, w, tq), lambda hb, i: (i, hb, 0))

    def body(q_ref, k_ref, kt_ref, v_ref, do_ref, lse_ref, delta_ref, dqt_ref, dk_ref, dv_ref):
        ki = pl.program_id(1)
        kv = [k_ref[:, _head(hh)] for hh in range(HEADS_PER_STEP)]
        vv = [v_ref[:, _head(hh)] for hh in range(HEADS_PER_STEP)]
        ktv = [kt_ref[_head(hh), :] for hh in range(HEADS_PER_STEP)]

        @pl.when(ki == 0)
        def _():
            dqt_ref[...] = jnp.zeros(dqt_ref.shape, F32)

        def block(qb, carry, diag):
            off = pl.multiple_of(qb * tq, tq)
            heads = range(HEADS_PER_STEP)
            qs = [q_ref[pl.ds(off, tq), _head(hh)] for hh in heads]
            dos = [do_ref[pl.ds(off, tq), _head(hh)].astype(MM_DTYPE) for hh in heads]
            raw = [lax.dot_general(kv[hh], qs[hh], _DIMS["nt"], preferred_element_type=F32) for hh in heads]
            dpr = [lax.dot_general(vv[hh], dos[hh], _DIMS["nt"], preferred_element_type=F32) for hh in heads]
            pr, ds = [], []
            for hh in heads:
                sc = raw[hh] * (scale * LOG2E)
                if diag:
                    sc = jnp.where(_chunk_mask(tq, True), sc, -1e30)
                p = jnp.exp2(sc - lse_ref[qb, hh:hh + 1, :])
                pr.append(p.astype(MM_DTYPE))
                ds.append((p * (dpr[hh] - delta_ref[qb, hh:hh + 1, :]) * scale).astype(MM_DTYPE))
            out = []
            for hh in heads:
                dk, dvv = carry[hh]
                dvv = dvv + jnp.dot(pr[hh], dos[hh], preferred_element_type=F32)
                dk = dk + jnp.dot(ds[hh], qs[hh], preferred_element_type=F32)
                dqt_ref[qb, _head(hh), :] += jnp.dot(ktv[hh], ds[hh], preferred_element_type=F32)
                out.append((dk, dvv))
            return tuple(out)

        init = tuple((jnp.zeros((tq, HEAD_PAD), F32), jnp.zeros((tq, HEAD_PAD), F32)) for _ in range(HEADS_PER_STEP))
        carry = lax.fori_loop(ki + 1, nq, lambda qb, c: block(qb, c, False), block(ki, init, True))
        for hh in range(HEADS_PER_STEP):
            dk_ref[:, _head(hh)] = carry[hh][0]
            dv_ref[:, _head(hh)] = carry[hh][1].astype(dv_ref.dtype)

    return pl.pallas_call(
        body, name="attn_bwd", grid=(groups, nq),
        in_specs=[whole, tile, kt_tile, tile, whole, rowvec, rowvec], out_specs=[blocks, tile, tile],
        out_shape=[jax.ShapeDtypeStruct((nq, hw, tq), F32), jax.ShapeDtypeStruct((s, hw), F32),
                   jax.ShapeDtypeStruct((s, hw), MM_DTYPE)],
        compiler_params=_params(2))(q, k, kt, v, do, lse_row, delta_row)


def _glu_with_halo(i, p_ref, ph_ref, ubuf, cw, tm):
    a, g = p_ref[:, 0:cw], p_ref[:, cw:2 * cw]
    ubuf[HALO:HALO + tm, :] = a * _sigmoid(g)
    ah, gh = ph_ref[:, 0:cw], ph_ref[:, cw:2 * cw]
    ubuf[0:HALO, :] = jnp.where(i > 0, ah * _sigmoid(gh), 0.0)
    return a, g


def _conv_taps(ubuf, w_ref, tm):
    base = HALO - (CONV_K - 1)
    y = w_ref[0:1, :] * ubuf[base:base + tm, :]
    for k in range(1, CONV_K):
        y = y + w_ref[k:k + 1, :] * ubuf[base + k:base + k + tm, :]
    return y


def _layer_norm_stats(y, cw):
    mu = jnp.sum(y, axis=-1, keepdims=True) * (1.0 / cw)
    yc = y - mu
    r = lax.rsqrt(jnp.sum(yc * yc, axis=-1, keepdims=True) * (1.0 / cw) + EPS)
    return yc * r, r


def _conv_fwd(p, w, cb, lg, lb):
    s = p.shape[0]
    cw = cb.shape[-1]
    tm = _tile(s, 256)

    def fn(i, n, p_ref, ph_ref, w_ref, cb_ref, lg_ref, lb_ref, o_ref, ubuf):
        _glu_with_halo(i, p_ref, ph_ref, ubuf, cw, tm)
        yh, _ = _layer_norm_stats(_conv_taps(ubuf, w_ref, tm) + cb_ref[...], cw)
        z = yh * lg_ref[...] + lb_ref[...]
        o_ref[...] = (z * _sigmoid(z)).astype(o_ref.dtype)

    return _rows("conv_fwd", fn, s, tm, [(p, "r"), (p, "p"), (w, "f"), (cb, "f"), (lg, "f"), (lb, "f")],
                 [((s, cw), MM_DTYPE, "r")], scratch=[pltpu.VMEM((tm + HALO, cw), F32)])[0]


def _conv_bwd_ln(p, w, cb, lg, lb, dc):
    s = p.shape[0]
    cw = cb.shape[-1]
    tm = _tile(s, 256)

    def fn(i, n, p_ref, ph_ref, w_ref, cb_ref, lg_ref, lb_ref, dc_ref, dy_ref, dlg_ref, dlb_ref, dcb_ref, ubuf):
        _glu_with_halo(i, p_ref, ph_ref, ubuf, cw, tm)
        yh, r = _layer_norm_stats(_conv_taps(ubuf, w_ref, tm) + cb_ref[...], cw)
        z = yh * lg_ref[...] + lb_ref[...]
        sg = _sigmoid(z)
        dz = dc_ref[...] * (sg * (1.0 + z * (1.0 - sg)))
        dyh = dz * lg_ref[...]
        m1 = jnp.sum(dyh, axis=-1, keepdims=True) * (1.0 / cw)
        m2 = jnp.sum(dyh * yh, axis=-1, keepdims=True) * (1.0 / cw)
        dy = r * (dyh - m1 - yh * m2)
        dy_ref[...] = dy
        _acc(dlg_ref, i, _sum0(dz * yh))
        _acc(dlb_ref, i, _sum0(dz))
        _acc(dcb_ref, i, _sum0(dy))

    return _rows("conv_bwd_ln", fn, s, tm,
                 [(p, "r"), (p, "p"), (w, "f"), (cb, "f"), (lg, "f"), (lb, "f"), (dc, "r")],
                 [((s, cw), F32, "r"), ((1, cw), F32, "a"), ((1, cw), F32, "a"), ((1, cw), F32, "a")],
                 scratch=[pltpu.VMEM((tm + HALO, cw), F32)])


def _conv_bwd_taps(p, w, dy):
    s = p.shape[0]
    cw = w.shape[-1]
    tm = _tile(s, 256)

    def fn(i, n, p_ref, ph_ref, w_ref, dy_ref, dyn_ref, dp_ref, dw_ref, ubuf, dybuf):
        a, g = _glu_with_halo(i, p_ref, ph_ref, ubuf, cw, tm)
        dyv = dy_ref[...]
        dybuf[0:tm, :] = dyv
        dybuf[tm:tm + HALO, :] = jnp.where(i < n - 1, dyn_ref[...], 0.0)
        base = HALO - (CONV_K - 1)
        du = jnp.zeros((tm, cw), F32)

        @pl.when(i == 0)
        def _():
            dw_ref[...] = jnp.zeros(dw_ref.shape, F32)

        for k in range(CONV_K):
            sh = CONV_K - 1 - k
            du = du + w_ref[k:k + 1, :] * dybuf[sh:sh + tm, :]
            dw_ref[k:k + 1, :] += _sum0(dyv * ubuf[base + k:base + k + tm, :])
        sg = _sigmoid(g)
        dp_ref[:, 0:cw] = (du * sg).astype(dp_ref.dtype)
        dp_ref[:, cw:2 * cw] = (du * a * sg * (1.0 - sg)).astype(dp_ref.dtype)

    return _rows("conv_bwd_taps", fn, s, tm, [(p, "r"), (p, "p"), (w, "f"), (dy, "r"), (dy, "n")],
                 [((s, 2 * cw), MM_DTYPE, "r"), ((CONV_K + 1, cw), F32, "a")],
                 scratch=[pltpu.VMEM((tm + HALO, cw), F32), pltpu.VMEM((tm + HALO, cw), F32)])


def _dp_assemble(p, gql, gkvl, dcq, dckv_parts, dp_conv, dpe, cw):
    s, width = p.shape
    ql, kvl = gql.shape[-1], gkvl.shape[-1]
    o_q, o_kv, o_pe = 2 * cw, 2 * cw + ql, 2 * cw + ql + kvl

    def fn(i, n, p_ref, gq_ref, gkv_ref, dcq_ref, dk1_ref, dk2_ref, dconv_ref, dpe_ref, dp_ref, dgq_ref, dgkv_ref):
        dp_ref[:, 0:o_q] = dconv_ref[...]
        dx, dg = _rms_bwd(p_ref[:, o_q:o_kv], gq_ref[...], dcq_ref[...], ql)
        dp_ref[:, o_q:o_kv] = dx.astype(dp_ref.dtype)
        _acc(dgq_ref, i, dg)
        dx, dg = _rms_bwd(p_ref[:, o_kv:o_pe], gkv_ref[...], dk1_ref[...] + dk2_ref[...], kvl)
        dp_ref[:, o_kv:o_pe] = dx.astype(dp_ref.dtype)
        _acc(dgkv_ref, i, dg)
        dp_ref[:, o_pe:width] = dpe_ref[...].astype(dp_ref.dtype)

    return _rows("dp_assemble", fn, s, _tile(s, 256),
                 [(p, "r"), (gql, "f"), (gkvl, "f"), (dcq, "r"), (dckv_parts[0], "r"), (dckv_parts[1], "r"),
                  (dp_conv, "r"), (dpe, "r")],
                 [((s, width), MM_DTYPE, "r"), ((1, ql), F32, "a"), ((1, kvl), F32, "a")])


def _add_leading(name, x, out_dtype):
    n, r, c = x.shape
    tm = _tile(r, 256)

    def fn(i, nt, x_ref, o_ref):
        tot = x_ref[0].astype(F32)
        for k in range(1, n):
            tot = tot + x_ref[k].astype(F32)
        o_ref[...] = tot.astype(o_ref.dtype)

    return _rows(name, fn, r, tm, [(x, "r")], [((r, c), out_dtype, "r")])[0]


def _prefetch_call(name, fn, idx, grid, in_specs, out_spec, out_shape, args, aliases=None):
    return pl.pallas_call(
        fn, name=name, out_shape=out_shape,
        grid_spec=pltpu.PrefetchScalarGridSpec(num_scalar_prefetch=1, grid=grid, in_specs=in_specs, out_specs=out_spec),
        input_output_aliases=aliases or {}, compiler_params=_params(len(grid)))(idx, *args)


def _adamw(w, g, m, v):
    r, c = w.shape
    tm = _tile(r, 256)

    def fn(i, n, w_ref, g_ref, m_ref, v_ref, d_ref, mo_ref, vo_ref):
        gv = g_ref[...]
        mn = ADAM_B1 * m_ref[...] + (1.0 - ADAM_B1) * gv
        vn = ADAM_B2 * v_ref[...] + (1.0 - ADAM_B2) * (gv * gv)
        mo_ref[...] = mn
        vo_ref[...] = vn
        m_hat = mn / (1.0 - ADAM_B1 ** ADAM_STEP)
        v_hat = vn / (1.0 - ADAM_B2 ** ADAM_STEP)
        d_ref[...] = -ADAM_LR * (m_hat / (jnp.sqrt(v_hat) + ADAM_EPS) + ADAM_WD * w_ref[...])

    return _rows("adamw", fn, r, tm, [(w, "r"), (g, "r"), (m, "r"), (v, "r")], [((r, c), F32, "r")] * 3)


def _place():
    x, y, c = lax.axis_index("x"), lax.axis_index("y"), lax.axis_index("c")
    other_chips = [(1 - x, y), (x, 1 - y), (1 - x, 1 - y)]
    return x, y, c, 2 * x + y, other_chips


def _gather_small(vec):
    flips = [(fx, fy, fc) for fx in (0, 1) for fy in (0, 1) for fc in (0, 1)][1:]

    def body(src, dst, send, recv, lsem):
        x, y, c, _, _ = _place()
        me = 4 * x + 2 * y + c
        local = pltpu.make_async_copy(src, dst.at[me], lsem)
        local.start()
        away = []
        for j, (fx, fy, fc) in enumerate(flips):
            px, py, pc = x ^ fx, y ^ fy, c ^ fc
            away.append(pltpu.make_async_remote_copy(
                src_ref=src, dst_ref=dst.at[me], send_sem=send.at[j], recv_sem=recv.at[j],
                device_id=(px, py, pc), device_id_type=MESH))
        for cp in away:
            cp.start()
        for j, (fx, fy, fc) in enumerate(flips):
            blk = dst.at[4 * (x ^ fx) + 2 * (y ^ fy) + (c ^ fc)]
            pltpu.make_async_remote_copy(
                src_ref=blk, dst_ref=blk, send_sem=send.at[j], recv_sem=recv.at[j],
                device_id=(x ^ fx, y ^ fy, c ^ fc), device_id_type=MESH).wait_recv()
        for cp in away:
            cp.wait_send()
        local.wait()

    return pl.pallas_call(
        body, name="gather_small", in_specs=[ANY], out_specs=ANY,
        out_shape=jax.ShapeDtypeStruct((8,) + vec.shape, vec.dtype),
        scratch_shapes=[pltpu.SemaphoreType.DMA((7,)), pltpu.SemaphoreType.DMA((7,)), pltpu.SemaphoreType.DMA(())],
        compiler_params=pltpu.CompilerParams(has_side_effects=True))(vec)


HBM = pl.BlockSpec(memory_space=pltpu.HBM)
SEM = pl.BlockSpec(memory_space=pltpu.SEMAPHORE)
EFFECT = pltpu.SideEffectType.DATAFLOW_SIDE_EFFECTING


def _exchange(name, mode, arrays, sem_shape, plan, sems=None, after=None):
    n = len(arrays)
    hbm = [pltpu.HBM(a.shape, a.dtype) for a in arrays]
    arrays = [pltpu.with_memory_space_constraint(a, pltpu.HBM) for a in arrays]
    if mode == "run":
        def body(*refs):
            outs, send, recv = refs[n:2 * n], refs[2 * n], refs[2 * n + 1]
            sends, recvs = plan(outs, send, recv)
            for cp in sends:
                cp.start()
            for cp in recvs:
                cp.wait_recv()
            for cp in sends:
                cp.wait_send()

        return list(pl.pallas_call(
            body, name=name, in_specs=[HBM] * n, out_specs=[HBM] * n, out_shape=hbm,
            input_output_aliases={i: i for i in range(n)},
            scratch_shapes=[pltpu.SemaphoreType.DMA(sem_shape), pltpu.SemaphoreType.DMA(sem_shape)],
            compiler_params=pltpu.CompilerParams(has_side_effects=EFFECT))(*arrays))
    if mode == "start":
        def body(*refs):
            send, recv, token, outs = refs[n], refs[n + 1], refs[n + 2], refs[n + 3:]
            sends, _ = plan(outs, send, recv)
            for cp in sends:
                cp.start()
            token[...] = jnp.zeros(token.shape, F32)

        res = pl.pallas_call(
            body, name=name, in_specs=[HBM] * n,
            out_specs=[SEM, SEM, pl.BlockSpec(memory_space=pltpu.VMEM)] + [HBM] * n,
            out_shape=[pltpu.SemaphoreType.DMA(sem_shape), pltpu.SemaphoreType.DMA(sem_shape),
                       jax.ShapeDtypeStruct((8, HEAD_PAD), F32)] + hbm,
            input_output_aliases={i: i + 3 for i in range(n)},
            compiler_params=pltpu.CompilerParams(has_side_effects=EFFECT))(*arrays)
        return (res[0], res[1]), res[2], list(res[3:])

    def body(*refs):
        ins, send, recv = refs[:n], refs[n], refs[n + 1]
        sends, recvs = plan(ins, send, recv)
        for cp in sends:
            cp.wait_send()
        for cp in recvs:
            cp.wait_recv()

    return list(pl.pallas_call(
        body, name=name, in_specs=[HBM] * n + [SEM, SEM, ANY], out_specs=[HBM] * n, out_shape=hbm,
        input_output_aliases={i: i for i in range(n)},
        compiler_params=pltpu.CompilerParams(has_side_effects=EFFECT))(*arrays, sems[0], sems[1], after))


def _rows_half(ref_rows, half):
    h = ref_rows // 2
    return pl.ds(pl.multiple_of(half * h, 8), h)


def _remote(src, dst, send_sem, recv_sem, device):
    return pltpu.make_async_remote_copy(src_ref=src, dst_ref=dst, send_sem=send_sem, recv_sem=recv_sem,
                                        device_id=device, device_id_type=MESH)


def _plan_gather_ici(k_n, l):
    def plan(refs, send, recv):
        x, y, c, me, chips = _place()
        sends, recvs = [], []
        for k in range(k_n):
            src, buf = refs[k], refs[k_n + k]
            rows = _rows_half(src.shape[1], c)
            for j, (cx, cy) in enumerate(chips):
                sends.append(_remote(src.at[l, rows], buf.at[0, me, rows], send.at[3 * k + j], recv.at[3 * k + j], (cx, cy, c)))
                got = buf.at[0, 2 * cx + cy, rows]
                recvs.append(_remote(got, got, send.at[3 * k + j], recv.at[3 * k + j], (cx, cy, c)))
        return sends, recvs
    return plan


def _plan_gather_d2d(k_n):
    def plan(refs, send, recv):
        x, y, c, me, chips = _place()
        sends, recvs = [], []
        for k in range(k_n):
            buf = refs[k]
            for j, (cx, cy) in enumerate(chips):
                mine = buf.at[0, 2 * cx + cy, _rows_half(buf.shape[2], c)]
                theirs = buf.at[0, 2 * cx + cy, _rows_half(buf.shape[2], 1 - c)]
                sends.append(_remote(mine, mine, send.at[3 * k + j], recv.at[3 * k + j], (x, y, 1 - c)))
                recvs.append(_remote(theirs, theirs, send.at[3 * k + j], recv.at[3 * k + j], (x, y, 1 - c)))
        return sends, recvs
    return plan


def _plan_swap(k_n):
    def plan(refs, send, recv):
        x, y, c, me, chips = _place()
        sends, recvs = [], []
        for k in range(k_n):
            src, land = refs[k], refs[k_n + k]
            sends.append(_remote(src.at[:, _rows_half(src.shape[1], 1 - c)], land, send.at[k], recv.at[k], (x, y, 1 - c)))
            recvs.append(_remote(land, land, send.at[k], recv.at[k], (x, y, 1 - c)))
        return sends, recvs
    return plan


def _plan_scatter(k_n):
    def plan(refs, send, recv):
        x, y, c, me, chips = _place()
        sends, recvs = [], []
        for k in range(k_n):
            src, land = refs[k], refs[k_n + k]
            for j, (cx, cy) in enumerate(chips):
                sends.append(_remote(src.at[2 * cx + cy], land.at[j], send.at[3 * k + j], recv.at[3 * k + j], (cx, cy, c)))
                recvs.append(_remote(land.at[j], land.at[j], send.at[3 * k + j], recv.at[3 * k + j], (cx, cy, c)))
        return sends, recvs
    return plan


def _plan_share(k_n, layers):
    def plan(refs, send, recv):
        x, y, c, me, chips = _place()
        sends, recvs = [], []
        for k in range(k_n):
            out = refs[k]
            for l in range(layers):
                mine = out.at[l, _rows_half(out.shape[1], c)]
                theirs = out.at[l, _rows_half(out.shape[1], 1 - c)]
                sends.append(_remote(mine, mine, send.at[layers * k + l], recv.at[layers * k + l], (x, y, 1 - c)))
                recvs.append(_remote(theirs, theirs, send.at[layers * k + l], recv.at[layers * k + l], (x, y, 1 - c)))
        return sends, recvs
    return plan


def _add_sibling_half(idx, mine, theirs):
    q, r, c = mine.shape
    h = r // 2
    tm = _tile(h, 256)

    def fn(idx_ref, a_ref, b_ref, o_ref):
        o_ref[...] = (a_ref[...].astype(F32) + b_ref[...].astype(F32)).astype(o_ref.dtype)

    return _prefetch_call(
        "add_sibling", fn, idx, (q, h // tm),
        [pl.BlockSpec((None, None, tm, c), lambda b, i, ix: (b, ix[0], i, 0)),
         pl.BlockSpec((None, tm, c), lambda b, i, ix: (b, i, 0))],
        pl.BlockSpec((None, tm, c), lambda b, i, ix: (b, i, 0)), jax.ShapeDtypeStruct((q, h, c), WIRE_DTYPE),
        [mine.reshape(q, 2, h, c), theirs])


def _add_chips_half(idx, own, others, layers, l, into):
    _, h, c = own.shape
    tm = _tile(h, 256)

    def fn(*refs):
        a_ref, b_ref, o_ref = refs[-3:]
        tot = a_ref[...].astype(F32)
        for k in range(3):
            tot = tot + b_ref[k].astype(F32)
        o_ref[...] = tot

    in_specs = [pl.BlockSpec((None, tm, c), lambda i, ix: (ix[1], i, 0)), pl.BlockSpec((3, tm, c), lambda i, ix: (0, i, 0))]
    args = [own, others]
    aliases = None
    if into is not None:
        in_specs = [ANY] + in_specs
        args = [into.reshape(layers, 2, h, c)] + args
        aliases = {1: 0}
    out = _prefetch_call(
        "add_chips", fn, idx, (h // tm,), in_specs,
        pl.BlockSpec((None, None, tm, c), lambda i, ix: (l, ix[0], i, 0)),
        jax.ShapeDtypeStruct((layers, 2, h, c), F32), args, aliases)
    return out.reshape(layers, 2 * h, c)


SHARDED = ["ffn1_w_gate", "ffn1_w_up", "ffn1_w_down", "w_in", "w_uq", "w_ukv", "conv_w", "w_out",
           "ffn2_w_gate", "ffn2_w_up", "ffn2_w_down"]
SMALL = ["ffn1_norm", "mix_norm", "q_latent_norm", "kv_latent_norm", "q_norm", "k_norm", "conv_b", "conv_ln_g",
         "conv_ln_b", "ffn2_norm", "post_norm"]
ALL_WEIGHTS = ["ffn1_norm", "ffn1_w_gate", "ffn1_w_up", "ffn1_w_down", "mix_norm", "w_in", "q_latent_norm", "w_uq",
               "kv_latent_norm", "w_ukv", "q_norm", "k_norm", "conv_w", "conv_b", "conv_ln_g", "conv_ln_b", "w_out",
               "ffn2_norm", "ffn2_w_gate", "ffn2_w_up", "ffn2_w_down", "post_norm"]


def _rope_tables(s):
    pos = jnp.arange(s, dtype=F32)
    inv_freq = 1.0 / (ROPE_THETA ** (jnp.arange(0, QK_ROPE, 2, dtype=F32) / QK_ROPE))
    ang = pos[:, None] * inv_freq[None, :]
    cos, sin = jnp.cos(ang), jnp.sin(ang)
    half = QK_ROPE // 2
    one = jnp.ones((s, QK_NOPE), F32)
    zero = jnp.zeros((s, QK_NOPE), F32)
    pad = HEAD_PAD - QK_DIM
    z16 = jnp.zeros((s, half), F32)
    cosf = jnp.concatenate([one, cos, cos, jnp.ones((s, pad), F32)], axis=1)
    s1 = jnp.concatenate([zero, -sin, z16, jnp.zeros((s, pad), F32)], axis=1)
    s2 = jnp.concatenate([zero, z16, sin, jnp.zeros((s, pad), F32)], axis=1)
    return cosf, s1, s2


def _ffn_fwd(n, w, names, l):
    a, b, hm = _ffn_up_fused(n, w[names[0]], w[names[1]], l)
    return a, b, hm, _blocked_sum("ffn_down", [hm], [w[names[2]]], l, "nn")


def _ffn_bwd(nt, a, b, hm, do, w, names, grads):
    gate, up, down = names
    da, db = _ffn_down_t_fused(do, w[down], a, b, 0)
    grads[down] = _wgrad("ffn_down_wgrad", hm, do, WIRE_DTYPE)
    g_gate, g_up = _wgrad_pair("ffn_up_wgrad", nt, da, db, 1, 0, None)
    grads[gate], grads[up] = g_gate[0], g_up[0]
    return [_blocked_sum("ffn_up_t", [da, db], [w[gate], w[up]], 0, "nt")]


def _vec(v, l):
    return v[l][None, :]


def _pad_lanes(v, width):
    return jnp.pad(v, [(0, 0)] * (v.ndim - 1) + [(0, width - v.shape[-1])])


def kernel(x, ffn1_norm, ffn1_w_gate, ffn1_w_up, ffn1_w_down, mix_norm, w_in, q_latent_norm, w_uq, kv_latent_norm, w_ukv, q_norm, k_norm, conv_w, conv_b, conv_ln_g, conv_ln_b, w_out, ffn2_norm, ffn2_w_gate, ffn2_w_up, ffn2_w_down, post_norm, loss_target, m_ffn1_norm, m_ffn1_w_gate, m_ffn1_w_up, m_ffn1_w_down, m_mix_norm, m_w_in, m_q_latent_norm, m_w_uq, m_kv_latent_norm, m_w_ukv, m_q_norm, m_k_norm, m_conv_w, m_conv_b, m_conv_ln_g, m_conv_ln_b, m_w_out, m_ffn2_norm, m_ffn2_w_gate, m_ffn2_w_up, m_ffn2_w_down, m_post_norm, v_ffn1_norm, v_ffn1_w_gate, v_ffn1_w_up, v_ffn1_w_down, v_mix_norm, v_w_in, v_q_latent_norm, v_w_uq, v_kv_latent_norm, v_w_ukv, v_q_norm, v_k_norm, v_conv_w, v_conv_b, v_conv_ln_g, v_conv_ln_b, v_w_out, v_ffn2_norm, v_ffn2_w_gate, v_ffn2_w_up, v_ffn2_w_down, v_post_norm):
    given = dict(locals())
    wts = {n: given[n] for n in ALL_WEIGHTS}
    mom = {n: given["m_" + n] for n in ALL_WEIGHTS}
    var = {n: given["v_" + n] for n in ALL_WEIGHTS}
    n_layers = ffn1_norm.shape[0]
    s, d = x.shape[1], x.shape[2]
    ql, kvl, cw = q_latent_norm.shape[-1], kv_latent_norm.shape[-1], conv_b.shape[-1]
    xs = x.reshape(s, d)
    target = loss_target.reshape(s, d)

    assert n_layers == 2, "the layers' exchanges run one behind the other"
    local = [jnp.pad(wts[n], ((0, 0), (0, 1), (0, 0))).astype(MM_DTYPE) if n == "conv_w" else wts[n].astype(MM_DTYPE)
             for n in SHARDED]
    L = n_layers
    k_n = len(SHARDED)
    my_c = lax.axis_index("c")
    my_chip = 2 * lax.axis_index("x") + lax.axis_index("y")
    place = jnp.stack([my_c, my_chip]).astype(jnp.int32)
    hw = N_HEADS * HEAD_PAD
    tabs = _rope_tables(s)
    pe_block = (2 * cw + ql + kvl) // HEAD_PAD
    gq_pad, gk_pad = _pad_lanes(q_norm, HEAD_PAD), _pad_lanes(k_norm, HEAD_PAD)
    tq = _tile(s, 256)

    def own_blocks(l):
        return [lax.dynamic_update_slice(lax.empty((1, N_CHIPS) + t.shape[1:], t.dtype), t[l][None, None],
                                         (0, my_chip, 0, 0)) for t in local]

    def layouts(bufs):
        w = dict(zip(SHARDED, bufs))
        win = w["w_in"].transpose(0, 2, 1, 3).reshape(1, d, -1)
        zc = lambda n: jnp.zeros((1, d, n), MM_DTYPE)
        o_pe, o_cv = ql + kvl, ql + kvl + QK_ROPE
        w["w_in"] = jnp.concatenate([win[..., o_cv:], win[..., :o_pe], zc(QK_NOPE), win[..., o_pe:o_cv],
                                     zc(HEAD_PAD - QK_DIM)], axis=-1)[:, None]
        wq = w["w_uq"].transpose(0, 2, 1, 3).reshape(1, ql, N_HEADS, QK_DIM)
        w["wq"] = _pad_lanes(wq, HEAD_PAD).reshape(1, 1, ql, hw)
        wkv = w["w_ukv"].transpose(0, 2, 1, 3).reshape(1, kvl, N_HEADS, QK_NOPE + V_DIM)
        w["wk"] = _pad_lanes(wkv[..., :QK_NOPE], HEAD_PAD).reshape(1, 1, kvl, hw)
        w["wv"] = _pad_lanes(wkv[..., QK_NOPE:], HEAD_PAD).reshape(1, 1, kvl, hw)
        w["wvt"] = w["wv"][:, 0].transpose(0, 2, 1)
        w["conv"] = w["conv_w"].transpose(0, 2, 1, 3).reshape(CONV_K + 1, cw).astype(F32)
        wout = w["w_out"].reshape(1, d, d)
        wo_a = wout[:, :N_HEADS * V_DIM].reshape(1, N_HEADS, V_DIM, d)
        w["wo_a"] = jnp.pad(wo_a, ((0, 0), (0, 0), (0, HEAD_PAD - V_DIM), (0, 0))).reshape(1, 1, hw, d)
        w["wo_c"] = wout[:, N_HEADS * V_DIM:][:, None]
        return w

    ici, d2d, sem_kj = (lambda l: _plan_gather_ici(k_n, l)), _plan_gather_d2d(k_n), (3 * k_n,)
    landed = _exchange("gather0_ici", "run", local + own_blocks(0), sem_kj, ici(0))
    w0 = layouts(_exchange("gather0_d2d", "run", landed[k_n:], sem_kj, d2d))
    sems_a, token_a, in_flight = _exchange("gather1_ici_start", "start", landed[:k_n] + own_blocks(1), sem_kj, ici(1))

    def fwd_first(l, cur, w, tok):
        sv = {"x": cur}
        n1, sv["n1t"] = _norm_fwd(cur, _vec(ffn1_norm, l) + tok)
        sv["a1"], sv["b1"], sv["hm1"], o1 = _ffn_fwd(n1, w, SHARDED[0:3], 0)
        sv["x1"], sv["h"] = _resid_norm(cur, [o1], 0.5, _vec(mix_norm, l), MM_DTYPE)
        sv["p"] = _proj("w_in", sv["h"], w["w_in"], 0, "nn")[0]
        sv["cq"], sv["ckv"], ckvt = _mla_pre(sv["p"], _vec(q_latent_norm, l), _vec(kv_latent_norm, l), cw)
        sv["q_pre"] = _proj("wq", sv["cq"], w["wq"], 0, "nn")[0]
        sv["k_pre"] = _proj("wk", sv["ckv"], w["wk"], 0, "nn")[0]
        sv["v"] = _proj("wv", sv["ckv"], w["wv"], 0, "nn", MM_DTYPE)[0]
        sv["vt"] = _mm("wv_transposed", w["wvt"], ckvt, "nn", (s // tq,),
                       pl.BlockSpec((None, hw, kvl), lambda i: (0, 0, 0)), pl.BlockSpec((kvl, tq), lambda i: (0, i)),
                       pl.BlockSpec((None, hw, tq), lambda i: (i, 0, 0)),
                       jax.ShapeDtypeStruct((s // tq, hw, tq), MM_DTYPE), 1, None)
        sv["q"], sv["k"], sv["kt"] = _qk_post(sv["q_pre"], sv["k_pre"], sv["p"], pe_block, tabs,
                                              _vec(gq_pad, l), _vec(gk_pad, l))
        return sv

    def fwd_second(l, sv, w, tok):
        sv["c"] = _conv_fwd(sv["p"], w["conv"], _vec(conv_b, l) + tok, _vec(conv_ln_g, l), _vec(conv_ln_b, l))
        sv["o"], sv["lse"] = _attn_fwd(sv["q"], sv["k"], sv.pop("vt"))
        mix_a = _contract("wo_a", sv["o"][None], w["wo_a"], 0, "nn")
        mix_c = _contract("wo_c", sv["c"][None], w["wo_c"], 0, "nn")
        sv["x2"], n2, sv["n2t"] = _resid_norm(sv["x1"], [mix_a, mix_c], 1.0, _vec(ffn2_norm, l), MM_DTYPE, True)
        sv["a2"], sv["b2"], sv["hm2"], o2 = _ffn_fwd(n2, w, SHARDED[8:11], 0)
        sv["x3"], out = _final_fwd(sv["x2"], o2, _vec(post_norm, l))
        return out

    sv0 = fwd_first(0, xs, w0, token_a[0, 0])
    in_flight = _exchange("gather1_ici_wait", "wait", in_flight, sem_kj, ici(1), sems=sems_a, after=sv0["q"])
    sems_b, token_b, in_flight = _exchange("gather1_d2d_start", "start", in_flight[k_n:], sem_kj, d2d)
    cur = fwd_second(0, sv0, w0, token_b[0, 0])
    w1 = layouts(_exchange("gather1_d2d_wait", "wait", in_flight, sem_kj, d2d, sems=sems_b, after=cur))
    sv1 = fwd_first(1, cur, w1, 0.0)
    cur = fwd_second(1, sv1, w1, 0.0)
    dy, loss_part = _loss(cur, target)

    small = {n: [None] * L for n in SMALL}

    def bwd_first(l, sv, dy, w, tok, grads):
        dx3, do2, small["post_norm"][l] = _norm_bwd(sv["x3"], _vec(post_norm, l) + tok, [dy], None, 0.5)
        dn2 = _ffn_bwd(sv["n2t"], sv["a2"], sv["b2"], sv["hm2"], do2, w, SHARDED[8:11], grads)
        dx2, dx2b, small["ffn2_norm"][l] = _norm_bwd(sv["x2"], _vec(ffn2_norm, l), dn2, dx3, 1.0)
        do = _proj("wo_a_t", dx2b, w["wo_a"], 0, "nt")[0]
        dc = _proj("wo_c_t", dx2b, w["wo_c"], 0, "nt")[0]
        g_wo_a = _wgrad("wo_a_wgrad", sv["o"], dx2b[None], WIRE_DTYPE)[0].reshape(N_HEADS, HEAD_PAD, d)[:, :V_DIM]
        g_wo_c = _wgrad("wo_c_wgrad", sv["c"], dx2b[None], WIRE_DTYPE)
        grads["w_out"] = jnp.concatenate(
            [g_wo_a.reshape(N_HEADS * V_DIM, d), g_wo_c[0]], axis=0).reshape(N_CHIPS, d // N_CHIPS, d)
        dq, dk, dv = _attn_bwd(sv["q"], sv["k"], sv["kt"], sv["v"], do, sv["lse"], _attn_delta(sv["o"], do))
        dq_pre, dk_pre, dpe, small["q_norm"][l], small["k_norm"][l] = _qk_post_bwd(
            sv["q_pre"], sv["k_pre"], sv["p"], pe_block, tabs, _vec(gq_pad, l), _vec(gk_pad, l), dq, dk)
        return dict(dx2=dx2, dc=dc, dq_pre=dq_pre, dk_pre=dk_pre, dv=dv, dpe=dpe)

    def bwd_second(l, sv, st, w, tok, grads):
        dq_pre, dk_pre, dv = st["dq_pre"], st["dk_pre"], st["dv"]
        dcq = _contract("wq_t", dq_pre[None], w["wq"], 0, "nt")
        dckv = [_contract("wk_t", dk_pre[None], w["wk"], 0, "nt"), _contract("wv_t", dv[None], w["wv"], 0, "nt")]
        per_head = lambda t, rows: t[0].reshape(rows, N_HEADS, HEAD_PAD)
        g_wq = per_head(_wgrad("wq_wgrad", sv["cq"], dq_pre[None], WIRE_DTYPE), ql)[..., :QK_DIM]
        grads["w_uq"] = g_wq.reshape(ql, N_CHIPS, -1).transpose(1, 0, 2)
        g_wkv = jnp.concatenate([per_head(_wgrad("wk_wgrad", sv["ckv"], dk_pre[None], WIRE_DTYPE), kvl)[..., :QK_NOPE],
                                 per_head(_wgrad("wv_wgrad", sv["ckv"], dv[None], WIRE_DTYPE), kvl)[..., :V_DIM]], axis=-1)
        grads["w_ukv"] = g_wkv.reshape(kvl, N_CHIPS, -1).transpose(1, 0, 2)
        dyc, small["conv_ln_g"][l], small["conv_ln_b"][l], small["conv_b"][l] = _conv_bwd_ln(
            sv["p"], w["conv"], _vec(conv_b, l) + tok, _vec(conv_ln_g, l), _vec(conv_ln_b, l), st["dc"])
        dp_conv, g_conv = _conv_bwd_taps(sv["p"], w["conv"], dyc)
        grads["conv_w"] = g_conv.reshape(CONV_K + 1, N_CHIPS, cw // N_CHIPS).transpose(1, 0, 2).astype(WIRE_DTYPE)
        dp, small["q_latent_norm"][l], small["kv_latent_norm"][l] = _dp_assemble(
            sv["p"], _vec(q_latent_norm, l), _vec(kv_latent_norm, l), dcq, dckv, dp_conv, st["dpe"], cw)
        dh = _contract("w_in_t", dp[None], w["w_in"], 0, "nt")
        g_win = _wgrad("w_in_wgrad", sv["h"], dp[None], WIRE_DTYPE)[0]
        o_q, o_pe_col = 2 * cw, 2 * cw + ql + kvl + QK_NOPE
        g_win = jnp.concatenate([g_win[:, o_q:o_q + ql + kvl], g_win[:, o_pe_col:o_pe_col + QK_ROPE], g_win[:, :o_q]], axis=1)
        grads["w_in"] = g_win.reshape(d, N_CHIPS, -1).transpose(1, 0, 2)
        dx1, do1, small["mix_norm"][l] = _norm_bwd(sv["x1"], _vec(mix_norm, l), [dh], st["dx2"], 0.5)
        dn1 = _ffn_bwd(sv["n1t"], sv["a1"], sv["b1"], sv["hm1"], do1, w, SHARDED[0:3], grads)
        dx, _, small["ffn1_norm"][l] = _norm_bwd(sv["x"], _vec(ffn1_norm, l), dn1, dx1, 1.0)
        return dx

    def halves(parts):
        return [lax.empty((N_CHIPS, t.shape[1] // 2, t.shape[2]), t.dtype) for t in parts]

    def thirds(parts):
        return [lax.empty((N_CHIPS - 1,) + t.shape[1:], t.dtype) for t in parts]

    def chip_sums(done):
        return [_add_sibling_half(place, g, t) for g, t in zip(done[:k_n], done[k_n:])]

    swap, scatter, sem_k = _plan_swap(k_n), _plan_scatter(k_n), (k_n,)
    grads1, grads0 = {}, {}
    st1 = bwd_first(1, sv1, dy, w1, 0.0, grads1)
    dy = bwd_second(1, sv1, st1, w1, 0.0, grads1)
    parts1 = [grads1[n] for n in SHARDED]
    sems_s, token_s, in_flight = _exchange("swap1_start", "start", parts1 + halves(parts1), sem_k, swap)
    st0 = bwd_first(0, sv0, dy, w0, token_s[0, 0], grads0)
    sums1 = chip_sums(_exchange("swap1_wait", "wait", in_flight, sem_k, swap, sems=sems_s, after=st0["dq_pre"]))
    sems_c, token_c, in_flight = _exchange("scatter1_start", "start", sums1 + thirds(sums1), sem_kj, scatter)
    dy = bwd_second(0, sv0, st0, w0, token_c[0, 0], grads0)
    grad_x = dy.reshape(x.shape)
    done = _exchange("scatter1_wait", "wait", in_flight, sem_kj, scatter, sems=sems_c, after=dy)
    outs = [_add_chips_half(place, own, others, L, 1, None) for own, others in zip(done[:k_n], done[k_n:])]
    parts0 = [grads0[n] for n in SHARDED]
    sums0 = chip_sums(_exchange("swap0", "run", parts0 + halves(parts0), sem_k, swap))
    done = _exchange("scatter0", "run", sums0 + thirds(sums0), sem_kj, scatter)
    outs = [_add_chips_half(place, own, others, L, 0, into) for own, others, into in zip(done[:k_n], done[k_n:], outs)]
    full = dict(zip(SHARDED, _exchange("share", "run", outs, (L * k_n,), _plan_share(k_n, L))))
    full["conv_w"] = full["conv_w"][:, :CONV_K]

    width = max(wts[n].shape[-1] for n in SMALL)
    pack = lambda vals: jnp.concatenate([_pad_lanes(v.reshape(-1, v.shape[-1]), width) for v in vals], axis=0)
    part = pack([jnp.concatenate(small[n], axis=0)[..., :wts[n].shape[-1]] for n in SMALL])
    rows = part.shape[0]
    part = jnp.pad(part, [(0, -rows % 8), (0, 0)])
    small_sum = _add_leading("add_devices", _gather_small(part), F32)
    for i, n in enumerate(SMALL):
        full[n] = small_sum[i * L:(i + 1) * L, :wts[n].shape[-1]]

    delta_w, new_m, new_v = {}, {}, {}
    for n in SHARDED:
        shape = wts[n].shape
        swap = shape[-1] % HEAD_PAD != 0 and shape[-2] % HEAD_PAD == 0
        view = (lambda t: jnp.swapaxes(t, 1, 2)) if swap else (lambda t: t)
        flat = lambda t: view(t).reshape(-1, view(t).shape[-1])
        r = flat(wts[n]).shape[0]
        fix = (lambda t: jnp.pad(flat(t), [(0, -r % 8), (0, 0)])) if r % 8 else flat
        outs = _adamw(fix(wts[n]), fix(full[n]), fix(mom[n]), fix(var[n]))
        delta_w[n], new_m[n], new_v[n] = [view(o[:r].reshape(view(wts[n]).shape)) for o in outs]
    pk = lambda src: jnp.pad(pack([src[n] for n in SMALL]), [(0, -rows % 8), (0, 0)])
    outs = _adamw(pk(wts), small_sum, pk(mom), pk(var))
    for i, n in enumerate(SMALL):
        delta_w[n], new_m[n], new_v[n] = [o[i * L:(i + 1) * L, :wts[n].shape[-1]] for o in outs]

    loss = lax.psum(loss_part[0, 0], ("x", "y", "c"))
    return (loss, grad_x, *[full[n] for n in ALL_WEIGHTS], *[delta_w[n] for n in ALL_WEIGHTS],
            *[new_m[n] for n in ALL_WEIGHTS], *[new_v[n] for n in ALL_WEIGHTS])
```

```python
import jax
import jax.numpy as jnp
from jax import lax
from jax.experimental import pallas as pl
from jax.experimental.pallas import tpu as pltpu

F32 = jnp.float32
MM_DTYPE = jnp.bfloat16
WIRE_DTYPE = jnp.bfloat16
EPS = 1e-6
N_HEADS = 8
QK_NOPE = 64
QK_ROPE = 32
QK_DIM = QK_NOPE + QK_ROPE
V_DIM = 64
HEAD_PAD = 128
HEADS_PER_STEP = 2
CHUNK_SHIFT = 6
LOG2E = 1.4426950408889634
CONV_K = 31
HALO = 32
ROPE_THETA = 10000.0
N_CHIPS = 4
ADAM_LR, ADAM_B1, ADAM_B2, ADAM_EPS, ADAM_WD, ADAM_STEP = 0.001, 0.9, 0.999, 1e-08, 0.01, 10
VMEM_LIMIT_BYTES = 56 * 2 ** 20
MESH = pl.DeviceIdType.MESH
ANY = pl.BlockSpec(memory_space=pl.ANY)

_DIMS = {
    "nn": (((1,), (0,)), ((), ())),
    "nt": (((1,), (1,)), ((), ())),
    "tn": (((0,), (0,)), ((), ())),
}


def _params(n_axes):
    return pltpu.CompilerParams(dimension_semantics=("arbitrary",) * n_axes, vmem_limit_bytes=VMEM_LIMIT_BYTES)


def _tile(n, cap):
    for step in (16, 8):
        for t in range(cap - cap % step, 0, -step):
            if n % t == 0:
                return t
    return n


def _mm(name, a, b, dims, grid, a_spec, b_spec, o_spec, out_shape, nk, acc_shape, into=None):
    n_axes = len(grid)

    def body(*refs):
        if into is not None:
            refs = refs[1:]
        a_ref, b_ref, o_ref = refs[:3]
        part = lax.dot_general(a_ref[...].astype(MM_DTYPE), b_ref[...].astype(MM_DTYPE), _DIMS[dims],
                               preferred_element_type=F32)
        if nk == 1:
            o_ref[...] = part.astype(o_ref.dtype)
        else:
            acc_ref = refs[3]
            k = pl.program_id(n_axes - 1)

            @pl.when(k == 0)
            def _():
                acc_ref[...] = part

            @pl.when(k > 0)
            def _():
                acc_ref[...] += part

            @pl.when(k == nk - 1)
            def _():
                o_ref[...] = acc_ref[...].astype(o_ref.dtype)

    in_specs = [a_spec, b_spec]
    args = [a, b]
    aliases = {}
    if into is not None:
        in_specs = [ANY] + in_specs
        args = [into] + args
        aliases = {0: 0}
    return pl.pallas_call(
        body, name=name, grid=grid, in_specs=in_specs, out_specs=o_spec, out_shape=out_shape,
        scratch_shapes=[] if nk == 1 else [pltpu.VMEM(acc_shape, F32)],
        input_output_aliases=aliases, compiler_params=_params(n_axes))(*args)


def _proj(name, a, w, l, dims, out_dtype=F32):
    s, k = a.shape
    g = w.shape[1]
    n = w.shape[3] if dims == "nn" else w.shape[2]
    tm = _tile(s, 512)
    return _mm(name, a, w, dims, (g, s // tm),
               pl.BlockSpec((tm, k), lambda j, i: (i, 0)),
               pl.BlockSpec((None, None) + w.shape[2:], lambda j, i: (l, j, 0, 0)),
               pl.BlockSpec((None, tm, n), lambda j, i: (j, i, 0)),
               jax.ShapeDtypeStruct((g, s, n), out_dtype), 1, None)


def _contract(name, a, w, l, dims, out_dtype=F32):
    g, s, kb = a.shape
    n = w.shape[3] if dims == "nn" else w.shape[2]
    tm = _tile(s, 512)
    return _mm(name, a, w, dims, (s // tm, g),
               pl.BlockSpec((None, tm, kb), lambda i, j: (j, i, 0)),
               pl.BlockSpec((None, None) + w.shape[2:], lambda i, j: (l, j, 0, 0)),
               pl.BlockSpec((tm, n), lambda i, j: (i, 0)),
               jax.ShapeDtypeStruct((s, n), out_dtype), g, (tm, n))


def _wgrad(name, a, b, out_dtype, layers=None, l=None, into=None):
    g = a.shape[0] if a.ndim == 3 else b.shape[0]
    s, k = a.shape[-2:]
    n = b.shape[-1]
    tm = _tile(s, 512)

    def spec(x, w):
        if x.ndim == 3:
            return pl.BlockSpec((None, tm, w), lambda j, i: (j, i, 0))
        return pl.BlockSpec((tm, w), lambda j, i: (i, 0))

    if layers is None:
        o_spec = pl.BlockSpec((None, k, n), lambda j, i: (j, 0, 0))
        out_shape = jax.ShapeDtypeStruct((g, k, n), out_dtype)
    else:
        o_spec = pl.BlockSpec((None, None, k, n), lambda j, i: (l, j, 0, 0))
        out_shape = jax.ShapeDtypeStruct((layers, g, k, n), out_dtype)
    return _mm(name, a, b, "tn", (g, s // tm), spec(a, k), spec(b, n), o_spec, out_shape, s // tm, (k, n), into=into)


def _ffn_up_fused(n, wg, wu, l):
    s, k = n.shape
    g, n4 = wg.shape[1], wg.shape[3]
    tm = _tile(s, 512)

    def body(n_ref, wg_ref, wu_ref, a_ref, b_ref, h_ref):
        nv = n_ref[...]
        a = jnp.dot(nv, wg_ref[...], preferred_element_type=F32)
        b = jnp.dot(nv, wu_ref[...], preferred_element_type=F32)
        a_ref[...] = a.astype(a_ref.dtype)
        b_ref[...] = b.astype(b_ref.dtype)
        h_ref[...] = (a * _sigmoid(a) * b).astype(h_ref.dtype)

    w_spec = pl.BlockSpec((None, None, k, n4), lambda j, i: (l, j, 0, 0))
    o_spec = pl.BlockSpec((None, tm, n4), lambda j, i: (j, i, 0))
    return pl.pallas_call(
        body, name="ffn_up_fused", grid=(g, s // tm),
        in_specs=[pl.BlockSpec((tm, k), lambda j, i: (i, 0)), w_spec, w_spec], out_specs=[o_spec] * 3,
        out_shape=[jax.ShapeDtypeStruct((g, s, n4), MM_DTYPE)] * 3, compiler_params=_params(2))(n, wg, wu)


def _ffn_down_t_fused(do, wd, a, b, l):
    s, k = do.shape
    g, n4 = wd.shape[1], wd.shape[2]
    tm = _tile(s, 512)

    def body(do_ref, wd_ref, a_ref, b_ref, da_ref, db_ref):
        dh = lax.dot_general(do_ref[...], wd_ref[...], _DIMS["nt"], preferred_element_type=F32)
        av = a_ref[...].astype(F32)
        sg = _sigmoid(av)
        da_ref[...] = (dh * b_ref[...].astype(F32) * (sg * (1.0 + av * (1.0 - sg)))).astype(da_ref.dtype)
        db_ref[...] = (dh * (av * sg)).astype(db_ref.dtype)

    t_spec = pl.BlockSpec((None, tm, n4), lambda j, i: (j, i, 0))
    return pl.pallas_call(
        body, name="ffn_down_t_fused", grid=(g, s // tm),
        in_specs=[pl.BlockSpec((tm, k), lambda j, i: (i, 0)),
                  pl.BlockSpec((None, None, n4, k), lambda j, i: (l, j, 0, 0)), t_spec, t_spec],
        out_specs=[t_spec] * 2, out_shape=[jax.ShapeDtypeStruct((g, s, n4), MM_DTYPE)] * 2,
        compiler_params=_params(2))(do, wd, a, b)


def _blocked_sum(name, acts, weights, l, dims):
    g, s, kb = acts[0].shape
    n = weights[0].shape[3] if dims == "nn" else weights[0].shape[2]
    tm = _tile(s, 256)
    n_pairs = len(acts)

    def body(*refs):
        o_ref = refs[2 * n_pairs]
        acc = None
        for p in range(n_pairs):
            for j in range(g):
                t = lax.dot_general(refs[p][j], refs[n_pairs + p][j], _DIMS[dims], preferred_element_type=F32)
                acc = t if acc is None else acc + t
        o_ref[...] = acc

    return pl.pallas_call(
        body, name=name, grid=(s // tm,),
        in_specs=[pl.BlockSpec((g, tm, kb), lambda i: (0, i, 0))] * n_pairs
        + [pl.BlockSpec((None,) + w.shape[1:], lambda i: (l, 0, 0, 0)) for w in weights],
        out_specs=pl.BlockSpec((tm, n), lambda i: (i, 0)), out_shape=jax.ShapeDtypeStruct((s, n), F32),
        compiler_params=_params(1))(*acts, *weights)


def _wgrad_pair(name, nt, da, db, layers, l, into):
    k, s = nt.shape
    g, _, n4 = da.shape
    tm = _tile(s, 512)
    nk = s // tm

    def body(*refs):
        if into is not None:
            refs = refs[2:]
        nt_ref, da_ref, db_ref, og_ref, ou_ref, accg, accu = refs
        i = pl.program_id(1)
        ntv = nt_ref[...]
        pg = jnp.dot(ntv, da_ref[...], preferred_element_type=F32)
        pu = jnp.dot(ntv, db_ref[...], preferred_element_type=F32)

        @pl.when(i == 0)
        def _():
            accg[...] = pg
            accu[...] = pu

        @pl.when(i > 0)
        def _():
            accg[...] += pg
            accu[...] += pu

        @pl.when(i == nk - 1)
        def _():
            og_ref[...] = accg[...].astype(og_ref.dtype)
            ou_ref[...] = accu[...].astype(ou_ref.dtype)

    t_spec = pl.BlockSpec((None, tm, n4), lambda j, i: (j, i, 0))
    o_spec = pl.BlockSpec((None, None, k, n4), lambda j, i: (l, j, 0, 0))
    in_specs = [pl.BlockSpec((k, tm), lambda j, i: (0, i)), t_spec, t_spec]
    args = [nt, da, db]
    aliases = {}
    if into is not None:
        in_specs = [ANY, ANY] + in_specs
        args = list(into) + args
        aliases = {0: 0, 1: 1}
    return pl.pallas_call(
        body, name=name, grid=(g, nk), in_specs=in_specs, out_specs=[o_spec] * 2,
        out_shape=[jax.ShapeDtypeStruct((layers, g, k, n4), WIRE_DTYPE)] * 2,
        scratch_shapes=[pltpu.VMEM((k, n4), F32)] * 2, input_output_aliases=aliases,
        compiler_params=_params(2))(*args)


def _rows(name, fn, s, tm, ins, outs, scratch=()):
    n = s // tm

    def spec(shape, kind):
        nd = len(shape)
        if kind == "r":
            return pl.BlockSpec(shape[:-2] + (tm, shape[-1]), lambda i: (0,) * (nd - 2) + (i, 0))
        if kind in ("f", "a"):
            return pl.BlockSpec(shape, lambda i: (0,) * nd)
        if kind == "t":
            return pl.BlockSpec((shape[0], tm), lambda i: (0, i))
        if kind == "b":
            return pl.BlockSpec((None,) + shape[1:], lambda i: (i, 0, 0))
        if kind == "p":
            return pl.BlockSpec((HALO, shape[-1]), lambda i: (jnp.maximum(i * (tm // HALO) - 1, 0), 0))
        if kind == "n":
            return pl.BlockSpec((HALO, shape[-1]), lambda i: (jnp.minimum((i + 1) * (tm // HALO), s // HALO - 1), 0))
        j = kind[1]
        return pl.BlockSpec((tm, HEAD_PAD), lambda i: (i, j))

    def body(*refs):
        fn(pl.program_id(0), n, *refs)

    return pl.pallas_call(
        body, name=name, grid=(n,),
        in_specs=[spec(a.shape, kind) for a, kind in ins],
        out_specs=[spec(shape, kind) for shape, _, kind in outs],
        out_shape=[jax.ShapeDtypeStruct(shape, dtype) for shape, dtype, _ in outs],
        scratch_shapes=list(scratch), compiler_params=_params(1))(*[a for a, _ in ins])


def _acc(ref, i, val):
    @pl.when(i == 0)
    def _():
        ref[...] = val

    @pl.when(i > 0)
    def _():
        ref[...] += val


def _sum0(x):
    return jnp.sum(x, axis=0, keepdims=True)


def _rstd(x, n):
    return lax.rsqrt(jnp.sum(x * x, axis=-1, keepdims=True) * (1.0 / n) + EPS)


def _rms_bwd(x, g, dy, n):
    r = _rstd(x, n)
    xh = x * r
    dyg = dy * g
    dx = r * (dyg - xh * (jnp.sum(dyg * xh, axis=-1, keepdims=True) * (1.0 / n)))
    return dx, _sum0(dy * xh)


def _sigmoid(x):
    return 1.0 / (1.0 + jnp.exp(-x))


def _norm_fwd(x, g):
    s, d = x.shape

    def fn(i, n, x_ref, g_ref, o_ref, ot_ref):
        xv = x_ref[...]
        y = xv * _rstd(xv, d) * g_ref[...]
        o_ref[...] = y.astype(o_ref.dtype)
        ot_ref[...] = y.T.astype(ot_ref.dtype)

    return _rows("norm_fwd", fn, s, _tile(s, 256), [(x, "r"), (g, "f")],
                 [((s, d), MM_DTYPE, "r"), ((d, s), MM_DTYPE, "t")])


def _resid_norm(x, adds, scale, g, norm_dtype, transposed=False):
    s, d = x.shape
    k = len(adds)

    def fn(i, n, *refs):
        x_ref, add_refs, g_ref, xo_ref, no_ref = refs[0], refs[1:1 + k], refs[1 + k], refs[2 + k], refs[3 + k]
        tot = add_refs[0][...]
        for r in add_refs[1:]:
            tot = tot + r[...]
        xv = x_ref[...] + scale * tot
        xo_ref[...] = xv
        y = xv * _rstd(xv, d) * g_ref[...]
        no_ref[...] = y.astype(no_ref.dtype)
        if transposed:
            refs[4 + k][...] = y.T.astype(refs[4 + k].dtype)

    return _rows("resid_norm", fn, s, _tile(s, 256), [(x, "r")] + [(a, "r") for a in adds] + [(g, "f")],
                 [((s, d), F32, "r"), ((s, d), norm_dtype, "r")] + ([((d, s), MM_DTYPE, "t")] if transposed else []))


def _norm_bwd(x, g, dn_parts, dres, out_scale):
    s, d = x.shape
    k = len(dn_parts)
    has_res = dres is not None

    def fn(i, n, *refs):
        x_ref, g_ref = refs[0], refs[1]
        dn_refs = refs[2:2 + k]
        pos = 2 + k
        dn = dn_refs[0][...]
        for r in dn_refs[1:]:
            dn = dn + r[...]
        dx, dg = _rms_bwd(x_ref[...], g_ref[...], dn, d)
        if has_res:
            dx = dx + refs[pos][...]
            pos += 1
        dx_ref, dxs_ref, dg_ref = refs[pos:pos + 3]
        dx_ref[...] = dx
        dxs_ref[...] = (out_scale * dx).astype(dxs_ref.dtype)
        _acc(dg_ref, i, dg)

    ins = [(x, "r"), (g, "f")] + [(p, "r") for p in dn_parts] + ([(dres, "r")] if has_res else [])
    return _rows("norm_bwd", fn, s, _tile(s, 256), ins,
                 [((s, d), F32, "r"), ((s, d), MM_DTYPE, "r"), ((1, d), F32, "a")])


def _final_fwd(x, o, g):
    return _resid_norm(x, [o], 0.5, g, F32)


def _loss(y, t):
    s, d = y.shape

    def fn(i, n, y_ref, t_ref, dy_ref, l_ref):
        e = y_ref[...] - t_ref[...]
        dy_ref[...] = e * (1.0 / d)
        _acc(l_ref, i, (0.5 / d) * jnp.sum(jnp.sum(e * e, axis=-1, keepdims=True), axis=0, keepdims=True))

    return _rows("loss", fn, s, _tile(s, 256), [(y, "r"), (t, "r")], [((s, d), F32, "r"), ((1, 1), F32, "a")])


def _mla_pre(p, gql, gkvl, cw):
    s = p.shape[0]
    ql, kvl = gql.shape[-1], gkvl.shape[-1]
    o_q, o_kv = 2 * cw, 2 * cw + ql

    def fn(i, n, p_ref, gq_ref, gkv_ref, cq_ref, ckv_ref, ckvt_ref):
        cq = p_ref[:, o_q:o_q + ql]
        cq_ref[...] = (cq * _rstd(cq, ql) * gq_ref[...]).astype(cq_ref.dtype)
        ckv = p_ref[:, o_kv:o_kv + kvl]
        y = ckv * _rstd(ckv, kvl) * gkv_ref[...]
        ckv_ref[...] = y.astype(ckv_ref.dtype)
        ckvt_ref[...] = y.T.astype(ckvt_ref.dtype)

    return _rows("mla_pre", fn, s, _tile(s, 256), [(p, "r"), (gql, "f"), (gkvl, "f")],
                 [((s, ql), MM_DTYPE, "r"), ((s, kvl), MM_DTYPE, "r"), ((kvl, s), MM_DTYPE, "t")])


def _rope(x, cosf, s1, s2):
    return x * cosf + pltpu.roll(x, HEAD_PAD - 16, 1) * s1 + pltpu.roll(x, 16, 1) * s2


def _rope_bwd(d, cosf, s1, s2):
    return d * cosf + pltpu.roll(d * s1, 16, 1) + pltpu.roll(d * s2, HEAD_PAD - 16, 1)


def _qk_post(q_pre, k_pre, p, pe_block, tabs, gq, gk):
    s, hw = q_pre.shape
    cosf, s1, s2 = tabs

    tm = _tile(s, 256)

    def fn(i, n, q_ref, k_ref, pe_ref, c_ref, s1_ref, s2_ref, gq_ref, gk_ref, qo_ref, ko_ref, kt_ref):
        c, a1, a2, pe = c_ref[...], s1_ref[...], s2_ref[...], pe_ref[...]
        for hh in range(N_HEADS):
            cols = slice(hh * HEAD_PAD, (hh + 1) * HEAD_PAD)
            qv = q_ref[:, cols]
            qo_ref[:, cols] = _rope(qv * _rstd(qv, QK_DIM) * gq_ref[...], c, a1, a2).astype(qo_ref.dtype)
            kv = k_ref[:, cols] + pe
            kout = _rope(kv * _rstd(kv, QK_DIM) * gk_ref[...], c, a1, a2)
            ko_ref[:, cols] = kout.astype(ko_ref.dtype)
            kt_ref[cols, :] = kout.T.astype(kt_ref.dtype)

    return _rows("qk_post", fn, s, tm,
                 [(q_pre, "r"), (k_pre, "r"), (p, ("c", pe_block)), (cosf, "r"), (s1, "r"), (s2, "r"), (gq, "f"), (gk, "f")],
                 [((s, hw), MM_DTYPE, "r"), ((s, hw), MM_DTYPE, "r"), ((s // tm, hw, tm), MM_DTYPE, "b")])


def _qk_post_bwd(q_pre, k_pre, p, pe_block, tabs, gq, gk, dq, dk):
    s, hw = q_pre.shape
    cosf, s1, s2 = tabs

    def fn(i, n, q_ref, k_ref, pe_ref, c_ref, s1_ref, s2_ref, gq_ref, gk_ref, dq_ref, dk_ref,
           dqo_ref, dko_ref, dpe_ref, dgq_ref, dgk_ref):
        c, a1, a2, pe = c_ref[...], s1_ref[...], s2_ref[...], pe_ref[...]
        lane = lax.broadcasted_iota(jnp.int32, pe.shape, 1)
        is_pe = (lane >= QK_NOPE) & (lane < QK_DIM)
        dpe = jnp.zeros(pe.shape, F32)
        dgq = jnp.zeros((1, HEAD_PAD), F32)
        dgk = jnp.zeros((1, HEAD_PAD), F32)
        for hh in range(N_HEADS):
            cols = slice(hh * HEAD_PAD, (hh + 1) * HEAD_PAD)
            dxq, g1 = _rms_bwd(q_ref[:, cols], gq_ref[...], _rope_bwd(dq_ref[cols, :].T, c, a1, a2), QK_DIM)
            dqo_ref[:, cols] = dxq.astype(dqo_ref.dtype)
            dxk, g2 = _rms_bwd(k_ref[:, cols] + pe, gk_ref[...], _rope_bwd(dk_ref[:, cols], c, a1, a2), QK_DIM)
            dko_ref[:, cols] = dxk.astype(dko_ref.dtype)
            dpe = dpe + dxk
            dgq = dgq + g1
            dgk = dgk + g2
        dpe_ref[...] = jnp.where(is_pe, dpe, 0.0)
        _acc(dgq_ref, i, dgq)
        _acc(dgk_ref, i, dgk)

    return _rows("qk_post_bwd", fn, s, _tile(s, 256),
                 [(q_pre, "r"), (k_pre, "r"), (p, ("c", pe_block)), (cosf, "r"), (s1, "r"), (s2, "r"), (gq, "f"), (gk, "f"),
                  (dq, "b"), (dk, "r")],
                 [((s, hw), MM_DTYPE, "r"), ((s, hw), MM_DTYPE, "r"), ((s, HEAD_PAD), F32, "r"),
                  ((1, HEAD_PAD), F32, "a"), ((1, HEAD_PAD), F32, "a")])


def _chunk_mask(tq, key_rows):
    r = jnp.right_shift(lax.broadcasted_iota(jnp.int32, (tq, tq), 0), CHUNK_SHIFT)
    c = jnp.right_shift(lax.broadcasted_iota(jnp.int32, (tq, tq), 1), CHUNK_SHIFT)
    return (r <= c) if key_rows else (c <= r)


def _attn_specs(s, tq):
    w = HEADS_PER_STEP * HEAD_PAD
    tile = pl.BlockSpec((tq, w), lambda hb, i: (i, hb))
    whole = pl.BlockSpec((s, w), lambda hb, i: (0, hb))
    rows = pl.BlockSpec((None, None, HEADS_PER_STEP, tq), lambda hb, i: (hb, i, 0, 0))
    blocks = pl.BlockSpec((s // tq, w, tq), lambda hb, i: (0, hb, 0))
    return tile, whole, rows, blocks


def _head(hh):
    return slice(hh * HEAD_PAD, (hh + 1) * HEAD_PAD)


def _attn_fwd(q, k, vt):
    s, hw = q.shape
    tq = _tile(s, 256)
    nq = s // tq
    groups = N_HEADS // HEADS_PER_STEP
    scale = QK_DIM ** -0.5
    tile, whole, rows, blocks = _attn_specs(s, tq)

    def body(q_ref, k_ref, vt_ref, o_ref, lse_ref):
        qi = pl.program_id(1)
        qv = [q_ref[:, _head(hh)] for hh in range(HEADS_PER_STEP)]

        def scores(kb):
            off = pl.multiple_of(kb * tq, tq)
            return tuple(lax.dot_general(k_ref[pl.ds(off, tq), _head(hh)], qv[hh], _DIMS["nt"],
                                         preferred_element_type=F32) for hh in range(HEADS_PER_STEP))

        def update(kb, stats, raw, diag):
            heads = range(HEADS_PER_STEP)
            mn, al, pr, ln = [], [], [], []
            for hh in heads:
                m, l, _ = stats[hh]
                sc = raw[hh] * (scale * LOG2E)
                if diag:
                    sc = jnp.where(_chunk_mask(tq, True), sc, -1e30)
                mn.append(jnp.maximum(m, jnp.max(sc, axis=0, keepdims=True)))
                al.append(jnp.exp2(m - mn[hh]))
                p = jnp.exp2(sc - mn[hh])
                ln.append(al[hh] * l + jnp.sum(p, axis=0, keepdims=True))
                pr.append(p.astype(MM_DTYPE))
            return tuple((mn[hh], ln[hh], al[hh] * stats[hh][2]
                          + jnp.dot(vt_ref[kb, _head(hh), :], pr[hh], preferred_element_type=F32)) for hh in heads)

        def step(kb, carry):
            stats, raw = carry
            nxt = scores(kb + 1)
            return update(kb, stats, raw, False), nxt

        init = tuple((jnp.full((1, tq), -1e30, F32), jnp.zeros((1, tq), F32), jnp.zeros((HEAD_PAD, tq), F32))
                     for _ in range(HEADS_PER_STEP))
        stats, raw = lax.fori_loop(0, qi, step, (init, scores(0)))
        carry = update(qi, stats, raw, True)
        for hh in range(HEADS_PER_STEP):
            m, l, acc = carry[hh]
            o_ref[:, _head(hh)] = (acc / l).T
            lse_ref[hh:hh + 1, :] = m + jnp.log(l) * LOG2E

    return pl.pallas_call(
        body, name="attn_fwd", grid=(groups, nq),
        in_specs=[tile, whole, blocks], out_specs=[tile, rows],
        out_shape=[jax.ShapeDtypeStruct((s, hw), F32), jax.ShapeDtypeStruct((groups, nq, HEADS_PER_STEP, tq), F32)],
        compiler_params=_params(2))(q, k, vt)


def _attn_delta(o, do):
    s, hw = o.shape
    tq = _tile(s, 256)
    groups = N_HEADS // HEADS_PER_STEP
    tile, _, rows, _ = _attn_specs(s, tq)

    def body(o_ref, do_ref, d_ref):
        for hh in range(HEADS_PER_STEP):
            prod = o_ref[:, _head(hh)] * do_ref[:, _head(hh)]
            d_ref[hh:hh + 1, :] = jnp.sum(prod.T, axis=0, keepdims=True)

    return pl.pallas_call(
        body, name="attn_delta", grid=(groups, s // tq), in_specs=[tile, tile], out_specs=rows,
        out_shape=jax.ShapeDtypeStruct((groups, s // tq, HEADS_PER_STEP, tq), F32),
        compiler_params=_params(2))(o, do)


def _attn_bwd(q, k, kt, v, do, lse_row, delta_row):
    s, hw = q.shape
    tq = _tile(s, 256)
    nq = s // tq
    groups = N_HEADS // HEADS_PER_STEP
    w = HEADS_PER_STEP * HEAD_PAD
    scale = QK_DIM ** -0.5
    tile, whole, _, blocks = _attn_specs(s, tq)
    rowvec = pl.BlockSpec((None, nq, HEADS_PER_STEP, tq), lambda hb, i: (hb, 0, 0, 0))
    kt_tile = pl.BlockSpec((None, w, tq), lambda hb, i: (i, hb, 0))

    def body(q_ref, k_ref, kt_ref, v_ref, do_ref, lse_ref, delta_ref, dqt_ref, dk_ref, dv_ref):
        ki = pl.program_id(1)
        kv = [k_ref[:, _head(hh)] for hh in range(HEADS_PER_STEP)]
        vv = [v_ref[:, _head(hh)] for hh in range(HEADS_PER_STEP)]
        ktv = [kt_ref[_head(hh), :] for hh in range(HEADS_PER_STEP)]

        @pl.when(ki == 0)
        def _():
            dqt_ref[...] = jnp.zeros(dqt_ref.shape, F32)

        def block(qb, carry, diag):
            off = pl.multiple_of(qb * tq, tq)
            heads = range(HEADS_PER_STEP)
            qs = [q_ref[pl.ds(off, tq), _head(hh)] for hh in heads]
            dos = [do_ref[pl.ds(off, tq), _head(hh)].astype(MM_DTYPE) for hh in heads]
            raw = [lax.dot_general(kv[hh], qs[hh], _DIMS["nt"], preferred_element_type=F32) for hh in heads]
            dpr = [lax.dot_general(vv[hh], dos[hh], _DIMS["nt"], preferred_element_type=F32) for hh in heads]
            pr, ds = [], []
            for hh in heads:
                sc = raw[hh] * (scale * LOG2E)
                if diag:
                    sc = jnp.where(_chunk_mask(tq, True), sc, -1e30)
                p = jnp.exp2(sc - lse_ref[qb, hh:hh + 1, :])
                pr.append(p.astype(MM_DTYPE))
                ds.append((p * (dpr[hh] - delta_ref[qb, hh:hh + 1, :]) * scale).astype(MM_DTYPE))
            out = []
            for hh in heads:
                dk, dvv = carry[hh]
                dvv = dvv + jnp.dot(pr[hh], dos[hh], preferred_element_type=F32)
                dk = dk + jnp.dot(ds[hh], qs[hh], preferred_element_type=F32)
                dqt_ref[qb, _head(hh), :] += jnp.dot(ktv[hh], ds[hh], preferred_element_type=F32)
                out.append((dk, dvv))
            return tuple(out)

        init = tuple((jnp.zeros((tq, HEAD_PAD), F32), jnp.zeros((tq, HEAD_PAD), F32)) for _ in range(HEADS_PER_STEP))
        carry = lax.fori_loop(ki + 1, nq, lambda qb, c: block(qb, c, False), block(ki, init, True))
        for hh in range(HEADS_PER_STEP):
            dk_ref[:, _head(hh)] = carry[hh][0]
            dv_ref[:, _head(hh)] = carry[hh][1].astype(dv_ref.dtype)

    return pl.pallas_call(
        body, name="attn_bwd", grid=(groups, nq),
        in_specs=[whole, tile, kt_tile, tile, whole, rowvec, rowvec], out_specs=[blocks, tile, tile],
        out_shape=[jax.ShapeDtypeStruct((nq, hw, tq), F32), jax.ShapeDtypeStruct((s, hw), F32),
                   jax.ShapeDtypeStruct((s, hw), MM_DTYPE)],
        compiler_params=_params(2))(q, k, kt, v, do, lse_row, delta_row)


def _glu_with_halo(i, p_ref, ph_ref, ubuf, cw, tm):
    a, g = p_ref[:, 0:cw], p_ref[:, cw:2 * cw]
    ubuf[HALO:HALO + tm, :] = a * _sigmoid(g)
    ah, gh = ph_ref[:, 0:cw], ph_ref[:, cw:2 * cw]
    ubuf[0:HALO, :] = jnp.where(i > 0, ah * _sigmoid(gh), 0.0)
    return a, g


def _conv_taps(ubuf, w_ref, tm):
    base = HALO - (CONV_K - 1)
    y = w_ref[0:1, :] * ubuf[base:base + tm, :]
    for k in range(1, CONV_K):
        y = y + w_ref[k:k + 1, :] * ubuf[base + k:base + k + tm, :]
    return y


def _layer_norm_stats(y, cw):
    mu = jnp.sum(y, axis=-1, keepdims=True) * (1.0 / cw)
    yc = y - mu
    r = lax.rsqrt(jnp.sum(yc * yc, axis=-1, keepdims=True) * (1.0 / cw) + EPS)
    return yc * r, r


def _conv_fwd(p, w, cb, lg, lb):
    s = p.shape[0]
    cw = cb.shape[-1]
    tm = _tile(s, 256)

    def fn(i, n, p_ref, ph_ref, w_ref, cb_ref, lg_ref, lb_ref, o_ref, ubuf):
        _glu_with_halo(i, p_ref, ph_ref, ubuf, cw, tm)
        yh, _ = _layer_norm_stats(_conv_taps(ubuf, w_ref, tm) + cb_ref[...], cw)
        z = yh * lg_ref[...] + lb_ref[...]
        o_ref[...] = (z * _sigmoid(z)).astype(o_ref.dtype)

    return _rows("conv_fwd", fn, s, tm, [(p, "r"), (p, "p"), (w, "f"), (cb, "f"), (lg, "f"), (lb, "f")],
                 [((s, cw), MM_DTYPE, "r")], scratch=[pltpu.VMEM((tm + HALO, cw), F32)])[0]


def _conv_bwd_ln(p, w, cb, lg, lb, dc):
    s = p.shape[0]
    cw = cb.shape[-1]
    tm = _tile(s, 256)

    def fn(i, n, p_ref, ph_ref, w_ref, cb_ref, lg_ref, lb_ref, dc_ref, dy_ref, dlg_ref, dlb_ref, dcb_ref, ubuf):
        _glu_with_halo(i, p_ref, ph_ref, ubuf, cw, tm)
        yh, r = _layer_norm_stats(_conv_taps(ubuf, w_ref, tm) + cb_ref[...], cw)
        z = yh * lg_ref[...] + lb_ref[...]
        sg = _sigmoid(z)
        dz = dc_ref[...] * (sg * (1.0 + z * (1.0 - sg)))
        dyh = dz * lg_ref[...]
        m1 = jnp.sum(dyh, axis=-1, keepdims=True) * (1.0 / cw)
        m2 = jnp.sum(dyh * yh, axis=-1, keepdims=True) * (1.0 / cw)
        dy = r * (dyh - m1 - yh * m2)
        dy_ref[...] = dy
        _acc(dlg_ref, i, _sum0(dz * yh))
        _acc(dlb_ref, i, _sum0(dz))
        _acc(dcb_ref, i, _sum0(dy))

    return _rows("conv_bwd_ln", fn, s, tm,
                 [(p, "r"), (p, "p"), (w, "f"), (cb, "f"), (lg, "f"), (lb, "f"), (dc, "r")],
                 [((s, cw), F32, "r"), ((1, cw), F32, "a"), ((1, cw), F32, "a"), ((1, cw), F32, "a")],
                 scratch=[pltpu.VMEM((tm + HALO, cw), F32)])


def _conv_bwd_taps(p, w, dy):
    s = p.shape[0]
    cw = w.shape[-1]
    tm = _tile(s, 256)

    def fn(i, n, p_ref, ph_ref, w_ref, dy_ref, dyn_ref, dp_ref, dw_ref, ubuf, dybuf):
        a, g = _glu_with_halo(i, p_ref, ph_ref, ubuf, cw, tm)
        dyv = dy_ref[...]
        dybuf[0:tm, :] = dyv
        dybuf[tm:tm + HALO, :] = jnp.where(i < n - 1, dyn_ref[...], 0.0)
        base = HALO - (CONV_K - 1)
        du = jnp.zeros((tm, cw), F32)

        @pl.when(i == 0)
        def _():
            dw_ref[...] = jnp.zeros(dw_ref.shape, F32)

        for k in range(CONV_K):
            sh = CONV_K - 1 - k
            du = du + w_ref[k:k + 1, :] * dybuf[sh:sh + tm, :]
            dw_ref[k:k + 1, :] += _sum0(dyv * ubuf[base + k:base + k + tm, :])
        sg = _sigmoid(g)
        dp_ref[:, 0:cw] = (du * sg).astype(dp_ref.dtype)
        dp_ref[:, cw:2 * cw] = (du * a * sg * (1.0 - sg)).astype(dp_ref.dtype)

    return _rows("conv_bwd_taps", fn, s, tm, [(p, "r"), (p, "p"), (w, "f"), (dy, "r"), (dy, "n")],
                 [((s, 2 * cw), MM_DTYPE, "r"), ((CONV_K + 1, cw), F32, "a")],
                 scratch=[pltpu.VMEM((tm + HALO, cw), F32), pltpu.VMEM((tm + HALO, cw), F32)])


def _dp_assemble(p, gql, gkvl, dcq, dckv_parts, dp_conv, dpe, cw):
    s, width = p.shape
    ql, kvl = gql.shape[-1], gkvl.shape[-1]
    o_q, o_kv, o_pe = 2 * cw, 2 * cw + ql, 2 * cw + ql + kvl

    def fn(i, n, p_ref, gq_ref, gkv_ref, dcq_ref, dk1_ref, dk2_ref, dconv_ref, dpe_ref, dp_ref, dgq_ref, dgkv_ref):
        dp_ref[:, 0:o_q] = dconv_ref[...]
        dx, dg = _rms_bwd(p_ref[:, o_q:o_kv], gq_ref[...], dcq_ref[...], ql)
        dp_ref[:, o_q:o_kv] = dx.astype(dp_ref.dtype)
        _acc(dgq_ref, i, dg)
        dx, dg = _rms_bwd(p_ref[:, o_kv:o_pe], gkv_ref[...], dk1_ref[...] + dk2_ref[...], kvl)
        dp_ref[:, o_kv:o_pe] = dx.astype(dp_ref.dtype)
        _acc(dgkv_ref, i, dg)
        dp_ref[:, o_pe:width] = dpe_ref[...].astype(dp_ref.dtype)

    return _rows("dp_assemble", fn, s, _tile(s, 256),
                 [(p, "r"), (gql, "f"), (gkvl, "f"), (dcq, "r"), (dckv_parts[0], "r"), (dckv_parts[1], "r"),
                  (dp_conv, "r"), (dpe, "r")],
                 [((s, width), MM_DTYPE, "r"), ((1, ql), F32, "a"), ((1, kvl), F32, "a")])


def _add_leading(name, x, out_dtype):
    n, r, c = x.shape
    tm = _tile(r, 256)

    def fn(i, nt, x_ref, o_ref):
        tot = x_ref[0].astype(F32)
        for k in range(1, n):
            tot = tot + x_ref[k].astype(F32)
        o_ref[...] = tot.astype(o_ref.dtype)

    return _rows(name, fn, r, tm, [(x, "r")], [((r, c), out_dtype, "r")])[0]


def _prefetch_call(name, fn, idx, grid, in_specs, out_spec, out_shape, args, aliases=None):
    return pl.pallas_call(
        fn, name=name, out_shape=out_shape,
        grid_spec=pltpu.PrefetchScalarGridSpec(num_scalar_prefetch=1, grid=grid, in_specs=in_specs, out_specs=out_spec),
        input_output_aliases=aliases or {}, compiler_params=_params(len(grid)))(idx, *args)


def _adamw(w, g, m, v):
    r, c = w.shape
    tm = _tile(r, 256)

    def fn(i, n, w_ref, g_ref, m_ref, v_ref, d_ref, mo_ref, vo_ref):
        gv = g_ref[...]
        mn = ADAM_B1 * m_ref[...] + (1.0 - ADAM_B1) * gv
        vn = ADAM_B2 * v_ref[...] + (1.0 - ADAM_B2) * (gv * gv)
        mo_ref[...] = mn
        vo_ref[...] = vn
        m_hat = mn / (1.0 - ADAM_B1 ** ADAM_STEP)
        v_hat = vn / (1.0 - ADAM_B2 ** ADAM_STEP)
        d_ref[...] = -ADAM_LR * (m_hat / (jnp.sqrt(v_hat) + ADAM_EPS) + ADAM_WD * w_ref[...])

    return _rows("adamw", fn, r, tm, [(w, "r"), (g, "r"), (m, "r"), (v, "r")], [((r, c), F32, "r")] * 3)


def _place():
    x, y, c = lax.axis_index("x"), lax.axis_index("y"), lax.axis_index("c")
    other_chips = [(1 - x, y), (x, 1 - y), (1 - x, 1 - y)]
    return x, y, c, 2 * x + y, other_chips


def _gather_small(vec):
    flips = [(fx, fy, fc) for fx in (0, 1) for fy in (0, 1) for fc in (0, 1)][1:]

    def body(src, dst, send, recv, lsem):
        x, y, c, _, _ = _place()
        me = 4 * x + 2 * y + c
        local = pltpu.make_async_copy(src, dst.at[me], lsem)
        local.start()
        away = []
        for j, (fx, fy, fc) in enumerate(flips):
            px, py, pc = x ^ fx, y ^ fy, c ^ fc
            away.append(pltpu.make_async_remote_copy(
                src_ref=src, dst_ref=dst.at[me], send_sem=send.at[j], recv_sem=recv.at[j],
                device_id=(px, py, pc), device_id_type=MESH))
        for cp in away:
            cp.start()
        for j, (fx, fy, fc) in enumerate(flips):
            blk = dst.at[4 * (x ^ fx) + 2 * (y ^ fy) + (c ^ fc)]
            pltpu.make_async_remote_copy(
                src_ref=blk, dst_ref=blk, send_sem=send.at[j], recv_sem=recv.at[j],
                device_id=(x ^ fx, y ^ fy, c ^ fc), device_id_type=MESH).wait_recv()
        for cp in away:
            cp.wait_send()
        local.wait()

    return pl.pallas_call(
        body, name="gather_small", in_specs=[ANY], out_specs=ANY,
        out_shape=jax.ShapeDtypeStruct((8,) + vec.shape, vec.dtype),
        scratch_shapes=[pltpu.SemaphoreType.DMA((7,)), pltpu.SemaphoreType.DMA((7,)), pltpu.SemaphoreType.DMA(())],
        compiler_params=pltpu.CompilerParams(has_side_effects=True))(vec)


HBM = pl.BlockSpec(memory_space=pltpu.HBM)
SEM = pl.BlockSpec(memory_space=pltpu.SEMAPHORE)
EFFECT = pltpu.SideEffectType.DATAFLOW_SIDE_EFFECTING


def _exchange(name, mode, arrays, sem_shape, plan, sems=None, after=None):
    n = len(arrays)
    hbm = [pltpu.HBM(a.shape, a.dtype) for a in arrays]
    arrays = [pltpu.with_memory_space_constraint(a, pltpu.HBM) for a in arrays]
    if mode == "run":
        def body(*refs):
            outs, send, recv = refs[n:2 * n], refs[2 * n], refs[2 * n + 1]
            sends, recvs = plan(outs, send, recv)
            for cp in sends:
                cp.start()
            for cp in recvs:
                cp.wait_recv()
            for cp in sends:
                cp.wait_send()

        return list(pl.pallas_call(
            body, name=name, in_specs=[HBM] * n, out_specs=[HBM] * n, out_shape=hbm,
            input_output_aliases={i: i for i in range(n)},
            scratch_shapes=[pltpu.SemaphoreType.DMA(sem_shape), pltpu.SemaphoreType.DMA(sem_shape)],
            compiler_params=pltpu.CompilerParams(has_side_effects=EFFECT))(*arrays))
    if mode == "start":
        def body(*refs):
            send, recv, token, outs = refs[n], refs[n + 1], refs[n + 2], refs[n + 3:]
            sends, _ = plan(outs, send, recv)
            for cp in sends:
                cp.start()
            token[...] = jnp.zeros(token.shape, F32)

        res = pl.pallas_call(
            body, name=name, in_specs=[HBM] * n,
            out_specs=[SEM, SEM, pl.BlockSpec(memory_space=pltpu.VMEM)] + [HBM] * n,
            out_shape=[pltpu.SemaphoreType.DMA(sem_shape), pltpu.SemaphoreType.DMA(sem_shape),
                       jax.ShapeDtypeStruct((8, HEAD_PAD), F32)] + hbm,
            input_output_aliases={i: i + 3 for i in range(n)},
            compiler_params=pltpu.CompilerParams(has_side_effects=EFFECT))(*arrays)
        return (res[0], res[1]), res[2], list(res[3:])

    def body(*refs):
        ins, send, recv = refs[:n], refs[n], refs[n + 1]
        sends, recvs = plan(ins, send, recv)
        for cp in sends:
            cp.wait_send()
        for cp in recvs:
            cp.wait_recv()

    return list(pl.pallas_call(
        body, name=name, in_specs=[HBM] * n + [SEM, SEM, ANY], out_specs=[HBM] * n, out_shape=hbm,
        input_output_aliases={i: i for i in range(n)},
        compiler_params=pltpu.CompilerParams(has_side_effects=EFFECT))(*arrays, sems[0], sems[1], after))


def _rows_half(ref_rows, half):
    h = ref_rows // 2
    return pl.ds(pl.multiple_of(half * h, 8), h)


def _remote(src, dst, send_sem, recv_sem, device):
    return pltpu.make_async_remote_copy(src_ref=src, dst_ref=dst, send_sem=send_sem, recv_sem=recv_sem,
                                        device_id=device, device_id_type=MESH)


def _plan_gather_ici(k_n, l):
    def plan(refs, send, recv):
        x, y, c, me, chips = _place()
        sends, recvs = [], []
        for k in range(k_n):
            src, buf = refs[k], refs[k_n + k]
            rows = _rows_half(src.shape[1], c)
            for j, (cx, cy) in enumerate(chips):
                sends.append(_remote(src.at[l, rows], buf.at[0, me, rows], send.at[3 * k + j], recv.at[3 * k + j], (cx, cy, c)))
                got = buf.at[0, 2 * cx + cy, rows]
                recvs.append(_remote(got, got, send.at[3 * k + j], recv.at[3 * k + j], (cx, cy, c)))
        return sends, recvs
    return plan


def _plan_gather_d2d(k_n):
    def plan(refs, send, recv):
        x, y, c, me, chips = _place()
        sends, recvs = [], []
        for k in range(k_n):
            buf = refs[k]
            for j, (cx, cy) in enumerate(chips):
                mine = buf.at[0, 2 * cx + cy, _rows_half(buf.shape[2], c)]
                theirs = buf.at[0, 2 * cx + cy, _rows_half(buf.shape[2], 1 - c)]
                sends.append(_remote(mine, mine, send.at[3 * k + j], recv.at[3 * k + j], (x, y, 1 - c)))
                recvs.append(_remote(theirs, theirs, send.at[3 * k + j], recv.at[3 * k + j], (x, y, 1 - c)))
        return sends, recvs
    return plan


def _plan_swap(k_n):
    def plan(refs, send, recv):
        x, y, c, me, chips = _place()
        sends, recvs = [], []
        for k in range(k_n):
            src, land = refs[k], refs[k_n + k]
            sends.append(_remote(src.at[:, _rows_half(src.shape[1], 1 - c)], land, send.at[k], recv.at[k], (x, y, 1 - c)))
            recvs.append(_remote(land, land, send.at[k], recv.at[k], (x, y, 1 - c)))
        return sends, recvs
    return plan


def _plan_scatter(k_n):
    def plan(refs, send, recv):
        x, y, c, me, chips = _place()
        sends, recvs = [], []
        for k in range(k_n):
            src, land = refs[k], refs[k_n + k]
            for j, (cx, cy) in enumerate(chips):
                sends.append(_remote(src.at[2 * cx + cy], land.at[j], send.at[3 * k + j], recv.at[3 * k + j], (cx, cy, c)))
                recvs.append(_remote(land.at[j], land.at[j], send.at[3 * k + j], recv.at[3 * k + j], (cx, cy, c)))
        return sends, recvs
    return plan


def _plan_share(k_n, layers):
    def plan(refs, send, recv):
        x, y, c, me, chips = _place()
        sends, recvs = [], []
        for k in range(k_n):
            out = refs[k]
            for l in range(layers):
                mine = out.at[l, _rows_half(out.shape[1], c)]
                theirs = out.at[l, _rows_half(out.shape[1], 1 - c)]
                sends.append(_remote(mine, mine, send.at[layers * k + l], recv.at[layers * k + l], (x, y, 1 - c)))
                recvs.append(_remote(theirs, theirs, send.at[layers * k + l], recv.at[layers * k + l], (x, y, 1 - c)))
        return sends, recvs
    return plan


def _add_sibling_half(idx, mine, theirs):
    q, r, c = mine.shape
    h = r // 2
    tm = _tile(h, 256)

    def fn(idx_ref, a_ref, b_ref, o_ref):
        o_ref[...] = (a_ref[...].astype(F32) + b_ref[...].astype(F32)).astype(o_ref.dtype)

    return _prefetch_call(
        "add_sibling", fn, idx, (q, h // tm),
        [pl.BlockSpec((None, None, tm, c), lambda b, i, ix: (b, ix[0], i, 0)),
         pl.BlockSpec((None, tm, c), lambda b, i, ix: (b, i, 0))],
        pl.BlockSpec((None, tm, c), lambda b, i, ix: (b, i, 0)), jax.ShapeDtypeStruct((q, h, c), WIRE_DTYPE),
        [mine.reshape(q, 2, h, c), theirs])


def _add_chips_half(idx, own, others, layers, l, into):
    _, h, c = own.shape
    tm = _tile(h, 256)

    def fn(*refs):
        a_ref, b_ref, o_ref = refs[-3:]
        tot = a_ref[...].astype(F32)
        for k in range(3):
            tot = tot + b_ref[k].astype(F32)
        o_ref[...] = tot

    in_specs = [pl.BlockSpec((None, tm, c), lambda i, ix: (ix[1], i, 0)), pl.BlockSpec((3, tm, c), lambda i, ix: (0, i, 0))]
    args = [own, others]
    aliases = None
    if into is not None:
        in_specs = [ANY] + in_specs
        args = [into.reshape(layers, 2, h, c)] + args
        aliases = {1: 0}
    out = _prefetch_call(
        "add_chips", fn, idx, (h // tm,), in_specs,
        pl.BlockSpec((None, None, tm, c), lambda i, ix: (l, ix[0], i, 0)),
        jax.ShapeDtypeStruct((layers, 2, h, c), F32), args, aliases)
    return out.reshape(layers, 2 * h, c)


SHARDED = ["ffn1_w_gate", "ffn1_w_up", "ffn1_w_down", "w_in", "w_uq", "w_ukv", "conv_w", "w_out",
           "ffn2_w_gate", "ffn2_w_up", "ffn2_w_down"]
SMALL = ["ffn1_norm", "mix_norm", "q_latent_norm", "kv_latent_norm", "q_norm", "k_norm", "conv_b", "conv_ln_g",
         "conv_ln_b", "ffn2_norm", "post_norm"]
ALL_WEIGHTS = ["ffn1_norm", "ffn1_w_gate", "ffn1_w_up", "ffn1_w_down", "mix_norm", "w_in", "q_latent_norm", "w_uq",
               "kv_latent_norm", "w_ukv", "q_norm", "k_norm", "conv_w", "conv_b", "conv_ln_g", "conv_ln_b", "w_out",
               "ffn2_norm", "ffn2_w_gate", "ffn2_w_up", "ffn2_w_down", "post_norm"]


def _rope_tables(s):
    pos = jnp.arange(s, dtype=F32)
    inv_freq = 1.0 / (ROPE_THETA ** (jnp.arange(0, QK_ROPE, 2, dtype=F32) / QK_ROPE))
    ang = pos[:, None] * inv_freq[None, :]
    cos, sin = jnp.cos(ang), jnp.sin(ang)
    half = QK_ROPE // 2
    one = jnp.ones((s, QK_NOPE), F32)
    zero = jnp.zeros((s, QK_NOPE), F32)
    pad = HEAD_PAD - QK_DIM
    z16 = jnp.zeros((s, half), F32)
    cosf = jnp.concatenate([one, cos, cos, jnp.ones((s, pad), F32)], axis=1)
    s1 = jnp.concatenate([zero, -sin, z16, jnp.zeros((s, pad), F32)], axis=1)
    s2 = jnp.concatenate([zero, z16, sin, jnp.zeros((s, pad), F32)], axis=1)
    return cosf, s1, s2


def _ffn_fwd(n, w, names, l):
    a, b, hm = _ffn_up_fused(n, w[names[0]], w[names[1]], l)
    return a, b, hm, _blocked_sum("ffn_down", [hm], [w[names[2]]], l, "nn")


def _ffn_bwd(nt, a, b, hm, do, w, names, grads):
    gate, up, down = names
    da, db = _ffn_down_t_fused(do, w[down], a, b, 0)
    grads[down] = _wgrad("ffn_down_wgrad", hm, do, WIRE_DTYPE)
    g_gate, g_up = _wgrad_pair("ffn_up_wgrad", nt, da, db, 1, 0, None)
    grads[gate], grads[up] = g_gate[0], g_up[0]
    return [_blocked_sum("ffn_up_t", [da, db], [w[gate], w[up]], 0, "nt")]


def _vec(v, l):
    return v[l][None, :]


def _pad_lanes(v, width):
    return jnp.pad(v, [(0, 0)] * (v.ndim - 1) + [(0, width - v.shape[-1])])


def kernel(x, ffn1_norm, ffn1_w_gate, ffn1_w_up, ffn1_w_down, mix_norm, w_in, q_latent_norm, w_uq, kv_latent_norm, w_ukv, q_norm, k_norm, conv_w, conv_b, conv_ln_g, conv_ln_b, w_out, ffn2_norm, ffn2_w_gate, ffn2_w_up, ffn2_w_down, post_norm, loss_target, m_ffn1_norm, m_ffn1_w_gate, m_ffn1_w_up, m_ffn1_w_down, m_mix_norm, m_w_in, m_q_latent_norm, m_w_uq, m_kv_latent_norm, m_w_ukv, m_q_norm, m_k_norm, m_conv_w, m_conv_b, m_conv_ln_g, m_conv_ln_b, m_w_out, m_ffn2_norm, m_ffn2_w_gate, m_ffn2_w_up, m_ffn2_w_down, m_post_norm, v_ffn1_norm, v_ffn1_w_gate, v_ffn1_w_up, v_ffn1_w_down, v_mix_norm, v_w_in, v_q_latent_norm, v_w_uq, v_kv_latent_norm, v_w_ukv, v_q_norm, v_k_norm, v_conv_w, v_conv_b, v_conv_ln_g, v_conv_ln_b, v_w_out, v_ffn2_norm, v_ffn2_w_gate, v_ffn2_w_up, v_ffn2_w_down, v_post_norm):
    given = dict(locals())
    wts = {n: given[n] for n in ALL_WEIGHTS}
    mom = {n: given["m_" + n] for n in ALL_WEIGHTS}
    var = {n: given["v_" + n] for n in ALL_WEIGHTS}
    n_layers = ffn1_norm.shape[0]
    s, d = x.shape[1], x.shape[2]
    ql, kvl, cw = q_latent_norm.shape[-1], kv_latent_norm.shape[-1], conv_b.shape[-1]
    xs = x.reshape(s, d)
    target = loss_target.reshape(s, d)

    assert n_layers == 2, "the layers' exchanges run one behind the other"
    local = [jnp.pad(wts[n], ((0, 0), (0, 1), (0, 0))).astype(MM_DTYPE) if n == "conv_w" else wts[n].astype(MM_DTYPE)
             for n in SHARDED]
    L = n_layers
    k_n = len(SHARDED)
    my_c = lax.axis_index("c")
    my_chip = 2 * lax.axis_index("x") + lax.axis_index("y")
    place = jnp.stack([my_c, my_chip]).astype(jnp.int32)
    hw = N_HEADS * HEAD_PAD
    tabs = _rope_tables(s)
    pe_block = (2 * cw + ql + kvl) // HEAD_PAD
    gq_pad, gk_pad = _pad_lanes(q_norm, HEAD_PAD), _pad_lanes(k_norm, HEAD_PAD)
    tq = _tile(s, 256)

    def own_blocks(l):
        return [lax.dynamic_update_slice(lax.empty((1, N_CHIPS) + t.shape[1:], t.dtype), t[l][None, None],
                                         (0, my_chip, 0, 0)) for t in local]

    def layouts(w):
        win = w["w_in"].transpose(0, 2, 1, 3).reshape(1, d, -1)
        zc = lambda n: jnp.zeros((1, d, n), MM_DTYPE)
        o_pe, o_cv = ql + kvl, ql + kvl + QK_ROPE
        w["w_in"] = jnp.concatenate([win[..., o_cv:], win[..., :o_pe], zc(QK_NOPE), win[..., o_pe:o_cv],
                                     zc(HEAD_PAD - QK_DIM)], axis=-1)[:, None]
        wq = w["w_uq"].transpose(0, 2, 1, 3).reshape(1, ql, N_HEADS, QK_DIM)
        w["wq"] = _pad_lanes(wq, HEAD_PAD).reshape(1, 1, ql, hw)
        wkv = w["w_ukv"].transpose(0, 2, 1, 3).reshape(1, kvl, N_HEADS, QK_NOPE + V_DIM)
        w["wk"] = _pad_lanes(wkv[..., :QK_NOPE], HEAD_PAD).reshape(1, 1, kvl, hw)
        w["wv"] = _pad_lanes(wkv[..., QK_NOPE:], HEAD_PAD).reshape(1, 1, kvl, hw)
        w["wvt"] = w["wv"][:, 0].transpose(0, 2, 1)
        w["conv"] = w["conv_w"].transpose(0, 2, 1, 3).reshape(CONV_K + 1, cw).astype(F32)
        wout = w["w_out"].reshape(1, d, d)
        wo_a = wout[:, :N_HEADS * V_DIM].reshape(1, N_HEADS, V_DIM, d)
        w["wo_a"] = jnp.pad(wo_a, ((0, 0), (0, 0), (0, HEAD_PAD - V_DIM), (0, 0))).reshape(1, 1, hw, d)
        w["wo_c"] = wout[:, N_HEADS * V_DIM:][:, None]
        return w

    ici, d2d, sem_kj = (lambda l: _plan_gather_ici(k_n, l)), _plan_gather_d2d(k_n), (3 * k_n,)
    n_a = 3
    n_b = k_n - n_a
    own0 = own_blocks(0)
    landed = _exchange("gather0a_ici", "run", local[:n_a] + own0[:n_a], (3 * n_a,), _plan_gather_ici(n_a, 0))
    w0 = dict(zip(SHARDED[:n_a], _exchange("gather0a_d2d", "run", landed[n_a:], (3 * n_a,), _plan_gather_d2d(n_a))))
    local = landed[:n_a] + local[n_a:]
    sems_r, token_r, rest = _exchange("gather0b_ici_start", "start", local[n_a:] + own0[n_a:], (3 * n_b,),
                                      _plan_gather_ici(n_b, 0))

    def fwd_ffn(l, cur, w, tok):
        sv = {"x": cur}
        n1, sv["n1t"] = _norm_fwd(cur, _vec(ffn1_norm, l) + tok)
        sv["a1"], sv["b1"], sv["hm1"], o1 = _ffn_fwd(n1, w, SHARDED[0:3], 0)
        sv["x1"], sv["h"] = _resid_norm(cur, [o1], 0.5, _vec(mix_norm, l), MM_DTYPE)
        return sv

    def fwd_first(l, sv, w, tok):
        sv["p"] = _proj("w_in", sv["h"], w["w_in"], 0, "nn")[0]
        sv["cq"], sv["ckv"], ckvt = _mla_pre(sv["p"], _vec(q_latent_norm, l) + tok, _vec(kv_latent_norm, l), cw)
        sv["q_pre"] = _proj("wq", sv["cq"], w["wq"], 0, "nn")[0]
        sv["k_pre"] = _proj("wk", sv["ckv"], w["wk"], 0, "nn")[0]
        sv["v"] = _proj("wv", sv["ckv"], w["wv"], 0, "nn", MM_DTYPE)[0]
        sv["vt"] = _mm("wv_transposed", w["wvt"], ckvt, "nn", (s // tq,),
                       pl.BlockSpec((None, hw, kvl), lambda i: (0, 0, 0)), pl.BlockSpec((kvl, tq), lambda i: (0, i)),
                       pl.BlockSpec((None, hw, tq), lambda i: (i, 0, 0)),
                       jax.ShapeDtypeStruct((s // tq, hw, tq), MM_DTYPE), 1, None)
        sv["q"], sv["k"], sv["kt"] = _qk_post(sv["q_pre"], sv["k_pre"], sv["p"], pe_block, tabs,
                                              _vec(gq_pad, l), _vec(gk_pad, l))
        return sv

    def fwd_second(l, sv, w, tok):
        sv["c"] = _conv_fwd(sv["p"], w["conv"], _vec(conv_b, l) + tok, _vec(conv_ln_g, l), _vec(conv_ln_b, l))
        sv["o"], sv["lse"] = _attn_fwd(sv["q"], sv["k"], sv.pop("vt"))
        mix_a = _contract("wo_a", sv["o"][None], w["wo_a"], 0, "nn")
        mix_c = _contract("wo_c", sv["c"][None], w["wo_c"], 0, "nn")
        sv["x2"], n2, sv["n2t"] = _resid_norm(sv["x1"], [mix_a, mix_c], 1.0, _vec(ffn2_norm, l), MM_DTYPE, True)
        sv["a2"], sv["b2"], sv["hm2"], o2 = _ffn_fwd(n2, w, SHARDED[8:11], 0)
        sv["x3"], out = _final_fwd(sv["x2"], o2, _vec(post_norm, l))
        return out

    sv0 = fwd_ffn(0, xs, w0, token_r[0, 0])
    rest = _exchange("gather0b_ici_wait", "wait", rest, (3 * n_b,), _plan_gather_ici(n_b, 0), sems=sems_r, after=sv0["h"])
    w0.update(zip(SHARDED[n_a:], _exchange("gather0b_d2d", "run", rest[n_b:], (3 * n_b,), _plan_gather_d2d(n_b))))
    w0 = layouts(w0)
    local = local[:n_a] + rest[:n_b]
    sems_a, token_a, in_flight = _exchange("gather1_ici_start", "start", local + own_blocks(1), sem_kj, ici(1))
    sv0 = fwd_first(0, sv0, w0, token_a[0, 0])
    in_flight = _exchange("gather1_ici_wait", "wait", in_flight, sem_kj, ici(1), sems=sems_a, after=sv0["q"])
    sems_b, token_b, in_flight = _exchange("gather1_d2d_start", "start", in_flight[k_n:], sem_kj, d2d)
    cur = fwd_second(0, sv0, w0, token_b[0, 0])
    w1 = layouts(dict(zip(SHARDED, _exchange("gather1_d2d_wait", "wait", in_flight, sem_kj, d2d, sems=sems_b, after=cur))))
    sv1 = fwd_first(1, fwd_ffn(1, cur, w1, 0.0), w1, 0.0)
    cur = fwd_second(1, sv1, w1, 0.0)
    dy, loss_part = _loss(cur, target)

    small = {n: [None] * L for n in SMALL}

    def bwd_first(l, sv, dy, w, tok, grads):
        dx3, do2, small["post_norm"][l] = _norm_bwd(sv["x3"], _vec(post_norm, l) + tok, [dy], None, 0.5)
        dn2 = _ffn_bwd(sv["n2t"], sv["a2"], sv["b2"], sv["hm2"], do2, w, SHARDED[8:11], grads)
        dx2, dx2b, small["ffn2_norm"][l] = _norm_bwd(sv["x2"], _vec(ffn2_norm, l), dn2, dx3, 1.0)
        do = _proj("wo_a_t", dx2b, w["wo_a"], 0, "nt")[0]
        dc = _proj("wo_c_t", dx2b, w["wo_c"], 0, "nt")[0]
        g_wo_a = _wgrad("wo_a_wgrad", sv["o"], dx2b[None], WIRE_DTYPE)[0].reshape(N_HEADS, HEAD_PAD, d)[:, :V_DIM]
        g_wo_c = _wgrad("wo_c_wgrad", sv["c"], dx2b[None], WIRE_DTYPE)
        grads["w_out"] = jnp.concatenate(
            [g_wo_a.reshape(N_HEADS * V_DIM, d), g_wo_c[0]], axis=0).reshape(N_CHIPS, d // N_CHIPS, d)
        dq, dk, dv = _attn_bwd(sv["q"], sv["k"], sv["kt"], sv["v"], do, sv["lse"], _attn_delta(sv["o"], do))
        dq_pre, dk_pre, dpe, small["q_norm"][l], small["k_norm"][l] = _qk_post_bwd(
            sv["q_pre"], sv["k_pre"], sv["p"], pe_block, tabs, _vec(gq_pad, l), _vec(gk_pad, l), dq, dk)
        return dict(dx2=dx2, dc=dc, dq_pre=dq_pre, dk_pre=dk_pre, dv=dv, dpe=dpe)

    def bwd_second(l, sv, st, w, tok, grads):
        dq_pre, dk_pre, dv = st["dq_pre"], st["dk_pre"], st["dv"]
        dcq = _contract("wq_t", dq_pre[None], w["wq"], 0, "nt")
        dckv = [_contract("wk_t", dk_pre[None], w["wk"], 0, "nt"), _contract("wv_t", dv[None], w["wv"], 0, "nt")]
        per_head = lambda t, rows: t[0].reshape(rows, N_HEADS, HEAD_PAD)
        g_wq = per_head(_wgrad("wq_wgrad", sv["cq"], dq_pre[None], WIRE_DTYPE), ql)[..., :QK_DIM]
        grads["w_uq"] = g_wq.reshape(ql, N_CHIPS, -1).transpose(1, 0, 2)
        g_wkv = jnp.concatenate([per_head(_wgrad("wk_wgrad", sv["ckv"], dk_pre[None], WIRE_DTYPE), kvl)[..., :QK_NOPE],
                                 per_head(_wgrad("wv_wgrad", sv["ckv"], dv[None], WIRE_DTYPE), kvl)[..., :V_DIM]], axis=-1)
        grads["w_ukv"] = g_wkv.reshape(kvl, N_CHIPS, -1).transpose(1, 0, 2)
        dyc, small["conv_ln_g"][l], small["conv_ln_b"][l], small["conv_b"][l] = _conv_bwd_ln(
            sv["p"], w["conv"], _vec(conv_b, l) + tok, _vec(conv_ln_g, l), _vec(conv_ln_b, l), st["dc"])
        dp_conv, g_conv = _conv_bwd_taps(sv["p"], w["conv"], dyc)
        grads["conv_w"] = g_conv.reshape(CONV_K + 1, N_CHIPS, cw // N_CHIPS).transpose(1, 0, 2).astype(WIRE_DTYPE)
        dp, small["q_latent_norm"][l], small["kv_latent_norm"][l] = _dp_assemble(
            sv["p"], _vec(q_latent_norm, l), _vec(kv_latent_norm, l), dcq, dckv, dp_conv, st["dpe"], cw)
        dh = _contract("w_in_t", dp[None], w["w_in"], 0, "nt")
        g_win = _wgrad("w_in_wgrad", sv["h"], dp[None], WIRE_DTYPE)[0]
        o_q, o_pe_col = 2 * cw, 2 * cw + ql + kvl + QK_NOPE
        g_win = jnp.concatenate([g_win[:, o_q:o_q + ql + kvl], g_win[:, o_pe_col:o_pe_col + QK_ROPE], g_win[:, :o_q]], axis=1)
        grads["w_in"] = g_win.reshape(d, N_CHIPS, -1).transpose(1, 0, 2)
        dx1, do1, small["mix_norm"][l] = _norm_bwd(sv["x1"], _vec(mix_norm, l), [dh], st["dx2"], 0.5)
        dn1 = _ffn_bwd(sv["n1t"], sv["a1"], sv["b1"], sv["hm1"], do1, w, SHARDED[0:3], grads)
        dx, _, small["ffn1_norm"][l] = _norm_bwd(sv["x"], _vec(ffn1_norm, l), dn1, dx1, 1.0)
        return dx

    def halves(parts):
        return [lax.empty((N_CHIPS, t.shape[1] // 2, t.shape[2]), t.dtype) for t in parts]

    def thirds(parts):
        return [lax.empty((N_CHIPS - 1,) + t.shape[1:], t.dtype) for t in parts]

    def chip_sums(done):
        return [_add_sibling_half(place, g, t) for g, t in zip(done[:k_n], done[k_n:])]

    swap, scatter, sem_k = _plan_swap(k_n), _plan_scatter(k_n), (k_n,)
    grads1, grads0 = {}, {}
    st1 = bwd_first(1, sv1, dy, w1, 0.0, grads1)
    dy = bwd_second(1, sv1, st1, w1, 0.0, grads1)
    parts1 = [grads1[n] for n in SHARDED]
    sems_s, token_s, in_flight = _exchange("swap1_start", "start", parts1 + halves(parts1), sem_k, swap)
    st0 = bwd_first(0, sv0, dy, w0, token_s[0, 0], grads0)
    sums1 = chip_sums(_exchange("swap1_wait", "wait", in_flight, sem_k, swap, sems=sems_s, after=st0["dq_pre"]))
    sems_c, token_c, in_flight = _exchange("scatter1_start", "start", sums1 + thirds(sums1), sem_kj, scatter)
    dy = bwd_second(0, sv0, st0, w0, token_c[0, 0], grads0)
    grad_x = dy.reshape(x.shape)
    done = _exchange("scatter1_wait", "wait", in_flight, sem_kj, scatter, sems=sems_c, after=dy)
    outs = [_add_chips_half(place, own, others, L, 1, None) for own, others in zip(done[:k_n], done[k_n:])]
    parts0 = [grads0[n] for n in SHARDED]
    sums0 = chip_sums(_exchange("swap0", "run", parts0 + halves(parts0), sem_k, swap))
    done = _exchange("scatter0", "run", sums0 + thirds(sums0), sem_kj, scatter)
    outs = [_add_chips_half(place, own, others, L, 0, into) for own, others, into in zip(done[:k_n], done[k_n:], outs)]
    full = dict(zip(SHARDED, _exchange("share", "run", outs, (L * k_n,), _plan_share(k_n, L))))
    full["conv_w"] = full["conv_w"][:, :CONV_K]

    width = max(wts[n].shape[-1] for n in SMALL)
    pack = lambda vals: jnp.concatenate([_pad_lanes(v.reshape(-1, v.shape[-1]), width) for v in vals], axis=0)
    part = pack([jnp.concatenate(small[n], axis=0)[..., :wts[n].shape[-1]] for n in SMALL])
    rows = part.shape[0]
    part = jnp.pad(part, [(0, -rows % 8), (0, 0)])
    small_sum = _add_leading("add_devices", _gather_small(part), F32)
    for i, n in enumerate(SMALL):
        full[n] = small_sum[i * L:(i + 1) * L, :wts[n].shape[-1]]

    delta_w, new_m, new_v = {}, {}, {}
    for n in SHARDED:
        shape = wts[n].shape
        swap = shape[-1] % HEAD_PAD != 0 and shape[-2] % HEAD_PAD == 0
        view = (lambda t: jnp.swapaxes(t, 1, 2)) if swap else (lambda t: t)
        flat = lambda t: view(t).reshape(-1, view(t).shape[-1])
        r = flat(wts[n]).shape[0]
        fix = (lambda t: jnp.pad(flat(t), [(0, -r % 8), (0, 0)])) if r % 8 else flat
        outs = _adamw(fix(wts[n]), fix(full[n]), fix(mom[n]), fix(var[n]))
        delta_w[n], new_m[n], new_v[n] = [view(o[:r].reshape(view(wts[n]).shape)) for o in outs]
    pk = lambda src: jnp.pad(pack([src[n] for n in SMALL]), [(0, -rows % 8), (0, 0)])
    outs = _adamw(pk(wts), small_sum, pk(mom), pk(var))
    for i, n in enumerate(SMALL):
        delta_w[n], new_m[n], new_v[n] = [o[i * L:(i + 1) * L, :wts[n].shape[-1]] for o in outs]

    loss = lax.psum(loss_part[0, 0], ("x", "y", "c"))
    return (loss, grad_x, *[full[n] for n in ALL_WEIGHTS], *[delta_w[n] for n in ALL_WEIGHTS],
            *[new_m[n] for n in ALL_WEIGHTS], *[new_v[n] for n in ALL_WEIGHTS])
```

```python
import jax
import jax.numpy as jnp
from jax import lax
from jax.experimental import pallas as pl
from jax.experimental.pallas import tpu as pltpu

F32 = jnp.float32
MM_DTYPE = jnp.bfloat16
WIRE_DTYPE = jnp.bfloat16
EPS = 1e-6
N_HEADS = 8
QK_NOPE = 64
QK_ROPE = 32
QK_DIM = QK_NOPE + QK_ROPE
V_DIM = 64
HEAD_PAD = 128
HEADS_PER_STEP = 2
CHUNK_SHIFT = 6
LOG2E = 1.4426950408889634
CONV_K = 31
HALO = 32
ROPE_THETA = 10000.0
N_CHIPS = 4
ADAM_LR, ADAM_B1, ADAM_B2, ADAM_EPS, ADAM_WD, ADAM_STEP = 0.001, 0.9, 0.999, 1e-08, 0.01, 10
VMEM_LIMIT_BYTES = 56 * 2 ** 20
WGRAD_ROWS = 2048
MESH = pl.DeviceIdType.MESH
ANY = pl.BlockSpec(memory_space=pl.ANY)

_DIMS = {
    "nn": (((1,), (0,)), ((), ())),
    "nt": (((1,), (1,)), ((), ())),
    "tn": (((0,), (0,)), ((), ())),
}


def _params(n_axes):
    return pltpu.CompilerParams(dimension_semantics=("arbitrary",) * n_axes, vmem_limit_bytes=VMEM_LIMIT_BYTES)


def _tile(n, cap):
    for step in (16, 8):
        for t in range(cap - cap % step, 0, -step):
            if n % t == 0:
                return t
    return n


def _mm(name, a, b, dims, grid, a_spec, b_spec, o_spec, out_shape, nk, acc_shape, into=None):
    n_axes = len(grid)

    def body(*refs):
        if into is not None:
            refs = refs[1:]
        a_ref, b_ref, o_ref = refs[:3]
        part = lax.dot_general(a_ref[...].astype(MM_DTYPE), b_ref[...].astype(MM_DTYPE), _DIMS[dims],
                               preferred_element_type=F32)
        if nk == 1:
            o_ref[...] = part.astype(o_ref.dtype)
        else:
            acc_ref = refs[3]
            k = pl.program_id(n_axes - 1)

            @pl.when(k == 0)
            def _():
                acc_ref[...] = part

            @pl.when(k > 0)
            def _():
                acc_ref[...] += part

            @pl.when(k == nk - 1)
            def _():
                o_ref[...] = acc_ref[...].astype(o_ref.dtype)

    in_specs = [a_spec, b_spec]
    args = [a, b]
    aliases = {}
    if into is not None:
        in_specs = [ANY] + in_specs
        args = [into] + args
        aliases = {0: 0}
    return pl.pallas_call(
        body, name=name, grid=grid, in_specs=in_specs, out_specs=o_spec, out_shape=out_shape,
        scratch_shapes=[] if nk == 1 else [pltpu.VMEM(acc_shape, F32)],
        input_output_aliases=aliases, compiler_params=_params(n_axes))(*args)


def _proj(name, a, w, l, dims, out_dtype=F32):
    s, k = a.shape
    g = w.shape[1]
    n = w.shape[3] if dims == "nn" else w.shape[2]
    tm = _tile(s, 512)
    return _mm(name, a, w, dims, (g, s // tm),
               pl.BlockSpec((tm, k), lambda j, i: (i, 0)),
               pl.BlockSpec((None, None) + w.shape[2:], lambda j, i: (l, j, 0, 0)),
               pl.BlockSpec((None, tm, n), lambda j, i: (j, i, 0)),
               jax.ShapeDtypeStruct((g, s, n), out_dtype), 1, None)


def _contract(name, a, w, l, dims, out_dtype=F32):
    g, s, kb = a.shape
    n = w.shape[3] if dims == "nn" else w.shape[2]
    tm = _tile(s, 512)
    return _mm(name, a, w, dims, (s // tm, g),
               pl.BlockSpec((None, tm, kb), lambda i, j: (j, i, 0)),
               pl.BlockSpec((None, None) + w.shape[2:], lambda i, j: (l, j, 0, 0)),
               pl.BlockSpec((tm, n), lambda i, j: (i, 0)),
               jax.ShapeDtypeStruct((s, n), out_dtype), g, (tm, n))


def _wgrad(name, a, b, out_dtype, layers=None, l=None, into=None):
    g = a.shape[0] if a.ndim == 3 else b.shape[0]
    s, k = a.shape[-2:]
    n = b.shape[-1]
    tm = _tile(s, WGRAD_ROWS)

    def spec(x, w):
        if x.ndim == 3:
            return pl.BlockSpec((None, tm, w), lambda j, i: (j, i, 0))
        return pl.BlockSpec((tm, w), lambda j, i: (i, 0))

    if layers is None:
        o_spec = pl.BlockSpec((None, k, n), lambda j, i: (j, 0, 0))
        out_shape = jax.ShapeDtypeStruct((g, k, n), out_dtype)
    else:
        o_spec = pl.BlockSpec((None, None, k, n), lambda j, i: (l, j, 0, 0))
        out_shape = jax.ShapeDtypeStruct((layers, g, k, n), out_dtype)
    return _mm(name, a, b, "tn", (g, s // tm), spec(a, k), spec(b, n), o_spec, out_shape, s // tm, (k, n), into=into)


def _ffn_up_fused(n, wg, wu, l):
    s, k = n.shape
    g, n4 = wg.shape[1], wg.shape[3]
    tm = _tile(s, 512)

    def body(n_ref, wg_ref, wu_ref, a_ref, b_ref, h_ref):
        nv = n_ref[...]
        a = jnp.dot(nv, wg_ref[...], preferred_element_type=F32)
        b = jnp.dot(nv, wu_ref[...], preferred_element_type=F32)
        a_ref[...] = a.astype(a_ref.dtype)
        b_ref[...] = b.astype(b_ref.dtype)
        h_ref[...] = (a * _sigmoid(a) * b).astype(h_ref.dtype)

    w_spec = pl.BlockSpec((None, None, k, n4), lambda j, i: (l, j, 0, 0))
    o_spec = pl.BlockSpec((None, tm, n4), lambda j, i: (j, i, 0))
    return pl.pallas_call(
        body, name="ffn_up_fused", grid=(g, s // tm),
        in_specs=[pl.BlockSpec((tm, k), lambda j, i: (i, 0)), w_spec, w_spec], out_specs=[o_spec] * 3,
        out_shape=[jax.ShapeDtypeStruct((g, s, n4), MM_DTYPE)] * 3, compiler_params=_params(2))(n, wg, wu)


def _ffn_down_t_fused(do, wd, a, b, l):
    s, k = do.shape
    g, n4 = wd.shape[1], wd.shape[2]
    tm = _tile(s, 512)

    def body(do_ref, wd_ref, a_ref, b_ref, da_ref, db_ref):
        dh = lax.dot_general(do_ref[...], wd_ref[...], _DIMS["nt"], preferred_element_type=F32)
        av = a_ref[...].astype(F32)
        sg = _sigmoid(av)
        da_ref[...] = (dh * b_ref[...].astype(F32) * (sg * (1.0 + av * (1.0 - sg)))).astype(da_ref.dtype)
        db_ref[...] = (dh * (av * sg)).astype(db_ref.dtype)

    t_spec = pl.BlockSpec((None, tm, n4), lambda j, i: (j, i, 0))
    return pl.pallas_call(
        body, name="ffn_down_t_fused", grid=(g, s // tm),
        in_specs=[pl.BlockSpec((tm, k), lambda j, i: (i, 0)),
                  pl.BlockSpec((None, None, n4, k), lambda j, i: (l, j, 0, 0)), t_spec, t_spec],
        out_specs=[t_spec] * 2, out_shape=[jax.ShapeDtypeStruct((g, s, n4), MM_DTYPE)] * 2,
        compiler_params=_params(2))(do, wd, a, b)


def _blocked_sum(name, acts, weights, l, dims):
    g, s, kb = acts[0].shape
    n = weights[0].shape[3] if dims == "nn" else weights[0].shape[2]
    tm = _tile(s, 256)
    n_pairs = len(acts)

    def body(*refs):
        o_ref = refs[2 * n_pairs]
        acc = None
        for p in range(n_pairs):
            for j in range(g):
                t = lax.dot_general(refs[p][j], refs[n_pairs + p][j], _DIMS[dims], preferred_element_type=F32)
                acc = t if acc is None else acc + t
        o_ref[...] = acc

    return pl.pallas_call(
        body, name=name, grid=(s // tm,),
        in_specs=[pl.BlockSpec((g, tm, kb), lambda i: (0, i, 0))] * n_pairs
        + [pl.BlockSpec((None,) + w.shape[1:], lambda i: (l, 0, 0, 0)) for w in weights],
        out_specs=pl.BlockSpec((tm, n), lambda i: (i, 0)), out_shape=jax.ShapeDtypeStruct((s, n), F32),
        compiler_params=_params(1))(*acts, *weights)


def _wgrad_pair(name, nt, da, db, layers, l, into):
    k, s = nt.shape
    g, _, n4 = da.shape
    tm = _tile(s, WGRAD_ROWS)
    nk = s // tm

    def body(*refs):
        if into is not None:
            refs = refs[2:]
        nt_ref, da_ref, db_ref, og_ref, ou_ref, accg, accu = refs
        i = pl.program_id(1)
        ntv = nt_ref[...]
        pg = jnp.dot(ntv, da_ref[...], preferred_element_type=F32)
        pu = jnp.dot(ntv, db_ref[...], preferred_element_type=F32)

        @pl.when(i == 0)
        def _():
            accg[...] = pg
            accu[...] = pu

        @pl.when(i > 0)
        def _():
            accg[...] += pg
            accu[...] += pu

        @pl.when(i == nk - 1)
        def _():
            og_ref[...] = accg[...].astype(og_ref.dtype)
            ou_ref[...] = accu[...].astype(ou_ref.dtype)

    t_spec = pl.BlockSpec((None, tm, n4), lambda j, i: (j, i, 0))
    o_spec = pl.BlockSpec((None, None, k, n4), lambda j, i: (l, j, 0, 0))
    in_specs = [pl.BlockSpec((k, tm), lambda j, i: (0, i)), t_spec, t_spec]
    args = [nt, da, db]
    aliases = {}
    if into is not None:
        in_specs = [ANY, ANY] + in_specs
        args = list(into) + args
        aliases = {0: 0, 1: 1}
    return pl.pallas_call(
        body, name=name, grid=(g, nk), in_specs=in_specs, out_specs=[o_spec] * 2,
        out_shape=[jax.ShapeDtypeStruct((layers, g, k, n4), WIRE_DTYPE)] * 2,
        scratch_shapes=[pltpu.VMEM((k, n4), F32)] * 2, input_output_aliases=aliases,
        compiler_params=_params(2))(*args)


def _rows(name, fn, s, tm, ins, outs, scratch=()):
    n = s // tm

    def spec(shape, kind):
        nd = len(shape)
        if kind == "r":
            return pl.BlockSpec(shape[:-2] + (tm, shape[-1]), lambda i: (0,) * (nd - 2) + (i, 0))
        if kind in ("f", "a"):
            return pl.BlockSpec(shape, lambda i: (0,) * nd)
        if kind == "t":
            return pl.BlockSpec((shape[0], tm), lambda i: (0, i))
        if kind == "b":
            return pl.BlockSpec((None,) + shape[1:], lambda i: (i, 0, 0))
        if kind == "p":
            return pl.BlockSpec((HALO, shape[-1]), lambda i: (jnp.maximum(i * (tm // HALO) - 1, 0), 0))
        if kind == "n":
            return pl.BlockSpec((HALO, shape[-1]), lambda i: (jnp.minimum((i + 1) * (tm // HALO), s // HALO - 1), 0))
        j = kind[1]
        return pl.BlockSpec((tm, HEAD_PAD), lambda i: (i, j))

    def body(*refs):
        fn(pl.program_id(0), n, *refs)

    return pl.pallas_call(
        body, name=name, grid=(n,),
        in_specs=[spec(a.shape, kind) for a, kind in ins],
        out_specs=[spec(shape, kind) for shape, _, kind in outs],
        out_shape=[jax.ShapeDtypeStruct(shape, dtype) for shape, dtype, _ in outs],
        scratch_shapes=list(scratch), compiler_params=_params(1))(*[a for a, _ in ins])


def _acc(ref, i, val):
    @pl.when(i == 0)
    def _():
        ref[...] = val

    @pl.when(i > 0)
    def _():
        ref[...] += val


def _sum0(x):
    return jnp.sum(x, axis=0, keepdims=True)


def _rstd(x, n):
    return lax.rsqrt(jnp.sum(x * x, axis=-1, keepdims=True) * (1.0 / n) + EPS)


def _rms_bwd(x, g, dy, n):
    r = _rstd(x, n)
    xh = x * r
    dyg = dy * g
    dx = r * (dyg - xh * (jnp.sum(dyg * xh, axis=-1, keepdims=True) * (1.0 / n)))
    return dx, _sum0(dy * xh)


def _sigmoid(x):
    return 1.0 / (1.0 + jnp.exp(-x))


def _norm_fwd(x, g):
    s, d = x.shape

    def fn(i, n, x_ref, g_ref, o_ref, ot_ref):
        xv = x_ref[...]
        y = xv * _rstd(xv, d) * g_ref[...]
        o_ref[...] = y.astype(o_ref.dtype)
        ot_ref[...] = y.T.astype(ot_ref.dtype)

    return _rows("norm_fwd", fn, s, _tile(s, 256), [(x, "r"), (g, "f")],
                 [((s, d), MM_DTYPE, "r"), ((d, s), MM_DTYPE, "t")])


def _resid_norm(x, adds, scale, g, norm_dtype, transposed=False):
    s, d = x.shape
    k = len(adds)

    def fn(i, n, *refs):
        x_ref, add_refs, g_ref, xo_ref, no_ref = refs[0], refs[1:1 + k], refs[1 + k], refs[2 + k], refs[3 + k]
        tot = add_refs[0][...]
        for r in add_refs[1:]:
            tot = tot + r[...]
        xv = x_ref[...] + scale * tot
        xo_ref[...] = xv
        y = xv * _rstd(xv, d) * g_ref[...]
        no_ref[...] = y.astype(no_ref.dtype)
        if transposed:
            refs[4 + k][...] = y.T.astype(refs[4 + k].dtype)

    return _rows("resid_norm", fn, s, _tile(s, 256), [(x, "r")] + [(a, "r") for a in adds] + [(g, "f")],
                 [((s, d), F32, "r"), ((s, d), norm_dtype, "r")] + ([((d, s), MM_DTYPE, "t")] if transposed else []))


def _norm_bwd(x, g, dn_parts, dres, out_scale):
    s, d = x.shape
    k = len(dn_parts)
    has_res = dres is not None

    def fn(i, n, *refs):
        x_ref, g_ref = refs[0], refs[1]
        dn_refs = refs[2:2 + k]
        pos = 2 + k
        dn = dn_refs[0][...]
        for r in dn_refs[1:]:
            dn = dn + r[...]
        dx, dg = _rms_bwd(x_ref[...], g_ref[...], dn, d)
        if has_res:
            dx = dx + refs[pos][...]
            pos += 1
        dx_ref, dxs_ref, dg_ref = refs[pos:pos + 3]
        dx_ref[...] = dx
        dxs_ref[...] = (out_scale * dx).astype(dxs_ref.dtype)
        _acc(dg_ref, i, dg)

    ins = [(x, "r"), (g, "f")] + [(p, "r") for p in dn_parts] + ([(dres, "r")] if has_res else [])
    return _rows("norm_bwd", fn, s, _tile(s, 256), ins,
                 [((s, d), F32, "r"), ((s, d), MM_DTYPE, "r"), ((1, d), F32, "a")])


def _final_fwd(x, o, g):
    return _resid_norm(x, [o], 0.5, g, F32)


def _loss(y, t):
    s, d = y.shape

    def fn(i, n, y_ref, t_ref, dy_ref, l_ref):
        e = y_ref[...] - t_ref[...]
        dy_ref[...] = e * (1.0 / d)
        _acc(l_ref, i, (0.5 / d) * jnp.sum(jnp.sum(e * e, axis=-1, keepdims=True), axis=0, keepdims=True))

    return _rows("loss", fn, s, _tile(s, 256), [(y, "r"), (t, "r")], [((s, d), F32, "r"), ((1, 1), F32, "a")])


def _mla_pre(p, gql, gkvl, cw):
    s = p.shape[0]
    ql, kvl = gql.shape[-1], gkvl.shape[-1]
    o_q, o_kv = 2 * cw, 2 * cw + ql

    def fn(i, n, p_ref, gq_ref, gkv_ref, cq_ref, ckv_ref, ckvt_ref):
        cq = p_ref[:, o_q:o_q + ql]
        cq_ref[...] = (cq * _rstd(cq, ql) * gq_ref[...]).astype(cq_ref.dtype)
        ckv = p_ref[:, o_kv:o_kv + kvl]
        y = ckv * _rstd(ckv, kvl) * gkv_ref[...]
        ckv_ref[...] = y.astype(ckv_ref.dtype)
        ckvt_ref[...] = y.T.astype(ckvt_ref.dtype)

    return _rows("mla_pre", fn, s, _tile(s, 256), [(p, "r"), (gql, "f"), (gkvl, "f")],
                 [((s, ql), MM_DTYPE, "r"), ((s, kvl), MM_DTYPE, "r"), ((kvl, s), MM_DTYPE, "t")])


def _rope(x, cosf, s1, s2):
    return x * cosf + pltpu.roll(x, HEAD_PAD - 16, 1) * s1 + pltpu.roll(x, 16, 1) * s2


def _rope_bwd(d, cosf, s1, s2):
    return d * cosf + pltpu.roll(d * s1, 16, 1) + pltpu.roll(d * s2, HEAD_PAD - 16, 1)


def _qk_post(q_pre, k_pre, p, pe_block, tabs, gq, gk):
    s, hw = q_pre.shape
    cosf, s1, s2 = tabs

    tm = _tile(s, 256)

    def fn(i, n, q_ref, k_ref, pe_ref, c_ref, s1_ref, s2_ref, gq_ref, gk_ref, qo_ref, ko_ref, kt_ref):
        c, a1, a2, pe = c_ref[...], s1_ref[...], s2_ref[...], pe_ref[...]
        for hh in range(N_HEADS):
            cols = slice(hh * HEAD_PAD, (hh + 1) * HEAD_PAD)
            qv = q_ref[:, cols]
            qo_ref[:, cols] = _rope(qv * _rstd(qv, QK_DIM) * gq_ref[...], c, a1, a2).astype(qo_ref.dtype)
            kv = k_ref[:, cols] + pe
            kout = _rope(kv * _rstd(kv, QK_DIM) * gk_ref[...], c, a1, a2)
            ko_ref[:, cols] = kout.astype(ko_ref.dtype)
            kt_ref[cols, :] = kout.T.astype(kt_ref.dtype)

    return _rows("qk_post", fn, s, tm,
                 [(q_pre, "r"), (k_pre, "r"), (p, ("c", pe_block)), (cosf, "r"), (s1, "r"), (s2, "r"), (gq, "f"), (gk, "f")],
                 [((s, hw), MM_DTYPE, "r"), ((s, hw), MM_DTYPE, "r"), ((s // tm, hw, tm), MM_DTYPE, "b")])


def _qk_post_bwd(q_pre, k_pre, p, pe_block, tabs, gq, gk, dq, dk):
    s, hw = q_pre.shape
    cosf, s1, s2 = tabs

    def fn(i, n, q_ref, k_ref, pe_ref, c_ref, s1_ref, s2_ref, gq_ref, gk_ref, dq_ref, dk_ref,
           dqo_ref, dko_ref, dpe_ref, dgq_ref, dgk_ref):
        c, a1, a2, pe = c_ref[...], s1_ref[...], s2_ref[...], pe_ref[...]
        lane = lax.broadcasted_iota(jnp.int32, pe.shape, 1)
        is_pe = (lane >= QK_NOPE) & (lane < QK_DIM)
        dpe = jnp.zeros(pe.shape, F32)
        dgq = jnp.zeros((1, HEAD_PAD), F32)
        dgk = jnp.zeros((1, HEAD_PAD), F32)
        for hh in range(N_HEADS):
            cols = slice(hh * HEAD_PAD, (hh + 1) * HEAD_PAD)
            dxq, g1 = _rms_bwd(q_ref[:, cols], gq_ref[...], _rope_bwd(dq_ref[cols, :].T, c, a1, a2), QK_DIM)
            dqo_ref[:, cols] = dxq.astype(dqo_ref.dtype)
            dxk, g2 = _rms_bwd(k_ref[:, cols] + pe, gk_ref[...], _rope_bwd(dk_ref[:, cols], c, a1, a2), QK_DIM)
            dko_ref[:, cols] = dxk.astype(dko_ref.dtype)
            dpe = dpe + dxk
            dgq = dgq + g1
            dgk = dgk + g2
        dpe_ref[...] = jnp.where(is_pe, dpe, 0.0)
        _acc(dgq_ref, i, dgq)
        _acc(dgk_ref, i, dgk)

    return _rows("qk_post_bwd", fn, s, _tile(s, 256),
                 [(q_pre, "r"), (k_pre, "r"), (p, ("c", pe_block)), (cosf, "r"), (s1, "r"), (s2, "r"), (gq, "f"), (gk, "f"),
                  (dq, "b"), (dk, "r")],
                 [((s, hw), MM_DTYPE, "r"), ((s, hw), MM_DTYPE, "r"), ((s, HEAD_PAD), F32, "r"),
                  ((1, HEAD_PAD), F32, "a"), ((1, HEAD_PAD), F32, "a")])


def _chunk_mask(tq, key_rows):
    r = jnp.right_shift(lax.broadcasted_iota(jnp.int32, (tq, tq), 0), CHUNK_SHIFT)
    c = jnp.right_shift(lax.broadcasted_iota(jnp.int32, (tq, tq), 1), CHUNK_SHIFT)
    return (r <= c) if key_rows else (c <= r)


def _attn_specs(s, tq):
    w = HEADS_PER_STEP * HEAD_PAD
    tile = pl.BlockSpec((tq, w), lambda hb, i: (i, hb))
    whole = pl.BlockSpec((s, w), lambda hb, i: (0, hb))
    rows = pl.BlockSpec((None, None, HEADS_PER_STEP, tq), lambda hb, i: (hb, i, 0, 0))
    blocks = pl.BlockSpec((s // tq, w, tq), lambda hb, i: (0, hb, 0))
    return tile, whole, rows, blocks


def _head(hh):
    return slice(hh * HEAD_PAD, (hh + 1) * HEAD_PAD)


def _attn_fwd(q, k, vt):
    s, hw = q.shape
    tq = _tile(s, 256)
    nq = s // tq
    groups = N_HEADS // HEADS_PER_STEP
    scale = QK_DIM ** -0.5
    tile, whole, rows, blocks = _attn_specs(s, tq)

    def body(q_ref, k_ref, vt_ref, o_ref, lse_ref):
        qi = pl.program_id(1)
        qv = [q_ref[:, _head(hh)] for hh in range(HEADS_PER_STEP)]

        def scores(kb):
            off = pl.multiple_of(kb * tq, tq)
            return tuple(lax.dot_general(k_ref[pl.ds(off, tq), _head(hh)], qv[hh], _DIMS["nt"],
                                         preferred_element_type=F32) for hh in range(HEADS_PER_STEP))

        def update(kb, stats, raw, diag):
            heads = range(HEADS_PER_STEP)
            mn, al, pr, ln = [], [], [], []
            for hh in heads:
                m, l, _ = stats[hh]
                sc = raw[hh] * (scale * LOG2E)
                if diag:
                    sc = jnp.where(_chunk_mask(tq, True), sc, -1e30)
                mn.append(jnp.maximum(m, jnp.max(sc, axis=0, keepdims=True)))
                al.append(jnp.exp2(m - mn[hh]))
                p = jnp.exp2(sc - mn[hh])
                ln.append(al[hh] * l + jnp.sum(p, axis=0, keepdims=True))
                pr.append(p.astype(MM_DTYPE))
            return tuple((mn[hh], ln[hh], al[hh] * stats[hh][2]
                          + jnp.dot(vt_ref[kb, _head(hh), :], pr[hh], preferred_element_type=F32)) for hh in heads)

        def step(kb, carry):
            stats, raw = carry
            nxt = scores(kb + 1)
            return update(kb, stats, raw, False), nxt

        init = tuple((jnp.full((1, tq), -1e30, F32), jnp.zeros((1, tq), F32), jnp.zeros((HEAD_PAD, tq), F32))
                     for _ in range(HEADS_PER_STEP))
        stats, raw = lax.fori_loop(0, qi, step, (init, scores(0)))
        carry = update(qi, stats, raw, True)
        for hh in range(HEADS_PER_STEP):
            m, l, acc = carry[hh]
            o_ref[:, _head(hh)] = (acc / l).T
            lse_ref[hh:hh + 1, :] = m + jnp.log(l) * LOG2E

    return pl.pallas_call(
        body, name="attn_fwd", grid=(groups, nq),
        in_specs=[tile, whole, blocks], out_specs=[tile, rows],
        out_shape=[jax.ShapeDtypeStruct((s, hw), F32), jax.ShapeDtypeStruct((groups, nq, HEADS_PER_STEP, tq), F32)],
        compiler_params=_params(2))(q, k, vt)


def _attn_delta(o, do):
    s, hw = o.shape
    tq = _tile(s, 256)
    groups = N_HEADS // HEADS_PER_STEP
    tile, _, rows, _ = _attn_specs(s, tq)

    def body(o_ref, do_ref, d_ref):
        for hh in range(HEADS_PER_STEP):
            prod = o_ref[:, _head(hh)] * do_ref[:, _head(hh)]
            d_ref[hh:hh + 1, :] = jnp.sum(prod.T, axis=0, keepdims=True)

    return pl.pallas_call(
        body, name="attn_delta", grid=(groups, s // tq), in_specs=[tile, tile], out_specs=rows,
        out_shape=jax.ShapeDtypeStruct((groups, s // tq, HEADS_PER_STEP, tq), F32),
        compiler_params=_params(2))(o, do)


def _attn_bwd(q, k, kt, v, do, lse_row, delta_row):
    s, hw = q.shape
    tq = _tile(s, 256)
    nq = s // tq
    groups = N_HEADS // HEADS_PER_STEP
    w = HEADS_PER_STEP * HEAD_PAD
    scale = QK_DIM ** -0.5
    tile, whole, _, blocks = _attn_specs(s, tq)
    rowvec = pl.BlockSpec((None, nq, HEADS_PER_STEP, tq), lambda hb, i: (hb, 0, 0, 0))
    kt_tile = pl.BlockSpec((None, w, tq), lambda hb, i: (i, hb, 0))

    def body(q_ref, k_ref, kt_ref, v_ref, do_ref, lse_ref, delta_ref, dqt_ref, dk_ref, dv_ref):
        ki = pl.program_id(1)
        kv = [k_ref[:, _head(hh)] for hh in range(HEADS_PER_STEP)]
        vv = [v_ref[:, _head(hh)] for hh in range(HEADS_PER_STEP)]
        ktv = [kt_ref[_head(hh), :] for hh in range(HEADS_PER_STEP)]

        @pl.when(ki == 0)
        def _():
            dqt_ref[...] = jnp.zeros(dqt_ref.shape, F32)

        def block(qb, carry, diag):
            off = pl.multiple_of(qb * tq, tq)
            heads = range(HEADS_PER_STEP)
            qs = [q_ref[pl.ds(off, tq), _head(hh)] for hh in heads]
            dos = [do_ref[pl.ds(off, tq), _head(hh)].astype(MM_DTYPE) for hh in heads]
            raw = [lax.dot_general(kv[hh], qs[hh], _DIMS["nt"], preferred_element_type=F32) for hh in heads]
            dpr = [lax.dot_general(vv[hh], dos[hh], _DIMS["nt"], preferred_element_type=F32) for hh in heads]
            pr, ds = [], []
            for hh in heads:
                sc = raw[hh] * (scale * LOG2E)
                if diag:
                    sc = jnp.where(_chunk_mask(tq, True), sc, -1e30)
                p = jnp.exp2(sc - lse_ref[qb, hh:hh + 1, :])
                pr.append(p.astype(MM_DTYPE))
                ds.append((p * (dpr[hh] - delta_ref[qb, hh:hh + 1, :]) * scale).astype(MM_DTYPE))
            out = []
            for hh in heads:
                dk, dvv = carry[hh]
                dvv = dvv + jnp.dot(pr[hh], dos[hh], preferred_element_type=F32)
                dk = dk + jnp.dot(ds[hh], qs[hh], preferred_element_type=F32)
                dqt_ref[qb, _head(hh), :] += jnp.dot(ktv[hh], ds[hh], preferred_element_type=F32)
                out.append((dk, dvv))
            return tuple(out)

        init = tuple((jnp.zeros((tq, HEAD_PAD), F32), jnp.zeros((tq, HEAD_PAD), F32)) for _ in range(HEADS_PER_STEP))
        carry = lax.fori_loop(ki + 1, nq, lambda qb, c: block(qb, c, False), block(ki, init, True))
        for hh in range(HEADS_PER_STEP):
            dk_ref[:, _head(hh)] = carry[hh][0]
            dv_ref[:, _head(hh)] = carry[hh][1].astype(dv_ref.dtype)

    return pl.pallas_call(
        body, name="attn_bwd", grid=(groups, nq),
        in_specs=[whole, tile, kt_tile, tile, whole, rowvec, rowvec], out_specs=[blocks, tile, tile],
        out_shape=[jax.ShapeDtypeStruct((nq, hw, tq), F32), jax.ShapeDtypeStruct((s, hw), F32),
                   jax.ShapeDtypeStruct((s, hw), MM_DTYPE)],
        compiler_params=_params(2))(q, k, kt, v, do, lse_row, delta_row)


def _glu_with_halo(i, p_ref, ph_ref, ubuf, cw, tm):
    a, g = p_ref[:, 0:cw], p_ref[:, cw:2 * cw]
    ubuf[HALO:HALO + tm, :] = a * _sigmoid(g)
    ah, gh = ph_ref[:, 0:cw], ph_ref[:, cw:2 * cw]
    ubuf[0:HALO, :] = jnp.where(i > 0, ah * _sigmoid(gh), 0.0)
    return a, g


def _conv_taps(ubuf, w_ref, tm):
    base = HALO - (CONV_K - 1)
    y = w_ref[0:1, :] * ubuf[base:base + tm, :]
    for k in range(1, CONV_K):
        y = y + w_ref[k:k + 1, :] * ubuf[base + k:base + k + tm, :]
    return y


def _layer_norm_stats(y, cw):
    mu = jnp.sum(y, axis=-1, keepdims=True) * (1.0 / cw)
    yc = y - mu
    r = lax.rsqrt(jnp.sum(yc * yc, axis=-1, keepdims=True) * (1.0 / cw) + EPS)
    return yc * r, r


def _conv_fwd(p, w, cb, lg, lb):
    s = p.shape[0]
    cw = cb.shape[-1]
    tm = _tile(s, 256)

    def fn(i, n, p_ref, ph_ref, w_ref, cb_ref, lg_ref, lb_ref, o_ref, ubuf):
        _glu_with_halo(i, p_ref, ph_ref, ubuf, cw, tm)
        yh, _ = _layer_norm_stats(_conv_taps(ubuf, w_ref, tm) + cb_ref[...], cw)
        z = yh * lg_ref[...] + lb_ref[...]
        o_ref[...] = (z * _sigmoid(z)).astype(o_ref.dtype)

    return _rows("conv_fwd", fn, s, tm, [(p, "r"), (p, "p"), (w, "f"), (cb, "f"), (lg, "f"), (lb, "f")],
                 [((s, cw), MM_DTYPE, "r")], scratch=[pltpu.VMEM((tm + HALO, cw), F32)])[0]


def _conv_bwd_ln(p, w, cb, lg, lb, dc):
    s = p.shape[0]
    cw = cb.shape[-1]
    tm = _tile(s, 256)

    def fn(i, n, p_ref, ph_ref, w_ref, cb_ref, lg_ref, lb_ref, dc_ref, dy_ref, dlg_ref, dlb_ref, dcb_ref, ubuf):
        _glu_with_halo(i, p_ref, ph_ref, ubuf, cw, tm)
        yh, r = _layer_norm_stats(_conv_taps(ubuf, w_ref, tm) + cb_ref[...], cw)
        z = yh * lg_ref[...] + lb_ref[...]
        sg = _sigmoid(z)
        dz = dc_ref[...] * (sg * (1.0 + z * (1.0 - sg)))
        dyh = dz * lg_ref[...]
        m1 = jnp.sum(dyh, axis=-1, keepdims=True) * (1.0 / cw)
        m2 = jnp.sum(dyh * yh, axis=-1, keepdims=True) * (1.0 / cw)
        dy = r * (dyh - m1 - yh * m2)
        dy_ref[...] = dy
        _acc(dlg_ref, i, _sum0(dz * yh))
        _acc(dlb_ref, i, _sum0(dz))
        _acc(dcb_ref, i, _sum0(dy))

    return _rows("conv_bwd_ln", fn, s, tm,
                 [(p, "r"), (p, "p"), (w, "f"), (cb, "f"), (lg, "f"), (lb, "f"), (dc, "r")],
                 [((s, cw), F32, "r"), ((1, cw), F32, "a"), ((1, cw), F32, "a"), ((1, cw), F32, "a")],
                 scratch=[pltpu.VMEM((tm + HALO, cw), F32)])


def _conv_bwd_taps(p, w, dy):
    s = p.shape[0]
    cw = w.shape[-1]
    tm = _tile(s, 256)

    def fn(i, n, p_ref, ph_ref, w_ref, dy_ref, dyn_ref, dp_ref, dw_ref, ubuf, dybuf):
        a, g = _glu_with_halo(i, p_ref, ph_ref, ubuf, cw, tm)
        dyv = dy_ref[...]
        dybuf[0:tm, :] = dyv
        dybuf[tm:tm + HALO, :] = jnp.where(i < n - 1, dyn_ref[...], 0.0)
        base = HALO - (CONV_K - 1)
        du = jnp.zeros((tm, cw), F32)

        @pl.when(i == 0)
        def _():
            dw_ref[...] = jnp.zeros(dw_ref.shape, F32)

        for k in range(CONV_K):
            sh = CONV_K - 1 - k
            du = du + w_ref[k:k + 1, :] * dybuf[sh:sh + tm, :]
            dw_ref[k:k + 1, :] += _sum0(dyv * ubuf[base + k:base + k + tm, :])
        sg = _sigmoid(g)
        dp_ref[:, 0:cw] = (du * sg).astype(dp_ref.dtype)
        dp_ref[:, cw:2 * cw] = (du * a * sg * (1.0 - sg)).astype(dp_ref.dtype)

    return _rows("conv_bwd_taps", fn, s, tm, [(p, "r"), (p, "p"), (w, "f"), (dy, "r"), (dy, "n")],
                 [((s, 2 * cw), MM_DTYPE, "r"), ((CONV_K + 1, cw), F32, "a")],
                 scratch=[pltpu.VMEM((tm + HALO, cw), F32), pltpu.VMEM((tm + HALO, cw), F32)])


def _dp_assemble(p, gql, gkvl, dcq, dckv_parts, dp_conv, dpe, cw):
    s, width = p.shape
    ql, kvl = gql.shape[-1], gkvl.shape[-1]
    o_q, o_kv, o_pe = 2 * cw, 2 * cw + ql, 2 * cw + ql + kvl

    def fn(i, n, p_ref, gq_ref, gkv_ref, dcq_ref, dk1_ref, dk2_ref, dconv_ref, dpe_ref, dp_ref, dgq_ref, dgkv_ref):
        dp_ref[:, 0:o_q] = dconv_ref[...]
        dx, dg = _rms_bwd(p_ref[:, o_q:o_kv], gq_ref[...], dcq_ref[...], ql)
        dp_ref[:, o_q:o_kv] = dx.astype(dp_ref.dtype)
        _acc(dgq_ref, i, dg)
        dx, dg = _rms_bwd(p_ref[:, o_kv:o_pe], gkv_ref[...], dk1_ref[...] + dk2_ref[...], kvl)
        dp_ref[:, o_kv:o_pe] = dx.astype(dp_ref.dtype)
        _acc(dgkv_ref, i, dg)
        dp_ref[:, o_pe:width] = dpe_ref[...].astype(dp_ref.dtype)

    return _rows("dp_assemble", fn, s, _tile(s, 256),
                 [(p, "r"), (gql, "f"), (gkvl, "f"), (dcq, "r"), (dckv_parts[0], "r"), (dckv_parts[1], "r"),
                  (dp_conv, "r"), (dpe, "r")],
                 [((s, width), MM_DTYPE, "r"), ((1, ql), F32, "a"), ((1, kvl), F32, "a")])


def _add_leading(name, x, out_dtype):
    n, r, c = x.shape
    tm = _tile(r, 256)

    def fn(i, nt, x_ref, o_ref):
        tot = x_ref[0].astype(F32)
        for k in range(1, n):
            tot = tot + x_ref[k].astype(F32)
        o_ref[...] = tot.astype(o_ref.dtype)

    return _rows(name, fn, r, tm, [(x, "r")], [((r, c), out_dtype, "r")])[0]


def _prefetch_call(name, fn, idx, grid, in_specs, out_spec, out_shape, args, aliases=None):
    return pl.pallas_call(
        fn, name=name, out_shape=out_shape,
        grid_spec=pltpu.PrefetchScalarGridSpec(num_scalar_prefetch=1, grid=grid, in_specs=in_specs, out_specs=out_spec),
        input_output_aliases=aliases or {}, compiler_params=_params(len(grid)))(idx, *args)


def _adamw(w, g, m, v):
    r, c = w.shape
    tm = _tile(r, 256)

    def fn(i, n, w_ref, g_ref, m_ref, v_ref, d_ref, mo_ref, vo_ref):
        gv = g_ref[...]
        mn = ADAM_B1 * m_ref[...] + (1.0 - ADAM_B1) * gv
        vn = ADAM_B2 * v_ref[...] + (1.0 - ADAM_B2) * (gv * gv)
        mo_ref[...] = mn
        vo_ref[...] = vn
        m_hat = mn / (1.0 - ADAM_B1 ** ADAM_STEP)
        v_hat = vn / (1.0 - ADAM_B2 ** ADAM_STEP)
        d_ref[...] = -ADAM_LR * (m_hat / (jnp.sqrt(v_hat) + ADAM_EPS) + ADAM_WD * w_ref[...])

    return _rows("adamw", fn, r, tm, [(w, "r"), (g, "r"), (m, "r"), (v, "r")], [((r, c), F32, "r")] * 3)


def _place():
    x, y, c = lax.axis_index("x"), lax.axis_index("y"), lax.axis_index("c")
    other_chips = [(1 - x, y), (x, 1 - y), (1 - x, 1 - y)]
    return x, y, c, 2 * x + y, other_chips


def _gather_small(vec):
    flips = [(fx, fy, fc) for fx in (0, 1) for fy in (0, 1) for fc in (0, 1)][1:]

    def body(src, dst, send, recv, lsem):
        x, y, c, _, _ = _place()
        me = 4 * x + 2 * y + c
        local = pltpu.make_async_copy(src, dst.at[me], lsem)
        local.start()
        away = []
        for j, (fx, fy, fc) in enumerate(flips):
            px, py, pc = x ^ fx, y ^ fy, c ^ fc
            away.append(pltpu.make_async_remote_copy(
                src_ref=src, dst_ref=dst.at[me], send_sem=send.at[j], recv_sem=recv.at[j],
                device_id=(px, py, pc), device_id_type=MESH))
        for cp in away:
            cp.start()
        for j, (fx, fy, fc) in enumerate(flips):
            blk = dst.at[4 * (x ^ fx) + 2 * (y ^ fy) + (c ^ fc)]
            pltpu.make_async_remote_copy(
                src_ref=blk, dst_ref=blk, send_sem=send.at[j], recv_sem=recv.at[j],
                device_id=(x ^ fx, y ^ fy, c ^ fc), device_id_type=MESH).wait_recv()
        for cp in away:
            cp.wait_send()
        local.wait()

    return pl.pallas_call(
        body, name="gather_small", in_specs=[ANY], out_specs=ANY,
        out_shape=jax.ShapeDtypeStruct((8,) + vec.shape, vec.dtype),
        scratch_shapes=[pltpu.SemaphoreType.DMA((7,)), pltpu.SemaphoreType.DMA((7,)), pltpu.SemaphoreType.DMA(())],
        compiler_params=pltpu.CompilerParams(has_side_effects=True))(vec)


HBM = pl.BlockSpec(memory_space=pltpu.HBM)
SEM = pl.BlockSpec(memory_space=pltpu.SEMAPHORE)
EFFECT = pltpu.SideEffectType.DATAFLOW_SIDE_EFFECTING


def _exchange(name, mode, arrays, sem_shape, plan, sems=None, after=None):
    n = len(arrays)
    hbm = [pltpu.HBM(a.shape, a.dtype) for a in arrays]
    arrays = [pltpu.with_memory_space_constraint(a, pltpu.HBM) for a in arrays]
    if mode == "run":
        def body(*refs):
            outs, send, recv = refs[n:2 * n], refs[2 * n], refs[2 * n + 1]
            sends, recvs = plan(outs, send, recv)
            for cp in sends:
                cp.start()
            for cp in recvs:
                cp.wait_recv()
            for cp in sends:
                cp.wait_send()

        return list(pl.pallas_call(
            body, name=name, in_specs=[HBM] * n, out_specs=[HBM] * n, out_shape=hbm,
            input_output_aliases={i: i for i in range(n)},
            scratch_shapes=[pltpu.SemaphoreType.DMA(sem_shape), pltpu.SemaphoreType.DMA(sem_shape)],
            compiler_params=pltpu.CompilerParams(has_side_effects=EFFECT))(*arrays))
    if mode == "start":
        def body(*refs):
            send, recv, token, outs = refs[n], refs[n + 1], refs[n + 2], refs[n + 3:]
            sends, _ = plan(outs, send, recv)
            for cp in sends:
                cp.start()
            token[...] = jnp.zeros(token.shape, F32)

        res = pl.pallas_call(
            body, name=name, in_specs=[HBM] * n,
            out_specs=[SEM, SEM, pl.BlockSpec(memory_space=pltpu.VMEM)] + [HBM] * n,
            out_shape=[pltpu.SemaphoreType.DMA(sem_shape), pltpu.SemaphoreType.DMA(sem_shape),
                       jax.ShapeDtypeStruct((8, HEAD_PAD), F32)] + hbm,
            input_output_aliases={i: i + 3 for i in range(n)},
            compiler_params=pltpu.CompilerParams(has_side_effects=EFFECT))(*arrays)
        return (res[0], res[1]), res[2], list(res[3:])

    def body(*refs):
        ins, send, recv = refs[:n], refs[n], refs[n + 1]
        sends, recvs = plan(ins, send, recv)
        for cp in sends:
            cp.wait_send()
        for cp in recvs:
            cp.wait_recv()

    return list(pl.pallas_call(
        body, name=name, in_specs=[HBM] * n + [SEM, SEM, ANY], out_specs=[HBM] * n, out_shape=hbm,
        input_output_aliases={i: i for i in range(n)},
        compiler_params=pltpu.CompilerParams(has_side_effects=EFFECT))(*arrays, sems[0], sems[1], after))


def _rows_half(ref_rows, half):
    h = ref_rows // 2
    return pl.ds(pl.multiple_of(half * h, 8), h)


def _remote(src, dst, send_sem, recv_sem, device):
    return pltpu.make_async_remote_copy(src_ref=src, dst_ref=dst, send_sem=send_sem, recv_sem=recv_sem,
                                        device_id=device, device_id_type=MESH)


def _plan_gather_ici(k_n, l):
    def plan(refs, send, recv):
        x, y, c, me, chips = _place()
        sends, recvs = [], []
        for k in range(k_n):
            src, buf = refs[k], refs[k_n + k]
            rows = _rows_half(src.shape[1], c)
            for j, (cx, cy) in enumerate(chips):
                sends.append(_remote(src.at[l, rows], buf.at[0, me, rows], send.at[3 * k + j], recv.at[3 * k + j], (cx, cy, c)))
                got = buf.at[0, 2 * cx + cy, rows]
                recvs.append(_remote(got, got, send.at[3 * k + j], recv.at[3 * k + j], (cx, cy, c)))
        return sends, recvs
    return plan


def _plan_gather_d2d(k_n):
    def plan(refs, send, recv):
        x, y, c, me, chips = _place()
        sends, recvs = [], []
        for k in range(k_n):
            buf = refs[k]
            for j, (cx, cy) in enumerate(chips):
                mine = buf.at[0, 2 * cx + cy, _rows_half(buf.shape[2], c)]
                theirs = buf.at[0, 2 * cx + cy, _rows_half(buf.shape[2], 1 - c)]
                sends.append(_remote(mine, mine, send.at[3 * k + j], recv.at[3 * k + j], (x, y, 1 - c)))
                recvs.append(_remote(theirs, theirs, send.at[3 * k + j], recv.at[3 * k + j], (x, y, 1 - c)))
        return sends, recvs
    return plan


def _plan_swap(k_n):
    def plan(refs, send, recv):
        x, y, c, me, chips = _place()
        sends, recvs = [], []
        for k in range(k_n):
            src, land = refs[k], refs[k_n + k]
            sends.append(_remote(src.at[:, _rows_half(src.shape[1], 1 - c)], land, send.at[k], recv.at[k], (x, y, 1 - c)))
            recvs.append(_remote(land, land, send.at[k], recv.at[k], (x, y, 1 - c)))
        return sends, recvs
    return plan


def _plan_scatter(k_n):
    def plan(refs, send, recv):
        x, y, c, me, chips = _place()
        sends, recvs = [], []
        for k in range(k_n):
            src, land = refs[k], refs[k_n + k]
            for j, (cx, cy) in enumerate(chips):
                sends.append(_remote(src.at[2 * cx + cy], land.at[j], send.at[3 * k + j], recv.at[3 * k + j], (cx, cy, c)))
                recvs.append(_remote(land.at[j], land.at[j], send.at[3 * k + j], recv.at[3 * k + j], (cx, cy, c)))
        return sends, recvs
    return plan


def _plan_share(k_n, layers):
    def plan(refs, send, recv):
        x, y, c, me, chips = _place()
        sends, recvs = [], []
        for k in range(k_n):
            out = refs[k]
            for l in range(layers):
                mine = out.at[l, _rows_half(out.shape[1], c)]
                theirs = out.at[l, _rows_half(out.shape[1], 1 - c)]
                sends.append(_remote(mine, mine, send.at[layers * k + l], recv.at[layers * k + l], (x, y, 1 - c)))
                recvs.append(_remote(theirs, theirs, send.at[layers * k + l], recv.at[layers * k + l], (x, y, 1 - c)))
        return sends, recvs
    return plan


def _add_sibling_half(idx, mine, theirs):
    q, r, c = mine.shape
    h = r // 2
    tm = _tile(h, 256)

    def fn(idx_ref, a_ref, b_ref, o_ref):
        o_ref[...] = (a_ref[...].astype(F32) + b_ref[...].astype(F32)).astype(o_ref.dtype)

    return _prefetch_call(
        "add_sibling", fn, idx, (q, h // tm),
        [pl.BlockSpec((None, None, tm, c), lambda b, i, ix: (b, ix[0], i, 0)),
         pl.BlockSpec((None, tm, c), lambda b, i, ix: (b, i, 0))],
        pl.BlockSpec((None, tm, c), lambda b, i, ix: (b, i, 0)), jax.ShapeDtypeStruct((q, h, c), WIRE_DTYPE),
        [mine.reshape(q, 2, h, c), theirs])


def _add_chips_half(idx, own, others, layers, l, into):
    _, h, c = own.shape
    tm = _tile(h, 256)

    def fn(*refs):
        a_ref, b_ref, o_ref = refs[-3:]
        tot = a_ref[...].astype(F32)
        for k in range(3):
            tot = tot + b_ref[k].astype(F32)
        o_ref[...] = tot

    in_specs = [pl.BlockSpec((None, tm, c), lambda i, ix: (ix[1], i, 0)), pl.BlockSpec((3, tm, c), lambda i, ix: (0, i, 0))]
    args = [own, others]
    aliases = None
    if into is not None:
        in_specs = [ANY] + in_specs
        args = [into.reshape(layers, 2, h, c)] + args
        aliases = {1: 0}
    out = _prefetch_call(
        "add_chips", fn, idx, (h // tm,), in_specs,
        pl.BlockSpec((None, None, tm, c), lambda i, ix: (l, ix[0], i, 0)),
        jax.ShapeDtypeStruct((layers, 2, h, c), F32), args, aliases)
    return out.reshape(layers, 2 * h, c)


SHARDED = ["ffn1_w_gate", "ffn1_w_up", "ffn1_w_down", "w_in", "w_uq", "w_ukv", "conv_w", "w_out",
           "ffn2_w_gate", "ffn2_w_up", "ffn2_w_down"]
SMALL = ["ffn1_norm", "mix_norm", "q_latent_norm", "kv_latent_norm", "q_norm", "k_norm", "conv_b", "conv_ln_g",
         "conv_ln_b", "ffn2_norm", "post_norm"]
ALL_WEIGHTS = ["ffn1_norm", "ffn1_w_gate", "ffn1_w_up", "ffn1_w_down", "mix_norm", "w_in", "q_latent_norm", "w_uq",
               "kv_latent_norm", "w_ukv", "q_norm", "k_norm", "conv_w", "conv_b", "conv_ln_g", "conv_ln_b", "w_out",
               "ffn2_norm", "ffn2_w_gate", "ffn2_w_up", "ffn2_w_down", "post_norm"]


def _rope_tables(s):
    pos = jnp.arange(s, dtype=F32)
    inv_freq = 1.0 / (ROPE_THETA ** (jnp.arange(0, QK_ROPE, 2, dtype=F32) / QK_ROPE))
    ang = pos[:, None] * inv_freq[None, :]
    cos, sin = jnp.cos(ang), jnp.sin(ang)
    half = QK_ROPE // 2
    one = jnp.ones((s, QK_NOPE), F32)
    zero = jnp.zeros((s, QK_NOPE), F32)
    pad = HEAD_PAD - QK_DIM
    z16 = jnp.zeros((s, half), F32)
    cosf = jnp.concatenate([one, cos, cos, jnp.ones((s, pad), F32)], axis=1)
    s1 = jnp.concatenate([zero, -sin, z16, jnp.zeros((s, pad), F32)], axis=1)
    s2 = jnp.concatenate([zero, z16, sin, jnp.zeros((s, pad), F32)], axis=1)
    return cosf, s1, s2


def _ffn_fwd(n, w, names, l):
    a, b, hm = _ffn_up_fused(n, w[names[0]], w[names[1]], l)
    return a, b, hm, _blocked_sum("ffn_down", [hm], [w[names[2]]], l, "nn")


def _ffn_bwd(nt, a, b, hm, do, w, names, grads):
    gate, up, down = names
    da, db = _ffn_down_t_fused(do, w[down], a, b, 0)
    grads[down] = _wgrad("ffn_down_wgrad", hm, do, WIRE_DTYPE)
    g_gate, g_up = _wgrad_pair("ffn_up_wgrad", nt, da, db, 1, 0, None)
    grads[gate], grads[up] = g_gate[0], g_up[0]
    return [_blocked_sum("ffn_up_t", [da, db], [w[gate], w[up]], 0, "nt")]


def _vec(v, l):
    return v[l][None, :]


def _pad_lanes(v, width):
    return jnp.pad(v, [(0, 0)] * (v.ndim - 1) + [(0, width - v.shape[-1])])


def kernel(x, ffn1_norm, ffn1_w_gate, ffn1_w_up, ffn1_w_down, mix_norm, w_in, q_latent_norm, w_uq, kv_latent_norm, w_ukv, q_norm, k_norm, conv_w, conv_b, conv_ln_g, conv_ln_b, w_out, ffn2_norm, ffn2_w_gate, ffn2_w_up, ffn2_w_down, post_norm, loss_target, m_ffn1_norm, m_ffn1_w_gate, m_ffn1_w_up, m_ffn1_w_down, m_mix_norm, m_w_in, m_q_latent_norm, m_w_uq, m_kv_latent_norm, m_w_ukv, m_q_norm, m_k_norm, m_conv_w, m_conv_b, m_conv_ln_g, m_conv_ln_b, m_w_out, m_ffn2_norm, m_ffn2_w_gate, m_ffn2_w_up, m_ffn2_w_down, m_post_norm, v_ffn1_norm, v_ffn1_w_gate, v_ffn1_w_up, v_ffn1_w_down, v_mix_norm, v_w_in, v_q_latent_norm, v_w_uq, v_kv_latent_norm, v_w_ukv, v_q_norm, v_k_norm, v_conv_w, v_conv_b, v_conv_ln_g, v_conv_ln_b, v_w_out, v_ffn2_norm, v_ffn2_w_gate, v_ffn2_w_up, v_ffn2_w_down, v_post_norm):
    given = dict(locals())
    wts = {n: given[n] for n in ALL_WEIGHTS}
    mom = {n: given["m_" + n] for n in ALL_WEIGHTS}
    var = {n: given["v_" + n] for n in ALL_WEIGHTS}
    n_layers = ffn1_norm.shape[0]
    s, d = x.shape[1], x.shape[2]
    ql, kvl, cw = q_latent_norm.shape[-1], kv_latent_norm.shape[-1], conv_b.shape[-1]
    xs = x.reshape(s, d)
    target = loss_target.reshape(s, d)

    assert n_layers == 2, "the layers' exchanges run one behind the other"
    local = [jnp.pad(wts[n], ((0, 0), (0, 1), (0, 0))).astype(MM_DTYPE) if n == "conv_w" else wts[n].astype(MM_DTYPE)
             for n in SHARDED]
    L = n_layers
    k_n = len(SHARDED)
    my_c = lax.axis_index("c")
    my_chip = 2 * lax.axis_index("x") + lax.axis_index("y")
    place = jnp.stack([my_c, my_chip]).astype(jnp.int32)
    hw = N_HEADS * HEAD_PAD
    tabs = _rope_tables(s)
    pe_block = (2 * cw + ql + kvl) // HEAD_PAD
    gq_pad, gk_pad = _pad_lanes(q_norm, HEAD_PAD), _pad_lanes(k_norm, HEAD_PAD)
    tq = _tile(s, 256)

    def own_blocks(l):
        return [lax.dynamic_update_slice(lax.empty((1, N_CHIPS) + t.shape[1:], t.dtype), t[l][None, None],
                                         (0, my_chip, 0, 0)) for t in local]

    def layouts(w):
        win = w["w_in"].transpose(0, 2, 1, 3).reshape(1, d, -1)
        zc = lambda n: jnp.zeros((1, d, n), MM_DTYPE)
        o_pe, o_cv = ql + kvl, ql + kvl + QK_ROPE
        w["w_in"] = jnp.concatenate([win[..., o_cv:], win[..., :o_pe], zc(QK_NOPE), win[..., o_pe:o_cv],
                                     zc(HEAD_PAD - QK_DIM)], axis=-1)[:, None]
        wq = w["w_uq"].transpose(0, 2, 1, 3).reshape(1, ql, N_HEADS, QK_DIM)
        w["wq"] = _pad_lanes(wq, HEAD_PAD).reshape(1, 1, ql, hw)
        wkv = w["w_ukv"].transpose(0, 2, 1, 3).reshape(1, kvl, N_HEADS, QK_NOPE + V_DIM)
        w["wk"] = _pad_lanes(wkv[..., :QK_NOPE], HEAD_PAD).reshape(1, 1, kvl, hw)
        w["wv"] = _pad_lanes(wkv[..., QK_NOPE:], HEAD_PAD).reshape(1, 1, kvl, hw)
        w["wvt"] = w["wv"][:, 0].transpose(0, 2, 1)
        w["conv"] = w["conv_w"].transpose(0, 2, 1, 3).reshape(CONV_K + 1, cw).astype(F32)
        wout = w["w_out"].reshape(1, d, d)
        wo_a = wout[:, :N_HEADS * V_DIM].reshape(1, N_HEADS, V_DIM, d)
        w["wo_a"] = jnp.pad(wo_a, ((0, 0), (0, 0), (0, HEAD_PAD - V_DIM), (0, 0))).reshape(1, 1, hw, d)
        w["wo_c"] = wout[:, N_HEADS * V_DIM:][:, None]
        return w

    ici, d2d, sem_kj = (lambda l: _plan_gather_ici(k_n, l)), _plan_gather_d2d(k_n), (3 * k_n,)
    n_a = 3
    n_b = k_n - n_a
    own0 = own_blocks(0)
    landed = _exchange("gather0a_ici", "run", local[:n_a] + own0[:n_a], (3 * n_a,), _plan_gather_ici(n_a, 0))
    w0 = dict(zip(SHARDED[:n_a], _exchange("gather0a_d2d", "run", landed[n_a:], (3 * n_a,), _plan_gather_d2d(n_a))))
    local = landed[:n_a] + local[n_a:]
    sems_r, token_r, rest = _exchange("gather0b_ici_start", "start", local[n_a:] + own0[n_a:], (3 * n_b,),
                                      _plan_gather_ici(n_b, 0))

    def fwd_ffn(l, cur, w, tok):
        sv = {"x": cur}
        n1, sv["n1t"] = _norm_fwd(cur, _vec(ffn1_norm, l) + tok)
        sv["a1"], sv["b1"], sv["hm1"], o1 = _ffn_fwd(n1, w, SHARDED[0:3], 0)
        sv["x1"], sv["h"] = _resid_norm(cur, [o1], 0.5, _vec(mix_norm, l), MM_DTYPE)
        return sv

    def fwd_first(l, sv, w, tok):
        sv["p"] = _proj("w_in", sv["h"], w["w_in"], 0, "nn")[0]
        sv["cq"], sv["ckv"], ckvt = _mla_pre(sv["p"], _vec(q_latent_norm, l) + tok, _vec(kv_latent_norm, l), cw)
        sv["q_pre"] = _proj("wq", sv["cq"], w["wq"], 0, "nn")[0]
        sv["k_pre"] = _proj("wk", sv["ckv"], w["wk"], 0, "nn")[0]
        sv["v"] = _proj("wv", sv["ckv"], w["wv"], 0, "nn", MM_DTYPE)[0]
        sv["vt"] = _mm("wv_transposed", w["wvt"], ckvt, "nn", (s // tq,),
                       pl.BlockSpec((None, hw, kvl), lambda i: (0, 0, 0)), pl.BlockSpec((kvl, tq), lambda i: (0, i)),
                       pl.BlockSpec((None, hw, tq), lambda i: (i, 0, 0)),
                       jax.ShapeDtypeStruct((s // tq, hw, tq), MM_DTYPE), 1, None)
        sv["q"], sv["k"], sv["kt"] = _qk_post(sv["q_pre"], sv["k_pre"], sv["p"], pe_block, tabs,
                                              _vec(gq_pad, l), _vec(gk_pad, l))
        return sv

    def fwd_attn(l, sv, w):
        sv["c"] = _conv_fwd(sv["p"], w["conv"], _vec(conv_b, l), _vec(conv_ln_g, l), _vec(conv_ln_b, l))
        sv["o"], sv["lse"] = _attn_fwd(sv["q"], sv["k"], sv.pop("vt"))
        return sv

    def fwd_second(l, sv, w, tok):
        mix_a = _contract("wo_a", sv["o"][None], w["wo_a"], 0, "nn")
        mix_c = _contract("wo_c", sv["c"][None], w["wo_c"], 0, "nn")
        sv["x2"], n2, sv["n2t"] = _resid_norm(sv["x1"], [mix_a, mix_c], 1.0, _vec(ffn2_norm, l) + tok, MM_DTYPE, True)
        sv["a2"], sv["b2"], sv["hm2"], o2 = _ffn_fwd(n2, w, SHARDED[8:11], 0)
        sv["x3"], out = _final_fwd(sv["x2"], o2, _vec(post_norm, l))
        return out

    sv0 = fwd_ffn(0, xs, w0, token_r[0, 0])
    rest = _exchange("gather0b_ici_wait", "wait", rest, (3 * n_b,), _plan_gather_ici(n_b, 0), sems=sems_r, after=sv0["h"])
    w0.update(zip(SHARDED[n_a:], _exchange("gather0b_d2d", "run", rest[n_b:], (3 * n_b,), _plan_gather_d2d(n_b))))
    w0 = layouts(w0)
    local = local[:n_a] + rest[:n_b]
    sems_a, token_a, in_flight = _exchange("gather1_ici_start", "start", local + own_blocks(1), sem_kj, ici(1))
    sv0 = fwd_attn(0, fwd_first(0, sv0, w0, token_a[0, 0]), w0)
    in_flight = _exchange("gather1_ici_wait", "wait", in_flight, sem_kj, ici(1), sems=sems_a, after=sv0["o"])
    sems_b, token_b, in_flight = _exchange("gather1_d2d_start", "start", in_flight[k_n:], sem_kj, d2d)
    cur = fwd_second(0, sv0, w0, token_b[0, 0])
    w1 = layouts(dict(zip(SHARDED, _exchange("gather1_d2d_wait", "wait", in_flight, sem_kj, d2d, sems=sems_b, after=cur))))
    sv1 = fwd_attn(1, fwd_first(1, fwd_ffn(1, cur, w1, 0.0), w1, 0.0), w1)
    cur = fwd_second(1, sv1, w1, 0.0)
    dy, loss_part = _loss(cur, target)

    small = {n: [None] * L for n in SMALL}

    def bwd_first(l, sv, dy, w, tok, grads):
        dx3, do2, small["post_norm"][l] = _norm_bwd(sv["x3"], _vec(post_norm, l) + tok, [dy], None, 0.5)
        dn2 = _ffn_bwd(sv["n2t"], sv["a2"], sv["b2"], sv["hm2"], do2, w, SHARDED[8:11], grads)
        dx2, dx2b, small["ffn2_norm"][l] = _norm_bwd(sv["x2"], _vec(ffn2_norm, l), dn2, dx3, 1.0)
        do = _proj("wo_a_t", dx2b, w["wo_a"], 0, "nt")[0]
        dc = _proj("wo_c_t", dx2b, w["wo_c"], 0, "nt")[0]
        g_wo_a = _wgrad("wo_a_wgrad", sv["o"], dx2b[None], WIRE_DTYPE)[0].reshape(N_HEADS, HEAD_PAD, d)[:, :V_DIM]
        g_wo_c = _wgrad("wo_c_wgrad", sv["c"], dx2b[None], WIRE_DTYPE)
        grads["w_out"] = jnp.concatenate(
            [g_wo_a.reshape(N_HEADS * V_DIM, d), g_wo_c[0]], axis=0).reshape(N_CHIPS, d // N_CHIPS, d)
        dq, dk, dv = _attn_bwd(sv["q"], sv["k"], sv["kt"], sv["v"], do, sv["lse"], _attn_delta(sv["o"], do))
        dq_pre, dk_pre, dpe, small["q_norm"][l], small["k_norm"][l] = _qk_post_bwd(
            sv["q_pre"], sv["k_pre"], sv["p"], pe_block, tabs, _vec(gq_pad, l), _vec(gk_pad, l), dq, dk)
        return dict(dx2=dx2, dc=dc, dq_pre=dq_pre, dk_pre=dk_pre, dv=dv, dpe=dpe)

    def bwd_second(l, sv, st, w, tok, grads):
        dq_pre, dk_pre, dv = st["dq_pre"], st["dk_pre"], st["dv"]
        dcq = _contract("wq_t", dq_pre[None], w["wq"], 0, "nt")
        dckv = [_contract("wk_t", dk_pre[None], w["wk"], 0, "nt"), _contract("wv_t", dv[None], w["wv"], 0, "nt")]
        per_head = lambda t, rows: t[0].reshape(rows, N_HEADS, HEAD_PAD)
        g_wq = per_head(_wgrad("wq_wgrad", sv["cq"], dq_pre[None], WIRE_DTYPE), ql)[..., :QK_DIM]
        grads["w_uq"] = g_wq.reshape(ql, N_CHIPS, -1).transpose(1, 0, 2)
        g_wkv = jnp.concatenate([per_head(_wgrad("wk_wgrad", sv["ckv"], dk_pre[None], WIRE_DTYPE), kvl)[..., :QK_NOPE],
                                 per_head(_wgrad("wv_wgrad", sv["ckv"], dv[None], WIRE_DTYPE), kvl)[..., :V_DIM]], axis=-1)
        grads["w_ukv"] = g_wkv.reshape(kvl, N_CHIPS, -1).transpose(1, 0, 2)
        dyc, small["conv_ln_g"][l], small["conv_ln_b"][l], small["conv_b"][l] = _conv_bwd_ln(
            sv["p"], w["conv"], _vec(conv_b, l) + tok, _vec(conv_ln_g, l), _vec(conv_ln_b, l), st["dc"])
        dp_conv, g_conv = _conv_bwd_taps(sv["p"], w["conv"], dyc)
        grads["conv_w"] = g_conv.reshape(CONV_K + 1, N_CHIPS, cw // N_CHIPS).transpose(1, 0, 2).astype(WIRE_DTYPE)
        dp, small["q_latent_norm"][l], small["kv_latent_norm"][l] = _dp_assemble(
            sv["p"], _vec(q_latent_norm, l), _vec(kv_latent_norm, l), dcq, dckv, dp_conv, st["dpe"], cw)
        dh = _contract("w_in_t", dp[None], w["w_in"], 0, "nt")
        g_win = _wgrad("w_in_wgrad", sv["h"], dp[None], WIRE_DTYPE)[0]
        o_q, o_pe_col = 2 * cw, 2 * cw + ql + kvl + QK_NOPE
        g_win = jnp.concatenate([g_win[:, o_q:o_q + ql + kvl], g_win[:, o_pe_col:o_pe_col + QK_ROPE], g_win[:, :o_q]], axis=1)
        grads["w_in"] = g_win.reshape(d, N_CHIPS, -1).transpose(1, 0, 2)
        dx1, do1, small["mix_norm"][l] = _norm_bwd(sv["x1"], _vec(mix_norm, l), [dh], st["dx2"], 0.5)
        dn1 = _ffn_bwd(sv["n1t"], sv["a1"], sv["b1"], sv["hm1"], do1, w, SHARDED[0:3], grads)
        dx, _, small["ffn1_norm"][l] = _norm_bwd(sv["x"], _vec(ffn1_norm, l), dn1, dx1, 1.0)
        return dx

    def halves(parts):
        return [lax.empty((N_CHIPS, t.shape[1] // 2, t.shape[2]), t.dtype) for t in parts]

    def thirds(parts):
        return [lax.empty((N_CHIPS - 1,) + t.shape[1:], t.dtype) for t in parts]

    def chip_sums(done):
        return [_add_sibling_half(place, g, t) for g, t in zip(done[:k_n], done[k_n:])]

    swap, scatter, sem_k = _plan_swap(k_n), _plan_scatter(k_n), (k_n,)
    grads1, grads0 = {}, {}
    st1 = bwd_first(1, sv1, dy, w1, 0.0, grads1)
    dy = bwd_second(1, sv1, st1, w1, 0.0, grads1)
    parts1 = [grads1[n] for n in SHARDED]
    sems_s, token_s, in_flight = _exchange("swap1_start", "start", parts1 + halves(parts1), sem_k, swap)
    st0 = bwd_first(0, sv0, dy, w0, token_s[0, 0], grads0)
    sums1 = chip_sums(_exchange("swap1_wait", "wait", in_flight, sem_k, swap, sems=sems_s, after=st0["dq_pre"]))
    sems_c, token_c, in_flight = _exchange("scatter1_start", "start", sums1 + thirds(sums1), sem_kj, scatter)
    dy = bwd_second(0, sv0, st0, w0, token_c[0, 0], grads0)
    grad_x = dy.reshape(x.shape)
    done = _exchange("scatter1_wait", "wait", in_flight, sem_kj, scatter, sems=sems_c, after=dy)
    outs = [_add_chips_half(place, own, others, L, 1, None) for own, others in zip(done[:k_n], done[k_n:])]
    parts0 = [grads0[n] for n in SHARDED]
    sums0 = chip_sums(_exchange("swap0", "run", parts0 + halves(parts0), sem_k, swap))
    done = _exchange("scatter0", "run", sums0 + thirds(sums0), sem_kj, scatter)
    outs = [_add_chips_half(place, own, others, L, 0, into) for own, others, into in zip(done[:k_n], done[k_n:], outs)]
    full = dict(zip(SHARDED, _exchange("share", "run", outs, (L * k_n,), _plan_share(k_n, L))))
    full["conv_w"] = full["conv_w"][:, :CONV_K]

    width = max(wts[n].shape[-1] for n in SMALL)
    pack = lambda vals: jnp.concatenate([_pad_lanes(v.reshape(-1, v.shape[-1]), width) for v in vals], axis=0)
    part = pack([jnp.concatenate(small[n], axis=0)[..., :wts[n].shape[-1]] for n in SMALL])
    rows = part.shape[0]
    part = jnp.pad(part, [(0, -rows % 8), (0, 0)])
    small_sum = _add_leading("add_devices", _gather_small(part), F32)
    for i, n in enumerate(SMALL):
        full[n] = small_sum[i * L:(i + 1) * L, :wts[n].shape[-1]]

    delta_w, new_m, new_v = {}, {}, {}
    for n in SHARDED:
        shape = wts[n].shape
        swap = shape[-1] % HEAD_PAD != 0 and shape[-2] % HEAD_PAD == 0
        view = (lambda t: jnp.swapaxes(t, 1, 2)) if swap else (lambda t: t)
        flat = lambda t: view(t).reshape(-1, view(t).shape[-1])
        r = flat(wts[n]).shape[0]
        fix = (lambda t: jnp.pad(flat(t), [(0, -r % 8), (0, 0)])) if r % 8 else flat
        outs = _adamw(fix(wts[n]), fix(full[n]), fix(mom[n]), fix(var[n]))
        delta_w[n], new_m[n], new_v[n] = [view(o[:r].reshape(view(wts[n]).shape)) for o in outs]
    pk = lambda src: jnp.pad(pack([src[n] for n in SMALL]), [(0, -rows % 8), (0, 0)])
    outs = _adamw(pk(wts), small_sum, pk(mom), pk(var))
    for i, n in enumerate(SMALL):
        delta_w[n], new_m[n], new_v[n] = [o[i * L:(i + 1) * L, :wts[n].shape[-1]] for o in outs]

    loss = lax.psum(loss_part[0, 0], ("x", "y", "c"))
    return (loss, grad_x, *[full[n] for n in ALL_WEIGHTS], *[delta_w[n] for n in ALL_WEIGHTS],
            *[new_m[n] for n in ALL_WEIGHTS], *[new_v[n] for n in ALL_WEIGHTS])
```

```python
import jax
import jax.numpy as jnp
from jax import lax
from jax.experimental import pallas as pl
from jax.experimental.pallas import tpu as pltpu

F32 = jnp.float32
MM_DTYPE = jnp.bfloat16
WIRE_DTYPE = jnp.bfloat16
EPS = 1e-6
N_HEADS = 8
QK_NOPE = 64
QK_ROPE = 32
QK_DIM = QK_NOPE + QK_ROPE
V_DIM = 64
HEAD_PAD = 128
HEADS_PER_STEP = 2
CHUNK_SHIFT = 6
LOG2E = 1.4426950408889634
CONV_K = 31
HALO = 32
ROPE_THETA = 10000.0
N_CHIPS = 4
ADAM_LR, ADAM_B1, ADAM_B2, ADAM_EPS, ADAM_WD, ADAM_STEP = 0.001, 0.9, 0.999, 1e-08, 0.01, 10
VMEM_LIMIT_BYTES = 56 * 2 ** 20
WGRAD_ROWS = 2048
MESH = pl.DeviceIdType.MESH
ANY = pl.BlockSpec(memory_space=pl.ANY)

_DIMS = {
    "nn": (((1,), (0,)), ((), ())),
    "nt": (((1,), (1,)), ((), ())),
    "tn": (((0,), (0,)), ((), ())),
}


def _params(n_axes):
    return pltpu.CompilerParams(dimension_semantics=("arbitrary",) * n_axes, vmem_limit_bytes=VMEM_LIMIT_BYTES)


def _tile(n, cap):
    for step in (16, 8):
        for t in range(cap - cap % step, 0, -step):
            if n % t == 0:
                return t
    return n


def _mm(name, a, b, dims, grid, a_spec, b_spec, o_spec, out_shape, nk, acc_shape, into=None):
    n_axes = len(grid)

    def body(*refs):
        if into is not None:
            refs = refs[1:]
        a_ref, b_ref, o_ref = refs[:3]
        part = lax.dot_general(a_ref[...].astype(MM_DTYPE), b_ref[...].astype(MM_DTYPE), _DIMS[dims],
                               preferred_element_type=F32)
        if nk == 1:
            o_ref[...] = part.astype(o_ref.dtype)
        else:
            acc_ref = refs[3]
            k = pl.program_id(n_axes - 1)

            @pl.when(k == 0)
            def _():
                acc_ref[...] = part

            @pl.when(k > 0)
            def _():
                acc_ref[...] += part

            @pl.when(k == nk - 1)
            def _():
                o_ref[...] = acc_ref[...].astype(o_ref.dtype)

    in_specs = [a_spec, b_spec]
    args = [a, b]
    aliases = {}
    if into is not None:
        in_specs = [ANY] + in_specs
        args = [into] + args
        aliases = {0: 0}
    return pl.pallas_call(
        body, name=name, grid=grid, in_specs=in_specs, out_specs=o_spec, out_shape=out_shape,
        scratch_shapes=[] if nk == 1 else [pltpu.VMEM(acc_shape, F32)],
        input_output_aliases=aliases, compiler_params=_params(n_axes))(*args)


def _proj(name, a, w, l, dims, out_dtype=F32):
    s, k = a.shape
    g = w.shape[1]
    n = w.shape[3] if dims == "nn" else w.shape[2]
    tm = _tile(s, 512)
    return _mm(name, a, w, dims, (g, s // tm),
               pl.BlockSpec((tm, k), lambda j, i: (i, 0)),
               pl.BlockSpec((None, None) + w.shape[2:], lambda j, i: (l, j, 0, 0)),
               pl.BlockSpec((None, tm, n), lambda j, i: (j, i, 0)),
               jax.ShapeDtypeStruct((g, s, n), out_dtype), 1, None)


def _contract(name, a, w, l, dims, out_dtype=F32):
    g, s, kb = a.shape
    n = w.shape[3] if dims == "nn" else w.shape[2]
    tm = _tile(s, 512)
    return _mm(name, a, w, dims, (s // tm, g),
               pl.BlockSpec((None, tm, kb), lambda i, j: (j, i, 0)),
               pl.BlockSpec((None, None) + w.shape[2:], lambda i, j: (l, j, 0, 0)),
               pl.BlockSpec((tm, n), lambda i, j: (i, 0)),
               jax.ShapeDtypeStruct((s, n), out_dtype), g, (tm, n))


def _wgrad(name, a, b, out_dtype, layers=None, l=None, into=None):
    g = a.shape[0] if a.ndim == 3 else b.shape[0]
    s, k = a.shape[-2:]
    n = b.shape[-1]
    tm = _tile(s, WGRAD_ROWS)

    def spec(x, w):
        if x.ndim == 3:
            return pl.BlockSpec((None, tm, w), lambda j, i: (j, i, 0))
        return pl.BlockSpec((tm, w), lambda j, i: (i, 0))

    if layers is None:
        o_spec = pl.BlockSpec((None, k, n), lambda j, i: (j, 0, 0))
        out_shape = jax.ShapeDtypeStruct((g, k, n), out_dtype)
    else:
        o_spec = pl.BlockSpec((None, None, k, n), lambda j, i: (l, j, 0, 0))
        out_shape = jax.ShapeDtypeStruct((layers, g, k, n), out_dtype)
    return _mm(name, a, b, "tn", (g, s // tm), spec(a, k), spec(b, n), o_spec, out_shape, s // tm, (k, n), into=into)


def _ffn_up_fused(n, wg, wu, l):
    s, k = n.shape
    g, n4 = wg.shape[1], wg.shape[3]
    tm = _tile(s, 512)

    def body(n_ref, wg_ref, wu_ref, a_ref, b_ref, h_ref):
        nv = n_ref[...]
        a = jnp.dot(nv, wg_ref[...], preferred_element_type=F32)
        b = jnp.dot(nv, wu_ref[...], preferred_element_type=F32)
        a_ref[...] = a.astype(a_ref.dtype)
        b_ref[...] = b.astype(b_ref.dtype)
        h_ref[...] = (a * _sigmoid(a) * b).astype(h_ref.dtype)

    w_spec = pl.BlockSpec((None, None, k, n4), lambda j, i: (l, j, 0, 0))
    o_spec = pl.BlockSpec((None, tm, n4), lambda j, i: (j, i, 0))
    return pl.pallas_call(
        body, name="ffn_up_fused", grid=(g, s // tm),
        in_specs=[pl.BlockSpec((tm, k), lambda j, i: (i, 0)), w_spec, w_spec], out_specs=[o_spec] * 3,
        out_shape=[jax.ShapeDtypeStruct((g, s, n4), MM_DTYPE)] * 3, compiler_params=_params(2))(n, wg, wu)


def _ffn_down_t_fused(do, wd, a, b, l):
    s, k = do.shape
    g, n4 = wd.shape[1], wd.shape[2]
    tm = _tile(s, 512)

    def body(do_ref, wd_ref, a_ref, b_ref, da_ref, db_ref):
        dh = lax.dot_general(do_ref[...], wd_ref[...], _DIMS["nt"], preferred_element_type=F32)
        av = a_ref[...].astype(F32)
        sg = _sigmoid(av)
        da_ref[...] = (dh * b_ref[...].astype(F32) * (sg * (1.0 + av * (1.0 - sg)))).astype(da_ref.dtype)
        db_ref[...] = (dh * (av * sg)).astype(db_ref.dtype)

    t_spec = pl.BlockSpec((None, tm, n4), lambda j, i: (j, i, 0))
    return pl.pallas_call(
        body, name="ffn_down_t_fused", grid=(g, s // tm),
        in_specs=[pl.BlockSpec((tm, k), lambda j, i: (i, 0)),
                  pl.BlockSpec((None, None, n4, k), lambda j, i: (l, j, 0, 0)), t_spec, t_spec],
        out_specs=[t_spec] * 2, out_shape=[jax.ShapeDtypeStruct((g, s, n4), MM_DTYPE)] * 2,
        compiler_params=_params(2))(do, wd, a, b)


def _blocked_sum(name, acts, weights, l, dims):
    g, s, kb = acts[0].shape
    n = weights[0].shape[3] if dims == "nn" else weights[0].shape[2]
    tm = _tile(s, 256)
    n_pairs = len(acts)

    def body(*refs):
        o_ref = refs[2 * n_pairs]
        acc = None
        for p in range(n_pairs):
            for j in range(g):
                t = lax.dot_general(refs[p][j], refs[n_pairs + p][j], _DIMS[dims], preferred_element_type=F32)
                acc = t if acc is None else acc + t
        o_ref[...] = acc

    return pl.pallas_call(
        body, name=name, grid=(s // tm,),
        in_specs=[pl.BlockSpec((g, tm, kb), lambda i: (0, i, 0))] * n_pairs
        + [pl.BlockSpec((None,) + w.shape[1:], lambda i: (l, 0, 0, 0)) for w in weights],
        out_specs=pl.BlockSpec((tm, n), lambda i: (i, 0)), out_shape=jax.ShapeDtypeStruct((s, n), F32),
        compiler_params=_params(1))(*acts, *weights)


def _wgrad_pair(name, nt, da, db, layers, l, into):
    k, s = nt.shape
    g, _, n4 = da.shape
    tm = _tile(s, WGRAD_ROWS)
    nk = s // tm

    def body(*refs):
        if into is not None:
            refs = refs[2:]
        nt_ref, da_ref, db_ref, og_ref, ou_ref, accg, accu = refs
        i = pl.program_id(1)
        ntv = nt_ref[...]
        pg = jnp.dot(ntv, da_ref[...], preferred_element_type=F32)
        pu = jnp.dot(ntv, db_ref[...], preferred_element_type=F32)

        @pl.when(i == 0)
        def _():
            accg[...] = pg
            accu[...] = pu

        @pl.when(i > 0)
        def _():
            accg[...] += pg
            accu[...] += pu

        @pl.when(i == nk - 1)
        def _():
            og_ref[...] = accg[...].astype(og_ref.dtype)
            ou_ref[...] = accu[...].astype(ou_ref.dtype)

    t_spec = pl.BlockSpec((None, tm, n4), lambda j, i: (j, i, 0))
    o_spec = pl.BlockSpec((None, None, k, n4), lambda j, i: (l, j, 0, 0))
    in_specs = [pl.BlockSpec((k, tm), lambda j, i: (0, i)), t_spec, t_spec]
    args = [nt, da, db]
    aliases = {}
    if into is not None:
        in_specs = [ANY, ANY] + in_specs
        args = list(into) + args
        aliases = {0: 0, 1: 1}
    return pl.pallas_call(
        body, name=name, grid=(g, nk), in_specs=in_specs, out_specs=[o_spec] * 2,
        out_shape=[jax.ShapeDtypeStruct((layers, g, k, n4), WIRE_DTYPE)] * 2,
        scratch_shapes=[pltpu.VMEM((k, n4), F32)] * 2, input_output_aliases=aliases,
        compiler_params=_params(2))(*args)


def _rows(name, fn, s, tm, ins, outs, scratch=()):
    n = s // tm

    def spec(shape, kind):
        nd = len(shape)
        if kind == "r":
            return pl.BlockSpec(shape[:-2] + (tm, shape[-1]), lambda i: (0,) * (nd - 2) + (i, 0))
        if kind in ("f", "a"):
            return pl.BlockSpec(shape, lambda i: (0,) * nd)
        if kind == "t":
            return pl.BlockSpec((shape[0], tm), lambda i: (0, i))
        if kind == "b":
            return pl.BlockSpec((None,) + shape[1:], lambda i: (i, 0, 0))
        if kind == "p":
            return pl.BlockSpec((HALO, shape[-1]), lambda i: (jnp.maximum(i * (tm // HALO) - 1, 0), 0))
        if kind == "n":
            return pl.BlockSpec((HALO, shape[-1]), lambda i: (jnp.minimum((i + 1) * (tm // HALO), s // HALO - 1), 0))
        j = kind[1]
        return pl.BlockSpec((tm, HEAD_PAD), lambda i: (i, j))

    def body(*refs):
        fn(pl.program_id(0), n, *refs)

    return pl.pallas_call(
        body, name=name, grid=(n,),
        in_specs=[spec(a.shape, kind) for a, kind in ins],
        out_specs=[spec(shape, kind) for shape, _, kind in outs],
        out_shape=[jax.ShapeDtypeStruct(shape, dtype) for shape, dtype, _ in outs],
        scratch_shapes=list(scratch), compiler_params=_params(1))(*[a for a, _ in ins])


def _acc(ref, i, val):
    @pl.when(i == 0)
    def _():
        ref[...] = val

    @pl.when(i > 0)
    def _():
        ref[...] += val


def _sum0(x):
    return jnp.sum(x, axis=0, keepdims=True)


def _rstd(x, n):
    return lax.rsqrt(jnp.sum(x * x, axis=-1, keepdims=True) * (1.0 / n) + EPS)


def _rms_bwd(x, g, dy, n):
    r = _rstd(x, n)
    xh = x * r
    dyg = dy * g
    dx = r * (dyg - xh * (jnp.sum(dyg * xh, axis=-1, keepdims=True) * (1.0 / n)))
    return dx, _sum0(dy * xh)


def _sigmoid(x):
    return 1.0 / (1.0 + jnp.exp(-x))


def _norm_fwd(x, g):
    s, d = x.shape

    def fn(i, n, x_ref, g_ref, o_ref, ot_ref):
        xv = x_ref[...]
        y = xv * _rstd(xv, d) * g_ref[...]
        o_ref[...] = y.astype(o_ref.dtype)
        ot_ref[...] = y.T.astype(ot_ref.dtype)

    return _rows("norm_fwd", fn, s, _tile(s, 256), [(x, "r"), (g, "f")],
                 [((s, d), MM_DTYPE, "r"), ((d, s), MM_DTYPE, "t")])


def _resid_norm(x, adds, scale, g, norm_dtype, transposed=False):
    s, d = x.shape
    k = len(adds)

    def fn(i, n, *refs):
        x_ref, add_refs, g_ref, xo_ref, no_ref = refs[0], refs[1:1 + k], refs[1 + k], refs[2 + k], refs[3 + k]
        tot = add_refs[0][...]
        for r in add_refs[1:]:
            tot = tot + r[...]
        xv = x_ref[...] + scale * tot
        xo_ref[...] = xv
        y = xv * _rstd(xv, d) * g_ref[...]
        no_ref[...] = y.astype(no_ref.dtype)
        if transposed:
            refs[4 + k][...] = y.T.astype(refs[4 + k].dtype)

    return _rows("resid_norm", fn, s, _tile(s, 256), [(x, "r")] + [(a, "r") for a in adds] + [(g, "f")],
                 [((s, d), F32, "r"), ((s, d), norm_dtype, "r")] + ([((d, s), MM_DTYPE, "t")] if transposed else []))


def _norm_bwd(x, g, dn_parts, dres, out_scale):
    s, d = x.shape
    k = len(dn_parts)
    has_res = dres is not None

    def fn(i, n, *refs):
        x_ref, g_ref = refs[0], refs[1]
        dn_refs = refs[2:2 + k]
        pos = 2 + k
        dn = dn_refs[0][...]
        for r in dn_refs[1:]:
            dn = dn + r[...]
        dx, dg = _rms_bwd(x_ref[...], g_ref[...], dn, d)
        if has_res:
            dx = dx + refs[pos][...]
            pos += 1
        dx_ref, dxs_ref, dg_ref = refs[pos:pos + 3]
        dx_ref[...] = dx
        dxs_ref[...] = (out_scale * dx).astype(dxs_ref.dtype)
        _acc(dg_ref, i, dg)

    ins = [(x, "r"), (g, "f")] + [(p, "r") for p in dn_parts] + ([(dres, "r")] if has_res else [])
    return _rows("norm_bwd", fn, s, _tile(s, 256), ins,
                 [((s, d), F32, "r"), ((s, d), MM_DTYPE, "r"), ((1, d), F32, "a")])


def _final_fwd(x, o, g):
    return _resid_norm(x, [o], 0.5, g, F32)


def _loss(y, t):
    s, d = y.shape

    def fn(i, n, y_ref, t_ref, dy_ref, l_ref):
        e = y_ref[...] - t_ref[...]
        dy_ref[...] = e * (1.0 / d)
        _acc(l_ref, i, (0.5 / d) * jnp.sum(jnp.sum(e * e, axis=-1, keepdims=True), axis=0, keepdims=True))

    return _rows("loss", fn, s, _tile(s, 256), [(y, "r"), (t, "r")], [((s, d), F32, "r"), ((1, 1), F32, "a")])


def _mla_pre(p, gql, gkvl, cw):
    s = p.shape[0]
    ql, kvl = gql.shape[-1], gkvl.shape[-1]
    o_q, o_kv = 2 * cw, 2 * cw + ql

    def fn(i, n, p_ref, gq_ref, gkv_ref, cq_ref, ckv_ref, ckvt_ref):
        cq = p_ref[:, o_q:o_q + ql]
        cq_ref[...] = (cq * _rstd(cq, ql) * gq_ref[...]).astype(cq_ref.dtype)
        ckv = p_ref[:, o_kv:o_kv + kvl]
        y = ckv * _rstd(ckv, kvl) * gkv_ref[...]
        ckv_ref[...] = y.astype(ckv_ref.dtype)
        ckvt_ref[...] = y.T.astype(ckvt_ref.dtype)

    return _rows("mla_pre", fn, s, _tile(s, 256), [(p, "r"), (gql, "f"), (gkvl, "f")],
                 [((s, ql), MM_DTYPE, "r"), ((s, kvl), MM_DTYPE, "r"), ((kvl, s), MM_DTYPE, "t")])


def _rope(x, cosf, s1, s2):
    return x * cosf + pltpu.roll(x, HEAD_PAD - 16, 1) * s1 + pltpu.roll(x, 16, 1) * s2


def _rope_bwd(d, cosf, s1, s2):
    return d * cosf + pltpu.roll(d * s1, 16, 1) + pltpu.roll(d * s2, HEAD_PAD - 16, 1)


def _qk_post(q_pre, k_pre, p, pe_block, tabs, gq, gk):
    s, hw = q_pre.shape
    cosf, s1, s2 = tabs

    tm = _tile(s, 256)

    def fn(i, n, q_ref, k_ref, pe_ref, c_ref, s1_ref, s2_ref, gq_ref, gk_ref, qo_ref, ko_ref, kt_ref):
        c, a1, a2, pe = c_ref[...], s1_ref[...], s2_ref[...], pe_ref[...]
        for hh in range(N_HEADS):
            cols = slice(hh * HEAD_PAD, (hh + 1) * HEAD_PAD)
            qv = q_ref[:, cols]
            qo_ref[:, cols] = _rope(qv * _rstd(qv, QK_DIM) * gq_ref[...], c, a1, a2).astype(qo_ref.dtype)
            kv = k_ref[:, cols] + pe
            kout = _rope(kv * _rstd(kv, QK_DIM) * gk_ref[...], c, a1, a2)
            ko_ref[:, cols] = kout.astype(ko_ref.dtype)
            kt_ref[cols, :] = kout.T.astype(kt_ref.dtype)

    return _rows("qk_post", fn, s, tm,
                 [(q_pre, "r"), (k_pre, "r"), (p, ("c", pe_block)), (cosf, "r"), (s1, "r"), (s2, "r"), (gq, "f"), (gk, "f")],
                 [((s, hw), MM_DTYPE, "r"), ((s, hw), MM_DTYPE, "r"), ((s // tm, hw, tm), MM_DTYPE, "b")])


def _qk_post_bwd(q_pre, k_pre, p, pe_block, tabs, gq, gk, dq, dk):
    s, hw = q_pre.shape
    cosf, s1, s2 = tabs

    def fn(i, n, q_ref, k_ref, pe_ref, c_ref, s1_ref, s2_ref, gq_ref, gk_ref, dq_ref, dk_ref,
           dqo_ref, dko_ref, dpe_ref, dgq_ref, dgk_ref):
        c, a1, a2, pe = c_ref[...], s1_ref[...], s2_ref[...], pe_ref[...]
        lane = lax.broadcasted_iota(jnp.int32, pe.shape, 1)
        is_pe = (lane >= QK_NOPE) & (lane < QK_DIM)
        dpe = jnp.zeros(pe.shape, F32)
        dgq = jnp.zeros((1, HEAD_PAD), F32)
        dgk = jnp.zeros((1, HEAD_PAD), F32)
        for hh in range(N_HEADS):
            cols = slice(hh * HEAD_PAD, (hh + 1) * HEAD_PAD)
            dxq, g1 = _rms_bwd(q_ref[:, cols], gq_ref[...], _rope_bwd(dq_ref[cols, :].T, c, a1, a2), QK_DIM)
            dqo_ref[:, cols] = dxq.astype(dqo_ref.dtype)
            dxk, g2 = _rms_bwd(k_ref[:, cols] + pe, gk_ref[...], _rope_bwd(dk_ref[:, cols], c, a1, a2), QK_DIM)
            dko_ref[:, cols] = dxk.astype(dko_ref.dtype)
            dpe = dpe + dxk
            dgq = dgq + g1
            dgk = dgk + g2
        dpe_ref[...] = jnp.where(is_pe, dpe, 0.0)
        _acc(dgq_ref, i, dgq)
        _acc(dgk_ref, i, dgk)

    return _rows("qk_post_bwd", fn, s, _tile(s, 256),
                 [(q_pre, "r"), (k_pre, "r"), (p, ("c", pe_block)), (cosf, "r"), (s1, "r"), (s2, "r"), (gq, "f"), (gk, "f"),
                  (dq, "b"), (dk, "r")],
                 [((s, hw), MM_DTYPE, "r"), ((s, hw), MM_DTYPE, "r"), ((s, HEAD_PAD), F32, "r"),
                  ((1, HEAD_PAD), F32, "a"), ((1, HEAD_PAD), F32, "a")])


def _chunk_mask(tq, key_rows):
    r = jnp.right_shift(lax.broadcasted_iota(jnp.int32, (tq, tq), 0), CHUNK_SHIFT)
    c = jnp.right_shift(lax.broadcasted_iota(jnp.int32, (tq, tq), 1), CHUNK_SHIFT)
    return (r <= c) if key_rows else (c <= r)


def _attn_specs(s, tq):
    w = HEADS_PER_STEP * HEAD_PAD
    tile = pl.BlockSpec((tq, w), lambda hb, i: (i, hb))
    whole = pl.BlockSpec((s, w), lambda hb, i: (0, hb))
    rows = pl.BlockSpec((None, None, HEADS_PER_STEP, tq), lambda hb, i: (hb, i, 0, 0))
    blocks = pl.BlockSpec((s // tq, w, tq), lambda hb, i: (0, hb, 0))
    return tile, whole, rows, blocks


def _head(hh):
    return slice(hh * HEAD_PAD, (hh + 1) * HEAD_PAD)


def _attn_fwd(q, k, vt):
    s, hw = q.shape
    tq = _tile(s, 256)
    nq = s // tq
    groups = N_HEADS // HEADS_PER_STEP
    scale = QK_DIM ** -0.5
    tile, whole, rows, blocks = _attn_specs(s, tq)

    def body(q_ref, k_ref, vt_ref, o_ref, lse_ref):
        qi = pl.program_id(1)
        qv = [q_ref[:, _head(hh)] for hh in range(HEADS_PER_STEP)]

        def scores(kb):
            off = pl.multiple_of(kb * tq, tq)
            return tuple(lax.dot_general(k_ref[pl.ds(off, tq), _head(hh)], qv[hh], _DIMS["nt"],
                                         preferred_element_type=F32) for hh in range(HEADS_PER_STEP))

        def update(kb, stats, raw, diag):
            heads = range(HEADS_PER_STEP)
            mn, al, pr, ln = [], [], [], []
            for hh in heads:
                m, l, _ = stats[hh]
                sc = raw[hh] * (scale * LOG2E)
                if diag:
                    sc = jnp.where(_chunk_mask(tq, True), sc, -1e30)
                mn.append(jnp.maximum(m, jnp.max(sc, axis=0, keepdims=True)))
                al.append(jnp.exp2(m - mn[hh]))
                p = jnp.exp2(sc - mn[hh])
                ln.append(al[hh] * l + jnp.sum(p, axis=0, keepdims=True))
                pr.append(p.astype(MM_DTYPE))
            return tuple((mn[hh], ln[hh], al[hh] * stats[hh][2]
                          + jnp.dot(vt_ref[kb, _head(hh), :], pr[hh], preferred_element_type=F32)) for hh in heads)

        def step(kb, carry):
            stats, raw = carry
            nxt = scores(kb + 1)
            return update(kb, stats, raw, False), nxt

        init = tuple((jnp.full((1, tq), -1e30, F32), jnp.zeros((1, tq), F32), jnp.zeros((HEAD_PAD, tq), F32))
                     for _ in range(HEADS_PER_STEP))
        stats, raw = lax.fori_loop(0, qi, step, (init, scores(0)))
        carry = update(qi, stats, raw, True)
        for hh in range(HEADS_PER_STEP):
            m, l, acc = carry[hh]
            o_ref[:, _head(hh)] = (acc / l).T
            lse_ref[hh:hh + 1, :] = m + jnp.log(l) * LOG2E

    return pl.pallas_call(
        body, name="attn_fwd", grid=(groups, nq),
        in_specs=[tile, whole, blocks], out_specs=[tile, rows],
        out_shape=[jax.ShapeDtypeStruct((s, hw), F32), jax.ShapeDtypeStruct((groups, nq, HEADS_PER_STEP, tq), F32)],
        compiler_params=_params(2))(q, k, vt)


def _attn_delta(o, do):
    s, hw = o.shape
    tq = _tile(s, 256)
    groups = N_HEADS // HEADS_PER_STEP
    tile, _, rows, _ = _attn_specs(s, tq)

    def body(o_ref, do_ref, d_ref):
        for hh in range(HEADS_PER_STEP):
            prod = o_ref[:, _head(hh)] * do_ref[:, _head(hh)]
            d_ref[hh:hh + 1, :] = jnp.sum(prod.T, axis=0, keepdims=True)

    return pl.pallas_call(
        body, name="attn_delta", grid=(groups, s // tq), in_specs=[tile, tile], out_specs=rows,
        out_shape=jax.ShapeDtypeStruct((groups, s // tq, HEADS_PER_STEP, tq), F32),
        compiler_params=_params(2))(o, do)


def _attn_bwd(q, k, kt, v, do, lse_row, delta_row):
    s, hw = q.shape
    tq = _tile(s, 256)
    nq = s // tq
    groups = N_HEADS // HEADS_PER_STEP
    w = HEADS_PER_STEP * HEAD_PAD
    scale = QK_DIM ** -0.5
    tile, whole, _, blocks = _attn_specs(s, tq)
    rowvec = pl.BlockSpec((None, nq, HEADS_PER_STEP, tq), lambda hb, i: (hb, 0, 0, 0))
    kt_tile = pl.BlockSpec((None, w, tq), lambda hb, i: (i, hb, 0))

    def body(q_ref, k_ref, kt_ref, v_ref, do_ref, lse_ref, delta_ref, dqt_ref, dk_ref, dv_ref):
        ki = pl.program_id(1)
        kv = [k_ref[:, _head(hh)] for hh in range(HEADS_PER_STEP)]
        vv = [v_ref[:, _head(hh)] for hh in range(HEADS_PER_STEP)]
        ktv = [kt_ref[_head(hh), :] for hh in range(HEADS_PER_STEP)]

        @pl.when(ki == 0)
        def _():
            dqt_ref[...] = jnp.zeros(dqt_ref.shape, F32)

        def block(qb, carry, diag):
            off = pl.multiple_of(qb * tq, tq)
            heads = range(HEADS_PER_STEP)
            qs = [q_ref[pl.ds(off, tq), _head(hh)] for hh in heads]
            dos = [do_ref[pl.ds(off, tq), _head(hh)].astype(MM_DTYPE) for hh in heads]
            raw = [lax.dot_general(kv[hh], qs[hh], _DIMS["nt"], preferred_element_type=F32) for hh in heads]
            dpr = [lax.dot_general(vv[hh], dos[hh], _DIMS["nt"], preferred_element_type=F32) for hh in heads]
            pr, ds = [], []
            for hh in heads:
                sc = raw[hh] * (scale * LOG2E)
                if diag:
                    sc = jnp.where(_chunk_mask(tq, True), sc, -1e30)
                p = jnp.exp2(sc - lse_ref[qb, hh:hh + 1, :])
                pr.append(p.astype(MM_DTYPE))
                ds.append((p * (dpr[hh] - delta_ref[qb, hh:hh + 1, :]) * scale).astype(MM_DTYPE))
            out = []
            for hh in heads:
                dk, dvv = carry[hh]
                dvv = dvv + jnp.dot(pr[hh], dos[hh], preferred_element_type=F32)
                dk = dk + jnp.dot(ds[hh], qs[hh], preferred_element_type=F32)
                dqt_ref[qb, _head(hh), :] += jnp.dot(ktv[hh], ds[hh], preferred_element_type=F32)
                out.append((dk, dvv))
            return tuple(out)

        init = tuple((jnp.zeros((tq, HEAD_PAD), F32), jnp.zeros((tq, HEAD_PAD), F32)) for _ in range(HEADS_PER_STEP))
        carry = lax.fori_loop(ki + 1, nq, lambda qb, c: block(qb, c, False), block(ki, init, True))
        for hh in range(HEADS_PER_STEP):
            dk_ref[:, _head(hh)] = carry[hh][0]
            dv_ref[:, _head(hh)] = carry[hh][1].astype(dv_ref.dtype)

    return pl.pallas_call(
        body, name="attn_bwd", grid=(groups, nq),
        in_specs=[whole, tile, kt_tile, tile, whole, rowvec, rowvec], out_specs=[blocks, tile, tile],
        out_shape=[jax.ShapeDtypeStruct((nq, hw, tq), F32), jax.ShapeDtypeStruct((s, hw), F32),
                   jax.ShapeDtypeStruct((s, hw), MM_DTYPE)],
        compiler_params=_params(2))(q, k, kt, v, do, lse_row, delta_row)


def _glu_with_halo(i, p_ref, ph_ref, ubuf, cw, tm):
    a, g = p_ref[:, 0:cw], p_ref[:, cw:2 * cw]
    ubuf[HALO:HALO + tm, :] = a * _sigmoid(g)
    ah, gh = ph_ref[:, 0:cw], ph_ref[:, cw:2 * cw]
    ubuf[0:HALO, :] = jnp.where(i > 0, ah * _sigmoid(gh), 0.0)
    return a, g


def _conv_taps(ubuf, w_ref, tm):
    base = HALO - (CONV_K - 1)
    y = w_ref[0:1, :] * ubuf[base:base + tm, :]
    for k in range(1, CONV_K):
        y = y + w_ref[k:k + 1, :] * ubuf[base + k:base + k + tm, :]
    return y


def _layer_norm_stats(y, cw):
    mu = jnp.sum(y, axis=-1, keepdims=True) * (1.0 / cw)
    yc = y - mu
    r = lax.rsqrt(jnp.sum(yc * yc, axis=-1, keepdims=True) * (1.0 / cw) + EPS)
    return yc * r, r


def _conv_fwd(p, w, cb, lg, lb):
    s = p.shape[0]
    cw = cb.shape[-1]
    tm = _tile(s, 256)

    def fn(i, n, p_ref, ph_ref, w_ref, cb_ref, lg_ref, lb_ref, o_ref, ubuf):
        _glu_with_halo(i, p_ref, ph_ref, ubuf, cw, tm)
        yh, _ = _layer_norm_stats(_conv_taps(ubuf, w_ref, tm) + cb_ref[...], cw)
        z = yh * lg_ref[...] + lb_ref[...]
        o_ref[...] = (z * _sigmoid(z)).astype(o_ref.dtype)

    return _rows("conv_fwd", fn, s, tm, [(p, "r"), (p, "p"), (w, "f"), (cb, "f"), (lg, "f"), (lb, "f")],
                 [((s, cw), MM_DTYPE, "r")], scratch=[pltpu.VMEM((tm + HALO, cw), F32)])[0]


def _conv_bwd_ln(p, w, cb, lg, lb, dc):
    s = p.shape[0]
    cw = cb.shape[-1]
    tm = _tile(s, 256)

    def fn(i, n, p_ref, ph_ref, w_ref, cb_ref, lg_ref, lb_ref, dc_ref, dy_ref, dlg_ref, dlb_ref, dcb_ref, ubuf):
        _glu_with_halo(i, p_ref, ph_ref, ubuf, cw, tm)
        yh, r = _layer_norm_stats(_conv_taps(ubuf, w_ref, tm) + cb_ref[...], cw)
        z = yh * lg_ref[...] + lb_ref[...]
        sg = _sigmoid(z)
        dz = dc_ref[...] * (sg * (1.0 + z * (1.0 - sg)))
        dyh = dz * lg_ref[...]
        m1 = jnp.sum(dyh, axis=-1, keepdims=True) * (1.0 / cw)
        m2 = jnp.sum(dyh * yh, axis=-1, keepdims=True) * (1.0 / cw)
        dy = r * (dyh - m1 - yh * m2)
        dy_ref[...] = dy
        _acc(dlg_ref, i, _sum0(dz * yh))
        _acc(dlb_ref, i, _sum0(dz))
        _acc(dcb_ref, i, _sum0(dy))

    return _rows("conv_bwd_ln", fn, s, tm,
                 [(p, "r"), (p, "p"), (w, "f"), (cb, "f"), (lg, "f"), (lb, "f"), (dc, "r")],
                 [((s, cw), F32, "r"), ((1, cw), F32, "a"), ((1, cw), F32, "a"), ((1, cw), F32, "a")],
                 scratch=[pltpu.VMEM((tm + HALO, cw), F32)])


def _conv_bwd_taps(p, w, dy):
    s = p.shape[0]
    cw = w.shape[-1]
    tm = _tile(s, 256)

    def fn(i, n, p_ref, ph_ref, w_ref, dy_ref, dyn_ref, dp_ref, dw_ref, ubuf, dybuf):
        a, g = _glu_with_halo(i, p_ref, ph_ref, ubuf, cw, tm)
        dyv = dy_ref[...]
        dybuf[0:tm, :] = dyv
        dybuf[tm:tm + HALO, :] = jnp.where(i < n - 1, dyn_ref[...], 0.0)
        base = HALO - (CONV_K - 1)
        du = jnp.zeros((tm, cw), F32)

        @pl.when(i == 0)
        def _():
            dw_ref[...] = jnp.zeros(dw_ref.shape, F32)

        for k in range(CONV_K):
            sh = CONV_K - 1 - k
            du = du + w_ref[k:k + 1, :] * dybuf[sh:sh + tm, :]
            dw_ref[k:k + 1, :] += _sum0(dyv * ubuf[base + k:base + k + tm, :])
        sg = _sigmoid(g)
        dp_ref[:, 0:cw] = (du * sg).astype(dp_ref.dtype)
        dp_ref[:, cw:2 * cw] = (du * a * sg * (1.0 - sg)).astype(dp_ref.dtype)

    return _rows("conv_bwd_taps", fn, s, tm, [(p, "r"), (p, "p"), (w, "f"), (dy, "r"), (dy, "n")],
                 [((s, 2 * cw), MM_DTYPE, "r"), ((CONV_K + 1, cw), F32, "a")],
                 scratch=[pltpu.VMEM((tm + HALO, cw), F32), pltpu.VMEM((tm + HALO, cw), F32)])


def _dp_assemble(p, gql, gkvl, dcq, dckv_parts, dp_conv, dpe, cw):
    s, width = p.shape
    ql, kvl = gql.shape[-1], gkvl.shape[-1]
    o_q, o_kv, o_pe = 2 * cw, 2 * cw + ql, 2 * cw + ql + kvl

    def fn(i, n, p_ref, gq_ref, gkv_ref, dcq_ref, dk1_ref, dk2_ref, dconv_ref, dpe_ref, dp_ref, dgq_ref, dgkv_ref):
        dp_ref[:, 0:o_q] = dconv_ref[...]
        dx, dg = _rms_bwd(p_ref[:, o_q:o_kv], gq_ref[...], dcq_ref[...], ql)
        dp_ref[:, o_q:o_kv] = dx.astype(dp_ref.dtype)
        _acc(dgq_ref, i, dg)
        dx, dg = _rms_bwd(p_ref[:, o_kv:o_pe], gkv_ref[...], dk1_ref[...] + dk2_ref[...], kvl)
        dp_ref[:, o_kv:o_pe] = dx.astype(dp_ref.dtype)
        _acc(dgkv_ref, i, dg)
        dp_ref[:, o_pe:width] = dpe_ref[...].astype(dp_ref.dtype)

    return _rows("dp_assemble", fn, s, _tile(s, 256),
                 [(p, "r"), (gql, "f"), (gkvl, "f"), (dcq, "r"), (dckv_parts[0], "r"), (dckv_parts[1], "r"),
                  (dp_conv, "r"), (dpe, "r")],
                 [((s, width), MM_DTYPE, "r"), ((1, ql), F32, "a"), ((1, kvl), F32, "a")])


def _add_leading(name, x, out_dtype):
    n, r, c = x.shape
    tm = _tile(r, 256)

    def fn(i, nt, x_ref, o_ref):
        tot = x_ref[0].astype(F32)
        for k in range(1, n):
            tot = tot + x_ref[k].astype(F32)
        o_ref[...] = tot.astype(o_ref.dtype)

    return _rows(name, fn, r, tm, [(x, "r")], [((r, c), out_dtype, "r")])[0]


def _prefetch_call(name, fn, idx, grid, in_specs, out_spec, out_shape, args, aliases=None):
    return pl.pallas_call(
        fn, name=name, out_shape=out_shape,
        grid_spec=pltpu.PrefetchScalarGridSpec(num_scalar_prefetch=1, grid=grid, in_specs=in_specs, out_specs=out_spec),
        input_output_aliases=aliases or {}, compiler_params=_params(len(grid)))(idx, *args)


def _adamw(w, g, m, v):
    r, c = w.shape
    tm = _tile(r, 256)

    def fn(i, n, w_ref, g_ref, m_ref, v_ref, d_ref, mo_ref, vo_ref):
        gv = g_ref[...]
        mn = ADAM_B1 * m_ref[...] + (1.0 - ADAM_B1) * gv
        vn = ADAM_B2 * v_ref[...] + (1.0 - ADAM_B2) * (gv * gv)
        mo_ref[...] = mn
        vo_ref[...] = vn
        m_hat = mn / (1.0 - ADAM_B1 ** ADAM_STEP)
        v_hat = vn / (1.0 - ADAM_B2 ** ADAM_STEP)
        d_ref[...] = -ADAM_LR * (m_hat / (jnp.sqrt(v_hat) + ADAM_EPS) + ADAM_WD * w_ref[...])

    return _rows("adamw", fn, r, tm, [(w, "r"), (g, "r"), (m, "r"), (v, "r")], [((r, c), F32, "r")] * 3)


def _place():
    x, y, c = lax.axis_index("x"), lax.axis_index("y"), lax.axis_index("c")
    other_chips = [(1 - x, y), (x, 1 - y), (1 - x, 1 - y)]
    return x, y, c, 2 * x + y, other_chips


def _gather_small(vec):
    flips = [(fx, fy, fc) for fx in (0, 1) for fy in (0, 1) for fc in (0, 1)][1:]

    def body(src, dst, send, recv, lsem):
        x, y, c, _, _ = _place()
        me = 4 * x + 2 * y + c
        local = pltpu.make_async_copy(src, dst.at[me], lsem)
        local.start()
        away = []
        for j, (fx, fy, fc) in enumerate(flips):
            px, py, pc = x ^ fx, y ^ fy, c ^ fc
            away.append(pltpu.make_async_remote_copy(
                src_ref=src, dst_ref=dst.at[me], send_sem=send.at[j], recv_sem=recv.at[j],
                device_id=(px, py, pc), device_id_type=MESH))
        for cp in away:
            cp.start()
        for j, (fx, fy, fc) in enumerate(flips):
            blk = dst.at[4 * (x ^ fx) + 2 * (y ^ fy) + (c ^ fc)]
            pltpu.make_async_remote_copy(
                src_ref=blk, dst_ref=blk, send_sem=send.at[j], recv_sem=recv.at[j],
                device_id=(x ^ fx, y ^ fy, c ^ fc), device_id_type=MESH).wait_recv()
        for cp in away:
            cp.wait_send()
        local.wait()

    return pl.pallas_call(
        body, name="gather_small", in_specs=[ANY], out_specs=ANY,
        out_shape=jax.ShapeDtypeStruct((8,) + vec.shape, vec.dtype),
        scratch_shapes=[pltpu.SemaphoreType.DMA((7,)), pltpu.SemaphoreType.DMA((7,)), pltpu.SemaphoreType.DMA(())],
        compiler_params=pltpu.CompilerParams(has_side_effects=True))(vec)


HBM = pl.BlockSpec(memory_space=pltpu.HBM)
SEM = pl.BlockSpec(memory_space=pltpu.SEMAPHORE)
EFFECT = pltpu.SideEffectType.DATAFLOW_SIDE_EFFECTING


def _exchange(name, mode, arrays, sem_shape, plan, sems=None, after=None):
    n = len(arrays)
    hbm = [pltpu.HBM(a.shape, a.dtype) for a in arrays]
    arrays = [pltpu.with_memory_space_constraint(a, pltpu.HBM) for a in arrays]
    if mode == "run":
        def body(*refs):
            outs, send, recv = refs[n:2 * n], refs[2 * n], refs[2 * n + 1]
            sends, recvs = plan(outs, send, recv)
            for cp in sends:
                cp.start()
            for cp in recvs:
                cp.wait_recv()
            for cp in sends:
                cp.wait_send()

        return list(pl.pallas_call(
            body, name=name, in_specs=[HBM] * n, out_specs=[HBM] * n, out_shape=hbm,
            input_output_aliases={i: i for i in range(n)},
            scratch_shapes=[pltpu.SemaphoreType.DMA(sem_shape), pltpu.SemaphoreType.DMA(sem_shape)],
            compiler_params=pltpu.CompilerParams(has_side_effects=EFFECT))(*arrays))
    if mode == "start":
        def body(*refs):
            send, recv, token, outs = refs[n], refs[n + 1], refs[n + 2], refs[n + 3:]
            sends, _ = plan(outs, send, recv)
            for cp in sends:
                cp.start()
            token[...] = jnp.zeros(token.shape, F32)

        res = pl.pallas_call(
            body, name=name, in_specs=[HBM] * n,
            out_specs=[SEM, SEM, pl.BlockSpec(memory_space=pltpu.VMEM)] + [HBM] * n,
            out_shape=[pltpu.SemaphoreType.DMA(sem_shape), pltpu.SemaphoreType.DMA(sem_shape),
                       jax.ShapeDtypeStruct((8, HEAD_PAD), F32)] + hbm,
            input_output_aliases={i: i + 3 for i in range(n)},
            compiler_params=pltpu.CompilerParams(has_side_effects=EFFECT))(*arrays)
        return (res[0], res[1]), res[2], list(res[3:])

    def body(*refs):
        ins, send, recv = refs[:n], refs[n], refs[n + 1]
        sends, recvs = plan(ins, send, recv)
        for cp in sends:
            cp.wait_send()
        for cp in recvs:
            cp.wait_recv()

    return list(pl.pallas_call(
        body, name=name, in_specs=[HBM] * n + [SEM, SEM, ANY], out_specs=[HBM] * n, out_shape=hbm,
        input_output_aliases={i: i for i in range(n)},
        compiler_params=pltpu.CompilerParams(has_side_effects=EFFECT))(*arrays, sems[0], sems[1], after))


def _rows_half(ref_rows, half):
    h = ref_rows // 2
    return pl.ds(pl.multiple_of(half * h, 8), h)


def _remote(src, dst, send_sem, recv_sem, device):
    return pltpu.make_async_remote_copy(src_ref=src, dst_ref=dst, send_sem=send_sem, recv_sem=recv_sem,
                                        device_id=device, device_id_type=MESH)


def _plan_gather_ici(k_n, l):
    def plan(refs, send, recv):
        x, y, c, me, chips = _place()
        sends, recvs = [], []
        for k in range(k_n):
            src, buf = refs[k], refs[k_n + k]
            rows = _rows_half(src.shape[1], c)
            for j, (cx, cy) in enumerate(chips):
                sends.append(_remote(src.at[l, rows], buf.at[0, me, rows], send.at[3 * k + j], recv.at[3 * k + j], (cx, cy, c)))
                got = buf.at[0, 2 * cx + cy, rows]
                recvs.append(_remote(got, got, send.at[3 * k + j], recv.at[3 * k + j], (cx, cy, c)))
        return sends, recvs
    return plan


def _plan_gather_d2d(k_n):
    def plan(refs, send, recv):
        x, y, c, me, chips = _place()
        sends, recvs = [], []
        for k in range(k_n):
            buf = refs[k]
            for j, (cx, cy) in enumerate(chips):
                mine = buf.at[0, 2 * cx + cy, _rows_half(buf.shape[2], c)]
                theirs = buf.at[0, 2 * cx + cy, _rows_half(buf.shape[2], 1 - c)]
                sends.append(_remote(mine, mine, send.at[3 * k + j], recv.at[3 * k + j], (x, y, 1 - c)))
                recvs.append(_remote(theirs, theirs, send.at[3 * k + j], recv.at[3 * k + j], (x, y, 1 - c)))
        return sends, recvs
    return plan


def _plan_swap(k_n):
    def plan(refs, send, recv):
        x, y, c, me, chips = _place()
        sends, recvs = [], []
        for k in range(k_n):
            src, land = refs[k], refs[k_n + k]
            sends.append(_remote(src.at[:, _rows_half(src.shape[1], 1 - c)], land, send.at[k], recv.at[k], (x, y, 1 - c)))
            recvs.append(_remote(land, land, send.at[k], recv.at[k], (x, y, 1 - c)))
        return sends, recvs
    return plan


def _plan_scatter(k_n):
    def plan(refs, send, recv):
        x, y, c, me, chips = _place()
        sends, recvs = [], []
        for k in range(k_n):
            src, land = refs[k], refs[k_n + k]
            for j, (cx, cy) in enumerate(chips):
                sends.append(_remote(src.at[2 * cx + cy], land.at[j], send.at[3 * k + j], recv.at[3 * k + j], (cx, cy, c)))
                recvs.append(_remote(land.at[j], land.at[j], send.at[3 * k + j], recv.at[3 * k + j], (cx, cy, c)))
        return sends, recvs
    return plan


def _plan_share(k_n, layers):
    def plan(refs, send, recv):
        x, y, c, me, chips = _place()
        sends, recvs = [], []
        for k in range(k_n):
            out = refs[k]
            for l in range(layers):
                mine = out.at[l, _rows_half(out.shape[1], c)]
                theirs = out.at[l, _rows_half(out.shape[1], 1 - c)]
                sends.append(_remote(mine, mine, send.at[layers * k + l], recv.at[layers * k + l], (x, y, 1 - c)))
                recvs.append(_remote(theirs, theirs, send.at[layers * k + l], recv.at[layers * k + l], (x, y, 1 - c)))
        return sends, recvs
    return plan


def _add_sibling_half(idx, mine, theirs):
    q, r, c = mine.shape
    h = r // 2
    tm = _tile(h, 256)

    def fn(idx_ref, a_ref, b_ref, o_ref):
        o_ref[...] = (a_ref[...].astype(F32) + b_ref[...].astype(F32)).astype(o_ref.dtype)

    return _prefetch_call(
        "add_sibling", fn, idx, (q, h // tm),
        [pl.BlockSpec((None, None, tm, c), lambda b, i, ix: (b, ix[0], i, 0)),
         pl.BlockSpec((None, tm, c), lambda b, i, ix: (b, i, 0))],
        pl.BlockSpec((None, tm, c), lambda b, i, ix: (b, i, 0)), jax.ShapeDtypeStruct((q, h, c), WIRE_DTYPE),
        [mine.reshape(q, 2, h, c), theirs])


def _add_chips_half(idx, own, others, layers, l, into):
    _, h, c = own.shape
    tm = _tile(h, 256)

    def fn(*refs):
        a_ref, b_ref, o_ref = refs[-3:]
        tot = a_ref[...].astype(F32)
        for k in range(3):
            tot = tot + b_ref[k].astype(F32)
        o_ref[...] = tot

    in_specs = [pl.BlockSpec((None, tm, c), lambda i, ix: (ix[1], i, 0)), pl.BlockSpec((3, tm, c), lambda i, ix: (0, i, 0))]
    args = [own, others]
    aliases = None
    if into is not None:
        in_specs = [ANY] + in_specs
        args = [into.reshape(layers, 2, h, c)] + args
        aliases = {1: 0}
    out = _prefetch_call(
        "add_chips", fn, idx, (h // tm,), in_specs,
        pl.BlockSpec((None, None, tm, c), lambda i, ix: (l, ix[0], i, 0)),
        jax.ShapeDtypeStruct((layers, 2, h, c), F32), args, aliases)
    return out.reshape(layers, 2 * h, c)


SHARDED = ["ffn1_w_gate", "ffn1_w_up", "ffn1_w_down", "w_in", "w_uq", "w_ukv", "conv_w", "w_out",
           "ffn2_w_gate", "ffn2_w_up", "ffn2_w_down"]
SMALL = ["ffn1_norm", "mix_norm", "q_latent_norm", "kv_latent_norm", "q_norm", "k_norm", "conv_b", "conv_ln_g",
         "conv_ln_b", "ffn2_norm", "post_norm"]
ALL_WEIGHTS = ["ffn1_norm", "ffn1_w_gate", "ffn1_w_up", "ffn1_w_down", "mix_norm", "w_in", "q_latent_norm", "w_uq",
               "kv_latent_norm", "w_ukv", "q_norm", "k_norm", "conv_w", "conv_b", "conv_ln_g", "conv_ln_b", "w_out",
               "ffn2_norm", "ffn2_w_gate", "ffn2_w_up", "ffn2_w_down", "post_norm"]


def _rope_tables(s):
    pos = jnp.arange(s, dtype=F32)
    inv_freq = 1.0 / (ROPE_THETA ** (jnp.arange(0, QK_ROPE, 2, dtype=F32) / QK_ROPE))
    ang = pos[:, None] * inv_freq[None, :]
    cos, sin = jnp.cos(ang), jnp.sin(ang)
    half = QK_ROPE // 2
    one = jnp.ones((s, QK_NOPE), F32)
    zero = jnp.zeros((s, QK_NOPE), F32)
    pad = HEAD_PAD - QK_DIM
    z16 = jnp.zeros((s, half), F32)
    cosf = jnp.concatenate([one, cos, cos, jnp.ones((s, pad), F32)], axis=1)
    s1 = jnp.concatenate([zero, -sin, z16, jnp.zeros((s, pad), F32)], axis=1)
    s2 = jnp.concatenate([zero, z16, sin, jnp.zeros((s, pad), F32)], axis=1)
    return cosf, s1, s2


def _ffn_fwd(n, w, names, l):
    a, b, hm = _ffn_up_fused(n, w[names[0]], w[names[1]], l)
    return a, b, hm, _blocked_sum("ffn_down", [hm], [w[names[2]]], l, "nn")


def _ffn_bwd(nt, a, b, hm, do, w, names, grads):
    gate, up, down = names
    da, db = _ffn_down_t_fused(do, w[down], a, b, 0)
    grads[down] = _wgrad("ffn_down_wgrad", hm, do, WIRE_DTYPE)
    g_gate, g_up = _wgrad_pair("ffn_up_wgrad", nt, da, db, 1, 0, None)
    grads[gate], grads[up] = g_gate[0], g_up[0]
    return [_blocked_sum("ffn_up_t", [da, db], [w[gate], w[up]], 0, "nt")]


def _vec(v, l):
    return v[l][None, :]


def _pad_lanes(v, width):
    return jnp.pad(v, [(0, 0)] * (v.ndim - 1) + [(0, width - v.shape[-1])])


def kernel(x, ffn1_norm, ffn1_w_gate, ffn1_w_up, ffn1_w_down, mix_norm, w_in, q_latent_norm, w_uq, kv_latent_norm, w_ukv, q_norm, k_norm, conv_w, conv_b, conv_ln_g, conv_ln_b, w_out, ffn2_norm, ffn2_w_gate, ffn2_w_up, ffn2_w_down, post_norm, loss_target, m_ffn1_norm, m_ffn1_w_gate, m_ffn1_w_up, m_ffn1_w_down, m_mix_norm, m_w_in, m_q_latent_norm, m_w_uq, m_kv_latent_norm, m_w_ukv, m_q_norm, m_k_norm, m_conv_w, m_conv_b, m_conv_ln_g, m_conv_ln_b, m_w_out, m_ffn2_norm, m_ffn2_w_gate, m_ffn2_w_up, m_ffn2_w_down, m_post_norm, v_ffn1_norm, v_ffn1_w_gate, v_ffn1_w_up, v_ffn1_w_down, v_mix_norm, v_w_in, v_q_latent_norm, v_w_uq, v_kv_latent_norm, v_w_ukv, v_q_norm, v_k_norm, v_conv_w, v_conv_b, v_conv_ln_g, v_conv_ln_b, v_w_out, v_ffn2_norm, v_ffn2_w_gate, v_ffn2_w_up, v_ffn2_w_down, v_post_norm):
    given = dict(locals())
    wts = {n: given[n] for n in ALL_WEIGHTS}
    mom = {n: given["m_" + n] for n in ALL_WEIGHTS}
    var = {n: given["v_" + n] for n in ALL_WEIGHTS}
    n_layers = ffn1_norm.shape[0]
    s, d = x.shape[1], x.shape[2]
    ql, kvl, cw = q_latent_norm.shape[-1], kv_latent_norm.shape[-1], conv_b.shape[-1]
    xs = x.reshape(s, d)
    target = loss_target.reshape(s, d)

    assert n_layers == 2, "the layers' exchanges run one behind the other"
    local = [jnp.pad(wts[n], ((0, 0), (0, 1), (0, 0))).astype(MM_DTYPE) if n == "conv_w" else wts[n].astype(MM_DTYPE)
             for n in SHARDED]
    L = n_layers
    k_n = len(SHARDED)
    my_c = lax.axis_index("c")
    my_chip = 2 * lax.axis_index("x") + lax.axis_index("y")
    place = jnp.stack([my_c, my_chip]).astype(jnp.int32)
    hw = N_HEADS * HEAD_PAD
    tabs = _rope_tables(s)
    pe_block = (2 * cw + ql + kvl) // HEAD_PAD
    gq_pad, gk_pad = _pad_lanes(q_norm, HEAD_PAD), _pad_lanes(k_norm, HEAD_PAD)
    tq = _tile(s, 256)

    def own_blocks(l):
        return [lax.dynamic_update_slice(lax.empty((1, N_CHIPS) + t.shape[1:], t.dtype), t[l][None, None],
                                         (0, my_chip, 0, 0)) for t in local]

    def layouts(w):
        win = w["w_in"].transpose(0, 2, 1, 3).reshape(1, d, -1)
        zc = lambda n: jnp.zeros((1, d, n), MM_DTYPE)
        o_pe, o_cv = ql + kvl, ql + kvl + QK_ROPE
        w["w_in"] = jnp.concatenate([win[..., o_cv:], win[..., :o_pe], zc(QK_NOPE), win[..., o_pe:o_cv],
                                     zc(HEAD_PAD - QK_DIM)], axis=-1)[:, None]
        wq = w["w_uq"].transpose(0, 2, 1, 3).reshape(1, ql, N_HEADS, QK_DIM)
        w["wq"] = _pad_lanes(wq, HEAD_PAD).reshape(1, 1, ql, hw)
        wkv = w["w_ukv"].transpose(0, 2, 1, 3).reshape(1, kvl, N_HEADS, QK_NOPE + V_DIM)
        w["wk"] = _pad_lanes(wkv[..., :QK_NOPE], HEAD_PAD).reshape(1, 1, kvl, hw)
        w["wv"] = _pad_lanes(wkv[..., QK_NOPE:], HEAD_PAD).reshape(1, 1, kvl, hw)
        w["wvt"] = w["wv"][:, 0].transpose(0, 2, 1)
        w["conv"] = w["conv_w"].transpose(0, 2, 1, 3).reshape(CONV_K + 1, cw).astype(F32)
        wout = w["w_out"].reshape(1, d, d)
        wo_a = wout[:, :N_HEADS * V_DIM].reshape(1, N_HEADS, V_DIM, d)
        w["wo_a"] = jnp.pad(wo_a, ((0, 0), (0, 0), (0, HEAD_PAD - V_DIM), (0, 0))).reshape(1, 1, hw, d)
        w["wo_c"] = wout[:, N_HEADS * V_DIM:][:, None]
        return w

    ici, d2d, sem_kj = (lambda l: _plan_gather_ici(k_n, l)), _plan_gather_d2d(k_n), (3 * k_n,)
    n_a = 3
    n_b = k_n - n_a
    own0 = own_blocks(0)
    landed = _exchange("gather0a_ici", "run", local[:n_a] + own0[:n_a], (3 * n_a,), _plan_gather_ici(n_a, 0))
    w0 = dict(zip(SHARDED[:n_a], _exchange("gather0a_d2d", "run", landed[n_a:], (3 * n_a,), _plan_gather_d2d(n_a))))
    local = landed[:n_a] + local[n_a:]
    sems_r, token_r, rest = _exchange("gather0b_ici_start", "start", local[n_a:] + own0[n_a:], (3 * n_b,),
                                      _plan_gather_ici(n_b, 0))

    def fwd_ffn(l, cur, w, tok):
        sv = {"x": cur}
        n1, sv["n1t"] = _norm_fwd(cur, _vec(ffn1_norm, l) + tok)
        sv["a1"], sv["b1"], sv["hm1"], o1 = _ffn_fwd(n1, w, SHARDED[0:3], 0)
        sv["x1"], sv["h"] = _resid_norm(cur, [o1], 0.5, _vec(mix_norm, l), MM_DTYPE)
        return sv

    def fwd_first(l, sv, w, tok):
        sv["p"] = _proj("w_in", sv["h"], w["w_in"], 0, "nn")[0]
        sv["cq"], sv["ckv"], ckvt = _mla_pre(sv["p"], _vec(q_latent_norm, l) + tok, _vec(kv_latent_norm, l), cw)
        sv["q_pre"] = _proj("wq", sv["cq"], w["wq"], 0, "nn")[0]
        sv["k_pre"] = _proj("wk", sv["ckv"], w["wk"], 0, "nn")[0]
        sv["v"] = _proj("wv", sv["ckv"], w["wv"], 0, "nn", MM_DTYPE)[0]
        sv["vt"] = _mm("wv_transposed", w["wvt"], ckvt, "nn", (s // tq,),
                       pl.BlockSpec((None, hw, kvl), lambda i: (0, 0, 0)), pl.BlockSpec((kvl, tq), lambda i: (0, i)),
                       pl.BlockSpec((None, hw, tq), lambda i: (i, 0, 0)),
                       jax.ShapeDtypeStruct((s // tq, hw, tq), MM_DTYPE), 1, None)
        sv["q"], sv["k"], sv["kt"] = _qk_post(sv["q_pre"], sv["k_pre"], sv["p"], pe_block, tabs,
                                              _vec(gq_pad, l), _vec(gk_pad, l))
        return sv

    def fwd_attn(l, sv, w):
        sv["c"] = _conv_fwd(sv["p"], w["conv"], _vec(conv_b, l), _vec(conv_ln_g, l), _vec(conv_ln_b, l))
        sv["o"], sv["lse"] = _attn_fwd(sv["q"], sv["k"], sv.pop("vt"))
        return sv

    def fwd_second(l, sv, w, tok):
        mix_a = _contract("wo_a", sv["o"][None], w["wo_a"], 0, "nn")
        mix_c = _contract("wo_c", sv["c"][None], w["wo_c"], 0, "nn")
        sv["x2"], n2, sv["n2t"] = _resid_norm(sv["x1"], [mix_a, mix_c], 1.0, _vec(ffn2_norm, l) + tok, MM_DTYPE, True)
        sv["a2"], sv["b2"], sv["hm2"], o2 = _ffn_fwd(n2, w, SHARDED[8:11], 0)
        sv["x3"], out = _final_fwd(sv["x2"], o2, _vec(post_norm, l))
        return out

    sv0 = fwd_ffn(0, xs, w0, token_r[0, 0])
    rest = _exchange("gather0b_ici_wait", "wait", rest, (3 * n_b,), _plan_gather_ici(n_b, 0), sems=sems_r, after=sv0["h"])
    w0.update(zip(SHARDED[n_a:], _exchange("gather0b_d2d", "run", rest[n_b:], (3 * n_b,), _plan_gather_d2d(n_b))))
    w0 = layouts(w0)
    local = local[:n_a] + rest[:n_b]
    sems_a, token_a, in_flight = _exchange("gather1_ici_start", "start", local + own_blocks(1), sem_kj, ici(1))
    sv0 = fwd_attn(0, fwd_first(0, sv0, w0, token_a[0, 0]), w0)
    in_flight = _exchange("gather1_ici_wait", "wait", in_flight, sem_kj, ici(1), sems=sems_a, after=sv0["o"])
    sems_b, token_b, in_flight = _exchange("gather1_d2d_start", "start", in_flight[k_n:], sem_kj, d2d)
    cur = fwd_second(0, sv0, w0, token_b[0, 0])
    w1 = layouts(dict(zip(SHARDED, _exchange("gather1_d2d_wait", "wait", in_flight, sem_kj, d2d, sems=sems_b, after=cur))))
    sv1 = fwd_attn(1, fwd_first(1, fwd_ffn(1, cur, w1, 0.0), w1, 0.0), w1)
    cur = fwd_second(1, sv1, w1, 0.0)
    dy, loss_part = _loss(cur, target)

    small = {n: [None] * L for n in SMALL}

    def bwd_first(l, sv, dy, w, tok, grads):
        dx3, do2, small["post_norm"][l] = _norm_bwd(sv["x3"], _vec(post_norm, l) + tok, [dy], None, 0.5)
        dn2 = _ffn_bwd(sv["n2t"], sv["a2"], sv["b2"], sv["hm2"], do2, w, SHARDED[8:11], grads)
        dx2, dx2b, small["ffn2_norm"][l] = _norm_bwd(sv["x2"], _vec(ffn2_norm, l), dn2, dx3, 1.0)
        do = _proj("wo_a_t", dx2b, w["wo_a"], 0, "nt")[0]
        dc = _proj("wo_c_t", dx2b, w["wo_c"], 0, "nt")[0]
        g_wo_a = _wgrad("wo_a_wgrad", sv["o"], dx2b[None], WIRE_DTYPE)[0].reshape(N_HEADS, HEAD_PAD, d)[:, :V_DIM]
        g_wo_c = _wgrad("wo_c_wgrad", sv["c"], dx2b[None], WIRE_DTYPE)
        grads["w_out"] = jnp.concatenate(
            [g_wo_a.reshape(N_HEADS * V_DIM, d), g_wo_c[0]], axis=0).reshape(N_CHIPS, d // N_CHIPS, d)
        dq, dk, dv = _attn_bwd(sv["q"], sv["k"], sv["kt"], sv["v"], do, sv["lse"], _attn_delta(sv["o"], do))
        dq_pre, dk_pre, dpe, small["q_norm"][l], small["k_norm"][l] = _qk_post_bwd(
            sv["q_pre"], sv["k_pre"], sv["p"], pe_block, tabs, _vec(gq_pad, l), _vec(gk_pad, l), dq, dk)
        return dict(dx2=dx2, dc=dc, dq_pre=dq_pre, dk_pre=dk_pre, dv=dv, dpe=dpe)

    def bwd_second(l, sv, st, w, tok, grads):
        dq_pre, dk_pre, dv = st["dq_pre"], st["dk_pre"], st["dv"]
        dcq = _contract("wq_t", dq_pre[None], w["wq"], 0, "nt")
        dckv = [_contract("wk_t", dk_pre[None], w["wk"], 0, "nt"), _contract("wv_t", dv[None], w["wv"], 0, "nt")]
        per_head = lambda t, rows: t[0].reshape(rows, N_HEADS, HEAD_PAD)
        g_wq = per_head(_wgrad("wq_wgrad", sv["cq"], dq_pre[None], WIRE_DTYPE), ql)[..., :QK_DIM]
        grads["w_uq"] = g_wq.reshape(ql, N_CHIPS, -1).transpose(1, 0, 2)
        g_wkv = jnp.concatenate([per_head(_wgrad("wk_wgrad", sv["ckv"], dk_pre[None], WIRE_DTYPE), kvl)[..., :QK_NOPE],
                                 per_head(_wgrad("wv_wgrad", sv["ckv"], dv[None], WIRE_DTYPE), kvl)[..., :V_DIM]], axis=-1)
        grads["w_ukv"] = g_wkv.reshape(kvl, N_CHIPS, -1).transpose(1, 0, 2)
        dyc, small["conv_ln_g"][l], small["conv_ln_b"][l], small["conv_b"][l] = _conv_bwd_ln(
            sv["p"], w["conv"], _vec(conv_b, l) + tok, _vec(conv_ln_g, l), _vec(conv_ln_b, l), st["dc"])
        dp_conv, g_conv = _conv_bwd_taps(sv["p"], w["conv"], dyc)
        grads["conv_w"] = g_conv.reshape(CONV_K + 1, N_CHIPS, cw // N_CHIPS).transpose(1, 0, 2).astype(WIRE_DTYPE)
        dp, small["q_latent_norm"][l], small["kv_latent_norm"][l] = _dp_assemble(
            sv["p"], _vec(q_latent_norm, l), _vec(kv_latent_norm, l), dcq, dckv, dp_conv, st["dpe"], cw)
        dh = _contract("w_in_t", dp[None], w["w_in"], 0, "nt")
        g_win = _wgrad("w_in_wgrad", sv["h"], dp[None], WIRE_DTYPE)[0]
        o_q, o_pe_col = 2 * cw, 2 * cw + ql + kvl + QK_NOPE
        g_win = jnp.concatenate([g_win[:, o_q:o_q + ql + kvl], g_win[:, o_pe_col:o_pe_col + QK_ROPE], g_win[:, :o_q]], axis=1)
        grads["w_in"] = g_win.reshape(d, N_CHIPS, -1).transpose(1, 0, 2)
        dx1, do1, small["mix_norm"][l] = _norm_bwd(sv["x1"], _vec(mix_norm, l), [dh], st["dx2"], 0.5)
        dn1 = _ffn_bwd(sv["n1t"], sv["a1"], sv["b1"], sv["hm1"], do1, w, SHARDED[0:3], grads)
        dx, _, small["ffn1_norm"][l] = _norm_bwd(sv["x"], _vec(ffn1_norm, l), dn1, dx1, 1.0)
        return dx

    def halves(parts):
        return [lax.empty((N_CHIPS, t.shape[1] // 2, t.shape[2]), t.dtype) for t in parts]

    def thirds(parts):
        return [lax.empty((N_CHIPS - 1,) + t.shape[1:], t.dtype) for t in parts]

    def chip_sums(done):
        n = len(done) // 2
        return [_add_sibling_half(place, g, t) for g, t in zip(done[:n], done[n:])]

    def finish(done, l, into):
        n = len(done) // 2
        return [_add_chips_half(place, own, others, L, l, t) for own, others, t in zip(done[:n], done[n:], into)]

    swap, scatter, sem_k = _plan_swap(k_n), _plan_scatter(k_n), (k_n,)
    grads1, grads0 = {}, {}
    st1 = bwd_first(1, sv1, dy, w1, 0.0, grads1)
    dy = bwd_second(1, sv1, st1, w1, 0.0, grads1)
    parts1 = [grads1[n] for n in SHARDED]
    sems_s, token_s, in_flight = _exchange("swap1_start", "start", parts1 + halves(parts1), sem_k, swap)
    st0 = bwd_first(0, sv0, dy, w0, token_s[0, 0], grads0)
    sums1 = chip_sums(_exchange("swap1_wait", "wait", in_flight, sem_k, swap, sems=sems_s, after=st0["dq_pre"]))
    sems_c, token_c, in_flight = _exchange("scatter1_start", "start", sums1 + thirds(sums1), sem_kj, scatter)
    early = [n for n in SHARDED if n in grads0]
    late = [n for n in SHARDED if n not in grads0]
    n_e, n_l = len(early), len(late)
    parts_e = [grads0[n] for n in early]
    sums_e = chip_sums(_exchange("swap0_early", "run", parts_e + halves(parts_e), (n_e,), _plan_swap(n_e)))
    sems_e, token_e, early_flight = _exchange("scatter0_early_start", "start", sums_e + thirds(sums_e), (3 * n_e,),
                                              _plan_scatter(n_e))
    dy = bwd_second(0, sv0, st0, w0, token_c[0, 0] + token_e[0, 0], grads0)
    grad_x = dy.reshape(x.shape)
    out = dict(zip(SHARDED, finish(_exchange("scatter1_wait", "wait", in_flight, sem_kj, scatter, sems=sems_c, after=dy),
                                   1, [None] * k_n)))
    done = _exchange("scatter0_early_wait", "wait", early_flight, (3 * n_e,), _plan_scatter(n_e), sems=sems_e, after=dy)
    out.update(zip(early, finish(done, 0, [out[n] for n in early])))
    parts_l = [grads0[n] for n in late]
    sums_l = chip_sums(_exchange("swap0_late", "run", parts_l + halves(parts_l), (n_l,), _plan_swap(n_l)))
    done = _exchange("scatter0_late", "run", sums_l + thirds(sums_l), (3 * n_l,), _plan_scatter(n_l))
    out.update(zip(late, finish(done, 0, [out[n] for n in late])))
    full = dict(zip(SHARDED, _exchange("share", "run", [out[n] for n in SHARDED], (L * k_n,), _plan_share(k_n, L))))
    full["conv_w"] = full["conv_w"][:, :CONV_K]

    width = max(wts[n].shape[-1] for n in SMALL)
    pack = lambda vals: jnp.concatenate([_pad_lanes(v.reshape(-1, v.shape[-1]), width) for v in vals], axis=0)
    part = pack([jnp.concatenate(small[n], axis=0)[..., :wts[n].shape[-1]] for n in SMALL])
    rows = part.shape[0]
    part = jnp.pad(part, [(0, -rows % 8), (0, 0)])
    small_sum = _add_leading("add_devices", _gather_small(part), F32)
    for i, n in enumerate(SMALL):
        full[n] = small_sum[i * L:(i + 1) * L, :wts[n].shape[-1]]

    delta_w, new_m, new_v = {}, {}, {}
    for n in SHARDED:
        shape = wts[n].shape
        swap = shape[-1] % HEAD_PAD != 0 and shape[-2] % HEAD_PAD == 0
        view = (lambda t: jnp.swapaxes(t, 1, 2)) if swap else (lambda t: t)
        flat = lambda t: view(t).reshape(-1, view(t).shape[-1])
        r = flat(wts[n]).shape[0]
        fix = (lambda t: jnp.pad(flat(t), [(0, -r % 8), (0, 0)])) if r % 8 else flat
        outs = _adamw(fix(wts[n]), fix(full[n]), fix(mom[n]), fix(var[n]))
        delta_w[n], new_m[n], new_v[n] = [view(o[:r].reshape(view(wts[n]).shape)) for o in outs]
    pk = lambda src: jnp.pad(pack([src[n] for n in SMALL]), [(0, -rows % 8), (0, 0)])
    outs = _adamw(pk(wts), small_sum, pk(mom), pk(var))
    for i, n in enumerate(SMALL):
        delta_w[n], new_m[n], new_v[n] = [o[i * L:(i + 1) * L, :wts[n].shape[-1]] for o in outs]

    loss = lax.psum(loss_part[0, 0], ("x", "y", "c"))
    return (loss, grad_x, *[full[n] for n in ALL_WEIGHTS], *[delta_w[n] for n in ALL_WEIGHTS],
            *[new_m[n] for n in ALL_WEIGHTS], *[new_v[n] for n in ALL_WEIGHTS])
```

```python
import jax
import jax.numpy as jnp
from jax import lax
from jax.experimental import pallas as pl
from jax.experimental.pallas import tpu as pltpu

F32 = jnp.float32
MM_DTYPE = jnp.bfloat16
WIRE_DTYPE = jnp.bfloat16
EPS = 1e-6
N_HEADS = 8
QK_NOPE = 64
QK_ROPE = 32
QK_DIM = QK_NOPE + QK_ROPE
V_DIM = 64
HEAD_PAD = 128
HEADS_PER_STEP = 4
CHUNK_SHIFT = 6
LOG2E = 1.4426950408889634
CONV_K = 31
HALO = 32
ROPE_THETA = 10000.0
N_CHIPS = 4
ADAM_LR, ADAM_B1, ADAM_B2, ADAM_EPS, ADAM_WD, ADAM_STEP = 0.001, 0.9, 0.999, 1e-08, 0.01, 10
VMEM_LIMIT_BYTES = 56 * 2 ** 20
WGRAD_ROWS = 2048
MESH = pl.DeviceIdType.MESH
ANY = pl.BlockSpec(memory_space=pl.ANY)

_DIMS = {
    "nn": (((1,), (0,)), ((), ())),
    "nt": (((1,), (1,)), ((), ())),
    "tn": (((0,), (0,)), ((), ())),
}


def _params(n_axes):
    return pltpu.CompilerParams(dimension_semantics=("arbitrary",) * n_axes, vmem_limit_bytes=VMEM_LIMIT_BYTES)


def _tile(n, cap):
    for step in (16, 8):
        for t in range(cap - cap % step, 0, -step):
            if n % t == 0:
                return t
    return n


def _mm(name, a, b, dims, grid, a_spec, b_spec, o_spec, out_shape, nk, acc_shape, into=None):
    n_axes = len(grid)

    def body(*refs):
        if into is not None:
            refs = refs[1:]
        a_ref, b_ref, o_ref = refs[:3]
        part = lax.dot_general(a_ref[...].astype(MM_DTYPE), b_ref[...].astype(MM_DTYPE), _DIMS[dims],
                               preferred_element_type=F32)
        if nk == 1:
            o_ref[...] = part.astype(o_ref.dtype)
        else:
            acc_ref = refs[3]
            k = pl.program_id(n_axes - 1)

            @pl.when(k == 0)
            def _():
                acc_ref[...] = part

            @pl.when(k > 0)
            def _():
                acc_ref[...] += part

            @pl.when(k == nk - 1)
            def _():
                o_ref[...] = acc_ref[...].astype(o_ref.dtype)

    in_specs = [a_spec, b_spec]
    args = [a, b]
    aliases = {}
    if into is not None:
        in_specs = [ANY] + in_specs
        args = [into] + args
        aliases = {0: 0}
    return pl.pallas_call(
        body, name=name, grid=grid, in_specs=in_specs, out_specs=o_spec, out_shape=out_shape,
        scratch_shapes=[] if nk == 1 else [pltpu.VMEM(acc_shape, F32)],
        input_output_aliases=aliases, compiler_params=_params(n_axes))(*args)


def _proj(name, a, w, l, dims, out_dtype=F32):
    s, k = a.shape
    g = w.shape[1]
    n = w.shape[3] if dims == "nn" else w.shape[2]
    tm = _tile(s, 512)
    return _mm(name, a, w, dims, (g, s // tm),
               pl.BlockSpec((tm, k), lambda j, i: (i, 0)),
               pl.BlockSpec((None, None) + w.shape[2:], lambda j, i: (l, j, 0, 0)),
               pl.BlockSpec((None, tm, n), lambda j, i: (j, i, 0)),
               jax.ShapeDtypeStruct((g, s, n), out_dtype), 1, None)


def _contract(name, a, w, l, dims, out_dtype=F32):
    g, s, kb = a.shape
    n = w.shape[3] if dims == "nn" else w.shape[2]
    tm = _tile(s, 512)
    return _mm(name, a, w, dims, (s // tm, g),
               pl.BlockSpec((None, tm, kb), lambda i, j: (j, i, 0)),
               pl.BlockSpec((None, None) + w.shape[2:], lambda i, j: (l, j, 0, 0)),
               pl.BlockSpec((tm, n), lambda i, j: (i, 0)),
               jax.ShapeDtypeStruct((s, n), out_dtype), g, (tm, n))


def _wgrad(name, a, b, out_dtype, layers=None, l=None, into=None):
    g = a.shape[0] if a.ndim == 3 else b.shape[0]
    s, k = a.shape[-2:]
    n = b.shape[-1]
    tm = _tile(s, WGRAD_ROWS)

    def spec(x, w):
        if x.ndim == 3:
            return pl.BlockSpec((None, tm, w), lambda j, i: (j, i, 0))
        return pl.BlockSpec((tm, w), lambda j, i: (i, 0))

    if layers is None:
        o_spec = pl.BlockSpec((None, k, n), lambda j, i: (j, 0, 0))
        out_shape = jax.ShapeDtypeStruct((g, k, n), out_dtype)
    else:
        o_spec = pl.BlockSpec((None, None, k, n), lambda j, i: (l, j, 0, 0))
        out_shape = jax.ShapeDtypeStruct((layers, g, k, n), out_dtype)
    return _mm(name, a, b, "tn", (g, s // tm), spec(a, k), spec(b, n), o_spec, out_shape, s // tm, (k, n), into=into)


def _ffn_up_fused(n, wg, wu, l):
    s, k = n.shape
    g, n4 = wg.shape[1], wg.shape[3]
    tm = _tile(s, 512)

    def body(n_ref, wg_ref, wu_ref, a_ref, b_ref, h_ref):
        nv = n_ref[...]
        a = jnp.dot(nv, wg_ref[...], preferred_element_type=F32)
        b = jnp.dot(nv, wu_ref[...], preferred_element_type=F32)
        a_ref[...] = a.astype(a_ref.dtype)
        b_ref[...] = b.astype(b_ref.dtype)
        h_ref[...] = (a * _sigmoid(a) * b).astype(h_ref.dtype)

    w_spec = pl.BlockSpec((None, None, k, n4), lambda j, i: (l, j, 0, 0))
    o_spec = pl.BlockSpec((None, tm, n4), lambda j, i: (j, i, 0))
    return pl.pallas_call(
        body, name="ffn_up_fused", grid=(g, s // tm),
        in_specs=[pl.BlockSpec((tm, k), lambda j, i: (i, 0)), w_spec, w_spec], out_specs=[o_spec] * 3,
        out_shape=[jax.ShapeDtypeStruct((g, s, n4), MM_DTYPE)] * 3, compiler_params=_params(2))(n, wg, wu)


def _ffn_down_t_fused(do, wd, a, b, l):
    s, k = do.shape
    g, n4 = wd.shape[1], wd.shape[2]
    tm = _tile(s, 512)

    def body(do_ref, wd_ref, a_ref, b_ref, da_ref, db_ref):
        dh = lax.dot_general(do_ref[...], wd_ref[...], _DIMS["nt"], preferred_element_type=F32)
        av = a_ref[...].astype(F32)
        sg = _sigmoid(av)
        da_ref[...] = (dh * b_ref[...].astype(F32) * (sg * (1.0 + av * (1.0 - sg)))).astype(da_ref.dtype)
        db_ref[...] = (dh * (av * sg)).astype(db_ref.dtype)

    t_spec = pl.BlockSpec((None, tm, n4), lambda j, i: (j, i, 0))
    return pl.pallas_call(
        body, name="ffn_down_t_fused", grid=(g, s // tm),
        in_specs=[pl.BlockSpec((tm, k), lambda j, i: (i, 0)),
                  pl.BlockSpec((None, None, n4, k), lambda j, i: (l, j, 0, 0)), t_spec, t_spec],
        out_specs=[t_spec] * 2, out_shape=[jax.ShapeDtypeStruct((g, s, n4), MM_DTYPE)] * 2,
        compiler_params=_params(2))(do, wd, a, b)


def _blocked_sum(name, acts, weights, l, dims):
    g, s, kb = acts[0].shape
    n = weights[0].shape[3] if dims == "nn" else weights[0].shape[2]
    tm = _tile(s, 256)
    n_pairs = len(acts)

    def body(*refs):
        o_ref = refs[2 * n_pairs]
        acc = None
        for p in range(n_pairs):
            for j in range(g):
                t = lax.dot_general(refs[p][j], refs[n_pairs + p][j], _DIMS[dims], preferred_element_type=F32)
                acc = t if acc is None else acc + t
        o_ref[...] = acc

    return pl.pallas_call(
        body, name=name, grid=(s // tm,),
        in_specs=[pl.BlockSpec((g, tm, kb), lambda i: (0, i, 0))] * n_pairs
        + [pl.BlockSpec((None,) + w.shape[1:], lambda i: (l, 0, 0, 0)) for w in weights],
        out_specs=pl.BlockSpec((tm, n), lambda i: (i, 0)), out_shape=jax.ShapeDtypeStruct((s, n), F32),
        compiler_params=_params(1))(*acts, *weights)


def _wgrad_pair(name, nt, da, db, layers, l, into):
    k, s = nt.shape
    g, _, n4 = da.shape
    tm = _tile(s, WGRAD_ROWS)
    nk = s // tm

    def body(*refs):
        if into is not None:
            refs = refs[2:]
        nt_ref, da_ref, db_ref, og_ref, ou_ref, accg, accu = refs
        i = pl.program_id(1)
        ntv = nt_ref[...]
        pg = jnp.dot(ntv, da_ref[...], preferred_element_type=F32)
        pu = jnp.dot(ntv, db_ref[...], preferred_element_type=F32)

        @pl.when(i == 0)
        def _():
            accg[...] = pg
            accu[...] = pu

        @pl.when(i > 0)
        def _():
            accg[...] += pg
            accu[...] += pu

        @pl.when(i == nk - 1)
        def _():
            og_ref[...] = accg[...].astype(og_ref.dtype)
            ou_ref[...] = accu[...].astype(ou_ref.dtype)

    t_spec = pl.BlockSpec((None, tm, n4), lambda j, i: (j, i, 0))
    o_spec = pl.BlockSpec((None, None, k, n4), lambda j, i: (l, j, 0, 0))
    in_specs = [pl.BlockSpec((k, tm), lambda j, i: (0, i)), t_spec, t_spec]
    args = [nt, da, db]
    aliases = {}
    if into is not None:
        in_specs = [ANY, ANY] + in_specs
        args = list(into) + args
        aliases = {0: 0, 1: 1}
    return pl.pallas_call(
        body, name=name, grid=(g, nk), in_specs=in_specs, out_specs=[o_spec] * 2,
        out_shape=[jax.ShapeDtypeStruct((layers, g, k, n4), WIRE_DTYPE)] * 2,
        scratch_shapes=[pltpu.VMEM((k, n4), F32)] * 2, input_output_aliases=aliases,
        compiler_params=_params(2))(*args)


def _rows(name, fn, s, tm, ins, outs, scratch=()):
    n = s // tm

    def spec(shape, kind):
        nd = len(shape)
        if kind == "r":
            return pl.BlockSpec(shape[:-2] + (tm, shape[-1]), lambda i: (0,) * (nd - 2) + (i, 0))
        if kind in ("f", "a"):
            return pl.BlockSpec(shape, lambda i: (0,) * nd)
        if kind == "t":
            return pl.BlockSpec((shape[0], tm), lambda i: (0, i))
        if kind == "b":
            return pl.BlockSpec((None,) + shape[1:], lambda i: (i, 0, 0))
        if kind == "p":
            return pl.BlockSpec((HALO, shape[-1]), lambda i: (jnp.maximum(i * (tm // HALO) - 1, 0), 0))
        if kind == "n":
            return pl.BlockSpec((HALO, shape[-1]), lambda i: (jnp.minimum((i + 1) * (tm // HALO), s // HALO - 1), 0))
        j = kind[1]
        return pl.BlockSpec((tm, HEAD_PAD), lambda i: (i, j))

    def body(*refs):
        fn(pl.program_id(0), n, *refs)

    return pl.pallas_call(
        body, name=name, grid=(n,),
        in_specs=[spec(a.shape, kind) for a, kind in ins],
        out_specs=[spec(shape, kind) for shape, _, kind in outs],
        out_shape=[jax.ShapeDtypeStruct(shape, dtype) for shape, dtype, _ in outs],
        scratch_shapes=list(scratch), compiler_params=_params(1))(*[a for a, _ in ins])


def _acc(ref, i, val):
    @pl.when(i == 0)
    def _():
        ref[...] = val

    @pl.when(i > 0)
    def _():
        ref[...] += val


def _sum0(x):
    return jnp.sum(x, axis=0, keepdims=True)


def _rstd(x, n):
    return lax.rsqrt(jnp.sum(x * x, axis=-1, keepdims=True) * (1.0 / n) + EPS)


def _rms_bwd(x, g, dy, n):
    r = _rstd(x, n)
    xh = x * r
    dyg = dy * g
    dx = r * (dyg - xh * (jnp.sum(dyg * xh, axis=-1, keepdims=True) * (1.0 / n)))
    return dx, _sum0(dy * xh)


def _sigmoid(x):
    return 1.0 / (1.0 + jnp.exp(-x))


def _norm_fwd(x, g):
    s, d = x.shape

    def fn(i, n, x_ref, g_ref, o_ref, ot_ref):
        xv = x_ref[...]
        y = xv * _rstd(xv, d) * g_ref[...]
        o_ref[...] = y.astype(o_ref.dtype)
        ot_ref[...] = y.T.astype(ot_ref.dtype)

    return _rows("norm_fwd", fn, s, _tile(s, 256), [(x, "r"), (g, "f")],
                 [((s, d), MM_DTYPE, "r"), ((d, s), MM_DTYPE, "t")])


def _resid_norm(x, adds, scale, g, norm_dtype, transposed=False):
    s, d = x.shape
    k = len(adds)

    def fn(i, n, *refs):
        x_ref, add_refs, g_ref, xo_ref, no_ref = refs[0], refs[1:1 + k], refs[1 + k], refs[2 + k], refs[3 + k]
        tot = add_refs[0][...]
        for r in add_refs[1:]:
            tot = tot + r[...]
        xv = x_ref[...] + scale * tot
        xo_ref[...] = xv
        y = xv * _rstd(xv, d) * g_ref[...]
        no_ref[...] = y.astype(no_ref.dtype)
        if transposed:
            refs[4 + k][...] = y.T.astype(refs[4 + k].dtype)

    return _rows("resid_norm", fn, s, _tile(s, 256), [(x, "r")] + [(a, "r") for a in adds] + [(g, "f")],
                 [((s, d), F32, "r"), ((s, d), norm_dtype, "r")] + ([((d, s), MM_DTYPE, "t")] if transposed else []))


def _norm_bwd(x, g, dn_parts, dres, out_scale):
    s, d = x.shape
    k = len(dn_parts)
    has_res = dres is not None

    def fn(i, n, *refs):
        x_ref, g_ref = refs[0], refs[1]
        dn_refs = refs[2:2 + k]
        pos = 2 + k
        dn = dn_refs[0][...]
        for r in dn_refs[1:]:
            dn = dn + r[...]
        dx, dg = _rms_bwd(x_ref[...], g_ref[...], dn, d)
        if has_res:
            dx = dx + refs[pos][...]
            pos += 1
        dx_ref, dxs_ref, dg_ref = refs[pos:pos + 3]
        dx_ref[...] = dx
        dxs_ref[...] = (out_scale * dx).astype(dxs_ref.dtype)
        _acc(dg_ref, i, dg)

    ins = [(x, "r"), (g, "f")] + [(p, "r") for p in dn_parts] + ([(dres, "r")] if has_res else [])
    return _rows("norm_bwd", fn, s, _tile(s, 256), ins,
                 [((s, d), F32, "r"), ((s, d), MM_DTYPE, "r"), ((1, d), F32, "a")])


def _final_fwd(x, o, g):
    return _resid_norm(x, [o], 0.5, g, F32)


def _loss(y, t):
    s, d = y.shape

    def fn(i, n, y_ref, t_ref, dy_ref, l_ref):
        e = y_ref[...] - t_ref[...]
        dy_ref[...] = e * (1.0 / d)
        _acc(l_ref, i, (0.5 / d) * jnp.sum(jnp.sum(e * e, axis=-1, keepdims=True), axis=0, keepdims=True))

    return _rows("loss", fn, s, _tile(s, 256), [(y, "r"), (t, "r")], [((s, d), F32, "r"), ((1, 1), F32, "a")])


def _mla_pre(p, gql, gkvl, cw):
    s = p.shape[0]
    ql, kvl = gql.shape[-1], gkvl.shape[-1]
    o_q, o_kv = 2 * cw, 2 * cw + ql

    def fn(i, n, p_ref, gq_ref, gkv_ref, cq_ref, ckv_ref, ckvt_ref):
        cq = p_ref[:, o_q:o_q + ql]
        cq_ref[...] = (cq * _rstd(cq, ql) * gq_ref[...]).astype(cq_ref.dtype)
        ckv = p_ref[:, o_kv:o_kv + kvl]
        y = ckv * _rstd(ckv, kvl) * gkv_ref[...]
        ckv_ref[...] = y.astype(ckv_ref.dtype)
        ckvt_ref[...] = y.T.astype(ckvt_ref.dtype)

    return _rows("mla_pre", fn, s, _tile(s, 256), [(p, "r"), (gql, "f"), (gkvl, "f")],
                 [((s, ql), MM_DTYPE, "r"), ((s, kvl), MM_DTYPE, "r"), ((kvl, s), MM_DTYPE, "t")])


def _rope(x, cosf, s1, s2):
    return x * cosf + pltpu.roll(x, HEAD_PAD - 16, 1) * s1 + pltpu.roll(x, 16, 1) * s2


def _rope_bwd(d, cosf, s1, s2):
    return d * cosf + pltpu.roll(d * s1, 16, 1) + pltpu.roll(d * s2, HEAD_PAD - 16, 1)


def _qk_post(q_pre, k_pre, p, pe_block, tabs, gq, gk):
    s, hw = q_pre.shape
    cosf, s1, s2 = tabs

    tm = _tile(s, 256)

    def fn(i, n, q_ref, k_ref, pe_ref, c_ref, s1_ref, s2_ref, gq_ref, gk_ref, qo_ref, ko_ref, kt_ref):
        c, a1, a2, pe = c_ref[...], s1_ref[...], s2_ref[...], pe_ref[...]
        for hh in range(N_HEADS):
            cols = slice(hh * HEAD_PAD, (hh + 1) * HEAD_PAD)
            qv = q_ref[:, cols]
            qo_ref[:, cols] = _rope(qv * _rstd(qv, QK_DIM) * gq_ref[...], c, a1, a2).astype(qo_ref.dtype)
            kv = k_ref[:, cols] + pe
            kout = _rope(kv * _rstd(kv, QK_DIM) * gk_ref[...], c, a1, a2)
            ko_ref[:, cols] = kout.astype(ko_ref.dtype)
            kt_ref[cols, :] = kout.T.astype(kt_ref.dtype)

    return _rows("qk_post", fn, s, tm,
                 [(q_pre, "r"), (k_pre, "r"), (p, ("c", pe_block)), (cosf, "r"), (s1, "r"), (s2, "r"), (gq, "f"), (gk, "f")],
                 [((s, hw), MM_DTYPE, "r"), ((s, hw), MM_DTYPE, "r"), ((s // tm, hw, tm), MM_DTYPE, "b")])


def _qk_post_bwd(q_pre, k_pre, p, pe_block, tabs, gq, gk, dq, dk):
    s, hw = q_pre.shape
    cosf, s1, s2 = tabs

    def fn(i, n, q_ref, k_ref, pe_ref, c_ref, s1_ref, s2_ref, gq_ref, gk_ref, dq_ref, dk_ref,
           dqo_ref, dko_ref, dpe_ref, dgq_ref, dgk_ref):
        c, a1, a2, pe = c_ref[...], s1_ref[...], s2_ref[...], pe_ref[...]
        lane = lax.broadcasted_iota(jnp.int32, pe.shape, 1)
        is_pe = (lane >= QK_NOPE) & (lane < QK_DIM)
        dpe = jnp.zeros(pe.shape, F32)
        dgq = jnp.zeros((1, HEAD_PAD), F32)
        dgk = jnp.zeros((1, HEAD_PAD), F32)
        for hh in range(N_HEADS):
            cols = slice(hh * HEAD_PAD, (hh + 1) * HEAD_PAD)
            dxq, g1 = _rms_bwd(q_ref[:, cols], gq_ref[...], _rope_bwd(dq_ref[cols, :].T, c, a1, a2), QK_DIM)
            dqo_ref[:, cols] = dxq.astype(dqo_ref.dtype)
            dxk, g2 = _rms_bwd(k_ref[:, cols] + pe, gk_ref[...], _rope_bwd(dk_ref[:, cols], c, a1, a2), QK_DIM)
            dko_ref[:, cols] = dxk.astype(dko_ref.dtype)
            dpe = dpe + dxk
            dgq = dgq + g1
            dgk = dgk + g2
        dpe_ref[...] = jnp.where(is_pe, dpe, 0.0)
        _acc(dgq_ref, i, dgq)
        _acc(dgk_ref, i, dgk)

    return _rows("qk_post_bwd", fn, s, _tile(s, 256),
                 [(q_pre, "r"), (k_pre, "r"), (p, ("c", pe_block)), (cosf, "r"), (s1, "r"), (s2, "r"), (gq, "f"), (gk, "f"),
                  (dq, "b"), (dk, "r")],
                 [((s, hw), MM_DTYPE, "r"), ((s, hw), MM_DTYPE, "r"), ((s, HEAD_PAD), F32, "r"),
                  ((1, HEAD_PAD), F32, "a"), ((1, HEAD_PAD), F32, "a")])


def _chunk_mask(tq, key_rows):
    r = jnp.right_shift(lax.broadcasted_iota(jnp.int32, (tq, tq), 0), CHUNK_SHIFT)
    c = jnp.right_shift(lax.broadcasted_iota(jnp.int32, (tq, tq), 1), CHUNK_SHIFT)
    return (r <= c) if key_rows else (c <= r)


def _attn_specs(s, tq):
    w = HEADS_PER_STEP * HEAD_PAD
    tile = pl.BlockSpec((tq, w), lambda hb, i: (i, hb))
    whole = pl.BlockSpec((s, w), lambda hb, i: (0, hb))
    rows = pl.BlockSpec((None, None, HEADS_PER_STEP, tq), lambda hb, i: (hb, i, 0, 0))
    blocks = pl.BlockSpec((s // tq, w, tq), lambda hb, i: (0, hb, 0))
    return tile, whole, rows, blocks


def _head(hh):
    return slice(hh * HEAD_PAD, (hh + 1) * HEAD_PAD)


def _attn_fwd(q, k, vt):
    s, hw = q.shape
    tq = _tile(s, 256)
    nq = s // tq
    groups = N_HEADS // HEADS_PER_STEP
    scale = QK_DIM ** -0.5
    tile, whole, rows, blocks = _attn_specs(s, tq)

    def body(q_ref, k_ref, vt_ref, o_ref, lse_ref):
        qi = pl.program_id(1)
        qv = [q_ref[:, _head(hh)] for hh in range(HEADS_PER_STEP)]

        def scores(kb):
            off = pl.multiple_of(kb * tq, tq)
            return tuple(lax.dot_general(k_ref[pl.ds(off, tq), _head(hh)], qv[hh], _DIMS["nt"],
                                         preferred_element_type=F32) for hh in range(HEADS_PER_STEP))

        def update(kb, stats, raw, diag):
            heads = range(HEADS_PER_STEP)
            mn, al, pr, ln = [], [], [], []
            for hh in heads:
                m, l, _ = stats[hh]
                sc = raw[hh] * (scale * LOG2E)
                if diag:
                    sc = jnp.where(_chunk_mask(tq, True), sc, -1e30)
                mn.append(jnp.maximum(m, jnp.max(sc, axis=0, keepdims=True)))
                al.append(jnp.exp2(m - mn[hh]))
                p = jnp.exp2(sc - mn[hh])
                ln.append(al[hh] * l + jnp.sum(p, axis=0, keepdims=True))
                pr.append(p.astype(MM_DTYPE))
            return tuple((mn[hh], ln[hh], al[hh] * stats[hh][2]
                          + jnp.dot(vt_ref[kb, _head(hh), :], pr[hh], preferred_element_type=F32)) for hh in heads)

        def step(kb, carry):
            stats, raw = carry
            nxt = scores(kb + 1)
            return update(kb, stats, raw, False), nxt

        init = tuple((jnp.full((1, tq), -1e30, F32), jnp.zeros((1, tq), F32), jnp.zeros((HEAD_PAD, tq), F32))
                     for _ in range(HEADS_PER_STEP))
        stats, raw = lax.fori_loop(0, qi, step, (init, scores(0)))
        carry = update(qi, stats, raw, True)
        for hh in range(HEADS_PER_STEP):
            m, l, acc = carry[hh]
            o_ref[:, _head(hh)] = (acc / l).T
            lse_ref[hh:hh + 1, :] = m + jnp.log(l) * LOG2E

    return pl.pallas_call(
        body, name="attn_fwd", grid=(groups, nq),
        in_specs=[tile, whole, blocks], out_specs=[tile, rows],
        out_shape=[jax.ShapeDtypeStruct((s, hw), F32), jax.ShapeDtypeStruct((groups, nq, HEADS_PER_STEP, tq), F32)],
        compiler_params=_params(2))(q, k, vt)


def _attn_delta(o, do):
    s, hw = o.shape
    tq = _tile(s, 256)
    groups = N_HEADS // HEADS_PER_STEP
    tile, _, rows, _ = _attn_specs(s, tq)

    def body(o_ref, do_ref, d_ref):
        for hh in range(HEADS_PER_STEP):
            prod = o_ref[:, _head(hh)] * do_ref[:, _head(hh)]
            d_ref[hh:hh + 1, :] = jnp.sum(prod.T, axis=0, keepdims=True)

    return pl.pallas_call(
        body, name="attn_delta", grid=(groups, s // tq), in_specs=[tile, tile], out_specs=rows,
        out_shape=jax.ShapeDtypeStruct((groups, s // tq, HEADS_PER_STEP, tq), F32),
        compiler_params=_params(2))(o, do)


def _attn_bwd(q, k, kt, v, do, lse_row, delta_row):
    s, hw = q.shape
    tq = _tile(s, 256)
    nq = s // tq
    groups = N_HEADS // HEADS_PER_STEP
    w = HEADS_PER_STEP * HEAD_PAD
    scale = QK_DIM ** -0.5
    tile, whole, _, blocks = _attn_specs(s, tq)
    rowvec = pl.BlockSpec((None, nq, HEADS_PER_STEP, tq), lambda hb, i: (hb, 0, 0, 0))
    kt_tile = pl.BlockSpec((None, w, tq), lambda hb, i: (i, hb, 0))

    def body(q_ref, k_ref, kt_ref, v_ref, do_ref, lse_ref, delta_ref, dqt_ref, dk_ref, dv_ref):
        ki = pl.program_id(1)
        kv = [k_ref[:, _head(hh)] for hh in range(HEADS_PER_STEP)]
        vv = [v_ref[:, _head(hh)] for hh in range(HEADS_PER_STEP)]
        ktv = [kt_ref[_head(hh), :] for hh in range(HEADS_PER_STEP)]

        @pl.when(ki == 0)
        def _():
            dqt_ref[...] = jnp.zeros(dqt_ref.shape, F32)

        def block(qb, carry, diag):
            off = pl.multiple_of(qb * tq, tq)
            heads = range(HEADS_PER_STEP)
            qs = [q_ref[pl.ds(off, tq), _head(hh)] for hh in heads]
            dos = [do_ref[pl.ds(off, tq), _head(hh)].astype(MM_DTYPE) for hh in heads]
            raw = [lax.dot_general(kv[hh], qs[hh], _DIMS["nt"], preferred_element_type=F32) for hh in heads]
            dpr = [lax.dot_general(vv[hh], dos[hh], _DIMS["nt"], preferred_element_type=F32) for hh in heads]
            pr, ds = [], []
            for hh in heads:
                sc = raw[hh] * (scale * LOG2E)
                if diag:
                    sc = jnp.where(_chunk_mask(tq, True), sc, -1e30)
                p = jnp.exp2(sc - lse_ref[qb, hh:hh + 1, :])
                pr.append(p.astype(MM_DTYPE))
                ds.append((p * (dpr[hh] - delta_ref[qb, hh:hh + 1, :]) * scale).astype(MM_DTYPE))
            out = []
            for hh in heads:
                dk, dvv = carry[hh]
                dvv = dvv + jnp.dot(pr[hh], dos[hh], preferred_element_type=F32)
                dk = dk + jnp.dot(ds[hh], qs[hh], preferred_element_type=F32)
                dqt_ref[qb, _head(hh), :] += jnp.dot(ktv[hh], ds[hh], preferred_element_type=F32)
                out.append((dk, dvv))
            return tuple(out)

        init = tuple((jnp.zeros((tq, HEAD_PAD), F32), jnp.zeros((tq, HEAD_PAD), F32)) for _ in range(HEADS_PER_STEP))
        carry = lax.fori_loop(ki + 1, nq, lambda qb, c: block(qb, c, False), block(ki, init, True))
        for hh in range(HEADS_PER_STEP):
            dk_ref[:, _head(hh)] = carry[hh][0]
            dv_ref[:, _head(hh)] = carry[hh][1].astype(dv_ref.dtype)

    return pl.pallas_call(
        body, name="attn_bwd", grid=(groups, nq),
        in_specs=[whole, tile, kt_tile, tile, whole, rowvec, rowvec], out_specs=[blocks, tile, tile],
        out_shape=[jax.ShapeDtypeStruct((nq, hw, tq), F32), jax.ShapeDtypeStruct((s, hw), F32),
                   jax.ShapeDtypeStruct((s, hw), MM_DTYPE)],
        compiler_params=_params(2))(q, k, kt, v, do, lse_row, delta_row)


def _glu_with_halo(i, p_ref, ph_ref, ubuf, cw, tm):
    a, g = p_ref[:, 0:cw], p_ref[:, cw:2 * cw]
    ubuf[HALO:HALO + tm, :] = a * _sigmoid(g)
    ah, gh = ph_ref[:, 0:cw], ph_ref[:, cw:2 * cw]
    ubuf[0:HALO, :] = jnp.where(i > 0, ah * _sigmoid(gh), 0.0)
    return a, g


def _conv_taps(ubuf, w_ref, tm):
    base = HALO - (CONV_K - 1)
    y = w_ref[0:1, :] * ubuf[base:base + tm, :]
    for k in range(1, CONV_K):
        y = y + w_ref[k:k + 1, :] * ubuf[base + k:base + k + tm, :]
    return y


def _layer_norm_stats(y, cw):
    mu = jnp.sum(y, axis=-1, keepdims=True) * (1.0 / cw)
    yc = y - mu
    r = lax.rsqrt(jnp.sum(yc * yc, axis=-1, keepdims=True) * (1.0 / cw) + EPS)
    return yc * r, r


def _conv_fwd(p, w, cb, lg, lb):
    s = p.shape[0]
    cw = cb.shape[-1]
    tm = _tile(s, 256)

    def fn(i, n, p_ref, ph_ref, w_ref, cb_ref, lg_ref, lb_ref, o_ref, ubuf):
        _glu_with_halo(i, p_ref, ph_ref, ubuf, cw, tm)
        yh, _ = _layer_norm_stats(_conv_taps(ubuf, w_ref, tm) + cb_ref[...], cw)
        z = yh * lg_ref[...] + lb_ref[...]
        o_ref[...] = (z * _sigmoid(z)).astype(o_ref.dtype)

    return _rows("conv_fwd", fn, s, tm, [(p, "r"), (p, "p"), (w, "f"), (cb, "f"), (lg, "f"), (lb, "f")],
                 [((s, cw), MM_DTYPE, "r")], scratch=[pltpu.VMEM((tm + HALO, cw), F32)])[0]


def _conv_bwd_ln(p, w, cb, lg, lb, dc):
    s = p.shape[0]
    cw = cb.shape[-1]
    tm = _tile(s, 256)

    def fn(i, n, p_ref, ph_ref, w_ref, cb_ref, lg_ref, lb_ref, dc_ref, dy_ref, dlg_ref, dlb_ref, dcb_ref, ubuf):
        _glu_with_halo(i, p_ref, ph_ref, ubuf, cw, tm)
        yh, r = _layer_norm_stats(_conv_taps(ubuf, w_ref, tm) + cb_ref[...], cw)
        z = yh * lg_ref[...] + lb_ref[...]
        sg = _sigmoid(z)
        dz = dc_ref[...] * (sg * (1.0 + z * (1.0 - sg)))
        dyh = dz * lg_ref[...]
        m1 = jnp.sum(dyh, axis=-1, keepdims=True) * (1.0 / cw)
        m2 = jnp.sum(dyh * yh, axis=-1, keepdims=True) * (1.0 / cw)
        dy = r * (dyh - m1 - yh * m2)
        dy_ref[...] = dy
        _acc(dlg_ref, i, _sum0(dz * yh))
        _acc(dlb_ref, i, _sum0(dz))
        _acc(dcb_ref, i, _sum0(dy))

    return _rows("conv_bwd_ln", fn, s, tm,
                 [(p, "r"), (p, "p"), (w, "f"), (cb, "f"), (lg, "f"), (lb, "f"), (dc, "r")],
                 [((s, cw), F32, "r"), ((1, cw), F32, "a"), ((1, cw), F32, "a"), ((1, cw), F32, "a")],
                 scratch=[pltpu.VMEM((tm + HALO, cw), F32)])


def _conv_bwd_taps(p, w, dy):
    s = p.shape[0]
    cw = w.shape[-1]
    tm = _tile(s, 256)

    def fn(i, n, p_ref, ph_ref, w_ref, dy_ref, dyn_ref, dp_ref, dw_ref, ubuf, dybuf):
        a, g = _glu_with_halo(i, p_ref, ph_ref, ubuf, cw, tm)
        dyv = dy_ref[...]
        dybuf[0:tm, :] = dyv
        dybuf[tm:tm + HALO, :] = jnp.where(i < n - 1, dyn_ref[...], 0.0)
        base = HALO - (CONV_K - 1)
        du = jnp.zeros((tm, cw), F32)

        @pl.when(i == 0)
        def _():
            dw_ref[...] = jnp.zeros(dw_ref.shape, F32)

        for k in range(CONV_K):
            sh = CONV_K - 1 - k
            du = du + w_ref[k:k + 1, :] * dybuf[sh:sh + tm, :]
            dw_ref[k:k + 1, :] += _sum0(dyv * ubuf[base + k:base + k + tm, :])
        sg = _sigmoid(g)
        dp_ref[:, 0:cw] = (du * sg).astype(dp_ref.dtype)
        dp_ref[:, cw:2 * cw] = (du * a * sg * (1.0 - sg)).astype(dp_ref.dtype)

    return _rows("conv_bwd_taps", fn, s, tm, [(p, "r"), (p, "p"), (w, "f"), (dy, "r"), (dy, "n")],
                 [((s, 2 * cw), MM_DTYPE, "r"), ((CONV_K + 1, cw), F32, "a")],
                 scratch=[pltpu.VMEM((tm + HALO, cw), F32), pltpu.VMEM((tm + HALO, cw), F32)])


def _dp_assemble(p, gql, gkvl, dcq, dckv_parts, dp_conv, dpe, cw):
    s, width = p.shape
    ql, kvl = gql.shape[-1], gkvl.shape[-1]
    o_q, o_kv, o_pe = 2 * cw, 2 * cw + ql, 2 * cw + ql + kvl

    def fn(i, n, p_ref, gq_ref, gkv_ref, dcq_ref, dk1_ref, dk2_ref, dconv_ref, dpe_ref, dp_ref, dgq_ref, dgkv_ref):
        dp_ref[:, 0:o_q] = dconv_ref[...]
        dx, dg = _rms_bwd(p_ref[:, o_q:o_kv], gq_ref[...], dcq_ref[...], ql)
        dp_ref[:, o_q:o_kv] = dx.astype(dp_ref.dtype)
        _acc(dgq_ref, i, dg)
        dx, dg = _rms_bwd(p_ref[:, o_kv:o_pe], gkv_ref[...], dk1_ref[...] + dk2_ref[...], kvl)
        dp_ref[:, o_kv:o_pe] = dx.astype(dp_ref.dtype)
        _acc(dgkv_ref, i, dg)
        dp_ref[:, o_pe:width] = dpe_ref[...].astype(dp_ref.dtype)

    return _rows("dp_assemble", fn, s, _tile(s, 256),
                 [(p, "r"), (gql, "f"), (gkvl, "f"), (dcq, "r"), (dckv_parts[0], "r"), (dckv_parts[1], "r"),
                  (dp_conv, "r"), (dpe, "r")],
                 [((s, width), MM_DTYPE, "r"), ((1, ql), F32, "a"), ((1, kvl), F32, "a")])


def _add_leading(name, x, out_dtype):
    n, r, c = x.shape
    tm = _tile(r, 256)

    def fn(i, nt, x_ref, o_ref):
        tot = x_ref[0].astype(F32)
        for k in range(1, n):
            tot = tot + x_ref[k].astype(F32)
        o_ref[...] = tot.astype(o_ref.dtype)

    return _rows(name, fn, r, tm, [(x, "r")], [((r, c), out_dtype, "r")])[0]


def _prefetch_call(name, fn, idx, grid, in_specs, out_spec, out_shape, args, aliases=None):
    return pl.pallas_call(
        fn, name=name, out_shape=out_shape,
        grid_spec=pltpu.PrefetchScalarGridSpec(num_scalar_prefetch=1, grid=grid, in_specs=in_specs, out_specs=out_spec),
        input_output_aliases=aliases or {}, compiler_params=_params(len(grid)))(idx, *args)


def _adamw(w, g, m, v):
    r, c = w.shape
    tm = _tile(r, 256)

    def fn(i, n, w_ref, g_ref, m_ref, v_ref, d_ref, mo_ref, vo_ref):
        gv = g_ref[...]
        mn = ADAM_B1 * m_ref[...] + (1.0 - ADAM_B1) * gv
        vn = ADAM_B2 * v_ref[...] + (1.0 - ADAM_B2) * (gv * gv)
        mo_ref[...] = mn
        vo_ref[...] = vn
        m_hat = mn / (1.0 - ADAM_B1 ** ADAM_STEP)
        v_hat = vn / (1.0 - ADAM_B2 ** ADAM_STEP)
        d_ref[...] = -ADAM_LR * (m_hat / (jnp.sqrt(v_hat) + ADAM_EPS) + ADAM_WD * w_ref[...])

    return _rows("adamw", fn, r, tm, [(w, "r"), (g, "r"), (m, "r"), (v, "r")], [((r, c), F32, "r")] * 3)


def _place():
    x, y, c = lax.axis_index("x"), lax.axis_index("y"), lax.axis_index("c")
    other_chips = [(1 - x, y), (x, 1 - y), (1 - x, 1 - y)]
    return x, y, c, 2 * x + y, other_chips


def _gather_small(vec):
    flips = [(fx, fy, fc) for fx in (0, 1) for fy in (0, 1) for fc in (0, 1)][1:]

    def body(src, dst, send, recv, lsem):
        x, y, c, _, _ = _place()
        me = 4 * x + 2 * y + c
        local = pltpu.make_async_copy(src, dst.at[me], lsem)
        local.start()
        away = []
        for j, (fx, fy, fc) in enumerate(flips):
            px, py, pc = x ^ fx, y ^ fy, c ^ fc
            away.append(pltpu.make_async_remote_copy(
                src_ref=src, dst_ref=dst.at[me], send_sem=send.at[j], recv_sem=recv.at[j],
                device_id=(px, py, pc), device_id_type=MESH))
        for cp in away:
            cp.start()
        for j, (fx, fy, fc) in enumerate(flips):
            blk = dst.at[4 * (x ^ fx) + 2 * (y ^ fy) + (c ^ fc)]
            pltpu.make_async_remote_copy(
                src_ref=blk, dst_ref=blk, send_sem=send.at[j], recv_sem=recv.at[j],
                device_id=(x ^ fx, y ^ fy, c ^ fc), device_id_type=MESH).wait_recv()
        for cp in away:
            cp.wait_send()
        local.wait()

    return pl.pallas_call(
        body, name="gather_small", in_specs=[ANY], out_specs=ANY,
        out_shape=jax.ShapeDtypeStruct((8,) + vec.shape, vec.dtype),
        scratch_shapes=[pltpu.SemaphoreType.DMA((7,)), pltpu.SemaphoreType.DMA((7,)), pltpu.SemaphoreType.DMA(())],
        compiler_params=pltpu.CompilerParams(has_side_effects=True))(vec)


HBM = pl.BlockSpec(memory_space=pltpu.HBM)
SEM = pl.BlockSpec(memory_space=pltpu.SEMAPHORE)
EFFECT = pltpu.SideEffectType.DATAFLOW_SIDE_EFFECTING


def _exchange(name, mode, arrays, sem_shape, plan, sems=None, after=None):
    n = len(arrays)
    hbm = [pltpu.HBM(a.shape, a.dtype) for a in arrays]
    arrays = [pltpu.with_memory_space_constraint(a, pltpu.HBM) for a in arrays]
    if mode == "run":
        def body(*refs):
            outs, send, recv = refs[n:2 * n], refs[2 * n], refs[2 * n + 1]
            sends, recvs = plan(outs, send, recv)
            for cp in sends:
                cp.start()
            for cp in recvs:
                cp.wait_recv()
            for cp in sends:
                cp.wait_send()

        return list(pl.pallas_call(
            body, name=name, in_specs=[HBM] * n, out_specs=[HBM] * n, out_shape=hbm,
            input_output_aliases={i: i for i in range(n)},
            scratch_shapes=[pltpu.SemaphoreType.DMA(sem_shape), pltpu.SemaphoreType.DMA(sem_shape)],
            compiler_params=pltpu.CompilerParams(has_side_effects=EFFECT))(*arrays))
    if mode == "start":
        def body(*refs):
            send, recv, token, outs = refs[n], refs[n + 1], refs[n + 2], refs[n + 3:]
            sends, _ = plan(outs, send, recv)
            for cp in sends:
                cp.start()
            token[...] = jnp.zeros(token.shape, F32)

        res = pl.pallas_call(
            body, name=name, in_specs=[HBM] * n,
            out_specs=[SEM, SEM, pl.BlockSpec(memory_space=pltpu.VMEM)] + [HBM] * n,
            out_shape=[pltpu.SemaphoreType.DMA(sem_shape), pltpu.SemaphoreType.DMA(sem_shape),
                       jax.ShapeDtypeStruct((8, HEAD_PAD), F32)] + hbm,
            input_output_aliases={i: i + 3 for i in range(n)},
            compiler_params=pltpu.CompilerParams(has_side_effects=EFFECT))(*arrays)
        return (res[0], res[1]), res[2], list(res[3:])

    def body(*refs):
        ins, send, recv = refs[:n], refs[n], refs[n + 1]
        sends, recvs = plan(ins, send, recv)
        for cp in sends:
            cp.wait_send()
        for cp in recvs:
            cp.wait_recv()

    return list(pl.pallas_call(
        body, name=name, in_specs=[HBM] * n + [SEM, SEM, ANY], out_specs=[HBM] * n, out_shape=hbm,
        input_output_aliases={i: i for i in range(n)},
        compiler_params=pltpu.CompilerParams(has_side_effects=EFFECT))(*arrays, sems[0], sems[1], after))


def _rows_half(ref_rows, half):
    h = ref_rows // 2
    return pl.ds(pl.multiple_of(half * h, 8), h)


def _remote(src, dst, send_sem, recv_sem, device):
    return pltpu.make_async_remote_copy(src_ref=src, dst_ref=dst, send_sem=send_sem, recv_sem=recv_sem,
                                        device_id=device, device_id_type=MESH)


def _plan_gather_ici(k_n, l):
    def plan(refs, send, recv):
        x, y, c, me, chips = _place()
        sends, recvs = [], []
        for k in range(k_n):
            src, buf = refs[k], refs[k_n + k]
            rows = _rows_half(src.shape[1], c)
            for j, (cx, cy) in enumerate(chips):
                sends.append(_remote(src.at[l, rows], buf.at[0, me, rows], send.at[3 * k + j], recv.at[3 * k + j], (cx, cy, c)))
                got = buf.at[0, 2 * cx + cy, rows]
                recvs.append(_remote(got, got, send.at[3 * k + j], recv.at[3 * k + j], (cx, cy, c)))
        return sends, recvs
    return plan


def _plan_gather_d2d(k_n):
    def plan(refs, send, recv):
        x, y, c, me, chips = _place()
        sends, recvs = [], []
        for k in range(k_n):
            buf = refs[k]
            for j, (cx, cy) in enumerate(chips):
                mine = buf.at[0, 2 * cx + cy, _rows_half(buf.shape[2], c)]
                theirs = buf.at[0, 2 * cx + cy, _rows_half(buf.shape[2], 1 - c)]
                sends.append(_remote(mine, mine, send.at[3 * k + j], recv.at[3 * k + j], (x, y, 1 - c)))
                recvs.append(_remote(theirs, theirs, send.at[3 * k + j], recv.at[3 * k + j], (x, y, 1 - c)))
        return sends, recvs
    return plan


def _plan_swap(k_n):
    def plan(refs, send, recv):
        x, y, c, me, chips = _place()
        sends, recvs = [], []
        for k in range(k_n):
            src, land = refs[k], refs[k_n + k]
            sends.append(_remote(src.at[:, _rows_half(src.shape[1], 1 - c)], land, send.at[k], recv.at[k], (x, y, 1 - c)))
            recvs.append(_remote(land, land, send.at[k], recv.at[k], (x, y, 1 - c)))
        return sends, recvs
    return plan


def _plan_scatter(k_n):
    def plan(refs, send, recv):
        x, y, c, me, chips = _place()
        sends, recvs = [], []
        for k in range(k_n):
            src, land = refs[k], refs[k_n + k]
            for j, (cx, cy) in enumerate(chips):
                sends.append(_remote(src.at[2 * cx + cy], land.at[j], send.at[3 * k + j], recv.at[3 * k + j], (cx, cy, c)))
                recvs.append(_remote(land.at[j], land.at[j], send.at[3 * k + j], recv.at[3 * k + j], (cx, cy, c)))
        return sends, recvs
    return plan


def _plan_share(k_n, layers):
    def plan(refs, send, recv):
        x, y, c, me, chips = _place()
        sends, recvs = [], []
        for k in range(k_n):
            out = refs[k]
            for l in range(layers):
                mine = out.at[l, _rows_half(out.shape[1], c)]
                theirs = out.at[l, _rows_half(out.shape[1], 1 - c)]
                sends.append(_remote(mine, mine, send.at[layers * k + l], recv.at[layers * k + l], (x, y, 1 - c)))
                recvs.append(_remote(theirs, theirs, send.at[layers * k + l], recv.at[layers * k + l], (x, y, 1 - c)))
        return sends, recvs
    return plan


def _add_sibling_half(idx, mine, theirs):
    q, r, c = mine.shape
    h = r // 2
    tm = _tile(h, 256)

    def fn(idx_ref, a_ref, b_ref, o_ref):
        o_ref[...] = (a_ref[...].astype(F32) + b_ref[...].astype(F32)).astype(o_ref.dtype)

    return _prefetch_call(
        "add_sibling", fn, idx, (q, h // tm),
        [pl.BlockSpec((None, None, tm, c), lambda b, i, ix: (b, ix[0], i, 0)),
         pl.BlockSpec((None, tm, c), lambda b, i, ix: (b, i, 0))],
        pl.BlockSpec((None, tm, c), lambda b, i, ix: (b, i, 0)), jax.ShapeDtypeStruct((q, h, c), WIRE_DTYPE),
        [mine.reshape(q, 2, h, c), theirs])


def _add_chips_half(idx, own, others, layers, l, into):
    _, h, c = own.shape
    tm = _tile(h, 256)

    def fn(*refs):
        a_ref, b_ref, o_ref = refs[-3:]
        tot = a_ref[...].astype(F32)
        for k in range(3):
            tot = tot + b_ref[k].astype(F32)
        o_ref[...] = tot

    in_specs = [pl.BlockSpec((None, tm, c), lambda i, ix: (ix[1], i, 0)), pl.BlockSpec((3, tm, c), lambda i, ix: (0, i, 0))]
    args = [own, others]
    aliases = None
    if into is not None:
        in_specs = [ANY] + in_specs
        args = [into.reshape(layers, 2, h, c)] + args
        aliases = {1: 0}
    out = _prefetch_call(
        "add_chips", fn, idx, (h // tm,), in_specs,
        pl.BlockSpec((None, None, tm, c), lambda i, ix: (l, ix[0], i, 0)),
        jax.ShapeDtypeStruct((layers, 2, h, c), F32), args, aliases)
    return out.reshape(layers, 2 * h, c)


SHARDED = ["ffn1_w_gate", "ffn1_w_up", "ffn1_w_down", "w_in", "w_uq", "w_ukv", "conv_w", "w_out",
           "ffn2_w_gate", "ffn2_w_up", "ffn2_w_down"]
SMALL = ["ffn1_norm", "mix_norm", "q_latent_norm", "kv_latent_norm", "q_norm", "k_norm", "conv_b", "conv_ln_g",
         "conv_ln_b", "ffn2_norm", "post_norm"]
ALL_WEIGHTS = ["ffn1_norm", "ffn1_w_gate", "ffn1_w_up", "ffn1_w_down", "mix_norm", "w_in", "q_latent_norm", "w_uq",
               "kv_latent_norm", "w_ukv", "q_norm", "k_norm", "conv_w", "conv_b", "conv_ln_g", "conv_ln_b", "w_out",
               "ffn2_norm", "ffn2_w_gate", "ffn2_w_up", "ffn2_w_down", "post_norm"]


def _rope_tables(s):
    pos = jnp.arange(s, dtype=F32)
    inv_freq = 1.0 / (ROPE_THETA ** (jnp.arange(0, QK_ROPE, 2, dtype=F32) / QK_ROPE))
    ang = pos[:, None] * inv_freq[None, :]
    cos, sin = jnp.cos(ang), jnp.sin(ang)
    half = QK_ROPE // 2
    one = jnp.ones((s, QK_NOPE), F32)
    zero = jnp.zeros((s, QK_NOPE), F32)
    pad = HEAD_PAD - QK_DIM
    z16 = jnp.zeros((s, half), F32)
    cosf = jnp.concatenate([one, cos, cos, jnp.ones((s, pad), F32)], axis=1)
    s1 = jnp.concatenate([zero, -sin, z16, jnp.zeros((s, pad), F32)], axis=1)
    s2 = jnp.concatenate([zero, z16, sin, jnp.zeros((s, pad), F32)], axis=1)
    return cosf, s1, s2


def _ffn_fwd(n, w, names, l):
    a, b, hm = _ffn_up_fused(n, w[names[0]], w[names[1]], l)
    return a, b, hm, _blocked_sum("ffn_down", [hm], [w[names[2]]], l, "nn")


def _ffn_bwd(nt, a, b, hm, do, w, names, grads):
    gate, up, down = names
    da, db = _ffn_down_t_fused(do, w[down], a, b, 0)
    grads[down] = _wgrad("ffn_down_wgrad", hm, do, WIRE_DTYPE)
    g_gate, g_up = _wgrad_pair("ffn_up_wgrad", nt, da, db, 1, 0, None)
    grads[gate], grads[up] = g_gate[0], g_up[0]
    return [_blocked_sum("ffn_up_t", [da, db], [w[gate], w[up]], 0, "nt")]


def _vec(v, l):
    return v[l][None, :]


def _pad_lanes(v, width):
    return jnp.pad(v, [(0, 0)] * (v.ndim - 1) + [(0, width - v.shape[-1])])


def kernel(x, ffn1_norm, ffn1_w_gate, ffn1_w_up, ffn1_w_down, mix_norm, w_in, q_latent_norm, w_uq, kv_latent_norm, w_ukv, q_norm, k_norm, conv_w, conv_b, conv_ln_g, conv_ln_b, w_out, ffn2_norm, ffn2_w_gate, ffn2_w_up, ffn2_w_down, post_norm, loss_target, m_ffn1_norm, m_ffn1_w_gate, m_ffn1_w_up, m_ffn1_w_down, m_mix_norm, m_w_in, m_q_latent_norm, m_w_uq, m_kv_latent_norm, m_w_ukv, m_q_norm, m_k_norm, m_conv_w, m_conv_b, m_conv_ln_g, m_conv_ln_b, m_w_out, m_ffn2_norm, m_ffn2_w_gate, m_ffn2_w_up, m_ffn2_w_down, m_post_norm, v_ffn1_norm, v_ffn1_w_gate, v_ffn1_w_up, v_ffn1_w_down, v_mix_norm, v_w_in, v_q_latent_norm, v_w_uq, v_kv_latent_norm, v_w_ukv, v_q_norm, v_k_norm, v_conv_w, v_conv_b, v_conv_ln_g, v_conv_ln_b, v_w_out, v_ffn2_norm, v_ffn2_w_gate, v_ffn2_w_up, v_ffn2_w_down, v_post_norm):
    given = dict(locals())
    wts = {n: given[n] for n in ALL_WEIGHTS}
    mom = {n: given["m_" + n] for n in ALL_WEIGHTS}
    var = {n: given["v_" + n] for n in ALL_WEIGHTS}
    n_layers = ffn1_norm.shape[0]
    s, d = x.shape[1], x.shape[2]
    ql, kvl, cw = q_latent_norm.shape[-1], kv_latent_norm.shape[-1], conv_b.shape[-1]
    xs = x.reshape(s, d)
    target = loss_target.reshape(s, d)

    assert n_layers == 2, "the layers' exchanges run one behind the other"
    local = [jnp.pad(wts[n], ((0, 0), (0, 1), (0, 0))).astype(MM_DTYPE) if n == "conv_w" else wts[n].astype(MM_DTYPE)
             for n in SHARDED]
    L = n_layers
    k_n = len(SHARDED)
    my_c = lax.axis_index("c")
    my_chip = 2 * lax.axis_index("x") + lax.axis_index("y")
    place = jnp.stack([my_c, my_chip]).astype(jnp.int32)
    hw = N_HEADS * HEAD_PAD
    tabs = _rope_tables(s)
    pe_block = (2 * cw + ql + kvl) // HEAD_PAD
    gq_pad, gk_pad = _pad_lanes(q_norm, HEAD_PAD), _pad_lanes(k_norm, HEAD_PAD)
    tq = _tile(s, 256)

    def own_blocks(l):
        return [lax.dynamic_update_slice(lax.empty((1, N_CHIPS) + t.shape[1:], t.dtype), t[l][None, None],
                                         (0, my_chip, 0, 0)) for t in local]

    def layouts(w):
        win = w["w_in"].transpose(0, 2, 1, 3).reshape(1, d, -1)
        zc = lambda n: jnp.zeros((1, d, n), MM_DTYPE)
        o_pe, o_cv = ql + kvl, ql + kvl + QK_ROPE
        w["w_in"] = jnp.concatenate([win[..., o_cv:], win[..., :o_pe], zc(QK_NOPE), win[..., o_pe:o_cv],
                                     zc(HEAD_PAD - QK_DIM)], axis=-1)[:, None]
        wq = w["w_uq"].transpose(0, 2, 1, 3).reshape(1, ql, N_HEADS, QK_DIM)
        w["wq"] = _pad_lanes(wq, HEAD_PAD).reshape(1, 1, ql, hw)
        wkv = w["w_ukv"].transpose(0, 2, 1, 3).reshape(1, kvl, N_HEADS, QK_NOPE + V_DIM)
        w["wk"] = _pad_lanes(wkv[..., :QK_NOPE], HEAD_PAD).reshape(1, 1, kvl, hw)
        w["wv"] = _pad_lanes(wkv[..., QK_NOPE:], HEAD_PAD).reshape(1, 1, kvl, hw)
        w["wvt"] = w["wv"][:, 0].transpose(0, 2, 1)
        w["conv"] = w["conv_w"].transpose(0, 2, 1, 3).reshape(CONV_K + 1, cw).astype(F32)
        wout = w["w_out"].reshape(1, d, d)
        wo_a = wout[:, :N_HEADS * V_DIM].reshape(1, N_HEADS, V_DIM, d)
        w["wo_a"] = jnp.pad(wo_a, ((0, 0), (0, 0), (0, HEAD_PAD - V_DIM), (0, 0))).reshape(1, 1, hw, d)
        w["wo_c"] = wout[:, N_HEADS * V_DIM:][:, None]
        return w

    ici, d2d, sem_kj = (lambda l: _plan_gather_ici(k_n, l)), _plan_gather_d2d(k_n), (3 * k_n,)
    n_a = 3
    n_b = k_n - n_a
    own0 = own_blocks(0)
    landed = _exchange("gather0a_ici", "run", local[:n_a] + own0[:n_a], (3 * n_a,), _plan_gather_ici(n_a, 0))
    w0 = dict(zip(SHARDED[:n_a], _exchange("gather0a_d2d", "run", landed[n_a:], (3 * n_a,), _plan_gather_d2d(n_a))))
    local = landed[:n_a] + local[n_a:]
    sems_r, token_r, rest = _exchange("gather0b_ici_start", "start", local[n_a:] + own0[n_a:], (3 * n_b,),
                                      _plan_gather_ici(n_b, 0))

    def fwd_ffn(l, cur, w, tok):
        sv = {"x": cur}
        n1, sv["n1t"] = _norm_fwd(cur, _vec(ffn1_norm, l) + tok)
        sv["a1"], sv["b1"], sv["hm1"], o1 = _ffn_fwd(n1, w, SHARDED[0:3], 0)
        sv["x1"], sv["h"] = _resid_norm(cur, [o1], 0.5, _vec(mix_norm, l), MM_DTYPE)
        return sv

    def fwd_first(l, sv, w, tok):
        sv["p"] = _proj("w_in", sv["h"], w["w_in"], 0, "nn")[0]
        sv["cq"], sv["ckv"], ckvt = _mla_pre(sv["p"], _vec(q_latent_norm, l) + tok, _vec(kv_latent_norm, l), cw)
        sv["q_pre"] = _proj("wq", sv["cq"], w["wq"], 0, "nn")[0]
        sv["k_pre"] = _proj("wk", sv["ckv"], w["wk"], 0, "nn")[0]
        sv["v"] = _proj("wv", sv["ckv"], w["wv"], 0, "nn", MM_DTYPE)[0]
        sv["vt"] = _mm("wv_transposed", w["wvt"], ckvt, "nn", (s // tq,),
                       pl.BlockSpec((None, hw, kvl), lambda i: (0, 0, 0)), pl.BlockSpec((kvl, tq), lambda i: (0, i)),
                       pl.BlockSpec((None, hw, tq), lambda i: (i, 0, 0)),
                       jax.ShapeDtypeStruct((s // tq, hw, tq), MM_DTYPE), 1, None)
        sv["q"], sv["k"], sv["kt"] = _qk_post(sv["q_pre"], sv["k_pre"], sv["p"], pe_block, tabs,
                                              _vec(gq_pad, l), _vec(gk_pad, l))
        return sv

    def fwd_attn(l, sv, w):
        sv["c"] = _conv_fwd(sv["p"], w["conv"], _vec(conv_b, l), _vec(conv_ln_g, l), _vec(conv_ln_b, l))
        sv["o"], sv["lse"] = _attn_fwd(sv["q"], sv["k"], sv.pop("vt"))
        return sv

    def fwd_second(l, sv, w, tok):
        mix_a = _contract("wo_a", sv["o"][None], w["wo_a"], 0, "nn")
        mix_c = _contract("wo_c", sv["c"][None], w["wo_c"], 0, "nn")
        sv["x2"], n2, sv["n2t"] = _resid_norm(sv["x1"], [mix_a, mix_c], 1.0, _vec(ffn2_norm, l) + tok, MM_DTYPE, True)
        sv["a2"], sv["b2"], sv["hm2"], o2 = _ffn_fwd(n2, w, SHARDED[8:11], 0)
        sv["x3"], out = _final_fwd(sv["x2"], o2, _vec(post_norm, l))
        return out

    sv0 = fwd_ffn(0, xs, w0, token_r[0, 0])
    rest = _exchange("gather0b_ici_wait", "wait", rest, (3 * n_b,), _plan_gather_ici(n_b, 0), sems=sems_r, after=sv0["h"])
    w0.update(zip(SHARDED[n_a:], _exchange("gather0b_d2d", "run", rest[n_b:], (3 * n_b,), _plan_gather_d2d(n_b))))
    w0 = layouts(w0)
    local = local[:n_a] + rest[:n_b]
    sems_a, token_a, in_flight = _exchange("gather1_ici_start", "start", local + own_blocks(1), sem_kj, ici(1))
    sv0 = fwd_attn(0, fwd_first(0, sv0, w0, token_a[0, 0]), w0)
    in_flight = _exchange("gather1_ici_wait", "wait", in_flight, sem_kj, ici(1), sems=sems_a, after=sv0["o"])
    sems_b, token_b, in_flight = _exchange("gather1_d2d_start", "start", in_flight[k_n:], sem_kj, d2d)
    cur = fwd_second(0, sv0, w0, token_b[0, 0])
    w1 = layouts(dict(zip(SHARDED, _exchange("gather1_d2d_wait", "wait", in_flight, sem_kj, d2d, sems=sems_b, after=cur))))
    sv1 = fwd_attn(1, fwd_first(1, fwd_ffn(1, cur, w1, 0.0), w1, 0.0), w1)
    cur = fwd_second(1, sv1, w1, 0.0)
    dy, loss_part = _loss(cur, target)

    small = {n: [None] * L for n in SMALL}

    def bwd_first(l, sv, dy, w, tok, grads):
        dx3, do2, small["post_norm"][l] = _norm_bwd(sv["x3"], _vec(post_norm, l) + tok, [dy], None, 0.5)
        dn2 = _ffn_bwd(sv["n2t"], sv["a2"], sv["b2"], sv["hm2"], do2, w, SHARDED[8:11], grads)
        dx2, dx2b, small["ffn2_norm"][l] = _norm_bwd(sv["x2"], _vec(ffn2_norm, l), dn2, dx3, 1.0)
        do = _proj("wo_a_t", dx2b, w["wo_a"], 0, "nt")[0]
        dc = _proj("wo_c_t", dx2b, w["wo_c"], 0, "nt")[0]
        g_wo_a = _wgrad("wo_a_wgrad", sv["o"], dx2b[None], WIRE_DTYPE)[0].reshape(N_HEADS, HEAD_PAD, d)[:, :V_DIM]
        g_wo_c = _wgrad("wo_c_wgrad", sv["c"], dx2b[None], WIRE_DTYPE)
        grads["w_out"] = jnp.concatenate(
            [g_wo_a.reshape(N_HEADS * V_DIM, d), g_wo_c[0]], axis=0).reshape(N_CHIPS, d // N_CHIPS, d)
        dq, dk, dv = _attn_bwd(sv["q"], sv["k"], sv["kt"], sv["v"], do, sv["lse"], _attn_delta(sv["o"], do))
        dq_pre, dk_pre, dpe, small["q_norm"][l], small["k_norm"][l] = _qk_post_bwd(
            sv["q_pre"], sv["k_pre"], sv["p"], pe_block, tabs, _vec(gq_pad, l), _vec(gk_pad, l), dq, dk)
        return dict(dx2=dx2, dc=dc, dq_pre=dq_pre, dk_pre=dk_pre, dv=dv, dpe=dpe)

    def bwd_second(l, sv, st, w, tok, grads):
        dq_pre, dk_pre, dv = st["dq_pre"], st["dk_pre"], st["dv"]
        dcq = _contract("wq_t", dq_pre[None], w["wq"], 0, "nt")
        dckv = [_contract("wk_t", dk_pre[None], w["wk"], 0, "nt"), _contract("wv_t", dv[None], w["wv"], 0, "nt")]
        per_head = lambda t, rows: t[0].reshape(rows, N_HEADS, HEAD_PAD)
        g_wq = per_head(_wgrad("wq_wgrad", sv["cq"], dq_pre[None], WIRE_DTYPE), ql)[..., :QK_DIM]
        grads["w_uq"] = g_wq.reshape(ql, N_CHIPS, -1).transpose(1, 0, 2)
        g_wkv = jnp.concatenate([per_head(_wgrad("wk_wgrad", sv["ckv"], dk_pre[None], WIRE_DTYPE), kvl)[..., :QK_NOPE],
                                 per_head(_wgrad("wv_wgrad", sv["ckv"], dv[None], WIRE_DTYPE), kvl)[..., :V_DIM]], axis=-1)
        grads["w_ukv"] = g_wkv.reshape(kvl, N_CHIPS, -1).transpose(1, 0, 2)
        dyc, small["conv_ln_g"][l], small["conv_ln_b"][l], small["conv_b"][l] = _conv_bwd_ln(
            sv["p"], w["conv"], _vec(conv_b, l) + tok, _vec(conv_ln_g, l), _vec(conv_ln_b, l), st["dc"])
        dp_conv, g_conv = _conv_bwd_taps(sv["p"], w["conv"], dyc)
        grads["conv_w"] = g_conv.reshape(CONV_K + 1, N_CHIPS, cw // N_CHIPS).transpose(1, 0, 2).astype(WIRE_DTYPE)
        dp, small["q_latent_norm"][l], small["kv_latent_norm"][l] = _dp_assemble(
            sv["p"], _vec(q_latent_norm, l), _vec(kv_latent_norm, l), dcq, dckv, dp_conv, st["dpe"], cw)
        dh = _contract("w_in_t", dp[None], w["w_in"], 0, "nt")
        g_win = _wgrad("w_in_wgrad", sv["h"], dp[None], WIRE_DTYPE)[0]
        o_q, o_pe_col = 2 * cw, 2 * cw + ql + kvl + QK_NOPE
        g_win = jnp.concatenate([g_win[:, o_q:o_q + ql + kvl], g_win[:, o_pe_col:o_pe_col + QK_ROPE], g_win[:, :o_q]], axis=1)
        grads["w_in"] = g_win.reshape(d, N_CHIPS, -1).transpose(1, 0, 2)
        dx1, do1, small["mix_norm"][l] = _norm_bwd(sv["x1"], _vec(mix_norm, l), [dh], st["dx2"], 0.5)
        dn1 = _ffn_bwd(sv["n1t"], sv["a1"], sv["b1"], sv["hm1"], do1, w, SHARDED[0:3], grads)
        dx, _, small["ffn1_norm"][l] = _norm_bwd(sv["x"], _vec(ffn1_norm, l), dn1, dx1, 1.0)
        return dx

    def halves(parts):
        return [lax.empty((N_CHIPS, t.shape[1] // 2, t.shape[2]), t.dtype) for t in parts]

    def thirds(parts):
        return [lax.empty((N_CHIPS - 1,) + t.shape[1:], t.dtype) for t in parts]

    def chip_sums(done):
        n = len(done) // 2
        return [_add_sibling_half(place, g, t) for g, t in zip(done[:n], done[n:])]

    def finish(done, l, into):
        n = len(done) // 2
        return [_add_chips_half(place, own, others, L, l, t) for own, others, t in zip(done[:n], done[n:], into)]

    swap, scatter, sem_k = _plan_swap(k_n), _plan_scatter(k_n), (k_n,)
    grads1, grads0 = {}, {}
    st1 = bwd_first(1, sv1, dy, w1, 0.0, grads1)
    dy = bwd_second(1, sv1, st1, w1, 0.0, grads1)
    parts1 = [grads1[n] for n in SHARDED]
    sems_s, token_s, in_flight = _exchange("swap1_start", "start", parts1 + halves(parts1), sem_k, swap)
    st0 = bwd_first(0, sv0, dy, w0, token_s[0, 0], grads0)
    sums1 = chip_sums(_exchange("swap1_wait", "wait", in_flight, sem_k, swap, sems=sems_s, after=st0["dq_pre"]))
    sems_c, token_c, in_flight = _exchange("scatter1_start", "start", sums1 + thirds(sums1), sem_kj, scatter)
    early = [n for n in SHARDED if n in grads0]
    late = [n for n in SHARDED if n not in grads0]
    n_e, n_l = len(early), len(late)
    parts_e = [grads0[n] for n in early]
    sums_e = chip_sums(_exchange("swap0_early", "run", parts_e + halves(parts_e), (n_e,), _plan_swap(n_e)))
    sems_e, token_e, early_flight = _exchange("scatter0_early_start", "start", sums_e + thirds(sums_e), (3 * n_e,),
                                              _plan_scatter(n_e))
    dy = bwd_second(0, sv0, st0, w0, token_c[0, 0] + token_e[0, 0], grads0)
    grad_x = dy.reshape(x.shape)
    out = dict(zip(SHARDED, finish(_exchange("scatter1_wait", "wait", in_flight, sem_kj, scatter, sems=sems_c, after=dy),
                                   1, [None] * k_n)))
    done = _exchange("scatter0_early_wait", "wait", early_flight, (3 * n_e,), _plan_scatter(n_e), sems=sems_e, after=dy)
    out.update(zip(early, finish(done, 0, [out[n] for n in early])))
    parts_l = [grads0[n] for n in late]
    sums_l = chip_sums(_exchange("swap0_late", "run", parts_l + halves(parts_l), (n_l,), _plan_swap(n_l)))
    done = _exchange("scatter0_late", "run", sums_l + thirds(sums_l), (3 * n_l,), _plan_scatter(n_l))
    out.update(zip(late, finish(done, 0, [out[n] for n in late])))
    full = dict(zip(SHARDED, _exchange("share", "run", [out[n] for n in SHARDED], (L * k_n,), _plan_share(k_n, L))))
    full["conv_w"] = full["conv_w"][:, :CONV_K]

    width = max(wts[n].shape[-1] for n in SMALL)
    pack = lambda vals: jnp.concatenate([_pad_lanes(v.reshape(-1, v.shape[-1]), width) for v in vals], axis=0)
    part = pack([jnp.concatenate(small[n], axis=0)[..., :wts[n].shape[-1]] for n in SMALL])
    rows = part.shape[0]
    part = jnp.pad(part, [(0, -rows % 8), (0, 0)])
    small_sum = _add_leading("add_devices", _gather_small(part), F32)
    for i, n in enumerate(SMALL):
        full[n] = small_sum[i * L:(i + 1) * L, :wts[n].shape[-1]]

    delta_w, new_m, new_v = {}, {}, {}
    for n in SHARDED:
        shape = wts[n].shape
        swap = shape[-1] % HEAD_PAD != 0 and shape[-2] % HEAD_PAD == 0
        view = (lambda t: jnp.swapaxes(t, 1, 2)) if swap else (lambda t: t)
        flat = lambda t: view(t).reshape(-1, view(t).shape[-1])
        r = flat(wts[n]).shape[0]
        fix = (lambda t: jnp.pad(flat(t), [(0, -r % 8), (0, 0)])) if r % 8 else flat
        outs = _adamw(fix(wts[n]), fix(full[n]), fix(mom[n]), fix(var[n]))
        delta_w[n], new_m[n], new_v[n] = [view(o[:r].reshape(view(wts[n]).shape)) for o in outs]
    pk = lambda src: jnp.pad(pack([src[n] for n in SMALL]), [(0, -rows % 8), (0, 0)])
    outs = _adamw(pk(wts), small_sum, pk(mom), pk(var))
    for i, n in enumerate(SMALL):
        delta_w[n], new_m[n], new_v[n] = [o[i * L:(i + 1) * L, :wts[n].shape[-1]] for o in outs]

    loss = lax.psum(loss_part[0, 0], ("x", "y", "c"))
    return (loss, grad_x, *[full[n] for n in ALL_WEIGHTS], *[delta_w[n] for n in ALL_WEIGHTS],
            *[new_m[n] for n in ALL_WEIGHTS], *[new_v[n] for n in ALL_WEIGHTS])
```

```python
import jax
import jax.numpy as jnp
from jax import lax
from jax.experimental import pallas as pl
from jax.experimental.pallas import tpu as pltpu

F32 = jnp.float32
MM_DTYPE = jnp.bfloat16
WIRE_DTYPE = jnp.bfloat16
EPS = 1e-6
N_HEADS = 8
QK_NOPE = 64
QK_ROPE = 32
QK_DIM = QK_NOPE + QK_ROPE
V_DIM = 64
HEAD_PAD = 128
HEADS_PER_STEP = 4
CHUNK_SHIFT = 6
LOG2E = 1.4426950408889634
CONV_K = 31
HALO = 32
ROPE_THETA = 10000.0
N_CHIPS = 4
ADAM_LR, ADAM_B1, ADAM_B2, ADAM_EPS, ADAM_WD, ADAM_STEP = 0.001, 0.9, 0.999, 1e-08, 0.01, 10
VMEM_LIMIT_BYTES = 56 * 2 ** 20
WGRAD_ROWS = 2048
MESH = pl.DeviceIdType.MESH
ANY = pl.BlockSpec(memory_space=pl.ANY)

_DIMS = {
    "nn": (((1,), (0,)), ((), ())),
    "nt": (((1,), (1,)), ((), ())),
    "tn": (((0,), (0,)), ((), ())),
}


def _params(n_axes):
    return pltpu.CompilerParams(dimension_semantics=("arbitrary",) * n_axes, vmem_limit_bytes=VMEM_LIMIT_BYTES)


def _tile(n, cap):
    for step in (16, 8):
        for t in range(cap - cap % step, 0, -step):
            if n % t == 0:
                return t
    return n


def _mm(name, a, b, dims, grid, a_spec, b_spec, o_spec, out_shape, nk, acc_shape, into=None):
    n_axes = len(grid)

    def body(*refs):
        if into is not None:
            refs = refs[1:]
        a_ref, b_ref, o_ref = refs[:3]
        part = lax.dot_general(a_ref[...].astype(MM_DTYPE), b_ref[...].astype(MM_DTYPE), _DIMS[dims],
                               preferred_element_type=F32)
        if nk == 1:
            o_ref[...] = part.astype(o_ref.dtype)
        else:
            acc_ref = refs[3]
            k = pl.program_id(n_axes - 1)

            @pl.when(k == 0)
            def _():
                acc_ref[...] = part

            @pl.when(k > 0)
            def _():
                acc_ref[...] += part

            @pl.when(k == nk - 1)
            def _():
                o_ref[...] = acc_ref[...].astype(o_ref.dtype)

    in_specs = [a_spec, b_spec]
    args = [a, b]
    aliases = {}
    if into is not None:
        in_specs = [ANY] + in_specs
        args = [into] + args
        aliases = {0: 0}
    return pl.pallas_call(
        body, name=name, grid=grid, in_specs=in_specs, out_specs=o_spec, out_shape=out_shape,
        scratch_shapes=[] if nk == 1 else [pltpu.VMEM(acc_shape, F32)],
        input_output_aliases=aliases, compiler_params=_params(n_axes))(*args)


def _proj(name, a, w, l, dims, out_dtype=F32):
    s, k = a.shape
    g = w.shape[1]
    n = w.shape[3] if dims == "nn" else w.shape[2]
    tm = _tile(s, 512)
    return _mm(name, a, w, dims, (g, s // tm),
               pl.BlockSpec((tm, k), lambda j, i: (i, 0)),
               pl.BlockSpec((None, None) + w.shape[2:], lambda j, i: (l, j, 0, 0)),
               pl.BlockSpec((None, tm, n), lambda j, i: (j, i, 0)),
               jax.ShapeDtypeStruct((g, s, n), out_dtype), 1, None)


def _contract(name, a, w, l, dims, out_dtype=F32):
    g, s, kb = a.shape
    n = w.shape[3] if dims == "nn" else w.shape[2]
    tm = _tile(s, 512)
    return _mm(name, a, w, dims, (s // tm, g),
               pl.BlockSpec((None, tm, kb), lambda i, j: (j, i, 0)),
               pl.BlockSpec((None, None) + w.shape[2:], lambda i, j: (l, j, 0, 0)),
               pl.BlockSpec((tm, n), lambda i, j: (i, 0)),
               jax.ShapeDtypeStruct((s, n), out_dtype), g, (tm, n))


def _wgrad(name, a, b, out_dtype, layers=None, l=None, into=None):
    g = a.shape[0] if a.ndim == 3 else b.shape[0]
    s, k = a.shape[-2:]
    n = b.shape[-1]
    tm = _tile(s, WGRAD_ROWS)

    def spec(x, w):
        if x.ndim == 3:
            return pl.BlockSpec((None, tm, w), lambda j, i: (j, i, 0))
        return pl.BlockSpec((tm, w), lambda j, i: (i, 0))

    if layers is None:
        o_spec = pl.BlockSpec((None, k, n), lambda j, i: (j, 0, 0))
        out_shape = jax.ShapeDtypeStruct((g, k, n), out_dtype)
    else:
        o_spec = pl.BlockSpec((None, None, k, n), lambda j, i: (l, j, 0, 0))
        out_shape = jax.ShapeDtypeStruct((layers, g, k, n), out_dtype)
    return _mm(name, a, b, "tn", (g, s // tm), spec(a, k), spec(b, n), o_spec, out_shape, s // tm, (k, n), into=into)


def _ffn_up_fused(n, wg, wu, l):
    s, k = n.shape
    g, n4 = wg.shape[1], wg.shape[3]
    tm = _tile(s, 512)

    def body(n_ref, wg_ref, wu_ref, a_ref, b_ref, h_ref):
        nv = n_ref[...]
        a = jnp.dot(nv, wg_ref[...], preferred_element_type=F32)
        b = jnp.dot(nv, wu_ref[...], preferred_element_type=F32)
        a_ref[...] = a.astype(a_ref.dtype)
        b_ref[...] = b.astype(b_ref.dtype)
        h_ref[...] = (a * _sigmoid(a) * b).astype(h_ref.dtype)

    w_spec = pl.BlockSpec((None, None, k, n4), lambda j, i: (l, j, 0, 0))
    o_spec = pl.BlockSpec((None, tm, n4), lambda j, i: (j, i, 0))
    return pl.pallas_call(
        body, name="ffn_up_fused", grid=(g, s // tm),
        in_specs=[pl.BlockSpec((tm, k), lambda j, i: (i, 0)), w_spec, w_spec], out_specs=[o_spec] * 3,
        out_shape=[jax.ShapeDtypeStruct((g, s, n4), MM_DTYPE)] * 3, compiler_params=_params(2))(n, wg, wu)


def _ffn_down_t_fused(do, wd, a, b, l):
    s, k = do.shape
    g, n4 = wd.shape[1], wd.shape[2]
    tm = _tile(s, 512)

    def body(do_ref, wd_ref, a_ref, b_ref, da_ref, db_ref):
        dh = lax.dot_general(do_ref[...], wd_ref[...], _DIMS["nt"], preferred_element_type=F32)
        av = a_ref[...].astype(F32)
        sg = _sigmoid(av)
        da_ref[...] = (dh * b_ref[...].astype(F32) * (sg * (1.0 + av * (1.0 - sg)))).astype(da_ref.dtype)
        db_ref[...] = (dh * (av * sg)).astype(db_ref.dtype)

    t_spec = pl.BlockSpec((None, tm, n4), lambda j, i: (j, i, 0))
    return pl.pallas_call(
        body, name="ffn_down_t_fused", grid=(g, s // tm),
        in_specs=[pl.BlockSpec((tm, k), lambda j, i: (i, 0)),
                  pl.BlockSpec((None, None, n4, k), lambda j, i: (l, j, 0, 0)), t_spec, t_spec],
        out_specs=[t_spec] * 2, out_shape=[jax.ShapeDtypeStruct((g, s, n4), MM_DTYPE)] * 2,
        compiler_params=_params(2))(do, wd, a, b)


def _blocked_sum(name, acts, weights, l, dims):
    g, s, kb = acts[0].shape
    n = weights[0].shape[3] if dims == "nn" else weights[0].shape[2]
    tm = _tile(s, 512)
    n_pairs = len(acts)

    def body(*refs):
        o_ref = refs[2 * n_pairs]
        acc = None
        for p in range(n_pairs):
            for j in range(g):
                t = lax.dot_general(refs[p][j], refs[n_pairs + p][j], _DIMS[dims], preferred_element_type=F32)
                acc = t if acc is None else acc + t
        o_ref[...] = acc

    return pl.pallas_call(
        body, name=name, grid=(s // tm,),
        in_specs=[pl.BlockSpec((g, tm, kb), lambda i: (0, i, 0))] * n_pairs
        + [pl.BlockSpec((None,) + w.shape[1:], lambda i: (l, 0, 0, 0)) for w in weights],
        out_specs=pl.BlockSpec((tm, n), lambda i: (i, 0)), out_shape=jax.ShapeDtypeStruct((s, n), F32),
        compiler_params=_params(1))(*acts, *weights)


def _wgrad_pair(name, nt, da, db, layers, l, into):
    k, s = nt.shape
    g, _, n4 = da.shape
    tm = _tile(s, WGRAD_ROWS)
    nk = s // tm

    def body(*refs):
        if into is not None:
            refs = refs[2:]
        nt_ref, da_ref, db_ref, og_ref, ou_ref, accg, accu = refs
        i = pl.program_id(1)
        ntv = nt_ref[...]
        pg = jnp.dot(ntv, da_ref[...], preferred_element_type=F32)
        pu = jnp.dot(ntv, db_ref[...], preferred_element_type=F32)

        @pl.when(i == 0)
        def _():
            accg[...] = pg
            accu[...] = pu

        @pl.when(i > 0)
        def _():
            accg[...] += pg
            accu[...] += pu

        @pl.when(i == nk - 1)
        def _():
            og_ref[...] = accg[...].astype(og_ref.dtype)
            ou_ref[...] = accu[...].astype(ou_ref.dtype)

    t_spec = pl.BlockSpec((None, tm, n4), lambda j, i: (j, i, 0))
    o_spec = pl.BlockSpec((None, None, k, n4), lambda j, i: (l, j, 0, 0))
    in_specs = [pl.BlockSpec((k, tm), lambda j, i: (0, i)), t_spec, t_spec]
    args = [nt, da, db]
    aliases = {}
    if into is not None:
        in_specs = [ANY, ANY] + in_specs
        args = list(into) + args
        aliases = {0: 0, 1: 1}
    return pl.pallas_call(
        body, name=name, grid=(g, nk), in_specs=in_specs, out_specs=[o_spec] * 2,
        out_shape=[jax.ShapeDtypeStruct((layers, g, k, n4), WIRE_DTYPE)] * 2,
        scratch_shapes=[pltpu.VMEM((k, n4), F32)] * 2, input_output_aliases=aliases,
        compiler_params=_params(2))(*args)


def _rows(name, fn, s, tm, ins, outs, scratch=()):
    n = s // tm

    def spec(shape, kind):
        nd = len(shape)
        if kind == "r":
            return pl.BlockSpec(shape[:-2] + (tm, shape[-1]), lambda i: (0,) * (nd - 2) + (i, 0))
        if kind in ("f", "a"):
            return pl.BlockSpec(shape, lambda i: (0,) * nd)
        if kind == "t":
            return pl.BlockSpec((shape[0], tm), lambda i: (0, i))
        if kind == "b":
            return pl.BlockSpec((None,) + shape[1:], lambda i: (i, 0, 0))
        if kind == "p":
            return pl.BlockSpec((HALO, shape[-1]), lambda i: (jnp.maximum(i * (tm // HALO) - 1, 0), 0))
        if kind == "n":
            return pl.BlockSpec((HALO, shape[-1]), lambda i: (jnp.minimum((i + 1) * (tm // HALO), s // HALO - 1), 0))
        j = kind[1]
        return pl.BlockSpec((tm, HEAD_PAD), lambda i: (i, j))

    def body(*refs):
        fn(pl.program_id(0), n, *refs)

    return pl.pallas_call(
        body, name=name, grid=(n,),
        in_specs=[spec(a.shape, kind) for a, kind in ins],
        out_specs=[spec(shape, kind) for shape, _, kind in outs],
        out_shape=[jax.ShapeDtypeStruct(shape, dtype) for shape, dtype, _ in outs],
        scratch_shapes=list(scratch), compiler_params=_params(1))(*[a for a, _ in ins])


def _acc(ref, i, val):
    @pl.when(i == 0)
    def _():
        ref[...] = val

    @pl.when(i > 0)
    def _():
        ref[...] += val


def _sum0(x):
    return jnp.sum(x, axis=0, keepdims=True)


def _rstd(x, n):
    return lax.rsqrt(jnp.sum(x * x, axis=-1, keepdims=True) * (1.0 / n) + EPS)


def _rms_bwd(x, g, dy, n):
    r = _rstd(x, n)
    xh = x * r
    dyg = dy * g
    dx = r * (dyg - xh * (jnp.sum(dyg * xh, axis=-1, keepdims=True) * (1.0 / n)))
    return dx, _sum0(dy * xh)


def _sigmoid(x):
    return 1.0 / (1.0 + jnp.exp(-x))


def _norm_fwd(x, g):
    s, d = x.shape

    def fn(i, n, x_ref, g_ref, o_ref, ot_ref):
        xv = x_ref[...]
        y = xv * _rstd(xv, d) * g_ref[...]
        o_ref[...] = y.astype(o_ref.dtype)
        ot_ref[...] = y.T.astype(ot_ref.dtype)

    return _rows("norm_fwd", fn, s, _tile(s, 256), [(x, "r"), (g, "f")],
                 [((s, d), MM_DTYPE, "r"), ((d, s), MM_DTYPE, "t")])


def _resid_norm(x, adds, scale, g, norm_dtype, transposed=False):
    s, d = x.shape
    k = len(adds)

    def fn(i, n, *refs):
        x_ref, add_refs, g_ref, xo_ref, no_ref = refs[0], refs[1:1 + k], refs[1 + k], refs[2 + k], refs[3 + k]
        tot = add_refs[0][...]
        for r in add_refs[1:]:
            tot = tot + r[...]
        xv = x_ref[...] + scale * tot
        xo_ref[...] = xv
        y = xv * _rstd(xv, d) * g_ref[...]
        no_ref[...] = y.astype(no_ref.dtype)
        if transposed:
            refs[4 + k][...] = y.T.astype(refs[4 + k].dtype)

    return _rows("resid_norm", fn, s, _tile(s, 256), [(x, "r")] + [(a, "r") for a in adds] + [(g, "f")],
                 [((s, d), F32, "r"), ((s, d), norm_dtype, "r")] + ([((d, s), MM_DTYPE, "t")] if transposed else []))


def _norm_bwd(x, g, dn_parts, dres, out_scale):
    s, d = x.shape
    k = len(dn_parts)
    has_res = dres is not None

    def fn(i, n, *refs):
        x_ref, g_ref = refs[0], refs[1]
        dn_refs = refs[2:2 + k]
        pos = 2 + k
        dn = dn_refs[0][...]
        for r in dn_refs[1:]:
            dn = dn + r[...]
        dx, dg = _rms_bwd(x_ref[...], g_ref[...], dn, d)
        if has_res:
            dx = dx + refs[pos][...]
            pos += 1
        dx_ref, dxs_ref, dg_ref = refs[pos:pos + 3]
        dx_ref[...] = dx
        dxs_ref[...] = (out_scale * dx).astype(dxs_ref.dtype)
        _acc(dg_ref, i, dg)

    ins = [(x, "r"), (g, "f")] + [(p, "r") for p in dn_parts] + ([(dres, "r")] if has_res else [])
    return _rows("norm_bwd", fn, s, _tile(s, 256), ins,
                 [((s, d), F32, "r"), ((s, d), MM_DTYPE, "r"), ((1, d), F32, "a")])


def _final_fwd(x, o, g):
    return _resid_norm(x, [o], 0.5, g, F32)


def _loss(y, t):
    s, d = y.shape

    def fn(i, n, y_ref, t_ref, dy_ref, l_ref):
        e = y_ref[...] - t_ref[...]
        dy_ref[...] = e * (1.0 / d)
        _acc(l_ref, i, (0.5 / d) * jnp.sum(jnp.sum(e * e, axis=-1, keepdims=True), axis=0, keepdims=True))

    return _rows("loss", fn, s, _tile(s, 256), [(y, "r"), (t, "r")], [((s, d), F32, "r"), ((1, 1), F32, "a")])


def _mla_pre(p, gql, gkvl, cw):
    s = p.shape[0]
    ql, kvl = gql.shape[-1], gkvl.shape[-1]
    o_q, o_kv = 2 * cw, 2 * cw + ql

    def fn(i, n, p_ref, gq_ref, gkv_ref, cq_ref, ckv_ref, ckvt_ref):
        cq = p_ref[:, o_q:o_q + ql]
        cq_ref[...] = (cq * _rstd(cq, ql) * gq_ref[...]).astype(cq_ref.dtype)
        ckv = p_ref[:, o_kv:o_kv + kvl]
        y = ckv * _rstd(ckv, kvl) * gkv_ref[...]
        ckv_ref[...] = y.astype(ckv_ref.dtype)
        ckvt_ref[...] = y.T.astype(ckvt_ref.dtype)

    return _rows("mla_pre", fn, s, _tile(s, 256), [(p, "r"), (gql, "f"), (gkvl, "f")],
                 [((s, ql), MM_DTYPE, "r"), ((s, kvl), MM_DTYPE, "r"), ((kvl, s), MM_DTYPE, "t")])


def _rope(x, cosf, s1, s2):
    return x * cosf + pltpu.roll(x, HEAD_PAD - 16, 1) * s1 + pltpu.roll(x, 16, 1) * s2


def _rope_bwd(d, cosf, s1, s2):
    return d * cosf + pltpu.roll(d * s1, 16, 1) + pltpu.roll(d * s2, HEAD_PAD - 16, 1)


def _qk_post(q_pre, k_pre, p, pe_block, tabs, gq, gk):
    s, hw = q_pre.shape
    cosf, s1, s2 = tabs

    tm = _tile(s, 256)

    def fn(i, n, q_ref, k_ref, pe_ref, c_ref, s1_ref, s2_ref, gq_ref, gk_ref, qo_ref, ko_ref, kt_ref):
        c, a1, a2, pe = c_ref[...], s1_ref[...], s2_ref[...], pe_ref[...]
        for hh in range(N_HEADS):
            cols = slice(hh * HEAD_PAD, (hh + 1) * HEAD_PAD)
            qv = q_ref[:, cols]
            qo_ref[:, cols] = _rope(qv * _rstd(qv, QK_DIM) * gq_ref[...], c, a1, a2).astype(qo_ref.dtype)
            kv = k_ref[:, cols] + pe
            kout = _rope(kv * _rstd(kv, QK_DIM) * gk_ref[...], c, a1, a2)
            ko_ref[:, cols] = kout.astype(ko_ref.dtype)
            kt_ref[cols, :] = kout.T.astype(kt_ref.dtype)

    return _rows("qk_post", fn, s, tm,
                 [(q_pre, "r"), (k_pre, "r"), (p, ("c", pe_block)), (cosf, "r"), (s1, "r"), (s2, "r"), (gq, "f"), (gk, "f")],
                 [((s, hw), MM_DTYPE, "r"), ((s, hw), MM_DTYPE, "r"), ((s // tm, hw, tm), MM_DTYPE, "b")])


def _qk_post_bwd(q_pre, k_pre, p, pe_block, tabs, gq, gk, dq, dk):
    s, hw = q_pre.shape
    cosf, s1, s2 = tabs

    def fn(i, n, q_ref, k_ref, pe_ref, c_ref, s1_ref, s2_ref, gq_ref, gk_ref, dq_ref, dk_ref,
           dqo_ref, dko_ref, dpe_ref, dgq_ref, dgk_ref):
        c, a1, a2, pe = c_ref[...], s1_ref[...], s2_ref[...], pe_ref[...]
        lane = lax.broadcasted_iota(jnp.int32, pe.shape, 1)
        is_pe = (lane >= QK_NOPE) & (lane < QK_DIM)
        dpe = jnp.zeros(pe.shape, F32)
        dgq = jnp.zeros((1, HEAD_PAD), F32)
        dgk = jnp.zeros((1, HEAD_PAD), F32)
        for hh in range(N_HEADS):
            cols = slice(hh * HEAD_PAD, (hh + 1) * HEAD_PAD)
            dxq, g1 = _rms_bwd(q_ref[:, cols], gq_ref[...], _rope_bwd(dq_ref[cols, :].T, c, a1, a2), QK_DIM)
            dqo_ref[:, cols] = dxq.astype(dqo_ref.dtype)
            dxk, g2 = _rms_bwd(k_ref[:, cols] + pe, gk_ref[...], _rope_bwd(dk_ref[:, cols], c, a1, a2), QK_DIM)
            dko_ref[:, cols] = dxk.astype(dko_ref.dtype)
            dpe = dpe + dxk
            dgq = dgq + g1
            dgk = dgk + g2
        dpe_ref[...] = jnp.where(is_pe, dpe, 0.0)
        _acc(dgq_ref, i, dgq)
        _acc(dgk_ref, i, dgk)

    return _rows("qk_post_bwd", fn, s, _tile(s, 256),
                 [(q_pre, "r"), (k_pre, "r"), (p, ("c", pe_block)), (cosf, "r"), (s1, "r"), (s2, "r"), (gq, "f"), (gk, "f"),
                  (dq, "b"), (dk, "r")],
                 [((s, hw), MM_DTYPE, "r"), ((s, hw), MM_DTYPE, "r"), ((s, HEAD_PAD), F32, "r"),
                  ((1, HEAD_PAD), F32, "a"), ((1, HEAD_PAD), F32, "a")])


def _chunk_mask(tq, key_rows):
    r = jnp.right_shift(lax.broadcasted_iota(jnp.int32, (tq, tq), 0), CHUNK_SHIFT)
    c = jnp.right_shift(lax.broadcasted_iota(jnp.int32, (tq, tq), 1), CHUNK_SHIFT)
    return (r <= c) if key_rows else (c <= r)


def _attn_specs(s, tq):
    w = HEADS_PER_STEP * HEAD_PAD
    tile = pl.BlockSpec((tq, w), lambda hb, i: (i, hb))
    whole = pl.BlockSpec((s, w), lambda hb, i: (0, hb))
    rows = pl.BlockSpec((None, None, HEADS_PER_STEP, tq), lambda hb, i: (hb, i, 0, 0))
    blocks = pl.BlockSpec((s // tq, w, tq), lambda hb, i: (0, hb, 0))
    return tile, whole, rows, blocks


def _head(hh):
    return slice(hh * HEAD_PAD, (hh + 1) * HEAD_PAD)


def _attn_fwd(q, k, vt):
    s, hw = q.shape
    tq = _tile(s, 256)
    nq = s // tq
    groups = N_HEADS // HEADS_PER_STEP
    scale = QK_DIM ** -0.5
    tile, whole, rows, blocks = _attn_specs(s, tq)

    def body(q_ref, k_ref, vt_ref, o_ref, lse_ref):
        qi = pl.program_id(1)
        qv = [q_ref[:, _head(hh)] for hh in range(HEADS_PER_STEP)]

        def scores(kb):
            off = pl.multiple_of(kb * tq, tq)
            return tuple(lax.dot_general(k_ref[pl.ds(off, tq), _head(hh)], qv[hh], _DIMS["nt"],
                                         preferred_element_type=F32) for hh in range(HEADS_PER_STEP))

        def update(kb, stats, raw, diag):
            heads = range(HEADS_PER_STEP)
            mn, al, pr, ln = [], [], [], []
            for hh in heads:
                m, l, _ = stats[hh]
                sc = raw[hh] * (scale * LOG2E)
                if diag:
                    sc = jnp.where(_chunk_mask(tq, True), sc, -1e30)
                mn.append(jnp.maximum(m, jnp.max(sc, axis=0, keepdims=True)))
                al.append(jnp.exp2(m - mn[hh]))
                p = jnp.exp2(sc - mn[hh])
                ln.append(al[hh] * l + jnp.sum(p, axis=0, keepdims=True))
                pr.append(p.astype(MM_DTYPE))
            return tuple((mn[hh], ln[hh], al[hh] * stats[hh][2]
                          + jnp.dot(vt_ref[kb, _head(hh), :], pr[hh], preferred_element_type=F32)) for hh in heads)

        def step(kb, carry):
            stats, raw = carry
            nxt = scores(kb + 1)
            return update(kb, stats, raw, False), nxt

        init = tuple((jnp.full((1, tq), -1e30, F32), jnp.zeros((1, tq), F32), jnp.zeros((HEAD_PAD, tq), F32))
                     for _ in range(HEADS_PER_STEP))
        stats, raw = lax.fori_loop(0, qi, step, (init, scores(0)))
        carry = update(qi, stats, raw, True)
        for hh in range(HEADS_PER_STEP):
            m, l, acc = carry[hh]
            o_ref[:, _head(hh)] = (acc / l).T
            lse_ref[hh:hh + 1, :] = m + jnp.log(l) * LOG2E

    return pl.pallas_call(
        body, name="attn_fwd", grid=(groups, nq),
        in_specs=[tile, whole, blocks], out_specs=[tile, rows],
        out_shape=[jax.ShapeDtypeStruct((s, hw), F32), jax.ShapeDtypeStruct((groups, nq, HEADS_PER_STEP, tq), F32)],
        compiler_params=_params(2))(q, k, vt)


def _attn_delta(o, do):
    s, hw = o.shape
    tq = _tile(s, 256)
    groups = N_HEADS // HEADS_PER_STEP
    tile, _, rows, _ = _attn_specs(s, tq)

    def body(o_ref, do_ref, d_ref):
        for hh in range(HEADS_PER_STEP):
            prod = o_ref[:, _head(hh)] * do_ref[:, _head(hh)]
            d_ref[hh:hh + 1, :] = jnp.sum(prod.T, axis=0, keepdims=True)

    return pl.pallas_call(
        body, name="attn_delta", grid=(groups, s // tq), in_specs=[tile, tile], out_specs=rows,
        out_shape=jax.ShapeDtypeStruct((groups, s // tq, HEADS_PER_STEP, tq), F32),
        compiler_params=_params(2))(o, do)


def _attn_bwd(q, k, kt, v, do, lse_row, delta_row):
    s, hw = q.shape
    tq = _tile(s, 256)
    nq = s // tq
    groups = N_HEADS // HEADS_PER_STEP
    w = HEADS_PER_STEP * HEAD_PAD
    scale = QK_DIM ** -0.5
    tile, whole, _, blocks = _attn_specs(s, tq)
    rowvec = pl.BlockSpec((None, nq, HEADS_PER_STEP, tq), lambda hb, i: (hb, 0, 0, 0))
    kt_tile = pl.BlockSpec((None, w, tq), lambda hb, i: (i, hb, 0))

    def body(q_ref, k_ref, kt_ref, v_ref, do_ref, lse_ref, delta_ref, dqt_ref, dk_ref, dv_ref):
        ki = pl.program_id(1)
        kv = [k_ref[:, _head(hh)] for hh in range(HEADS_PER_STEP)]
        vv = [v_ref[:, _head(hh)] for hh in range(HEADS_PER_STEP)]
        ktv = [kt_ref[_head(hh), :] for hh in range(HEADS_PER_STEP)]

        @pl.when(ki == 0)
        def _():
            dqt_ref[...] = jnp.zeros(dqt_ref.shape, F32)

        def block(qb, carry, diag):
            off = pl.multiple_of(qb * tq, tq)
            heads = range(HEADS_PER_STEP)
            qs = [q_ref[pl.ds(off, tq), _head(hh)] for hh in heads]
            dos = [do_ref[pl.ds(off, tq), _head(hh)].astype(MM_DTYPE) for hh in heads]
            raw = [lax.dot_general(kv[hh], qs[hh], _DIMS["nt"], preferred_element_type=F32) for hh in heads]
            dpr = [lax.dot_general(vv[hh], dos[hh], _DIMS["nt"], preferred_element_type=F32) for hh in heads]
            pr, ds = [], []
            for hh in heads:
                sc = raw[hh] * (scale * LOG2E)
                if diag:
                    sc = jnp.where(_chunk_mask(tq, True), sc, -1e30)
                p = jnp.exp2(sc - lse_ref[qb, hh:hh + 1, :])
                pr.append(p.astype(MM_DTYPE))
                ds.append((p * (dpr[hh] - delta_ref[qb, hh:hh + 1, :]) * scale).astype(MM_DTYPE))
            out = []
            for hh in heads:
                dk, dvv = carry[hh]
                dvv = dvv + jnp.dot(pr[hh], dos[hh], preferred_element_type=F32)
                dk = dk + jnp.dot(ds[hh], qs[hh], preferred_element_type=F32)
                dqt_ref[qb, _head(hh), :] += jnp.dot(ktv[hh], ds[hh], preferred_element_type=F32)
                out.append((dk, dvv))
            return tuple(out)

        init = tuple((jnp.zeros((tq, HEAD_PAD), F32), jnp.zeros((tq, HEAD_PAD), F32)) for _ in range(HEADS_PER_STEP))
        carry = lax.fori_loop(ki + 1, nq, lambda qb, c: block(qb, c, False), block(ki, init, True))
        for hh in range(HEADS_PER_STEP):
            dk_ref[:, _head(hh)] = carry[hh][0]
            dv_ref[:, _head(hh)] = carry[hh][1].astype(dv_ref.dtype)

    return pl.pallas_call(
        body, name="attn_bwd", grid=(groups, nq),
        in_specs=[whole, tile, kt_tile, tile, whole, rowvec, rowvec], out_specs=[blocks, tile, tile],
        out_shape=[jax.ShapeDtypeStruct((nq, hw, tq), F32), jax.ShapeDtypeStruct((s, hw), F32),
                   jax.ShapeDtypeStruct((s, hw), MM_DTYPE)],
        compiler_params=_params(2))(q, k, kt, v, do, lse_row, delta_row)


def _glu_with_halo(i, p_ref, ph_ref, ubuf, cw, tm):
    a, g = p_ref[:, 0:cw], p_ref[:, cw:2 * cw]
    ubuf[HALO:HALO + tm, :] = a * _sigmoid(g)
    ah, gh = ph_ref[:, 0:cw], ph_ref[:, cw:2 * cw]
    ubuf[0:HALO, :] = jnp.where(i > 0, ah * _sigmoid(gh), 0.0)
    return a, g


def _conv_taps(ubuf, w_ref, tm):
    base = HALO - (CONV_K - 1)
    y = w_ref[0:1, :] * ubuf[base:base + tm, :]
    for k in range(1, CONV_K):
        y = y + w_ref[k:k + 1, :] * ubuf[base + k:base + k + tm, :]
    return y


def _layer_norm_stats(y, cw):
    mu = jnp.sum(y, axis=-1, keepdims=True) * (1.0 / cw)
    yc = y - mu
    r = lax.rsqrt(jnp.sum(yc * yc, axis=-1, keepdims=True) * (1.0 / cw) + EPS)
    return yc * r, r


def _conv_fwd(p, w, cb, lg, lb):
    s = p.shape[0]
    cw = cb.shape[-1]
    tm = _tile(s, 256)

    def fn(i, n, p_ref, ph_ref, w_ref, cb_ref, lg_ref, lb_ref, o_ref, ubuf):
        _glu_with_halo(i, p_ref, ph_ref, ubuf, cw, tm)
        yh, _ = _layer_norm_stats(_conv_taps(ubuf, w_ref, tm) + cb_ref[...], cw)
        z = yh * lg_ref[...] + lb_ref[...]
        o_ref[...] = (z * _sigmoid(z)).astype(o_ref.dtype)

    return _rows("conv_fwd", fn, s, tm, [(p, "r"), (p, "p"), (w, "f"), (cb, "f"), (lg, "f"), (lb, "f")],
                 [((s, cw), MM_DTYPE, "r")], scratch=[pltpu.VMEM((tm + HALO, cw), F32)])[0]


def _conv_bwd_ln(p, w, cb, lg, lb, dc):
    s = p.shape[0]
    cw = cb.shape[-1]
    tm = _tile(s, 256)

    def fn(i, n, p_ref, ph_ref, w_ref, cb_ref, lg_ref, lb_ref, dc_ref, dy_ref, dlg_ref, dlb_ref, dcb_ref, ubuf):
        _glu_with_halo(i, p_ref, ph_ref, ubuf, cw, tm)
        yh, r = _layer_norm_stats(_conv_taps(ubuf, w_ref, tm) + cb_ref[...], cw)
        z = yh * lg_ref[...] + lb_ref[...]
        sg = _sigmoid(z)
        dz = dc_ref[...] * (sg * (1.0 + z * (1.0 - sg)))
        dyh = dz * lg_ref[...]
        m1 = jnp.sum(dyh, axis=-1, keepdims=True) * (1.0 / cw)
        m2 = jnp.sum(dyh * yh, axis=-1, keepdims=True) * (1.0 / cw)
        dy = r * (dyh - m1 - yh * m2)
        dy_ref[...] = dy
        _acc(dlg_ref, i, _sum0(dz * yh))
        _acc(dlb_ref, i, _sum0(dz))
        _acc(dcb_ref, i, _sum0(dy))

    return _rows("conv_bwd_ln", fn, s, tm,
                 [(p, "r"), (p, "p"), (w, "f"), (cb, "f"), (lg, "f"), (lb, "f"), (dc, "r")],
                 [((s, cw), F32, "r"), ((1, cw), F32, "a"), ((1, cw), F32, "a"), ((1, cw), F32, "a")],
                 scratch=[pltpu.VMEM((tm + HALO, cw), F32)])


def _conv_bwd_taps(p, w, dy):
    s = p.shape[0]
    cw = w.shape[-1]
    tm = _tile(s, 256)

    def fn(i, n, p_ref, ph_ref, w_ref, dy_ref, dyn_ref, dp_ref, dw_ref, ubuf, dybuf):
        a, g = _glu_with_halo(i, p_ref, ph_ref, ubuf, cw, tm)
        dyv = dy_ref[...]
        dybuf[0:tm, :] = dyv
        dybuf[tm:tm + HALO, :] = jnp.where(i < n - 1, dyn_ref[...], 0.0)
        base = HALO - (CONV_K - 1)
        du = jnp.zeros((tm, cw), F32)

        @pl.when(i == 0)
        def _():
            dw_ref[...] = jnp.zeros(dw_ref.shape, F32)

        for k in range(CONV_K):
            sh = CONV_K - 1 - k
            du = du + w_ref[k:k + 1, :] * dybuf[sh:sh + tm, :]
            dw_ref[k:k + 1, :] += _sum0(dyv * ubuf[base + k:base + k + tm, :])
        sg = _sigmoid(g)
        dp_ref[:, 0:cw] = (du * sg).astype(dp_ref.dtype)
        dp_ref[:, cw:2 * cw] = (du * a * sg * (1.0 - sg)).astype(dp_ref.dtype)

    return _rows("conv_bwd_taps", fn, s, tm, [(p, "r"), (p, "p"), (w, "f"), (dy, "r"), (dy, "n")],
                 [((s, 2 * cw), MM_DTYPE, "r"), ((CONV_K + 1, cw), F32, "a")],
                 scratch=[pltpu.VMEM((tm + HALO, cw), F32), pltpu.VMEM((tm + HALO, cw), F32)])


def _dp_assemble(p, gql, gkvl, dcq, dckv_parts, dp_conv, dpe, cw):
    s, width = p.shape
    ql, kvl = gql.shape[-1], gkvl.shape[-1]
    o_q, o_kv, o_pe = 2 * cw, 2 * cw + ql, 2 * cw + ql + kvl

    def fn(i, n, p_ref, gq_ref, gkv_ref, dcq_ref, dk1_ref, dk2_ref, dconv_ref, dpe_ref, dp_ref, dgq_ref, dgkv_ref):
        dp_ref[:, 0:o_q] = dconv_ref[...]
        dx, dg = _rms_bwd(p_ref[:, o_q:o_kv], gq_ref[...], dcq_ref[...], ql)
        dp_ref[:, o_q:o_kv] = dx.astype(dp_ref.dtype)
        _acc(dgq_ref, i, dg)
        dx, dg = _rms_bwd(p_ref[:, o_kv:o_pe], gkv_ref[...], dk1_ref[...] + dk2_ref[...], kvl)
        dp_ref[:, o_kv:o_pe] = dx.astype(dp_ref.dtype)
        _acc(dgkv_ref, i, dg)
        dp_ref[:, o_pe:width] = dpe_ref[...].astype(dp_ref.dtype)

    return _rows("dp_assemble", fn, s, _tile(s, 256),
                 [(p, "r"), (gql, "f"), (gkvl, "f"), (dcq, "r"), (dckv_parts[0], "r"), (dckv_parts[1], "r"),
                  (dp_conv, "r"), (dpe, "r")],
                 [((s, width), MM_DTYPE, "r"), ((1, ql), F32, "a"), ((1, kvl), F32, "a")])


def _add_leading(name, x, out_dtype):
    n, r, c = x.shape
    tm = _tile(r, 256)

    def fn(i, nt, x_ref, o_ref):
        tot = x_ref[0].astype(F32)
        for k in range(1, n):
            tot = tot + x_ref[k].astype(F32)
        o_ref[...] = tot.astype(o_ref.dtype)

    return _rows(name, fn, r, tm, [(x, "r")], [((r, c), out_dtype, "r")])[0]


def _prefetch_call(name, fn, idx, grid, in_specs, out_spec, out_shape, args, aliases=None):
    return pl.pallas_call(
        fn, name=name, out_shape=out_shape,
        grid_spec=pltpu.PrefetchScalarGridSpec(num_scalar_prefetch=1, grid=grid, in_specs=in_specs, out_specs=out_spec),
        input_output_aliases=aliases or {}, compiler_params=_params(len(grid)))(idx, *args)


def _adamw(w, g, m, v):
    r, c = w.shape
    tm = _tile(r, 256)

    def fn(i, n, w_ref, g_ref, m_ref, v_ref, d_ref, mo_ref, vo_ref):
        gv = g_ref[...]
        mn = ADAM_B1 * m_ref[...] + (1.0 - ADAM_B1) * gv
        vn = ADAM_B2 * v_ref[...] + (1.0 - ADAM_B2) * (gv * gv)
        mo_ref[...] = mn
        vo_ref[...] = vn
        m_hat = mn / (1.0 - ADAM_B1 ** ADAM_STEP)
        v_hat = vn / (1.0 - ADAM_B2 ** ADAM_STEP)
        d_ref[...] = -ADAM_LR * (m_hat / (jnp.sqrt(v_hat) + ADAM_EPS) + ADAM_WD * w_ref[...])

    return _rows("adamw", fn, r, tm, [(w, "r"), (g, "r"), (m, "r"), (v, "r")], [((r, c), F32, "r")] * 3)


def _place():
    x, y, c = lax.axis_index("x"), lax.axis_index("y"), lax.axis_index("c")
    other_chips = [(1 - x, y), (x, 1 - y), (1 - x, 1 - y)]
    return x, y, c, 2 * x + y, other_chips


def _gather_small(vec):
    flips = [(fx, fy, fc) for fx in (0, 1) for fy in (0, 1) for fc in (0, 1)][1:]

    def body(src, dst, send, recv, lsem):
        x, y, c, _, _ = _place()
        me = 4 * x + 2 * y + c
        local = pltpu.make_async_copy(src, dst.at[me], lsem)
        local.start()
        away = []
        for j, (fx, fy, fc) in enumerate(flips):
            px, py, pc = x ^ fx, y ^ fy, c ^ fc
            away.append(pltpu.make_async_remote_copy(
                src_ref=src, dst_ref=dst.at[me], send_sem=send.at[j], recv_sem=recv.at[j],
                device_id=(px, py, pc), device_id_type=MESH))
        for cp in away:
            cp.start()
        for j, (fx, fy, fc) in enumerate(flips):
            blk = dst.at[4 * (x ^ fx) + 2 * (y ^ fy) + (c ^ fc)]
            pltpu.make_async_remote_copy(
                src_ref=blk, dst_ref=blk, send_sem=send.at[j], recv_sem=recv.at[j],
                device_id=(x ^ fx, y ^ fy, c ^ fc), device_id_type=MESH).wait_recv()
        for cp in away:
            cp.wait_send()
        local.wait()

    return pl.pallas_call(
        body, name="gather_small", in_specs=[ANY], out_specs=ANY,
        out_shape=jax.ShapeDtypeStruct((8,) + vec.shape, vec.dtype),
        scratch_shapes=[pltpu.SemaphoreType.DMA((7,)), pltpu.SemaphoreType.DMA((7,)), pltpu.SemaphoreType.DMA(())],
        compiler_params=pltpu.CompilerParams(has_side_effects=True))(vec)


HBM = pl.BlockSpec(memory_space=pltpu.HBM)
SEM = pl.BlockSpec(memory_space=pltpu.SEMAPHORE)
EFFECT = pltpu.SideEffectType.DATAFLOW_SIDE_EFFECTING


def _exchange(name, mode, arrays, sem_shape, plan, sems=None, after=None):
    n = len(arrays)
    hbm = [pltpu.HBM(a.shape, a.dtype) for a in arrays]
    arrays = [pltpu.with_memory_space_constraint(a, pltpu.HBM) for a in arrays]
    if mode == "run":
        def body(*refs):
            outs, send, recv = refs[n:2 * n], refs[2 * n], refs[2 * n + 1]
            sends, recvs = plan(outs, send, recv)
            for cp in sends:
                cp.start()
            for cp in recvs:
                cp.wait_recv()
            for cp in sends:
                cp.wait_send()

        return list(pl.pallas_call(
            body, name=name, in_specs=[HBM] * n, out_specs=[HBM] * n, out_shape=hbm,
            input_output_aliases={i: i for i in range(n)},
            scratch_shapes=[pltpu.SemaphoreType.DMA(sem_shape), pltpu.SemaphoreType.DMA(sem_shape)],
            compiler_params=pltpu.CompilerParams(has_side_effects=EFFECT))(*arrays))
    if mode == "start":
        def body(*refs):
            send, recv, token, outs = refs[n], refs[n + 1], refs[n + 2], refs[n + 3:]
            sends, _ = plan(outs, send, recv)
            for cp in sends:
                cp.start()
            token[...] = jnp.zeros(token.shape, F32)

        res = pl.pallas_call(
            body, name=name, in_specs=[HBM] * n,
            out_specs=[SEM, SEM, pl.BlockSpec(memory_space=pltpu.VMEM)] + [HBM] * n,
            out_shape=[pltpu.SemaphoreType.DMA(sem_shape), pltpu.SemaphoreType.DMA(sem_shape),
                       jax.ShapeDtypeStruct((8, HEAD_PAD), F32)] + hbm,
            input_output_aliases={i: i + 3 for i in range(n)},
            compiler_params=pltpu.CompilerParams(has_side_effects=EFFECT))(*arrays)
        return (res[0], res[1]), res[2], list(res[3:])

    def body(*refs):
        ins, send, recv = refs[:n], refs[n], refs[n + 1]
        sends, recvs = plan(ins, send, recv)
        for cp in sends:
            cp.wait_send()
        for cp in recvs:
            cp.wait_recv()

    return list(pl.pallas_call(
        body, name=name, in_specs=[HBM] * n + [SEM, SEM, ANY], out_specs=[HBM] * n, out_shape=hbm,
        input_output_aliases={i: i for i in range(n)},
        compiler_params=pltpu.CompilerParams(has_side_effects=EFFECT))(*arrays, sems[0], sems[1], after))


def _rows_half(ref_rows, half):
    h = ref_rows // 2
    return pl.ds(pl.multiple_of(half * h, 8), h)


def _remote(src, dst, send_sem, recv_sem, device):
    return pltpu.make_async_remote_copy(src_ref=src, dst_ref=dst, send_sem=send_sem, recv_sem=recv_sem,
                                        device_id=device, device_id_type=MESH)


def _plan_gather_ici(k_n, l):
    def plan(refs, send, recv):
        x, y, c, me, chips = _place()
        sends, recvs = [], []
        for k in range(k_n):
            src, buf = refs[k], refs[k_n + k]
            rows = _rows_half(src.shape[1], c)
            for j, (cx, cy) in enumerate(chips):
                sends.append(_remote(src.at[l, rows], buf.at[0, me, rows], send.at[3 * k + j], recv.at[3 * k + j], (cx, cy, c)))
                got = buf.at[0, 2 * cx + cy, rows]
                recvs.append(_remote(got, got, send.at[3 * k + j], recv.at[3 * k + j], (cx, cy, c)))
        return sends, recvs
    return plan


def _plan_gather_d2d(k_n):
    def plan(refs, send, recv):
        x, y, c, me, chips = _place()
        sends, recvs = [], []
        for k in range(k_n):
            buf = refs[k]
            for j, (cx, cy) in enumerate(chips):
                mine = buf.at[0, 2 * cx + cy, _rows_half(buf.shape[2], c)]
                theirs = buf.at[0, 2 * cx + cy, _rows_half(buf.shape[2], 1 - c)]
                sends.append(_remote(mine, mine, send.at[3 * k + j], recv.at[3 * k + j], (x, y, 1 - c)))
                recvs.append(_remote(theirs, theirs, send.at[3 * k + j], recv.at[3 * k + j], (x, y, 1 - c)))
        return sends, recvs
    return plan


def _plan_swap(k_n):
    def plan(refs, send, recv):
        x, y, c, me, chips = _place()
        sends, recvs = [], []
        for k in range(k_n):
            src, land = refs[k], refs[k_n + k]
            sends.append(_remote(src.at[:, _rows_half(src.shape[1], 1 - c)], land, send.at[k], recv.at[k], (x, y, 1 - c)))
            recvs.append(_remote(land, land, send.at[k], recv.at[k], (x, y, 1 - c)))
        return sends, recvs
    return plan


def _plan_scatter(k_n):
    def plan(refs, send, recv):
        x, y, c, me, chips = _place()
        sends, recvs = [], []
        for k in range(k_n):
            src, land = refs[k], refs[k_n + k]
            for j, (cx, cy) in enumerate(chips):
                sends.append(_remote(src.at[2 * cx + cy], land.at[j], send.at[3 * k + j], recv.at[3 * k + j], (cx, cy, c)))
                recvs.append(_remote(land.at[j], land.at[j], send.at[3 * k + j], recv.at[3 * k + j], (cx, cy, c)))
        return sends, recvs
    return plan


def _plan_share(k_n, layers):
    def plan(refs, send, recv):
        x, y, c, me, chips = _place()
        sends, recvs = [], []
        for k in range(k_n):
            out = refs[k]
            for l in range(layers):
                mine = out.at[l, _rows_half(out.shape[1], c)]
                theirs = out.at[l, _rows_half(out.shape[1], 1 - c)]
                sends.append(_remote(mine, mine, send.at[layers * k + l], recv.at[layers * k + l], (x, y, 1 - c)))
                recvs.append(_remote(theirs, theirs, send.at[layers * k + l], recv.at[layers * k + l], (x, y, 1 - c)))
        return sends, recvs
    return plan


def _add_sibling_half(idx, mine, theirs):
    q, r, c = mine.shape
    h = r // 2
    tm = _tile(h, 256)

    def fn(idx_ref, a_ref, b_ref, o_ref):
        o_ref[...] = (a_ref[...].astype(F32) + b_ref[...].astype(F32)).astype(o_ref.dtype)

    return _prefetch_call(
        "add_sibling", fn, idx, (q, h // tm),
        [pl.BlockSpec((None, None, tm, c), lambda b, i, ix: (b, ix[0], i, 0)),
         pl.BlockSpec((None, tm, c), lambda b, i, ix: (b, i, 0))],
        pl.BlockSpec((None, tm, c), lambda b, i, ix: (b, i, 0)), jax.ShapeDtypeStruct((q, h, c), WIRE_DTYPE),
        [mine.reshape(q, 2, h, c), theirs])


def _add_chips_half(idx, own, others, layers, l, into):
    _, h, c = own.shape
    tm = _tile(h, 256)

    def fn(*refs):
        a_ref, b_ref, o_ref = refs[-3:]
        tot = a_ref[...].astype(F32)
        for k in range(3):
            tot = tot + b_ref[k].astype(F32)
        o_ref[...] = tot

    in_specs = [pl.BlockSpec((None, tm, c), lambda i, ix: (ix[1], i, 0)), pl.BlockSpec((3, tm, c), lambda i, ix: (0, i, 0))]
    args = [own, others]
    aliases = None
    if into is not None:
        in_specs = [ANY] + in_specs
        args = [into.reshape(layers, 2, h, c)] + args
        aliases = {1: 0}
    out = _prefetch_call(
        "add_chips", fn, idx, (h // tm,), in_specs,
        pl.BlockSpec((None, None, tm, c), lambda i, ix: (l, ix[0], i, 0)),
        jax.ShapeDtypeStruct((layers, 2, h, c), F32), args, aliases)
    return out.reshape(layers, 2 * h, c)


SHARDED = ["ffn1_w_gate", "ffn1_w_up", "ffn1_w_down", "w_in", "w_uq", "w_ukv", "conv_w", "w_out",
           "ffn2_w_gate", "ffn2_w_up", "ffn2_w_down"]
SMALL = ["ffn1_norm", "mix_norm", "q_latent_norm", "kv_latent_norm", "q_norm", "k_norm", "conv_b", "conv_ln_g",
         "conv_ln_b", "ffn2_norm", "post_norm"]
ALL_WEIGHTS = ["ffn1_norm", "ffn1_w_gate", "ffn1_w_up", "ffn1_w_down", "mix_norm", "w_in", "q_latent_norm", "w_uq",
               "kv_latent_norm", "w_ukv", "q_norm", "k_norm", "conv_w", "conv_b", "conv_ln_g", "conv_ln_b", "w_out",
               "ffn2_norm", "ffn2_w_gate", "ffn2_w_up", "ffn2_w_down", "post_norm"]


def _rope_tables(s):
    pos = jnp.arange(s, dtype=F32)
    inv_freq = 1.0 / (ROPE_THETA ** (jnp.arange(0, QK_ROPE, 2, dtype=F32) / QK_ROPE))
    ang = pos[:, None] * inv_freq[None, :]
    cos, sin = jnp.cos(ang), jnp.sin(ang)
    half = QK_ROPE // 2
    one = jnp.ones((s, QK_NOPE), F32)
    zero = jnp.zeros((s, QK_NOPE), F32)
    pad = HEAD_PAD - QK_DIM
    z16 = jnp.zeros((s, half), F32)
    cosf = jnp.concatenate([one, cos, cos, jnp.ones((s, pad), F32)], axis=1)
    s1 = jnp.concatenate([zero, -sin, z16, jnp.zeros((s, pad), F32)], axis=1)
    s2 = jnp.concatenate([zero, z16, sin, jnp.zeros((s, pad), F32)], axis=1)
    return cosf, s1, s2


def _ffn_fwd(n, w, names, l):
    a, b, hm = _ffn_up_fused(n, w[names[0]], w[names[1]], l)
    return a, b, hm, _blocked_sum("ffn_down", [hm], [w[names[2]]], l, "nn")


def _ffn_bwd(nt, a, b, hm, do, w, names, grads):
    gate, up, down = names
    da, db = _ffn_down_t_fused(do, w[down], a, b, 0)
    grads[down] = _wgrad("ffn_down_wgrad", hm, do, WIRE_DTYPE)
    g_gate, g_up = _wgrad_pair("ffn_up_wgrad", nt, da, db, 1, 0, None)
    grads[gate], grads[up] = g_gate[0], g_up[0]
    return [_blocked_sum("ffn_up_t", [da, db], [w[gate], w[up]], 0, "nt")]


def _vec(v, l):
    return v[l][None, :]


def _pad_lanes(v, width):
    return jnp.pad(v, [(0, 0)] * (v.ndim - 1) + [(0, width - v.shape[-1])])


def kernel(x, ffn1_norm, ffn1_w_gate, ffn1_w_up, ffn1_w_down, mix_norm, w_in, q_latent_norm, w_uq, kv_latent_norm, w_ukv, q_norm, k_norm, conv_w, conv_b, conv_ln_g, conv_ln_b, w_out, ffn2_norm, ffn2_w_gate, ffn2_w_up, ffn2_w_down, post_norm, loss_target, m_ffn1_norm, m_ffn1_w_gate, m_ffn1_w_up, m_ffn1_w_down, m_mix_norm, m_w_in, m_q_latent_norm, m_w_uq, m_kv_latent_norm, m_w_ukv, m_q_norm, m_k_norm, m_conv_w, m_conv_b, m_conv_ln_g, m_conv_ln_b, m_w_out, m_ffn2_norm, m_ffn2_w_gate, m_ffn2_w_up, m_ffn2_w_down, m_post_norm, v_ffn1_norm, v_ffn1_w_gate, v_ffn1_w_up, v_ffn1_w_down, v_mix_norm, v_w_in, v_q_latent_norm, v_w_uq, v_kv_latent_norm, v_w_ukv, v_q_norm, v_k_norm, v_conv_w, v_conv_b, v_conv_ln_g, v_conv_ln_b, v_w_out, v_ffn2_norm, v_ffn2_w_gate, v_ffn2_w_up, v_ffn2_w_down, v_post_norm):
    given = dict(locals())
    wts = {n: given[n] for n in ALL_WEIGHTS}
    mom = {n: given["m_" + n] for n in ALL_WEIGHTS}
    var = {n: given["v_" + n] for n in ALL_WEIGHTS}
    n_layers = ffn1_norm.shape[0]
    s, d = x.shape[1], x.shape[2]
    ql, kvl, cw = q_latent_norm.shape[-1], kv_latent_norm.shape[-1], conv_b.shape[-1]
    xs = x.reshape(s, d)
    target = loss_target.reshape(s, d)

    assert n_layers == 2, "the layers' exchanges run one behind the other"
    local = [jnp.pad(wts[n], ((0, 0), (0, 1), (0, 0))).astype(MM_DTYPE) if n == "conv_w" else wts[n].astype(MM_DTYPE)
             for n in SHARDED]
    L = n_layers
    k_n = len(SHARDED)
    my_c = lax.axis_index("c")
    my_chip = 2 * lax.axis_index("x") + lax.axis_index("y")
    place = jnp.stack([my_c, my_chip]).astype(jnp.int32)
    hw = N_HEADS * HEAD_PAD
    tabs = _rope_tables(s)
    pe_block = (2 * cw + ql + kvl) // HEAD_PAD
    gq_pad, gk_pad = _pad_lanes(q_norm, HEAD_PAD), _pad_lanes(k_norm, HEAD_PAD)
    tq = _tile(s, 256)

    def own_blocks(l):
        return [lax.dynamic_update_slice(lax.empty((1, N_CHIPS) + t.shape[1:], t.dtype), t[l][None, None],
                                         (0, my_chip, 0, 0)) for t in local]

    def layouts(w):
        win = w["w_in"].transpose(0, 2, 1, 3).reshape(1, d, -1)
        zc = lambda n: jnp.zeros((1, d, n), MM_DTYPE)
        o_pe, o_cv = ql + kvl, ql + kvl + QK_ROPE
        w["w_in"] = jnp.concatenate([win[..., o_cv:], win[..., :o_pe], zc(QK_NOPE), win[..., o_pe:o_cv],
                                     zc(HEAD_PAD - QK_DIM)], axis=-1)[:, None]
        wq = w["w_uq"].transpose(0, 2, 1, 3).reshape(1, ql, N_HEADS, QK_DIM)
        w["wq"] = _pad_lanes(wq, HEAD_PAD).reshape(1, 1, ql, hw)
        wkv = w["w_ukv"].transpose(0, 2, 1, 3).reshape(1, kvl, N_HEADS, QK_NOPE + V_DIM)
        w["wk"] = _pad_lanes(wkv[..., :QK_NOPE], HEAD_PAD).reshape(1, 1, kvl, hw)
        w["wv"] = _pad_lanes(wkv[..., QK_NOPE:], HEAD_PAD).reshape(1, 1, kvl, hw)
        w["wvt"] = w["wv"][:, 0].transpose(0, 2, 1)
        w["conv"] = w["conv_w"].transpose(0, 2, 1, 3).reshape(CONV_K + 1, cw).astype(F32)
        wout = w["w_out"].reshape(1, d, d)
        wo_a = wout[:, :N_HEADS * V_DIM].reshape(1, N_HEADS, V_DIM, d)
        w["wo_a"] = jnp.pad(wo_a, ((0, 0), (0, 0), (0, HEAD_PAD - V_DIM), (0, 0))).reshape(1, 1, hw, d)
        w["wo_c"] = wout[:, N_HEADS * V_DIM:][:, None]
        return w

    ici, d2d, sem_kj = (lambda l: _plan_gather_ici(k_n, l)), _plan_gather_d2d(k_n), (3 * k_n,)
    n_a = 3
    n_b = k_n - n_a
    own0 = own_blocks(0)
    landed = _exchange("gather0a_ici", "run", local[:n_a] + own0[:n_a], (3 * n_a,), _plan_gather_ici(n_a, 0))
    w0 = dict(zip(SHARDED[:n_a], _exchange("gather0a_d2d", "run", landed[n_a:], (3 * n_a,), _plan_gather_d2d(n_a))))
    local = landed[:n_a] + local[n_a:]
    sems_r, token_r, rest = _exchange("gather0b_ici_start", "start", local[n_a:] + own0[n_a:], (3 * n_b,),
                                      _plan_gather_ici(n_b, 0))

    def fwd_ffn(l, cur, w, tok):
        sv = {"x": cur}
        n1, sv["n1t"] = _norm_fwd(cur, _vec(ffn1_norm, l) + tok)
        sv["a1"], sv["b1"], sv["hm1"], o1 = _ffn_fwd(n1, w, SHARDED[0:3], 0)
        sv["x1"], sv["h"] = _resid_norm(cur, [o1], 0.5, _vec(mix_norm, l), MM_DTYPE)
        return sv

    def fwd_first(l, sv, w, tok):
        sv["p"] = _proj("w_in", sv["h"], w["w_in"], 0, "nn")[0]
        sv["cq"], sv["ckv"], ckvt = _mla_pre(sv["p"], _vec(q_latent_norm, l) + tok, _vec(kv_latent_norm, l), cw)
        sv["q_pre"] = _proj("wq", sv["cq"], w["wq"], 0, "nn")[0]
        sv["k_pre"] = _proj("wk", sv["ckv"], w["wk"], 0, "nn")[0]
        sv["v"] = _proj("wv", sv["ckv"], w["wv"], 0, "nn", MM_DTYPE)[0]
        sv["vt"] = _mm("wv_transposed", w["wvt"], ckvt, "nn", (s // tq,),
                       pl.BlockSpec((None, hw, kvl), lambda i: (0, 0, 0)), pl.BlockSpec((kvl, tq), lambda i: (0, i)),
                       pl.BlockSpec((None, hw, tq), lambda i: (i, 0, 0)),
                       jax.ShapeDtypeStruct((s // tq, hw, tq), MM_DTYPE), 1, None)
        sv["q"], sv["k"], sv["kt"] = _qk_post(sv["q_pre"], sv["k_pre"], sv["p"], pe_block, tabs,
                                              _vec(gq_pad, l), _vec(gk_pad, l))
        return sv

    def fwd_attn(l, sv, w):
        sv["c"] = _conv_fwd(sv["p"], w["conv"], _vec(conv_b, l), _vec(conv_ln_g, l), _vec(conv_ln_b, l))
        sv["o"], sv["lse"] = _attn_fwd(sv["q"], sv["k"], sv.pop("vt"))
        return sv

    def fwd_second(l, sv, w, tok):
        mix_a = _contract("wo_a", sv["o"][None], w["wo_a"], 0, "nn")
        mix_c = _contract("wo_c", sv["c"][None], w["wo_c"], 0, "nn")
        sv["x2"], n2, sv["n2t"] = _resid_norm(sv["x1"], [mix_a, mix_c], 1.0, _vec(ffn2_norm, l) + tok, MM_DTYPE, True)
        sv["a2"], sv["b2"], sv["hm2"], o2 = _ffn_fwd(n2, w, SHARDED[8:11], 0)
        sv["x3"], out = _final_fwd(sv["x2"], o2, _vec(post_norm, l))
        return out

    sv0 = fwd_ffn(0, xs, w0, token_r[0, 0])
    rest = _exchange("gather0b_ici_wait", "wait", rest, (3 * n_b,), _plan_gather_ici(n_b, 0), sems=sems_r, after=sv0["h"])
    w0.update(zip(SHARDED[n_a:], _exchange("gather0b_d2d", "run", rest[n_b:], (3 * n_b,), _plan_gather_d2d(n_b))))
    w0 = layouts(w0)
    local = local[:n_a] + rest[:n_b]
    sems_a, token_a, in_flight = _exchange("gather1_ici_start", "start", local + own_blocks(1), sem_kj, ici(1))
    sv0 = fwd_attn(0, fwd_first(0, sv0, w0, token_a[0, 0]), w0)
    in_flight = _exchange("gather1_ici_wait", "wait", in_flight, sem_kj, ici(1), sems=sems_a, after=sv0["o"])
    sems_b, token_b, in_flight = _exchange("gather1_d2d_start", "start", in_flight[k_n:], sem_kj, d2d)
    cur = fwd_second(0, sv0, w0, token_b[0, 0])
    w1 = layouts(dict(zip(SHARDED, _exchange("gather1_d2d_wait", "wait", in_flight, sem_kj, d2d, sems=sems_b, after=cur))))
    sv1 = fwd_attn(1, fwd_first(1, fwd_ffn(1, cur, w1, 0.0), w1, 0.0), w1)
    cur = fwd_second(1, sv1, w1, 0.0)
    dy, loss_part = _loss(cur, target)

    small = {n: [None] * L for n in SMALL}

    def bwd_first(l, sv, dy, w, tok, grads):
        dx3, do2, small["post_norm"][l] = _norm_bwd(sv["x3"], _vec(post_norm, l) + tok, [dy], None, 0.5)
        dn2 = _ffn_bwd(sv["n2t"], sv["a2"], sv["b2"], sv["hm2"], do2, w, SHARDED[8:11], grads)
        dx2, dx2b, small["ffn2_norm"][l] = _norm_bwd(sv["x2"], _vec(ffn2_norm, l), dn2, dx3, 1.0)
        do = _proj("wo_a_t", dx2b, w["wo_a"], 0, "nt")[0]
        dc = _proj("wo_c_t", dx2b, w["wo_c"], 0, "nt")[0]
        g_wo_a = _wgrad("wo_a_wgrad", sv["o"], dx2b[None], WIRE_DTYPE)[0].reshape(N_HEADS, HEAD_PAD, d)[:, :V_DIM]
        g_wo_c = _wgrad("wo_c_wgrad", sv["c"], dx2b[None], WIRE_DTYPE)
        grads["w_out"] = jnp.concatenate(
            [g_wo_a.reshape(N_HEADS * V_DIM, d), g_wo_c[0]], axis=0).reshape(N_CHIPS, d // N_CHIPS, d)
        dq, dk, dv = _attn_bwd(sv["q"], sv["k"], sv["kt"], sv["v"], do, sv["lse"], _attn_delta(sv["o"], do))
        dq_pre, dk_pre, dpe, small["q_norm"][l], small["k_norm"][l] = _qk_post_bwd(
            sv["q_pre"], sv["k_pre"], sv["p"], pe_block, tabs, _vec(gq_pad, l), _vec(gk_pad, l), dq, dk)
        return dict(dx2=dx2, dc=dc, dq_pre=dq_pre, dk_pre=dk_pre, dv=dv, dpe=dpe)

    def bwd_second(l, sv, st, w, tok, grads):
        dq_pre, dk_pre, dv = st["dq_pre"], st["dk_pre"], st["dv"]
        dcq = _contract("wq_t", dq_pre[None], w["wq"], 0, "nt")
        dckv = [_contract("wk_t", dk_pre[None], w["wk"], 0, "nt"), _contract("wv_t", dv[None], w["wv"], 0, "nt")]
        per_head = lambda t, rows: t[0].reshape(rows, N_HEADS, HEAD_PAD)
        g_wq = per_head(_wgrad("wq_wgrad", sv["cq"], dq_pre[None], WIRE_DTYPE), ql)[..., :QK_DIM]
        grads["w_uq"] = g_wq.reshape(ql, N_CHIPS, -1).transpose(1, 0, 2)
        g_wkv = jnp.concatenate([per_head(_wgrad("wk_wgrad", sv["ckv"], dk_pre[None], WIRE_DTYPE), kvl)[..., :QK_NOPE],
                                 per_head(_wgrad("wv_wgrad", sv["ckv"], dv[None], WIRE_DTYPE), kvl)[..., :V_DIM]], axis=-1)
        grads["w_ukv"] = g_wkv.reshape(kvl, N_CHIPS, -1).transpose(1, 0, 2)
        dyc, small["conv_ln_g"][l], small["conv_ln_b"][l], small["conv_b"][l] = _conv_bwd_ln(
            sv["p"], w["conv"], _vec(conv_b, l) + tok, _vec(conv_ln_g, l), _vec(conv_ln_b, l), st["dc"])
        dp_conv, g_conv = _conv_bwd_taps(sv["p"], w["conv"], dyc)
        grads["conv_w"] = g_conv.reshape(CONV_K + 1, N_CHIPS, cw // N_CHIPS).transpose(1, 0, 2).astype(WIRE_DTYPE)
        dp, small["q_latent_norm"][l], small["kv_latent_norm"][l] = _dp_assemble(
            sv["p"], _vec(q_latent_norm, l), _vec(kv_latent_norm, l), dcq, dckv, dp_conv, st["dpe"], cw)
        dh = _contract("w_in_t", dp[None], w["w_in"], 0, "nt")
        g_win = _wgrad("w_in_wgrad", sv["h"], dp[None], WIRE_DTYPE)[0]
        o_q, o_pe_col = 2 * cw, 2 * cw + ql + kvl + QK_NOPE
        g_win = jnp.concatenate([g_win[:, o_q:o_q + ql + kvl], g_win[:, o_pe_col:o_pe_col + QK_ROPE], g_win[:, :o_q]], axis=1)
        grads["w_in"] = g_win.reshape(d, N_CHIPS, -1).transpose(1, 0, 2)
        dx1, do1, small["mix_norm"][l] = _norm_bwd(sv["x1"], _vec(mix_norm, l), [dh], st["dx2"], 0.5)
        dn1 = _ffn_bwd(sv["n1t"], sv["a1"], sv["b1"], sv["hm1"], do1, w, SHARDED[0:3], grads)
        dx, _, small["ffn1_norm"][l] = _norm_bwd(sv["x"], _vec(ffn1_norm, l), dn1, dx1, 1.0)
        return dx

    def halves(parts):
        return [lax.empty((N_CHIPS, t.shape[1] // 2, t.shape[2]), t.dtype) for t in parts]

    def thirds(parts):
        return [lax.empty((N_CHIPS - 1,) + t.shape[1:], t.dtype) for t in parts]

    def chip_sums(done):
        n = len(done) // 2
        return [_add_sibling_half(place, g, t) for g, t in zip(done[:n], done[n:])]

    def finish(done, l, into):
        n = len(done) // 2
        return [_add_chips_half(place, own, others, L, l, t) for own, others, t in zip(done[:n], done[n:], into)]

    swap, scatter, sem_k = _plan_swap(k_n), _plan_scatter(k_n), (k_n,)
    grads1, grads0 = {}, {}
    st1 = bwd_first(1, sv1, dy, w1, 0.0, grads1)
    dy = bwd_second(1, sv1, st1, w1, 0.0, grads1)
    parts1 = [grads1[n] for n in SHARDED]
    sems_s, token_s, in_flight = _exchange("swap1_start", "start", parts1 + halves(parts1), sem_k, swap)
    st0 = bwd_first(0, sv0, dy, w0, token_s[0, 0], grads0)
    sums1 = chip_sums(_exchange("swap1_wait", "wait", in_flight, sem_k, swap, sems=sems_s, after=st0["dq_pre"]))
    sems_c, token_c, in_flight = _exchange("scatter1_start", "start", sums1 + thirds(sums1), sem_kj, scatter)
    early = [n for n in SHARDED if n in grads0]
    late = [n for n in SHARDED if n not in grads0]
    n_e, n_l = len(early), len(late)
    parts_e = [grads0[n] for n in early]
    sums_e = chip_sums(_exchange("swap0_early", "run", parts_e + halves(parts_e), (n_e,), _plan_swap(n_e)))
    sems_e, token_e, early_flight = _exchange("scatter0_early_start", "start", sums_e + thirds(sums_e), (3 * n_e,),
                                              _plan_scatter(n_e))
    dy = bwd_second(0, sv0, st0, w0, token_c[0, 0] + token_e[0, 0], grads0)
    grad_x = dy.reshape(x.shape)
    out = dict(zip(SHARDED, finish(_exchange("scatter1_wait", "wait", in_flight, sem_kj, scatter, sems=sems_c, after=dy),
                                   1, [None] * k_n)))
    done = _exchange("scatter0_early_wait", "wait", early_flight, (3 * n_e,), _plan_scatter(n_e), sems=sems_e, after=dy)
    out.update(zip(early, finish(done, 0, [out[n] for n in early])))
    parts_l = [grads0[n] for n in late]
    sums_l = chip_sums(_exchange("swap0_late", "run", parts_l + halves(parts_l), (n_l,), _plan_swap(n_l)))
    sems_l, token_l, late_flight = _exchange("scatter0_late_start", "start", sums_l + thirds(sums_l), (3 * n_l,),
                                             _plan_scatter(n_l))

    delta_w, new_m, new_v = {}, {}, {}
    width = max(wts[n].shape[-1] for n in SMALL)
    pack = lambda vals: jnp.concatenate([_pad_lanes(v.reshape(-1, v.shape[-1]), width) for v in vals], axis=0)
    part = pack([jnp.concatenate(small[n], axis=0)[..., :wts[n].shape[-1]] for n in SMALL])
    rows = part.shape[0]
    part = jnp.pad(part, [(0, -rows % 8), (0, 0)])
    small_sum = _add_leading("add_devices", _gather_small(part + token_l[0, 0]), F32)
    pk = lambda src: jnp.pad(pack([src[n] for n in SMALL]), [(0, -rows % 8), (0, 0)])
    small_outs = _adamw(pk(wts), small_sum, pk(mom), pk(var))

    done = _exchange("scatter0_late_wait", "wait", late_flight, (3 * n_l,), _plan_scatter(n_l), sems=sems_l,
                     after=small_outs[0])
    out.update(zip(late, finish(done, 0, [out[n] for n in late])))
    full = dict(zip(SHARDED, _exchange("share", "run", [out[n] for n in SHARDED], (L * k_n,), _plan_share(k_n, L))))
    full["conv_w"] = full["conv_w"][:, :CONV_K]
    for i, n in enumerate(SMALL):
        full[n] = small_sum[i * L:(i + 1) * L, :wts[n].shape[-1]]
        delta_w[n], new_m[n], new_v[n] = [o[i * L:(i + 1) * L, :wts[n].shape[-1]] for o in small_outs]

    for n in SHARDED:
        shape = wts[n].shape
        swap = shape[-1] % HEAD_PAD != 0 and shape[-2] % HEAD_PAD == 0
        view = (lambda t: jnp.swapaxes(t, 1, 2)) if swap else (lambda t: t)
        flat = lambda t: view(t).reshape(-1, view(t).shape[-1])
        r = flat(wts[n]).shape[0]
        fix = (lambda t: jnp.pad(flat(t), [(0, -r % 8), (0, 0)])) if r % 8 else flat
        outs = _adamw(fix(wts[n]), fix(full[n]), fix(mom[n]), fix(var[n]))
        delta_w[n], new_m[n], new_v[n] = [view(o[:r].reshape(view(wts[n]).shape)) for o in outs]
    loss = lax.psum(loss_part[0, 0], ("x", "y", "c"))
    return (loss, grad_x, *[full[n] for n in ALL_WEIGHTS], *[delta_w[n] for n in ALL_WEIGHTS],
            *[new_m[n] for n in ALL_WEIGHTS], *[new_v[n] for n in ALL_WEIGHTS])
```

```python
import jax
import jax.numpy as jnp
from jax import lax
from jax.experimental import pallas as pl
from jax.experimental.pallas import tpu as pltpu

F32 = jnp.float32
MM_DTYPE = jnp.bfloat16
WIRE_DTYPE = jnp.bfloat16
EPS = 1e-6
N_HEADS = 8
QK_NOPE = 64
QK_ROPE = 32
QK_DIM = QK_NOPE + QK_ROPE
V_DIM = 64
HEAD_PAD = 128
HEADS_PER_STEP = 4
CHUNK_SHIFT = 6
LOG2E = 1.4426950408889634
CONV_K = 31
HALO = 32
ROPE_THETA = 10000.0
N_CHIPS = 4
ADAM_LR, ADAM_B1, ADAM_B2, ADAM_EPS, ADAM_WD, ADAM_STEP = 0.001, 0.9, 0.999, 1e-08, 0.01, 10
VMEM_LIMIT_BYTES = 56 * 2 ** 20
WGRAD_ROWS = 2048
MESH = pl.DeviceIdType.MESH
ANY = pl.BlockSpec(memory_space=pl.ANY)

_DIMS = {
    "nn": (((1,), (0,)), ((), ())),
    "nt": (((1,), (1,)), ((), ())),
    "tn": (((0,), (0,)), ((), ())),
}


def _params(n_axes):
    return pltpu.CompilerParams(dimension_semantics=("arbitrary",) * n_axes, vmem_limit_bytes=VMEM_LIMIT_BYTES)


def _tile(n, cap):
    for step in (16, 8):
        for t in range(cap - cap % step, 0, -step):
            if n % t == 0:
                return t
    return n


def _mm(name, a, b, dims, grid, a_spec, b_spec, o_spec, out_shape, nk, acc_shape, into=None):
    n_axes = len(grid)

    def body(*refs):
        if into is not None:
            refs = refs[1:]
        a_ref, b_ref, o_ref = refs[:3]
        part = lax.dot_general(a_ref[...].astype(MM_DTYPE), b_ref[...].astype(MM_DTYPE), _DIMS[dims],
                               preferred_element_type=F32)
        if nk == 1:
            o_ref[...] = part.astype(o_ref.dtype)
        else:
            acc_ref = refs[3]
            k = pl.program_id(n_axes - 1)

            @pl.when(k == 0)
            def _():
                acc_ref[...] = part

            @pl.when(k > 0)
            def _():
                acc_ref[...] += part

            @pl.when(k == nk - 1)
            def _():
                o_ref[...] = acc_ref[...].astype(o_ref.dtype)

    in_specs = [a_spec, b_spec]
    args = [a, b]
    aliases = {}
    if into is not None:
        in_specs = [ANY] + in_specs
        args = [into] + args
        aliases = {0: 0}
    return pl.pallas_call(
        body, name=name, grid=grid, in_specs=in_specs, out_specs=o_spec, out_shape=out_shape,
        scratch_shapes=[] if nk == 1 else [pltpu.VMEM(acc_shape, F32)],
        input_output_aliases=aliases, compiler_params=_params(n_axes))(*args)


def _proj(name, a, w, l, dims, out_dtype=F32):
    s, k = a.shape
    g = w.shape[1]
    n = w.shape[3] if dims == "nn" else w.shape[2]
    tm = _tile(s, 512)
    return _mm(name, a, w, dims, (g, s // tm),
               pl.BlockSpec((tm, k), lambda j, i: (i, 0)),
               pl.BlockSpec((None, None) + w.shape[2:], lambda j, i: (l, j, 0, 0)),
               pl.BlockSpec((None, tm, n), lambda j, i: (j, i, 0)),
               jax.ShapeDtypeStruct((g, s, n), out_dtype), 1, None)


def _contract(name, a, w, l, dims, out_dtype=F32):
    g, s, kb = a.shape
    n = w.shape[3] if dims == "nn" else w.shape[2]
    tm = _tile(s, 512)
    return _mm(name, a, w, dims, (s // tm, g),
               pl.BlockSpec((None, tm, kb), lambda i, j: (j, i, 0)),
               pl.BlockSpec((None, None) + w.shape[2:], lambda i, j: (l, j, 0, 0)),
               pl.BlockSpec((tm, n), lambda i, j: (i, 0)),
               jax.ShapeDtypeStruct((s, n), out_dtype), g, (tm, n))


def _wgrad(name, a, b, out_dtype, layers=None, l=None, into=None):
    g = a.shape[0] if a.ndim == 3 else b.shape[0]
    s, k = a.shape[-2:]
    n = b.shape[-1]
    tm = _tile(s, WGRAD_ROWS)

    def spec(x, w):
        if x.ndim == 3:
            return pl.BlockSpec((None, tm, w), lambda j, i: (j, i, 0))
        return pl.BlockSpec((tm, w), lambda j, i: (i, 0))

    if layers is None:
        o_spec = pl.BlockSpec((None, k, n), lambda j, i: (j, 0, 0))
        out_shape = jax.ShapeDtypeStruct((g, k, n), out_dtype)
    else:
        o_spec = pl.BlockSpec((None, None, k, n), lambda j, i: (l, j, 0, 0))
        out_shape = jax.ShapeDtypeStruct((layers, g, k, n), out_dtype)
    return _mm(name, a, b, "tn", (g, s // tm), spec(a, k), spec(b, n), o_spec, out_shape, s // tm, (k, n), into=into)


def _ffn_up_fused(n, wg, wu, l):
    s, k = n.shape
    g, n4 = wg.shape[1], wg.shape[3]
    tm = _tile(s, 512)

    def body(n_ref, wg_ref, wu_ref, a_ref, b_ref, h_ref):
        nv = n_ref[...]
        a = jnp.dot(nv, wg_ref[...], preferred_element_type=F32)
        b = jnp.dot(nv, wu_ref[...], preferred_element_type=F32)
        a_ref[...] = a.astype(a_ref.dtype)
        b_ref[...] = b.astype(b_ref.dtype)
        h_ref[...] = (a * _sigmoid(a) * b).astype(h_ref.dtype)

    w_spec = pl.BlockSpec((None, None, k, n4), lambda j, i: (l, j, 0, 0))
    o_spec = pl.BlockSpec((None, tm, n4), lambda j, i: (j, i, 0))
    return pl.pallas_call(
        body, name="ffn_up_fused", grid=(g, s // tm),
        in_specs=[pl.BlockSpec((tm, k), lambda j, i: (i, 0)), w_spec, w_spec], out_specs=[o_spec] * 3,
        out_shape=[jax.ShapeDtypeStruct((g, s, n4), MM_DTYPE)] * 3, compiler_params=_params(2))(n, wg, wu)


def _ffn_down_t_fused(do, wd, a, b, l):
    s, k = do.shape
    g, n4 = wd.shape[1], wd.shape[2]
    tm = _tile(s, 512)

    def body(do_ref, wd_ref, a_ref, b_ref, da_ref, db_ref):
        dh = lax.dot_general(do_ref[...], wd_ref[...], _DIMS["nt"], preferred_element_type=F32)
        av = a_ref[...].astype(F32)
        sg = _sigmoid(av)
        da_ref[...] = (dh * b_ref[...].astype(F32) * (sg * (1.0 + av * (1.0 - sg)))).astype(da_ref.dtype)
        db_ref[...] = (dh * (av * sg)).astype(db_ref.dtype)

    t_spec = pl.BlockSpec((None, tm, n4), lambda j, i: (j, i, 0))
    return pl.pallas_call(
        body, name="ffn_down_t_fused", grid=(g, s // tm),
        in_specs=[pl.BlockSpec((tm, k), lambda j, i: (i, 0)),
                  pl.BlockSpec((None, None, n4, k), lambda j, i: (l, j, 0, 0)), t_spec, t_spec],
        out_specs=[t_spec] * 2, out_shape=[jax.ShapeDtypeStruct((g, s, n4), MM_DTYPE)] * 2,
        compiler_params=_params(2))(do, wd, a, b)


def _blocked_sum(name, acts, weights, l, dims):
    g, s, kb = acts[0].shape
    n = weights[0].shape[3] if dims == "nn" else weights[0].shape[2]
    tm = _tile(s, 512)
    n_pairs = len(acts)

    def body(*refs):
        o_ref = refs[2 * n_pairs]
        acc = None
        for p in range(n_pairs):
            for j in range(g):
                t = lax.dot_general(refs[p][j], refs[n_pairs + p][j], _DIMS[dims], preferred_element_type=F32)
                acc = t if acc is None else acc + t
        o_ref[...] = acc

    return pl.pallas_call(
        body, name=name, grid=(s // tm,),
        in_specs=[pl.BlockSpec((g, tm, kb), lambda i: (0, i, 0))] * n_pairs
        + [pl.BlockSpec((None,) + w.shape[1:], lambda i: (l, 0, 0, 0)) for w in weights],
        out_specs=pl.BlockSpec((tm, n), lambda i: (i, 0)), out_shape=jax.ShapeDtypeStruct((s, n), F32),
        compiler_params=_params(1))(*acts, *weights)


def _wgrad_pair(name, nt, da, db, layers, l, into):
    k, s = nt.shape
    g, _, n4 = da.shape
    tm = _tile(s, WGRAD_ROWS)
    nk = s // tm

    def body(*refs):
        if into is not None:
            refs = refs[2:]
        nt_ref, da_ref, db_ref, og_ref, ou_ref, accg, accu = refs
        i = pl.program_id(1)
        ntv = nt_ref[...]
        pg = jnp.dot(ntv, da_ref[...], preferred_element_type=F32)
        pu = jnp.dot(ntv, db_ref[...], preferred_element_type=F32)

        @pl.when(i == 0)
        def _():
            accg[...] = pg
            accu[...] = pu

        @pl.when(i > 0)
        def _():
            accg[...] += pg
            accu[...] += pu

        @pl.when(i == nk - 1)
        def _():
            og_ref[...] = accg[...].astype(og_ref.dtype)
            ou_ref[...] = accu[...].astype(ou_ref.dtype)

    t_spec = pl.BlockSpec((None, tm, n4), lambda j, i: (j, i, 0))
    o_spec = pl.BlockSpec((None, None, k, n4), lambda j, i: (l, j, 0, 0))
    in_specs = [pl.BlockSpec((k, tm), lambda j, i: (0, i)), t_spec, t_spec]
    args = [nt, da, db]
    aliases = {}
    if into is not None:
        in_specs = [ANY, ANY] + in_specs
        args = list(into) + args
        aliases = {0: 0, 1: 1}
    return pl.pallas_call(
        body, name=name, grid=(g, nk), in_specs=in_specs, out_specs=[o_spec] * 2,
        out_shape=[jax.ShapeDtypeStruct((layers, g, k, n4), WIRE_DTYPE)] * 2,
        scratch_shapes=[pltpu.VMEM((k, n4), F32)] * 2, input_output_aliases=aliases,
        compiler_params=_params(2))(*args)


def _rows(name, fn, s, tm, ins, outs, scratch=()):
    n = s // tm

    def spec(shape, kind):
        nd = len(shape)
        if kind == "r":
            return pl.BlockSpec(shape[:-2] + (tm, shape[-1]), lambda i: (0,) * (nd - 2) + (i, 0))
        if kind in ("f", "a"):
            return pl.BlockSpec(shape, lambda i: (0,) * nd)
        if kind == "t":
            return pl.BlockSpec((shape[0], tm), lambda i: (0, i))
        if kind == "b":
            return pl.BlockSpec((None,) + shape[1:], lambda i: (i, 0, 0))
        if kind == "p":
            return pl.BlockSpec((HALO, shape[-1]), lambda i: (jnp.maximum(i * (tm // HALO) - 1, 0), 0))
        if kind == "n":
            return pl.BlockSpec((HALO, shape[-1]), lambda i: (jnp.minimum((i + 1) * (tm // HALO), s // HALO - 1), 0))
        j = kind[1]
        return pl.BlockSpec((tm, HEAD_PAD), lambda i: (i, j))

    def body(*refs):
        fn(pl.program_id(0), n, *refs)

    return pl.pallas_call(
        body, name=name, grid=(n,),
        in_specs=[spec(a.shape, kind) for a, kind in ins],
        out_specs=[spec(shape, kind) for shape, _, kind in outs],
        out_shape=[jax.ShapeDtypeStruct(shape, dtype) for shape, dtype, _ in outs],
        scratch_shapes=list(scratch), compiler_params=_params(1))(*[a for a, _ in ins])


def _acc(ref, i, val):
    @pl.when(i == 0)
    def _():
        ref[...] = val

    @pl.when(i > 0)
    def _():
        ref[...] += val


def _sum0(x):
    return jnp.sum(x, axis=0, keepdims=True)


def _rstd(x, n):
    return lax.rsqrt(jnp.sum(x * x, axis=-1, keepdims=True) * (1.0 / n) + EPS)


def _rms_bwd(x, g, dy, n):
    r = _rstd(x, n)
    xh = x * r
    dyg = dy * g
    dx = r * (dyg - xh * (jnp.sum(dyg * xh, axis=-1, keepdims=True) * (1.0 / n)))
    return dx, _sum0(dy * xh)


def _sigmoid(x):
    return 1.0 / (1.0 + jnp.exp(-x))


def _norm_fwd(x, g):
    s, d = x.shape

    def fn(i, n, x_ref, g_ref, o_ref, ot_ref):
        xv = x_ref[...]
        y = xv * _rstd(xv, d) * g_ref[...]
        o_ref[...] = y.astype(o_ref.dtype)
        ot_ref[...] = y.T.astype(ot_ref.dtype)

    return _rows("norm_fwd", fn, s, _tile(s, 256), [(x, "r"), (g, "f")],
                 [((s, d), MM_DTYPE, "r"), ((d, s), MM_DTYPE, "t")])


def _resid_norm(x, adds, scale, g, norm_dtype, transposed=False):
    s, d = x.shape
    k = len(adds)

    def fn(i, n, *refs):
        x_ref, add_refs, g_ref, xo_ref, no_ref = refs[0], refs[1:1 + k], refs[1 + k], refs[2 + k], refs[3 + k]
        tot = add_refs[0][...]
        for r in add_refs[1:]:
            tot = tot + r[...]
        xv = x_ref[...] + scale * tot
        xo_ref[...] = xv
        y = xv * _rstd(xv, d) * g_ref[...]
        no_ref[...] = y.astype(no_ref.dtype)
        if transposed:
            refs[4 + k][...] = y.T.astype(refs[4 + k].dtype)

    return _rows("resid_norm", fn, s, _tile(s, 256), [(x, "r")] + [(a, "r") for a in adds] + [(g, "f")],
                 [((s, d), F32, "r"), ((s, d), norm_dtype, "r")] + ([((d, s), MM_DTYPE, "t")] if transposed else []))


def _norm_bwd(x, g, dn_parts, dres, out_scale):
    s, d = x.shape
    k = len(dn_parts)
    has_res = dres is not None

    def fn(i, n, *refs):
        x_ref, g_ref = refs[0], refs[1]
        dn_refs = refs[2:2 + k]
        pos = 2 + k
        dn = dn_refs[0][...]
        for r in dn_refs[1:]:
            dn = dn + r[...]
        dx, dg = _rms_bwd(x_ref[...], g_ref[...], dn, d)
        if has_res:
            dx = dx + refs[pos][...]
            pos += 1
        dx_ref, dxs_ref, dg_ref = refs[pos:pos + 3]
        dx_ref[...] = dx
        dxs_ref[...] = (out_scale * dx).astype(dxs_ref.dtype)
        _acc(dg_ref, i, dg)

    ins = [(x, "r"), (g, "f")] + [(p, "r") for p in dn_parts] + ([(dres, "r")] if has_res else [])
    return _rows("norm_bwd", fn, s, _tile(s, 256), ins,
                 [((s, d), F32, "r"), ((s, d), MM_DTYPE, "r"), ((1, d), F32, "a")])


def _final_fwd(x, o, g):
    return _resid_norm(x, [o], 0.5, g, F32)


def _loss(y, t):
    s, d = y.shape

    def fn(i, n, y_ref, t_ref, dy_ref, l_ref):
        e = y_ref[...] - t_ref[...]
        dy_ref[...] = e * (1.0 / d)
        _acc(l_ref, i, (0.5 / d) * jnp.sum(jnp.sum(e * e, axis=-1, keepdims=True), axis=0, keepdims=True))

    return _rows("loss", fn, s, _tile(s, 256), [(y, "r"), (t, "r")], [((s, d), F32, "r"), ((1, 1), F32, "a")])


def _mla_pre(p, gql, gkvl, cw):
    s = p.shape[0]
    ql, kvl = gql.shape[-1], gkvl.shape[-1]
    o_q, o_kv = 2 * cw, 2 * cw + ql

    def fn(i, n, p_ref, gq_ref, gkv_ref, cq_ref, ckv_ref, ckvt_ref):
        cq = p_ref[:, o_q:o_q + ql]
        cq_ref[...] = (cq * _rstd(cq, ql) * gq_ref[...]).astype(cq_ref.dtype)
        ckv = p_ref[:, o_kv:o_kv + kvl]
        y = ckv * _rstd(ckv, kvl) * gkv_ref[...]
        ckv_ref[...] = y.astype(ckv_ref.dtype)
        ckvt_ref[...] = y.T.astype(ckvt_ref.dtype)

    return _rows("mla_pre", fn, s, _tile(s, 256), [(p, "r"), (gql, "f"), (gkvl, "f")],
                 [((s, ql), MM_DTYPE, "r"), ((s, kvl), MM_DTYPE, "r"), ((kvl, s), MM_DTYPE, "t")])


def _rope(x, cosf, s1, s2):
    return x * cosf + pltpu.roll(x, HEAD_PAD - 16, 1) * s1 + pltpu.roll(x, 16, 1) * s2


def _rope_bwd(d, cosf, s1, s2):
    return d * cosf + pltpu.roll(d * s1, 16, 1) + pltpu.roll(d * s2, HEAD_PAD - 16, 1)


def _qk_post(q_pre, k_pre, p, pe_block, tabs, gq, gk):
    s, hw = q_pre.shape
    cosf, s1, s2 = tabs

    tm = _tile(s, 256)

    def fn(i, n, q_ref, k_ref, pe_ref, c_ref, s1_ref, s2_ref, gq_ref, gk_ref, qo_ref, ko_ref, kt_ref):
        c, a1, a2, pe = c_ref[...], s1_ref[...], s2_ref[...], pe_ref[...]
        for hh in range(N_HEADS):
            cols = slice(hh * HEAD_PAD, (hh + 1) * HEAD_PAD)
            qv = q_ref[:, cols]
            qo_ref[:, cols] = _rope(qv * _rstd(qv, QK_DIM) * gq_ref[...], c, a1, a2).astype(qo_ref.dtype)
            kv = k_ref[:, cols] + pe
            kout = _rope(kv * _rstd(kv, QK_DIM) * gk_ref[...], c, a1, a2)
            ko_ref[:, cols] = kout.astype(ko_ref.dtype)
            kt_ref[cols, :] = kout.T.astype(kt_ref.dtype)

    return _rows("qk_post", fn, s, tm,
                 [(q_pre, "r"), (k_pre, "r"), (p, ("c", pe_block)), (cosf, "r"), (s1, "r"), (s2, "r"), (gq, "f"), (gk, "f")],
                 [((s, hw), MM_DTYPE, "r"), ((s, hw), MM_DTYPE, "r"), ((s // tm, hw, tm), MM_DTYPE, "b")])


def _qk_post_bwd(q_pre, k_pre, p, pe_block, tabs, gq, gk, dq, dk):
    s, hw = q_pre.shape
    cosf, s1, s2 = tabs

    def fn(i, n, q_ref, k_ref, pe_ref, c_ref, s1_ref, s2_ref, gq_ref, gk_ref, dq_ref, dk_ref,
           dqo_ref, dko_ref, dpe_ref, dgq_ref, dgk_ref):
        c, a1, a2, pe = c_ref[...], s1_ref[...], s2_ref[...], pe_ref[...]
        lane = lax.broadcasted_iota(jnp.int32, pe.shape, 1)
        is_pe = (lane >= QK_NOPE) & (lane < QK_DIM)
        dpe = jnp.zeros(pe.shape, F32)
        dgq = jnp.zeros((1, HEAD_PAD), F32)
        dgk = jnp.zeros((1, HEAD_PAD), F32)
        for hh in range(N_HEADS):
            cols = slice(hh * HEAD_PAD, (hh + 1) * HEAD_PAD)
            dxq, g1 = _rms_bwd(q_ref[:, cols], gq_ref[...], _rope_bwd(dq_ref[cols, :].T, c, a1, a2), QK_DIM)
            dqo_ref[:, cols] = dxq.astype(dqo_ref.dtype)
            dxk, g2 = _rms_bwd(k_ref[:, cols] + pe, gk_ref[...], _rope_bwd(dk_ref[:, cols], c, a1, a2), QK_DIM)
            dko_ref[:, cols] = dxk.astype(dko_ref.dtype)
            dpe = dpe + dxk
            dgq = dgq + g1
            dgk = dgk + g2
        dpe_ref[...] = jnp.where(is_pe, dpe, 0.0)
        _acc(dgq_ref, i, dgq)
        _acc(dgk_ref, i, dgk)

    return _rows("qk_post_bwd", fn, s, _tile(s, 256),
                 [(q_pre, "r"), (k_pre, "r"), (p, ("c", pe_block)), (cosf, "r"), (s1, "r"), (s2, "r"), (gq, "f"), (gk, "f"),
                  (dq, "b"), (dk, "r")],
                 [((s, hw), MM_DTYPE, "r"), ((s, hw), MM_DTYPE, "r"), ((s, HEAD_PAD), F32, "r"),
                  ((1, HEAD_PAD), F32, "a"), ((1, HEAD_PAD), F32, "a")])


def _chunk_mask(tq, key_rows):
    r = jnp.right_shift(lax.broadcasted_iota(jnp.int32, (tq, tq), 0), CHUNK_SHIFT)
    c = jnp.right_shift(lax.broadcasted_iota(jnp.int32, (tq, tq), 1), CHUNK_SHIFT)
    return (r <= c) if key_rows else (c <= r)


def _attn_specs(s, tq):
    w = HEADS_PER_STEP * HEAD_PAD
    tile = pl.BlockSpec((tq, w), lambda hb, i: (i, hb))
    whole = pl.BlockSpec((s, w), lambda hb, i: (0, hb))
    rows = pl.BlockSpec((None, None, HEADS_PER_STEP, tq), lambda hb, i: (hb, i, 0, 0))
    blocks = pl.BlockSpec((s // tq, w, tq), lambda hb, i: (0, hb, 0))
    return tile, whole, rows, blocks


def _head(hh):
    return slice(hh * HEAD_PAD, (hh + 1) * HEAD_PAD)


def _attn_fwd(q, k, vt):
    s, hw = q.shape
    tq = _tile(s, 256)
    nq = s // tq
    groups = N_HEADS // HEADS_PER_STEP
    scale = QK_DIM ** -0.5
    tile, whole, rows, blocks = _attn_specs(s, tq)

    def body(q_ref, k_ref, vt_ref, o_ref, lse_ref):
        qi = pl.program_id(1)
        qv = [q_ref[:, _head(hh)] for hh in range(HEADS_PER_STEP)]

        def scores(kb):
            off = pl.multiple_of(kb * tq, tq)
            return tuple(lax.dot_general(k_ref[pl.ds(off, tq), _head(hh)], qv[hh], _DIMS["nt"],
                                         preferred_element_type=F32) for hh in range(HEADS_PER_STEP))

        def update(kb, stats, raw, diag):
            heads = range(HEADS_PER_STEP)
            mn, al, pr, ln = [], [], [], []
            for hh in heads:
                m, l, _ = stats[hh]
                sc = raw[hh] * (scale * LOG2E)
                if diag:
                    sc = jnp.where(_chunk_mask(tq, True), sc, -1e30)
                mn.append(jnp.maximum(m, jnp.max(sc, axis=0, keepdims=True)))
                al.append(jnp.exp2(m - mn[hh]))
                p = jnp.exp2(sc - mn[hh])
                ln.append(al[hh] * l + jnp.sum(p, axis=0, keepdims=True))
                pr.append(p.astype(MM_DTYPE))
            return tuple((mn[hh], ln[hh], al[hh] * stats[hh][2]
                          + jnp.dot(vt_ref[kb, _head(hh), :], pr[hh], preferred_element_type=F32)) for hh in heads)

        def step(kb, carry):
            stats, raw = carry
            nxt = scores(kb + 1)
            return update(kb, stats, raw, False), nxt

        init = tuple((jnp.full((1, tq), -1e30, F32), jnp.zeros((1, tq), F32), jnp.zeros((HEAD_PAD, tq), F32))
                     for _ in range(HEADS_PER_STEP))
        stats, raw = lax.fori_loop(0, qi, step, (init, scores(0)))
        carry = update(qi, stats, raw, True)
        for hh in range(HEADS_PER_STEP):
            m, l, acc = carry[hh]
            o_ref[:, _head(hh)] = (acc / l).T
            lse_ref[hh:hh + 1, :] = m + jnp.log(l) * LOG2E

    return pl.pallas_call(
        body, name="attn_fwd", grid=(groups, nq),
        in_specs=[tile, whole, blocks], out_specs=[tile, rows],
        out_shape=[jax.ShapeDtypeStruct((s, hw), F32), jax.ShapeDtypeStruct((groups, nq, HEADS_PER_STEP, tq), F32)],
        compiler_params=_params(2))(q, k, vt)


def _attn_delta(o, do):
    s, hw = o.shape
    tq = _tile(s, 256)
    groups = N_HEADS // HEADS_PER_STEP
    tile, _, rows, _ = _attn_specs(s, tq)

    def body(o_ref, do_ref, d_ref):
        for hh in range(HEADS_PER_STEP):
            prod = o_ref[:, _head(hh)] * do_ref[:, _head(hh)]
            d_ref[hh:hh + 1, :] = jnp.sum(prod.T, axis=0, keepdims=True)

    return pl.pallas_call(
        body, name="attn_delta", grid=(groups, s // tq), in_specs=[tile, tile], out_specs=rows,
        out_shape=jax.ShapeDtypeStruct((groups, s // tq, HEADS_PER_STEP, tq), F32),
        compiler_params=_params(2))(o, do)


def _attn_bwd(q, k, kt, v, do, lse_row, delta_row):
    s, hw = q.shape
    tq = _tile(s, 256)
    nq = s // tq
    groups = N_HEADS // HEADS_PER_STEP
    w = HEADS_PER_STEP * HEAD_PAD
    scale = QK_DIM ** -0.5
    tile, whole, _, blocks = _attn_specs(s, tq)
    rowvec = pl.BlockSpec((None, nq, HEADS_PER_STEP, tq), lambda hb, i: (hb, 0, 0, 0))
    kt_tile = pl.BlockSpec((None, w, tq), lambda hb, i: (i, hb, 0))

    def body(q_ref, k_ref, kt_ref, v_ref, do_ref, lse_ref, delta_ref, dqt_ref, dk_ref, dv_ref):
        ki = pl.program_id(1)
        kv = [k_ref[:, _head(hh)] for hh in range(HEADS_PER_STEP)]
        vv = [v_ref[:, _head(hh)] for hh in range(HEADS_PER_STEP)]
        ktv = [kt_ref[_head(hh), :] for hh in range(HEADS_PER_STEP)]

        @pl.when(ki == 0)
        def _():
            dqt_ref[...] = jnp.zeros(dqt_ref.shape, F32)

        def block(qb, carry, diag):
            off = pl.multiple_of(qb * tq, tq)
            heads = range(HEADS_PER_STEP)
            qs = [q_ref[pl.ds(off, tq), _head(hh)] for hh in heads]
            dos = [do_ref[pl.ds(off, tq), _head(hh)].astype(MM_DTYPE) for hh in heads]
            raw = [lax.dot_general(kv[hh], qs[hh], _DIMS["nt"], preferred_element_type=F32) for hh in heads]
            dpr = [lax.dot_general(vv[hh], dos[hh], _DIMS["nt"], preferred_element_type=F32) for hh in heads]
            pr, ds = [], []
            for hh in heads:
                sc = raw[hh] * (scale * LOG2E)
                if diag:
                    sc = jnp.where(_chunk_mask(tq, True), sc, -1e30)
                p = jnp.exp2(sc - lse_ref[qb, hh:hh + 1, :])
                pr.append(p.astype(MM_DTYPE))
                ds.append((p * (dpr[hh] - delta_ref[qb, hh:hh + 1, :]) * scale).astype(MM_DTYPE))
            out = []
            for hh in heads:
                dk, dvv = carry[hh]
                dvv = dvv + jnp.dot(pr[hh], dos[hh], preferred_element_type=F32)
                dk = dk + jnp.dot(ds[hh], qs[hh], preferred_element_type=F32)
                dqt_ref[qb, _head(hh), :] += jnp.dot(ktv[hh], ds[hh], preferred_element_type=F32)
                out.append((dk, dvv))
            return tuple(out)

        init = tuple((jnp.zeros((tq, HEAD_PAD), F32), jnp.zeros((tq, HEAD_PAD), F32)) for _ in range(HEADS_PER_STEP))
        carry = lax.fori_loop(ki + 1, nq, lambda qb, c: block(qb, c, False), block(ki, init, True))
        for hh in range(HEADS_PER_STEP):
            dk_ref[:, _head(hh)] = carry[hh][0]
            dv_ref[:, _head(hh)] = carry[hh][1].astype(dv_ref.dtype)

    return pl.pallas_call(
        body, name="attn_bwd", grid=(groups, nq),
        in_specs=[whole, tile, kt_tile, tile, whole, rowvec, rowvec], out_specs=[blocks, tile, tile],
        out_shape=[jax.ShapeDtypeStruct((nq, hw, tq), F32), jax.ShapeDtypeStruct((s, hw), F32),
                   jax.ShapeDtypeStruct((s, hw), MM_DTYPE)],
        compiler_params=_params(2))(q, k, kt, v, do, lse_row, delta_row)


def _glu_with_halo(i, p_ref, ph_ref, ubuf, cw, tm):
    a, g = p_ref[:, 0:cw], p_ref[:, cw:2 * cw]
    ubuf[HALO:HALO + tm, :] = a * _sigmoid(g)
    ah, gh = ph_ref[:, 0:cw], ph_ref[:, cw:2 * cw]
    ubuf[0:HALO, :] = jnp.where(i > 0, ah * _sigmoid(gh), 0.0)
    return a, g


def _conv_taps(ubuf, w_ref, tm):
    base = HALO - (CONV_K - 1)
    y = w_ref[0:1, :] * ubuf[base:base + tm, :]
    for k in range(1, CONV_K):
        y = y + w_ref[k:k + 1, :] * ubuf[base + k:base + k + tm, :]
    return y


def _layer_norm_stats(y, cw):
    mu = jnp.sum(y, axis=-1, keepdims=True) * (1.0 / cw)
    yc = y - mu
    r = lax.rsqrt(jnp.sum(yc * yc, axis=-1, keepdims=True) * (1.0 / cw) + EPS)
    return yc * r, r


def _conv_fwd(p, w, cb, lg, lb):
    s = p.shape[0]
    cw = cb.shape[-1]
    tm = _tile(s, 256)

    def fn(i, n, p_ref, ph_ref, w_ref, cb_ref, lg_ref, lb_ref, o_ref, ubuf):
        _glu_with_halo(i, p_ref, ph_ref, ubuf, cw, tm)
        yh, _ = _layer_norm_stats(_conv_taps(ubuf, w_ref, tm) + cb_ref[...], cw)
        z = yh * lg_ref[...] + lb_ref[...]
        o_ref[...] = (z * _sigmoid(z)).astype(o_ref.dtype)

    return _rows("conv_fwd", fn, s, tm, [(p, "r"), (p, "p"), (w, "f"), (cb, "f"), (lg, "f"), (lb, "f")],
                 [((s, cw), MM_DTYPE, "r")], scratch=[pltpu.VMEM((tm + HALO, cw), F32)])[0]


def _conv_bwd_ln(p, w, cb, lg, lb, dc):
    s = p.shape[0]
    cw = cb.shape[-1]
    tm = _tile(s, 256)

    def fn(i, n, p_ref, ph_ref, w_ref, cb_ref, lg_ref, lb_ref, dc_ref, dy_ref, dlg_ref, dlb_ref, dcb_ref, ubuf):
        _glu_with_halo(i, p_ref, ph_ref, ubuf, cw, tm)
        yh, r = _layer_norm_stats(_conv_taps(ubuf, w_ref, tm) + cb_ref[...], cw)
        z = yh * lg_ref[...] + lb_ref[...]
        sg = _sigmoid(z)
        dz = dc_ref[...] * (sg * (1.0 + z * (1.0 - sg)))
        dyh = dz * lg_ref[...]
        m1 = jnp.sum(dyh, axis=-1, keepdims=True) * (1.0 / cw)
        m2 = jnp.sum(dyh * yh, axis=-1, keepdims=True) * (1.0 / cw)
        dy = r * (dyh - m1 - yh * m2)
        dy_ref[...] = dy
        _acc(dlg_ref, i, _sum0(dz * yh))
        _acc(dlb_ref, i, _sum0(dz))
        _acc(dcb_ref, i, _sum0(dy))

    return _rows("conv_bwd_ln", fn, s, tm,
                 [(p, "r"), (p, "p"), (w, "f"), (cb, "f"), (lg, "f"), (lb, "f"), (dc, "r")],
                 [((s, cw), F32, "r"), ((1, cw), F32, "a"), ((1, cw), F32, "a"), ((1, cw), F32, "a")],
                 scratch=[pltpu.VMEM((tm + HALO, cw), F32)])


def _conv_bwd_taps(p, w, dy):
    s = p.shape[0]
    cw = w.shape[-1]
    tm = _tile(s, 256)

    def fn(i, n, p_ref, ph_ref, w_ref, dy_ref, dyn_ref, dp_ref, dw_ref, ubuf, dybuf):
        a, g = _glu_with_halo(i, p_ref, ph_ref, ubuf, cw, tm)
        dyv = dy_ref[...]
        dybuf[0:tm, :] = dyv
        dybuf[tm:tm + HALO, :] = jnp.where(i < n - 1, dyn_ref[...], 0.0)
        base = HALO - (CONV_K - 1)
        du = jnp.zeros((tm, cw), F32)

        @pl.when(i == 0)
        def _():
            dw_ref[...] = jnp.zeros(dw_ref.shape, F32)

        for k in range(CONV_K):
            sh = CONV_K - 1 - k
            du = du + w_ref[k:k + 1, :] * dybuf[sh:sh + tm, :]
            dw_ref[k:k + 1, :] += _sum0(dyv * ubuf[base + k:base + k + tm, :])
        sg = _sigmoid(g)
        dp_ref[:, 0:cw] = (du * sg).astype(dp_ref.dtype)
        dp_ref[:, cw:2 * cw] = (du * a * sg * (1.0 - sg)).astype(dp_ref.dtype)

    return _rows("conv_bwd_taps", fn, s, tm, [(p, "r"), (p, "p"), (w, "f"), (dy, "r"), (dy, "n")],
                 [((s, 2 * cw), MM_DTYPE, "r"), ((CONV_K + 1, cw), F32, "a")],
                 scratch=[pltpu.VMEM((tm + HALO, cw), F32), pltpu.VMEM((tm + HALO, cw), F32)])


def _dp_assemble(p, gql, gkvl, dcq, dckv_parts, dp_conv, dpe, cw):
    s, width = p.shape
    ql, kvl = gql.shape[-1], gkvl.shape[-1]
    o_q, o_kv, o_pe = 2 * cw, 2 * cw + ql, 2 * cw + ql + kvl

    def fn(i, n, p_ref, gq_ref, gkv_ref, dcq_ref, dk1_ref, dk2_ref, dconv_ref, dpe_ref, dp_ref, dgq_ref, dgkv_ref):
        dp_ref[:, 0:o_q] = dconv_ref[...]
        dx, dg = _rms_bwd(p_ref[:, o_q:o_kv], gq_ref[...], dcq_ref[...], ql)
        dp_ref[:, o_q:o_kv] = dx.astype(dp_ref.dtype)
        _acc(dgq_ref, i, dg)
        dx, dg = _rms_bwd(p_ref[:, o_kv:o_pe], gkv_ref[...], dk1_ref[...] + dk2_ref[...], kvl)
        dp_ref[:, o_kv:o_pe] = dx.astype(dp_ref.dtype)
        _acc(dgkv_ref, i, dg)
        dp_ref[:, o_pe:width] = dpe_ref[...].astype(dp_ref.dtype)

    return _rows("dp_assemble", fn, s, _tile(s, 256),
                 [(p, "r"), (gql, "f"), (gkvl, "f"), (dcq, "r"), (dckv_parts[0], "r"), (dckv_parts[1], "r"),
                  (dp_conv, "r"), (dpe, "r")],
                 [((s, width), MM_DTYPE, "r"), ((1, ql), F32, "a"), ((1, kvl), F32, "a")])


def _add_leading(name, x, out_dtype):
    n, r, c = x.shape
    tm = _tile(r, 256)

    def fn(i, nt, x_ref, o_ref):
        tot = x_ref[0].astype(F32)
        for k in range(1, n):
            tot = tot + x_ref[k].astype(F32)
        o_ref[...] = tot.astype(o_ref.dtype)

    return _rows(name, fn, r, tm, [(x, "r")], [((r, c), out_dtype, "r")])[0]


def _prefetch_call(name, fn, idx, grid, in_specs, out_spec, out_shape, args, aliases=None):
    return pl.pallas_call(
        fn, name=name, out_shape=out_shape,
        grid_spec=pltpu.PrefetchScalarGridSpec(num_scalar_prefetch=1, grid=grid, in_specs=in_specs, out_specs=out_spec),
        input_output_aliases=aliases or {}, compiler_params=_params(len(grid)))(idx, *args)


def _adamw(w, g, m, v):
    r, c = w.shape
    tm = _tile(r, 256)

    def fn(i, n, w_ref, g_ref, m_ref, v_ref, d_ref, mo_ref, vo_ref):
        gv = g_ref[...]
        mn = ADAM_B1 * m_ref[...] + (1.0 - ADAM_B1) * gv
        vn = ADAM_B2 * v_ref[...] + (1.0 - ADAM_B2) * (gv * gv)
        mo_ref[...] = mn
        vo_ref[...] = vn
        m_hat = mn / (1.0 - ADAM_B1 ** ADAM_STEP)
        v_hat = vn / (1.0 - ADAM_B2 ** ADAM_STEP)
        d_ref[...] = -ADAM_LR * (m_hat / (jnp.sqrt(v_hat) + ADAM_EPS) + ADAM_WD * w_ref[...])

    return _rows("adamw", fn, r, tm, [(w, "r"), (g, "r"), (m, "r"), (v, "r")], [((r, c), F32, "r")] * 3)


def _place():
    x, y, c = lax.axis_index("x"), lax.axis_index("y"), lax.axis_index("c")
    other_chips = [(1 - x, y), (x, 1 - y), (1 - x, 1 - y)]
    return x, y, c, 2 * x + y, other_chips


def _gather_small(vec):
    flips = [(fx, fy, fc) for fx in (0, 1) for fy in (0, 1) for fc in (0, 1)][1:]

    def body(src, dst, send, recv, lsem):
        x, y, c, _, _ = _place()
        me = 4 * x + 2 * y + c
        local = pltpu.make_async_copy(src, dst.at[me], lsem)
        local.start()
        away = []
        for j, (fx, fy, fc) in enumerate(flips):
            px, py, pc = x ^ fx, y ^ fy, c ^ fc
            away.append(pltpu.make_async_remote_copy(
                src_ref=src, dst_ref=dst.at[me], send_sem=send.at[j], recv_sem=recv.at[j],
                device_id=(px, py, pc), device_id_type=MESH))
        for cp in away:
            cp.start()
        for j, (fx, fy, fc) in enumerate(flips):
            blk = dst.at[4 * (x ^ fx) + 2 * (y ^ fy) + (c ^ fc)]
            pltpu.make_async_remote_copy(
                src_ref=blk, dst_ref=blk, send_sem=send.at[j], recv_sem=recv.at[j],
                device_id=(x ^ fx, y ^ fy, c ^ fc), device_id_type=MESH).wait_recv()
        for cp in away:
            cp.wait_send()
        local.wait()

    return pl.pallas_call(
        body, name="gather_small", in_specs=[ANY], out_specs=ANY,
        out_shape=jax.ShapeDtypeStruct((8,) + vec.shape, vec.dtype),
        scratch_shapes=[pltpu.SemaphoreType.DMA((7,)), pltpu.SemaphoreType.DMA((7,)), pltpu.SemaphoreType.DMA(())],
        compiler_params=pltpu.CompilerParams(has_side_effects=True))(vec)


HBM = pl.BlockSpec(memory_space=pltpu.HBM)
SEM = pl.BlockSpec(memory_space=pltpu.SEMAPHORE)
EFFECT = pltpu.SideEffectType.DATAFLOW_SIDE_EFFECTING


def _exchange(name, mode, arrays, sem_shape, plan, sems=None, after=None):
    n = len(arrays)
    hbm = [pltpu.HBM(a.shape, a.dtype) for a in arrays]
    arrays = [pltpu.with_memory_space_constraint(a, pltpu.HBM) for a in arrays]
    if mode == "run":
        def body(*refs):
            outs, send, recv = refs[n:2 * n], refs[2 * n], refs[2 * n + 1]
            sends, recvs = plan(outs, send, recv)
            for cp in sends:
                cp.start()
            for cp in recvs:
                cp.wait_recv()
            for cp in sends:
                cp.wait_send()

        return list(pl.pallas_call(
            body, name=name, in_specs=[HBM] * n, out_specs=[HBM] * n, out_shape=hbm,
            input_output_aliases={i: i for i in range(n)},
            scratch_shapes=[pltpu.SemaphoreType.DMA(sem_shape), pltpu.SemaphoreType.DMA(sem_shape)],
            compiler_params=pltpu.CompilerParams(has_side_effects=EFFECT))(*arrays))
    if mode == "start":
        def body(*refs):
            send, recv, token, outs = refs[n], refs[n + 1], refs[n + 2], refs[n + 3:]
            sends, _ = plan(outs, send, recv)
            for cp in sends:
                cp.start()
            token[...] = jnp.zeros(token.shape, F32)

        res = pl.pallas_call(
            body, name=name, in_specs=[HBM] * n,
            out_specs=[SEM, SEM, pl.BlockSpec(memory_space=pltpu.VMEM)] + [HBM] * n,
            out_shape=[pltpu.SemaphoreType.DMA(sem_shape), pltpu.SemaphoreType.DMA(sem_shape),
                       jax.ShapeDtypeStruct((8, HEAD_PAD), F32)] + hbm,
            input_output_aliases={i: i + 3 for i in range(n)},
            compiler_params=pltpu.CompilerParams(has_side_effects=EFFECT))(*arrays)
        return (res[0], res[1]), res[2], list(res[3:])

    def body(*refs):
        ins, send, recv = refs[:n], refs[n], refs[n + 1]
        sends, recvs = plan(ins, send, recv)
        for cp in sends:
            cp.wait_send()
        for cp in recvs:
            cp.wait_recv()

    return list(pl.pallas_call(
        body, name=name, in_specs=[HBM] * n + [SEM, SEM, ANY], out_specs=[HBM] * n, out_shape=hbm,
        input_output_aliases={i: i for i in range(n)},
        compiler_params=pltpu.CompilerParams(has_side_effects=EFFECT))(*arrays, sems[0], sems[1], after))


def _rows_half(ref_rows, half):
    h = ref_rows // 2
    return pl.ds(pl.multiple_of(half * h, 8), h)


def _remote(src, dst, send_sem, recv_sem, device):
    return pltpu.make_async_remote_copy(src_ref=src, dst_ref=dst, send_sem=send_sem, recv_sem=recv_sem,
                                        device_id=device, device_id_type=MESH)


def _plan_gather_ici(k_n, l):
    def plan(refs, send, recv):
        x, y, c, me, chips = _place()
        sends, recvs = [], []
        for k in range(k_n):
            src, buf = refs[k], refs[k_n + k]
            rows = _rows_half(src.shape[1], c)
            for j, (cx, cy) in enumerate(chips):
                sends.append(_remote(src.at[l, rows], buf.at[0, me, rows], send.at[3 * k + j], recv.at[3 * k + j], (cx, cy, c)))
                got = buf.at[0, 2 * cx + cy, rows]
                recvs.append(_remote(got, got, send.at[3 * k + j], recv.at[3 * k + j], (cx, cy, c)))
        return sends, recvs
    return plan


def _plan_gather_d2d(k_n):
    def plan(refs, send, recv):
        x, y, c, me, chips = _place()
        sends, recvs = [], []
        for k in range(k_n):
            buf = refs[k]
            for j, (cx, cy) in enumerate(chips):
                mine = buf.at[0, 2 * cx + cy, _rows_half(buf.shape[2], c)]
                theirs = buf.at[0, 2 * cx + cy, _rows_half(buf.shape[2], 1 - c)]
                sends.append(_remote(mine, mine, send.at[3 * k + j], recv.at[3 * k + j], (x, y, 1 - c)))
                recvs.append(_remote(theirs, theirs, send.at[3 * k + j], recv.at[3 * k + j], (x, y, 1 - c)))
        return sends, recvs
    return plan


def _plan_swap(k_n):
    def plan(refs, send, recv):
        x, y, c, me, chips = _place()
        sends, recvs = [], []
        for k in range(k_n):
            src, land = refs[k], refs[k_n + k]
            sends.append(_remote(src.at[:, _rows_half(src.shape[1], 1 - c)], land, send.at[k], recv.at[k], (x, y, 1 - c)))
            recvs.append(_remote(land, land, send.at[k], recv.at[k], (x, y, 1 - c)))
        return sends, recvs
    return plan


def _plan_scatter(k_n):
    def plan(refs, send, recv):
        x, y, c, me, chips = _place()
        sends, recvs = [], []
        for k in range(k_n):
            src, land = refs[k], refs[k_n + k]
            for j, (cx, cy) in enumerate(chips):
                sends.append(_remote(src.at[2 * cx + cy], land.at[j], send.at[3 * k + j], recv.at[3 * k + j], (cx, cy, c)))
                recvs.append(_remote(land.at[j], land.at[j], send.at[3 * k + j], recv.at[3 * k + j], (cx, cy, c)))
        return sends, recvs
    return plan


def _plan_share(k_n, layers):
    def plan(refs, send, recv):
        x, y, c, me, chips = _place()
        sends, recvs = [], []
        for k in range(k_n):
            out = refs[k]
            for l in range(layers):
                mine = out.at[l, _rows_half(out.shape[1], c)]
                theirs = out.at[l, _rows_half(out.shape[1], 1 - c)]
                sends.append(_remote(mine, mine, send.at[layers * k + l], recv.at[layers * k + l], (x, y, 1 - c)))
                recvs.append(_remote(theirs, theirs, send.at[layers * k + l], recv.at[layers * k + l], (x, y, 1 - c)))
        return sends, recvs
    return plan


def _add_sibling_half(idx, mine, theirs):
    q, r, c = mine.shape
    h = r // 2
    tm = _tile(h, 256)

    def fn(idx_ref, a_ref, b_ref, o_ref):
        o_ref[...] = (a_ref[...].astype(F32) + b_ref[...].astype(F32)).astype(o_ref.dtype)

    return _prefetch_call(
        "add_sibling", fn, idx, (q, h // tm),
        [pl.BlockSpec((None, None, tm, c), lambda b, i, ix: (b, ix[0], i, 0)),
         pl.BlockSpec((None, tm, c), lambda b, i, ix: (b, i, 0))],
        pl.BlockSpec((None, tm, c), lambda b, i, ix: (b, i, 0)), jax.ShapeDtypeStruct((q, h, c), WIRE_DTYPE),
        [mine.reshape(q, 2, h, c), theirs])


def _add_chips_half(idx, own, others, layers, l, into):
    _, h, c = own.shape
    tm = _tile(h, 256)

    def fn(*refs):
        a_ref, b_ref, o_ref = refs[-3:]
        tot = a_ref[...].astype(F32)
        for k in range(3):
            tot = tot + b_ref[k].astype(F32)
        o_ref[...] = tot

    in_specs = [pl.BlockSpec((None, tm, c), lambda i, ix: (ix[1], i, 0)), pl.BlockSpec((3, tm, c), lambda i, ix: (0, i, 0))]
    args = [own, others]
    aliases = None
    if into is not None:
        in_specs = [ANY] + in_specs
        args = [into.reshape(layers, 2, h, c)] + args
        aliases = {1: 0}
    out = _prefetch_call(
        "add_chips", fn, idx, (h // tm,), in_specs,
        pl.BlockSpec((None, None, tm, c), lambda i, ix: (l, ix[0], i, 0)),
        jax.ShapeDtypeStruct((layers, 2, h, c), F32), args, aliases)
    return out.reshape(layers, 2 * h, c)


SHARDED = ["ffn1_w_gate", "ffn1_w_up", "ffn1_w_down", "w_in", "w_uq", "w_ukv", "conv_w", "w_out",
           "ffn2_w_gate", "ffn2_w_up", "ffn2_w_down"]
SMALL = ["ffn1_norm", "mix_norm", "q_latent_norm", "kv_latent_norm", "q_norm", "k_norm", "conv_b", "conv_ln_g",
         "conv_ln_b", "ffn2_norm", "post_norm"]
ALL_WEIGHTS = ["ffn1_norm", "ffn1_w_gate", "ffn1_w_up", "ffn1_w_down", "mix_norm", "w_in", "q_latent_norm", "w_uq",
               "kv_latent_norm", "w_ukv", "q_norm", "k_norm", "conv_w", "conv_b", "conv_ln_g", "conv_ln_b", "w_out",
               "ffn2_norm", "ffn2_w_gate", "ffn2_w_up", "ffn2_w_down", "post_norm"]


def _rope_tables(s):
    pos = jnp.arange(s, dtype=F32)
    inv_freq = 1.0 / (ROPE_THETA ** (jnp.arange(0, QK_ROPE, 2, dtype=F32) / QK_ROPE))
    ang = pos[:, None] * inv_freq[None, :]
    cos, sin = jnp.cos(ang), jnp.sin(ang)
    half = QK_ROPE // 2
    one = jnp.ones((s, QK_NOPE), F32)
    zero = jnp.zeros((s, QK_NOPE), F32)
    pad = HEAD_PAD - QK_DIM
    z16 = jnp.zeros((s, half), F32)
    cosf = jnp.concatenate([one, cos, cos, jnp.ones((s, pad), F32)], axis=1)
    s1 = jnp.concatenate([zero, -sin, z16, jnp.zeros((s, pad), F32)], axis=1)
    s2 = jnp.concatenate([zero, z16, sin, jnp.zeros((s, pad), F32)], axis=1)
    return cosf, s1, s2


def _ffn_fwd(n, w, names, l):
    a, b, hm = _ffn_up_fused(n, w[names[0]], w[names[1]], l)
    return a, b, hm, _blocked_sum("ffn_down", [hm], [w[names[2]]], l, "nn")


def _ffn_bwd(nt, a, b, hm, do, w, names, grads):
    gate, up, down = names
    da, db = _ffn_down_t_fused(do, w[down], a, b, 0)
    grads[down] = _wgrad("ffn_down_wgrad", hm, do, WIRE_DTYPE)
    g_gate, g_up = _wgrad_pair("ffn_up_wgrad", nt, da, db, 1, 0, None)
    grads[gate], grads[up] = g_gate[0], g_up[0]
    return [_blocked_sum("ffn_up_t", [da, db], [w[gate], w[up]], 0, "nt")]


def _vec(v, l):
    return v[l][None, :]


def _pad_lanes(v, width):
    return jnp.pad(v, [(0, 0)] * (v.ndim - 1) + [(0, width - v.shape[-1])])


def kernel(x, ffn1_norm, ffn1_w_gate, ffn1_w_up, ffn1_w_down, mix_norm, w_in, q_latent_norm, w_uq, kv_latent_norm, w_ukv, q_norm, k_norm, conv_w, conv_b, conv_ln_g, conv_ln_b, w_out, ffn2_norm, ffn2_w_gate, ffn2_w_up, ffn2_w_down, post_norm, loss_target, m_ffn1_norm, m_ffn1_w_gate, m_ffn1_w_up, m_ffn1_w_down, m_mix_norm, m_w_in, m_q_latent_norm, m_w_uq, m_kv_latent_norm, m_w_ukv, m_q_norm, m_k_norm, m_conv_w, m_conv_b, m_conv_ln_g, m_conv_ln_b, m_w_out, m_ffn2_norm, m_ffn2_w_gate, m_ffn2_w_up, m_ffn2_w_down, m_post_norm, v_ffn1_norm, v_ffn1_w_gate, v_ffn1_w_up, v_ffn1_w_down, v_mix_norm, v_w_in, v_q_latent_norm, v_w_uq, v_kv_latent_norm, v_w_ukv, v_q_norm, v_k_norm, v_conv_w, v_conv_b, v_conv_ln_g, v_conv_ln_b, v_w_out, v_ffn2_norm, v_ffn2_w_gate, v_ffn2_w_up, v_ffn2_w_down, v_post_norm):
    given = dict(locals())
    wts = {n: given[n] for n in ALL_WEIGHTS}
    mom = {n: given["m_" + n] for n in ALL_WEIGHTS}
    var = {n: given["v_" + n] for n in ALL_WEIGHTS}
    n_layers = ffn1_norm.shape[0]
    s, d = x.shape[1], x.shape[2]
    ql, kvl, cw = q_latent_norm.shape[-1], kv_latent_norm.shape[-1], conv_b.shape[-1]
    xs = x.reshape(s, d)
    target = loss_target.reshape(s, d)

    assert n_layers == 2, "the layers' exchanges run one behind the other"
    local = [jnp.pad(wts[n], ((0, 0), (0, 1), (0, 0))).astype(MM_DTYPE) if n == "conv_w" else wts[n].astype(MM_DTYPE)
             for n in SHARDED]
    L = n_layers
    k_n = len(SHARDED)
    my_c = lax.axis_index("c")
    my_chip = 2 * lax.axis_index("x") + lax.axis_index("y")
    place = jnp.stack([my_c, my_chip]).astype(jnp.int32)
    hw = N_HEADS * HEAD_PAD
    tabs = _rope_tables(s)
    pe_block = (2 * cw + ql + kvl) // HEAD_PAD
    gq_pad, gk_pad = _pad_lanes(q_norm, HEAD_PAD), _pad_lanes(k_norm, HEAD_PAD)
    tq = _tile(s, 256)

    def own_blocks(l):
        return [lax.dynamic_update_slice(lax.empty((1, N_CHIPS) + t.shape[1:], t.dtype), t[l][None, None],
                                         (0, my_chip, 0, 0)) for t in local]

    def layouts(w):
        win = w["w_in"].transpose(0, 2, 1, 3).reshape(1, d, -1)
        zc = lambda n: jnp.zeros((1, d, n), MM_DTYPE)
        o_pe, o_cv = ql + kvl, ql + kvl + QK_ROPE
        w["w_in"] = jnp.concatenate([win[..., o_cv:], win[..., :o_pe], zc(QK_NOPE), win[..., o_pe:o_cv],
                                     zc(HEAD_PAD - QK_DIM)], axis=-1)[:, None]
        wq = w["w_uq"].transpose(0, 2, 1, 3).reshape(1, ql, N_HEADS, QK_DIM)
        w["wq"] = _pad_lanes(wq, HEAD_PAD).reshape(1, 1, ql, hw)
        wkv = w["w_ukv"].transpose(0, 2, 1, 3).reshape(1, kvl, N_HEADS, QK_NOPE + V_DIM)
        w["wk"] = _pad_lanes(wkv[..., :QK_NOPE], HEAD_PAD).reshape(1, 1, kvl, hw)
        w["wv"] = _pad_lanes(wkv[..., QK_NOPE:], HEAD_PAD).reshape(1, 1, kvl, hw)
        w["wvt"] = w["wv"][:, 0].transpose(0, 2, 1)
        w["conv"] = w["conv_w"].transpose(0, 2, 1, 3).reshape(CONV_K + 1, cw).astype(F32)
        wout = w["w_out"].reshape(1, d, d)
        wo_a = wout[:, :N_HEADS * V_DIM].reshape(1, N_HEADS, V_DIM, d)
        w["wo_a"] = jnp.pad(wo_a, ((0, 0), (0, 0), (0, HEAD_PAD - V_DIM), (0, 0))).reshape(1, 1, hw, d)
        w["wo_c"] = wout[:, N_HEADS * V_DIM:][:, None]
        return w

    ici, d2d, sem_kj = (lambda l: _plan_gather_ici(k_n, l)), _plan_gather_d2d(k_n), (3 * k_n,)
    n_a = 3
    n_b = k_n - n_a
    own0 = own_blocks(0)
    landed = _exchange("gather0a_ici", "run", local[:n_a] + own0[:n_a], (3 * n_a,), _plan_gather_ici(n_a, 0))
    w0 = dict(zip(SHARDED[:n_a], _exchange("gather0a_d2d", "run", landed[n_a:], (3 * n_a,), _plan_gather_d2d(n_a))))
    local = landed[:n_a] + local[n_a:]
    sems_r, token_r, rest = _exchange("gather0b_ici_start", "start", local[n_a:] + own0[n_a:], (3 * n_b,),
                                      _plan_gather_ici(n_b, 0))

    def fwd_ffn(l, cur, w, tok):
        sv = {"x": cur}
        n1, sv["n1t"] = _norm_fwd(cur, _vec(ffn1_norm, l) + tok)
        sv["a1"], sv["b1"], sv["hm1"], o1 = _ffn_fwd(n1, w, SHARDED[0:3], 0)
        sv["x1"], sv["h"] = _resid_norm(cur, [o1], 0.5, _vec(mix_norm, l), MM_DTYPE)
        return sv

    def fwd_first(l, sv, w, tok):
        sv["p"] = _proj("w_in", sv["h"], w["w_in"], 0, "nn")[0]
        sv["cq"], sv["ckv"], ckvt = _mla_pre(sv["p"], _vec(q_latent_norm, l) + tok, _vec(kv_latent_norm, l), cw)
        sv["q_pre"] = _proj("wq", sv["cq"], w["wq"], 0, "nn")[0]
        sv["k_pre"] = _proj("wk", sv["ckv"], w["wk"], 0, "nn")[0]
        sv["v"] = _proj("wv", sv["ckv"], w["wv"], 0, "nn", MM_DTYPE)[0]
        sv["vt"] = _mm("wv_transposed", w["wvt"], ckvt, "nn", (s // tq,),
                       pl.BlockSpec((None, hw, kvl), lambda i: (0, 0, 0)), pl.BlockSpec((kvl, tq), lambda i: (0, i)),
                       pl.BlockSpec((None, hw, tq), lambda i: (i, 0, 0)),
                       jax.ShapeDtypeStruct((s // tq, hw, tq), MM_DTYPE), 1, None)
        sv["q"], sv["k"], sv["kt"] = _qk_post(sv["q_pre"], sv["k_pre"], sv["p"], pe_block, tabs,
                                              _vec(gq_pad, l), _vec(gk_pad, l))
        return sv

    def fwd_attn(l, sv, w):
        sv["c"] = _conv_fwd(sv["p"], w["conv"], _vec(conv_b, l), _vec(conv_ln_g, l), _vec(conv_ln_b, l))
        sv["o"], sv["lse"] = _attn_fwd(sv["q"], sv["k"], sv.pop("vt"))
        return sv

    def fwd_second(l, sv, w, tok):
        mix_a = _contract("wo_a", sv["o"][None], w["wo_a"], 0, "nn")
        mix_c = _contract("wo_c", sv["c"][None], w["wo_c"], 0, "nn")
        sv["x2"], n2, sv["n2t"] = _resid_norm(sv["x1"], [mix_a, mix_c], 1.0, _vec(ffn2_norm, l) + tok, MM_DTYPE, True)
        sv["a2"], sv["b2"], sv["hm2"], o2 = _ffn_fwd(n2, w, SHARDED[8:11], 0)
        sv["x3"], out = _final_fwd(sv["x2"], o2, _vec(post_norm, l))
        return out

    sv0 = fwd_ffn(0, xs, w0, token_r[0, 0])
    rest = _exchange("gather0b_ici_wait", "wait", rest, (3 * n_b,), _plan_gather_ici(n_b, 0), sems=sems_r, after=sv0["h"])
    w0.update(zip(SHARDED[n_a:], _exchange("gather0b_d2d", "run", rest[n_b:], (3 * n_b,), _plan_gather_d2d(n_b))))
    w0 = layouts(w0)
    local = local[:n_a] + rest[:n_b]
    sems_a, token_a, in_flight = _exchange("gather1_ici_start", "start", local + own_blocks(1), sem_kj, ici(1))
    sv0 = fwd_attn(0, fwd_first(0, sv0, w0, token_a[0, 0]), w0)
    in_flight = _exchange("gather1_ici_wait", "wait", in_flight, sem_kj, ici(1), sems=sems_a, after=sv0["o"])
    sems_b, token_b, in_flight = _exchange("gather1_d2d_start", "start", in_flight[k_n:], sem_kj, d2d)
    cur = fwd_second(0, sv0, w0, token_b[0, 0])
    w1 = layouts(dict(zip(SHARDED, _exchange("gather1_d2d_wait", "wait", in_flight, sem_kj, d2d, sems=sems_b, after=cur))))
    sv1 = fwd_attn(1, fwd_first(1, fwd_ffn(1, cur, w1, 0.0), w1, 0.0), w1)
    cur = fwd_second(1, sv1, w1, 0.0)
    dy, loss_part = _loss(cur, target)

    small = {n: [None] * L for n in SMALL}

    def bwd_first(l, sv, dy, w, tok, grads):
        dx3, do2, small["post_norm"][l] = _norm_bwd(sv["x3"], _vec(post_norm, l) + tok, [dy], None, 0.5)
        dn2 = _ffn_bwd(sv["n2t"], sv["a2"], sv["b2"], sv["hm2"], do2, w, SHARDED[8:11], grads)
        dx2, dx2b, small["ffn2_norm"][l] = _norm_bwd(sv["x2"], _vec(ffn2_norm, l), dn2, dx3, 1.0)
        do = _proj("wo_a_t", dx2b, w["wo_a"], 0, "nt")[0]
        dc = _proj("wo_c_t", dx2b, w["wo_c"], 0, "nt")[0]
        g_wo_a = _wgrad("wo_a_wgrad", sv["o"], dx2b[None], WIRE_DTYPE)[0].reshape(N_HEADS, HEAD_PAD, d)[:, :V_DIM]
        g_wo_c = _wgrad("wo_c_wgrad", sv["c"], dx2b[None], WIRE_DTYPE)
        grads["w_out"] = jnp.concatenate(
            [g_wo_a.reshape(N_HEADS * V_DIM, d), g_wo_c[0]], axis=0).reshape(N_CHIPS, d // N_CHIPS, d)
        dq, dk, dv = _attn_bwd(sv["q"], sv["k"], sv["kt"], sv["v"], do, sv["lse"], _attn_delta(sv["o"], do))
        dq_pre, dk_pre, dpe, small["q_norm"][l], small["k_norm"][l] = _qk_post_bwd(
            sv["q_pre"], sv["k_pre"], sv["p"], pe_block, tabs, _vec(gq_pad, l), _vec(gk_pad, l), dq, dk)
        return dict(dx2=dx2, dc=dc, dq_pre=dq_pre, dk_pre=dk_pre, dv=dv, dpe=dpe)

    def bwd_second(l, sv, st, w, tok, grads):
        dq_pre, dk_pre, dv = st["dq_pre"], st["dk_pre"], st["dv"]
        dcq = _contract("wq_t", dq_pre[None], w["wq"], 0, "nt")
        dckv = [_contract("wk_t", dk_pre[None], w["wk"], 0, "nt"), _contract("wv_t", dv[None], w["wv"], 0, "nt")]
        per_head = lambda t, rows: t[0].reshape(rows, N_HEADS, HEAD_PAD)
        g_wq = per_head(_wgrad("wq_wgrad", sv["cq"], dq_pre[None], WIRE_DTYPE), ql)[..., :QK_DIM]
        grads["w_uq"] = g_wq.reshape(ql, N_CHIPS, -1).transpose(1, 0, 2)
        g_wkv = jnp.concatenate([per_head(_wgrad("wk_wgrad", sv["ckv"], dk_pre[None], WIRE_DTYPE), kvl)[..., :QK_NOPE],
                                 per_head(_wgrad("wv_wgrad", sv["ckv"], dv[None], WIRE_DTYPE), kvl)[..., :V_DIM]], axis=-1)
        grads["w_ukv"] = g_wkv.reshape(kvl, N_CHIPS, -1).transpose(1, 0, 2)
        dyc, small["conv_ln_g"][l], small["conv_ln_b"][l], small["conv_b"][l] = _conv_bwd_ln(
            sv["p"], w["conv"], _vec(conv_b, l) + tok, _vec(conv_ln_g, l), _vec(conv_ln_b, l), st["dc"])
        dp_conv, g_conv = _conv_bwd_taps(sv["p"], w["conv"], dyc)
        grads["conv_w"] = g_conv.reshape(CONV_K + 1, N_CHIPS, cw // N_CHIPS).transpose(1, 0, 2).astype(WIRE_DTYPE)
        dp, small["q_latent_norm"][l], small["kv_latent_norm"][l] = _dp_assemble(
            sv["p"], _vec(q_latent_norm, l), _vec(kv_latent_norm, l), dcq, dckv, dp_conv, st["dpe"], cw)
        dh = _contract("w_in_t", dp[None], w["w_in"], 0, "nt")
        g_win = _wgrad("w_in_wgrad", sv["h"], dp[None], WIRE_DTYPE)[0]
        o_q, o_pe_col = 2 * cw, 2 * cw + ql + kvl + QK_NOPE
        g_win = jnp.concatenate([g_win[:, o_q:o_q + ql + kvl], g_win[:, o_pe_col:o_pe_col + QK_ROPE], g_win[:, :o_q]], axis=1)
        grads["w_in"] = g_win.reshape(d, N_CHIPS, -1).transpose(1, 0, 2)
        dx1, do1, small["mix_norm"][l] = _norm_bwd(sv["x1"], _vec(mix_norm, l), [dh], st["dx2"], 0.5)
        dn1 = _ffn_bwd(sv["n1t"], sv["a1"], sv["b1"], sv["hm1"], do1, w, SHARDED[0:3], grads)
        dx, _, small["ffn1_norm"][l] = _norm_bwd(sv["x"], _vec(ffn1_norm, l), dn1, dx1, 1.0)
        return dx

    def halves(parts):
        return [lax.empty((N_CHIPS, t.shape[1] // 2, t.shape[2]), t.dtype) for t in parts]

    def thirds(parts):
        return [lax.empty((N_CHIPS - 1,) + t.shape[1:], t.dtype) for t in parts]

    def chip_sums(done):
        n = len(done) // 2
        return [_add_sibling_half(place, g, t) for g, t in zip(done[:n], done[n:])]

    def finish(done, l, into):
        n = len(done) // 2
        return [_add_chips_half(place, own, others, L, l, t) for own, others, t in zip(done[:n], done[n:], into)]

    swap, scatter, sem_k = _plan_swap(k_n), _plan_scatter(k_n), (k_n,)
    grads1, grads0 = {}, {}
    st1 = bwd_first(1, sv1, dy, w1, 0.0, grads1)
    dy = bwd_second(1, sv1, st1, w1, 0.0, grads1)
    parts1 = [grads1[n] for n in SHARDED]
    sems_s, token_s, in_flight = _exchange("swap1_start", "start", parts1 + halves(parts1), sem_k, swap)
    st0 = bwd_first(0, sv0, dy, w0, token_s[0, 0], grads0)
    sums1 = chip_sums(_exchange("swap1_wait", "wait", in_flight, sem_k, swap, sems=sems_s, after=st0["dq_pre"]))
    sems_c, token_c, in_flight = _exchange("scatter1_start", "start", sums1 + thirds(sums1), sem_kj, scatter)
    early = [n for n in SHARDED if n in grads0]
    late = [n for n in SHARDED if n not in grads0]
    n_e, n_l = len(early), len(late)
    parts_e = [grads0[n] for n in early]
    sums_e = chip_sums(_exchange("swap0_early", "run", parts_e + halves(parts_e), (n_e,), _plan_swap(n_e)))
    sems_e, token_e, early_flight = _exchange("scatter0_early_start", "start", sums_e + thirds(sums_e), (3 * n_e,),
                                              _plan_scatter(n_e))
    dy = bwd_second(0, sv0, st0, w0, token_c[0, 0] + token_e[0, 0], grads0)
    grad_x = dy.reshape(x.shape)
    out = dict(zip(SHARDED, finish(_exchange("scatter1_wait", "wait", in_flight, sem_kj, scatter, sems=sems_c, after=dy),
                                   1, [None] * k_n)))
    done = _exchange("scatter0_early_wait", "wait", early_flight, (3 * n_e,), _plan_scatter(n_e), sems=sems_e, after=dy)
    out.update(zip(early, finish(done, 0, [out[n] for n in early])))
    delta_w, new_m, new_v, full = {}, {}, {}, {}
    width = max(wts[n].shape[-1] for n in SMALL)
    pack = lambda vals: jnp.concatenate([_pad_lanes(v.reshape(-1, v.shape[-1]), width) for v in vals], axis=0)
    part = pack([jnp.concatenate(small[n], axis=0)[..., :wts[n].shape[-1]] for n in SMALL])
    rows = part.shape[0]
    part = jnp.pad(part, [(0, -rows % 8), (0, 0)])
    small_sum = _add_leading("add_devices", _gather_small(part), F32)
    pk = lambda src: jnp.pad(pack([src[n] for n in SMALL]), [(0, -rows % 8), (0, 0)])
    small_outs = _adamw(pk(wts), small_sum, pk(mom), pk(var))
    for i, n in enumerate(SMALL):
        full[n] = small_sum[i * L:(i + 1) * L, :wts[n].shape[-1]]
        delta_w[n], new_m[n], new_v[n] = [o[i * L:(i + 1) * L, :wts[n].shape[-1]] for o in small_outs]

    def update(names):
        for n in names:
            shape = wts[n].shape
            swap = shape[-1] % HEAD_PAD != 0 and shape[-2] % HEAD_PAD == 0
            view = (lambda t: jnp.swapaxes(t, 1, 2)) if swap else (lambda t: t)
            flat = lambda t: view(t).reshape(-1, view(t).shape[-1])
            r = flat(wts[n]).shape[0]
            fix = (lambda t: jnp.pad(flat(t), [(0, -r % 8), (0, 0)])) if r % 8 else flat
            g = full[n][:, :CONV_K] if n == "conv_w" else full[n]
            outs = _adamw(fix(wts[n]), fix(g), fix(mom[n]), fix(var[n]))
            delta_w[n], new_m[n], new_v[n] = [view(o[:r].reshape(view(wts[n]).shape)) for o in outs]

    parts_l = [grads0[n] for n in late]
    sums_l = chip_sums(_exchange("swap0_late", "run", parts_l + halves(parts_l), (n_l,), _plan_swap(n_l)))
    sems_l, token_l, late_flight = _exchange("scatter0_late_start", "start", sums_l + thirds(sums_l), (3 * n_l,),
                                             _plan_scatter(n_l))
    full.update(zip(early, _exchange("share_early", "run", [out[n] for n in early], (L * n_e,), _plan_share(n_e, L))))
    update(early)
    done = _exchange("scatter0_late_wait", "wait", late_flight, (3 * n_l,), _plan_scatter(n_l), sems=sems_l,
                     after=delta_w[early[-1]])
    out.update(zip(late, finish(done, 0, [out[n] for n in late])))
    full.update(zip(late, _exchange("share_late", "run", [out[n] for n in late], (L * n_l,), _plan_share(n_l, L))))
    update(late)
    full["conv_w"] = full["conv_w"][:, :CONV_K]
    loss = lax.psum(loss_part[0, 0], ("x", "y", "c"))
    return (loss, grad_x, *[full[n] for n in ALL_WEIGHTS], *[delta_w[n] for n in ALL_WEIGHTS],
            *[new_m[n] for n in ALL_WEIGHTS], *[new_v[n] for n in ALL_WEIGHTS])
```

```python
import jax
import jax.numpy as jnp
from jax import lax
from jax.experimental import pallas as pl
from jax.experimental.pallas import tpu as pltpu

F32 = jnp.float32
MM_DTYPE = jnp.bfloat16
WIRE_DTYPE = jnp.bfloat16
EPS = 1e-6
N_HEADS = 8
QK_NOPE = 64
QK_ROPE = 32
QK_DIM = QK_NOPE + QK_ROPE
V_DIM = 64
HEAD_PAD = 128
HEADS_PER_STEP = 4
CHUNK_SHIFT = 6
LOG2E = 1.4426950408889634
CONV_K = 31
HALO = 32
ROPE_THETA = 10000.0
N_CHIPS = 4
ADAM_LR, ADAM_B1, ADAM_B2, ADAM_EPS, ADAM_WD, ADAM_STEP = 0.001, 0.9, 0.999, 1e-08, 0.01, 10
VMEM_LIMIT_BYTES = 56 * 2 ** 20
WGRAD_ROWS = 2048
MESH = pl.DeviceIdType.MESH
ANY = pl.BlockSpec(memory_space=pl.ANY)

_DIMS = {
    "nn": (((1,), (0,)), ((), ())),
    "nt": (((1,), (1,)), ((), ())),
    "tn": (((0,), (0,)), ((), ())),
}


def _params(n_axes):
    return pltpu.CompilerParams(dimension_semantics=("arbitrary",) * n_axes, vmem_limit_bytes=VMEM_LIMIT_BYTES)


def _tile(n, cap):
    for step in (16, 8):
        for t in range(cap - cap % step, 0, -step):
            if n % t == 0:
                return t
    return n


def _mm(name, a, b, dims, grid, a_spec, b_spec, o_spec, out_shape, nk, acc_shape, into=None):
    n_axes = len(grid)

    def body(*refs):
        if into is not None:
            refs = refs[1:]
        a_ref, b_ref, o_ref = refs[:3]
        part = lax.dot_general(a_ref[...].astype(MM_DTYPE), b_ref[...].astype(MM_DTYPE), _DIMS[dims],
                               preferred_element_type=F32)
        if nk == 1:
            o_ref[...] = part.astype(o_ref.dtype)
        else:
            acc_ref = refs[3]
            k = pl.program_id(n_axes - 1)

            @pl.when(k == 0)
            def _():
                acc_ref[...] = part

            @pl.when(k > 0)
            def _():
                acc_ref[...] += part

            @pl.when(k == nk - 1)
            def _():
                o_ref[...] = acc_ref[...].astype(o_ref.dtype)

    in_specs = [a_spec, b_spec]
    args = [a, b]
    aliases = {}
    if into is not None:
        in_specs = [ANY] + in_specs
        args = [into] + args
        aliases = {0: 0}
    return pl.pallas_call(
        body, name=name, grid=grid, in_specs=in_specs, out_specs=o_spec, out_shape=out_shape,
        scratch_shapes=[] if nk == 1 else [pltpu.VMEM(acc_shape, F32)],
        input_output_aliases=aliases, compiler_params=_params(n_axes))(*args)


def _proj(name, a, w, l, dims, out_dtype=F32):
    s, k = a.shape
    g = w.shape[1]
    n = w.shape[3] if dims == "nn" else w.shape[2]
    tm = _tile(s, 512)
    return _mm(name, a, w, dims, (g, s // tm),
               pl.BlockSpec((tm, k), lambda j, i: (i, 0)),
               pl.BlockSpec((None, None) + w.shape[2:], lambda j, i: (l, j, 0, 0)),
               pl.BlockSpec((None, tm, n), lambda j, i: (j, i, 0)),
               jax.ShapeDtypeStruct((g, s, n), out_dtype), 1, None)


def _contract(name, a, w, l, dims, out_dtype=F32):
    g, s, kb = a.shape
    n = w.shape[3] if dims == "nn" else w.shape[2]
    tm = _tile(s, 512)
    return _mm(name, a, w, dims, (s // tm, g),
               pl.BlockSpec((None, tm, kb), lambda i, j: (j, i, 0)),
               pl.BlockSpec((None, None) + w.shape[2:], lambda i, j: (l, j, 0, 0)),
               pl.BlockSpec((tm, n), lambda i, j: (i, 0)),
               jax.ShapeDtypeStruct((s, n), out_dtype), g, (tm, n))


def _wgrad(name, a, b, out_dtype, layers=None, l=None, into=None):
    g = a.shape[0] if a.ndim == 3 else b.shape[0]
    s, k = a.shape[-2:]
    n = b.shape[-1]
    tm = _tile(s, WGRAD_ROWS)

    def spec(x, w):
        if x.ndim == 3:
            return pl.BlockSpec((None, tm, w), lambda j, i: (j, i, 0))
        return pl.BlockSpec((tm, w), lambda j, i: (i, 0))

    if layers is None:
        o_spec = pl.BlockSpec((None, k, n), lambda j, i: (j, 0, 0))
        out_shape = jax.ShapeDtypeStruct((g, k, n), out_dtype)
    else:
        o_spec = pl.BlockSpec((None, None, k, n), lambda j, i: (l, j, 0, 0))
        out_shape = jax.ShapeDtypeStruct((layers, g, k, n), out_dtype)
    return _mm(name, a, b, "tn", (g, s // tm), spec(a, k), spec(b, n), o_spec, out_shape, s // tm, (k, n), into=into)


def _ffn_up_fused(n, wg, wu, l):
    s, k = n.shape
    g, n4 = wg.shape[1], wg.shape[3]
    tm = _tile(s, 512)

    def body(n_ref, wg_ref, wu_ref, a_ref, b_ref, h_ref):
        nv = n_ref[...]
        a = jnp.dot(nv, wg_ref[...], preferred_element_type=F32)
        b = jnp.dot(nv, wu_ref[...], preferred_element_type=F32)
        a_ref[...] = a.astype(a_ref.dtype)
        b_ref[...] = b.astype(b_ref.dtype)
        h_ref[...] = (a * _sigmoid(a) * b).astype(h_ref.dtype)

    w_spec = pl.BlockSpec((None, None, k, n4), lambda j, i: (l, j, 0, 0))
    o_spec = pl.BlockSpec((None, tm, n4), lambda j, i: (j, i, 0))
    return pl.pallas_call(
        body, name="ffn_up_fused", grid=(g, s // tm),
        in_specs=[pl.BlockSpec((tm, k), lambda j, i: (i, 0)), w_spec, w_spec], out_specs=[o_spec] * 3,
        out_shape=[jax.ShapeDtypeStruct((g, s, n4), MM_DTYPE)] * 3, compiler_params=_params(2))(n, wg, wu)


def _ffn_down_t_fused(do, wd, a, b, l):
    s, k = do.shape
    g, n4 = wd.shape[1], wd.shape[2]
    tm = _tile(s, 512)

    def body(do_ref, wd_ref, a_ref, b_ref, da_ref, db_ref):
        dh = lax.dot_general(do_ref[...], wd_ref[...], _DIMS["nt"], preferred_element_type=F32)
        av = a_ref[...].astype(F32)
        sg = _sigmoid(av)
        da_ref[...] = (dh * b_ref[...].astype(F32) * (sg * (1.0 + av * (1.0 - sg)))).astype(da_ref.dtype)
        db_ref[...] = (dh * (av * sg)).astype(db_ref.dtype)

    t_spec = pl.BlockSpec((None, tm, n4), lambda j, i: (j, i, 0))
    return pl.pallas_call(
        body, name="ffn_down_t_fused", grid=(g, s // tm),
        in_specs=[pl.BlockSpec((tm, k), lambda j, i: (i, 0)),
                  pl.BlockSpec((None, None, n4, k), lambda j, i: (l, j, 0, 0)), t_spec, t_spec],
        out_specs=[t_spec] * 2, out_shape=[jax.ShapeDtypeStruct((g, s, n4), MM_DTYPE)] * 2,
        compiler_params=_params(2))(do, wd, a, b)


def _blocked_sum(name, acts, weights, l, dims):
    g, s, kb = acts[0].shape
    n = weights[0].shape[3] if dims == "nn" else weights[0].shape[2]
    tm = _tile(s, 512)
    n_pairs = len(acts)

    def body(*refs):
        o_ref = refs[2 * n_pairs]
        acc = None
        for p in range(n_pairs):
            for j in range(g):
                t = lax.dot_general(refs[p][j], refs[n_pairs + p][j], _DIMS[dims], preferred_element_type=F32)
                acc = t if acc is None else acc + t
        o_ref[...] = acc

    return pl.pallas_call(
        body, name=name, grid=(s // tm,),
        in_specs=[pl.BlockSpec((g, tm, kb), lambda i: (0, i, 0))] * n_pairs
        + [pl.BlockSpec((None,) + w.shape[1:], lambda i: (l, 0, 0, 0)) for w in weights],
        out_specs=pl.BlockSpec((tm, n), lambda i: (i, 0)), out_shape=jax.ShapeDtypeStruct((s, n), F32),
        compiler_params=_params(1))(*acts, *weights)


def _wgrad_pair(name, nt, da, db, layers, l, into):
    k, s = nt.shape
    g, _, n4 = da.shape
    tm = _tile(s, WGRAD_ROWS)
    nk = s // tm

    def body(*refs):
        if into is not None:
            refs = refs[2:]
        nt_ref, da_ref, db_ref, og_ref, ou_ref, accg, accu = refs
        i = pl.program_id(1)
        ntv = nt_ref[...]
        pg = jnp.dot(ntv, da_ref[...], preferred_element_type=F32)
        pu = jnp.dot(ntv, db_ref[...], preferred_element_type=F32)

        @pl.when(i == 0)
        def _():
            accg[...] = pg
            accu[...] = pu

        @pl.when(i > 0)
        def _():
            accg[...] += pg
            accu[...] += pu

        @pl.when(i == nk - 1)
        def _():
            og_ref[...] = accg[...].astype(og_ref.dtype)
            ou_ref[...] = accu[...].astype(ou_ref.dtype)

    t_spec = pl.BlockSpec((None, tm, n4), lambda j, i: (j, i, 0))
    o_spec = pl.BlockSpec((None, None, k, n4), lambda j, i: (l, j, 0, 0))
    in_specs = [pl.BlockSpec((k, tm), lambda j, i: (0, i)), t_spec, t_spec]
    args = [nt, da, db]
    aliases = {}
    if into is not None:
        in_specs = [ANY, ANY] + in_specs
        args = list(into) + args
        aliases = {0: 0, 1: 1}
    return pl.pallas_call(
        body, name=name, grid=(g, nk), in_specs=in_specs, out_specs=[o_spec] * 2,
        out_shape=[jax.ShapeDtypeStruct((layers, g, k, n4), WIRE_DTYPE)] * 2,
        scratch_shapes=[pltpu.VMEM((k, n4), F32)] * 2, input_output_aliases=aliases,
        compiler_params=_params(2))(*args)


def _rows(name, fn, s, tm, ins, outs, scratch=()):
    n = s // tm

    def spec(shape, kind):
        nd = len(shape)
        if kind == "r":
            return pl.BlockSpec(shape[:-2] + (tm, shape[-1]), lambda i: (0,) * (nd - 2) + (i, 0))
        if kind in ("f", "a"):
            return pl.BlockSpec(shape, lambda i: (0,) * nd)
        if kind == "t":
            return pl.BlockSpec((shape[0], tm), lambda i: (0, i))
        if kind == "b":
            return pl.BlockSpec((None,) + shape[1:], lambda i: (i, 0, 0))
        if kind == "p":
            return pl.BlockSpec((HALO, shape[-1]), lambda i: (jnp.maximum(i * (tm // HALO) - 1, 0), 0))
        if kind == "n":
            return pl.BlockSpec((HALO, shape[-1]), lambda i: (jnp.minimum((i + 1) * (tm // HALO), s // HALO - 1), 0))
        j = kind[1]
        return pl.BlockSpec((tm, HEAD_PAD), lambda i: (i, j))

    def body(*refs):
        fn(pl.program_id(0), n, *refs)

    return pl.pallas_call(
        body, name=name, grid=(n,),
        in_specs=[spec(a.shape, kind) for a, kind in ins],
        out_specs=[spec(shape, kind) for shape, _, kind in outs],
        out_shape=[jax.ShapeDtypeStruct(shape, dtype) for shape, dtype, _ in outs],
        scratch_shapes=list(scratch), compiler_params=_params(1))(*[a for a, _ in ins])


def _acc(ref, i, val):
    @pl.when(i == 0)
    def _():
        ref[...] = val

    @pl.when(i > 0)
    def _():
        ref[...] += val


def _sum0(x):
    return jnp.sum(x, axis=0, keepdims=True)


def _rstd(x, n):
    return lax.rsqrt(jnp.sum(x * x, axis=-1, keepdims=True) * (1.0 / n) + EPS)


def _rms_bwd(x, g, dy, n):
    r = _rstd(x, n)
    xh = x * r
    dyg = dy * g
    dx = r * (dyg - xh * (jnp.sum(dyg * xh, axis=-1, keepdims=True) * (1.0 / n)))
    return dx, _sum0(dy * xh)


def _sigmoid(x):
    return 1.0 / (1.0 + jnp.exp(-x))


def _norm_fwd(x, g):
    s, d = x.shape

    def fn(i, n, x_ref, g_ref, o_ref, ot_ref):
        xv = x_ref[...]
        y = xv * _rstd(xv, d) * g_ref[...]
        o_ref[...] = y.astype(o_ref.dtype)
        ot_ref[...] = y.T.astype(ot_ref.dtype)

    return _rows("norm_fwd", fn, s, _tile(s, 256), [(x, "r"), (g, "f")],
                 [((s, d), MM_DTYPE, "r"), ((d, s), MM_DTYPE, "t")])


def _resid_norm(x, adds, scale, g, norm_dtype, transposed=False):
    s, d = x.shape
    k = len(adds)

    def fn(i, n, *refs):
        x_ref, add_refs, g_ref, xo_ref, no_ref = refs[0], refs[1:1 + k], refs[1 + k], refs[2 + k], refs[3 + k]
        tot = add_refs[0][...]
        for r in add_refs[1:]:
            tot = tot + r[...]
        xv = x_ref[...] + scale * tot
        xo_ref[...] = xv
        y = xv * _rstd(xv, d) * g_ref[...]
        no_ref[...] = y.astype(no_ref.dtype)
        if transposed:
            refs[4 + k][...] = y.T.astype(refs[4 + k].dtype)

    return _rows("resid_norm", fn, s, _tile(s, 256), [(x, "r")] + [(a, "r") for a in adds] + [(g, "f")],
                 [((s, d), F32, "r"), ((s, d), norm_dtype, "r")] + ([((d, s), MM_DTYPE, "t")] if transposed else []))


def _norm_bwd(x, g, dn_parts, dres, out_scale):
    s, d = x.shape
    k = len(dn_parts)
    has_res = dres is not None

    def fn(i, n, *refs):
        x_ref, g_ref = refs[0], refs[1]
        dn_refs = refs[2:2 + k]
        pos = 2 + k
        dn = dn_refs[0][...]
        for r in dn_refs[1:]:
            dn = dn + r[...]
        dx, dg = _rms_bwd(x_ref[...], g_ref[...], dn, d)
        if has_res:
            dx = dx + refs[pos][...]
            pos += 1
        dx_ref, dxs_ref, dg_ref = refs[pos:pos + 3]
        dx_ref[...] = dx
        dxs_ref[...] = (out_scale * dx).astype(dxs_ref.dtype)
        _acc(dg_ref, i, dg)

    ins = [(x, "r"), (g, "f")] + [(p, "r") for p in dn_parts] + ([(dres, "r")] if has_res else [])
    return _rows("norm_bwd", fn, s, _tile(s, 256), ins,
                 [((s, d), F32, "r"), ((s, d), MM_DTYPE, "r"), ((1, d), F32, "a")])


def _final_fwd(x, o, g):
    return _resid_norm(x, [o], 0.5, g, F32)


def _loss(y, t):
    s, d = y.shape

    def fn(i, n, y_ref, t_ref, dy_ref, l_ref):
        e = y_ref[...] - t_ref[...]
        dy_ref[...] = e * (1.0 / d)
        _acc(l_ref, i, (0.5 / d) * jnp.sum(jnp.sum(e * e, axis=-1, keepdims=True), axis=0, keepdims=True))

    return _rows("loss", fn, s, _tile(s, 256), [(y, "r"), (t, "r")], [((s, d), F32, "r"), ((1, 1), F32, "a")])


def _mla_pre(p, gql, gkvl, cw):
    s = p.shape[0]
    ql, kvl = gql.shape[-1], gkvl.shape[-1]
    o_q, o_kv = 2 * cw, 2 * cw + ql

    def fn(i, n, p_ref, gq_ref, gkv_ref, cq_ref, ckv_ref, ckvt_ref):
        cq = p_ref[:, o_q:o_q + ql]
        cq_ref[...] = (cq * _rstd(cq, ql) * gq_ref[...]).astype(cq_ref.dtype)
        ckv = p_ref[:, o_kv:o_kv + kvl]
        y = ckv * _rstd(ckv, kvl) * gkv_ref[...]
        ckv_ref[...] = y.astype(ckv_ref.dtype)
        ckvt_ref[...] = y.T.astype(ckvt_ref.dtype)

    return _rows("mla_pre", fn, s, _tile(s, 256), [(p, "r"), (gql, "f"), (gkvl, "f")],
                 [((s, ql), MM_DTYPE, "r"), ((s, kvl), MM_DTYPE, "r"), ((kvl, s), MM_DTYPE, "t")])


def _rope(x, cosf, s1, s2):
    return x * cosf + pltpu.roll(x, HEAD_PAD - 16, 1) * s1 + pltpu.roll(x, 16, 1) * s2


def _rope_bwd(d, cosf, s1, s2):
    return d * cosf + pltpu.roll(d * s1, 16, 1) + pltpu.roll(d * s2, HEAD_PAD - 16, 1)


def _qk_post(q_pre, k_pre, p, pe_block, tabs, gq, gk):
    s, hw = q_pre.shape
    cosf, s1, s2 = tabs

    tm = _tile(s, 256)

    def fn(i, n, q_ref, k_ref, pe_ref, c_ref, s1_ref, s2_ref, gq_ref, gk_ref, qo_ref, ko_ref, kt_ref):
        c, a1, a2, pe = c_ref[...], s1_ref[...], s2_ref[...], pe_ref[...]
        for hh in range(N_HEADS):
            cols = slice(hh * HEAD_PAD, (hh + 1) * HEAD_PAD)
            qv = q_ref[:, cols]
            qo_ref[:, cols] = _rope(qv * _rstd(qv, QK_DIM) * gq_ref[...], c, a1, a2).astype(qo_ref.dtype)
            kv = k_ref[:, cols] + pe
            kout = _rope(kv * _rstd(kv, QK_DIM) * gk_ref[...], c, a1, a2)
            ko_ref[:, cols] = kout.astype(ko_ref.dtype)
            kt_ref[cols, :] = kout.T.astype(kt_ref.dtype)

    return _rows("qk_post", fn, s, tm,
                 [(q_pre, "r"), (k_pre, "r"), (p, ("c", pe_block)), (cosf, "r"), (s1, "r"), (s2, "r"), (gq, "f"), (gk, "f")],
                 [((s, hw), MM_DTYPE, "r"), ((s, hw), MM_DTYPE, "r"), ((s // tm, hw, tm), MM_DTYPE, "b")])


def _qk_post_bwd(q_pre, k_pre, p, pe_block, tabs, gq, gk, dq, dk):
    s, hw = q_pre.shape
    cosf, s1, s2 = tabs

    def fn(i, n, q_ref, k_ref, pe_ref, c_ref, s1_ref, s2_ref, gq_ref, gk_ref, dq_ref, dk_ref,
           dqo_ref, dko_ref, dpe_ref, dgq_ref, dgk_ref):
        c, a1, a2, pe = c_ref[...], s1_ref[...], s2_ref[...], pe_ref[...]
        lane = lax.broadcasted_iota(jnp.int32, pe.shape, 1)
        is_pe = (lane >= QK_NOPE) & (lane < QK_DIM)
        dpe = jnp.zeros(pe.shape, F32)
        dgq = jnp.zeros((1, HEAD_PAD), F32)
        dgk = jnp.zeros((1, HEAD_PAD), F32)
        for hh in range(N_HEADS):
            cols = slice(hh * HEAD_PAD, (hh + 1) * HEAD_PAD)
            dxq, g1 = _rms_bwd(q_ref[:, cols], gq_ref[...], _rope_bwd(dq_ref[cols, :].T, c, a1, a2), QK_DIM)
            dqo_ref[:, cols] = dxq.astype(dqo_ref.dtype)
            dxk, g2 = _rms_bwd(k_ref[:, cols] + pe, gk_ref[...], _rope_bwd(dk_ref[:, cols], c, a1, a2), QK_DIM)
            dko_ref[:, cols] = dxk.astype(dko_ref.dtype)
            dpe = dpe + dxk
            dgq = dgq + g1
            dgk = dgk + g2
        dpe_ref[...] = jnp.where(is_pe, dpe, 0.0)
        _acc(dgq_ref, i, dgq)
        _acc(dgk_ref, i, dgk)

    return _rows("qk_post_bwd", fn, s, _tile(s, 256),
                 [(q_pre, "r"), (k_pre, "r"), (p, ("c", pe_block)), (cosf, "r"), (s1, "r"), (s2, "r"), (gq, "f"), (gk, "f"),
                  (dq, "b"), (dk, "r")],
                 [((s, hw), MM_DTYPE, "r"), ((s, hw), MM_DTYPE, "r"), ((s, HEAD_PAD), F32, "r"),
                  ((1, HEAD_PAD), F32, "a"), ((1, HEAD_PAD), F32, "a")])


def _chunk_mask(tq, key_rows):
    r = jnp.right_shift(lax.broadcasted_iota(jnp.int32, (tq, tq), 0), CHUNK_SHIFT)
    c = jnp.right_shift(lax.broadcasted_iota(jnp.int32, (tq, tq), 1), CHUNK_SHIFT)
    return (r <= c) if key_rows else (c <= r)


def _attn_specs(s, tq):
    w = HEADS_PER_STEP * HEAD_PAD
    tile = pl.BlockSpec((tq, w), lambda hb, i: (i, hb))
    whole = pl.BlockSpec((s, w), lambda hb, i: (0, hb))
    rows = pl.BlockSpec((None, None, HEADS_PER_STEP, tq), lambda hb, i: (hb, i, 0, 0))
    blocks = pl.BlockSpec((s // tq, w, tq), lambda hb, i: (0, hb, 0))
    return tile, whole, rows, blocks


def _head(hh):
    return slice(hh * HEAD_PAD, (hh + 1) * HEAD_PAD)


def _attn_fwd(q, k, vt):
    s, hw = q.shape
    tq = _tile(s, 256)
    nq = s // tq
    groups = N_HEADS // HEADS_PER_STEP
    scale = QK_DIM ** -0.5
    tile, whole, rows, blocks = _attn_specs(s, tq)

    def body(q_ref, k_ref, vt_ref, o_ref, lse_ref):
        qi = pl.program_id(1)
        qv = [q_ref[:, _head(hh)] for hh in range(HEADS_PER_STEP)]

        def scores(kb):
            off = pl.multiple_of(kb * tq, tq)
            return tuple(lax.dot_general(k_ref[pl.ds(off, tq), _head(hh)], qv[hh], _DIMS["nt"],
                                         preferred_element_type=F32) for hh in range(HEADS_PER_STEP))

        def update(kb, stats, raw, diag):
            heads = range(HEADS_PER_STEP)
            mn, al, pr, ln = [], [], [], []
            for hh in heads:
                m, l, _ = stats[hh]
                sc = raw[hh] * (scale * LOG2E)
                if diag:
                    sc = jnp.where(_chunk_mask(tq, True), sc, -1e30)
                mn.append(jnp.maximum(m, jnp.max(sc, axis=0, keepdims=True)))
                al.append(jnp.exp2(m - mn[hh]))
                p = jnp.exp2(sc - mn[hh])
                ln.append(al[hh] * l + jnp.sum(p, axis=0, keepdims=True))
                pr.append(p.astype(MM_DTYPE))
            return tuple((mn[hh], ln[hh], al[hh] * stats[hh][2]
                          + jnp.dot(vt_ref[kb, _head(hh), :], pr[hh], preferred_element_type=F32)) for hh in heads)

        def step(kb, carry):
            stats, raw = carry
            nxt = scores(kb + 1)
            return update(kb, stats, raw, False), nxt

        init = tuple((jnp.full((1, tq), -1e30, F32), jnp.zeros((1, tq), F32), jnp.zeros((HEAD_PAD, tq), F32))
                     for _ in range(HEADS_PER_STEP))
        stats, raw = lax.fori_loop(0, qi, step, (init, scores(0)))
        carry = update(qi, stats, raw, True)
        for hh in range(HEADS_PER_STEP):
            m, l, acc = carry[hh]
            o_ref[:, _head(hh)] = (acc / l).T
            lse_ref[hh:hh + 1, :] = m + jnp.log(l) * LOG2E

    return pl.pallas_call(
        body, name="attn_fwd", grid=(groups, nq),
        in_specs=[tile, whole, blocks], out_specs=[tile, rows],
        out_shape=[jax.ShapeDtypeStruct((s, hw), F32), jax.ShapeDtypeStruct((groups, nq, HEADS_PER_STEP, tq), F32)],
        compiler_params=_params(2))(q, k, vt)


def _attn_delta(o, do):
    s, hw = o.shape
    tq = _tile(s, 256)
    groups = N_HEADS // HEADS_PER_STEP
    tile, _, rows, _ = _attn_specs(s, tq)

    def body(o_ref, do_ref, d_ref):
        for hh in range(HEADS_PER_STEP):
            prod = o_ref[:, _head(hh)] * do_ref[:, _head(hh)]
            d_ref[hh:hh + 1, :] = jnp.sum(prod.T, axis=0, keepdims=True)

    return pl.pallas_call(
        body, name="attn_delta", grid=(groups, s // tq), in_specs=[tile, tile], out_specs=rows,
        out_shape=jax.ShapeDtypeStruct((groups, s // tq, HEADS_PER_STEP, tq), F32),
        compiler_params=_params(2))(o, do)


def _attn_bwd(q, k, kt, v, do, lse_row, delta_row):
    s, hw = q.shape
    tq = _tile(s, 256)
    nq = s // tq
    groups = N_HEADS // HEADS_PER_STEP
    w = HEADS_PER_STEP * HEAD_PAD
    scale = QK_DIM ** -0.5
    tile, whole, _, blocks = _attn_specs(s, tq)
    rowvec = pl.BlockSpec((None, nq, HEADS_PER_STEP, tq), lambda hb, i: (hb, 0, 0, 0))
    kt_tile = pl.BlockSpec((None, w, tq), lambda hb, i: (i, hb, 0))

    def body(q_ref, k_ref, kt_ref, v_ref, do_ref, lse_ref, delta_ref, dqt_ref, dk_ref, dv_ref):
        ki = pl.program_id(1)
        kv = [k_ref[:, _head(hh)] for hh in range(HEADS_PER_STEP)]
        vv = [v_ref[:, _head(hh)] for hh in range(HEADS_PER_STEP)]
        ktv = [kt_ref[_head(hh), :] for hh in range(HEADS_PER_STEP)]

        @pl.when(ki == 0)
        def _():
            dqt_ref[...] = jnp.zeros(dqt_ref.shape, F32)

        def block(qb, carry, diag):
            off = pl.multiple_of(qb * tq, tq)
            heads = range(HEADS_PER_STEP)
            qs = [q_ref[pl.ds(off, tq), _head(hh)] for hh in heads]
            dos = [do_ref[pl.ds(off, tq), _head(hh)].astype(MM_DTYPE) for hh in heads]
            raw = [lax.dot_general(kv[hh], qs[hh], _DIMS["nt"], preferred_element_type=F32) for hh in heads]
            dpr = [lax.dot_general(vv[hh], dos[hh], _DIMS["nt"], preferred_element_type=F32) for hh in heads]
            pr, ds = [], []
            for hh in heads:
                sc = raw[hh] * (scale * LOG2E)
                if diag:
                    sc = jnp.where(_chunk_mask(tq, True), sc, -1e30)
                p = jnp.exp2(sc - lse_ref[qb, hh:hh + 1, :])
                pr.append(p.astype(MM_DTYPE))
                ds.append((p * (dpr[hh] - delta_ref[qb, hh:hh + 1, :]) * scale).astype(MM_DTYPE))
            out = []
            for hh in heads:
                dk, dvv = carry[hh]
                dvv = dvv + jnp.dot(pr[hh], dos[hh], preferred_element_type=F32)
                dk = dk + jnp.dot(ds[hh], qs[hh], preferred_element_type=F32)
                dqt_ref[qb, _head(hh), :] += jnp.dot(ktv[hh], ds[hh], preferred_element_type=F32)
                out.append((dk, dvv))
            return tuple(out)

        init = tuple((jnp.zeros((tq, HEAD_PAD), F32), jnp.zeros((tq, HEAD_PAD), F32)) for _ in range(HEADS_PER_STEP))
        carry = lax.fori_loop(ki + 1, nq, lambda qb, c: block(qb, c, False), block(ki, init, True))
        for hh in range(HEADS_PER_STEP):
            dk_ref[:, _head(hh)] = carry[hh][0]
            dv_ref[:, _head(hh)] = carry[hh][1].astype(dv_ref.dtype)

    return pl.pallas_call(
        body, name="attn_bwd", grid=(groups, nq),
        in_specs=[whole, tile, kt_tile, tile, whole, rowvec, rowvec], out_specs=[blocks, tile, tile],
        out_shape=[jax.ShapeDtypeStruct((nq, hw, tq), F32), jax.ShapeDtypeStruct((s, hw), F32),
                   jax.ShapeDtypeStruct((s, hw), MM_DTYPE)],
        compiler_params=_params(2))(q, k, kt, v, do, lse_row, delta_row)


def _glu_with_halo(i, p_ref, ph_ref, ubuf, cw, tm):
    a, g = p_ref[:, 0:cw], p_ref[:, cw:2 * cw]
    ubuf[HALO:HALO + tm, :] = a * _sigmoid(g)
    ah, gh = ph_ref[:, 0:cw], ph_ref[:, cw:2 * cw]
    ubuf[0:HALO, :] = jnp.where(i > 0, ah * _sigmoid(gh), 0.0)
    return a, g


def _conv_taps(ubuf, w_ref, tm):
    base = HALO - (CONV_K - 1)
    y = w_ref[0:1, :] * ubuf[base:base + tm, :]
    for k in range(1, CONV_K):
        y = y + w_ref[k:k + 1, :] * ubuf[base + k:base + k + tm, :]
    return y


def _layer_norm_stats(y, cw):
    mu = jnp.sum(y, axis=-1, keepdims=True) * (1.0 / cw)
    yc = y - mu
    r = lax.rsqrt(jnp.sum(yc * yc, axis=-1, keepdims=True) * (1.0 / cw) + EPS)
    return yc * r, r


def _conv_fwd(p, w, cb, lg, lb):
    s = p.shape[0]
    cw = cb.shape[-1]
    tm = _tile(s, 256)

    def fn(i, n, p_ref, ph_ref, w_ref, cb_ref, lg_ref, lb_ref, o_ref, ubuf):
        _glu_with_halo(i, p_ref, ph_ref, ubuf, cw, tm)
        yh, _ = _layer_norm_stats(_conv_taps(ubuf, w_ref, tm) + cb_ref[...], cw)
        z = yh * lg_ref[...] + lb_ref[...]
        o_ref[...] = (z * _sigmoid(z)).astype(o_ref.dtype)

    return _rows("conv_fwd", fn, s, tm, [(p, "r"), (p, "p"), (w, "f"), (cb, "f"), (lg, "f"), (lb, "f")],
                 [((s, cw), MM_DTYPE, "r")], scratch=[pltpu.VMEM((tm + HALO, cw), F32)])[0]


def _conv_bwd_ln(p, w, cb, lg, lb, dc):
    s = p.shape[0]
    cw = cb.shape[-1]
    tm = _tile(s, 256)

    def fn(i, n, p_ref, ph_ref, w_ref, cb_ref, lg_ref, lb_ref, dc_ref, dy_ref, dlg_ref, dlb_ref, dcb_ref, ubuf):
        _glu_with_halo(i, p_ref, ph_ref, ubuf, cw, tm)
        yh, r = _layer_norm_stats(_conv_taps(ubuf, w_ref, tm) + cb_ref[...], cw)
        z = yh * lg_ref[...] + lb_ref[...]
        sg = _sigmoid(z)
        dz = dc_ref[...] * (sg * (1.0 + z * (1.0 - sg)))
        dyh = dz * lg_ref[...]
        m1 = jnp.sum(dyh, axis=-1, keepdims=True) * (1.0 / cw)
        m2 = jnp.sum(dyh * yh, axis=-1, keepdims=True) * (1.0 / cw)
        dy = r * (dyh - m1 - yh * m2)
        dy_ref[...] = dy
        _acc(dlg_ref, i, _sum0(dz * yh))
        _acc(dlb_ref, i, _sum0(dz))
        _acc(dcb_ref, i, _sum0(dy))

    return _rows("conv_bwd_ln", fn, s, tm,
                 [(p, "r"), (p, "p"), (w, "f"), (cb, "f"), (lg, "f"), (lb, "f"), (dc, "r")],
                 [((s, cw), F32, "r"), ((1, cw), F32, "a"), ((1, cw), F32, "a"), ((1, cw), F32, "a")],
                 scratch=[pltpu.VMEM((tm + HALO, cw), F32)])


def _conv_bwd_taps(p, w, dy):
    s = p.shape[0]
    cw = w.shape[-1]
    tm = _tile(s, 256)

    def fn(i, n, p_ref, ph_ref, w_ref, dy_ref, dyn_ref, dp_ref, dw_ref, ubuf, dybuf):
        a, g = _glu_with_halo(i, p_ref, ph_ref, ubuf, cw, tm)
        dyv = dy_ref[...]
        dybuf[0:tm, :] = dyv
        dybuf[tm:tm + HALO, :] = jnp.where(i < n - 1, dyn_ref[...], 0.0)
        base = HALO - (CONV_K - 1)
        du = jnp.zeros((tm, cw), F32)

        @pl.when(i == 0)
        def _():
            dw_ref[...] = jnp.zeros(dw_ref.shape, F32)

        for k in range(CONV_K):
            sh = CONV_K - 1 - k
            du = du + w_ref[k:k + 1, :] * dybuf[sh:sh + tm, :]
            dw_ref[k:k + 1, :] += _sum0(dyv * ubuf[base + k:base + k + tm, :])
        sg = _sigmoid(g)
        dp_ref[:, 0:cw] = (du * sg).astype(dp_ref.dtype)
        dp_ref[:, cw:2 * cw] = (du * a * sg * (1.0 - sg)).astype(dp_ref.dtype)

    return _rows("conv_bwd_taps", fn, s, tm, [(p, "r"), (p, "p"), (w, "f"), (dy, "r"), (dy, "n")],
                 [((s, 2 * cw), MM_DTYPE, "r"), ((CONV_K + 1, cw), F32, "a")],
                 scratch=[pltpu.VMEM((tm + HALO, cw), F32), pltpu.VMEM((tm + HALO, cw), F32)])


def _dp_assemble(p, gql, gkvl, dcq, dckv_parts, dp_conv, dpe, cw):
    s, width = p.shape
    ql, kvl = gql.shape[-1], gkvl.shape[-1]
    o_q, o_kv, o_pe = 2 * cw, 2 * cw + ql, 2 * cw + ql + kvl

    def fn(i, n, p_ref, gq_ref, gkv_ref, dcq_ref, dk1_ref, dk2_ref, dconv_ref, dpe_ref, dp_ref, dgq_ref, dgkv_ref):
        dp_ref[:, 0:o_q] = dconv_ref[...]
        dx, dg = _rms_bwd(p_ref[:, o_q:o_kv], gq_ref[...], dcq_ref[...], ql)
        dp_ref[:, o_q:o_kv] = dx.astype(dp_ref.dtype)
        _acc(dgq_ref, i, dg)
        dx, dg = _rms_bwd(p_ref[:, o_kv:o_pe], gkv_ref[...], dk1_ref[...] + dk2_ref[...], kvl)
        dp_ref[:, o_kv:o_pe] = dx.astype(dp_ref.dtype)
        _acc(dgkv_ref, i, dg)
        dp_ref[:, o_pe:width] = dpe_ref[...].astype(dp_ref.dtype)

    return _rows("dp_assemble", fn, s, _tile(s, 256),
                 [(p, "r"), (gql, "f"), (gkvl, "f"), (dcq, "r"), (dckv_parts[0], "r"), (dckv_parts[1], "r"),
                  (dp_conv, "r"), (dpe, "r")],
                 [((s, width), MM_DTYPE, "r"), ((1, ql), F32, "a"), ((1, kvl), F32, "a")])


def _add_leading(name, x, out_dtype):
    n, r, c = x.shape
    tm = _tile(r, 256)

    def fn(i, nt, x_ref, o_ref):
        tot = x_ref[0].astype(F32)
        for k in range(1, n):
            tot = tot + x_ref[k].astype(F32)
        o_ref[...] = tot.astype(o_ref.dtype)

    return _rows(name, fn, r, tm, [(x, "r")], [((r, c), out_dtype, "r")])[0]


def _prefetch_call(name, fn, idx, grid, in_specs, out_spec, out_shape, args, aliases=None):
    return pl.pallas_call(
        fn, name=name, out_shape=out_shape,
        grid_spec=pltpu.PrefetchScalarGridSpec(num_scalar_prefetch=1, grid=grid, in_specs=in_specs, out_specs=out_spec),
        input_output_aliases=aliases or {}, compiler_params=_params(len(grid)))(idx, *args)


def _adamw(w, g, m, v):
    r, c = w.shape
    tm = _tile(r, 256)

    def fn(i, n, w_ref, g_ref, m_ref, v_ref, d_ref, mo_ref, vo_ref):
        gv = g_ref[...]
        mn = ADAM_B1 * m_ref[...] + (1.0 - ADAM_B1) * gv
        vn = ADAM_B2 * v_ref[...] + (1.0 - ADAM_B2) * (gv * gv)
        mo_ref[...] = mn
        vo_ref[...] = vn
        m_hat = mn / (1.0 - ADAM_B1 ** ADAM_STEP)
        v_hat = vn / (1.0 - ADAM_B2 ** ADAM_STEP)
        d_ref[...] = -ADAM_LR * (m_hat / (jnp.sqrt(v_hat) + ADAM_EPS) + ADAM_WD * w_ref[...])

    return _rows("adamw", fn, r, tm, [(w, "r"), (g, "r"), (m, "r"), (v, "r")], [((r, c), F32, "r")] * 3)


def _place():
    x, y, c = lax.axis_index("x"), lax.axis_index("y"), lax.axis_index("c")
    other_chips = [(1 - x, y), (x, 1 - y), (1 - x, 1 - y)]
    return x, y, c, 2 * x + y, other_chips


def _gather_small(vec):
    flips = [(fx, fy, fc) for fx in (0, 1) for fy in (0, 1) for fc in (0, 1)][1:]

    def body(src, dst, send, recv, lsem):
        x, y, c, _, _ = _place()
        me = 4 * x + 2 * y + c
        local = pltpu.make_async_copy(src, dst.at[me], lsem)
        local.start()
        away = []
        for j, (fx, fy, fc) in enumerate(flips):
            px, py, pc = x ^ fx, y ^ fy, c ^ fc
            away.append(pltpu.make_async_remote_copy(
                src_ref=src, dst_ref=dst.at[me], send_sem=send.at[j], recv_sem=recv.at[j],
                device_id=(px, py, pc), device_id_type=MESH))
        for cp in away:
            cp.start()
        for j, (fx, fy, fc) in enumerate(flips):
            blk = dst.at[4 * (x ^ fx) + 2 * (y ^ fy) + (c ^ fc)]
            pltpu.make_async_remote_copy(
                src_ref=blk, dst_ref=blk, send_sem=send.at[j], recv_sem=recv.at[j],
                device_id=(x ^ fx, y ^ fy, c ^ fc), device_id_type=MESH).wait_recv()
        for cp in away:
            cp.wait_send()
        local.wait()

    return pl.pallas_call(
        body, name="gather_small", in_specs=[ANY], out_specs=ANY,
        out_shape=jax.ShapeDtypeStruct((8,) + vec.shape, vec.dtype),
        scratch_shapes=[pltpu.SemaphoreType.DMA((7,)), pltpu.SemaphoreType.DMA((7,)), pltpu.SemaphoreType.DMA(())],
        compiler_params=pltpu.CompilerParams(has_side_effects=True))(vec)


HBM = pl.BlockSpec(memory_space=pltpu.HBM)
SEM = pl.BlockSpec(memory_space=pltpu.SEMAPHORE)
EFFECT = pltpu.SideEffectType.DATAFLOW_SIDE_EFFECTING


def _exchange(name, mode, arrays, sem_shape, plan, sems=None, after=None):
    n = len(arrays)
    hbm = [pltpu.HBM(a.shape, a.dtype) for a in arrays]
    arrays = [pltpu.with_memory_space_constraint(a, pltpu.HBM) for a in arrays]
    if mode == "run":
        def body(*refs):
            outs, send, recv = refs[n:2 * n], refs[2 * n], refs[2 * n + 1]
            sends, recvs = plan(outs, send, recv)
            for cp in sends:
                cp.start()
            for cp in recvs:
                cp.wait_recv()
            for cp in sends:
                cp.wait_send()

        return list(pl.pallas_call(
            body, name=name, in_specs=[HBM] * n, out_specs=[HBM] * n, out_shape=hbm,
            input_output_aliases={i: i for i in range(n)},
            scratch_shapes=[pltpu.SemaphoreType.DMA(sem_shape), pltpu.SemaphoreType.DMA(sem_shape)],
            compiler_params=pltpu.CompilerParams(has_side_effects=EFFECT))(*arrays))
    if mode == "start":
        def body(*refs):
            send, recv, token, outs = refs[n], refs[n + 1], refs[n + 2], refs[n + 3:]
            sends, _ = plan(outs, send, recv)
            for cp in sends:
                cp.start()
            token[...] = jnp.zeros(token.shape, F32)

        res = pl.pallas_call(
            body, name=name, in_specs=[HBM] * n,
            out_specs=[SEM, SEM, pl.BlockSpec(memory_space=pltpu.VMEM)] + [HBM] * n,
            out_shape=[pltpu.SemaphoreType.DMA(sem_shape), pltpu.SemaphoreType.DMA(sem_shape),
                       jax.ShapeDtypeStruct((8, HEAD_PAD), F32)] + hbm,
            input_output_aliases={i: i + 3 for i in range(n)},
            compiler_params=pltpu.CompilerParams(has_side_effects=EFFECT))(*arrays)
        return (res[0], res[1]), res[2], list(res[3:])

    def body(*refs):
        ins, send, recv = refs[:n], refs[n], refs[n + 1]
        sends, recvs = plan(ins, send, recv)
        for cp in sends:
            cp.wait_send()
        for cp in recvs:
            cp.wait_recv()

    return list(pl.pallas_call(
        body, name=name, in_specs=[HBM] * n + [SEM, SEM, ANY], out_specs=[HBM] * n, out_shape=hbm,
        input_output_aliases={i: i for i in range(n)},
        compiler_params=pltpu.CompilerParams(has_side_effects=EFFECT))(*arrays, sems[0], sems[1], after))


def _rows_half(ref_rows, half):
    h = ref_rows // 2
    return pl.ds(pl.multiple_of(half * h, 8), h)


def _remote(src, dst, send_sem, recv_sem, device):
    return pltpu.make_async_remote_copy(src_ref=src, dst_ref=dst, send_sem=send_sem, recv_sem=recv_sem,
                                        device_id=device, device_id_type=MESH)


def _plan_gather_ici(k_n, l):
    def plan(refs, send, recv):
        x, y, c, me, chips = _place()
        sends, recvs = [], []
        for k in range(k_n):
            src, buf = refs[k], refs[k_n + k]
            rows = _rows_half(src.shape[1], c)
            for j, (cx, cy) in enumerate(chips):
                sends.append(_remote(src.at[l, rows], buf.at[0, me, rows], send.at[3 * k + j], recv.at[3 * k + j], (cx, cy, c)))
                got = buf.at[0, 2 * cx + cy, rows]
                recvs.append(_remote(got, got, send.at[3 * k + j], recv.at[3 * k + j], (cx, cy, c)))
        return sends, recvs
    return plan


def _plan_gather_d2d(k_n):
    def plan(refs, send, recv):
        x, y, c, me, chips = _place()
        sends, recvs = [], []
        for k in range(k_n):
            buf = refs[k]
            for j, (cx, cy) in enumerate(chips):
                mine = buf.at[0, 2 * cx + cy, _rows_half(buf.shape[2], c)]
                theirs = buf.at[0, 2 * cx + cy, _rows_half(buf.shape[2], 1 - c)]
                sends.append(_remote(mine, mine, send.at[3 * k + j], recv.at[3 * k + j], (x, y, 1 - c)))
                recvs.append(_remote(theirs, theirs, send.at[3 * k + j], recv.at[3 * k + j], (x, y, 1 - c)))
        return sends, recvs
    return plan


def _plan_swap(k_n):
    def plan(refs, send, recv):
        x, y, c, me, chips = _place()
        sends, recvs = [], []
        for k in range(k_n):
            src, land = refs[k], refs[k_n + k]
            sends.append(_remote(src.at[:, _rows_half(src.shape[1], 1 - c)], land, send.at[k], recv.at[k], (x, y, 1 - c)))
            recvs.append(_remote(land, land, send.at[k], recv.at[k], (x, y, 1 - c)))
        return sends, recvs
    return plan


def _plan_scatter(k_n):
    def plan(refs, send, recv):
        x, y, c, me, chips = _place()
        sends, recvs = [], []
        for k in range(k_n):
            src, land = refs[k], refs[k_n + k]
            for j, (cx, cy) in enumerate(chips):
                sends.append(_remote(src.at[2 * cx + cy], land.at[j], send.at[3 * k + j], recv.at[3 * k + j], (cx, cy, c)))
                recvs.append(_remote(land.at[j], land.at[j], send.at[3 * k + j], recv.at[3 * k + j], (cx, cy, c)))
        return sends, recvs
    return plan


def _plan_share(k_n, layers):
    def plan(refs, send, recv):
        x, y, c, me, chips = _place()
        sends, recvs = [], []
        for k in range(k_n):
            out = refs[k]
            for l in range(layers):
                mine = out.at[l, _rows_half(out.shape[1], c)]
                theirs = out.at[l, _rows_half(out.shape[1], 1 - c)]
                sends.append(_remote(mine, mine, send.at[layers * k + l], recv.at[layers * k + l], (x, y, 1 - c)))
                recvs.append(_remote(theirs, theirs, send.at[layers * k + l], recv.at[layers * k + l], (x, y, 1 - c)))
        return sends, recvs
    return plan


def _add_sibling_half(idx, mine, theirs):
    q, r, c = mine.shape
    h = r // 2
    tm = _tile(h, 256)

    def fn(idx_ref, a_ref, b_ref, o_ref):
        o_ref[...] = (a_ref[...].astype(F32) + b_ref[...].astype(F32)).astype(o_ref.dtype)

    return _prefetch_call(
        "add_sibling", fn, idx, (q, h // tm),
        [pl.BlockSpec((None, None, tm, c), lambda b, i, ix: (b, ix[0], i, 0)),
         pl.BlockSpec((None, tm, c), lambda b, i, ix: (b, i, 0))],
        pl.BlockSpec((None, tm, c), lambda b, i, ix: (b, i, 0)), jax.ShapeDtypeStruct((q, h, c), WIRE_DTYPE),
        [mine.reshape(q, 2, h, c), theirs])


def _add_chips_half(idx, own, others, layers, l, into):
    _, h, c = own.shape
    tm = _tile(h, 256)

    def fn(*refs):
        a_ref, b_ref, o_ref = refs[-3:]
        tot = a_ref[...].astype(F32)
        for k in range(3):
            tot = tot + b_ref[k].astype(F32)
        o_ref[...] = tot

    in_specs = [pl.BlockSpec((None, tm, c), lambda i, ix: (ix[1], i, 0)), pl.BlockSpec((3, tm, c), lambda i, ix: (0, i, 0))]
    args = [own, others]
    aliases = None
    if into is not None:
        in_specs = [ANY] + in_specs
        args = [into.reshape(layers, 2, h, c)] + args
        aliases = {1: 0}
    out = _prefetch_call(
        "add_chips", fn, idx, (h // tm,), in_specs,
        pl.BlockSpec((None, None, tm, c), lambda i, ix: (l, ix[0], i, 0)),
        jax.ShapeDtypeStruct((layers, 2, h, c), F32), args, aliases)
    return out.reshape(layers, 2 * h, c)


SHARDED = ["ffn1_w_gate", "ffn1_w_up", "ffn1_w_down", "w_in", "w_uq", "w_ukv", "conv_w", "w_out",
           "ffn2_w_gate", "ffn2_w_up", "ffn2_w_down"]
SMALL = ["ffn1_norm", "mix_norm", "q_latent_norm", "kv_latent_norm", "q_norm", "k_norm", "conv_b", "conv_ln_g",
         "conv_ln_b", "ffn2_norm", "post_norm"]
ALL_WEIGHTS = ["ffn1_norm", "ffn1_w_gate", "ffn1_w_up", "ffn1_w_down", "mix_norm", "w_in", "q_latent_norm", "w_uq",
               "kv_latent_norm", "w_ukv", "q_norm", "k_norm", "conv_w", "conv_b", "conv_ln_g", "conv_ln_b", "w_out",
               "ffn2_norm", "ffn2_w_gate", "ffn2_w_up", "ffn2_w_down", "post_norm"]


def _rope_tables(s):
    pos = jnp.arange(s, dtype=F32)
    inv_freq = 1.0 / (ROPE_THETA ** (jnp.arange(0, QK_ROPE, 2, dtype=F32) / QK_ROPE))
    ang = pos[:, None] * inv_freq[None, :]
    cos, sin = jnp.cos(ang), jnp.sin(ang)
    half = QK_ROPE // 2
    one = jnp.ones((s, QK_NOPE), F32)
    zero = jnp.zeros((s, QK_NOPE), F32)
    pad = HEAD_PAD - QK_DIM
    z16 = jnp.zeros((s, half), F32)
    cosf = jnp.concatenate([one, cos, cos, jnp.ones((s, pad), F32)], axis=1)
    s1 = jnp.concatenate([zero, -sin, z16, jnp.zeros((s, pad), F32)], axis=1)
    s2 = jnp.concatenate([zero, z16, sin, jnp.zeros((s, pad), F32)], axis=1)
    return cosf, s1, s2


def _ffn_fwd(n, w, names, l):
    a, b, hm = _ffn_up_fused(n, w[names[0]], w[names[1]], l)
    return a, b, hm, _blocked_sum("ffn_down", [hm], [w[names[2]]], l, "nn")


def _ffn_bwd(nt, a, b, hm, do, w, names, grads):
    gate, up, down = names
    da, db = _ffn_down_t_fused(do, w[down], a, b, 0)
    grads[down] = _wgrad("ffn_down_wgrad", hm, do, WIRE_DTYPE)
    g_gate, g_up = _wgrad_pair("ffn_up_wgrad", nt, da, db, 1, 0, None)
    grads[gate], grads[up] = g_gate[0], g_up[0]
    return [_blocked_sum("ffn_up_t", [da, db], [w[gate], w[up]], 0, "nt")]


def _vec(v, l):
    return v[l][None, :]


def _pad_lanes(v, width):
    return jnp.pad(v, [(0, 0)] * (v.ndim - 1) + [(0, width - v.shape[-1])])


def kernel(x, ffn1_norm, ffn1_w_gate, ffn1_w_up, ffn1_w_down, mix_norm, w_in, q_latent_norm, w_uq, kv_latent_norm, w_ukv, q_norm, k_norm, conv_w, conv_b, conv_ln_g, conv_ln_b, w_out, ffn2_norm, ffn2_w_gate, ffn2_w_up, ffn2_w_down, post_norm, loss_target, m_ffn1_norm, m_ffn1_w_gate, m_ffn1_w_up, m_ffn1_w_down, m_mix_norm, m_w_in, m_q_latent_norm, m_w_uq, m_kv_latent_norm, m_w_ukv, m_q_norm, m_k_norm, m_conv_w, m_conv_b, m_conv_ln_g, m_conv_ln_b, m_w_out, m_ffn2_norm, m_ffn2_w_gate, m_ffn2_w_up, m_ffn2_w_down, m_post_norm, v_ffn1_norm, v_ffn1_w_gate, v_ffn1_w_up, v_ffn1_w_down, v_mix_norm, v_w_in, v_q_latent_norm, v_w_uq, v_kv_latent_norm, v_w_ukv, v_q_norm, v_k_norm, v_conv_w, v_conv_b, v_conv_ln_g, v_conv_ln_b, v_w_out, v_ffn2_norm, v_ffn2_w_gate, v_ffn2_w_up, v_ffn2_w_down, v_post_norm):
    given = dict(locals())
    wts = {n: given[n] for n in ALL_WEIGHTS}
    mom = {n: given["m_" + n] for n in ALL_WEIGHTS}
    var = {n: given["v_" + n] for n in ALL_WEIGHTS}
    n_layers = ffn1_norm.shape[0]
    s, d = x.shape[1], x.shape[2]
    ql, kvl, cw = q_latent_norm.shape[-1], kv_latent_norm.shape[-1], conv_b.shape[-1]
    xs = x.reshape(s, d)
    target = loss_target.reshape(s, d)

    assert n_layers == 2, "the layers' exchanges run one behind the other"
    local = [jnp.pad(wts[n], ((0, 0), (0, 1), (0, 0))).astype(MM_DTYPE) if n == "conv_w" else wts[n].astype(MM_DTYPE)
             for n in SHARDED]
    L = n_layers
    k_n = len(SHARDED)
    my_c = lax.axis_index("c")
    my_chip = 2 * lax.axis_index("x") + lax.axis_index("y")
    place = jnp.stack([my_c, my_chip]).astype(jnp.int32)
    hw = N_HEADS * HEAD_PAD
    tabs = _rope_tables(s)
    pe_block = (2 * cw + ql + kvl) // HEAD_PAD
    gq_pad, gk_pad = _pad_lanes(q_norm, HEAD_PAD), _pad_lanes(k_norm, HEAD_PAD)
    tq = _tile(s, 256)

    def own_blocks(l):
        return [lax.dynamic_update_slice(lax.empty((1, N_CHIPS) + t.shape[1:], t.dtype), t[l][None, None],
                                         (0, my_chip, 0, 0)) for t in local]

    def layouts(w):
        win = w["w_in"].transpose(0, 2, 1, 3).reshape(1, d, -1)
        zc = lambda n: jnp.zeros((1, d, n), MM_DTYPE)
        o_pe, o_cv = ql + kvl, ql + kvl + QK_ROPE
        w["w_in"] = jnp.concatenate([win[..., o_cv:], win[..., :o_pe], zc(QK_NOPE), win[..., o_pe:o_cv],
                                     zc(HEAD_PAD - QK_DIM)], axis=-1)[:, None]
        wq = w["w_uq"].transpose(0, 2, 1, 3).reshape(1, ql, N_HEADS, QK_DIM)
        w["wq"] = _pad_lanes(wq, HEAD_PAD).reshape(1, 1, ql, hw)
        wkv = w["w_ukv"].transpose(0, 2, 1, 3).reshape(1, kvl, N_HEADS, QK_NOPE + V_DIM)
        w["wk"] = _pad_lanes(wkv[..., :QK_NOPE], HEAD_PAD).reshape(1, 1, kvl, hw)
        w["wv"] = _pad_lanes(wkv[..., QK_NOPE:], HEAD_PAD).reshape(1, 1, kvl, hw)
        w["wvt"] = w["wv"][:, 0].transpose(0, 2, 1)
        w["conv"] = w["conv_w"].transpose(0, 2, 1, 3).reshape(CONV_K + 1, cw).astype(F32)
        wout = w["w_out"].reshape(1, d, d)
        wo_a = wout[:, :N_HEADS * V_DIM].reshape(1, N_HEADS, V_DIM, d)
        w["wo_a"] = jnp.pad(wo_a, ((0, 0), (0, 0), (0, HEAD_PAD - V_DIM), (0, 0))).reshape(1, 1, hw, d)
        w["wo_c"] = wout[:, N_HEADS * V_DIM:][:, None]
        return w

    ici, d2d, sem_kj = (lambda l: _plan_gather_ici(k_n, l)), _plan_gather_d2d(k_n), (3 * k_n,)
    n_a = 3
    n_b = k_n - n_a
    own0 = own_blocks(0)
    landed = _exchange("gather0a_ici", "run", local[:n_a] + own0[:n_a], (3 * n_a,), _plan_gather_ici(n_a, 0))
    w0 = dict(zip(SHARDED[:n_a], _exchange("gather0a_d2d", "run", landed[n_a:], (3 * n_a,), _plan_gather_d2d(n_a))))
    local = landed[:n_a] + local[n_a:]
    sems_r, token_r, rest = _exchange("gather0b_ici_start", "start", local[n_a:] + own0[n_a:], (3 * n_b,),
                                      _plan_gather_ici(n_b, 0))

    def fwd_ffn(l, cur, w, tok):
        sv = {"x": cur}
        n1, sv["n1t"] = _norm_fwd(cur, _vec(ffn1_norm, l) + tok)
        sv["a1"], sv["b1"], sv["hm1"], o1 = _ffn_fwd(n1, w, SHARDED[0:3], 0)
        sv["x1"], sv["h"] = _resid_norm(cur, [o1], 0.5, _vec(mix_norm, l), MM_DTYPE)
        return sv

    def fwd_first(l, sv, w, tok):
        sv["p"] = _proj("w_in", sv["h"], w["w_in"], 0, "nn")[0]
        sv["cq"], sv["ckv"], ckvt = _mla_pre(sv["p"], _vec(q_latent_norm, l) + tok, _vec(kv_latent_norm, l), cw)
        sv["q_pre"] = _proj("wq", sv["cq"], w["wq"], 0, "nn")[0]
        sv["k_pre"] = _proj("wk", sv["ckv"], w["wk"], 0, "nn")[0]
        sv["v"] = _proj("wv", sv["ckv"], w["wv"], 0, "nn", MM_DTYPE)[0]
        sv["vt"] = _mm("wv_transposed", w["wvt"], ckvt, "nn", (s // tq,),
                       pl.BlockSpec((None, hw, kvl), lambda i: (0, 0, 0)), pl.BlockSpec((kvl, tq), lambda i: (0, i)),
                       pl.BlockSpec((None, hw, tq), lambda i: (i, 0, 0)),
                       jax.ShapeDtypeStruct((s // tq, hw, tq), MM_DTYPE), 1, None)
        sv["q"], sv["k"], sv["kt"] = _qk_post(sv["q_pre"], sv["k_pre"], sv["p"], pe_block, tabs,
                                              _vec(gq_pad, l), _vec(gk_pad, l))
        return sv

    def fwd_attn(l, sv, w):
        sv["c"] = _conv_fwd(sv["p"], w["conv"], _vec(conv_b, l), _vec(conv_ln_g, l), _vec(conv_ln_b, l))
        sv["o"], sv["lse"] = _attn_fwd(sv["q"], sv["k"], sv.pop("vt"))
        return sv

    def fwd_second(l, sv, w, tok):
        mix_a = _contract("wo_a", sv["o"][None], w["wo_a"], 0, "nn")
        mix_c = _contract("wo_c", sv["c"][None], w["wo_c"], 0, "nn")
        sv["x2"], n2, sv["n2t"] = _resid_norm(sv["x1"], [mix_a, mix_c], 1.0, _vec(ffn2_norm, l) + tok, MM_DTYPE, True)
        sv["a2"], sv["b2"], sv["hm2"], o2 = _ffn_fwd(n2, w, SHARDED[8:11], 0)
        sv["x3"], out = _final_fwd(sv["x2"], o2, _vec(post_norm, l))
        return out

    sv0 = fwd_ffn(0, xs, w0, token_r[0, 0])
    rest = _exchange("gather0b_ici_wait", "wait", rest, (3 * n_b,), _plan_gather_ici(n_b, 0), sems=sems_r, after=sv0["h"])
    w0.update(zip(SHARDED[n_a:], _exchange("gather0b_d2d", "run", rest[n_b:], (3 * n_b,), _plan_gather_d2d(n_b))))
    w0 = layouts(w0)
    local = local[:n_a] + rest[:n_b]
    sems_a, token_a, in_flight = _exchange("gather1_ici_start", "start", local + own_blocks(1), sem_kj, ici(1))
    sv0 = fwd_attn(0, fwd_first(0, sv0, w0, token_a[0, 0]), w0)
    in_flight = _exchange("gather1_ici_wait", "wait", in_flight, sem_kj, ici(1), sems=sems_a, after=sv0["o"])
    sems_b, token_b, in_flight = _exchange("gather1_d2d_start", "start", in_flight[k_n:], sem_kj, d2d)
    cur = fwd_second(0, sv0, w0, token_b[0, 0])
    w1 = layouts(dict(zip(SHARDED, _exchange("gather1_d2d_wait", "wait", in_flight, sem_kj, d2d, sems=sems_b, after=cur))))
    sv1 = fwd_attn(1, fwd_first(1, fwd_ffn(1, cur, w1, 0.0), w1, 0.0), w1)
    cur = fwd_second(1, sv1, w1, 0.0)
    dy, loss_part = _loss(cur, target)

    small = {n: [None] * L for n in SMALL}

    def bwd_first(l, sv, dy, w, tok, grads):
        dx3, do2, small["post_norm"][l] = _norm_bwd(sv["x3"], _vec(post_norm, l) + tok, [dy], None, 0.5)
        dn2 = _ffn_bwd(sv["n2t"], sv["a2"], sv["b2"], sv["hm2"], do2, w, SHARDED[8:11], grads)
        dx2, dx2b, small["ffn2_norm"][l] = _norm_bwd(sv["x2"], _vec(ffn2_norm, l), dn2, dx3, 1.0)
        do = _proj("wo_a_t", dx2b, w["wo_a"], 0, "nt")[0]
        dc = _proj("wo_c_t", dx2b, w["wo_c"], 0, "nt")[0]
        g_wo_a = _wgrad("wo_a_wgrad", sv["o"], dx2b[None], WIRE_DTYPE)[0].reshape(N_HEADS, HEAD_PAD, d)[:, :V_DIM]
        g_wo_c = _wgrad("wo_c_wgrad", sv["c"], dx2b[None], WIRE_DTYPE)
        grads["w_out"] = jnp.concatenate(
            [g_wo_a.reshape(N_HEADS * V_DIM, d), g_wo_c[0]], axis=0).reshape(N_CHIPS, d // N_CHIPS, d)
        dq, dk, dv = _attn_bwd(sv["q"], sv["k"], sv["kt"], sv["v"], do.astype(MM_DTYPE), sv["lse"],
                               _attn_delta(sv["o"], do))
        dq_pre, dk_pre, dpe, small["q_norm"][l], small["k_norm"][l] = _qk_post_bwd(
            sv["q_pre"], sv["k_pre"], sv["p"], pe_block, tabs, _vec(gq_pad, l), _vec(gk_pad, l), dq, dk)
        return dict(dx2=dx2, dc=dc, dq_pre=dq_pre, dk_pre=dk_pre, dv=dv, dpe=dpe)

    def bwd_second(l, sv, st, w, tok, grads):
        dq_pre, dk_pre, dv = st["dq_pre"], st["dk_pre"], st["dv"]
        dcq = _contract("wq_t", dq_pre[None], w["wq"], 0, "nt")
        dckv = [_contract("wk_t", dk_pre[None], w["wk"], 0, "nt"), _contract("wv_t", dv[None], w["wv"], 0, "nt")]
        per_head = lambda t, rows: t[0].reshape(rows, N_HEADS, HEAD_PAD)
        g_wq = per_head(_wgrad("wq_wgrad", sv["cq"], dq_pre[None], WIRE_DTYPE), ql)[..., :QK_DIM]
        grads["w_uq"] = g_wq.reshape(ql, N_CHIPS, -1).transpose(1, 0, 2)
        g_wkv = jnp.concatenate([per_head(_wgrad("wk_wgrad", sv["ckv"], dk_pre[None], WIRE_DTYPE), kvl)[..., :QK_NOPE],
                                 per_head(_wgrad("wv_wgrad", sv["ckv"], dv[None], WIRE_DTYPE), kvl)[..., :V_DIM]], axis=-1)
        grads["w_ukv"] = g_wkv.reshape(kvl, N_CHIPS, -1).transpose(1, 0, 2)
        dyc, small["conv_ln_g"][l], small["conv_ln_b"][l], small["conv_b"][l] = _conv_bwd_ln(
            sv["p"], w["conv"], _vec(conv_b, l) + tok, _vec(conv_ln_g, l), _vec(conv_ln_b, l), st["dc"])
        dp_conv, g_conv = _conv_bwd_taps(sv["p"], w["conv"], dyc)
        grads["conv_w"] = g_conv.reshape(CONV_K + 1, N_CHIPS, cw // N_CHIPS).transpose(1, 0, 2).astype(WIRE_DTYPE)
        dp, small["q_latent_norm"][l], small["kv_latent_norm"][l] = _dp_assemble(
            sv["p"], _vec(q_latent_norm, l), _vec(kv_latent_norm, l), dcq, dckv, dp_conv, st["dpe"], cw)
        dh = _contract("w_in_t", dp[None], w["w_in"], 0, "nt")
        g_win = _wgrad("w_in_wgrad", sv["h"], dp[None], WIRE_DTYPE)[0]
        o_q, o_pe_col = 2 * cw, 2 * cw + ql + kvl + QK_NOPE
        g_win = jnp.concatenate([g_win[:, o_q:o_q + ql + kvl], g_win[:, o_pe_col:o_pe_col + QK_ROPE], g_win[:, :o_q]], axis=1)
        grads["w_in"] = g_win.reshape(d, N_CHIPS, -1).transpose(1, 0, 2)
        dx1, do1, small["mix_norm"][l] = _norm_bwd(sv["x1"], _vec(mix_norm, l), [dh], st["dx2"], 0.5)
        dn1 = _ffn_bwd(sv["n1t"], sv["a1"], sv["b1"], sv["hm1"], do1, w, SHARDED[0:3], grads)
        dx, _, small["ffn1_norm"][l] = _norm_bwd(sv["x"], _vec(ffn1_norm, l), dn1, dx1, 1.0)
        return dx

    def halves(parts):
        return [lax.empty((N_CHIPS, t.shape[1] // 2, t.shape[2]), t.dtype) for t in parts]

    def thirds(parts):
        return [lax.empty((N_CHIPS - 1,) + t.shape[1:], t.dtype) for t in parts]

    def chip_sums(done):
        n = len(done) // 2
        return [_add_sibling_half(place, g, t) for g, t in zip(done[:n], done[n:])]

    def finish(done, l, into):
        n = len(done) // 2
        return [_add_chips_half(place, own, others, L, l, t) for own, others, t in zip(done[:n], done[n:], into)]

    swap, scatter, sem_k = _plan_swap(k_n), _plan_scatter(k_n), (k_n,)
    grads1, grads0 = {}, {}
    st1 = bwd_first(1, sv1, dy, w1, 0.0, grads1)
    dy = bwd_second(1, sv1, st1, w1, 0.0, grads1)
    parts1 = [grads1[n] for n in SHARDED]
    sems_s, token_s, in_flight = _exchange("swap1_start", "start", parts1 + halves(parts1), sem_k, swap)
    st0 = bwd_first(0, sv0, dy, w0, token_s[0, 0], grads0)
    sums1 = chip_sums(_exchange("swap1_wait", "wait", in_flight, sem_k, swap, sems=sems_s, after=st0["dq_pre"]))
    sems_c, token_c, in_flight = _exchange("scatter1_start", "start", sums1 + thirds(sums1), sem_kj, scatter)
    early = [n for n in SHARDED if n in grads0]
    late = [n for n in SHARDED if n not in grads0]
    n_e, n_l = len(early), len(late)
    parts_e = [grads0[n] for n in early]
    sums_e = chip_sums(_exchange("swap0_early", "run", parts_e + halves(parts_e), (n_e,), _plan_swap(n_e)))
    sems_e, token_e, early_flight = _exchange("scatter0_early_start", "start", sums_e + thirds(sums_e), (3 * n_e,),
                                              _plan_scatter(n_e))
    dy = bwd_second(0, sv0, st0, w0, token_c[0, 0] + token_e[0, 0], grads0)
    grad_x = dy.reshape(x.shape)
    out = dict(zip(SHARDED, finish(_exchange("scatter1_wait", "wait", in_flight, sem_kj, scatter, sems=sems_c, after=dy),
                                   1, [None] * k_n)))
    done = _exchange("scatter0_early_wait", "wait", early_flight, (3 * n_e,), _plan_scatter(n_e), sems=sems_e, after=dy)
    out.update(zip(early, finish(done, 0, [out[n] for n in early])))
    parts_l = [grads0[n] for n in late]
    sums_l = chip_sums(_exchange("swap0_late", "run", parts_l + halves(parts_l), (n_l,), _plan_swap(n_l)))
    sems_l, token_l, late_flight = _exchange("scatter0_late_start", "start", sums_l + thirds(sums_l), (3 * n_l,),
                                             _plan_scatter(n_l))

    delta_w, new_m, new_v = {}, {}, {}
    width = max(wts[n].shape[-1] for n in SMALL)
    pack = lambda vals: jnp.concatenate([_pad_lanes(v.reshape(-1, v.shape[-1]), width) for v in vals], axis=0)
    part = pack([jnp.concatenate(small[n], axis=0)[..., :wts[n].shape[-1]] for n in SMALL])
    rows = part.shape[0]
    part = jnp.pad(part, [(0, -rows % 8), (0, 0)])
    small_sum = _add_leading("add_devices", _gather_small(part + token_l[0, 0]), F32)
    pk = lambda src: jnp.pad(pack([src[n] for n in SMALL]), [(0, -rows % 8), (0, 0)])
    small_outs = _adamw(pk(wts), small_sum, pk(mom), pk(var))

    done = _exchange("scatter0_late_wait", "wait", late_flight, (3 * n_l,), _plan_scatter(n_l), sems=sems_l,
                     after=small_outs[0])
    out.update(zip(late, finish(done, 0, [out[n] for n in late])))
    full = dict(zip(SHARDED, _exchange("share", "run", [out[n] for n in SHARDED], (L * k_n,), _plan_share(k_n, L))))
    full["conv_w"] = full["conv_w"][:, :CONV_K]
    for i, n in enumerate(SMALL):
        full[n] = small_sum[i * L:(i + 1) * L, :wts[n].shape[-1]]
        delta_w[n], new_m[n], new_v[n] = [o[i * L:(i + 1) * L, :wts[n].shape[-1]] for o in small_outs]

    for n in SHARDED:
        shape = wts[n].shape
        swap = shape[-1] % HEAD_PAD != 0 and shape[-2] % HEAD_PAD == 0
        view = (lambda t: jnp.swapaxes(t, 1, 2)) if swap else (lambda t: t)
        flat = lambda t: view(t).reshape(-1, view(t).shape[-1])
        r = flat(wts[n]).shape[0]
        fix = (lambda t: jnp.pad(flat(t), [(0, -r % 8), (0, 0)])) if r % 8 else flat
        outs = _adamw(fix(wts[n]), fix(full[n]), fix(mom[n]), fix(var[n]))
        delta_w[n], new_m[n], new_v[n] = [view(o[:r].reshape(view(wts[n]).shape)) for o in outs]
    loss = lax.psum(loss_part[0, 0], ("x", "y", "c"))
    return (loss, grad_x, *[full[n] for n in ALL_WEIGHTS], *[delta_w[n] for n in ALL_WEIGHTS],
            *[new_m[n] for n in ALL_WEIGHTS], *[new_v[n] for n in ALL_WEIGHTS])
```
